```python
import math
import jax, jax.numpy as jnp
from jax import lax
import numpy as np

D_MODEL = 1024
BATCH = 8
SEQ = 4096
DEPTH = 2

GRID_W = 64
Q_BLOCK = 128
HEAD_DIM = 64
N_HEADS_A = 8
N_KV_A = 2
N_HEADS_B = 4
MLA_Q_RANK = 192
MLA_KV_RANK = 128
MLA_NOPE = 64
MLA_ROPE = 32
MLA_V = 64
N_HEADS_C = 4
DIFF_QK = 32
DIFF_V = 64
D_FF = 2816
CONV_W = 3
ROPE_BASE = 10000.0
EPS = 1e-6
A_WIDTH = N_HEADS_A * HEAD_DIM + 2 * N_KV_A * HEAD_DIM
B_WIDTH = MLA_Q_RANK + MLA_KV_RANK + MLA_ROPE
C_WIDTH = 2 * N_HEADS_C * 2 * DIFF_QK + N_HEADS_C * DIFF_V
IN_WIDTH = A_WIDTH + B_WIDTH + C_WIDTH
MIX_WIDTH = N_HEADS_A * HEAD_DIM + N_HEADS_B * MLA_V + N_HEADS_C * DIFF_V

kernel_name = 'hybrid_parallel_heads_encoder'


def rmsnorm(x, g):
    xf = x.astype(jnp.float32)
    y = xf * lax.rsqrt(jnp.mean(xf * xf, axis=-1, keepdims=True) + EPS)
    return (y * g.astype(jnp.float32)).astype(x.dtype)


def softmax_f32(logits, scale, dtype):
    return jax.nn.softmax(logits.astype(jnp.float32) * scale, axis=-1).astype(dtype)


def axial_rope_tables(rows, cols, dim):
    nf = dim // 4
    inv = ROPE_BASE ** (-jnp.arange(nf, dtype=jnp.float32) / nf)
    ar = rows.astype(jnp.float32)[:, None] * inv
    ac = cols.astype(jnp.float32)[:, None] * inv
    return (jnp.cos(ar), jnp.sin(ar), jnp.cos(ac), jnp.sin(ac))


def _rotate(v, c, s):
    h = v.shape[-1] // 2
    v1, v2 = v[..., :h], v[..., h:]
    return jnp.concatenate([v1 * c - v2 * s, v2 * c + v1 * s], axis=-1)


def apply_axial_rope(x, tabs):
    S = x.shape[1]
    shp = (S,) + (1,) * (x.ndim - 3) + (-1,)
    cr, sr, cc, sc = [t.reshape(shp) for t in tabs]
    xf = x.astype(jnp.float32)
    half = x.shape[-1] // 2
    out = jnp.concatenate([_rotate(xf[..., :half], cr, sr), _rotate(xf[..., half:], cc, sc)], axis=-1)
    return out.astype(x.dtype)


def to_blocks(x):
    B, S = x.shape[:2]
    return jnp.moveaxis(x.reshape((B, S // Q_BLOCK, Q_BLOCK) + x.shape[2:]), 1, 0)


def from_blocks(o):
    nb, B, blk = o.shape[:3]
    return jnp.moveaxis(o, 0, 1).reshape((B, nb * blk) + o.shape[3:])


def gqa_axial_mixer(qa, ka, va, q_gain, k_gain, tabs):
    B, S, _ = qa.shape
    G = N_HEADS_A // N_KV_A
    q = apply_axial_rope(rmsnorm(qa.reshape(B, S, N_HEADS_A, HEAD_DIM), q_gain), tabs)
    k = apply_axial_rope(rmsnorm(ka.reshape(B, S, N_KV_A, HEAD_DIM), k_gain), tabs)
    v = va.reshape(B, S, N_KV_A, HEAD_DIM)
    scale = HEAD_DIM ** -0.5

    def block(qb):
        qg = qb.reshape(qb.shape[0], Q_BLOCK, N_KV_A, G, HEAD_DIM)
        logits = jnp.einsum('bqkgd,bskd->bkgqs', qg, k)
        p = softmax_f32(logits, scale, v.dtype)
        o = jnp.einsum('bkgqs,bskd->bqkgd', p, v)
        return o.reshape(qb.shape[0], Q_BLOCK, N_HEADS_A * HEAD_DIM)

    return from_blocks(lax.map(block, to_blocks(q)))


def mla_mixer(cq, ckv, kr, q_gain, w_uq, kv_gain, w_ukv, tabs):
    B, S, _ = cq.shape
    q = (rmsnorm(cq, q_gain) @ w_uq).reshape(B, S, N_HEADS_B, MLA_NOPE + MLA_ROPE)
    qn = q[..., :MLA_NOPE]
    qr = apply_axial_rope(q[..., MLA_NOPE:], tabs)
    kv = (rmsnorm(ckv, kv_gain) @ w_ukv).reshape(B, S, N_HEADS_B, MLA_NOPE + MLA_V)
    kn, v = kv[..., :MLA_NOPE], kv[..., MLA_NOPE:]
    kr = apply_axial_rope(kr, tabs)
    scale = (MLA_NOPE + MLA_ROPE) ** -0.5

    def block(args):
        qnb, qrb = args
        logits = jnp.einsum('bqhd,bshd->bhqs', qnb, kn) + jnp.einsum('bqhr,bsr->bhqs', qrb, kr)
        p = softmax_f32(logits, scale, v.dtype)
        return jnp.einsum('bhqs,bshe->bqhe', p, v)

    o = from_blocks(lax.map(block, (to_blocks(qn), to_blocks(qr))))
    return o.reshape(B, S, N_HEADS_B * MLA_V)


def diff_alibi_mixer(qc, kc, vc, lq1, lk1, lq2, lk2, subln, lam_init, pos):
    B, S, _ = qc.shape
    q = qc.reshape(B, S, N_HEADS_C, 2, DIFF_QK)
    k = kc.reshape(B, S, N_HEADS_C, 2, DIFF_QK)
    v = vc.reshape(B, S, N_HEADS_C, DIFF_V)
    f32 = jnp.float32
    lam = (jnp.exp(jnp.sum(lq1.astype(f32) * lk1.astype(f32)))
           - jnp.exp(jnp.sum(lq2.astype(f32) * lk2.astype(f32))) + lam_init)
    slopes = 2.0 ** (-8.0 * jnp.arange(1, N_HEADS_C + 1, dtype=f32) / N_HEADS_C)
    scale = DIFF_QK ** -0.5

    def block(args):
        qb, qp = args
        logits = jnp.einsum('bqhid,bshid->bhiqs', qb, k).astype(f32) * scale
        bias = -slopes[:, None, None, None] * jnp.abs(qp[:, None] - pos[None, :])[None, None]
        p = jax.nn.softmax(logits + bias, axis=-1)
        a = (p[:, :, 0] - lam * p[:, :, 1]).astype(v.dtype)
        return jnp.einsum('bhqs,bshe->bqhe', a, v)

    o = from_blocks(lax.map(block, (to_blocks(q), pos.reshape(S // Q_BLOCK, Q_BLOCK))))
    o = rmsnorm(o, subln) * (1.0 - lam_init)
    return o.reshape(B, S, N_HEADS_C * DIFF_V)


def conv_gated_mlp(h, w_up, conv_w, conv_b, w_down):
    u = h @ w_up
    up = jnp.pad(u, ((0, 0), (1, 1), (0, 0)))
    u = up[:, :-2] * conv_w[0] + up[:, 1:-1] * conv_w[1] + up[:, 2:] * conv_w[2] + conv_b
    g, val = u[..., :D_FF], u[..., D_FF:]
    return (jax.nn.silu(g) * val) @ w_down


def setup_inputs(seed: int = 0) -> dict:
    key = jax.random.key(seed)
    ks = jax.random.split(key, 21)

    def nrm(k, shape, scale):
        return scale * jax.random.normal(k, shape, jnp.float32)

    def gain(k, shape):
        return 1.0 + 0.02 * jax.random.normal(k, shape, jnp.float32)

    L = DEPTH
    return dict(
        x=jax.random.normal(ks[0], (BATCH, SEQ, D_MODEL), jnp.float32),
        norm_attn=gain(ks[1], (L, D_MODEL)),
        w_in=nrm(ks[2], (L, D_MODEL, IN_WIDTH), D_MODEL ** -0.5),
        q_norm_a=gain(ks[3], (L, HEAD_DIM)),
        k_norm_a=gain(ks[4], (L, HEAD_DIM)),
        q_a_norm_b=gain(ks[5], (L, MLA_Q_RANK)),
        w_uq_b=nrm(ks[6], (L, MLA_Q_RANK, N_HEADS_B * (MLA_NOPE + MLA_ROPE)), MLA_Q_RANK ** -0.5),
        kv_a_norm_b=gain(ks[7], (L, MLA_KV_RANK)),
        w_ukv_b=nrm(ks[8], (L, MLA_KV_RANK, N_HEADS_B * (MLA_NOPE + MLA_V)), MLA_KV_RANK ** -0.5),
        lambda_q1_c=nrm(ks[9], (L, DIFF_QK), 0.1),
        lambda_k1_c=nrm(ks[10], (L, DIFF_QK), 0.1),
        lambda_q2_c=nrm(ks[11], (L, DIFF_QK), 0.1),
        lambda_k2_c=nrm(ks[12], (L, DIFF_QK), 0.1),
        subln_c=gain(ks[13], (L, DIFF_V)),
        w_out=nrm(ks[14], (L, MIX_WIDTH, D_MODEL), MIX_WIDTH ** -0.5),
        norm_ffn=gain(ks[15], (L, D_MODEL)),
        w_up=nrm(ks[16], (L, D_MODEL, 2 * D_FF), D_MODEL ** -0.5),
        conv_w=nrm(ks[17], (L, CONV_W, 2 * D_FF), CONV_W ** -0.5),
        conv_b=nrm(ks[18], (L, 2 * D_FF), 0.01),
        w_down=nrm(ks[19], (L, D_FF, D_MODEL), D_FF ** -0.5),
        final_norm=gain(ks[20], (D_MODEL,)),
    )


def reference(x, norm_attn, w_in, q_norm_a, k_norm_a, q_a_norm_b, w_uq_b, kv_a_norm_b, w_ukv_b,
              lambda_q1_c, lambda_k1_c, lambda_q2_c, lambda_k2_c, subln_c, w_out,
              norm_ffn, w_up, conv_w, conv_b, w_down, final_norm):
    B, S, _ = x.shape
    ROWS = S // GRID_W
    rows = jnp.repeat(jnp.arange(ROWS, dtype=jnp.int32), GRID_W)
    cols = jnp.tile(jnp.arange(GRID_W, dtype=jnp.int32), ROWS)
    tabs_a = axial_rope_tables(rows, cols, HEAD_DIM)
    tabs_b = axial_rope_tables(rows, cols, MLA_ROPE)
    pos = jnp.arange(S, dtype=jnp.float32)
    sizes = [N_HEADS_A * HEAD_DIM, N_KV_A * HEAD_DIM, N_KV_A * HEAD_DIM,
             MLA_Q_RANK, MLA_KV_RANK, MLA_ROPE,
             N_HEADS_C * 2 * DIFF_QK, N_HEADS_C * 2 * DIFF_QK, N_HEADS_C * DIFF_V]
    splits = [int(c) for c in np.cumsum(sizes)[:-1]]

    for l in range(DEPTH):
        lam_init = 0.8 - 0.6 * math.exp(-0.3 * l)
        h = rmsnorm(x, norm_attn[l])
        z = h @ w_in[l]
        qa, ka, va, cq, ckv, kr, qc, kc, vc = jnp.split(z, splits, axis=-1)
        o_a = gqa_axial_mixer(qa, ka, va, q_norm_a[l], k_norm_a[l], tabs_a)
        o_b = mla_mixer(cq, ckv, kr, q_a_norm_b[l], w_uq_b[l], kv_a_norm_b[l], w_ukv_b[l], tabs_b)
        o_c = diff_alibi_mixer(qc, kc, vc, lambda_q1_c[l], lambda_k1_c[l], lambda_q2_c[l],
                               lambda_k2_c[l], subln_c[l], lam_init, pos)
        x = x + jnp.concatenate([o_a, o_b, o_c], axis=-1) @ w_out[l]
        h = rmsnorm(x, norm_ffn[l])
        x = x + conv_gated_mlp(h, w_up[l], conv_w[l], conv_b[l], w_down[l])

    return rmsnorm(x, final_norm)
```

```python
import functools
import math

import jax
import jax.numpy as jnp
from jax import lax
from jax.experimental import pallas as pl
from jax.experimental.pallas import tpu as pltpu

D_MODEL = 1024
GRID_W = 64
HEAD_DIM = 64
N_HEADS_A = 8
N_KV_A = 2
N_HEADS_B = 4
MLA_Q_RANK = 192
MLA_KV_RANK = 128
MLA_NOPE = 64
MLA_ROPE = 32
MLA_V = 64
N_HEADS_C = 4
DIFF_QK = 32
DIFF_V = 64
D_FF = 2816
ROPE_BASE = 10000.0
EPS = 1e-6

LANES = 128
BF16_SUBLANES = 16
VMEM_LIMIT_BYTES = 56 * 1024 * 1024

LOG2E = math.log2(math.e)
V_ONES_LANE = 64

OFF_QA = 0
OFF_KA = OFF_QA + N_HEADS_A * LANES
OFF_VA = OFF_KA + N_KV_A * LANES
OFF_CQ = OFF_VA + N_KV_A * LANES
CQ_PAD = 2 * LANES
OFF_CKV = OFF_CQ + CQ_PAD
OFF_KR = OFF_CKV + MLA_KV_RANK
OFF_QC = OFF_KR + LANES
OFF_KC = OFF_QC + N_HEADS_C * LANES
OFF_VC = OFF_KC + N_HEADS_C * LANES
IN_PAD = OFF_VC + N_HEADS_C * LANES

F32 = jnp.float32
BF16 = jnp.bfloat16

TM_IN = 512
TM_OUT = 512
TM_FFN = 512
FC_FFN = 1408
TQ = 256
TK = 512


def _dot(a, b):
    return jnp.dot(a, b, preferred_element_type=F32)


def _rms(x, n):
    ms = jnp.sum(x * x, axis=-1, keepdims=True) * (1.0 / n)
    return x * lax.rsqrt(ms + EPS)


def _in_kernel(x_ref, gat_ref, w_ref, cos_a_ref, sin_a_ref, cos_b_ref, sin_b_ref,
               qg_a_ref, kg_a_ref, qg_b_ref, kvg_b_ref, wuq_ref, wukv_ref,
               qa_ref, kta_ref, va_ref, qb_ref, ktb_ref, vb_ref, qc_ref, ktc_ref, vc_ref):
    x = x_ref[...]
    h = (_rms(x, D_MODEL) * gat_ref[...]).astype(BF16)

    lane = lax.broadcasted_iota(jnp.int32, (1, LANES), 1)
    ones_col = (lane == V_ONES_LANE).astype(F32)

    def seg(off, width):
        return _dot(h, w_ref[:, off:off + width])

    def swap(y, nf):
        up = pltpu.roll(y, LANES - nf, axis=1)
        down = pltpu.roll(y, nf, axis=1)
        return jnp.where((lane & nf) == 0, up, down)

    def rope(y, c, s, nf):
        return y * c + swap(y, nf) * s

    def group(z, g):
        return z[:, g * LANES:(g + 1) * LANES]

    cos_a, sin_a = cos_a_ref[...], sin_a_ref[...]
    nf_a = HEAD_DIM // 4
    zq = seg(OFF_QA, N_HEADS_A * LANES)
    for hh in range(N_HEADS_A):
        y = rope(_rms(group(zq, hh), HEAD_DIM) * qg_a_ref[...], cos_a, sin_a, nf_a)
        qa_ref[0, hh] = (y * (HEAD_DIM ** -0.5 * LOG2E)).astype(BF16)
    zk = seg(OFF_KA, N_KV_A * LANES)
    zv = seg(OFF_VA, N_KV_A * LANES)
    for g in range(N_KV_A):
        y = rope(_rms(group(zk, g), HEAD_DIM) * kg_a_ref[...], cos_a, sin_a, nf_a)
        kta_ref[0, g] = y.T.astype(BF16)
        va_ref[0, g] = (group(zv, g) + ones_col).astype(BF16)

    cos_b, sin_b = cos_b_ref[...], sin_b_ref[...]
    nf_b = MLA_ROPE // 4
    cq = (_rms(seg(OFF_CQ, CQ_PAD), MLA_Q_RANK) * qg_b_ref[...]).astype(BF16)
    qb = _dot(cq, wuq_ref[...])
    scale_b = (MLA_NOPE + MLA_ROPE) ** -0.5 * LOG2E
    for hh in range(N_HEADS_B):
        qb_ref[0, hh] = (rope(group(qb, hh), cos_b, sin_b, nf_b) * scale_b).astype(BF16)
    ckv = (_rms(seg(OFF_CKV, MLA_KV_RANK), MLA_KV_RANK) * kvg_b_ref[...]).astype(BF16)
    kvb = _dot(ckv, wukv_ref[...])
    kr = rope(seg(OFF_KR, LANES), cos_b, sin_b, nf_b)
    for hh in range(N_HEADS_B):
        ktb_ref[0, hh] = (group(kvb, hh) + kr).T.astype(BF16)
        vb_ref[0, hh] = (group(kvb, N_HEADS_B + hh) + ones_col).astype(BF16)

    zq = seg(OFF_QC, N_HEADS_C * LANES)
    zk = seg(OFF_KC, N_HEADS_C * LANES)
    zv = seg(OFF_VC, N_HEADS_C * LANES)
    row = lax.broadcasted_iota(jnp.int32, (LANES, 1), 0)
    for hh in range(N_HEADS_C):
        qc_ref[0, hh] = (group(zq, hh) * (DIFF_QK ** -0.5 * LOG2E)).astype(BF16)
        kt = group(zk, hh).T
        ktc_ref[0, 2 * hh] = jnp.where(row < DIFF_QK, kt, 0.0).astype(BF16)
        ktc_ref[0, 2 * hh + 1] = jnp.where(row >= DIFF_QK, kt, 0.0).astype(BF16)
        vc_ref[0, hh] = (group(zv, hh) + ones_col).astype(BF16)


def _in_proj(x2d, gat, w_in_p, tabs, qg_a, kg_a, qg_b, kvg_b, wuq_p, wukv_p, B, S):
    T = B * S
    tm = TM_IN
    nst = S // tm
    const = lambda shape: pl.BlockSpec(shape, lambda i: (0,) * len(shape))
    tab = pl.BlockSpec((tm, LANES), lambda i: (i % nst, 0))
    hm = lambda H: pl.BlockSpec((1, H, tm, LANES), lambda i: (i // nst, 0, i % nst, 0))
    hmt = lambda H: pl.BlockSpec((1, H, LANES, tm), lambda i: (i // nst, 0, 0, i % nst))
    sds = lambda H: jax.ShapeDtypeStruct((B, H, S, LANES), BF16)
    sdt = lambda H: jax.ShapeDtypeStruct((B, H, LANES, S), BF16)
    return pl.pallas_call(
        _in_kernel,
        grid=(T // tm,),
        in_specs=[pl.BlockSpec((tm, D_MODEL), lambda i: (i, 0)),
                  const((1, D_MODEL)), const((D_MODEL, IN_PAD)),
                  tab, tab, tab, tab,
                  const((1, LANES)), const((1, LANES)), const((1, CQ_PAD)), const((1, MLA_KV_RANK)),
                  const((CQ_PAD, N_HEADS_B * LANES)), const((MLA_KV_RANK, 2 * N_HEADS_B * LANES))],
        out_specs=[hm(N_HEADS_A), hmt(N_KV_A), hm(N_KV_A),
                   hm(N_HEADS_B), hmt(N_HEADS_B), hm(N_HEADS_B),
                   hm(N_HEADS_C), hmt(2 * N_HEADS_C), hm(N_HEADS_C)],
        out_shape=[sds(N_HEADS_A), sdt(N_KV_A), sds(N_KV_A),
                   sds(N_HEADS_B), sdt(N_HEADS_B), sds(N_HEADS_B),
                   sds(N_HEADS_C), sdt(2 * N_HEADS_C), sds(N_HEADS_C)],
        compiler_params=pltpu.CompilerParams(dimension_semantics=("arbitrary",),
                                             vmem_limit_bytes=VMEM_LIMIT_BYTES),
        name="in_proj",
    )(x2d, gat, w_in_p, *tabs, qg_a, kg_a, qg_b, kvg_b, wuq_p, wukv_p)


def _softmax_pv(q, kt_ref, kt_idx, v_ref, v_idx, s_ref, n_keys, bias_fn=None):
    tq = q.shape[0]
    n_chunks = n_keys // TK
    m_run = jnp.full((tq, LANES), -jnp.inf, F32)
    for c in range(n_chunks):
        s = _dot(q, kt_ref[0, kt_idx, :, c * TK:(c + 1) * TK])
        if bias_fn is not None:
            s = s + bias_fn(c)
        s_ref[:, c * TK:(c + 1) * TK] = s
        for j in range(TK // LANES):
            m_run = jnp.maximum(m_run, s[:, j * LANES:(j + 1) * LANES])
    m = jnp.max(m_run, axis=-1, keepdims=True)
    acc = jnp.zeros((tq, LANES), F32)
    for c in range(n_chunks):
        p = jnp.exp2(s_ref[:, c * TK:(c + 1) * TK] - m).astype(BF16)
        acc = acc + _dot(p, v_ref[0, v_idx, c * TK:(c + 1) * TK, :])
    return acc


def _attn_ab_kernel(q_ref, kt_ref, v_ref, o_ref, s_ref, acc_ref, *, n_heads, group, n_keys):
    def head(h, carry):
        kv = h // group
        acc_ref[h] = _softmax_pv(q_ref[0, h], kt_ref, kv, v_ref, kv, s_ref, n_keys)
        return carry

    lax.fori_loop(0, n_heads, head, 0)
    outs = []
    for h in range(n_heads):
        a = acc_ref[h]
        outs.append(a[:, :HEAD_DIM] * (1.0 / a[:, V_ONES_LANE:V_ONES_LANE + 1]))
    o_ref[...] = jnp.concatenate(outs, axis=-1).astype(BF16)


def _attn_c_kernel(q_ref, kt_ref, v_ref, lq1_ref, lk1_ref, lq2_ref, lk2_ref, subln_ref,
                   o_ref, s_ref, acc_ref, *, n_keys, lam_init):
    tq = q_ref.shape[2]
    i0 = pl.program_id(1) * tq
    rel = (lax.broadcasted_iota(jnp.int32, (tq, TK), 0)
           - lax.broadcasted_iota(jnp.int32, (tq, TK), 1)).astype(F32)

    def sub_head(idx, carry):
        hh = idx // 2
        neg_slope = jnp.float32(-LOG2E * 2.0 ** (-8.0 * N_HEADS_C / N_HEADS_C))
        for k in range(N_HEADS_C - 2, -1, -1):
            neg_slope = jnp.where(hh == k, -LOG2E * 2.0 ** (-8.0 * (k + 1) / N_HEADS_C), neg_slope)

        def bias(c):
            return jnp.abs(rel + (i0 - c * TK).astype(F32)) * neg_slope

        acc_ref[idx] = _softmax_pv(q_ref[0, hh], kt_ref, idx, v_ref, hh, s_ref, n_keys, bias)
        return carry

    lax.fori_loop(0, 2 * N_HEADS_C, sub_head, 0)

    lam = (jnp.exp(jnp.sum(lq1_ref[...] * lk1_ref[...], axis=-1, keepdims=True))
           - jnp.exp(jnp.sum(lq2_ref[...] * lk2_ref[...], axis=-1, keepdims=True)) + lam_init)
    outs = []
    for hh in range(N_HEADS_C):
        a0 = acc_ref[2 * hh]
        a1 = acc_ref[2 * hh + 1]
        o = (a0[:, :DIFF_V] * (1.0 / a0[:, V_ONES_LANE:V_ONES_LANE + 1])
             - lam * (a1[:, :DIFF_V] * (1.0 / a1[:, V_ONES_LANE:V_ONES_LANE + 1])))
        outs.append(_rms(o, DIFF_V) * subln_ref[...] * (1.0 - lam_init))
    o_ref[...] = jnp.concatenate(outs, axis=-1).astype(BF16)


def _attention(kernel_fn, q, kt, v, extras, out_width, B, S, n_acc):
    nq = S // TQ
    hq, hk, hv = q.shape[1], kt.shape[1], v.shape[1]
    extra_specs = [pl.BlockSpec(e.shape, lambda b, i, n=e.ndim: (0,) * n) for e in extras]
    return pl.pallas_call(
        kernel_fn,
        grid=(B, nq),
        in_specs=[pl.BlockSpec((1, hq, TQ, LANES), lambda b, i: (b, 0, i, 0)),
                  pl.BlockSpec((1, hk, LANES, S), lambda b, i: (b, 0, 0, 0)),
                  pl.BlockSpec((1, hv, S, LANES), lambda b, i: (b, 0, 0, 0))] + extra_specs,
        out_specs=pl.BlockSpec((TQ, out_width), lambda b, i: (b * nq + i, 0)),
        out_shape=jax.ShapeDtypeStruct((B * S, out_width), BF16),
        scratch_shapes=[pltpu.VMEM((TQ, S), F32), pltpu.VMEM((n_acc, TQ, LANES), F32)],
        compiler_params=pltpu.CompilerParams(dimension_semantics=("arbitrary", "arbitrary"),
                                             vmem_limit_bytes=VMEM_LIMIT_BYTES),
        name=kernel_fn.func.__name__.strip("_"),
    )(q, kt, v, *extras)


def _out_kernel(x_ref, oa_ref, ob_ref, oc_ref, wo_ref, g_ref, x1_ref, h2_ref):
    na = N_HEADS_A * HEAD_DIM
    nb = na + N_HEADS_B * MLA_V
    y = (x_ref[...] + _dot(oa_ref[...], wo_ref[:na]) + _dot(ob_ref[...], wo_ref[na:nb])
         + _dot(oc_ref[...], wo_ref[nb:]))
    x1_ref[...] = y
    h2_ref[...] = (_rms(y, D_MODEL) * g_ref[...]).astype(BF16)


def _out_proj(x2d, oa, ob, oc, w_out, g_ffn):
    T = x2d.shape[0]
    tm = TM_OUT
    rows = lambda w: pl.BlockSpec((tm, w), lambda i: (i, 0))
    return pl.pallas_call(
        _out_kernel,
        grid=(T // tm,),
        in_specs=[rows(D_MODEL), rows(oa.shape[1]), rows(ob.shape[1]), rows(oc.shape[1]),
                  pl.BlockSpec(w_out.shape, lambda i: (0, 0)),
                  pl.BlockSpec((1, D_MODEL), lambda i: (0, 0))],
        out_specs=[rows(D_MODEL), rows(D_MODEL)],
        out_shape=[jax.ShapeDtypeStruct((T, D_MODEL), F32),
                   jax.ShapeDtypeStruct((T, D_MODEL), BF16)],
        compiler_params=pltpu.CompilerParams(dimension_semantics=("arbitrary",),
                                             vmem_limit_bytes=VMEM_LIMIT_BYTES),
        name="out_proj",
    )(x2d, oa, ob, oc, w_out, g_ffn)


def _ffn_kernel(hp_ref, h_ref, hn_ref, x1_ref, wg_ref, wv_ref, cw_ref, cb_ref, wd_ref, fg_ref,
                o_ref, acc_ref, *, tiles_per_seq, final_norm):
    i = pl.program_id(0)
    f = pl.program_id(1)
    tm = h_ref.shape[0]
    halo = hp_ref.shape[0]
    at_start = (i % tiles_per_seq) == 0
    at_end = (i % tiles_per_seq) == tiles_per_seq - 1
    hp = jnp.where(at_start, jnp.zeros_like(hp_ref[...]), hp_ref[...])
    hn = jnp.where(at_end, jnp.zeros_like(hn_ref[...]), hn_ref[...])
    hext = jnp.concatenate([hp, h_ref[...], hn], axis=0)

    def branch(w_ref, k):
        u = _dot(hext, w_ref[...])
        cw = cw_ref[k]
        return (u[halo - 1:halo - 1 + tm] * cw[0:1] + u[halo:halo + tm] * cw[1:2]
                + u[halo + 1:halo + 1 + tm] * cw[2:3] + cb_ref[k])

    g = branch(wg_ref, 0)
    val = branch(wv_ref, 1)
    act = (g / (1.0 + jnp.exp(-g)) * val).astype(BF16)
    part = _dot(act, wd_ref[...])

    @pl.when(f == 0)
    def _():
        acc_ref[...] = x1_ref[...] + part

    @pl.when(f > 0)
    def _():
        acc_ref[...] += part

    @pl.when(f == pl.num_programs(1) - 1)
    def _():
        y = acc_ref[...]
        if final_norm:
            y = _rms(y, D_MODEL) * fg_ref[...]
        o_ref[...] = y


def _ffn(h2, x1, w_up, conv_w, conv_b, w_down, final_g, S, final_norm):
    T = h2.shape[0]
    tm, fc, halo = TM_FFN, FC_FFN, BF16_SUBLANES
    nfc = D_FF // fc
    per = tm // halo
    last = T // halo - 1
    return pl.pallas_call(
        functools.partial(_ffn_kernel, tiles_per_seq=S // tm, final_norm=final_norm),
        grid=(T // tm, nfc),
        in_specs=[pl.BlockSpec((halo, D_MODEL), lambda i, f: (jnp.maximum(i * per - 1, 0), 0)),
                  pl.BlockSpec((tm, D_MODEL), lambda i, f: (i, 0)),
                  pl.BlockSpec((halo, D_MODEL), lambda i, f: (jnp.minimum((i + 1) * per, last), 0)),
                  pl.BlockSpec((tm, D_MODEL), lambda i, f: (i, 0)),
                  pl.BlockSpec((D_MODEL, fc), lambda i, f: (0, f)),
                  pl.BlockSpec((D_MODEL, fc), lambda i, f: (0, nfc + f)),
                  pl.BlockSpec((2, 3, fc), lambda i, f: (0, 0, f)),
                  pl.BlockSpec((2, 1, fc), lambda i, f: (0, 0, f)),
                  pl.BlockSpec((fc, D_MODEL), lambda i, f: (f, 0)),
                  pl.BlockSpec((1, D_MODEL), lambda i, f: (0, 0))],
        out_specs=pl.BlockSpec((tm, D_MODEL), lambda i, f: (i, 0)),
        out_shape=jax.ShapeDtypeStruct((T, D_MODEL), F32),
        scratch_shapes=[pltpu.VMEM((tm, D_MODEL), F32)],
        compiler_params=pltpu.CompilerParams(dimension_semantics=("arbitrary", "arbitrary"),
                                             vmem_limit_bytes=VMEM_LIMIT_BYTES),
        name="conv_mlp",
    )(h2, h2, h2, x1, w_up, w_up, conv_w, conv_b, w_down, final_g)


def _pad_groups(w, n_groups, width):
    r = w.shape[0]
    w = w.reshape(r, n_groups, width)
    w = jnp.pad(w, ((0, 0), (0, 0), (0, LANES - width)))
    return w.reshape(r, n_groups * LANES)


def _rope_tables(S, dim, offset, identity_below):
    nf = dim // 4
    t = jnp.arange(S, dtype=jnp.int32)
    rows = (t // GRID_W).astype(F32)
    cols = (t % GRID_W).astype(F32)
    inv = ROPE_BASE ** (-jnp.arange(nf, dtype=F32) / nf)
    ar = rows[:, None] * inv
    ac = cols[:, None] * inv
    cos = jnp.concatenate([jnp.cos(ar), jnp.cos(ar), jnp.cos(ac), jnp.cos(ac)], axis=-1)
    sin = jnp.concatenate([-jnp.sin(ar), jnp.sin(ar), -jnp.sin(ac), jnp.sin(ac)], axis=-1)
    below = jnp.ones if identity_below else jnp.zeros
    cos = jnp.concatenate([below((S, offset), F32), cos,
                           jnp.zeros((S, LANES - offset - dim), F32)], axis=-1)
    sin = jnp.concatenate([jnp.zeros((S, offset), F32), sin,
                           jnp.zeros((S, LANES - offset - dim), F32)], axis=-1)
    return cos, sin


def _prep_w_in(w):
    d = w.shape[0]
    na, nkv = N_HEADS_A * HEAD_DIM, N_KV_A * HEAD_DIM
    nc_qk, nc_v = N_HEADS_C * 2 * DIFF_QK, N_HEADS_C * DIFF_V
    sizes = [na, nkv, nkv, MLA_Q_RANK, MLA_KV_RANK, MLA_ROPE, nc_qk, nc_qk, nc_v]
    offs = [0]
    for s in sizes:
        offs.append(offs[-1] + s)
    qa, ka, va, cq, ckv, kr, qc, kc, vc = [w[:, offs[k]:offs[k + 1]] for k in range(len(sizes))]
    parts = [
        _pad_groups(qa, N_HEADS_A, HEAD_DIM),
        _pad_groups(ka, N_KV_A, HEAD_DIM),
        _pad_groups(va, N_KV_A, HEAD_DIM),
        jnp.pad(cq, ((0, 0), (0, CQ_PAD - MLA_Q_RANK))),
        ckv,
        jnp.pad(kr, ((0, 0), (MLA_NOPE, LANES - MLA_NOPE - MLA_ROPE))),
        _pad_groups(qc, N_HEADS_C, 2 * DIFF_QK),
        _pad_groups(kc, N_HEADS_C, 2 * DIFF_QK),
        _pad_groups(vc, N_HEADS_C, DIFF_V),
    ]
    out = jnp.concatenate(parts, axis=-1).astype(BF16)
    assert out.shape == (d, IN_PAD)
    return out


def _pad_lanes(v, width):
    return jnp.pad(v, (0, width - v.shape[0])).reshape(1, width)


def kernel(x, norm_attn, w_in, q_norm_a, k_norm_a, q_a_norm_b, w_uq_b, kv_a_norm_b, w_ukv_b,
           lambda_q1_c, lambda_k1_c, lambda_q2_c, lambda_k2_c, subln_c, w_out,
           norm_ffn, w_up, conv_w, conv_b, w_down, final_norm):
    B, S, D = x.shape
    depth = w_in.shape[0]
    assert D == D_MODEL and S % max(TM_IN, TM_OUT, TM_FFN, TQ, TK) == 0 and S % GRID_W == 0
    T = B * S

    tabs = _rope_tables(S, HEAD_DIM, 0, False) + _rope_tables(S, MLA_ROPE, MLA_NOPE, True)
    row = lambda v: v.reshape(1, -1)
    xc = x.reshape(T, D)
    for l in range(depth):
        lam_init = 0.8 - 0.6 * math.exp(-0.3 * l)
        wuq_p = jnp.pad(_pad_groups(w_uq_b[l], N_HEADS_B, MLA_NOPE + MLA_ROPE),
                        ((0, CQ_PAD - MLA_Q_RANK), (0, 0))).astype(BF16)
        wukv = w_ukv_b[l].reshape(MLA_KV_RANK, N_HEADS_B, MLA_NOPE + MLA_V)
        wukv_p = jnp.concatenate(
            [_pad_groups(wukv[:, :, :MLA_NOPE].reshape(MLA_KV_RANK, -1), N_HEADS_B, MLA_NOPE),
             _pad_groups(wukv[:, :, MLA_NOPE:].reshape(MLA_KV_RANK, -1), N_HEADS_B, MLA_V)],
            axis=-1).astype(BF16)

        qa, kta, va, qb, ktb, vb, qc, ktc, vc = _in_proj(
            xc, row(norm_attn[l]), _prep_w_in(w_in[l]), tabs,
            _pad_lanes(q_norm_a[l], LANES), _pad_lanes(k_norm_a[l], LANES),
            _pad_lanes(q_a_norm_b[l], CQ_PAD), row(kv_a_norm_b[l]), wuq_p, wukv_p, B, S)

        oa = _attention(
            functools.partial(_attn_ab_kernel, n_heads=N_HEADS_A, group=N_HEADS_A // N_KV_A,
                              n_keys=S),
            qa, kta, va, [], N_HEADS_A * HEAD_DIM, B, S, N_HEADS_A)
        ob = _attention(
            functools.partial(_attn_ab_kernel, n_heads=N_HEADS_B, group=1, n_keys=S),
            qb, ktb, vb, [], N_HEADS_B * MLA_V, B, S, N_HEADS_B)
        oc = _attention(
            functools.partial(_attn_c_kernel, n_keys=S, lam_init=lam_init),
            qc, ktc, vc,
            [row(lambda_q1_c[l]), row(lambda_k1_c[l]), row(lambda_q2_c[l]), row(lambda_k2_c[l]),
             row(subln_c[l])],
            N_HEADS_C * DIFF_V, B, S, 2 * N_HEADS_C)

        x1, h2 = _out_proj(xc, oa, ob, oc, w_out[l].astype(BF16), row(norm_ffn[l]))
        xc = _ffn(h2, x1, w_up[l].astype(BF16),
                  conv_w[l].reshape(3, 2, D_FF).transpose(1, 0, 2),
                  conv_b[l].reshape(2, 1, D_FF), w_down[l].astype(BF16),
                  row(final_norm), S, final_norm=(l == depth - 1))
    return xc.reshape(B, S, D)
```

```python
import functools
import math

import jax
import jax.numpy as jnp
from jax import lax
from jax.experimental import pallas as pl
from jax.experimental.pallas import tpu as pltpu

D_MODEL = 1024
GRID_W = 64
HEAD_DIM = 64
N_HEADS_A = 8
N_KV_A = 2
N_HEADS_B = 4
MLA_Q_RANK = 192
MLA_KV_RANK = 128
MLA_NOPE = 64
MLA_ROPE = 32
MLA_V = 64
N_HEADS_C = 4
DIFF_QK = 32
DIFF_V = 64
D_FF = 2816
ROPE_BASE = 10000.0
EPS = 1e-6

LANES = 128
BF16_SUBLANES = 16
VMEM_LIMIT_BYTES = 56 * 1024 * 1024

LOG2E = math.log2(math.e)
V_ONES_LANE = 64

OFF_QA = 0
OFF_KA = OFF_QA + N_HEADS_A * LANES
OFF_VA = OFF_KA + N_KV_A * LANES
OFF_CQ = OFF_VA + N_KV_A * LANES
CQ_PAD = 2 * LANES
OFF_CKV = OFF_CQ + CQ_PAD
OFF_KR = OFF_CKV + MLA_KV_RANK
OFF_QC = OFF_KR + LANES
OFF_KC = OFF_QC + N_HEADS_C * LANES
OFF_VC = OFF_KC + N_HEADS_C * LANES
IN_PAD = OFF_VC + N_HEADS_C * LANES

F32 = jnp.float32
BF16 = jnp.bfloat16

TM_IN = 512
TM_OUT = 512
TM_FFN = 512
FC_FFN = 1408
TQ = 512
TQ_SUB = 256
TK = 512


def _dot(a, b):
    return jnp.dot(a, b, preferred_element_type=F32)


def _rms(x, n):
    ms = jnp.sum(x * x, axis=-1, keepdims=True) * (1.0 / n)
    return x * lax.rsqrt(ms + EPS)


def _in_kernel(x_ref, gat_ref, w_ref, cos_a_ref, sin_a_ref, cos_b_ref, sin_b_ref,
               qg_a_ref, kg_a_ref, qg_b_ref, kvg_b_ref, wuq_ref, wukv_ref,
               qa_ref, kta_ref, va_ref, qb_ref, ktb_ref, vb_ref, qc_ref, ktc_ref, vc_ref):
    x = x_ref[...]
    h = (_rms(x, D_MODEL) * gat_ref[...]).astype(BF16)

    lane = lax.broadcasted_iota(jnp.int32, (1, LANES), 1)
    ones_col = (lane == V_ONES_LANE).astype(F32)

    def seg(off, width):
        return _dot(h, w_ref[:, off:off + width])

    def swap(y, nf):
        up = pltpu.roll(y, LANES - nf, axis=1)
        down = pltpu.roll(y, nf, axis=1)
        return jnp.where((lane & nf) == 0, up, down)

    def rope(y, c, s, nf):
        return y * c + swap(y, nf) * s

    def group(z, g):
        return z[:, g * LANES:(g + 1) * LANES]

    cos_a, sin_a = cos_a_ref[...], sin_a_ref[...]
    nf_a = HEAD_DIM // 4
    zq = seg(OFF_QA, N_HEADS_A * LANES)
    for hh in range(N_HEADS_A):
        y = rope(_rms(group(zq, hh), HEAD_DIM) * qg_a_ref[...], cos_a, sin_a, nf_a)
        qa_ref[0, hh] = (y * (HEAD_DIM ** -0.5 * LOG2E)).astype(BF16)
    zk = seg(OFF_KA, N_KV_A * LANES)
    zv = seg(OFF_VA, N_KV_A * LANES)
    for g in range(N_KV_A):
        y = rope(_rms(group(zk, g), HEAD_DIM) * kg_a_ref[...], cos_a, sin_a, nf_a)
        kta_ref[0, g] = y.T.astype(BF16)
        va_ref[0, g] = (group(zv, g) + ones_col).astype(BF16)

    cos_b, sin_b = cos_b_ref[...], sin_b_ref[...]
    nf_b = MLA_ROPE // 4
    cq = (_rms(seg(OFF_CQ, CQ_PAD), MLA_Q_RANK) * qg_b_ref[...]).astype(BF16)
    qb = _dot(cq, wuq_ref[...])
    scale_b = (MLA_NOPE + MLA_ROPE) ** -0.5 * LOG2E
    for hh in range(N_HEADS_B):
        qb_ref[0, hh] = (rope(group(qb, hh), cos_b, sin_b, nf_b) * scale_b).astype(BF16)
    ckv = (_rms(seg(OFF_CKV, MLA_KV_RANK), MLA_KV_RANK) * kvg_b_ref[...]).astype(BF16)
    kvb = _dot(ckv, wukv_ref[...])
    kr = rope(seg(OFF_KR, LANES), cos_b, sin_b, nf_b)
    for hh in range(N_HEADS_B):
        ktb_ref[0, hh] = (group(kvb, hh) + kr).T.astype(BF16)
        vb_ref[0, hh] = (group(kvb, N_HEADS_B + hh) + ones_col).astype(BF16)

    zq = seg(OFF_QC, N_HEADS_C * LANES)
    zk = seg(OFF_KC, N_HEADS_C * LANES)
    zv = seg(OFF_VC, N_HEADS_C * LANES)
    row = lax.broadcasted_iota(jnp.int32, (LANES, 1), 0)
    for hh in range(N_HEADS_C):
        qc_ref[0, hh] = (group(zq, hh) * (DIFF_QK ** -0.5 * LOG2E)).astype(BF16)
        kt = group(zk, hh).T
        ktc_ref[0, 2 * hh] = jnp.where(row < DIFF_QK, kt, 0.0).astype(BF16)
        ktc_ref[0, 2 * hh + 1] = jnp.where(row >= DIFF_QK, kt, 0.0).astype(BF16)
        vc_ref[0, hh] = (group(zv, hh) + ones_col).astype(BF16)


def _in_proj(x2d, gat, w_in_p, tabs, qg_a, kg_a, qg_b, kvg_b, wuq_p, wukv_p, B, S):
    T = B * S
    tm = TM_IN
    nst = S // tm
    const = lambda shape: pl.BlockSpec(shape, lambda i: (0,) * len(shape))
    tab = pl.BlockSpec((tm, LANES), lambda i: (i % nst, 0))
    hm = lambda H: pl.BlockSpec((1, H, tm, LANES), lambda i: (i // nst, 0, i % nst, 0))
    hmt = lambda H: pl.BlockSpec((1, H, LANES, tm), lambda i: (i // nst, 0, 0, i % nst))
    sds = lambda H: jax.ShapeDtypeStruct((B, H, S, LANES), BF16)
    sdt = lambda H: jax.ShapeDtypeStruct((B, H, LANES, S), BF16)
    return pl.pallas_call(
        _in_kernel,
        grid=(T // tm,),
        in_specs=[pl.BlockSpec((tm, D_MODEL), lambda i: (i, 0)),
                  const((1, D_MODEL)), const((D_MODEL, IN_PAD)),
                  tab, tab, tab, tab,
                  const((1, LANES)), const((1, LANES)), const((1, CQ_PAD)), const((1, MLA_KV_RANK)),
                  const((CQ_PAD, N_HEADS_B * LANES)), const((MLA_KV_RANK, 2 * N_HEADS_B * LANES))],
        out_specs=[hm(N_HEADS_A), hmt(N_KV_A), hm(N_KV_A),
                   hm(N_HEADS_B), hmt(N_HEADS_B), hm(N_HEADS_B),
                   hm(N_HEADS_C), hmt(2 * N_HEADS_C), hm(N_HEADS_C)],
        out_shape=[sds(N_HEADS_A), sdt(N_KV_A), sds(N_KV_A),
                   sds(N_HEADS_B), sdt(N_HEADS_B), sds(N_HEADS_B),
                   sds(N_HEADS_C), sdt(2 * N_HEADS_C), sds(N_HEADS_C)],
        compiler_params=pltpu.CompilerParams(dimension_semantics=("arbitrary",),
                                             vmem_limit_bytes=VMEM_LIMIT_BYTES),
        name="in_proj",
    )(x2d, gat, w_in_p, *tabs, qg_a, kg_a, qg_b, kvg_b, wuq_p, wukv_p)


def _softmax_pv(q, kt_ref, kt_idx, v_ref, v_idx, s_ref, n_keys, bias_fn=None):
    tq = q.shape[0]
    n_chunks = n_keys // TK
    subs = [(r, r + TQ_SUB) for r in range(0, tq, TQ_SUB)]
    maxes = []
    for r0, r1 in subs:
        qs = q[r0:r1]
        m_run = jnp.full((TQ_SUB, LANES), -jnp.inf, F32)
        for c in range(n_chunks):
            s = _dot(qs, kt_ref[0, kt_idx, :, c * TK:(c + 1) * TK])
            if bias_fn is not None:
                s = s + bias_fn(c, r0)
            s_ref[r0:r1, c * TK:(c + 1) * TK] = s
            for j in range(TK // LANES):
                m_run = jnp.maximum(m_run, s[:, j * LANES:(j + 1) * LANES])
        maxes.append(jnp.max(m_run, axis=-1, keepdims=True))
    accs = []
    for (r0, r1), m in zip(subs, maxes):
        acc = jnp.zeros((TQ_SUB, LANES), F32)
        for c in range(n_chunks):
            p = jnp.exp2(s_ref[r0:r1, c * TK:(c + 1) * TK] - m).astype(BF16)
            acc = acc + _dot(p, v_ref[0, v_idx, c * TK:(c + 1) * TK, :])
        accs.append(acc)
    return jnp.concatenate(accs, axis=0)


def _attn_ab_kernel(q_ref, kt_ref, v_ref, o_ref, s_ref, acc_ref, *, n_heads, group, n_keys):
    def head(h, carry):
        kv = h // group
        acc_ref[h] = _softmax_pv(q_ref[0, h], kt_ref, kv, v_ref, kv, s_ref, n_keys)
        return carry

    lax.fori_loop(0, n_heads, head, 0)
    outs = []
    for h in range(n_heads):
        a = acc_ref[h]
        outs.append(a[:, :HEAD_DIM] * (1.0 / a[:, V_ONES_LANE:V_ONES_LANE + 1]))
    o_ref[...] = jnp.concatenate(outs, axis=-1).astype(BF16)


def _attn_c_kernel(q_ref, kt_ref, v_ref, lq1_ref, lk1_ref, lq2_ref, lk2_ref, subln_ref,
                   o_ref, s_ref, acc_ref, *, n_keys, lam_init):
    tq = q_ref.shape[2]
    i0 = pl.program_id(1) * tq
    rel = (lax.broadcasted_iota(jnp.int32, (TQ_SUB, TK), 0)
           - lax.broadcasted_iota(jnp.int32, (TQ_SUB, TK), 1)).astype(F32)

    def sub_head(idx, carry):
        hh = idx // 2
        neg_slope = jnp.float32(-LOG2E * 2.0 ** (-8.0 * N_HEADS_C / N_HEADS_C))
        for k in range(N_HEADS_C - 2, -1, -1):
            neg_slope = jnp.where(hh == k, -LOG2E * 2.0 ** (-8.0 * (k + 1) / N_HEADS_C), neg_slope)

        def bias(c, r0):
            return jnp.abs(rel + (i0 + (r0 - c * TK)).astype(F32)) * neg_slope

        acc_ref[idx] = _softmax_pv(q_ref[0, hh], kt_ref, idx, v_ref, hh, s_ref, n_keys, bias)
        return carry

    lax.fori_loop(0, 2 * N_HEADS_C, sub_head, 0)

    lam = (jnp.exp(jnp.sum(lq1_ref[...] * lk1_ref[...], axis=-1, keepdims=True))
           - jnp.exp(jnp.sum(lq2_ref[...] * lk2_ref[...], axis=-1, keepdims=True)) + lam_init)
    outs = []
    for hh in range(N_HEADS_C):
        a0 = acc_ref[2 * hh]
        a1 = acc_ref[2 * hh + 1]
        o = (a0[:, :DIFF_V] * (1.0 / a0[:, V_ONES_LANE:V_ONES_LANE + 1])
             - lam * (a1[:, :DIFF_V] * (1.0 / a1[:, V_ONES_LANE:V_ONES_LANE + 1])))
        outs.append(_rms(o, DIFF_V) * subln_ref[...] * (1.0 - lam_init))
    o_ref[...] = jnp.concatenate(outs, axis=-1).astype(BF16)


def _attention(kernel_fn, q, kt, v, extras, out_width, B, S, n_acc):
    nq = S // TQ
    hq, hk, hv = q.shape[1], kt.shape[1], v.shape[1]
    extra_specs = [pl.BlockSpec(e.shape, lambda b, i, n=e.ndim: (0,) * n) for e in extras]
    return pl.pallas_call(
        kernel_fn,
        grid=(B, nq),
        in_specs=[pl.BlockSpec((1, hq, TQ, LANES), lambda b, i: (b, 0, i, 0)),
                  pl.BlockSpec((1, hk, LANES, S), lambda b, i: (b, 0, 0, 0)),
                  pl.BlockSpec((1, hv, S, LANES), lambda b, i: (b, 0, 0, 0))] + extra_specs,
        out_specs=pl.BlockSpec((TQ, out_width), lambda b, i: (b * nq + i, 0)),
        out_shape=jax.ShapeDtypeStruct((B * S, out_width), BF16),
        scratch_shapes=[pltpu.VMEM((TQ, S), F32), pltpu.VMEM((n_acc, TQ, LANES), F32)],
        compiler_params=pltpu.CompilerParams(dimension_semantics=("arbitrary", "arbitrary"),
                                             vmem_limit_bytes=VMEM_LIMIT_BYTES),
        name=kernel_fn.func.__name__.strip("_"),
    )(q, kt, v, *extras)


def _out_kernel(x_ref, oa_ref, ob_ref, oc_ref, wo_ref, g_ref, x1_ref, h2_ref):
    na = N_HEADS_A * HEAD_DIM
    nb = na + N_HEADS_B * MLA_V
    y = (x_ref[...] + _dot(oa_ref[...], wo_ref[:na]) + _dot(ob_ref[...], wo_ref[na:nb])
         + _dot(oc_ref[...], wo_ref[nb:]))
    x1_ref[...] = y
    h2_ref[...] = (_rms(y, D_MODEL) * g_ref[...]).astype(BF16)


def _out_proj(x2d, oa, ob, oc, w_out, g_ffn):
    T = x2d.shape[0]
    tm = TM_OUT
    rows = lambda w: pl.BlockSpec((tm, w), lambda i: (i, 0))
    return pl.pallas_call(
        _out_kernel,
        grid=(T // tm,),
        in_specs=[rows(D_MODEL), rows(oa.shape[1]), rows(ob.shape[1]), rows(oc.shape[1]),
                  pl.BlockSpec(w_out.shape, lambda i: (0, 0)),
                  pl.BlockSpec((1, D_MODEL), lambda i: (0, 0))],
        out_specs=[rows(D_MODEL), rows(D_MODEL)],
        out_shape=[jax.ShapeDtypeStruct((T, D_MODEL), F32),
                   jax.ShapeDtypeStruct((T, D_MODEL), BF16)],
        compiler_params=pltpu.CompilerParams(dimension_semantics=("arbitrary",),
                                             vmem_limit_bytes=VMEM_LIMIT_BYTES),
        name="out_proj",
    )(x2d, oa, ob, oc, w_out, g_ffn)


def _ffn_kernel(hp_ref, h_ref, hn_ref, x1_ref, wg_ref, wv_ref, cw_ref, cb_ref, wd_ref, fg_ref,
                o_ref, acc_ref, *, tiles_per_seq, final_norm):
    i = pl.program_id(0)
    f = pl.program_id(1)
    tm = h_ref.shape[0]
    halo = hp_ref.shape[0]
    at_start = (i % tiles_per_seq) == 0
    at_end = (i % tiles_per_seq) == tiles_per_seq - 1
    hp = jnp.where(at_start, jnp.zeros_like(hp_ref[...]), hp_ref[...])
    hn = jnp.where(at_end, jnp.zeros_like(hn_ref[...]), hn_ref[...])
    hext = jnp.concatenate([hp, h_ref[...], hn], axis=0)

    def branch(w_ref, k):
        u = _dot(hext, w_ref[...])
        cw = cw_ref[k]
        return (u[halo - 1:halo - 1 + tm] * cw[0:1] + u[halo:halo + tm] * cw[1:2]
                + u[halo + 1:halo + 1 + tm] * cw[2:3] + cb_ref[k])

    g = branch(wg_ref, 0)
    val = branch(wv_ref, 1)
    act = (g / (1.0 + jnp.exp(-g)) * val).astype(BF16)
    part = _dot(act, wd_ref[...])

    @pl.when(f == 0)
    def _():
        acc_ref[...] = x1_ref[...] + part

    @pl.when(f > 0)
    def _():
        acc_ref[...] += part

    @pl.when(f == pl.num_programs(1) - 1)
    def _():
        y = acc_ref[...]
        if final_norm:
            y = _rms(y, D_MODEL) * fg_ref[...]
        o_ref[...] = y


def _ffn(h2, x1, w_up, conv_w, conv_b, w_down, final_g, S, final_norm):
    T = h2.shape[0]
    tm, fc, halo = TM_FFN, FC_FFN, BF16_SUBLANES
    nfc = D_FF // fc
    per = tm // halo
    last = T // halo - 1
    return pl.pallas_call(
        functools.partial(_ffn_kernel, tiles_per_seq=S // tm, final_norm=final_norm),
        grid=(T // tm, nfc),
        in_specs=[pl.BlockSpec((halo, D_MODEL), lambda i, f: (jnp.maximum(i * per - 1, 0), 0)),
                  pl.BlockSpec((tm, D_MODEL), lambda i, f: (i, 0)),
                  pl.BlockSpec((halo, D_MODEL), lambda i, f: (jnp.minimum((i + 1) * per, last), 0)),
                  pl.BlockSpec((tm, D_MODEL), lambda i, f: (i, 0)),
                  pl.BlockSpec((D_MODEL, fc), lambda i, f: (0, f)),
                  pl.BlockSpec((D_MODEL, fc), lambda i, f: (0, nfc + f)),
                  pl.BlockSpec((2, 3, fc), lambda i, f: (0, 0, f)),
                  pl.BlockSpec((2, 1, fc), lambda i, f: (0, 0, f)),
                  pl.BlockSpec((fc, D_MODEL), lambda i, f: (f, 0)),
                  pl.BlockSpec((1, D_MODEL), lambda i, f: (0, 0))],
        out_specs=pl.BlockSpec((tm, D_MODEL), lambda i, f: (i, 0)),
        out_shape=jax.ShapeDtypeStruct((T, D_MODEL), F32),
        scratch_shapes=[pltpu.VMEM((tm, D_MODEL), F32)],
        compiler_params=pltpu.CompilerParams(dimension_semantics=("arbitrary", "arbitrary"),
                                             vmem_limit_bytes=VMEM_LIMIT_BYTES),
        name="conv_mlp",
    )(h2, h2, h2, x1, w_up, w_up, conv_w, conv_b, w_down, final_g)


def _pad_groups(w, n_groups, width):
    r = w.shape[0]
    w = w.reshape(r, n_groups, width)
    w = jnp.pad(w, ((0, 0), (0, 0), (0, LANES - width)))
    return w.reshape(r, n_groups * LANES)


def _rope_tables(S, dim, offset, identity_below):
    nf = dim // 4
    t = jnp.arange(S, dtype=jnp.int32)
    rows = (t // GRID_W).astype(F32)
    cols = (t % GRID_W).astype(F32)
    inv = ROPE_BASE ** (-jnp.arange(nf, dtype=F32) / nf)
    ar = rows[:, None] * inv
    ac = cols[:, None] * inv
    cos = jnp.concatenate([jnp.cos(ar), jnp.cos(ar), jnp.cos(ac), jnp.cos(ac)], axis=-1)
    sin = jnp.concatenate([-jnp.sin(ar), jnp.sin(ar), -jnp.sin(ac), jnp.sin(ac)], axis=-1)
    below = jnp.ones if identity_below else jnp.zeros
    cos = jnp.concatenate([below((S, offset), F32), cos,
                           jnp.zeros((S, LANES - offset - dim), F32)], axis=-1)
    sin = jnp.concatenate([jnp.zeros((S, offset), F32), sin,
                           jnp.zeros((S, LANES - offset - dim), F32)], axis=-1)
    return cos, sin


def _prep_w_in(w):
    d = w.shape[0]
    na, nkv = N_HEADS_A * HEAD_DIM, N_KV_A * HEAD_DIM
    nc_qk, nc_v = N_HEADS_C * 2 * DIFF_QK, N_HEADS_C * DIFF_V
    sizes = [na, nkv, nkv, MLA_Q_RANK, MLA_KV_RANK, MLA_ROPE, nc_qk, nc_qk, nc_v]
    offs = [0]
    for s in sizes:
        offs.append(offs[-1] + s)
    qa, ka, va, cq, ckv, kr, qc, kc, vc = [w[:, offs[k]:offs[k + 1]] for k in range(len(sizes))]
    parts = [
        _pad_groups(qa, N_HEADS_A, HEAD_DIM),
        _pad_groups(ka, N_KV_A, HEAD_DIM),
        _pad_groups(va, N_KV_A, HEAD_DIM),
        jnp.pad(cq, ((0, 0), (0, CQ_PAD - MLA_Q_RANK))),
        ckv,
        jnp.pad(kr, ((0, 0), (MLA_NOPE, LANES - MLA_NOPE - MLA_ROPE))),
        _pad_groups(qc, N_HEADS_C, 2 * DIFF_QK),
        _pad_groups(kc, N_HEADS_C, 2 * DIFF_QK),
        _pad_groups(vc, N_HEADS_C, DIFF_V),
    ]
    out = jnp.concatenate(parts, axis=-1).astype(BF16)
    assert out.shape == (d, IN_PAD)
    return out


def _pad_lanes(v, width):
    return jnp.pad(v, (0, width - v.shape[0])).reshape(1, width)


def kernel(x, norm_attn, w_in, q_norm_a, k_norm_a, q_a_norm_b, w_uq_b, kv_a_norm_b, w_ukv_b,
           lambda_q1_c, lambda_k1_c, lambda_q2_c, lambda_k2_c, subln_c, w_out,
           norm_ffn, w_up, conv_w, conv_b, w_down, final_norm):
    B, S, D = x.shape
    depth = w_in.shape[0]
    assert D == D_MODEL and S % max(TM_IN, TM_OUT, TM_FFN, TQ, TK) == 0 and S % GRID_W == 0
    T = B * S

    tabs = _rope_tables(S, HEAD_DIM, 0, False) + _rope_tables(S, MLA_ROPE, MLA_NOPE, True)
    row = lambda v: v.reshape(1, -1)
    xc = x.reshape(T, D)
    for l in range(depth):
        lam_init = 0.8 - 0.6 * math.exp(-0.3 * l)
        wuq_p = jnp.pad(_pad_groups(w_uq_b[l], N_HEADS_B, MLA_NOPE + MLA_ROPE),
                        ((0, CQ_PAD - MLA_Q_RANK), (0, 0))).astype(BF16)
        wukv = w_ukv_b[l].reshape(MLA_KV_RANK, N_HEADS_B, MLA_NOPE + MLA_V)
        wukv_p = jnp.concatenate(
            [_pad_groups(wukv[:, :, :MLA_NOPE].reshape(MLA_KV_RANK, -1), N_HEADS_B, MLA_NOPE),
             _pad_groups(wukv[:, :, MLA_NOPE:].reshape(MLA_KV_RANK, -1), N_HEADS_B, MLA_V)],
            axis=-1).astype(BF16)

        qa, kta, va, qb, ktb, vb, qc, ktc, vc = _in_proj(
            xc, row(norm_attn[l]), _prep_w_in(w_in[l]), tabs,
            _pad_lanes(q_norm_a[l], LANES), _pad_lanes(k_norm_a[l], LANES),
            _pad_lanes(q_a_norm_b[l], CQ_PAD), row(kv_a_norm_b[l]), wuq_p, wukv_p, B, S)

        oa = _attention(
            functools.partial(_attn_ab_kernel, n_heads=N_HEADS_A, group=N_HEADS_A // N_KV_A,
                              n_keys=S),
            qa, kta, va, [], N_HEADS_A * HEAD_DIM, B, S, N_HEADS_A)
        ob = _attention(
            functools.partial(_attn_ab_kernel, n_heads=N_HEADS_B, group=1, n_keys=S),
            qb, ktb, vb, [], N_HEADS_B * MLA_V, B, S, N_HEADS_B)
        oc = _attention(
            functools.partial(_attn_c_kernel, n_keys=S, lam_init=lam_init),
            qc, ktc, vc,
            [row(lambda_q1_c[l]), row(lambda_k1_c[l]), row(lambda_q2_c[l]), row(lambda_k2_c[l]),
             row(subln_c[l])],
            N_HEADS_C * DIFF_V, B, S, 2 * N_HEADS_C)

        x1, h2 = _out_proj(xc, oa, ob, oc, w_out[l].astype(BF16), row(norm_ffn[l]))
        xc = _ffn(h2, x1, w_up[l].astype(BF16),
                  conv_w[l].reshape(3, 2, D_FF).transpose(1, 0, 2),
                  conv_b[l].reshape(2, 1, D_FF), w_down[l].astype(BF16),
                  row(final_norm), S, final_norm=(l == depth - 1))
    return xc.reshape(B, S, D)
```

```python
import functools
import math

import jax
import jax.numpy as jnp
import numpy as np
from jax import lax
from jax.experimental import pallas as pl
from jax.experimental.pallas import tpu as pltpu

D_MODEL = 1024
GRID_W = 64
HEAD_DIM = 64
N_HEADS_A = 8
N_KV_A = 2
N_HEADS_B = 4
MLA_Q_RANK = 192
MLA_KV_RANK = 128
MLA_NOPE = 64
MLA_ROPE = 32
MLA_V = 64
N_HEADS_C = 4
DIFF_QK = 32
DIFF_V = 64
D_FF = 2816
ROPE_BASE = 10000.0
EPS = 1e-6

LANES = 128
BF16_SUBLANES = 16
VMEM_LIMIT_BYTES = 56 * 1024 * 1024

LOG2E = math.log2(math.e)
V_ONES_LANE = 64

OFF_QA = 0
OFF_KA = OFF_QA + N_HEADS_A * LANES
OFF_VA = OFF_KA + N_KV_A * LANES
OFF_CQ = OFF_VA + N_KV_A * LANES
CQ_PAD = 2 * LANES
OFF_CKV = OFF_CQ + CQ_PAD
OFF_KR = OFF_CKV + MLA_KV_RANK
OFF_QC = OFF_KR + LANES
OFF_KC = OFF_QC + N_HEADS_C * LANES
OFF_VC = OFF_KC + N_HEADS_C * LANES
IN_PAD = OFF_VC + N_HEADS_C * LANES

NRM_QB = 0
NRM_KB = NRM_QB + N_HEADS_B
NRM_QC = NRM_KB + N_HEADS_B
NRM_KC = NRM_QC + N_HEADS_C
NRM_GROUPS = NRM_KC + N_HEADS_C
NRM_MARGIN = 1.03
SAFE_LOG2_RANGE = 55.0

F32 = jnp.float32
BF16 = jnp.bfloat16

TM_IN = 512
TM_OUT = 512
TM_FFN = 512
FC_FFN = 1408
TQ = 512
TQ_SUB = 128
TK = 512
SCORE_LOOKAHEAD = 2


def _dot(a, b):
    return jnp.dot(a, b, preferred_element_type=F32)


def _rms(x, n):
    ms = jnp.sum(x * x, axis=-1, keepdims=True) * (1.0 / n)
    return x * lax.rsqrt(ms + EPS)


def _in_kernel(x_ref, gat_ref, w_ref, cos_a_ref, sin_a_ref, cos_b_ref, sin_b_ref,
               qg_a_ref, kg_a_ref, qg_b_ref, kvg_b_ref, wuq_ref, wukv_ref, sel_ref,
               qa_ref, kta_ref, va_ref, qb_ref, ktb_ref, vb_ref, qc_ref, ktc_ref, vc_ref, nrm_ref):
    x = x_ref[...]
    h = (_rms(x, D_MODEL) * gat_ref[...]).astype(BF16)

    lane = lax.broadcasted_iota(jnp.int32, (1, LANES), 1)
    ones_col = (lane == V_ONES_LANE).astype(F32)

    def seg(off, width):
        return _dot(h, w_ref[:, off:off + width])

    def swap(y, nf):
        up = pltpu.roll(y, LANES - nf, axis=1)
        down = pltpu.roll(y, nf, axis=1)
        return jnp.where((lane & nf) == 0, up, down)

    def rope(y, c, s, nf):
        return y * c + swap(y, nf) * s

    def group(z, g):
        return z[:, g * LANES:(g + 1) * LANES]

    squares = []

    def note(y):
        squares.append((y * y).astype(BF16))

    cos_a, sin_a = cos_a_ref[...], sin_a_ref[...]
    nf_a = HEAD_DIM // 4
    zq = seg(OFF_QA, N_HEADS_A * LANES)
    for hh in range(N_HEADS_A):
        y = rope(_rms(group(zq, hh), HEAD_DIM) * qg_a_ref[...], cos_a, sin_a, nf_a)
        qa_ref[0, hh] = (y * (HEAD_DIM ** -0.5 * LOG2E)).astype(BF16)
    zk = seg(OFF_KA, N_KV_A * LANES)
    zv = seg(OFF_VA, N_KV_A * LANES)
    for g in range(N_KV_A):
        y = rope(_rms(group(zk, g), HEAD_DIM) * kg_a_ref[...], cos_a, sin_a, nf_a)
        kta_ref[0, g] = y.T.astype(BF16)
        va_ref[0, g] = (group(zv, g) + ones_col).astype(BF16)

    cos_b, sin_b = cos_b_ref[...], sin_b_ref[...]
    nf_b = MLA_ROPE // 4
    cq = (_rms(seg(OFF_CQ, CQ_PAD), MLA_Q_RANK) * qg_b_ref[...]).astype(BF16)
    qb = _dot(cq, wuq_ref[...])
    scale_b = (MLA_NOPE + MLA_ROPE) ** -0.5 * LOG2E
    for hh in range(N_HEADS_B):
        y = rope(group(qb, hh), cos_b, sin_b, nf_b) * scale_b
        qb_ref[0, hh] = y.astype(BF16)
        note(y)
    ckv = (_rms(seg(OFF_CKV, MLA_KV_RANK), MLA_KV_RANK) * kvg_b_ref[...]).astype(BF16)
    kvb = _dot(ckv, wukv_ref[...])
    kr = rope(seg(OFF_KR, LANES), cos_b, sin_b, nf_b)
    for hh in range(N_HEADS_B):
        y = group(kvb, hh) + kr
        ktb_ref[0, hh] = y.T.astype(BF16)
        note(y)
        vb_ref[0, hh] = (group(kvb, N_HEADS_B + hh) + ones_col).astype(BF16)

    zq = seg(OFF_QC, N_HEADS_C * LANES)
    zk = seg(OFF_KC, N_HEADS_C * LANES)
    zv = seg(OFF_VC, N_HEADS_C * LANES)
    row = lax.broadcasted_iota(jnp.int32, (LANES, 1), 0)
    k_squares = []
    for hh in range(N_HEADS_C):
        y = group(zq, hh) * (DIFF_QK ** -0.5 * LOG2E)
        qc_ref[0, hh] = y.astype(BF16)
        note(y)
        k_squares.append(group(zk, hh))
        kt = group(zk, hh).T
        ktc_ref[0, 2 * hh, 0] = jnp.where(row < DIFF_QK, kt, 0.0).astype(BF16)
        ktc_ref[0, 2 * hh + 1, 0] = jnp.where(row >= DIFF_QK, kt, 0.0).astype(BF16)
        vc_ref[0, hh] = (group(zv, hh) + ones_col).astype(BF16)

    for kb in k_squares:
        note(kb)
    n2 = _dot(jnp.concatenate(squares, axis=-1), sel_ref[...])
    nrm_ref[0] = jnp.max(n2, axis=0, keepdims=True)


def _norm_routing():
    sel = np.zeros((NRM_GROUPS, LANES, LANES), np.float32)
    whole = ([NRM_QB + h for h in range(N_HEADS_B)] + [NRM_KB + h for h in range(N_HEADS_B)]
             + [NRM_QC + h for h in range(N_HEADS_C)])
    for g, slot in enumerate(whole):
        sel[g, :, slot] = 1.0
    for h in range(N_HEADS_C):
        sel[len(whole) + h, :DIFF_QK, NRM_KC + 2 * h] = 1.0
        sel[len(whole) + h, DIFF_QK:2 * DIFF_QK, NRM_KC + 2 * h + 1] = 1.0
    return jnp.asarray(sel.reshape(NRM_GROUPS * LANES, LANES), BF16)


def _in_proj(x2d, gat, w_in_p, tabs, qg_a, kg_a, qg_b, kvg_b, wuq_p, wukv_p, B, S):
    T = B * S
    tm = TM_IN
    nst = S // tm
    const = lambda shape: pl.BlockSpec(shape, lambda i: (0,) * len(shape))
    tab = pl.BlockSpec((tm, LANES), lambda i: (i % nst, 0))
    hm = lambda H: pl.BlockSpec((1, H, tm, LANES), lambda i: (i // nst, 0, i % nst, 0))
    hmt = lambda H: pl.BlockSpec((1, H, LANES, tm), lambda i: (i // nst, 0, 0, i % nst))
    sds = lambda H: jax.ShapeDtypeStruct((B, H, S, LANES), BF16)
    sdt = lambda H: jax.ShapeDtypeStruct((B, H, LANES, S), BF16)
    return pl.pallas_call(
        _in_kernel,
        grid=(T // tm,),
        in_specs=[pl.BlockSpec((tm, D_MODEL), lambda i: (i, 0)),
                  const((1, D_MODEL)), const((D_MODEL, IN_PAD)),
                  tab, tab, tab, tab,
                  const((1, LANES)), const((1, LANES)), const((1, CQ_PAD)), const((1, MLA_KV_RANK)),
                  const((CQ_PAD, N_HEADS_B * LANES)), const((MLA_KV_RANK, 2 * N_HEADS_B * LANES)),
                  const((NRM_GROUPS * LANES, LANES))],
        out_specs=[hm(N_HEADS_A), hmt(N_KV_A), hm(N_KV_A),
                   hm(N_HEADS_B), hmt(N_HEADS_B), hm(N_HEADS_B),
                   hm(N_HEADS_C),
                   pl.BlockSpec((1, 2 * N_HEADS_C, 1, LANES, tm),
                                lambda i: (i // nst, 0, i % nst, 0, 0)),
                   hm(N_HEADS_C),
                   pl.BlockSpec((1, 1, LANES), lambda i: (i, 0, 0))],
        out_shape=[sds(N_HEADS_A), sdt(N_KV_A), sds(N_KV_A),
                   sds(N_HEADS_B), sdt(N_HEADS_B), sds(N_HEADS_B),
                   sds(N_HEADS_C),
                   jax.ShapeDtypeStruct((B, 2 * N_HEADS_C, S // tm, LANES, tm), BF16),
                   sds(N_HEADS_C),
                   jax.ShapeDtypeStruct((T // tm, 1, LANES), F32)],
        compiler_params=pltpu.CompilerParams(dimension_semantics=("arbitrary",),
                                             vmem_limit_bytes=VMEM_LIMIT_BYTES),
        name="in_proj",
    )(x2d, gat, w_in_p, *tabs, qg_a, kg_a, qg_b, kvg_b, wuq_p, wukv_p, _norm_routing())


def _lane_group_max(s):
    m = s[:, :LANES]
    for j in range(1, s.shape[1] // LANES):
        m = jnp.maximum(m, s[:, j * LANES:(j + 1) * LANES])
    return m


def _softmax_pv(q, n_chunks, scores, values, s_ref, row_off, bounded):
    tq = q.shape[0]
    subs = [(r, r + TQ_SUB) for r in range(0, tq, TQ_SUB)]
    accs = []
    if bounded:
        items = [(r, si) for r in range(n_chunks) for si in range(len(subs))]
        accs = [None] * len(subs)
        in_flight = {}
        for k in range(len(items) + SCORE_LOOKAHEAD):
            if k < len(items):
                r, si = items[k]
                s = scores(q[subs[si][0]:subs[si][1]], r, si)
                in_flight[k] = s if row_off[r][si] is None else s + row_off[r][si]
            if k >= SCORE_LOOKAHEAD:
                r, si = items[k - SCORE_LOOKAHEAD]
                d = _dot(jnp.exp2(in_flight.pop(k - SCORE_LOOKAHEAD)).astype(BF16), values(r))
                accs[si] = d if accs[si] is None else accs[si] + d
        return jnp.concatenate(accs, axis=0)

    maxes = []
    for si, (r0, r1) in enumerate(subs):
        qs = q[r0:r1]
        m_run = None
        for r in range(n_chunks):
            s = scores(qs, r, si)
            s_ref[r0:r1, r * TK:(r + 1) * TK] = s
            m = _lane_group_max(s)
            if row_off[r][si] is not None:
                m = m + row_off[r][si]
            m_run = m if m_run is None else jnp.maximum(m_run, m)
        maxes.append(jnp.max(m_run, axis=-1, keepdims=True))
    for si, (r0, r1) in enumerate(subs):
        acc = None
        for r in range(n_chunks):
            off = row_off[r][si]
            shift = maxes[si] if off is None else maxes[si] - off
            p = jnp.exp2(s_ref[r0:r1, r * TK:(r + 1) * TK] - shift).astype(BF16)
            d = _dot(p, values(r))
            acc = d if acc is None else acc + d
        accs.append(acc)
    return jnp.concatenate(accs, axis=0)


def _attn_ab_kernel(bounded_ref, q_ref, kt_ref, v_ref, o_ref, s_ref, acc_ref, *,
                    n_heads, group, n_keys):
    n_chunks = n_keys // TK
    no_off = [[None] * (TQ // TQ_SUB)] * n_chunks

    def head(h, carry):
        kv = h // group
        scores = lambda qs, r, si: _dot(qs, kt_ref[0, kv, :, r * TK:(r + 1) * TK])
        values = lambda r: v_ref[0, kv, r * TK:(r + 1) * TK, :]
        bounded = bounded_ref[pl.program_id(0), h] != 0

        @pl.when(bounded)
        def _():
            acc_ref[h] = _softmax_pv(q_ref[0, h], n_chunks, scores, values, s_ref, no_off, True)

        @pl.when(jnp.logical_not(bounded))
        def _():
            acc_ref[h] = _softmax_pv(q_ref[0, h], n_chunks, scores, values, s_ref, no_off, False)

        return carry

    lax.fori_loop(0, n_heads, head, 0)
    outs = []
    for h in range(n_heads):
        a = acc_ref[h]
        outs.append(a[:, :HEAD_DIM] * (1.0 / a[:, V_ONES_LANE:V_ONES_LANE + 1]))
    o_ref[...] = jnp.concatenate(outs, axis=-1).astype(BF16)


def _attn_c_kernel(bounded_ref, q_ref, kt_ref, v_ref, lq1_ref, lk1_ref, lq2_ref, lk2_ref,
                   subln_ref, o_ref, s_ref, acc_ref, *, n_keys, lam_init):
    assert TQ == TK
    nc = n_keys // TK
    cd = pl.program_id(1)
    col = lax.broadcasted_iota(jnp.int32, (1, TK), 1).astype(F32)
    row = lax.broadcasted_iota(jnp.int32, (TQ_SUB, 1), 0).astype(F32)
    subs = [(r, r + TQ_SUB) for r in range(0, TQ, TQ_SUB)]
    neg_dist = [-jnp.abs((lax.broadcasted_iota(jnp.int32, (TQ_SUB, TK), 0) + r0
                          - lax.broadcasted_iota(jnp.int32, (TQ_SUB, TK), 1)).astype(F32))
                for r0, _ in subs]

    def sub_head(idx, carry):
        hh = idx // 2
        slope = jnp.float32(LOG2E * 2.0 ** (-8.0 * N_HEADS_C / N_HEADS_C))
        for k in range(N_HEADS_C - 2, -1, -1):
            slope = jnp.where(hh == k, LOG2E * 2.0 ** (-8.0 * (k + 1) / N_HEADS_C), slope)
        q = q_ref[0, hh]

        chunk, key_term, row_off = [cd], [None], [[None] * len(subs)]
        for r in range(1, nc):
            c = (cd + r) % nc
            side = jnp.where(c < cd, slope, -slope)
            chunk.append(c)
            key_term.append(col * side)
            base = -slope * (jnp.abs(c - cd) * TK).astype(F32)
            row_off.append([base - side * (row + float(r0)) for r0, _ in subs])

        def scores(qs, r, si):
            s = _dot(qs, kt_ref[0, idx, chunk[r]])
            return s + (neg_dist[si] * slope if r == 0 else key_term[r])

        def values(r):
            return v_ref[0, hh, pl.ds(pl.multiple_of(chunk[r] * TK, TK), TK), :]

        bounded = bounded_ref[pl.program_id(0), idx] != 0

        @pl.when(bounded)
        def _():
            acc_ref[idx] = _softmax_pv(q, nc, scores, values, s_ref, row_off, True)

        @pl.when(jnp.logical_not(bounded))
        def _():
            acc_ref[idx] = _softmax_pv(q, nc, scores, values, s_ref, row_off, False)

        return carry

    lax.fori_loop(0, 2 * N_HEADS_C, sub_head, 0)

    lam = (jnp.exp(jnp.sum(lq1_ref[...] * lk1_ref[...], axis=-1, keepdims=True))
           - jnp.exp(jnp.sum(lq2_ref[...] * lk2_ref[...], axis=-1, keepdims=True)) + lam_init)
    outs = []
    for hh in range(N_HEADS_C):
        a0 = acc_ref[2 * hh]
        a1 = acc_ref[2 * hh + 1]
        o = (a0[:, :DIFF_V] * (1.0 / a0[:, V_ONES_LANE:V_ONES_LANE + 1])
             - lam * (a1[:, :DIFF_V] * (1.0 / a1[:, V_ONES_LANE:V_ONES_LANE + 1])))
        outs.append(_rms(o, DIFF_V) * subln_ref[...] * (1.0 - lam_init))
    o_ref[...] = jnp.concatenate(outs, axis=-1).astype(BF16)


def _attention(kernel_fn, bounded, q, kt, v, extras, out_width, B, S):
    nq = S // TQ
    hq, hv = q.shape[1], v.shape[1]
    extra_specs = [pl.BlockSpec(e.shape, lambda b, i, f, n=e.ndim: (0,) * n) for e in extras]
    return pl.pallas_call(
        kernel_fn,
        grid_spec=pltpu.PrefetchScalarGridSpec(
            num_scalar_prefetch=1,
            grid=(B, nq),
            in_specs=[pl.BlockSpec((1, hq, TQ, LANES), lambda b, i, f: (b, 0, i, 0)),
                      pl.BlockSpec((1,) + kt.shape[1:],
                                   lambda b, i, f, n=kt.ndim: (b,) + (0,) * (n - 1)),
                      pl.BlockSpec((1, hv, S, LANES), lambda b, i, f: (b, 0, 0, 0))] + extra_specs,
            out_specs=pl.BlockSpec((TQ, out_width), lambda b, i, f: (b * nq + i, 0)),
            scratch_shapes=[pltpu.VMEM((TQ, S), F32),
                            pltpu.VMEM((bounded.shape[1], TQ, LANES), F32)]),
        out_shape=jax.ShapeDtypeStruct((B * S, out_width), BF16),
        compiler_params=pltpu.CompilerParams(dimension_semantics=("arbitrary", "arbitrary"),
                                             vmem_limit_bytes=VMEM_LIMIT_BYTES),
        name=kernel_fn.func.__name__.strip("_"),
    )(bounded, q, kt, v, *extras)


def _out_kernel(x_ref, oa_ref, ob_ref, oc_ref, wo_ref, g_ref, x1_ref, h2_ref):
    na = N_HEADS_A * HEAD_DIM
    nb = na + N_HEADS_B * MLA_V
    y = (x_ref[...] + _dot(oa_ref[...], wo_ref[:na]) + _dot(ob_ref[...], wo_ref[na:nb])
         + _dot(oc_ref[...], wo_ref[nb:]))
    x1_ref[...] = y
    h2_ref[...] = (_rms(y, D_MODEL) * g_ref[...]).astype(BF16)


def _out_proj(x2d, oa, ob, oc, w_out, g_ffn):
    T = x2d.shape[0]
    tm = TM_OUT
    rows = lambda w: pl.BlockSpec((tm, w), lambda i: (i, 0))
    return pl.pallas_call(
        _out_kernel,
        grid=(T // tm,),
        in_specs=[rows(D_MODEL), rows(oa.shape[1]), rows(ob.shape[1]), rows(oc.shape[1]),
                  pl.BlockSpec(w_out.shape, lambda i: (0, 0)),
                  pl.BlockSpec((1, D_MODEL), lambda i: (0, 0))],
        out_specs=[rows(D_MODEL), rows(D_MODEL)],
        out_shape=[jax.ShapeDtypeStruct((T, D_MODEL), F32),
                   jax.ShapeDtypeStruct((T, D_MODEL), BF16)],
        compiler_params=pltpu.CompilerParams(dimension_semantics=("arbitrary",),
                                             vmem_limit_bytes=VMEM_LIMIT_BYTES),
        name="out_proj",
    )(x2d, oa, ob, oc, w_out, g_ffn)


def _ffn_kernel(hp_ref, h_ref, hn_ref, x1_ref, wg_ref, wv_ref, cw_ref, cb_ref, wd_ref, fg_ref,
                o_ref, acc_ref, *, tiles_per_seq, final_norm):
    i = pl.program_id(0)
    f = pl.program_id(1)
    tm = h_ref.shape[0]
    halo = hp_ref.shape[0]
    at_start = (i % tiles_per_seq) == 0
    at_end = (i % tiles_per_seq) == tiles_per_seq - 1
    hp = jnp.where(at_start, jnp.zeros_like(hp_ref[...]), hp_ref[...])
    hn = jnp.where(at_end, jnp.zeros_like(hn_ref[...]), hn_ref[...])
    hext = jnp.concatenate([hp, h_ref[...], hn], axis=0)

    def branch(w_ref, k):
        u = _dot(hext, w_ref[...])
        cw = cw_ref[k]
        return (u[halo - 1:halo - 1 + tm] * cw[0:1] + u[halo:halo + tm] * cw[1:2]
                + u[halo + 1:halo + 1 + tm] * cw[2:3] + cb_ref[k])

    g = branch(wg_ref, 0)
    val = branch(wv_ref, 1)
    act = (g / (1.0 + jnp.exp(-g)) * val).astype(BF16)
    part = _dot(act, wd_ref[...])

    @pl.when(f == 0)
    def _():
        acc_ref[...] = x1_ref[...] + part

    @pl.when(f > 0)
    def _():
        acc_ref[...] += part

    @pl.when(f == pl.num_programs(1) - 1)
    def _():
        y = acc_ref[...]
        if final_norm:
            y = _rms(y, D_MODEL) * fg_ref[...]
        o_ref[...] = y


def _ffn(h2, x1, w_up, conv_w, conv_b, w_down, final_g, S, final_norm):
    T = h2.shape[0]
    tm, fc, halo = TM_FFN, FC_FFN, BF16_SUBLANES
    nfc = D_FF // fc
    per = tm // halo
    last = T // halo - 1
    return pl.pallas_call(
        functools.partial(_ffn_kernel, tiles_per_seq=S // tm, final_norm=final_norm),
        grid=(T // tm, nfc),
        in_specs=[pl.BlockSpec((halo, D_MODEL), lambda i, f: (jnp.maximum(i * per - 1, 0), 0)),
                  pl.BlockSpec((tm, D_MODEL), lambda i, f: (i, 0)),
                  pl.BlockSpec((halo, D_MODEL), lambda i, f: (jnp.minimum((i + 1) * per, last), 0)),
                  pl.BlockSpec((tm, D_MODEL), lambda i, f: (i, 0)),
                  pl.BlockSpec((D_MODEL, fc), lambda i, f: (0, f)),
                  pl.BlockSpec((D_MODEL, fc), lambda i, f: (0, nfc + f)),
                  pl.BlockSpec((2, 3, fc), lambda i, f: (0, 0, f)),
                  pl.BlockSpec((2, 1, fc), lambda i, f: (0, 0, f)),
                  pl.BlockSpec((fc, D_MODEL), lambda i, f: (f, 0)),
                  pl.BlockSpec((1, D_MODEL), lambda i, f: (0, 0))],
        out_specs=pl.BlockSpec((tm, D_MODEL), lambda i, f: (i, 0)),
        out_shape=jax.ShapeDtypeStruct((T, D_MODEL), F32),
        scratch_shapes=[pltpu.VMEM((tm, D_MODEL), F32)],
        compiler_params=pltpu.CompilerParams(dimension_semantics=("arbitrary", "arbitrary"),
                                             vmem_limit_bytes=VMEM_LIMIT_BYTES),
        name="conv_mlp",
    )(h2, h2, h2, x1, w_up, w_up, conv_w, conv_b, w_down, final_g)


def _pad_groups(w, n_groups, width):
    r = w.shape[0]
    w = w.reshape(r, n_groups, width)
    w = jnp.pad(w, ((0, 0), (0, 0), (0, LANES - width)))
    return w.reshape(r, n_groups * LANES)


def _rope_tables(S, dim, offset, identity_below):
    nf = dim // 4
    t = jnp.arange(S, dtype=jnp.int32)
    rows = (t // GRID_W).astype(F32)
    cols = (t % GRID_W).astype(F32)
    inv = ROPE_BASE ** (-jnp.arange(nf, dtype=F32) / nf)
    ar = rows[:, None] * inv
    ac = cols[:, None] * inv
    cos = jnp.concatenate([jnp.cos(ar), jnp.cos(ar), jnp.cos(ac), jnp.cos(ac)], axis=-1)
    sin = jnp.concatenate([-jnp.sin(ar), jnp.sin(ar), -jnp.sin(ac), jnp.sin(ac)], axis=-1)
    below = jnp.ones if identity_below else jnp.zeros
    cos = jnp.concatenate([below((S, offset), F32), cos,
                           jnp.zeros((S, LANES - offset - dim), F32)], axis=-1)
    sin = jnp.concatenate([jnp.zeros((S, offset), F32), sin,
                           jnp.zeros((S, LANES - offset - dim), F32)], axis=-1)
    return cos, sin


def _prep_w_in(w):
    d = w.shape[0]
    na, nkv = N_HEADS_A * HEAD_DIM, N_KV_A * HEAD_DIM
    nc_qk, nc_v = N_HEADS_C * 2 * DIFF_QK, N_HEADS_C * DIFF_V
    sizes = [na, nkv, nkv, MLA_Q_RANK, MLA_KV_RANK, MLA_ROPE, nc_qk, nc_qk, nc_v]
    offs = [0]
    for s in sizes:
        offs.append(offs[-1] + s)
    qa, ka, va, cq, ckv, kr, qc, kc, vc = [w[:, offs[k]:offs[k + 1]] for k in range(len(sizes))]
    parts = [
        _pad_groups(qa, N_HEADS_A, HEAD_DIM),
        _pad_groups(ka, N_KV_A, HEAD_DIM),
        _pad_groups(va, N_KV_A, HEAD_DIM),
        jnp.pad(cq, ((0, 0), (0, CQ_PAD - MLA_Q_RANK))),
        ckv,
        jnp.pad(kr, ((0, 0), (MLA_NOPE, LANES - MLA_NOPE - MLA_ROPE))),
        _pad_groups(qc, N_HEADS_C, 2 * DIFF_QK),
        _pad_groups(kc, N_HEADS_C, 2 * DIFF_QK),
        _pad_groups(vc, N_HEADS_C, DIFF_V),
    ]
    out = jnp.concatenate(parts, axis=-1).astype(BF16)
    assert out.shape == (d, IN_PAD)
    return out


def _pad_lanes(v, width):
    return jnp.pad(v, (0, width - v.shape[0])).reshape(1, width)


def kernel(x, norm_attn, w_in, q_norm_a, k_norm_a, q_a_norm_b, w_uq_b, kv_a_norm_b, w_ukv_b,
           lambda_q1_c, lambda_k1_c, lambda_q2_c, lambda_k2_c, subln_c, w_out,
           norm_ffn, w_up, conv_w, conv_b, w_down, final_norm):
    B, S, D = x.shape
    depth = w_in.shape[0]
    assert D == D_MODEL and S % max(TM_IN, TM_OUT, TM_FFN, TQ, TK) == 0 and S % GRID_W == 0
    T = B * S

    tabs = _rope_tables(S, HEAD_DIM, 0, False) + _rope_tables(S, MLA_ROPE, MLA_NOPE, True)
    row = lambda v: v.reshape(1, -1)
    xc = x.reshape(T, D)
    for l in range(depth):
        lam_init = 0.8 - 0.6 * math.exp(-0.3 * l)
        wuq_p = jnp.pad(_pad_groups(w_uq_b[l], N_HEADS_B, MLA_NOPE + MLA_ROPE),
                        ((0, CQ_PAD - MLA_Q_RANK), (0, 0))).astype(BF16)
        wukv = w_ukv_b[l].reshape(MLA_KV_RANK, N_HEADS_B, MLA_NOPE + MLA_V)
        wukv_p = jnp.concatenate(
            [_pad_groups(wukv[:, :, :MLA_NOPE].reshape(MLA_KV_RANK, -1), N_HEADS_B, MLA_NOPE),
             _pad_groups(wukv[:, :, MLA_NOPE:].reshape(MLA_KV_RANK, -1), N_HEADS_B, MLA_V)],
            axis=-1).astype(BF16)

        qa, kta, va, qb, ktb, vb, qc, ktc, vc, nrm = _in_proj(
            xc, row(norm_attn[l]), _prep_w_in(w_in[l]), tabs,
            _pad_lanes(q_norm_a[l], LANES), _pad_lanes(k_norm_a[l], LANES),
            _pad_lanes(q_a_norm_b[l], CQ_PAD), row(kv_a_norm_b[l]), wuq_p, wukv_p, B, S)

        nrm = jnp.max(nrm.reshape(B, -1, LANES), axis=1)

        def bounded(q0, nq, q_rep, k0, nk, k_rep):
            q2 = jnp.repeat(nrm[:, q0:q0 + nq], q_rep, axis=1)
            k2 = jnp.repeat(nrm[:, k0:k0 + nk], k_rep, axis=1)
            return (q2 * k2 * NRM_MARGIN <= SAFE_LOG2_RANGE ** 2).astype(jnp.int32)

        qk_a = (HEAD_DIM * jnp.max(jnp.abs(q_norm_a[l])) * jnp.max(jnp.abs(k_norm_a[l]))
                * (HEAD_DIM ** -0.5 * LOG2E))
        bounded_a = jnp.broadcast_to(
            (qk_a * NRM_MARGIN <= SAFE_LOG2_RANGE).astype(jnp.int32), (B, N_HEADS_A))

        oa = _attention(
            functools.partial(_attn_ab_kernel, n_heads=N_HEADS_A, group=N_HEADS_A // N_KV_A,
                              n_keys=S),
            bounded_a,
            qa, kta, va, [], N_HEADS_A * HEAD_DIM, B, S)
        ob = _attention(
            functools.partial(_attn_ab_kernel, n_heads=N_HEADS_B, group=1, n_keys=S),
            bounded(NRM_QB, N_HEADS_B, 1, NRM_KB, N_HEADS_B, 1),
            qb, ktb, vb, [], N_HEADS_B * MLA_V, B, S)
        oc = _attention(
            functools.partial(_attn_c_kernel, n_keys=S, lam_init=lam_init),
            bounded(NRM_QC, N_HEADS_C, 2, NRM_KC, 2 * N_HEADS_C, 1),
            qc, ktc, vc,
            [row(lambda_q1_c[l]), row(lambda_k1_c[l]), row(lambda_q2_c[l]), row(lambda_k2_c[l]),
             row(subln_c[l])],
            N_HEADS_C * DIFF_V, B, S)

        x1, h2 = _out_proj(xc, oa, ob, oc, w_out[l].astype(BF16), row(norm_ffn[l]))
        xc = _ffn(h2, x1, w_up[l].astype(BF16),
                  conv_w[l].reshape(3, 2, D_FF).transpose(1, 0, 2),
                  conv_b[l].reshape(2, 1, D_FF), w_down[l].astype(BF16),
                  row(final_norm), S, final_norm=(l == depth - 1))
    return xc.reshape(B, S, D)
```

```python
import functools
import math

import jax
import jax.numpy as jnp
import numpy as np
from jax import lax
from jax.experimental import pallas as pl
from jax.experimental.pallas import tpu as pltpu

D_MODEL = 1024
GRID_W = 64
HEAD_DIM = 64
N_HEADS_A = 8
N_KV_A = 2
N_HEADS_B = 4
MLA_Q_RANK = 192
MLA_KV_RANK = 128
MLA_NOPE = 64
MLA_ROPE = 32
MLA_V = 64
N_HEADS_C = 4
DIFF_QK = 32
DIFF_V = 64
D_FF = 2816
ROPE_BASE = 10000.0
EPS = 1e-6

LANES = 128
BF16_SUBLANES = 16
VMEM_LIMIT_BYTES = 56 * 1024 * 1024

LOG2E = math.log2(math.e)
V_ONES_LANE = 64

OFF_QA = 0
OFF_KA = OFF_QA + N_HEADS_A * LANES
OFF_VA = OFF_KA + N_KV_A * LANES
OFF_CQ = OFF_VA + N_KV_A * LANES
CQ_PAD = 2 * LANES
OFF_CKV = OFF_CQ + CQ_PAD
OFF_KR = OFF_CKV + MLA_KV_RANK
OFF_QC = OFF_KR + LANES
OFF_KC = OFF_QC + N_HEADS_C * LANES
OFF_VC = OFF_KC + N_HEADS_C * LANES
IN_PAD = OFF_VC + N_HEADS_C * LANES

NRM_QB = 0
NRM_KB = NRM_QB + N_HEADS_B
NRM_QC = NRM_KB + N_HEADS_B
NRM_KC = NRM_QC + N_HEADS_C
NRM_GROUPS = NRM_KC + N_HEADS_C
NRM_MARGIN = 1.03
SAFE_LOG2_RANGE = 55.0

F32 = jnp.float32
BF16 = jnp.bfloat16

TM_IN = 512
TM_OUT = 512
TM_FFN = 512
FFN_CHUNK = 256
TQ = 512
TQ_SUB = 128
TK = 512
SCORE_LOOKAHEAD = 2


def _dot(a, b):
    return jnp.dot(a, b, preferred_element_type=F32)


def _rms(x, n):
    ms = jnp.sum(x * x, axis=-1, keepdims=True) * (1.0 / n)
    return x * lax.rsqrt(ms + EPS)


def _in_kernel(x_ref, gat_ref, w_ref, cos_a_ref, sin_a_ref, cos_b_ref, sin_b_ref,
               qg_a_ref, kg_a_ref, qg_b_ref, kvg_b_ref, wuq_ref, wukv_ref, sel_ref,
               qa_ref, kta_ref, va_ref, qb_ref, ktb_ref, vb_ref, qc_ref, ktc_ref, vc_ref, nrm_ref):
    x = x_ref[...]
    h = (_rms(x, D_MODEL) * gat_ref[...]).astype(BF16)

    lane = lax.broadcasted_iota(jnp.int32, (1, LANES), 1)
    ones_col = (lane == V_ONES_LANE).astype(F32)

    def seg(off, width):
        return _dot(h, w_ref[:, off:off + width])

    def swap(y, nf):
        up = pltpu.roll(y, LANES - nf, axis=1)
        down = pltpu.roll(y, nf, axis=1)
        return jnp.where((lane & nf) == 0, up, down)

    def rope(y, c, s, nf):
        return y * c + swap(y, nf) * s

    def group(z, g):
        return z[:, g * LANES:(g + 1) * LANES]

    squares = []

    def note(y):
        squares.append((y * y).astype(BF16))

    cos_a, sin_a = cos_a_ref[...], sin_a_ref[...]
    nf_a = HEAD_DIM // 4
    zq = seg(OFF_QA, N_HEADS_A * LANES)
    for hh in range(N_HEADS_A):
        y = rope(_rms(group(zq, hh), HEAD_DIM) * qg_a_ref[...], cos_a, sin_a, nf_a)
        qa_ref[0, hh] = (y * (HEAD_DIM ** -0.5 * LOG2E)).astype(BF16)
    zk = seg(OFF_KA, N_KV_A * LANES)
    zv = seg(OFF_VA, N_KV_A * LANES)
    for g in range(N_KV_A):
        y = rope(_rms(group(zk, g), HEAD_DIM) * kg_a_ref[...], cos_a, sin_a, nf_a)
        kta_ref[0, g] = y.T.astype(BF16)
        va_ref[0, g] = (group(zv, g) + ones_col).astype(BF16)

    cos_b, sin_b = cos_b_ref[...], sin_b_ref[...]
    nf_b = MLA_ROPE // 4
    cq = (_rms(seg(OFF_CQ, CQ_PAD), MLA_Q_RANK) * qg_b_ref[...]).astype(BF16)
    qb = _dot(cq, wuq_ref[...])
    scale_b = (MLA_NOPE + MLA_ROPE) ** -0.5 * LOG2E
    for hh in range(N_HEADS_B):
        y = rope(group(qb, hh), cos_b, sin_b, nf_b) * scale_b
        qb_ref[0, hh] = y.astype(BF16)
        note(y)
    ckv = (_rms(seg(OFF_CKV, MLA_KV_RANK), MLA_KV_RANK) * kvg_b_ref[...]).astype(BF16)
    kvb = _dot(ckv, wukv_ref[...])
    kr = rope(seg(OFF_KR, LANES), cos_b, sin_b, nf_b)
    for hh in range(N_HEADS_B):
        y = group(kvb, hh) + kr
        ktb_ref[0, hh] = y.T.astype(BF16)
        note(y)
        vb_ref[0, hh] = (group(kvb, N_HEADS_B + hh) + ones_col).astype(BF16)

    zq = seg(OFF_QC, N_HEADS_C * LANES)
    zk = seg(OFF_KC, N_HEADS_C * LANES)
    zv = seg(OFF_VC, N_HEADS_C * LANES)
    row = lax.broadcasted_iota(jnp.int32, (LANES, 1), 0)
    k_squares = []
    for hh in range(N_HEADS_C):
        y = group(zq, hh) * (DIFF_QK ** -0.5 * LOG2E)
        qc_ref[0, hh] = y.astype(BF16)
        note(y)
        k_squares.append(group(zk, hh))
        kt = group(zk, hh).T
        ktc_ref[0, 2 * hh, 0] = jnp.where(row < DIFF_QK, kt, 0.0).astype(BF16)
        ktc_ref[0, 2 * hh + 1, 0] = jnp.where(row >= DIFF_QK, kt, 0.0).astype(BF16)
        vc_ref[0, hh] = (group(zv, hh) + ones_col).astype(BF16)

    for kb in k_squares:
        note(kb)
    n2 = _dot(jnp.concatenate(squares, axis=-1), sel_ref[...])
    nrm_ref[0] = jnp.max(n2, axis=0, keepdims=True)


def _norm_routing():
    sel = np.zeros((NRM_GROUPS, LANES, LANES), np.float32)
    whole = ([NRM_QB + h for h in range(N_HEADS_B)] + [NRM_KB + h for h in range(N_HEADS_B)]
             + [NRM_QC + h for h in range(N_HEADS_C)])
    for g, slot in enumerate(whole):
        sel[g, :, slot] = 1.0
    for h in range(N_HEADS_C):
        sel[len(whole) + h, :DIFF_QK, NRM_KC + 2 * h] = 1.0
        sel[len(whole) + h, DIFF_QK:2 * DIFF_QK, NRM_KC + 2 * h + 1] = 1.0
    return jnp.asarray(sel.reshape(NRM_GROUPS * LANES, LANES), BF16)


def _in_proj(x2d, gat, w_in_p, tabs, qg_a, kg_a, qg_b, kvg_b, wuq_p, wukv_p, B, S):
    T = B * S
    tm = TM_IN
    nst = S // tm
    const = lambda shape: pl.BlockSpec(shape, lambda i: (0,) * len(shape))
    tab = pl.BlockSpec((tm, LANES), lambda i: (i % nst, 0))
    hm = lambda H: pl.BlockSpec((1, H, tm, LANES), lambda i: (i // nst, 0, i % nst, 0))
    hmt = lambda H: pl.BlockSpec((1, H, LANES, tm), lambda i: (i // nst, 0, 0, i % nst))
    sds = lambda H: jax.ShapeDtypeStruct((B, H, S, LANES), BF16)
    sdt = lambda H: jax.ShapeDtypeStruct((B, H, LANES, S), BF16)
    return pl.pallas_call(
        _in_kernel,
        grid=(T // tm,),
        in_specs=[pl.BlockSpec((tm, D_MODEL), lambda i: (i, 0)),
                  const((1, D_MODEL)), const((D_MODEL, IN_PAD)),
                  tab, tab, tab, tab,
                  const((1, LANES)), const((1, LANES)), const((1, CQ_PAD)), const((1, MLA_KV_RANK)),
                  const((CQ_PAD, N_HEADS_B * LANES)), const((MLA_KV_RANK, 2 * N_HEADS_B * LANES)),
                  const((NRM_GROUPS * LANES, LANES))],
        out_specs=[hm(N_HEADS_A), hmt(N_KV_A), hm(N_KV_A),
                   hm(N_HEADS_B), hmt(N_HEADS_B), hm(N_HEADS_B),
                   hm(N_HEADS_C),
                   pl.BlockSpec((1, 2 * N_HEADS_C, 1, LANES, tm),
                                lambda i: (i // nst, 0, i % nst, 0, 0)),
                   hm(N_HEADS_C),
                   pl.BlockSpec((1, 1, LANES), lambda i: (i, 0, 0))],
        out_shape=[sds(N_HEADS_A), sdt(N_KV_A), sds(N_KV_A),
                   sds(N_HEADS_B), sdt(N_HEADS_B), sds(N_HEADS_B),
                   sds(N_HEADS_C),
                   jax.ShapeDtypeStruct((B, 2 * N_HEADS_C, S // tm, LANES, tm), BF16),
                   sds(N_HEADS_C),
                   jax.ShapeDtypeStruct((T // tm, 1, LANES), F32)],
        compiler_params=pltpu.CompilerParams(dimension_semantics=("arbitrary",),
                                             vmem_limit_bytes=VMEM_LIMIT_BYTES),
        name="in_proj",
    )(x2d, gat, w_in_p, *tabs, qg_a, kg_a, qg_b, kvg_b, wuq_p, wukv_p, _norm_routing())


def _lane_group_max(s):
    m = s[:, :LANES]
    for j in range(1, s.shape[1] // LANES):
        m = jnp.maximum(m, s[:, j * LANES:(j + 1) * LANES])
    return m


def _softmax_pv(q, n_chunks, scores, values, s_ref, row_off, bounded):
    tq = q.shape[0]
    subs = [(r, r + TQ_SUB) for r in range(0, tq, TQ_SUB)]
    accs = []
    if bounded:
        items = [(r, si) for r in range(n_chunks) for si in range(len(subs))]
        accs = [None] * len(subs)
        in_flight = {}
        for k in range(len(items) + SCORE_LOOKAHEAD):
            if k < len(items):
                r, si = items[k]
                s = scores(q[subs[si][0]:subs[si][1]], r, si)
                in_flight[k] = s if row_off[r][si] is None else s + row_off[r][si]
            if k >= SCORE_LOOKAHEAD:
                r, si = items[k - SCORE_LOOKAHEAD]
                d = _dot(jnp.exp2(in_flight.pop(k - SCORE_LOOKAHEAD)).astype(BF16), values(r))
                accs[si] = d if accs[si] is None else accs[si] + d
        return jnp.concatenate(accs, axis=0)

    maxes = []
    for si, (r0, r1) in enumerate(subs):
        qs = q[r0:r1]
        m_run = None
        for r in range(n_chunks):
            s = scores(qs, r, si)
            s_ref[r0:r1, r * TK:(r + 1) * TK] = s
            m = _lane_group_max(s)
            if row_off[r][si] is not None:
                m = m + row_off[r][si]
            m_run = m if m_run is None else jnp.maximum(m_run, m)
        maxes.append(jnp.max(m_run, axis=-1, keepdims=True))
    for si, (r0, r1) in enumerate(subs):
        acc = None
        for r in range(n_chunks):
            off = row_off[r][si]
            shift = maxes[si] if off is None else maxes[si] - off
            p = jnp.exp2(s_ref[r0:r1, r * TK:(r + 1) * TK] - shift).astype(BF16)
            d = _dot(p, values(r))
            acc = d if acc is None else acc + d
        accs.append(acc)
    return jnp.concatenate(accs, axis=0)


def _attn_ab_kernel(bounded_ref, q_ref, kt_ref, v_ref, o_ref, s_ref, acc_ref, *,
                    n_heads, group, n_keys):
    n_chunks = n_keys // TK
    no_off = [[None] * (TQ // TQ_SUB)] * n_chunks

    def head(h, carry):
        kv = h // group
        scores = lambda qs, r, si: _dot(qs, kt_ref[0, kv, :, r * TK:(r + 1) * TK])
        values = lambda r: v_ref[0, kv, r * TK:(r + 1) * TK, :]
        bounded = bounded_ref[pl.program_id(0), h] != 0

        @pl.when(bounded)
        def _():
            acc_ref[h] = _softmax_pv(q_ref[0, h], n_chunks, scores, values, s_ref, no_off, True)

        @pl.when(jnp.logical_not(bounded))
        def _():
            acc_ref[h] = _softmax_pv(q_ref[0, h], n_chunks, scores, values, s_ref, no_off, False)

        return carry

    lax.fori_loop(0, n_heads, head, 0)
    outs = []
    for h in range(n_heads):
        a = acc_ref[h]
        outs.append(a[:, :HEAD_DIM] * (1.0 / a[:, V_ONES_LANE:V_ONES_LANE + 1]))
    o_ref[...] = jnp.concatenate(outs, axis=-1).astype(BF16)


def _attn_c_kernel(bounded_ref, q_ref, kt_ref, v_ref, lq1_ref, lk1_ref, lq2_ref, lk2_ref,
                   subln_ref, o_ref, s_ref, acc_ref, *, n_keys, lam_init):
    assert TQ == TK
    nc = n_keys // TK
    cd = pl.program_id(1)
    col = lax.broadcasted_iota(jnp.int32, (1, TK), 1).astype(F32)
    row = lax.broadcasted_iota(jnp.int32, (TQ_SUB, 1), 0).astype(F32)
    subs = [(r, r + TQ_SUB) for r in range(0, TQ, TQ_SUB)]
    neg_dist = [-jnp.abs((lax.broadcasted_iota(jnp.int32, (TQ_SUB, TK), 0) + r0
                          - lax.broadcasted_iota(jnp.int32, (TQ_SUB, TK), 1)).astype(F32))
                for r0, _ in subs]

    def sub_head(idx, carry):
        hh = idx // 2
        slope = jnp.float32(LOG2E * 2.0 ** (-8.0 * N_HEADS_C / N_HEADS_C))
        for k in range(N_HEADS_C - 2, -1, -1):
            slope = jnp.where(hh == k, LOG2E * 2.0 ** (-8.0 * (k + 1) / N_HEADS_C), slope)
        q = q_ref[0, hh]

        chunk, key_term, row_off = [cd], [None], [[None] * len(subs)]
        for r in range(1, nc):
            c = (cd + r) % nc
            side = jnp.where(c < cd, slope, -slope)
            chunk.append(c)
            key_term.append(col * side)
            base = -slope * (jnp.abs(c - cd) * TK).astype(F32)
            row_off.append([base - side * (row + float(r0)) for r0, _ in subs])

        def scores(qs, r, si):
            s = _dot(qs, kt_ref[0, idx, chunk[r]])
            return s + (neg_dist[si] * slope if r == 0 else key_term[r])

        def values(r):
            return v_ref[0, hh, pl.ds(pl.multiple_of(chunk[r] * TK, TK), TK), :]

        bounded = bounded_ref[pl.program_id(0), idx] != 0

        @pl.when(bounded)
        def _():
            acc_ref[idx] = _softmax_pv(q, nc, scores, values, s_ref, row_off, True)

        @pl.when(jnp.logical_not(bounded))
        def _():
            acc_ref[idx] = _softmax_pv(q, nc, scores, values, s_ref, row_off, False)

        return carry

    lax.fori_loop(0, 2 * N_HEADS_C, sub_head, 0)

    lam = (jnp.exp(jnp.sum(lq1_ref[...] * lk1_ref[...], axis=-1, keepdims=True))
           - jnp.exp(jnp.sum(lq2_ref[...] * lk2_ref[...], axis=-1, keepdims=True)) + lam_init)
    outs = []
    for hh in range(N_HEADS_C):
        a0 = acc_ref[2 * hh]
        a1 = acc_ref[2 * hh + 1]
        o = (a0[:, :DIFF_V] * (1.0 / a0[:, V_ONES_LANE:V_ONES_LANE + 1])
             - lam * (a1[:, :DIFF_V] * (1.0 / a1[:, V_ONES_LANE:V_ONES_LANE + 1])))
        outs.append(_rms(o, DIFF_V) * subln_ref[...] * (1.0 - lam_init))
    o_ref[...] = jnp.concatenate(outs, axis=-1).astype(BF16)


def _attention(kernel_fn, bounded, q, kt, v, extras, out_width, B, S):
    nq = S // TQ
    hq, hv = q.shape[1], v.shape[1]
    extra_specs = [pl.BlockSpec(e.shape, lambda b, i, f, n=e.ndim: (0,) * n) for e in extras]
    return pl.pallas_call(
        kernel_fn,
        grid_spec=pltpu.PrefetchScalarGridSpec(
            num_scalar_prefetch=1,
            grid=(B, nq),
            in_specs=[pl.BlockSpec((1, hq, TQ, LANES), lambda b, i, f: (b, 0, i, 0)),
                      pl.BlockSpec((1,) + kt.shape[1:],
                                   lambda b, i, f, n=kt.ndim: (b,) + (0,) * (n - 1)),
                      pl.BlockSpec((1, hv, S, LANES), lambda b, i, f: (b, 0, 0, 0))] + extra_specs,
            out_specs=pl.BlockSpec((TQ, out_width), lambda b, i, f: (b * nq + i, 0)),
            scratch_shapes=[pltpu.VMEM((TQ, S), F32),
                            pltpu.VMEM((bounded.shape[1], TQ, LANES), F32)]),
        out_shape=jax.ShapeDtypeStruct((B * S, out_width), BF16),
        compiler_params=pltpu.CompilerParams(dimension_semantics=("arbitrary", "arbitrary"),
                                             vmem_limit_bytes=VMEM_LIMIT_BYTES),
        name=kernel_fn.func.__name__.strip("_"),
    )(bounded, q, kt, v, *extras)


def _out_kernel(x_ref, oa_ref, ob_ref, oc_ref, wo_ref, g_ref, x1_ref, h2_ref):
    na = N_HEADS_A * HEAD_DIM
    nb = na + N_HEADS_B * MLA_V
    y = (x_ref[...] + _dot(oa_ref[...], wo_ref[:na]) + _dot(ob_ref[...], wo_ref[na:nb])
         + _dot(oc_ref[...], wo_ref[nb:]))
    x1_ref[...] = y
    h2_ref[...] = (_rms(y, D_MODEL) * g_ref[...]).astype(BF16)


def _out_proj(x2d, oa, ob, oc, w_out, g_ffn):
    T = x2d.shape[0]
    tm = TM_OUT
    rows = lambda w: pl.BlockSpec((tm, w), lambda i: (i, 0))
    return pl.pallas_call(
        _out_kernel,
        grid=(T // tm,),
        in_specs=[rows(D_MODEL), rows(oa.shape[1]), rows(ob.shape[1]), rows(oc.shape[1]),
                  pl.BlockSpec(w_out.shape, lambda i: (0, 0)),
                  pl.BlockSpec((1, D_MODEL), lambda i: (0, 0))],
        out_specs=[rows(D_MODEL), rows(D_MODEL)],
        out_shape=[jax.ShapeDtypeStruct((T, D_MODEL), F32),
                   jax.ShapeDtypeStruct((T, D_MODEL), BF16)],
        compiler_params=pltpu.CompilerParams(dimension_semantics=("arbitrary",),
                                             vmem_limit_bytes=VMEM_LIMIT_BYTES),
        name="out_proj",
    )(x2d, oa, ob, oc, w_out, g_ffn)


def _ffn_kernel(hp_ref, h_ref, hn_ref, x1_ref, wu_ref, cw_ref, cb_ref, wd_ref, fg_ref,
                o_ref, act_ref, *, tiles_per_seq, final_norm):
    i = pl.program_id(0)
    tm = h_ref.shape[0]
    halo = hp_ref.shape[0]
    at_start = (i % tiles_per_seq) == 0
    at_end = (i % tiles_per_seq) == tiles_per_seq - 1
    hp = jnp.where(at_start, jnp.zeros_like(hp_ref[...]), hp_ref[...])
    hn = jnp.where(at_end, jnp.zeros_like(hn_ref[...]), hn_ref[...])
    hext = jnp.concatenate([hp, h_ref[...], hn], axis=0)

    def branch(k, c0):
        u = _dot(hext, wu_ref[:, k * D_FF + c0:k * D_FF + c0 + FFN_CHUNK])
        cw = cw_ref[k, :, c0:c0 + FFN_CHUNK]
        return (u[halo - 1:halo - 1 + tm] * cw[0:1] + u[halo:halo + tm] * cw[1:2]
                + u[halo + 1:halo + 1 + tm] * cw[2:3] + cb_ref[k, :, c0:c0 + FFN_CHUNK])

    for c0 in range(0, D_FF, FFN_CHUNK):
        g = branch(0, c0)
        val = branch(1, c0)
        act_ref[:, c0:c0 + FFN_CHUNK] = (g / (1.0 + jnp.exp(-g)) * val).astype(BF16)

    y = x1_ref[...] + _dot(act_ref[...], wd_ref[...])
    if final_norm:
        y = _rms(y, D_MODEL) * fg_ref[...]
    o_ref[...] = y


def _ffn(h2, x1, w_up, conv_w, conv_b, w_down, final_g, S, final_norm):
    T = h2.shape[0]
    tm, halo = TM_FFN, BF16_SUBLANES
    per = tm // halo
    last = T // halo - 1
    resident = lambda shape: pl.BlockSpec(shape, lambda i: (0,) * len(shape),
                                          pipeline_mode=pl.Buffered(1))
    return pl.pallas_call(
        functools.partial(_ffn_kernel, tiles_per_seq=S // tm, final_norm=final_norm),
        grid=(T // tm,),
        in_specs=[pl.BlockSpec((halo, D_MODEL), lambda i: (jnp.maximum(i * per - 1, 0), 0)),
                  pl.BlockSpec((tm, D_MODEL), lambda i: (i, 0)),
                  pl.BlockSpec((halo, D_MODEL), lambda i: (jnp.minimum((i + 1) * per, last), 0)),
                  pl.BlockSpec((tm, D_MODEL), lambda i: (i, 0)),
                  resident((D_MODEL, 2 * D_FF)), resident((2, 3, D_FF)), resident((2, 1, D_FF)),
                  resident((D_FF, D_MODEL)), resident((1, D_MODEL))],
        out_specs=pl.BlockSpec((tm, D_MODEL), lambda i: (i, 0)),
        out_shape=jax.ShapeDtypeStruct((T, D_MODEL), F32),
        scratch_shapes=[pltpu.VMEM((tm, D_FF), BF16)],
        compiler_params=pltpu.CompilerParams(dimension_semantics=("arbitrary",),
                                             vmem_limit_bytes=VMEM_LIMIT_BYTES),
        name="conv_mlp",
    )(h2, h2, h2, x1, w_up, conv_w, conv_b, w_down, final_g)


def _pad_groups(w, n_groups, width):
    r = w.shape[0]
    w = w.reshape(r, n_groups, width)
    w = jnp.pad(w, ((0, 0), (0, 0), (0, LANES - width)))
    return w.reshape(r, n_groups * LANES)


def _rope_tables(S, dim, offset, identity_below):
    nf = dim // 4
    t = jnp.arange(S, dtype=jnp.int32)
    rows = (t // GRID_W).astype(F32)
    cols = (t % GRID_W).astype(F32)
    inv = ROPE_BASE ** (-jnp.arange(nf, dtype=F32) / nf)
    ar = rows[:, None] * inv
    ac = cols[:, None] * inv
    cos = jnp.concatenate([jnp.cos(ar), jnp.cos(ar), jnp.cos(ac), jnp.cos(ac)], axis=-1)
    sin = jnp.concatenate([-jnp.sin(ar), jnp.sin(ar), -jnp.sin(ac), jnp.sin(ac)], axis=-1)
    below = jnp.ones if identity_below else jnp.zeros
    cos = jnp.concatenate([below((S, offset), F32), cos,
                           jnp.zeros((S, LANES - offset - dim), F32)], axis=-1)
    sin = jnp.concatenate([jnp.zeros((S, offset), F32), sin,
                           jnp.zeros((S, LANES - offset - dim), F32)], axis=-1)
    return cos, sin


def _prep_w_in(w):
    d = w.shape[0]
    na, nkv = N_HEADS_A * HEAD_DIM, N_KV_A * HEAD_DIM
    nc_qk, nc_v = N_HEADS_C * 2 * DIFF_QK, N_HEADS_C * DIFF_V
    sizes = [na, nkv, nkv, MLA_Q_RANK, MLA_KV_RANK, MLA_ROPE, nc_qk, nc_qk, nc_v]
    offs = [0]
    for s in sizes:
        offs.append(offs[-1] + s)
    qa, ka, va, cq, ckv, kr, qc, kc, vc = [w[:, offs[k]:offs[k + 1]] for k in range(len(sizes))]
    parts = [
        _pad_groups(qa, N_HEADS_A, HEAD_DIM),
        _pad_groups(ka, N_KV_A, HEAD_DIM),
        _pad_groups(va, N_KV_A, HEAD_DIM),
        jnp.pad(cq, ((0, 0), (0, CQ_PAD - MLA_Q_RANK))),
        ckv,
        jnp.pad(kr, ((0, 0), (MLA_NOPE, LANES - MLA_NOPE - MLA_ROPE))),
        _pad_groups(qc, N_HEADS_C, 2 * DIFF_QK),
        _pad_groups(kc, N_HEADS_C, 2 * DIFF_QK),
        _pad_groups(vc, N_HEADS_C, DIFF_V),
    ]
    out = jnp.concatenate(parts, axis=-1).astype(BF16)
    assert out.shape == (d, IN_PAD)
    return out


def _pad_lanes(v, width):
    return jnp.pad(v, (0, width - v.shape[0])).reshape(1, width)


def kernel(x, norm_attn, w_in, q_norm_a, k_norm_a, q_a_norm_b, w_uq_b, kv_a_norm_b, w_ukv_b,
           lambda_q1_c, lambda_k1_c, lambda_q2_c, lambda_k2_c, subln_c, w_out,
           norm_ffn, w_up, conv_w, conv_b, w_down, final_norm):
    B, S, D = x.shape
    depth = w_in.shape[0]
    assert D == D_MODEL and S % max(TM_IN, TM_OUT, TM_FFN, TQ, TK) == 0 and S % GRID_W == 0
    T = B * S

    tabs = _rope_tables(S, HEAD_DIM, 0, False) + _rope_tables(S, MLA_ROPE, MLA_NOPE, True)
    row = lambda v: v.reshape(1, -1)
    xc = x.reshape(T, D)
    for l in range(depth):
        lam_init = 0.8 - 0.6 * math.exp(-0.3 * l)
        wuq_p = jnp.pad(_pad_groups(w_uq_b[l], N_HEADS_B, MLA_NOPE + MLA_ROPE),
                        ((0, CQ_PAD - MLA_Q_RANK), (0, 0))).astype(BF16)
        wukv = w_ukv_b[l].reshape(MLA_KV_RANK, N_HEADS_B, MLA_NOPE + MLA_V)
        wukv_p = jnp.concatenate(
            [_pad_groups(wukv[:, :, :MLA_NOPE].reshape(MLA_KV_RANK, -1), N_HEADS_B, MLA_NOPE),
             _pad_groups(wukv[:, :, MLA_NOPE:].reshape(MLA_KV_RANK, -1), N_HEADS_B, MLA_V)],
            axis=-1).astype(BF16)

        qa, kta, va, qb, ktb, vb, qc, ktc, vc, nrm = _in_proj(
            xc, row(norm_attn[l]), _prep_w_in(w_in[l]), tabs,
            _pad_lanes(q_norm_a[l], LANES), _pad_lanes(k_norm_a[l], LANES),
            _pad_lanes(q_a_norm_b[l], CQ_PAD), row(kv_a_norm_b[l]), wuq_p, wukv_p, B, S)

        nrm = jnp.max(nrm.reshape(B, -1, LANES), axis=1)

        def bounded(q0, nq, q_rep, k0, nk, k_rep):
            q2 = jnp.repeat(nrm[:, q0:q0 + nq], q_rep, axis=1)
            k2 = jnp.repeat(nrm[:, k0:k0 + nk], k_rep, axis=1)
            return (q2 * k2 * NRM_MARGIN <= SAFE_LOG2_RANGE ** 2).astype(jnp.int32)

        qk_a = (HEAD_DIM * jnp.max(jnp.abs(q_norm_a[l])) * jnp.max(jnp.abs(k_norm_a[l]))
                * (HEAD_DIM ** -0.5 * LOG2E))
        bounded_a = jnp.broadcast_to(
            (qk_a * NRM_MARGIN <= SAFE_LOG2_RANGE).astype(jnp.int32), (B, N_HEADS_A))

        oa = _attention(
            functools.partial(_attn_ab_kernel, n_heads=N_HEADS_A, group=N_HEADS_A // N_KV_A,
                              n_keys=S),
            bounded_a,
            qa, kta, va, [], N_HEADS_A * HEAD_DIM, B, S)
        ob = _attention(
            functools.partial(_attn_ab_kernel, n_heads=N_HEADS_B, group=1, n_keys=S),
            bounded(NRM_QB, N_HEADS_B, 1, NRM_KB, N_HEADS_B, 1),
            qb, ktb, vb, [], N_HEADS_B * MLA_V, B, S)
        oc = _attention(
            functools.partial(_attn_c_kernel, n_keys=S, lam_init=lam_init),
            bounded(NRM_QC, N_HEADS_C, 2, NRM_KC, 2 * N_HEADS_C, 1),
            qc, ktc, vc,
            [row(lambda_q1_c[l]), row(lambda_k1_c[l]), row(lambda_q2_c[l]), row(lambda_k2_c[l]),
             row(subln_c[l])],
            N_HEADS_C * DIFF_V, B, S)

        x1, h2 = _out_proj(xc, oa, ob, oc, w_out[l].astype(BF16), row(norm_ffn[l]))
        xc = _ffn(h2, x1, w_up[l].astype(BF16),
                  conv_w[l].reshape(3, 2, D_FF).transpose(1, 0, 2),
                  conv_b[l].reshape(2, 1, D_FF), w_down[l].astype(BF16),
                  row(final_norm), S, final_norm=(l == depth - 1))
    return xc.reshape(B, S, D)
```

```python
import functools
import math

import jax
import jax.numpy as jnp
import numpy as np
from jax import lax
from jax.experimental import pallas as pl
from jax.experimental.pallas import tpu as pltpu

D_MODEL = 1024
GRID_W = 64
HEAD_DIM = 64
N_HEADS_A = 8
N_KV_A = 2
N_HEADS_B = 4
MLA_Q_RANK = 192
MLA_KV_RANK = 128
MLA_NOPE = 64
MLA_ROPE = 32
MLA_V = 64
N_HEADS_C = 4
DIFF_QK = 32
DIFF_V = 64
D_FF = 2816
ROPE_BASE = 10000.0
EPS = 1e-6

LANES = 128
BF16_SUBLANES = 16
VMEM_LIMIT_BYTES = 56 * 1024 * 1024

LOG2E = math.log2(math.e)
V_ONES_LANE = 64

OFF_QA = 0
OFF_KA = OFF_QA + N_HEADS_A * LANES
OFF_VA = OFF_KA + N_KV_A * LANES
OFF_CQ = OFF_VA + N_KV_A * LANES
CQ_PAD = 2 * LANES
OFF_CKV = OFF_CQ + CQ_PAD
OFF_KR = OFF_CKV + MLA_KV_RANK
OFF_QC = OFF_KR + LANES
OFF_KC = OFF_QC + N_HEADS_C * LANES
OFF_VC = OFF_KC + N_HEADS_C * LANES
IN_PAD = OFF_VC + N_HEADS_C * LANES

NRM_QB = 0
NRM_KB = NRM_QB + N_HEADS_B
NRM_QC = NRM_KB + N_HEADS_B
NRM_KC = NRM_QC + N_HEADS_C
NRM_GROUPS = NRM_KC + N_HEADS_C
NRM_MARGIN = 1.03
SAFE_LOG2_RANGE = 55.0

F32 = jnp.float32
BF16 = jnp.bfloat16

TM_IN = 512
TM_OUT = 512
TM_FFN = 512
FFN_CHUNK = 256
TQ = 512
TQ_SUB = 128
TK = 512
SCORE_LOOKAHEAD = 2


def _dot(a, b):
    return jnp.dot(a, b, preferred_element_type=F32)


def _rms(x, n):
    ms = jnp.sum(x * x, axis=-1, keepdims=True) * (1.0 / n)
    return x * lax.rsqrt(ms + EPS)


def _in_kernel(x_ref, gat_ref, w_ref, cos_a_ref, sin_a_ref, cos_b_ref, sin_b_ref,
               qg_a_ref, kg_a_ref, qg_b_ref, kvg_b_ref, wuq_ref, wukv_ref, sel_ref,
               qa_ref, kta_ref, va_ref, qb_ref, ktb_ref, vb_ref, qc_ref, ktc_ref, vc_ref, nrm_ref):
    x = x_ref[...]
    h = (_rms(x, D_MODEL) * gat_ref[...]).astype(BF16)

    lane = lax.broadcasted_iota(jnp.int32, (1, LANES), 1)
    ones_col = (lane == V_ONES_LANE).astype(F32)

    def seg(off, width):
        return _dot(h, w_ref[:, off:off + width])

    def swap(y, nf):
        up = pltpu.roll(y, LANES - nf, axis=1)
        down = pltpu.roll(y, nf, axis=1)
        return jnp.where((lane & nf) == 0, up, down)

    def rope(y, c, s, nf):
        return y * c + swap(y, nf) * s

    def group(z, g):
        return z[:, g * LANES:(g + 1) * LANES]

    squares = []

    def note(y):
        squares.append((y * y).astype(BF16))

    cos_a, sin_a = cos_a_ref[...], sin_a_ref[...]
    nf_a = HEAD_DIM // 4
    zq = seg(OFF_QA, N_HEADS_A * LANES)
    for hh in range(N_HEADS_A):
        y = rope(_rms(group(zq, hh), HEAD_DIM) * qg_a_ref[...], cos_a, sin_a, nf_a)
        qa_ref[0, hh] = (y * (HEAD_DIM ** -0.5 * LOG2E)).astype(BF16)
    zk = seg(OFF_KA, N_KV_A * LANES)
    zv = seg(OFF_VA, N_KV_A * LANES)
    for g in range(N_KV_A):
        y = rope(_rms(group(zk, g), HEAD_DIM) * kg_a_ref[...], cos_a, sin_a, nf_a)
        kta_ref[0, g] = y.T.astype(BF16)
        va_ref[0, g] = (group(zv, g) + ones_col).astype(BF16)

    cos_b, sin_b = cos_b_ref[...], sin_b_ref[...]
    nf_b = MLA_ROPE // 4
    cq = (_rms(seg(OFF_CQ, CQ_PAD), MLA_Q_RANK) * qg_b_ref[...]).astype(BF16)
    qb = _dot(cq, wuq_ref[...])
    scale_b = (MLA_NOPE + MLA_ROPE) ** -0.5 * LOG2E
    for hh in range(N_HEADS_B):
        y = rope(group(qb, hh), cos_b, sin_b, nf_b) * scale_b
        qb_ref[0, hh] = y.astype(BF16)
        note(y)
    ckv = (_rms(seg(OFF_CKV, MLA_KV_RANK), MLA_KV_RANK) * kvg_b_ref[...]).astype(BF16)
    kvb = _dot(ckv, wukv_ref[...])
    kr = rope(seg(OFF_KR, LANES), cos_b, sin_b, nf_b)
    for hh in range(N_HEADS_B):
        y = group(kvb, hh) + kr
        ktb_ref[0, hh] = y.T.astype(BF16)
        note(y)
        vb_ref[0, hh] = (group(kvb, N_HEADS_B + hh) + ones_col).astype(BF16)

    zq = seg(OFF_QC, N_HEADS_C * LANES)
    zk = seg(OFF_KC, N_HEADS_C * LANES)
    zv = seg(OFF_VC, N_HEADS_C * LANES)
    row = lax.broadcasted_iota(jnp.int32, (LANES, 1), 0)
    k_squares = []
    for hh in range(N_HEADS_C):
        y = group(zq, hh) * (DIFF_QK ** -0.5 * LOG2E)
        qc_ref[0, hh] = y.astype(BF16)
        note(y)
        k_squares.append(group(zk, hh))
        kt = group(zk, hh).T
        ktc_ref[0, 2 * hh, 0] = jnp.where(row < DIFF_QK, kt, 0.0).astype(BF16)
        ktc_ref[0, 2 * hh + 1, 0] = jnp.where(row >= DIFF_QK, kt, 0.0).astype(BF16)
        vc_ref[0, hh] = (group(zv, hh) + ones_col).astype(BF16)

    for kb in k_squares:
        note(kb)
    n2 = _dot(jnp.concatenate(squares, axis=-1), sel_ref[...])
    nrm_ref[0] = jnp.max(n2, axis=0, keepdims=True)


def _norm_routing():
    sel = np.zeros((NRM_GROUPS, LANES, LANES), np.float32)
    whole = ([NRM_QB + h for h in range(N_HEADS_B)] + [NRM_KB + h for h in range(N_HEADS_B)]
             + [NRM_QC + h for h in range(N_HEADS_C)])
    for g, slot in enumerate(whole):
        sel[g, :, slot] = 1.0
    for h in range(N_HEADS_C):
        sel[len(whole) + h, :DIFF_QK, NRM_KC + 2 * h] = 1.0
        sel[len(whole) + h, DIFF_QK:2 * DIFF_QK, NRM_KC + 2 * h + 1] = 1.0
    return jnp.asarray(sel.reshape(NRM_GROUPS * LANES, LANES), BF16)


def _in_proj(x2d, gat, w_in_p, tabs, qg_a, kg_a, qg_b, kvg_b, wuq_p, wukv_p, B, S):
    T = B * S
    tm = TM_IN
    nst = S // tm
    const = lambda shape: pl.BlockSpec(shape, lambda i: (0,) * len(shape))
    tab = pl.BlockSpec((tm, LANES), lambda i: (i % nst, 0))
    hm = lambda H: pl.BlockSpec((1, H, tm, LANES), lambda i: (i // nst, 0, i % nst, 0))
    hmt = lambda H: pl.BlockSpec((1, H, LANES, tm), lambda i: (i // nst, 0, 0, i % nst))
    sds = lambda H: jax.ShapeDtypeStruct((B, H, S, LANES), BF16)
    sdt = lambda H: jax.ShapeDtypeStruct((B, H, LANES, S), BF16)
    return pl.pallas_call(
        _in_kernel,
        grid=(T // tm,),
        in_specs=[pl.BlockSpec((tm, D_MODEL), lambda i: (i, 0)),
                  const((1, D_MODEL)), const((D_MODEL, IN_PAD)),
                  tab, tab, tab, tab,
                  const((1, LANES)), const((1, LANES)), const((1, CQ_PAD)), const((1, MLA_KV_RANK)),
                  const((CQ_PAD, N_HEADS_B * LANES)), const((MLA_KV_RANK, 2 * N_HEADS_B * LANES)),
                  const((NRM_GROUPS * LANES, LANES))],
        out_specs=[hm(N_HEADS_A), hmt(N_KV_A), hm(N_KV_A),
                   hm(N_HEADS_B), hmt(N_HEADS_B), hm(N_HEADS_B),
                   hm(N_HEADS_C),
                   pl.BlockSpec((1, 2 * N_HEADS_C, 1, LANES, tm),
                                lambda i: (i // nst, 0, i % nst, 0, 0)),
                   hm(N_HEADS_C),
                   pl.BlockSpec((1, 1, LANES), lambda i: (i, 0, 0))],
        out_shape=[sds(N_HEADS_A), sdt(N_KV_A), sds(N_KV_A),
                   sds(N_HEADS_B), sdt(N_HEADS_B), sds(N_HEADS_B),
                   sds(N_HEADS_C),
                   jax.ShapeDtypeStruct((B, 2 * N_HEADS_C, S // tm, LANES, tm), BF16),
                   sds(N_HEADS_C),
                   jax.ShapeDtypeStruct((T // tm, 1, LANES), F32)],
        compiler_params=pltpu.CompilerParams(dimension_semantics=("arbitrary",),
                                             vmem_limit_bytes=VMEM_LIMIT_BYTES),
        name="in_proj",
    )(x2d, gat, w_in_p, *tabs, qg_a, kg_a, qg_b, kvg_b, wuq_p, wukv_p, _norm_routing())


def _lane_group_max(s):
    m = s[:, :LANES]
    for j in range(1, s.shape[1] // LANES):
        m = jnp.maximum(m, s[:, j * LANES:(j + 1) * LANES])
    return m


def _pv_bounded(streams, n_chunks):
    items = [(r, i) for r in range(n_chunks) for i in range(len(streams))]
    accs = [None] * len(streams)
    in_flight = {}
    for k in range(len(items) + SCORE_LOOKAHEAD):
        if k < len(items):
            r, i = items[k]
            scores, _, row_off = streams[i]
            s, off = scores(r), row_off(r)
            in_flight[k] = s if off is None else s + off
        if k >= SCORE_LOOKAHEAD:
            r, i = items[k - SCORE_LOOKAHEAD]
            d = _dot(jnp.exp2(in_flight.pop(k - SCORE_LOOKAHEAD)).astype(BF16), streams[i][1](r))
            accs[i] = d if accs[i] is None else accs[i] + d
    return accs


def _pv_exact(streams, n_chunks, s_ref):
    maxes = []
    for i, (scores, _, row_off) in enumerate(streams):
        m_run = None
        for r in range(n_chunks):
            s = scores(r)
            s_ref[i * TQ_SUB:(i + 1) * TQ_SUB, r * TK:(r + 1) * TK] = s
            m, off = _lane_group_max(s), row_off(r)
            if off is not None:
                m = m + off
            m_run = m if m_run is None else jnp.maximum(m_run, m)
        maxes.append(jnp.max(m_run, axis=-1, keepdims=True))
    accs = []
    for i, (_, values, row_off) in enumerate(streams):
        acc = None
        for r in range(n_chunks):
            off = row_off(r)
            shift = maxes[i] if off is None else maxes[i] - off
            p = jnp.exp2(s_ref[i * TQ_SUB:(i + 1) * TQ_SUB, r * TK:(r + 1) * TK] - shift)
            d = _dot(p.astype(BF16), values(r))
            acc = d if acc is None else acc + d
        accs.append(acc)
    return accs


def _run_head_pair(bounded, heads, head_streams, n_chunks, s_ref, acc_ref):
    n_sub = TQ // TQ_SUB

    @pl.when(bounded)
    def _():
        accs = _pv_bounded([s for h in heads for s in head_streams(h)], n_chunks)
        for k, h in enumerate(heads):
            acc_ref[h] = jnp.concatenate(accs[k * n_sub:(k + 1) * n_sub], axis=0)

    @pl.when(jnp.logical_not(bounded))
    def _():
        for h in heads:
            acc_ref[h] = jnp.concatenate(_pv_exact(head_streams(h), n_chunks, s_ref), axis=0)


def _attn_ab_kernel(bounded_ref, q_ref, kt_ref, v_ref, o_ref, s_ref, acc_ref, *,
                    n_heads, group, n_keys):
    n_chunks = n_keys // TK

    def head_streams(h):
        kv = h // group
        q = q_ref[0, h]
        return [(lambda r, qs=q[r0:r0 + TQ_SUB]: _dot(qs, kt_ref[0, kv, :, r * TK:(r + 1) * TK]),
                 lambda r: v_ref[0, kv, r * TK:(r + 1) * TK, :],
                 lambda r: None) for r0 in range(0, TQ, TQ_SUB)]

    def pair(j, carry):
        _run_head_pair(bounded_ref[pl.program_id(0), j] != 0, [2 * j, 2 * j + 1], head_streams,
                       n_chunks, s_ref, acc_ref)
        return carry

    lax.fori_loop(0, n_heads // 2, pair, 0)
    outs = []
    for h in range(n_heads):
        a = acc_ref[h]
        outs.append(a[:, :HEAD_DIM] * (1.0 / a[:, V_ONES_LANE:V_ONES_LANE + 1]))
    o_ref[...] = jnp.concatenate(outs, axis=-1).astype(BF16)


def _attn_c_kernel(bounded_ref, q_ref, kt_ref, v_ref, lq1_ref, lk1_ref, lq2_ref, lk2_ref,
                   subln_ref, o_ref, s_ref, acc_ref, *, n_keys, lam_init):
    assert TQ == TK
    nc = n_keys // TK
    cd = pl.program_id(1)
    col = lax.broadcasted_iota(jnp.int32, (1, TK), 1).astype(F32)
    row = lax.broadcasted_iota(jnp.int32, (TQ_SUB, 1), 0).astype(F32)
    subs = [(r, r + TQ_SUB) for r in range(0, TQ, TQ_SUB)]
    neg_dist = [-jnp.abs((lax.broadcasted_iota(jnp.int32, (TQ_SUB, TK), 0) + r0
                          - lax.broadcasted_iota(jnp.int32, (TQ_SUB, TK), 1)).astype(F32))
                for r0, _ in subs]

    def head(hh, carry):
        slope = jnp.float32(LOG2E * 2.0 ** (-8.0 * N_HEADS_C / N_HEADS_C))
        for k in range(N_HEADS_C - 2, -1, -1):
            slope = jnp.where(hh == k, LOG2E * 2.0 ** (-8.0 * (k + 1) / N_HEADS_C), slope)

        chunk, key_term, row_off = [cd], [None], [[None] * len(subs)]
        for r in range(1, nc):
            c = (cd + r) % nc
            side = jnp.where(c < cd, slope, -slope)
            chunk.append(c)
            key_term.append(col * side)
            base = -slope * (jnp.abs(c - cd) * TK).astype(F32)
            row_off.append([base - side * (row + float(r0)) for r0, _ in subs])

        def sub_head_streams(idx):
            q = q_ref[0, hh]

            def scores(r, si):
                s = _dot(q[subs[si][0]:subs[si][1]], kt_ref[0, idx, chunk[r]])
                return s + (neg_dist[si] * slope if r == 0 else key_term[r])

            return [(functools.partial(scores, si=si),
                     lambda r: v_ref[0, hh, pl.ds(pl.multiple_of(chunk[r] * TK, TK), TK), :],
                     lambda r, si=si: row_off[r][si]) for si in range(len(subs))]

        _run_head_pair(bounded_ref[pl.program_id(0), hh] != 0, [2 * hh, 2 * hh + 1],
                       sub_head_streams, nc, s_ref, acc_ref)
        return carry

    lax.fori_loop(0, N_HEADS_C, head, 0)

    lam = (jnp.exp(jnp.sum(lq1_ref[...] * lk1_ref[...], axis=-1, keepdims=True))
           - jnp.exp(jnp.sum(lq2_ref[...] * lk2_ref[...], axis=-1, keepdims=True)) + lam_init)
    outs = []
    for hh in range(N_HEADS_C):
        a0 = acc_ref[2 * hh]
        a1 = acc_ref[2 * hh + 1]
        o = (a0[:, :DIFF_V] * (1.0 / a0[:, V_ONES_LANE:V_ONES_LANE + 1])
             - lam * (a1[:, :DIFF_V] * (1.0 / a1[:, V_ONES_LANE:V_ONES_LANE + 1])))
        outs.append(_rms(o, DIFF_V) * subln_ref[...] * (1.0 - lam_init))
    o_ref[...] = jnp.concatenate(outs, axis=-1).astype(BF16)


def _attention(kernel_fn, bounded, q, kt, v, extras, out_width, B, S):
    nq = S // TQ
    hq, hv = q.shape[1], v.shape[1]
    extra_specs = [pl.BlockSpec(e.shape, lambda b, i, f, n=e.ndim: (0,) * n) for e in extras]
    return pl.pallas_call(
        kernel_fn,
        grid_spec=pltpu.PrefetchScalarGridSpec(
            num_scalar_prefetch=1,
            grid=(B, nq),
            in_specs=[pl.BlockSpec((1, hq, TQ, LANES), lambda b, i, f: (b, 0, i, 0)),
                      pl.BlockSpec((1,) + kt.shape[1:],
                                   lambda b, i, f, n=kt.ndim: (b,) + (0,) * (n - 1)),
                      pl.BlockSpec((1, hv, S, LANES), lambda b, i, f: (b, 0, 0, 0))] + extra_specs,
            out_specs=pl.BlockSpec((TQ, out_width), lambda b, i, f: (b * nq + i, 0)),
            scratch_shapes=[pltpu.VMEM((TQ, S), F32),
                            pltpu.VMEM((2 * bounded.shape[1], TQ, LANES), F32)]),
        out_shape=jax.ShapeDtypeStruct((B * S, out_width), BF16),
        compiler_params=pltpu.CompilerParams(dimension_semantics=("arbitrary", "arbitrary"),
                                             vmem_limit_bytes=VMEM_LIMIT_BYTES),
        name=kernel_fn.func.__name__.strip("_"),
    )(bounded, q, kt, v, *extras)


def _out_kernel(x_ref, oa_ref, ob_ref, oc_ref, wo_ref, g_ref, x1_ref, h2_ref):
    na = N_HEADS_A * HEAD_DIM
    nb = na + N_HEADS_B * MLA_V
    y = (x_ref[...] + _dot(oa_ref[...], wo_ref[:na]) + _dot(ob_ref[...], wo_ref[na:nb])
         + _dot(oc_ref[...], wo_ref[nb:]))
    x1_ref[...] = y
    h2_ref[...] = (_rms(y, D_MODEL) * g_ref[...]).astype(BF16)


def _out_proj(x2d, oa, ob, oc, w_out, g_ffn):
    T = x2d.shape[0]
    tm = TM_OUT
    rows = lambda w: pl.BlockSpec((tm, w), lambda i: (i, 0))
    return pl.pallas_call(
        _out_kernel,
        grid=(T // tm,),
        in_specs=[rows(D_MODEL), rows(oa.shape[1]), rows(ob.shape[1]), rows(oc.shape[1]),
                  pl.BlockSpec(w_out.shape, lambda i: (0, 0)),
                  pl.BlockSpec((1, D_MODEL), lambda i: (0, 0))],
        out_specs=[rows(D_MODEL), rows(D_MODEL)],
        out_shape=[jax.ShapeDtypeStruct((T, D_MODEL), F32),
                   jax.ShapeDtypeStruct((T, D_MODEL), BF16)],
        compiler_params=pltpu.CompilerParams(dimension_semantics=("arbitrary",),
                                             vmem_limit_bytes=VMEM_LIMIT_BYTES),
        name="out_proj",
    )(x2d, oa, ob, oc, w_out, g_ffn)


def _ffn_kernel(hp_ref, h_ref, hn_ref, x1_ref, wu_ref, cw_ref, cb_ref, wd_ref, fg_ref,
                o_ref, act_ref, *, tiles_per_seq, final_norm):
    i = pl.program_id(0)
    tm = h_ref.shape[0]
    halo = hp_ref.shape[0]
    at_start = (i % tiles_per_seq) == 0
    at_end = (i % tiles_per_seq) == tiles_per_seq - 1
    hp = jnp.where(at_start, jnp.zeros_like(hp_ref[...]), hp_ref[...])
    hn = jnp.where(at_end, jnp.zeros_like(hn_ref[...]), hn_ref[...])
    hext = jnp.concatenate([hp, h_ref[...], hn], axis=0)

    def branch(k, c0):
        u = _dot(hext, wu_ref[:, k * D_FF + c0:k * D_FF + c0 + FFN_CHUNK])
        cw = cw_ref[k, :, c0:c0 + FFN_CHUNK]
        return (u[halo - 1:halo - 1 + tm] * cw[0:1] + u[halo:halo + tm] * cw[1:2]
                + u[halo + 1:halo + 1 + tm] * cw[2:3] + cb_ref[k, :, c0:c0 + FFN_CHUNK])

    for c0 in range(0, D_FF, FFN_CHUNK):
        g = branch(0, c0)
        val = branch(1, c0)
        act_ref[:, c0:c0 + FFN_CHUNK] = (g / (1.0 + jnp.exp(-g)) * val).astype(BF16)

    y = x1_ref[...] + _dot(act_ref[...], wd_ref[...])
    if final_norm:
        y = _rms(y, D_MODEL) * fg_ref[...]
    o_ref[...] = y


def _ffn(h2, x1, w_up, conv_w, conv_b, w_down, final_g, S, final_norm):
    T = h2.shape[0]
    tm, halo = TM_FFN, BF16_SUBLANES
    per = tm // halo
    last = T // halo - 1
    resident = lambda shape: pl.BlockSpec(shape, lambda i: (0,) * len(shape),
                                          pipeline_mode=pl.Buffered(1))
    return pl.pallas_call(
        functools.partial(_ffn_kernel, tiles_per_seq=S // tm, final_norm=final_norm),
        grid=(T // tm,),
        in_specs=[pl.BlockSpec((halo, D_MODEL), lambda i: (jnp.maximum(i * per - 1, 0), 0)),
                  pl.BlockSpec((tm, D_MODEL), lambda i: (i, 0)),
                  pl.BlockSpec((halo, D_MODEL), lambda i: (jnp.minimum((i + 1) * per, last), 0)),
                  pl.BlockSpec((tm, D_MODEL), lambda i: (i, 0)),
                  resident((D_MODEL, 2 * D_FF)), resident((2, 3, D_FF)), resident((2, 1, D_FF)),
                  resident((D_FF, D_MODEL)), resident((1, D_MODEL))],
        out_specs=pl.BlockSpec((tm, D_MODEL), lambda i: (i, 0)),
        out_shape=jax.ShapeDtypeStruct((T, D_MODEL), F32),
        scratch_shapes=[pltpu.VMEM((tm, D_FF), BF16)],
        compiler_params=pltpu.CompilerParams(dimension_semantics=("arbitrary",),
                                             vmem_limit_bytes=VMEM_LIMIT_BYTES),
        name="conv_mlp",
    )(h2, h2, h2, x1, w_up, conv_w, conv_b, w_down, final_g)


def _pad_groups(w, n_groups, width):
    r = w.shape[0]
    w = w.reshape(r, n_groups, width)
    w = jnp.pad(w, ((0, 0), (0, 0), (0, LANES - width)))
    return w.reshape(r, n_groups * LANES)


def _rope_tables(S, dim, offset, identity_below):
    nf = dim // 4
    t = jnp.arange(S, dtype=jnp.int32)
    rows = (t // GRID_W).astype(F32)
    cols = (t % GRID_W).astype(F32)
    inv = ROPE_BASE ** (-jnp.arange(nf, dtype=F32) / nf)
    ar = rows[:, None] * inv
    ac = cols[:, None] * inv
    cos = jnp.concatenate([jnp.cos(ar), jnp.cos(ar), jnp.cos(ac), jnp.cos(ac)], axis=-1)
    sin = jnp.concatenate([-jnp.sin(ar), jnp.sin(ar), -jnp.sin(ac), jnp.sin(ac)], axis=-1)
    below = jnp.ones if identity_below else jnp.zeros
    cos = jnp.concatenate([below((S, offset), F32), cos,
                           jnp.zeros((S, LANES - offset - dim), F32)], axis=-1)
    sin = jnp.concatenate([jnp.zeros((S, offset), F32), sin,
                           jnp.zeros((S, LANES - offset - dim), F32)], axis=-1)
    return cos, sin


def _prep_w_in(w):
    d = w.shape[0]
    na, nkv = N_HEADS_A * HEAD_DIM, N_KV_A * HEAD_DIM
    nc_qk, nc_v = N_HEADS_C * 2 * DIFF_QK, N_HEADS_C * DIFF_V
    sizes = [na, nkv, nkv, MLA_Q_RANK, MLA_KV_RANK, MLA_ROPE, nc_qk, nc_qk, nc_v]
    offs = [0]
    for s in sizes:
        offs.append(offs[-1] + s)
    qa, ka, va, cq, ckv, kr, qc, kc, vc = [w[:, offs[k]:offs[k + 1]] for k in range(len(sizes))]
    parts = [
        _pad_groups(qa, N_HEADS_A, HEAD_DIM),
        _pad_groups(ka, N_KV_A, HEAD_DIM),
        _pad_groups(va, N_KV_A, HEAD_DIM),
        jnp.pad(cq, ((0, 0), (0, CQ_PAD - MLA_Q_RANK))),
        ckv,
        jnp.pad(kr, ((0, 0), (MLA_NOPE, LANES - MLA_NOPE - MLA_ROPE))),
        _pad_groups(qc, N_HEADS_C, 2 * DIFF_QK),
        _pad_groups(kc, N_HEADS_C, 2 * DIFF_QK),
        _pad_groups(vc, N_HEADS_C, DIFF_V),
    ]
    out = jnp.concatenate(parts, axis=-1).astype(BF16)
    assert out.shape == (d, IN_PAD)
    return out


def _pad_lanes(v, width):
    return jnp.pad(v, (0, width - v.shape[0])).reshape(1, width)


def kernel(x, norm_attn, w_in, q_norm_a, k_norm_a, q_a_norm_b, w_uq_b, kv_a_norm_b, w_ukv_b,
           lambda_q1_c, lambda_k1_c, lambda_q2_c, lambda_k2_c, subln_c, w_out,
           norm_ffn, w_up, conv_w, conv_b, w_down, final_norm):
    B, S, D = x.shape
    depth = w_in.shape[0]
    assert D == D_MODEL and S % max(TM_IN, TM_OUT, TM_FFN, TQ, TK) == 0 and S % GRID_W == 0
    T = B * S

    tabs = _rope_tables(S, HEAD_DIM, 0, False) + _rope_tables(S, MLA_ROPE, MLA_NOPE, True)
    row = lambda v: v.reshape(1, -1)
    xc = x.reshape(T, D)
    for l in range(depth):
        lam_init = 0.8 - 0.6 * math.exp(-0.3 * l)
        wuq_p = jnp.pad(_pad_groups(w_uq_b[l], N_HEADS_B, MLA_NOPE + MLA_ROPE),
                        ((0, CQ_PAD - MLA_Q_RANK), (0, 0))).astype(BF16)
        wukv = w_ukv_b[l].reshape(MLA_KV_RANK, N_HEADS_B, MLA_NOPE + MLA_V)
        wukv_p = jnp.concatenate(
            [_pad_groups(wukv[:, :, :MLA_NOPE].reshape(MLA_KV_RANK, -1), N_HEADS_B, MLA_NOPE),
             _pad_groups(wukv[:, :, MLA_NOPE:].reshape(MLA_KV_RANK, -1), N_HEADS_B, MLA_V)],
            axis=-1).astype(BF16)

        qa, kta, va, qb, ktb, vb, qc, ktc, vc, nrm = _in_proj(
            xc, row(norm_attn[l]), _prep_w_in(w_in[l]), tabs,
            _pad_lanes(q_norm_a[l], LANES), _pad_lanes(k_norm_a[l], LANES),
            _pad_lanes(q_a_norm_b[l], CQ_PAD), row(kv_a_norm_b[l]), wuq_p, wukv_p, B, S)

        nrm = jnp.max(nrm.reshape(B, -1, LANES), axis=1)

        def bounded(q0, nq, q_rep, k0, nk, k_rep):
            q2 = jnp.repeat(nrm[:, q0:q0 + nq], q_rep, axis=1)
            k2 = jnp.repeat(nrm[:, k0:k0 + nk], k_rep, axis=1)
            ok = q2 * k2 * NRM_MARGIN <= SAFE_LOG2_RANGE ** 2
            return jnp.all(ok.reshape(B, -1, 2), axis=-1).astype(jnp.int32)

        qk_a = (HEAD_DIM * jnp.max(jnp.abs(q_norm_a[l])) * jnp.max(jnp.abs(k_norm_a[l]))
                * (HEAD_DIM ** -0.5 * LOG2E))
        bounded_a = jnp.broadcast_to(
            (qk_a * NRM_MARGIN <= SAFE_LOG2_RANGE).astype(jnp.int32), (B, N_HEADS_A // 2))

        oa = _attention(
            functools.partial(_attn_ab_kernel, n_heads=N_HEADS_A, group=N_HEADS_A // N_KV_A,
                              n_keys=S),
            bounded_a,
            qa, kta, va, [], N_HEADS_A * HEAD_DIM, B, S)
        ob = _attention(
            functools.partial(_attn_ab_kernel, n_heads=N_HEADS_B, group=1, n_keys=S),
            bounded(NRM_QB, N_HEADS_B, 1, NRM_KB, N_HEADS_B, 1),
            qb, ktb, vb, [], N_HEADS_B * MLA_V, B, S)
        oc = _attention(
            functools.partial(_attn_c_kernel, n_keys=S, lam_init=lam_init),
            bounded(NRM_QC, N_HEADS_C, 2, NRM_KC, 2 * N_HEADS_C, 1),
            qc, ktc, vc,
            [row(lambda_q1_c[l]), row(lambda_k1_c[l]), row(lambda_q2_c[l]), row(lambda_k2_c[l]),
             row(subln_c[l])],
            N_HEADS_C * DIFF_V, B, S)

        x1, h2 = _out_proj(xc, oa, ob, oc, w_out[l].astype(BF16), row(norm_ffn[l]))
        xc = _ffn(h2, x1, w_up[l].astype(BF16),
                  conv_w[l].reshape(3, 2, D_FF).transpose(1, 0, 2),
                  conv_b[l].reshape(2, 1, D_FF), w_down[l].astype(BF16),
                  row(final_norm), S, final_norm=(l == depth - 1))
    return xc.reshape(B, S, D)
```

```python
import functools
import math

import jax
import jax.numpy as jnp
import numpy as np
from jax import lax
from jax.experimental import pallas as pl
from jax.experimental.pallas import tpu as pltpu

D_MODEL = 1024
GRID_W = 64
HEAD_DIM = 64
N_HEADS_A = 8
N_KV_A = 2
N_HEADS_B = 4
MLA_Q_RANK = 192
MLA_KV_RANK = 128
MLA_NOPE = 64
MLA_ROPE = 32
MLA_V = 64
N_HEADS_C = 4
DIFF_QK = 32
DIFF_V = 64
D_FF = 2816
ROPE_BASE = 10000.0
EPS = 1e-6

LANES = 128
HALF = LANES // 2
BF16_SUBLANES = 16
VMEM_LIMIT_BYTES = 56 * 1024 * 1024

LOG2E = math.log2(math.e)
assert HEAD_DIM == MLA_V == DIFF_V == 2 * DIFF_QK == HALF

OFF_QA = 0
OFF_KA = OFF_QA + N_HEADS_A * HEAD_DIM
OFF_VA = OFF_KA + LANES
OFF_CQ = OFF_VA + LANES
CQ_PAD = 2 * LANES
OFF_CKV = OFF_CQ + CQ_PAD
OFF_KR = OFF_CKV + MLA_KV_RANK
OFF_QC = OFF_KR + LANES
OFF_KC = OFF_QC + N_HEADS_C * HALF
OFF_VC = OFF_KC + N_HEADS_C * HALF
IN_COLS = OFF_VC + N_HEADS_C * HALF

NRM_QB = 0
NRM_KB = NRM_QB + N_HEADS_B
NRM_QC = NRM_KB + N_HEADS_B
NRM_KC = NRM_QC + N_HEADS_C
NRM_GROUPS = 2 * N_HEADS_B + N_HEADS_C
NRM_MARGIN = 1.03
SAFE_LOG2_RANGE = 55.0

F32 = jnp.float32
BF16 = jnp.bfloat16

TM_IN = 512
TM_OUT = 512
TM_FFN = 512
FFN_CHUNK = 256
TQ = 512
TQ_SUB = 128
TK = 512
SCORE_LOOKAHEAD = 2


def _dot(a, b):
    return jnp.dot(a, b, preferred_element_type=F32)


def _rms(x, n):
    ms = jnp.sum(x * x, axis=-1, keepdims=True) * (1.0 / n)
    return x * lax.rsqrt(ms + EPS)


def _low_lanes():
    return lax.broadcasted_iota(jnp.int32, (1, LANES), 1) < HALF


def _half_rms(x):
    low = _low_lanes()
    sq = x * x
    s_lo = jnp.sum(jnp.where(low, sq, 0.0), axis=-1, keepdims=True)
    s_hi = jnp.sum(jnp.where(low, 0.0, sq), axis=-1, keepdims=True)
    return x * lax.rsqrt(jnp.where(low, s_lo, s_hi) * (1.0 / HALF) + EPS)


def _in_kernel(x_ref, gat_ref, w_ref, cos_a_ref, sin_a_ref, cos_b_ref, sin_b_ref,
               qg_a_ref, kg_a_ref, qg_b_ref, kvg_b_ref, wuq_ref, wukv_ref, sel_ref,
               qa_ref, kta_ref, va_ref, qb_ref, ktb_ref, vb_ref, qc_ref, ktc_ref, vc_ref, nrm_ref):
    x = x_ref[...]
    tm = x.shape[0]
    h = (_rms(x, D_MODEL) * gat_ref[...]).astype(BF16)

    lane = lax.broadcasted_iota(jnp.int32, (1, LANES), 1)
    low = lane < HALF
    row = lax.broadcasted_iota(jnp.int32, (LANES, 1), 0)

    z = _dot(h, w_ref[...])

    def seg(off, width):
        return z[:, off:off + width]

    def swap(y, nf):
        up = pltpu.roll(y, LANES - nf, axis=1)
        down = pltpu.roll(y, nf, axis=1)
        return jnp.where((lane & nf) == 0, up, down)

    def rope(y, c, s, nf):
        return y * c + swap(y, nf) * s

    def group(z, g):
        return z[:, g * LANES:(g + 1) * LANES]

    squares = []

    def note(y):
        squares.append((y * y).astype(BF16))

    def noted_row_sums(first):
        return _dot(jnp.concatenate(squares[first:], axis=-1),
                    sel_ref[first * LANES:len(squares) * LANES])

    cos_b, sin_b = cos_b_ref[...], sin_b_ref[...]
    nf_b = MLA_ROPE // 4
    cq = (_rms(seg(OFF_CQ, CQ_PAD), MLA_Q_RANK) * qg_b_ref[...]).astype(BF16)
    qb = _dot(cq, wuq_ref[...])
    scale_b = (MLA_NOPE + MLA_ROPE) ** -0.5 * LOG2E
    for hh in range(N_HEADS_B):
        y = rope(group(qb, hh), cos_b, sin_b, nf_b) * scale_b
        qb_ref[0, hh] = y.astype(BF16)
        note(y)
    ckv = (_rms(seg(OFF_CKV, MLA_KV_RANK), MLA_KV_RANK) * kvg_b_ref[...]).astype(BF16)
    kvb = _dot(ckv, wukv_ref[...])
    kr = rope(seg(OFF_KR, LANES), cos_b, sin_b, nf_b)
    for hh in range(N_HEADS_B):
        y = group(kvb, hh) + kr
        ktb_ref[0, hh] = y.T.astype(BF16)
        note(y)
        ones_other_half = jnp.where(low, float(hh % 2), float(1 - hh % 2))
        vb_ref[0, hh] = (group(kvb, N_HEADS_B + hh) + ones_other_half).astype(BF16)
    n2 = noted_row_sums(0)
    n_noted_b = len(squares)

    zq = seg(OFF_QC, N_HEADS_C * HALF)
    zk = seg(OFF_KC, N_HEADS_C * HALF)
    zv = seg(OFF_VC, N_HEADS_C * HALF)
    k_groups = []
    for p in range(N_HEADS_C // 2):
        y = group(zq, p) * (DIFF_QK ** -0.5 * LOG2E)
        qc_ref[0, p] = y.astype(BF16)
        note(y)
        k_groups.append(group(zk, p))
        kt = group(zk, p).T
        for sub in range(4):
            mine = (row >= sub * DIFF_QK) & (row < (sub + 1) * DIFF_QK)
            ktc_ref[0, 4 * p + sub, 0] = jnp.where(mine, kt, 0.0).astype(BF16)
        vz = group(zv, p)
        vc_ref[0, 2 * p] = jnp.where(low, vz, 1.0).astype(BF16)
        vc_ref[0, 2 * p + 1] = jnp.where(low, 1.0, vz).astype(BF16)
    for kz in k_groups:
        note(kz)
    n2 = n2 + noted_row_sums(n_noted_b)
    nrm_ref[0] = jnp.max(n2, axis=0, keepdims=True)

    cos_a, sin_a = cos_a_ref[...], sin_a_ref[...]
    nf_a = HEAD_DIM // 4
    zq = seg(OFF_QA, N_HEADS_A * HEAD_DIM)
    for p in range(N_HEADS_A // 2):
        y = rope(_half_rms(group(zq, p)) * qg_a_ref[...], cos_a, sin_a, nf_a)
        qa_ref[0, p] = (y * (HEAD_DIM ** -0.5 * LOG2E)).astype(BF16)
    y = rope(_half_rms(seg(OFF_KA, LANES)) * kg_a_ref[...], cos_a, sin_a, nf_a)
    kt = y.T.astype(BF16)
    zeros = jnp.zeros((HALF, tm), BF16)
    for g in range(N_KV_A):
        kg = kt[g * HALF:(g + 1) * HALF]
        kta_ref[0, 2 * g] = jnp.concatenate([kg, zeros], axis=0)
        kta_ref[0, 2 * g + 1] = jnp.concatenate([zeros, kg], axis=0)
    zv = seg(OFF_VA, LANES)
    va_ref[0, 0] = jnp.where(low, zv, 1.0).astype(BF16)
    va_ref[0, 1] = jnp.where(low, 1.0, zv).astype(BF16)


def _norm_routing():
    sel = np.zeros((NRM_GROUPS, LANES, LANES), np.float32)
    g = 0
    for slot in ([NRM_QB + h for h in range(N_HEADS_B)] + [NRM_KB + h for h in range(N_HEADS_B)]):
        sel[g, :, slot] = 1.0
        g += 1
    for p in range(N_HEADS_C // 2):
        for f in range(2):
            sel[g, f * HALF:(f + 1) * HALF, NRM_QC + 2 * p + f] = 1.0
        g += 1
    for p in range(N_HEADS_C // 2):
        for sub in range(4):
            sel[g, sub * DIFF_QK:(sub + 1) * DIFF_QK, NRM_KC + 4 * p + sub] = 1.0
        g += 1
    assert g == NRM_GROUPS
    return jnp.asarray(sel.reshape(NRM_GROUPS * LANES, LANES), BF16)


def _in_proj(x2d, gat, w_in_r, tabs, qg_a, kg_a, qg_b, kvg_b, wuq_p, wukv_p, B, S):
    T = B * S
    tm = TM_IN
    nst = S // tm
    const = lambda shape: pl.BlockSpec(shape, lambda i: (0,) * len(shape))
    tab = pl.BlockSpec((tm, LANES), lambda i: (i % nst, 0))
    hm = lambda H: pl.BlockSpec((1, H, tm, LANES), lambda i: (i // nst, 0, i % nst, 0))
    hmt = lambda H: pl.BlockSpec((1, H, LANES, tm), lambda i: (i // nst, 0, 0, i % nst))
    sds = lambda H: jax.ShapeDtypeStruct((B, H, S, LANES), BF16)
    sdt = lambda H: jax.ShapeDtypeStruct((B, H, LANES, S), BF16)
    return pl.pallas_call(
        _in_kernel,
        grid=(T // tm,),
        in_specs=[pl.BlockSpec((tm, D_MODEL), lambda i: (i, 0)),
                  const((1, D_MODEL)), const((D_MODEL, IN_COLS)),
                  tab, tab, tab, tab,
                  const((1, LANES)), const((1, LANES)), const((1, CQ_PAD)), const((1, MLA_KV_RANK)),
                  const((CQ_PAD, N_HEADS_B * LANES)), const((MLA_KV_RANK, 2 * N_HEADS_B * LANES)),
                  const((NRM_GROUPS * LANES, LANES))],
        out_specs=[hm(N_HEADS_A // 2), hmt(2 * N_KV_A), hm(N_KV_A),
                   hm(N_HEADS_B), hmt(N_HEADS_B), hm(N_HEADS_B),
                   hm(N_HEADS_C // 2),
                   pl.BlockSpec((1, 2 * N_HEADS_C, 1, LANES, tm),
                                lambda i: (i // nst, 0, i % nst, 0, 0)),
                   hm(N_HEADS_C),
                   pl.BlockSpec((1, 1, LANES), lambda i: (i, 0, 0))],
        out_shape=[sds(N_HEADS_A // 2), sdt(2 * N_KV_A), sds(N_KV_A),
                   sds(N_HEADS_B), sdt(N_HEADS_B), sds(N_HEADS_B),
                   sds(N_HEADS_C // 2),
                   jax.ShapeDtypeStruct((B, 2 * N_HEADS_C, S // tm, LANES, tm), BF16),
                   sds(N_HEADS_C),
                   jax.ShapeDtypeStruct((T // tm, 1, LANES), F32)],
        compiler_params=pltpu.CompilerParams(dimension_semantics=("arbitrary",),
                                             vmem_limit_bytes=VMEM_LIMIT_BYTES),
        name="in_proj",
    )(x2d, gat, w_in_r, *tabs, qg_a, kg_a, qg_b, kvg_b, wuq_p, wukv_p, _norm_routing())


def _lane_group_max(s):
    m = s[:, :LANES]
    for j in range(1, s.shape[1] // LANES):
        m = jnp.maximum(m, s[:, j * LANES:(j + 1) * LANES])
    return m


def _pv_bounded(streams, n_chunks):
    items = [(r, i) for r in range(n_chunks) for i in range(len(streams))]
    accs = [None] * len(streams)
    in_flight = {}
    for k in range(len(items) + SCORE_LOOKAHEAD):
        if k < len(items):
            r, i = items[k]
            scores, _, row_off = streams[i]
            s, off = scores(r), row_off(r)
            in_flight[k] = s if off is None else s + off
        if k >= SCORE_LOOKAHEAD:
            r, i = items[k - SCORE_LOOKAHEAD]
            d = _dot(jnp.exp2(in_flight.pop(k - SCORE_LOOKAHEAD)).astype(BF16), streams[i][1](r))
            accs[i] = d if accs[i] is None else accs[i] + d
    return accs


def _pv_exact(streams, n_chunks, s_ref):
    maxes = []
    for i, (scores, _, row_off) in enumerate(streams):
        m_run = None
        for r in range(n_chunks):
            s = scores(r)
            s_ref[i * TQ_SUB:(i + 1) * TQ_SUB, r * TK:(r + 1) * TK] = s
            m, off = _lane_group_max(s), row_off(r)
            if off is not None:
                m = m + off
            m_run = m if m_run is None else jnp.maximum(m_run, m)
        maxes.append(jnp.max(m_run, axis=-1, keepdims=True))
    accs = []
    for i, (_, values, row_off) in enumerate(streams):
        acc = None
        for r in range(n_chunks):
            off = row_off(r)
            shift = maxes[i] if off is None else maxes[i] - off
            p = jnp.exp2(s_ref[i * TQ_SUB:(i + 1) * TQ_SUB, r * TK:(r + 1) * TK] - shift)
            d = _dot(p.astype(BF16), values(r))
            acc = d if acc is None else acc + d
        accs.append(acc)
    return accs


def _run_head_pair(bounded, heads, head_streams, n_chunks, s_ref, acc_ref):
    n_sub = TQ // TQ_SUB

    @pl.when(bounded)
    def _():
        accs = _pv_bounded([s for h in heads for s in head_streams(h)], n_chunks)
        for k, h in enumerate(heads):
            acc_ref[h] = jnp.concatenate(accs[k * n_sub:(k + 1) * n_sub], axis=0)

    @pl.when(jnp.logical_not(bounded))
    def _():
        for h in heads:
            acc_ref[h] = jnp.concatenate(_pv_exact(head_streams(h), n_chunks, s_ref), axis=0)


def _normalised_pair(acc_low, acc_high):
    low = _low_lanes()
    num = jnp.where(low, acc_low, acc_high)
    den = pltpu.roll(jnp.where(low, acc_high, acc_low), HALF, axis=1)
    return num / den


def _attn_ab_kernel(bounded_ref, q_ref, kt_ref, v_ref, o_ref, s_ref, acc_ref, *,
                    n_heads, q_index, kt_index, v_index, out_pairs, n_keys):
    n_chunks = n_keys // TK

    def head_streams(h):
        q = q_ref[0, q_index(h)]
        return [(lambda r, qs=q[r0:r0 + TQ_SUB]:
                 _dot(qs, kt_ref[0, kt_index(h), :, r * TK:(r + 1) * TK]),
                 lambda r: v_ref[0, v_index(h), r * TK:(r + 1) * TK, :],
                 lambda r: None) for r0 in range(0, TQ, TQ_SUB)]

    def pair(j, carry):
        _run_head_pair(bounded_ref[pl.program_id(0), j] != 0, [2 * j, 2 * j + 1], head_streams,
                       n_chunks, s_ref, acc_ref)
        return carry

    lax.fori_loop(0, n_heads // 2, pair, 0)
    o_ref[...] = jnp.concatenate([_normalised_pair(acc_ref[a], acc_ref[b]) for a, b in out_pairs],
                                 axis=-1).astype(BF16)


def _attn_c_kernel(bounded_ref, q_ref, kt_ref, v_ref, lq1_ref, lk1_ref, lq2_ref, lk2_ref,
                   subln_ref, o_ref, s_ref, acc_ref, *, n_keys, lam_init):
    assert TQ == TK
    nc = n_keys // TK
    cd = pl.program_id(1)
    col = lax.broadcasted_iota(jnp.int32, (1, TK), 1).astype(F32)
    row = lax.broadcasted_iota(jnp.int32, (TQ_SUB, 1), 0).astype(F32)
    subs = [(r, r + TQ_SUB) for r in range(0, TQ, TQ_SUB)]
    neg_dist = [-jnp.abs((lax.broadcasted_iota(jnp.int32, (TQ_SUB, TK), 0) + r0
                          - lax.broadcasted_iota(jnp.int32, (TQ_SUB, TK), 1)).astype(F32))
                for r0, _ in subs]

    def head(hh, carry):
        slope = jnp.float32(LOG2E * 2.0 ** (-8.0 * N_HEADS_C / N_HEADS_C))
        for k in range(N_HEADS_C - 2, -1, -1):
            slope = jnp.where(hh == k, LOG2E * 2.0 ** (-8.0 * (k + 1) / N_HEADS_C), slope)

        chunk, key_term, row_off = [cd], [None], [[None] * len(subs)]
        for r in range(1, nc):
            c = (cd + r) % nc
            side = jnp.where(c < cd, slope, -slope)
            chunk.append(c)
            key_term.append(col * side)
            base = -slope * (jnp.abs(c - cd) * TK).astype(F32)
            row_off.append([base - side * (row + float(r0)) for r0, _ in subs])

        def sub_head_streams(idx):
            q = q_ref[0, hh // 2]

            def scores(r, si):
                s = _dot(q[subs[si][0]:subs[si][1]], kt_ref[0, idx, chunk[r]])
                return s + (neg_dist[si] * slope if r == 0 else key_term[r])

            return [(functools.partial(scores, si=si),
                     lambda r: v_ref[0, hh, pl.ds(pl.multiple_of(chunk[r] * TK, TK), TK), :],
                     lambda r, si=si: row_off[r][si]) for si in range(len(subs))]

        _run_head_pair(bounded_ref[pl.program_id(0), hh] != 0, [2 * hh, 2 * hh + 1],
                       sub_head_streams, nc, s_ref, acc_ref)
        return carry

    lax.fori_loop(0, N_HEADS_C, head, 0)

    lam = (jnp.exp(jnp.sum(lq1_ref[...] * lk1_ref[...], axis=-1, keepdims=True))
           - jnp.exp(jnp.sum(lq2_ref[...] * lk2_ref[...], axis=-1, keepdims=True)) + lam_init)
    outs = []
    for p in range(N_HEADS_C // 2):
        o = (_normalised_pair(acc_ref[4 * p], acc_ref[4 * p + 2])
             - lam * _normalised_pair(acc_ref[4 * p + 1], acc_ref[4 * p + 3]))
        outs.append(_half_rms(o) * subln_ref[...] * (1.0 - lam_init))
    o_ref[...] = jnp.concatenate(outs, axis=-1).astype(BF16)


def _attention(kernel_fn, bounded, q, kt, v, extras, out_width, B, S):
    nq = S // TQ
    hq, hv = q.shape[1], v.shape[1]
    extra_specs = [pl.BlockSpec(e.shape, lambda b, i, f, n=e.ndim: (0,) * n) for e in extras]
    return pl.pallas_call(
        kernel_fn,
        grid_spec=pltpu.PrefetchScalarGridSpec(
            num_scalar_prefetch=1,
            grid=(B, nq),
            in_specs=[pl.BlockSpec((1, hq, TQ, LANES), lambda b, i, f: (b, 0, i, 0)),
                      pl.BlockSpec((1,) + kt.shape[1:],
                                   lambda b, i, f, n=kt.ndim: (b,) + (0,) * (n - 1)),
                      pl.BlockSpec((1, hv, S, LANES), lambda b, i, f: (b, 0, 0, 0))] + extra_specs,
            out_specs=pl.BlockSpec((TQ, out_width), lambda b, i, f: (b * nq + i, 0)),
            scratch_shapes=[pltpu.VMEM((TQ, S), F32),
                            pltpu.VMEM((2 * bounded.shape[1], TQ, LANES), F32)]),
        out_shape=jax.ShapeDtypeStruct((B * S, out_width), BF16),
        compiler_params=pltpu.CompilerParams(dimension_semantics=("arbitrary", "arbitrary"),
                                             vmem_limit_bytes=VMEM_LIMIT_BYTES),
        name=kernel_fn.func.__name__.strip("_"),
    )(bounded, q, kt, v, *extras)


def _out_kernel(x_ref, oa_ref, ob_ref, oc_ref, wo_ref, g_ref, x1_ref, h2_ref):
    na = N_HEADS_A * HEAD_DIM
    nb = na + N_HEADS_B * MLA_V
    y = (x_ref[...] + _dot(oa_ref[...], wo_ref[:na]) + _dot(ob_ref[...], wo_ref[na:nb])
         + _dot(oc_ref[...], wo_ref[nb:]))
    x1_ref[...] = y
    h2_ref[...] = (_rms(y, D_MODEL) * g_ref[...]).astype(BF16)


def _out_proj(x2d, oa, ob, oc, w_out, g_ffn):
    T = x2d.shape[0]
    tm = TM_OUT
    rows = lambda w: pl.BlockSpec((tm, w), lambda i: (i, 0))
    return pl.pallas_call(
        _out_kernel,
        grid=(T // tm,),
        in_specs=[rows(D_MODEL), rows(oa.shape[1]), rows(ob.shape[1]), rows(oc.shape[1]),
                  pl.BlockSpec(w_out.shape, lambda i: (0, 0)),
                  pl.BlockSpec((1, D_MODEL), lambda i: (0, 0))],
        out_specs=[rows(D_MODEL), rows(D_MODEL)],
        out_shape=[jax.ShapeDtypeStruct((T, D_MODEL), F32),
                   jax.ShapeDtypeStruct((T, D_MODEL), BF16)],
        compiler_params=pltpu.CompilerParams(dimension_semantics=("arbitrary",),
                                             vmem_limit_bytes=VMEM_LIMIT_BYTES),
        name="out_proj",
    )(x2d, oa, ob, oc, w_out, g_ffn)


def _ffn_kernel(hp_ref, h_ref, hn_ref, x1_ref, wu_ref, cw_ref, cb_ref, wd_ref, fg_ref,
                o_ref, act_ref, *, tiles_per_seq, final_norm):
    i = pl.program_id(0)
    tm = h_ref.shape[0]
    halo = hp_ref.shape[0]
    at_start = (i % tiles_per_seq) == 0
    at_end = (i % tiles_per_seq) == tiles_per_seq - 1
    hp = jnp.where(at_start, jnp.zeros_like(hp_ref[...]), hp_ref[...])
    hn = jnp.where(at_end, jnp.zeros_like(hn_ref[...]), hn_ref[...])
    hext = jnp.concatenate([hp, h_ref[...], hn], axis=0)

    def branch(k, c0):
        u = _dot(hext, wu_ref[:, k * D_FF + c0:k * D_FF + c0 + FFN_CHUNK])
        cw = cw_ref[k, :, c0:c0 + FFN_CHUNK]
        return (u[halo - 1:halo - 1 + tm] * cw[0:1] + u[halo:halo + tm] * cw[1:2]
                + u[halo + 1:halo + 1 + tm] * cw[2:3] + cb_ref[k, :, c0:c0 + FFN_CHUNK])

    for c0 in range(0, D_FF, FFN_CHUNK):
        g = branch(0, c0)
        val = branch(1, c0)
        act_ref[:, c0:c0 + FFN_CHUNK] = (g / (1.0 + jnp.exp(-g)) * val).astype(BF16)

    y = x1_ref[...] + _dot(act_ref[...], wd_ref[...])
    if final_norm:
        y = _rms(y, D_MODEL) * fg_ref[...]
    o_ref[...] = y


def _ffn(h2, x1, w_up, conv_w, conv_b, w_down, final_g, S, final_norm):
    T = h2.shape[0]
    tm, halo = TM_FFN, BF16_SUBLANES
    per = tm // halo
    last = T // halo - 1
    resident = lambda shape: pl.BlockSpec(shape, lambda i: (0,) * len(shape),
                                          pipeline_mode=pl.Buffered(1))
    return pl.pallas_call(
        functools.partial(_ffn_kernel, tiles_per_seq=S // tm, final_norm=final_norm),
        grid=(T // tm,),
        in_specs=[pl.BlockSpec((halo, D_MODEL), lambda i: (jnp.maximum(i * per - 1, 0), 0)),
                  pl.BlockSpec((tm, D_MODEL), lambda i: (i, 0)),
                  pl.BlockSpec((halo, D_MODEL), lambda i: (jnp.minimum((i + 1) * per, last), 0)),
                  pl.BlockSpec((tm, D_MODEL), lambda i: (i, 0)),
                  resident((D_MODEL, 2 * D_FF)), resident((2, 3, D_FF)), resident((2, 1, D_FF)),
                  resident((D_FF, D_MODEL)), resident((1, D_MODEL))],
        out_specs=pl.BlockSpec((tm, D_MODEL), lambda i: (i, 0)),
        out_shape=jax.ShapeDtypeStruct((T, D_MODEL), F32),
        scratch_shapes=[pltpu.VMEM((tm, D_FF), BF16)],
        compiler_params=pltpu.CompilerParams(dimension_semantics=("arbitrary",),
                                             vmem_limit_bytes=VMEM_LIMIT_BYTES),
        name="conv_mlp",
    )(h2, h2, h2, x1, w_up, conv_w, conv_b, w_down, final_g)


def _pad_groups(w, n_groups, width, offset=0):
    r = w.shape[0]
    w = w.reshape(r, n_groups, width)
    w = jnp.pad(w, ((0, 0), (0, 0), (offset, LANES - width - offset)))
    return w.reshape(r, n_groups * LANES)


def _rope_core(S, dim):
    nf = dim // 4
    t = jnp.arange(S, dtype=jnp.int32)
    rows = (t // GRID_W).astype(F32)
    cols = (t % GRID_W).astype(F32)
    inv = ROPE_BASE ** (-jnp.arange(nf, dtype=F32) / nf)
    ar = rows[:, None] * inv
    ac = cols[:, None] * inv
    cos = jnp.concatenate([jnp.cos(ar), jnp.cos(ar), jnp.cos(ac), jnp.cos(ac)], axis=-1)
    sin = jnp.concatenate([-jnp.sin(ar), jnp.sin(ar), -jnp.sin(ac), jnp.sin(ac)], axis=-1)
    return cos, sin


def _rope_tables(S):
    cos_a, sin_a = _rope_core(S, HEAD_DIM)
    cos_b, sin_b = _rope_core(S, MLA_ROPE)
    pad = LANES - MLA_NOPE - MLA_ROPE
    return (jnp.tile(cos_a, (1, 2)), jnp.tile(sin_a, (1, 2)),
            jnp.concatenate([jnp.ones((S, MLA_NOPE), F32), cos_b, jnp.zeros((S, pad), F32)], -1),
            jnp.concatenate([jnp.zeros((S, MLA_NOPE), F32), sin_b, jnp.zeros((S, pad), F32)], -1))


def _prep_w_in(w):
    a_end = (N_HEADS_A + 2 * N_KV_A) * HEAD_DIM
    b_end = a_end + MLA_Q_RANK + MLA_KV_RANK + MLA_ROPE
    cq = w[:, a_end:a_end + MLA_Q_RANK]
    ckv = w[:, a_end + MLA_Q_RANK:a_end + MLA_Q_RANK + MLA_KV_RANK]
    kr = w[:, b_end - MLA_ROPE:b_end]
    out = jnp.concatenate([
        w[:, :a_end],
        jnp.pad(cq, ((0, 0), (0, CQ_PAD - MLA_Q_RANK))),
        ckv,
        jnp.pad(kr, ((0, 0), (MLA_NOPE, LANES - MLA_NOPE - MLA_ROPE))),
        w[:, b_end:],
    ], axis=-1).astype(BF16)
    assert out.shape[1] == IN_COLS
    return out


def _twice(v):
    return jnp.tile(v, 2).reshape(1, 2 * v.shape[0])


def kernel(x, norm_attn, w_in, q_norm_a, k_norm_a, q_a_norm_b, w_uq_b, kv_a_norm_b, w_ukv_b,
           lambda_q1_c, lambda_k1_c, lambda_q2_c, lambda_k2_c, subln_c, w_out,
           norm_ffn, w_up, conv_w, conv_b, w_down, final_norm):
    B, S, D = x.shape
    depth = w_in.shape[0]
    assert D == D_MODEL and S % max(TM_IN, TM_OUT, TM_FFN, TQ, TK) == 0 and S % GRID_W == 0
    assert TM_IN == TK
    T = B * S

    tabs = _rope_tables(S)
    row = lambda v: v.reshape(1, -1)
    a_pairs = [(j, j + N_HEADS_A // 2) for j in range(N_HEADS_A // 2)]
    a_rows = np.concatenate([np.arange(h * HEAD_DIM, (h + 1) * HEAD_DIM)
                             for pair in a_pairs for h in pair])
    xc = x.reshape(T, D)
    for l in range(depth):
        lam_init = 0.8 - 0.6 * math.exp(-0.3 * l)
        wuq_p = jnp.pad(_pad_groups(w_uq_b[l], N_HEADS_B, MLA_NOPE + MLA_ROPE),
                        ((0, CQ_PAD - MLA_Q_RANK), (0, 0))).astype(BF16)
        wukv = w_ukv_b[l].reshape(MLA_KV_RANK, N_HEADS_B, MLA_NOPE + MLA_V)
        v_cols = [_pad_groups(wukv[:, h, MLA_NOPE:], 1, MLA_V, offset=HALF * (h % 2))
                  for h in range(N_HEADS_B)]
        wukv_p = jnp.concatenate(
            [_pad_groups(wukv[:, :, :MLA_NOPE].reshape(MLA_KV_RANK, -1), N_HEADS_B, MLA_NOPE)]
            + v_cols, axis=-1).astype(BF16)

        qa, kta, va, qb, ktb, vb, qc, ktc, vc, nrm = _in_proj(
            xc, row(norm_attn[l]), _prep_w_in(w_in[l]), tabs,
            _twice(q_norm_a[l]), _twice(k_norm_a[l]),
            jnp.pad(q_a_norm_b[l], (0, CQ_PAD - MLA_Q_RANK)).reshape(1, CQ_PAD),
            row(kv_a_norm_b[l]), wuq_p, wukv_p, B, S)

        nrm = jnp.max(nrm.reshape(B, -1, LANES), axis=1)

        def bounded(q0, nq, q_rep, k0, nk, k_rep):
            q2 = jnp.repeat(nrm[:, q0:q0 + nq], q_rep, axis=1)
            k2 = jnp.repeat(nrm[:, k0:k0 + nk], k_rep, axis=1)
            ok = q2 * k2 * NRM_MARGIN <= SAFE_LOG2_RANGE ** 2
            return jnp.all(ok.reshape(B, -1, 2), axis=-1).astype(jnp.int32)

        qk_a = (HEAD_DIM * jnp.max(jnp.abs(q_norm_a[l])) * jnp.max(jnp.abs(k_norm_a[l]))
                * (HEAD_DIM ** -0.5 * LOG2E))
        bounded_a = jnp.broadcast_to(
            (qk_a * NRM_MARGIN <= SAFE_LOG2_RANGE).astype(jnp.int32), (B, N_HEADS_A // 2))

        group = N_HEADS_A // N_KV_A
        oa = _attention(
            functools.partial(_attn_ab_kernel, n_heads=N_HEADS_A, q_index=lambda h: h // 2,
                              kt_index=lambda h: 2 * (h // group) + h % 2,
                              v_index=lambda h: h // group, out_pairs=a_pairs, n_keys=S),
            bounded_a, qa, kta, va, [], N_HEADS_A * HEAD_DIM, B, S)
        same = lambda h: h
        ob = _attention(
            functools.partial(_attn_ab_kernel, n_heads=N_HEADS_B, q_index=same, kt_index=same,
                              v_index=same,
                              out_pairs=[(2 * j, 2 * j + 1) for j in range(N_HEADS_B // 2)],
                              n_keys=S),
            bounded(NRM_QB, N_HEADS_B, 1, NRM_KB, N_HEADS_B, 1),
            qb, ktb, vb, [], N_HEADS_B * MLA_V, B, S)
        oc = _attention(
            functools.partial(_attn_c_kernel, n_keys=S, lam_init=lam_init),
            bounded(NRM_QC, N_HEADS_C, 2, NRM_KC, 2 * N_HEADS_C, 1),
            qc, ktc, vc,
            [row(lambda_q1_c[l]), row(lambda_k1_c[l]), row(lambda_q2_c[l]), row(lambda_k2_c[l]),
             _twice(subln_c[l])],
            N_HEADS_C * DIFF_V, B, S)

        w_out_l = jnp.concatenate([w_out[l][a_rows], w_out[l][N_HEADS_A * HEAD_DIM:]], axis=0)
        x1, h2 = _out_proj(xc, oa, ob, oc, w_out_l.astype(BF16), row(norm_ffn[l]))
        xc = _ffn(h2, x1, w_up[l].astype(BF16),
                  conv_w[l].reshape(3, 2, D_FF).transpose(1, 0, 2),
                  conv_b[l].reshape(2, 1, D_FF), w_down[l].astype(BF16),
                  row(final_norm), S, final_norm=(l == depth - 1))
    return xc.reshape(B, S, D)
```

```python
import functools
import math

import jax
import jax.numpy as jnp
import numpy as np
from jax import lax
from jax.experimental import pallas as pl
from jax.experimental.pallas import tpu as pltpu

D_MODEL = 1024
GRID_W = 64
HEAD_DIM = 64
N_HEADS_A = 8
N_KV_A = 2
N_HEADS_B = 4
MLA_Q_RANK = 192
MLA_KV_RANK = 128
MLA_NOPE = 64
MLA_ROPE = 32
MLA_V = 64
N_HEADS_C = 4
DIFF_QK = 32
DIFF_V = 64
D_FF = 2816
ROPE_BASE = 10000.0
EPS = 1e-6

LANES = 128
HALF = LANES // 2
BF16_SUBLANES = 16
VMEM_LIMIT_BYTES = 56 * 1024 * 1024

LOG2E = math.log2(math.e)
assert HEAD_DIM == MLA_V == DIFF_V == 2 * DIFF_QK == HALF

OFF_QA = 0
OFF_KA = OFF_QA + N_HEADS_A * HEAD_DIM
OFF_VA = OFF_KA + LANES
OFF_CQ = OFF_VA + LANES
CQ_PAD = 2 * LANES
OFF_CKV = OFF_CQ + CQ_PAD
OFF_KR = OFF_CKV + MLA_KV_RANK
OFF_QC = OFF_KR + LANES
OFF_KC = OFF_QC + N_HEADS_C * HALF
OFF_VC = OFF_KC + N_HEADS_C * HALF
IN_COLS = OFF_VC + N_HEADS_C * HALF

NRM_QB = 0
NRM_KB = NRM_QB + N_HEADS_B
NRM_QC = NRM_KB + N_HEADS_B
NRM_KC = NRM_QC + N_HEADS_C
NRM_GROUPS = 2 * N_HEADS_B + N_HEADS_C
NRM_MARGIN = 1.03
SAFE_LOG2_RANGE = 55.0

F32 = jnp.float32
BF16 = jnp.bfloat16

TM_IN = 512
TM_OUT = 512
TM_FFN = 512
FFN_CHUNK = 256
TQ = 512
TQ_SUB = 128
TK = 512
SCORE_LOOKAHEAD = 2
HEADS_PER_PIPELINE = 4


def _dot(a, b):
    return jnp.dot(a, b, preferred_element_type=F32)


def _rms(x, n):
    ms = jnp.sum(x * x, axis=-1, keepdims=True) * (1.0 / n)
    return x * lax.rsqrt(ms + EPS)


def _low_lanes():
    return lax.broadcasted_iota(jnp.int32, (1, LANES), 1) < HALF


def _half_rms(x):
    low = _low_lanes()
    sq = x * x
    s_lo = jnp.sum(jnp.where(low, sq, 0.0), axis=-1, keepdims=True)
    s_hi = jnp.sum(jnp.where(low, 0.0, sq), axis=-1, keepdims=True)
    return x * lax.rsqrt(jnp.where(low, s_lo, s_hi) * (1.0 / HALF) + EPS)


def _in_kernel(x_ref, gat_ref, w_ref, cos_a_ref, sin_a_ref, cos_b_ref, sin_b_ref,
               qg_a_ref, kg_a_ref, qg_b_ref, kvg_b_ref, wuq_ref, wukv_ref, sel_ref,
               qa_ref, kta_ref, va_ref, qb_ref, ktb_ref, vb_ref, qc_ref, ktc_ref, vc_ref, nrm_ref):
    x = x_ref[...]
    tm = x.shape[0]
    h = (_rms(x, D_MODEL) * gat_ref[...]).astype(BF16)

    lane = lax.broadcasted_iota(jnp.int32, (1, LANES), 1)
    low = lane < HALF
    row = lax.broadcasted_iota(jnp.int32, (LANES, 1), 0)

    z = _dot(h, w_ref[...])

    def seg(off, width):
        return z[:, off:off + width]

    def swap(y, nf):
        up = pltpu.roll(y, LANES - nf, axis=1)
        down = pltpu.roll(y, nf, axis=1)
        return jnp.where((lane & nf) == 0, up, down)

    def rope(y, c, s, nf):
        return y * c + swap(y, nf) * s

    def group(z, g):
        return z[:, g * LANES:(g + 1) * LANES]

    squares = []

    def note(y):
        squares.append((y * y).astype(BF16))

    def noted_row_sums(first):
        return _dot(jnp.concatenate(squares[first:], axis=-1),
                    sel_ref[first * LANES:len(squares) * LANES])

    cos_b, sin_b = cos_b_ref[...], sin_b_ref[...]
    nf_b = MLA_ROPE // 4
    cq = (_rms(seg(OFF_CQ, CQ_PAD), MLA_Q_RANK) * qg_b_ref[...]).astype(BF16)
    qb = _dot(cq, wuq_ref[...])
    scale_b = (MLA_NOPE + MLA_ROPE) ** -0.5 * LOG2E
    for hh in range(N_HEADS_B):
        y = rope(group(qb, hh), cos_b, sin_b, nf_b) * scale_b
        qb_ref[0, hh] = y.astype(BF16)
        note(y)
    ckv = (_rms(seg(OFF_CKV, MLA_KV_RANK), MLA_KV_RANK) * kvg_b_ref[...]).astype(BF16)
    kvb = _dot(ckv, wukv_ref[...])
    kr = rope(seg(OFF_KR, LANES), cos_b, sin_b, nf_b)
    for hh in range(N_HEADS_B):
        y = group(kvb, hh) + kr
        ktb_ref[0, hh] = y.T.astype(BF16)
        note(y)
        ones_other_half = jnp.where(low, float(hh % 2), float(1 - hh % 2))
        vb_ref[0, hh] = (group(kvb, N_HEADS_B + hh) + ones_other_half).astype(BF16)
    n2 = noted_row_sums(0)
    n_noted_b = len(squares)

    zq = seg(OFF_QC, N_HEADS_C * HALF)
    zk = seg(OFF_KC, N_HEADS_C * HALF)
    zv = seg(OFF_VC, N_HEADS_C * HALF)
    k_groups = []
    for p in range(N_HEADS_C // 2):
        y = group(zq, p) * (DIFF_QK ** -0.5 * LOG2E)
        qc_ref[0, p] = y.astype(BF16)
        note(y)
        k_groups.append(group(zk, p))
        kt = group(zk, p).T
        for sub in range(4):
            mine = (row >= sub * DIFF_QK) & (row < (sub + 1) * DIFF_QK)
            ktc_ref[0, 4 * p + sub, 0] = jnp.where(mine, kt, 0.0).astype(BF16)
        vz = group(zv, p)
        vc_ref[0, 2 * p] = jnp.where(low, vz, 1.0).astype(BF16)
        vc_ref[0, 2 * p + 1] = jnp.where(low, 1.0, vz).astype(BF16)
    for kz in k_groups:
        note(kz)
    n2 = n2 + noted_row_sums(n_noted_b)
    nrm_ref[0] = jnp.max(n2, axis=0, keepdims=True)

    cos_a, sin_a = cos_a_ref[...], sin_a_ref[...]
    nf_a = HEAD_DIM // 4
    zq = seg(OFF_QA, N_HEADS_A * HEAD_DIM)
    for p in range(N_HEADS_A // 2):
        y = rope(_half_rms(group(zq, p)) * qg_a_ref[...], cos_a, sin_a, nf_a)
        qa_ref[0, p] = (y * (HEAD_DIM ** -0.5 * LOG2E)).astype(BF16)
    y = rope(_half_rms(seg(OFF_KA, LANES)) * kg_a_ref[...], cos_a, sin_a, nf_a)
    kt = y.T.astype(BF16)
    zeros = jnp.zeros((HALF, tm), BF16)
    for g in range(N_KV_A):
        kg = kt[g * HALF:(g + 1) * HALF]
        kta_ref[0, 2 * g] = jnp.concatenate([kg, zeros], axis=0)
        kta_ref[0, 2 * g + 1] = jnp.concatenate([zeros, kg], axis=0)
    zv = seg(OFF_VA, LANES)
    va_ref[0, 0] = jnp.where(low, zv, 1.0).astype(BF16)
    va_ref[0, 1] = jnp.where(low, 1.0, zv).astype(BF16)


def _norm_routing():
    sel = np.zeros((NRM_GROUPS, LANES, LANES), np.float32)
    g = 0
    for slot in ([NRM_QB + h for h in range(N_HEADS_B)] + [NRM_KB + h for h in range(N_HEADS_B)]):
        sel[g, :, slot] = 1.0
        g += 1
    for p in range(N_HEADS_C // 2):
        for f in range(2):
            sel[g, f * HALF:(f + 1) * HALF, NRM_QC + 2 * p + f] = 1.0
        g += 1
    for p in range(N_HEADS_C // 2):
        for sub in range(4):
            sel[g, sub * DIFF_QK:(sub + 1) * DIFF_QK, NRM_KC + 4 * p + sub] = 1.0
        g += 1
    assert g == NRM_GROUPS
    return jnp.asarray(sel.reshape(NRM_GROUPS * LANES, LANES), BF16)


def _in_proj(x2d, gat, w_in_r, tabs, qg_a, kg_a, qg_b, kvg_b, wuq_p, wukv_p, B, S):
    T = B * S
    tm = TM_IN
    nst = S // tm
    const = lambda shape: pl.BlockSpec(shape, lambda i: (0,) * len(shape))
    tab = pl.BlockSpec((tm, LANES), lambda i: (i % nst, 0))
    hm = lambda H: pl.BlockSpec((1, H, tm, LANES), lambda i: (i // nst, 0, i % nst, 0))
    hmt = lambda H: pl.BlockSpec((1, H, LANES, tm), lambda i: (i // nst, 0, 0, i % nst))
    sds = lambda H: jax.ShapeDtypeStruct((B, H, S, LANES), BF16)
    sdt = lambda H: jax.ShapeDtypeStruct((B, H, LANES, S), BF16)
    return pl.pallas_call(
        _in_kernel,
        grid=(T // tm,),
        in_specs=[pl.BlockSpec((tm, D_MODEL), lambda i: (i, 0)),
                  const((1, D_MODEL)), const((D_MODEL, IN_COLS)),
                  tab, tab, tab, tab,
                  const((1, LANES)), const((1, LANES)), const((1, CQ_PAD)), const((1, MLA_KV_RANK)),
                  const((CQ_PAD, N_HEADS_B * LANES)), const((MLA_KV_RANK, 2 * N_HEADS_B * LANES)),
                  const((NRM_GROUPS * LANES, LANES))],
        out_specs=[hm(N_HEADS_A // 2), hmt(2 * N_KV_A), hm(N_KV_A),
                   hm(N_HEADS_B), hmt(N_HEADS_B), hm(N_HEADS_B),
                   hm(N_HEADS_C // 2),
                   pl.BlockSpec((1, 2 * N_HEADS_C, 1, LANES, tm),
                                lambda i: (i // nst, 0, i % nst, 0, 0)),
                   hm(N_HEADS_C),
                   pl.BlockSpec((1, 1, LANES), lambda i: (i, 0, 0))],
        out_shape=[sds(N_HEADS_A // 2), sdt(2 * N_KV_A), sds(N_KV_A),
                   sds(N_HEADS_B), sdt(N_HEADS_B), sds(N_HEADS_B),
                   sds(N_HEADS_C // 2),
                   jax.ShapeDtypeStruct((B, 2 * N_HEADS_C, S // tm, LANES, tm), BF16),
                   sds(N_HEADS_C),
                   jax.ShapeDtypeStruct((T // tm, 1, LANES), F32)],
        compiler_params=pltpu.CompilerParams(dimension_semantics=("arbitrary",),
                                             vmem_limit_bytes=VMEM_LIMIT_BYTES),
        name="in_proj",
    )(x2d, gat, w_in_r, *tabs, qg_a, kg_a, qg_b, kvg_b, wuq_p, wukv_p, _norm_routing())


def _lane_group_max(s):
    m = s[:, :LANES]
    for j in range(1, s.shape[1] // LANES):
        m = jnp.maximum(m, s[:, j * LANES:(j + 1) * LANES])
    return m


def _pv_bounded(streams, n_chunks):
    items = [(r, i) for r in range(n_chunks) for i in range(len(streams))]
    accs = [None] * len(streams)
    in_flight = {}
    for k in range(len(items) + SCORE_LOOKAHEAD):
        if k < len(items):
            r, i = items[k]
            scores, _, row_off = streams[i]
            s, off = scores(r), row_off(r)
            in_flight[k] = s if off is None else s + off
        if k >= SCORE_LOOKAHEAD:
            r, i = items[k - SCORE_LOOKAHEAD]
            d = _dot(jnp.exp2(in_flight.pop(k - SCORE_LOOKAHEAD)).astype(BF16), streams[i][1](r))
            accs[i] = d if accs[i] is None else accs[i] + d
    return accs


def _pv_exact(streams, n_chunks, s_ref):
    maxes = []
    for i, (scores, _, row_off) in enumerate(streams):
        m_run = None
        for r in range(n_chunks):
            s = scores(r)
            s_ref[i * TQ_SUB:(i + 1) * TQ_SUB, r * TK:(r + 1) * TK] = s
            m, off = _lane_group_max(s), row_off(r)
            if off is not None:
                m = m + off
            m_run = m if m_run is None else jnp.maximum(m_run, m)
        maxes.append(jnp.max(m_run, axis=-1, keepdims=True))
    accs = []
    for i, (_, values, row_off) in enumerate(streams):
        acc = None
        for r in range(n_chunks):
            off = row_off(r)
            shift = maxes[i] if off is None else maxes[i] - off
            p = jnp.exp2(s_ref[i * TQ_SUB:(i + 1) * TQ_SUB, r * TK:(r + 1) * TK] - shift)
            d = _dot(p.astype(BF16), values(r))
            acc = d if acc is None else acc + d
        accs.append(acc)
    return accs


def _run_heads(bounded, heads, head_streams, n_chunks, s_ref, acc_ref):
    n_sub = TQ // TQ_SUB

    @pl.when(bounded)
    def _():
        accs = _pv_bounded([s for h in heads for s in head_streams(h)], n_chunks)
        for k, h in enumerate(heads):
            acc_ref[h] = jnp.concatenate(accs[k * n_sub:(k + 1) * n_sub], axis=0)

    @pl.when(jnp.logical_not(bounded))
    def _():
        for h in heads:
            acc_ref[h] = jnp.concatenate(_pv_exact(head_streams(h), n_chunks, s_ref), axis=0)


def _normalised_pair(acc_low, acc_high):
    low = _low_lanes()
    num = jnp.where(low, acc_low, acc_high)
    den = pltpu.roll(jnp.where(low, acc_high, acc_low), HALF, axis=1)
    return num / den


def _attn_ab_kernel(bounded_ref, q_ref, kt_ref, v_ref, o_ref, s_ref, acc_ref, *,
                    n_heads, q_index, kt_index, v_index, out_pairs, n_keys):
    n_chunks = n_keys // TK

    def head_streams(h):
        q = q_ref[0, q_index(h)]
        return [(lambda r, qs=q[r0:r0 + TQ_SUB]:
                 _dot(qs, kt_ref[0, kt_index(h), :, r * TK:(r + 1) * TK]),
                 lambda r: v_ref[0, v_index(h), r * TK:(r + 1) * TK, :],
                 lambda r: None) for r0 in range(0, TQ, TQ_SUB)]

    def head_group(j, carry):
        heads = [HEADS_PER_PIPELINE * j + k for k in range(HEADS_PER_PIPELINE)]
        _run_heads(bounded_ref[pl.program_id(0), j] != 0, heads, head_streams, n_chunks, s_ref,
                   acc_ref)
        return carry

    lax.fori_loop(0, n_heads // HEADS_PER_PIPELINE, head_group, 0)
    o_ref[...] = jnp.concatenate([_normalised_pair(acc_ref[a], acc_ref[b]) for a, b in out_pairs],
                                 axis=-1).astype(BF16)


def _attn_c_kernel(bounded_ref, q_ref, kt_ref, v_ref, lq1_ref, lk1_ref, lq2_ref, lk2_ref,
                   subln_ref, o_ref, s_ref, acc_ref, *, n_keys, lam_init):
    assert TQ == TK
    nc = n_keys // TK
    cd = pl.program_id(1)
    col = lax.broadcasted_iota(jnp.int32, (1, TK), 1).astype(F32)
    row = lax.broadcasted_iota(jnp.int32, (TQ_SUB, 1), 0).astype(F32)
    subs = [(r, r + TQ_SUB) for r in range(0, TQ, TQ_SUB)]
    neg_dist = [-jnp.abs((lax.broadcasted_iota(jnp.int32, (TQ_SUB, TK), 0) + r0
                          - lax.broadcasted_iota(jnp.int32, (TQ_SUB, TK), 1)).astype(F32))
                for r0, _ in subs]

    chunk = [(cd + r) % nc for r in range(nc)]

    def head_bias(hh):
        slope = jnp.float32(LOG2E * 2.0 ** (-8.0 * N_HEADS_C / N_HEADS_C))
        for k in range(N_HEADS_C - 2, -1, -1):
            slope = jnp.where(hh == k, LOG2E * 2.0 ** (-8.0 * (k + 1) / N_HEADS_C), slope)
        key_term, row_off = [None], [[None] * len(subs)]
        for r in range(1, nc):
            side = jnp.where(chunk[r] < cd, slope, -slope)
            key_term.append(col * side)
            base = -slope * (jnp.abs(chunk[r] - cd) * TK).astype(F32)
            row_off.append([base - side * (row + float(r0)) for r0, _ in subs])
        return slope, key_term, row_off

    def head_pair(p, carry):
        bias = [head_bias(2 * p + f) for f in range(2)]

        def streams_of(k):
            f, idx, hh = k // 2, 4 * p + k, 2 * p + k // 2
            slope, key_term, row_off = bias[f]

            def scores(r, si):
                s = _dot(q_ref[0, p][subs[si][0]:subs[si][1]], kt_ref[0, idx, chunk[r]])
                return s + (neg_dist[si] * slope if r == 0 else key_term[r])

            return [(functools.partial(scores, si=si),
                     lambda r: v_ref[0, hh, pl.ds(pl.multiple_of(chunk[r] * TK, TK), TK), :],
                     lambda r, si=si: row_off[r][si]) for si in range(len(subs))]

        _run_heads(bounded_ref[pl.program_id(0), p] != 0, list(range(4)), streams_of, nc, s_ref,
                   acc_ref.at[pl.ds(4 * p, 4)])
        return carry

    lax.fori_loop(0, N_HEADS_C // 2, head_pair, 0)

    lam = (jnp.exp(jnp.sum(lq1_ref[...] * lk1_ref[...], axis=-1, keepdims=True))
           - jnp.exp(jnp.sum(lq2_ref[...] * lk2_ref[...], axis=-1, keepdims=True)) + lam_init)
    outs = []
    for p in range(N_HEADS_C // 2):
        o = (_normalised_pair(acc_ref[4 * p], acc_ref[4 * p + 2])
             - lam * _normalised_pair(acc_ref[4 * p + 1], acc_ref[4 * p + 3]))
        outs.append(_half_rms(o) * subln_ref[...] * (1.0 - lam_init))
    o_ref[...] = jnp.concatenate(outs, axis=-1).astype(BF16)


def _attention(kernel_fn, bounded, q, kt, v, extras, out_width, B, S):
    nq = S // TQ
    hq, hv = q.shape[1], v.shape[1]
    extra_specs = [pl.BlockSpec(e.shape, lambda b, i, f, n=e.ndim: (0,) * n) for e in extras]
    return pl.pallas_call(
        kernel_fn,
        grid_spec=pltpu.PrefetchScalarGridSpec(
            num_scalar_prefetch=1,
            grid=(B, nq),
            in_specs=[pl.BlockSpec((1, hq, TQ, LANES), lambda b, i, f: (b, 0, i, 0)),
                      pl.BlockSpec((1,) + kt.shape[1:],
                                   lambda b, i, f, n=kt.ndim: (b,) + (0,) * (n - 1)),
                      pl.BlockSpec((1, hv, S, LANES), lambda b, i, f: (b, 0, 0, 0))] + extra_specs,
            out_specs=pl.BlockSpec((TQ, out_width), lambda b, i, f: (b * nq + i, 0)),
            scratch_shapes=[pltpu.VMEM((TQ, S), F32),
                            pltpu.VMEM((HEADS_PER_PIPELINE * bounded.shape[1], TQ, LANES), F32)]),
        out_shape=jax.ShapeDtypeStruct((B * S, out_width), BF16),
        compiler_params=pltpu.CompilerParams(dimension_semantics=("arbitrary", "arbitrary"),
                                             vmem_limit_bytes=VMEM_LIMIT_BYTES),
        name=kernel_fn.func.__name__.strip("_"),
    )(bounded, q, kt, v, *extras)


def _out_kernel(x_ref, oa_ref, ob_ref, oc_ref, wo_ref, g_ref, x1_ref, h2_ref):
    na = N_HEADS_A * HEAD_DIM
    nb = na + N_HEADS_B * MLA_V
    y = (x_ref[...] + _dot(oa_ref[...], wo_ref[:na]) + _dot(ob_ref[...], wo_ref[na:nb])
         + _dot(oc_ref[...], wo_ref[nb:]))
    x1_ref[...] = y
    h2_ref[...] = (_rms(y, D_MODEL) * g_ref[...]).astype(BF16)


def _out_proj(x2d, oa, ob, oc, w_out, g_ffn):
    T = x2d.shape[0]
    tm = TM_OUT
    rows = lambda w: pl.BlockSpec((tm, w), lambda i: (i, 0))
    return pl.pallas_call(
        _out_kernel,
        grid=(T // tm,),
        in_specs=[rows(D_MODEL), rows(oa.shape[1]), rows(ob.shape[1]), rows(oc.shape[1]),
                  pl.BlockSpec(w_out.shape, lambda i: (0, 0)),
                  pl.BlockSpec((1, D_MODEL), lambda i: (0, 0))],
        out_specs=[rows(D_MODEL), rows(D_MODEL)],
        out_shape=[jax.ShapeDtypeStruct((T, D_MODEL), F32),
                   jax.ShapeDtypeStruct((T, D_MODEL), BF16)],
        compiler_params=pltpu.CompilerParams(dimension_semantics=("arbitrary",),
                                             vmem_limit_bytes=VMEM_LIMIT_BYTES),
        name="out_proj",
    )(x2d, oa, ob, oc, w_out, g_ffn)


def _ffn_kernel(hp_ref, h_ref, hn_ref, x1_ref, wu_ref, cw_ref, cb_ref, wd_ref, fg_ref,
                o_ref, act_ref, *, tiles_per_seq, final_norm):
    i = pl.program_id(0)
    tm = h_ref.shape[0]
    halo = hp_ref.shape[0]
    at_start = (i % tiles_per_seq) == 0
    at_end = (i % tiles_per_seq) == tiles_per_seq - 1
    hp = jnp.where(at_start, jnp.zeros_like(hp_ref[...]), hp_ref[...])
    hn = jnp.where(at_end, jnp.zeros_like(hn_ref[...]), hn_ref[...])
    hext = jnp.concatenate([hp, h_ref[...], hn], axis=0)

    def branch(k, c0):
        u = _dot(hext, wu_ref[:, k * D_FF + c0:k * D_FF + c0 + FFN_CHUNK])
        cw = cw_ref[k, :, c0:c0 + FFN_CHUNK]
        return (u[halo - 1:halo - 1 + tm] * cw[0:1] + u[halo:halo + tm] * cw[1:2]
                + u[halo + 1:halo + 1 + tm] * cw[2:3] + cb_ref[k, :, c0:c0 + FFN_CHUNK])

    for c0 in range(0, D_FF, FFN_CHUNK):
        g = branch(0, c0)
        val = branch(1, c0)
        act_ref[:, c0:c0 + FFN_CHUNK] = (g / (1.0 + jnp.exp(-g)) * val).astype(BF16)

    y = x1_ref[...] + _dot(act_ref[...], wd_ref[...])
    if final_norm:
        y = _rms(y, D_MODEL) * fg_ref[...]
    o_ref[...] = y


def _ffn(h2, x1, w_up, conv_w, conv_b, w_down, final_g, S, final_norm):
    T = h2.shape[0]
    tm, halo = TM_FFN, BF16_SUBLANES
    per = tm // halo
    last = T // halo - 1
    resident = lambda shape: pl.BlockSpec(shape, lambda i: (0,) * len(shape),
                                          pipeline_mode=pl.Buffered(1))
    return pl.pallas_call(
        functools.partial(_ffn_kernel, tiles_per_seq=S // tm, final_norm=final_norm),
        grid=(T // tm,),
        in_specs=[pl.BlockSpec((halo, D_MODEL), lambda i: (jnp.maximum(i * per - 1, 0), 0)),
                  pl.BlockSpec((tm, D_MODEL), lambda i: (i, 0)),
                  pl.BlockSpec((halo, D_MODEL), lambda i: (jnp.minimum((i + 1) * per, last), 0)),
                  pl.BlockSpec((tm, D_MODEL), lambda i: (i, 0)),
                  resident((D_MODEL, 2 * D_FF)), resident((2, 3, D_FF)), resident((2, 1, D_FF)),
                  resident((D_FF, D_MODEL)), resident((1, D_MODEL))],
        out_specs=pl.BlockSpec((tm, D_MODEL), lambda i: (i, 0)),
        out_shape=jax.ShapeDtypeStruct((T, D_MODEL), F32),
        scratch_shapes=[pltpu.VMEM((tm, D_FF), BF16)],
        compiler_params=pltpu.CompilerParams(dimension_semantics=("arbitrary",),
                                             vmem_limit_bytes=VMEM_LIMIT_BYTES),
        name="conv_mlp",
    )(h2, h2, h2, x1, w_up, conv_w, conv_b, w_down, final_g)


def _pad_groups(w, n_groups, width, offset=0):
    r = w.shape[0]
    w = w.reshape(r, n_groups, width)
    w = jnp.pad(w, ((0, 0), (0, 0), (offset, LANES - width - offset)))
    return w.reshape(r, n_groups * LANES)


def _rope_core(S, dim):
    nf = dim // 4
    t = jnp.arange(S, dtype=jnp.int32)
    rows = (t // GRID_W).astype(F32)
    cols = (t % GRID_W).astype(F32)
    inv = ROPE_BASE ** (-jnp.arange(nf, dtype=F32) / nf)
    ar = rows[:, None] * inv
    ac = cols[:, None] * inv
    cos = jnp.concatenate([jnp.cos(ar), jnp.cos(ar), jnp.cos(ac), jnp.cos(ac)], axis=-1)
    sin = jnp.concatenate([-jnp.sin(ar), jnp.sin(ar), -jnp.sin(ac), jnp.sin(ac)], axis=-1)
    return cos, sin


def _rope_tables(S):
    cos_a, sin_a = _rope_core(S, HEAD_DIM)
    cos_b, sin_b = _rope_core(S, MLA_ROPE)
    pad = LANES - MLA_NOPE - MLA_ROPE
    return (jnp.tile(cos_a, (1, 2)), jnp.tile(sin_a, (1, 2)),
            jnp.concatenate([jnp.ones((S, MLA_NOPE), F32), cos_b, jnp.zeros((S, pad), F32)], -1),
            jnp.concatenate([jnp.zeros((S, MLA_NOPE), F32), sin_b, jnp.zeros((S, pad), F32)], -1))


def _prep_w_in(w):
    a_end = (N_HEADS_A + 2 * N_KV_A) * HEAD_DIM
    b_end = a_end + MLA_Q_RANK + MLA_KV_RANK + MLA_ROPE
    cq = w[:, a_end:a_end + MLA_Q_RANK]
    ckv = w[:, a_end + MLA_Q_RANK:a_end + MLA_Q_RANK + MLA_KV_RANK]
    kr = w[:, b_end - MLA_ROPE:b_end]
    out = jnp.concatenate([
        w[:, :a_end],
        jnp.pad(cq, ((0, 0), (0, CQ_PAD - MLA_Q_RANK))),
        ckv,
        jnp.pad(kr, ((0, 0), (MLA_NOPE, LANES - MLA_NOPE - MLA_ROPE))),
        w[:, b_end:],
    ], axis=-1).astype(BF16)
    assert out.shape[1] == IN_COLS
    return out


def _twice(v):
    return jnp.tile(v, 2).reshape(1, 2 * v.shape[0])


def kernel(x, norm_attn, w_in, q_norm_a, k_norm_a, q_a_norm_b, w_uq_b, kv_a_norm_b, w_ukv_b,
           lambda_q1_c, lambda_k1_c, lambda_q2_c, lambda_k2_c, subln_c, w_out,
           norm_ffn, w_up, conv_w, conv_b, w_down, final_norm):
    B, S, D = x.shape
    depth = w_in.shape[0]
    assert D == D_MODEL and S % max(TM_IN, TM_OUT, TM_FFN, TQ, TK) == 0 and S % GRID_W == 0
    assert TM_IN == TK
    T = B * S

    tabs = _rope_tables(S)
    row = lambda v: v.reshape(1, -1)
    a_pairs = [(j, j + N_HEADS_A // 2) for j in range(N_HEADS_A // 2)]
    a_rows = np.concatenate([np.arange(h * HEAD_DIM, (h + 1) * HEAD_DIM)
                             for pair in a_pairs for h in pair])
    xc = x.reshape(T, D)
    for l in range(depth):
        lam_init = 0.8 - 0.6 * math.exp(-0.3 * l)
        wuq_p = jnp.pad(_pad_groups(w_uq_b[l], N_HEADS_B, MLA_NOPE + MLA_ROPE),
                        ((0, CQ_PAD - MLA_Q_RANK), (0, 0))).astype(BF16)
        wukv = w_ukv_b[l].reshape(MLA_KV_RANK, N_HEADS_B, MLA_NOPE + MLA_V)
        v_cols = [_pad_groups(wukv[:, h, MLA_NOPE:], 1, MLA_V, offset=HALF * (h % 2))
                  for h in range(N_HEADS_B)]
        wukv_p = jnp.concatenate(
            [_pad_groups(wukv[:, :, :MLA_NOPE].reshape(MLA_KV_RANK, -1), N_HEADS_B, MLA_NOPE)]
            + v_cols, axis=-1).astype(BF16)

        qa, kta, va, qb, ktb, vb, qc, ktc, vc, nrm = _in_proj(
            xc, row(norm_attn[l]), _prep_w_in(w_in[l]), tabs,
            _twice(q_norm_a[l]), _twice(k_norm_a[l]),
            jnp.pad(q_a_norm_b[l], (0, CQ_PAD - MLA_Q_RANK)).reshape(1, CQ_PAD),
            row(kv_a_norm_b[l]), wuq_p, wukv_p, B, S)

        nrm = jnp.max(nrm.reshape(B, -1, LANES), axis=1)

        def bounded(q0, nq, q_rep, k0, nk, k_rep):
            q2 = jnp.repeat(nrm[:, q0:q0 + nq], q_rep, axis=1)
            k2 = jnp.repeat(nrm[:, k0:k0 + nk], k_rep, axis=1)
            ok = q2 * k2 * NRM_MARGIN <= SAFE_LOG2_RANGE ** 2
            return jnp.all(ok.reshape(B, -1, HEADS_PER_PIPELINE), axis=-1).astype(jnp.int32)

        qk_a = (HEAD_DIM * jnp.max(jnp.abs(q_norm_a[l])) * jnp.max(jnp.abs(k_norm_a[l]))
                * (HEAD_DIM ** -0.5 * LOG2E))
        bounded_a = jnp.broadcast_to(
            (qk_a * NRM_MARGIN <= SAFE_LOG2_RANGE).astype(jnp.int32),
            (B, N_HEADS_A // HEADS_PER_PIPELINE))

        group = N_HEADS_A // N_KV_A
        oa = _attention(
            functools.partial(_attn_ab_kernel, n_heads=N_HEADS_A, q_index=lambda h: h // 2,
                              kt_index=lambda h: 2 * (h // group) + h % 2,
                              v_index=lambda h: h // group, out_pairs=a_pairs, n_keys=S),
            bounded_a, qa, kta, va, [], N_HEADS_A * HEAD_DIM, B, S)
        same = lambda h: h
        ob = _attention(
            functools.partial(_attn_ab_kernel, n_heads=N_HEADS_B, q_index=same, kt_index=same,
                              v_index=same,
                              out_pairs=[(2 * j, 2 * j + 1) for j in range(N_HEADS_B // 2)],
                              n_keys=S),
            bounded(NRM_QB, N_HEADS_B, 1, NRM_KB, N_HEADS_B, 1),
            qb, ktb, vb, [], N_HEADS_B * MLA_V, B, S)
        oc = _attention(
            functools.partial(_attn_c_kernel, n_keys=S, lam_init=lam_init),
            bounded(NRM_QC, N_HEADS_C, 2, NRM_KC, 2 * N_HEADS_C, 1),
            qc, ktc, vc,
            [row(lambda_q1_c[l]), row(lambda_k1_c[l]), row(lambda_q2_c[l]), row(lambda_k2_c[l]),
             _twice(subln_c[l])],
            N_HEADS_C * DIFF_V, B, S)

        w_out_l = jnp.concatenate([w_out[l][a_rows], w_out[l][N_HEADS_A * HEAD_DIM:]], axis=0)
        x1, h2 = _out_proj(xc, oa, ob, oc, w_out_l.astype(BF16), row(norm_ffn[l]))
        xc = _ffn(h2, x1, w_up[l].astype(BF16),
                  conv_w[l].reshape(3, 2, D_FF).transpose(1, 0, 2),
                  conv_b[l].reshape(2, 1, D_FF), w_down[l].astype(BF16),
                  row(final_norm), S, final_norm=(l == depth - 1))
    return xc.reshape(B, S, D)
```

```python
import functools
import math

import jax
import jax.numpy as jnp
import numpy as np
from jax import lax
from jax.experimental import pallas as pl
from jax.experimental.pallas import tpu as pltpu

D_MODEL = 1024
GRID_W = 64
HEAD_DIM = 64
N_HEADS_A = 8
N_KV_A = 2
N_HEADS_B = 4
MLA_Q_RANK = 192
MLA_KV_RANK = 128
MLA_NOPE = 64
MLA_ROPE = 32
MLA_V = 64
N_HEADS_C = 4
DIFF_QK = 32
DIFF_V = 64
D_FF = 2816
ROPE_BASE = 10000.0
EPS = 1e-6

LANES = 128
HALF = LANES // 2
BF16_SUBLANES = 16
VMEM_LIMIT_BYTES = 56 * 1024 * 1024

LOG2E = math.log2(math.e)
assert HEAD_DIM == MLA_V == DIFF_V == 2 * DIFF_QK == HALF

OFF_QA = 0
OFF_KA = OFF_QA + N_HEADS_A * HEAD_DIM
OFF_VA = OFF_KA + LANES
OFF_CQ = OFF_VA + LANES
CQ_PAD = 2 * LANES
OFF_CKV = OFF_CQ + CQ_PAD
OFF_KR = OFF_CKV + MLA_KV_RANK
OFF_QC = OFF_KR + LANES
OFF_KC = OFF_QC + N_HEADS_C * HALF
OFF_VC = OFF_KC + N_HEADS_C * HALF
IN_COLS = OFF_VC + N_HEADS_C * HALF

NRM_QB = 0
NRM_KB = NRM_QB + N_HEADS_B
NRM_QC = NRM_KB + N_HEADS_B
NRM_KC = NRM_QC + N_HEADS_C
NRM_GROUPS = 2 * N_HEADS_B + N_HEADS_C
NRM_MARGIN = 1.03
SAFE_LOG2_RANGE = 55.0

F32 = jnp.float32
BF16 = jnp.bfloat16

TM_IN = 512
TM_OUT = 512
TM_FFN = 512
FFN_CHUNK = 256
TQ = 512
TQ_SUB = 256
TK = 512
SCORE_LOOKAHEAD = 2
HEADS_PER_PIPELINE = 4


def _dot(a, b):
    return jnp.dot(a, b, preferred_element_type=F32)


def _rms(x, n):
    ms = jnp.sum(x * x, axis=-1, keepdims=True) * (1.0 / n)
    return x * lax.rsqrt(ms + EPS)


def _low_lanes():
    return lax.broadcasted_iota(jnp.int32, (1, LANES), 1) < HALF


def _half_rms(x):
    low = _low_lanes()
    sq = x * x
    s_lo = jnp.sum(jnp.where(low, sq, 0.0), axis=-1, keepdims=True)
    s_hi = jnp.sum(jnp.where(low, 0.0, sq), axis=-1, keepdims=True)
    return x * lax.rsqrt(jnp.where(low, s_lo, s_hi) * (1.0 / HALF) + EPS)


def _in_kernel(x_ref, gat_ref, w_ref, cos_a_ref, sin_a_ref, cos_b_ref, sin_b_ref,
               qg_a_ref, kg_a_ref, qg_b_ref, kvg_b_ref, wuq_ref, wukv_ref, sel_ref,
               qa_ref, kta_ref, va_ref, qb_ref, ktb_ref, vb_ref, qc_ref, ktc_ref, vc_ref, nrm_ref):
    x = x_ref[...]
    tm = x.shape[0]
    h = (_rms(x, D_MODEL) * gat_ref[...]).astype(BF16)

    lane = lax.broadcasted_iota(jnp.int32, (1, LANES), 1)
    low = lane < HALF
    row = lax.broadcasted_iota(jnp.int32, (LANES, 1), 0)

    z = _dot(h, w_ref[...])

    def seg(off, width):
        return z[:, off:off + width]

    def swap(y, nf):
        up = pltpu.roll(y, LANES - nf, axis=1)
        down = pltpu.roll(y, nf, axis=1)
        return jnp.where((lane & nf) == 0, up, down)

    def rope(y, c, s, nf):
        return y * c + swap(y, nf) * s

    def group(z, g):
        return z[:, g * LANES:(g + 1) * LANES]

    squares = []

    def note(y):
        squares.append((y * y).astype(BF16))

    def noted_row_sums(first):
        return _dot(jnp.concatenate(squares[first:], axis=-1),
                    sel_ref[first * LANES:len(squares) * LANES])

    cos_b, sin_b = cos_b_ref[...], sin_b_ref[...]
    nf_b = MLA_ROPE // 4
    cq = (_rms(seg(OFF_CQ, CQ_PAD), MLA_Q_RANK) * qg_b_ref[...]).astype(BF16)
    qb = _dot(cq, wuq_ref[...])
    scale_b = (MLA_NOPE + MLA_ROPE) ** -0.5 * LOG2E
    for hh in range(N_HEADS_B):
        y = rope(group(qb, hh), cos_b, sin_b, nf_b) * scale_b
        qb_ref[0, hh] = y.astype(BF16)
        note(y)
    ckv = (_rms(seg(OFF_CKV, MLA_KV_RANK), MLA_KV_RANK) * kvg_b_ref[...]).astype(BF16)
    kvb = _dot(ckv, wukv_ref[...])
    kr = rope(seg(OFF_KR, LANES), cos_b, sin_b, nf_b)
    for hh in range(N_HEADS_B):
        y = group(kvb, hh) + kr
        ktb_ref[0, hh] = y.T.astype(BF16)
        note(y)
        ones_other_half = jnp.where(low, float(hh % 2), float(1 - hh % 2))
        vb_ref[0, hh] = (group(kvb, N_HEADS_B + hh) + ones_other_half).astype(BF16)
    n2 = noted_row_sums(0)
    n_noted_b = len(squares)

    zq = seg(OFF_QC, N_HEADS_C * HALF)
    zk = seg(OFF_KC, N_HEADS_C * HALF)
    zv = seg(OFF_VC, N_HEADS_C * HALF)
    k_groups = []
    for p in range(N_HEADS_C // 2):
        y = group(zq, p) * (DIFF_QK ** -0.5 * LOG2E)
        qc_ref[0, p] = y.astype(BF16)
        note(y)
        k_groups.append(group(zk, p))
        kt = group(zk, p).T
        for sub in range(4):
            mine = (row >= sub * DIFF_QK) & (row < (sub + 1) * DIFF_QK)
            ktc_ref[0, 4 * p + sub, 0] = jnp.where(mine, kt, 0.0).astype(BF16)
        vz = group(zv, p)
        vc_ref[0, 2 * p] = jnp.where(low, vz, 1.0).astype(BF16)
        vc_ref[0, 2 * p + 1] = jnp.where(low, 1.0, vz).astype(BF16)
    for kz in k_groups:
        note(kz)
    n2 = n2 + noted_row_sums(n_noted_b)
    nrm_ref[0] = jnp.max(n2, axis=0, keepdims=True)

    cos_a, sin_a = cos_a_ref[...], sin_a_ref[...]
    nf_a = HEAD_DIM // 4
    zq = seg(OFF_QA, N_HEADS_A * HEAD_DIM)
    for p in range(N_HEADS_A // 2):
        y = rope(_half_rms(group(zq, p)) * qg_a_ref[...], cos_a, sin_a, nf_a)
        qa_ref[0, p] = (y * (HEAD_DIM ** -0.5 * LOG2E)).astype(BF16)
    y = rope(_half_rms(seg(OFF_KA, LANES)) * kg_a_ref[...], cos_a, sin_a, nf_a)
    kt = y.T.astype(BF16)
    zeros = jnp.zeros((HALF, tm), BF16)
    for g in range(N_KV_A):
        kg = kt[g * HALF:(g + 1) * HALF]
        kta_ref[0, 2 * g] = jnp.concatenate([kg, zeros], axis=0)
        kta_ref[0, 2 * g + 1] = jnp.concatenate([zeros, kg], axis=0)
    zv = seg(OFF_VA, LANES)
    va_ref[0, 0] = jnp.where(low, zv, 1.0).astype(BF16)
    va_ref[0, 1] = jnp.where(low, 1.0, zv).astype(BF16)


def _norm_routing():
    sel = np.zeros((NRM_GROUPS, LANES, LANES), np.float32)
    g = 0
    for slot in ([NRM_QB + h for h in range(N_HEADS_B)] + [NRM_KB + h for h in range(N_HEADS_B)]):
        sel[g, :, slot] = 1.0
        g += 1
    for p in range(N_HEADS_C // 2):
        for f in range(2):
            sel[g, f * HALF:(f + 1) * HALF, NRM_QC + 2 * p + f] = 1.0
        g += 1
    for p in range(N_HEADS_C // 2):
        for sub in range(4):
            sel[g, sub * DIFF_QK:(sub + 1) * DIFF_QK, NRM_KC + 4 * p + sub] = 1.0
        g += 1
    assert g == NRM_GROUPS
    return jnp.asarray(sel.reshape(NRM_GROUPS * LANES, LANES), BF16)


def _in_proj(x2d, gat, w_in_r, tabs, qg_a, kg_a, qg_b, kvg_b, wuq_p, wukv_p, B, S):
    T = B * S
    tm = TM_IN
    nst = S // tm
    const = lambda shape: pl.BlockSpec(shape, lambda i: (0,) * len(shape))
    tab = pl.BlockSpec((tm, LANES), lambda i: (i % nst, 0))
    hm = lambda H: pl.BlockSpec((1, H, tm, LANES), lambda i: (i // nst, 0, i % nst, 0))
    hmt = lambda H: pl.BlockSpec((1, H, LANES, tm), lambda i: (i // nst, 0, 0, i % nst))
    sds = lambda H: jax.ShapeDtypeStruct((B, H, S, LANES), BF16)
    sdt = lambda H: jax.ShapeDtypeStruct((B, H, LANES, S), BF16)
    return pl.pallas_call(
        _in_kernel,
        grid=(T // tm,),
        in_specs=[pl.BlockSpec((tm, D_MODEL), lambda i: (i, 0)),
                  const((1, D_MODEL)), const((D_MODEL, IN_COLS)),
                  tab, tab, tab, tab,
                  const((1, LANES)), const((1, LANES)), const((1, CQ_PAD)), const((1, MLA_KV_RANK)),
                  const((CQ_PAD, N_HEADS_B * LANES)), const((MLA_KV_RANK, 2 * N_HEADS_B * LANES)),
                  const((NRM_GROUPS * LANES, LANES))],
        out_specs=[hm(N_HEADS_A // 2), hmt(2 * N_KV_A), hm(N_KV_A),
                   hm(N_HEADS_B), hmt(N_HEADS_B), hm(N_HEADS_B),
                   hm(N_HEADS_C // 2),
                   pl.BlockSpec((1, 2 * N_HEADS_C, 1, LANES, tm),
                                lambda i: (i // nst, 0, i % nst, 0, 0)),
                   hm(N_HEADS_C),
                   pl.BlockSpec((1, 1, LANES), lambda i: (i, 0, 0))],
        out_shape=[sds(N_HEADS_A // 2), sdt(2 * N_KV_A), sds(N_KV_A),
                   sds(N_HEADS_B), sdt(N_HEADS_B), sds(N_HEADS_B),
                   sds(N_HEADS_C // 2),
                   jax.ShapeDtypeStruct((B, 2 * N_HEADS_C, S // tm, LANES, tm), BF16),
                   sds(N_HEADS_C),
                   jax.ShapeDtypeStruct((T // tm, 1, LANES), F32)],
        compiler_params=pltpu.CompilerParams(dimension_semantics=("arbitrary",),
                                             vmem_limit_bytes=VMEM_LIMIT_BYTES),
        name="in_proj",
    )(x2d, gat, w_in_r, *tabs, qg_a, kg_a, qg_b, kvg_b, wuq_p, wukv_p, _norm_routing())


def _lane_group_max(s):
    m = s[:, :LANES]
    for j in range(1, s.shape[1] // LANES):
        m = jnp.maximum(m, s[:, j * LANES:(j + 1) * LANES])
    return m


def _pv_bounded(streams, n_chunks):
    items = [(r, i) for r in range(n_chunks) for i in range(len(streams))]
    accs = [None] * len(streams)
    in_flight = {}
    for k in range(len(items) + SCORE_LOOKAHEAD):
        if k < len(items):
            r, i = items[k]
            scores, _, row_off = streams[i]
            s, off = scores(r), row_off(r)
            in_flight[k] = s if off is None else s + off
        if k >= SCORE_LOOKAHEAD:
            r, i = items[k - SCORE_LOOKAHEAD]
            d = _dot(jnp.exp2(in_flight.pop(k - SCORE_LOOKAHEAD)).astype(BF16), streams[i][1](r))
            accs[i] = d if accs[i] is None else accs[i] + d
    return accs


def _pv_exact(streams, n_chunks, s_ref):
    maxes = []
    for i, (scores, _, row_off) in enumerate(streams):
        m_run = None
        for r in range(n_chunks):
            s = scores(r)
            s_ref[i * TQ_SUB:(i + 1) * TQ_SUB, r * TK:(r + 1) * TK] = s
            m, off = _lane_group_max(s), row_off(r)
            if off is not None:
                m = m + off
            m_run = m if m_run is None else jnp.maximum(m_run, m)
        maxes.append(jnp.max(m_run, axis=-1, keepdims=True))
    accs = []
    for i, (_, values, row_off) in enumerate(streams):
        acc = None
        for r in range(n_chunks):
            off = row_off(r)
            shift = maxes[i] if off is None else maxes[i] - off
            p = jnp.exp2(s_ref[i * TQ_SUB:(i + 1) * TQ_SUB, r * TK:(r + 1) * TK] - shift)
            d = _dot(p.astype(BF16), values(r))
            acc = d if acc is None else acc + d
        accs.append(acc)
    return accs


def _run_heads(bounded, heads, head_streams, n_chunks, s_ref, acc_ref):
    n_sub = TQ // TQ_SUB

    @pl.when(bounded)
    def _():
        accs = _pv_bounded([s for h in heads for s in head_streams(h)], n_chunks)
        for k, h in enumerate(heads):
            acc_ref[h] = jnp.concatenate(accs[k * n_sub:(k + 1) * n_sub], axis=0)

    @pl.when(jnp.logical_not(bounded))
    def _():
        for h in heads:
            acc_ref[h] = jnp.concatenate(_pv_exact(head_streams(h), n_chunks, s_ref), axis=0)


def _normalised_pair(acc_low, acc_high):
    low = _low_lanes()
    num = jnp.where(low, acc_low, acc_high)
    den = pltpu.roll(jnp.where(low, acc_high, acc_low), HALF, axis=1)
    return num / den


def _attn_ab_kernel(bounded_ref, q_ref, kt_ref, v_ref, o_ref, s_ref, acc_ref, *,
                    n_heads, q_index, kt_index, v_index, out_pairs, n_keys):
    n_chunks = n_keys // TK

    def head_streams(h):
        q = q_ref[0, q_index(h)]
        return [(lambda r, qs=q[r0:r0 + TQ_SUB]:
                 _dot(qs, kt_ref[0, kt_index(h), :, r * TK:(r + 1) * TK]),
                 lambda r: v_ref[0, v_index(h), r * TK:(r + 1) * TK, :],
                 lambda r: None) for r0 in range(0, TQ, TQ_SUB)]

    def head_group(j, carry):
        heads = [HEADS_PER_PIPELINE * j + k for k in range(HEADS_PER_PIPELINE)]
        _run_heads(bounded_ref[pl.program_id(0), j] != 0, heads, head_streams, n_chunks, s_ref,
                   acc_ref)
        return carry

    lax.fori_loop(0, n_heads // HEADS_PER_PIPELINE, head_group, 0)
    o_ref[...] = jnp.concatenate([_normalised_pair(acc_ref[a], acc_ref[b]) for a, b in out_pairs],
                                 axis=-1).astype(BF16)


def _attn_c_kernel(bounded_ref, q_ref, kt_ref, v_ref, lq1_ref, lk1_ref, lq2_ref, lk2_ref,
                   subln_ref, o_ref, s_ref, acc_ref, *, n_keys, lam_init):
    assert TQ == TK
    nc = n_keys // TK
    cd = pl.program_id(1)
    col = lax.broadcasted_iota(jnp.int32, (1, TK), 1).astype(F32)
    row = lax.broadcasted_iota(jnp.int32, (TQ_SUB, 1), 0).astype(F32)
    subs = [(r, r + TQ_SUB) for r in range(0, TQ, TQ_SUB)]
    neg_dist = [-jnp.abs((lax.broadcasted_iota(jnp.int32, (TQ_SUB, TK), 0) + r0
                          - lax.broadcasted_iota(jnp.int32, (TQ_SUB, TK), 1)).astype(F32))
                for r0, _ in subs]

    chunk = [(cd + r) % nc for r in range(nc)]

    def head_bias(hh):
        slope = jnp.float32(LOG2E * 2.0 ** (-8.0 * N_HEADS_C / N_HEADS_C))
        for k in range(N_HEADS_C - 2, -1, -1):
            slope = jnp.where(hh == k, LOG2E * 2.0 ** (-8.0 * (k + 1) / N_HEADS_C), slope)
        key_term, row_off = [None], [[None] * len(subs)]
        for r in range(1, nc):
            side = jnp.where(chunk[r] < cd, slope, -slope)
            key_term.append(col * side)
            base = -slope * (jnp.abs(chunk[r] - cd) * TK).astype(F32)
            row_off.append([base - side * (row + float(r0)) for r0, _ in subs])
        return slope, key_term, row_off

    def head_pair(p, carry):
        bias = [head_bias(2 * p + f) for f in range(2)]

        def streams_of(k):
            f, idx, hh = k // 2, 4 * p + k, 2 * p + k // 2
            slope, key_term, row_off = bias[f]

            def scores(r, si):
                s = _dot(q_ref[0, p][subs[si][0]:subs[si][1]], kt_ref[0, idx, chunk[r]])
                return s + (neg_dist[si] * slope if r == 0 else key_term[r])

            return [(functools.partial(scores, si=si),
                     lambda r: v_ref[0, hh, pl.ds(pl.multiple_of(chunk[r] * TK, TK), TK), :],
                     lambda r, si=si: row_off[r][si]) for si in range(len(subs))]

        _run_heads(bounded_ref[pl.program_id(0), p] != 0, list(range(4)), streams_of, nc, s_ref,
                   acc_ref.at[pl.ds(4 * p, 4)])
        return carry

    lax.fori_loop(0, N_HEADS_C // 2, head_pair, 0)

    lam = (jnp.exp(jnp.sum(lq1_ref[...] * lk1_ref[...], axis=-1, keepdims=True))
           - jnp.exp(jnp.sum(lq2_ref[...] * lk2_ref[...], axis=-1, keepdims=True)) + lam_init)
    outs = []
    for p in range(N_HEADS_C // 2):
        o = (_normalised_pair(acc_ref[4 * p], acc_ref[4 * p + 2])
             - lam * _normalised_pair(acc_ref[4 * p + 1], acc_ref[4 * p + 3]))
        outs.append(_half_rms(o) * subln_ref[...] * (1.0 - lam_init))
    o_ref[...] = jnp.concatenate(outs, axis=-1).astype(BF16)


def _attention(kernel_fn, bounded, q, kt, v, extras, out_width, B, S):
    nq = S // TQ
    hq, hv = q.shape[1], v.shape[1]
    extra_specs = [pl.BlockSpec(e.shape, lambda b, i, f, n=e.ndim: (0,) * n) for e in extras]
    return pl.pallas_call(
        kernel_fn,
        grid_spec=pltpu.PrefetchScalarGridSpec(
            num_scalar_prefetch=1,
            grid=(B, nq),
            in_specs=[pl.BlockSpec((1, hq, TQ, LANES), lambda b, i, f: (b, 0, i, 0)),
                      pl.BlockSpec((1,) + kt.shape[1:],
                                   lambda b, i, f, n=kt.ndim: (b,) + (0,) * (n - 1)),
                      pl.BlockSpec((1, hv, S, LANES), lambda b, i, f: (b, 0, 0, 0))] + extra_specs,
            out_specs=pl.BlockSpec((TQ, out_width), lambda b, i, f: (b * nq + i, 0)),
            scratch_shapes=[pltpu.VMEM((TQ, S), F32),
                            pltpu.VMEM((HEADS_PER_PIPELINE * bounded.shape[1], TQ, LANES), F32)]),
        out_shape=jax.ShapeDtypeStruct((B * S, out_width), BF16),
        compiler_params=pltpu.CompilerParams(dimension_semantics=("arbitrary", "arbitrary"),
                                             vmem_limit_bytes=VMEM_LIMIT_BYTES),
        name=kernel_fn.func.__name__.strip("_"),
    )(bounded, q, kt, v, *extras)


def _out_kernel(x_ref, oa_ref, ob_ref, oc_ref, wo_ref, g_ref, x1_ref, h2_ref):
    na = N_HEADS_A * HEAD_DIM
    nb = na + N_HEADS_B * MLA_V
    y = (x_ref[...] + _dot(oa_ref[...], wo_ref[:na]) + _dot(ob_ref[...], wo_ref[na:nb])
         + _dot(oc_ref[...], wo_ref[nb:]))
    x1_ref[...] = y
    h2_ref[...] = (_rms(y, D_MODEL) * g_ref[...]).astype(BF16)


def _out_proj(x2d, oa, ob, oc, w_out, g_ffn):
    T = x2d.shape[0]
    tm = TM_OUT
    rows = lambda w: pl.BlockSpec((tm, w), lambda i: (i, 0))
    return pl.pallas_call(
        _out_kernel,
        grid=(T // tm,),
        in_specs=[rows(D_MODEL), rows(oa.shape[1]), rows(ob.shape[1]), rows(oc.shape[1]),
                  pl.BlockSpec(w_out.shape, lambda i: (0, 0)),
                  pl.BlockSpec((1, D_MODEL), lambda i: (0, 0))],
        out_specs=[rows(D_MODEL), rows(D_MODEL)],
        out_shape=[jax.ShapeDtypeStruct((T, D_MODEL), F32),
                   jax.ShapeDtypeStruct((T, D_MODEL), BF16)],
        compiler_params=pltpu.CompilerParams(dimension_semantics=("arbitrary",),
                                             vmem_limit_bytes=VMEM_LIMIT_BYTES),
        name="out_proj",
    )(x2d, oa, ob, oc, w_out, g_ffn)


def _ffn_kernel(hp_ref, h_ref, hn_ref, x1_ref, wu_ref, cw_ref, cb_ref, wd_ref, fg_ref,
                o_ref, act_ref, *, tiles_per_seq, final_norm):
    i = pl.program_id(0)
    tm = h_ref.shape[0]
    halo = hp_ref.shape[0]
    at_start = (i % tiles_per_seq) == 0
    at_end = (i % tiles_per_seq) == tiles_per_seq - 1
    hp = jnp.where(at_start, jnp.zeros_like(hp_ref[...]), hp_ref[...])
    hn = jnp.where(at_end, jnp.zeros_like(hn_ref[...]), hn_ref[...])
    hext = jnp.concatenate([hp, h_ref[...], hn], axis=0)

    def branch(k, c0):
        u = _dot(hext, wu_ref[:, k * D_FF + c0:k * D_FF + c0 + FFN_CHUNK])
        cw = cw_ref[k, :, c0:c0 + FFN_CHUNK]
        return (u[halo - 1:halo - 1 + tm] * cw[0:1] + u[halo:halo + tm] * cw[1:2]
                + u[halo + 1:halo + 1 + tm] * cw[2:3] + cb_ref[k, :, c0:c0 + FFN_CHUNK])

    for c0 in range(0, D_FF, FFN_CHUNK):
        g = branch(0, c0)
        val = branch(1, c0)
        act_ref[:, c0:c0 + FFN_CHUNK] = (g / (1.0 + jnp.exp(-g)) * val).astype(BF16)

    y = x1_ref[...] + _dot(act_ref[...], wd_ref[...])
    if final_norm:
        y = _rms(y, D_MODEL) * fg_ref[...]
    o_ref[...] = y


def _ffn(h2, x1, w_up, conv_w, conv_b, w_down, final_g, S, final_norm):
    T = h2.shape[0]
    tm, halo = TM_FFN, BF16_SUBLANES
    per = tm // halo
    last = T // halo - 1
    resident = lambda shape: pl.BlockSpec(shape, lambda i: (0,) * len(shape),
                                          pipeline_mode=pl.Buffered(1))
    return pl.pallas_call(
        functools.partial(_ffn_kernel, tiles_per_seq=S // tm, final_norm=final_norm),
        grid=(T // tm,),
        in_specs=[pl.BlockSpec((halo, D_MODEL), lambda i: (jnp.maximum(i * per - 1, 0), 0)),
                  pl.BlockSpec((tm, D_MODEL), lambda i: (i, 0)),
                  pl.BlockSpec((halo, D_MODEL), lambda i: (jnp.minimum((i + 1) * per, last), 0)),
                  pl.BlockSpec((tm, D_MODEL), lambda i: (i, 0)),
                  resident((D_MODEL, 2 * D_FF)), resident((2, 3, D_FF)), resident((2, 1, D_FF)),
                  resident((D_FF, D_MODEL)), resident((1, D_MODEL))],
        out_specs=pl.BlockSpec((tm, D_MODEL), lambda i: (i, 0)),
        out_shape=jax.ShapeDtypeStruct((T, D_MODEL), F32),
        scratch_shapes=[pltpu.VMEM((tm, D_FF), BF16)],
        compiler_params=pltpu.CompilerParams(dimension_semantics=("arbitrary",),
                                             vmem_limit_bytes=VMEM_LIMIT_BYTES),
        name="conv_mlp",
    )(h2, h2, h2, x1, w_up, conv_w, conv_b, w_down, final_g)


def _pad_groups(w, n_groups, width, offset=0):
    r = w.shape[0]
    w = w.reshape(r, n_groups, width)
    w = jnp.pad(w, ((0, 0), (0, 0), (offset, LANES - width - offset)))
    return w.reshape(r, n_groups * LANES)


def _rope_core(S, dim):
    nf = dim // 4
    t = jnp.arange(S, dtype=jnp.int32)
    rows = (t // GRID_W).astype(F32)
    cols = (t % GRID_W).astype(F32)
    inv = ROPE_BASE ** (-jnp.arange(nf, dtype=F32) / nf)
    ar = rows[:, None] * inv
    ac = cols[:, None] * inv
    cos = jnp.concatenate([jnp.cos(ar), jnp.cos(ar), jnp.cos(ac), jnp.cos(ac)], axis=-1)
    sin = jnp.concatenate([-jnp.sin(ar), jnp.sin(ar), -jnp.sin(ac), jnp.sin(ac)], axis=-1)
    return cos, sin


def _rope_tables(S):
    cos_a, sin_a = _rope_core(S, HEAD_DIM)
    cos_b, sin_b = _rope_core(S, MLA_ROPE)
    pad = LANES - MLA_NOPE - MLA_ROPE
    return (jnp.tile(cos_a, (1, 2)), jnp.tile(sin_a, (1, 2)),
            jnp.concatenate([jnp.ones((S, MLA_NOPE), F32), cos_b, jnp.zeros((S, pad), F32)], -1),
            jnp.concatenate([jnp.zeros((S, MLA_NOPE), F32), sin_b, jnp.zeros((S, pad), F32)], -1))


def _prep_w_in(w):
    a_end = (N_HEADS_A + 2 * N_KV_A) * HEAD_DIM
    b_end = a_end + MLA_Q_RANK + MLA_KV_RANK + MLA_ROPE
    cq = w[:, a_end:a_end + MLA_Q_RANK]
    ckv = w[:, a_end + MLA_Q_RANK:a_end + MLA_Q_RANK + MLA_KV_RANK]
    kr = w[:, b_end - MLA_ROPE:b_end]
    out = jnp.concatenate([
        w[:, :a_end],
        jnp.pad(cq, ((0, 0), (0, CQ_PAD - MLA_Q_RANK))),
        ckv,
        jnp.pad(kr, ((0, 0), (MLA_NOPE, LANES - MLA_NOPE - MLA_ROPE))),
        w[:, b_end:],
    ], axis=-1).astype(BF16)
    assert out.shape[1] == IN_COLS
    return out


def _twice(v):
    return jnp.tile(v, 2).reshape(1, 2 * v.shape[0])


def kernel(x, norm_attn, w_in, q_norm_a, k_norm_a, q_a_norm_b, w_uq_b, kv_a_norm_b, w_ukv_b,
           lambda_q1_c, lambda_k1_c, lambda_q2_c, lambda_k2_c, subln_c, w_out,
           norm_ffn, w_up, conv_w, conv_b, w_down, final_norm):
    B, S, D = x.shape
    depth = w_in.shape[0]
    assert D == D_MODEL and S % max(TM_IN, TM_OUT, TM_FFN, TQ, TK) == 0 and S % GRID_W == 0
    assert TM_IN == TK
    T = B * S

    tabs = _rope_tables(S)
    row = lambda v: v.reshape(1, -1)
    a_pairs = [(j, j + N_HEADS_A // 2) for j in range(N_HEADS_A // 2)]
    a_rows = np.concatenate([np.arange(h * HEAD_DIM, (h + 1) * HEAD_DIM)
                             for pair in a_pairs for h in pair])
    xc = x.reshape(T, D)
    for l in range(depth):
        lam_init = 0.8 - 0.6 * math.exp(-0.3 * l)
        wuq_p = jnp.pad(_pad_groups(w_uq_b[l], N_HEADS_B, MLA_NOPE + MLA_ROPE),
                        ((0, CQ_PAD - MLA_Q_RANK), (0, 0))).astype(BF16)
        wukv = w_ukv_b[l].reshape(MLA_KV_RANK, N_HEADS_B, MLA_NOPE + MLA_V)
        v_cols = [_pad_groups(wukv[:, h, MLA_NOPE:], 1, MLA_V, offset=HALF * (h % 2))
                  for h in range(N_HEADS_B)]
        wukv_p = jnp.concatenate(
            [_pad_groups(wukv[:, :, :MLA_NOPE].reshape(MLA_KV_RANK, -1), N_HEADS_B, MLA_NOPE)]
            + v_cols, axis=-1).astype(BF16)

        qa, kta, va, qb, ktb, vb, qc, ktc, vc, nrm = _in_proj(
            xc, row(norm_attn[l]), _prep_w_in(w_in[l]), tabs,
            _twice(q_norm_a[l]), _twice(k_norm_a[l]),
            jnp.pad(q_a_norm_b[l], (0, CQ_PAD - MLA_Q_RANK)).reshape(1, CQ_PAD),
            row(kv_a_norm_b[l]), wuq_p, wukv_p, B, S)

        nrm = jnp.max(nrm.reshape(B, -1, LANES), axis=1)

        def bounded(q0, nq, q_rep, k0, nk, k_rep):
            q2 = jnp.repeat(nrm[:, q0:q0 + nq], q_rep, axis=1)
            k2 = jnp.repeat(nrm[:, k0:k0 + nk], k_rep, axis=1)
            ok = q2 * k2 * NRM_MARGIN <= SAFE_LOG2_RANGE ** 2
            return jnp.all(ok.reshape(B, -1, HEADS_PER_PIPELINE), axis=-1).astype(jnp.int32)

        qk_a = (HEAD_DIM * jnp.max(jnp.abs(q_norm_a[l])) * jnp.max(jnp.abs(k_norm_a[l]))
                * (HEAD_DIM ** -0.5 * LOG2E))
        bounded_a = jnp.broadcast_to(
            (qk_a * NRM_MARGIN <= SAFE_LOG2_RANGE).astype(jnp.int32),
            (B, N_HEADS_A // HEADS_PER_PIPELINE))

        group = N_HEADS_A // N_KV_A
        oa = _attention(
            functools.partial(_attn_ab_kernel, n_heads=N_HEADS_A, q_index=lambda h: h // 2,
                              kt_index=lambda h: 2 * (h // group) + h % 2,
                              v_index=lambda h: h // group, out_pairs=a_pairs, n_keys=S),
            bounded_a, qa, kta, va, [], N_HEADS_A * HEAD_DIM, B, S)
        same = lambda h: h
        ob = _attention(
            functools.partial(_attn_ab_kernel, n_heads=N_HEADS_B, q_index=same, kt_index=same,
                              v_index=same,
                              out_pairs=[(2 * j, 2 * j + 1) for j in range(N_HEADS_B // 2)],
                              n_keys=S),
            bounded(NRM_QB, N_HEADS_B, 1, NRM_KB, N_HEADS_B, 1),
            qb, ktb, vb, [], N_HEADS_B * MLA_V, B, S)
        oc = _attention(
            functools.partial(_attn_c_kernel, n_keys=S, lam_init=lam_init),
            bounded(NRM_QC, N_HEADS_C, 2, NRM_KC, 2 * N_HEADS_C, 1),
            qc, ktc, vc,
            [row(lambda_q1_c[l]), row(lambda_k1_c[l]), row(lambda_q2_c[l]), row(lambda_k2_c[l]),
             _twice(subln_c[l])],
            N_HEADS_C * DIFF_V, B, S)

        w_out_l = jnp.concatenate([w_out[l][a_rows], w_out[l][N_HEADS_A * HEAD_DIM:]], axis=0)
        x1, h2 = _out_proj(xc, oa, ob, oc, w_out_l.astype(BF16), row(norm_ffn[l]))
        xc = _ffn(h2, x1, w_up[l].astype(BF16),
                  conv_w[l].reshape(3, 2, D_FF).transpose(1, 0, 2),
                  conv_b[l].reshape(2, 1, D_FF), w_down[l].astype(BF16),
                  row(final_norm), S, final_norm=(l == depth - 1))
    return xc.reshape(B, S, D)
```

```python
import functools
import math

import jax
import jax.numpy as jnp
import numpy as np
from jax import lax
from jax.experimental import pallas as pl
from jax.experimental.pallas import tpu as pltpu

D_MODEL = 1024
GRID_W = 64
HEAD_DIM = 64
N_HEADS_A = 8
N_KV_A = 2
N_HEADS_B = 4
MLA_Q_RANK = 192
MLA_KV_RANK = 128
MLA_NOPE = 64
MLA_ROPE = 32
MLA_V = 64
N_HEADS_C = 4
DIFF_QK = 32
DIFF_V = 64
D_FF = 2816
ROPE_BASE = 10000.0
EPS = 1e-6

LANES = 128
HALF = LANES // 2
BF16_SUBLANES = 16
VMEM_LIMIT_BYTES = 56 * 1024 * 1024

LOG2E = math.log2(math.e)
assert HEAD_DIM == MLA_V == DIFF_V == 2 * DIFF_QK == HALF

OFF_QA = 0
OFF_KA = OFF_QA + N_HEADS_A * HEAD_DIM
OFF_VA = OFF_KA + LANES
OFF_CQ = OFF_VA + LANES
CQ_PAD = 2 * LANES
OFF_CKV = OFF_CQ + CQ_PAD
OFF_KR = OFF_CKV + MLA_KV_RANK
OFF_QC = OFF_KR + LANES
OFF_KC = OFF_QC + N_HEADS_C * HALF
OFF_VC = OFF_KC + N_HEADS_C * HALF
IN_COLS = OFF_VC + N_HEADS_C * HALF

NRM_QB = 0
NRM_KB = NRM_QB + N_HEADS_B
NRM_QC = NRM_KB + N_HEADS_B
NRM_KC = NRM_QC + N_HEADS_C
NRM_GROUPS = 2 * N_HEADS_B + N_HEADS_C
NRM_MARGIN = 1.03
SAFE_LOG2_RANGE = 55.0

F32 = jnp.float32
BF16 = jnp.bfloat16

TM_IN = 512
TM_OUT = 512
TM_FFN = 512
FFN_CHUNK = 256
TQ = 512
TQ_SUB = 512
TK = 512
SCORE_LOOKAHEAD = 2
HEADS_PER_PIPELINE = 4


def _dot(a, b):
    return jnp.dot(a, b, preferred_element_type=F32)


def _rms(x, n):
    ms = jnp.sum(x * x, axis=-1, keepdims=True) * (1.0 / n)
    return x * lax.rsqrt(ms + EPS)


def _low_lanes():
    return lax.broadcasted_iota(jnp.int32, (1, LANES), 1) < HALF


def _half_rms(x):
    low = _low_lanes()
    sq = x * x
    s_lo = jnp.sum(jnp.where(low, sq, 0.0), axis=-1, keepdims=True)
    s_hi = jnp.sum(jnp.where(low, 0.0, sq), axis=-1, keepdims=True)
    return x * lax.rsqrt(jnp.where(low, s_lo, s_hi) * (1.0 / HALF) + EPS)


def _in_kernel(x_ref, gat_ref, w_ref, cos_a_ref, sin_a_ref, cos_b_ref, sin_b_ref,
               qg_a_ref, kg_a_ref, qg_b_ref, kvg_b_ref, wuq_ref, wukv_ref, sel_ref,
               qa_ref, kta_ref, va_ref, qb_ref, ktb_ref, vb_ref, qc_ref, ktc_ref, vc_ref, nrm_ref):
    x = x_ref[...]
    tm = x.shape[0]
    h = (_rms(x, D_MODEL) * gat_ref[...]).astype(BF16)

    lane = lax.broadcasted_iota(jnp.int32, (1, LANES), 1)
    low = lane < HALF
    row = lax.broadcasted_iota(jnp.int32, (LANES, 1), 0)

    z = _dot(h, w_ref[...])

    def seg(off, width):
        return z[:, off:off + width]

    def swap(y, nf):
        up = pltpu.roll(y, LANES - nf, axis=1)
        down = pltpu.roll(y, nf, axis=1)
        return jnp.where((lane & nf) == 0, up, down)

    def rope(y, c, s, nf):
        return y * c + swap(y, nf) * s

    def group(z, g):
        return z[:, g * LANES:(g + 1) * LANES]

    squares = []

    def note(y):
        squares.append((y * y).astype(BF16))

    def noted_row_sums(first):
        return _dot(jnp.concatenate(squares[first:], axis=-1),
                    sel_ref[first * LANES:len(squares) * LANES])

    cos_b, sin_b = cos_b_ref[...], sin_b_ref[...]
    nf_b = MLA_ROPE // 4
    cq = (_rms(seg(OFF_CQ, CQ_PAD), MLA_Q_RANK) * qg_b_ref[...]).astype(BF16)
    qb = _dot(cq, wuq_ref[...])
    scale_b = (MLA_NOPE + MLA_ROPE) ** -0.5 * LOG2E
    for hh in range(N_HEADS_B):
        y = rope(group(qb, hh), cos_b, sin_b, nf_b) * scale_b
        qb_ref[0, hh] = y.astype(BF16)
        note(y)
    ckv = (_rms(seg(OFF_CKV, MLA_KV_RANK), MLA_KV_RANK) * kvg_b_ref[...]).astype(BF16)
    kvb = _dot(ckv, wukv_ref[...])
    kr = rope(seg(OFF_KR, LANES), cos_b, sin_b, nf_b)
    for hh in range(N_HEADS_B):
        y = group(kvb, hh) + kr
        ktb_ref[0, hh] = y.T.astype(BF16)
        note(y)
        ones_other_half = jnp.where(low, float(hh % 2), float(1 - hh % 2))
        vb_ref[0, hh] = (group(kvb, N_HEADS_B + hh) + ones_other_half).astype(BF16)
    n2 = noted_row_sums(0)
    n_noted_b = len(squares)

    zq = seg(OFF_QC, N_HEADS_C * HALF)
    zk = seg(OFF_KC, N_HEADS_C * HALF)
    zv = seg(OFF_VC, N_HEADS_C * HALF)
    k_groups = []
    for p in range(N_HEADS_C // 2):
        y = group(zq, p) * (DIFF_QK ** -0.5 * LOG2E)
        qc_ref[0, p] = y.astype(BF16)
        note(y)
        k_groups.append(group(zk, p))
        kt = group(zk, p).T
        for sub in range(4):
            mine = (row >= sub * DIFF_QK) & (row < (sub + 1) * DIFF_QK)
            ktc_ref[0, 4 * p + sub, 0] = jnp.where(mine, kt, 0.0).astype(BF16)
        vz = group(zv, p)
        vc_ref[0, 2 * p] = jnp.where(low, vz, 1.0).astype(BF16)
        vc_ref[0, 2 * p + 1] = jnp.where(low, 1.0, vz).astype(BF16)
    for kz in k_groups:
        note(kz)
    n2 = n2 + noted_row_sums(n_noted_b)
    nrm_ref[0] = jnp.max(n2, axis=0, keepdims=True)

    cos_a, sin_a = cos_a_ref[...], sin_a_ref[...]
    nf_a = HEAD_DIM // 4
    zq = seg(OFF_QA, N_HEADS_A * HEAD_DIM)
    for p in range(N_HEADS_A // 2):
        y = rope(_half_rms(group(zq, p)) * qg_a_ref[...], cos_a, sin_a, nf_a)
        qa_ref[0, p] = (y * (HEAD_DIM ** -0.5 * LOG2E)).astype(BF16)
    y = rope(_half_rms(seg(OFF_KA, LANES)) * kg_a_ref[...], cos_a, sin_a, nf_a)
    kt = y.T.astype(BF16)
    zeros = jnp.zeros((HALF, tm), BF16)
    for g in range(N_KV_A):
        kg = kt[g * HALF:(g + 1) * HALF]
        kta_ref[0, 2 * g] = jnp.concatenate([kg, zeros], axis=0)
        kta_ref[0, 2 * g + 1] = jnp.concatenate([zeros, kg], axis=0)
    zv = seg(OFF_VA, LANES)
    va_ref[0, 0] = jnp.where(low, zv, 1.0).astype(BF16)
    va_ref[0, 1] = jnp.where(low, 1.0, zv).astype(BF16)


def _norm_routing():
    sel = np.zeros((NRM_GROUPS, LANES, LANES), np.float32)
    g = 0
    for slot in ([NRM_QB + h for h in range(N_HEADS_B)] + [NRM_KB + h for h in range(N_HEADS_B)]):
        sel[g, :, slot] = 1.0
        g += 1
    for p in range(N_HEADS_C // 2):
        for f in range(2):
            sel[g, f * HALF:(f + 1) * HALF, NRM_QC + 2 * p + f] = 1.0
        g += 1
    for p in range(N_HEADS_C // 2):
        for sub in range(4):
            sel[g, sub * DIFF_QK:(sub + 1) * DIFF_QK, NRM_KC + 4 * p + sub] = 1.0
        g += 1
    assert g == NRM_GROUPS
    return jnp.asarray(sel.reshape(NRM_GROUPS * LANES, LANES), BF16)


def _in_proj(x2d, gat, w_in_r, tabs, qg_a, kg_a, qg_b, kvg_b, wuq_p, wukv_p, B, S):
    T = B * S
    tm = TM_IN
    nst = S // tm
    const = lambda shape: pl.BlockSpec(shape, lambda i: (0,) * len(shape))
    tab = pl.BlockSpec((tm, LANES), lambda i: (i % nst, 0))
    hm = lambda H: pl.BlockSpec((1, H, tm, LANES), lambda i: (i // nst, 0, i % nst, 0))
    hmt = lambda H: pl.BlockSpec((1, H, LANES, tm), lambda i: (i // nst, 0, 0, i % nst))
    sds = lambda H: jax.ShapeDtypeStruct((B, H, S, LANES), BF16)
    sdt = lambda H: jax.ShapeDtypeStruct((B, H, LANES, S), BF16)
    return pl.pallas_call(
        _in_kernel,
        grid=(T // tm,),
        in_specs=[pl.BlockSpec((tm, D_MODEL), lambda i: (i, 0)),
                  const((1, D_MODEL)), const((D_MODEL, IN_COLS)),
                  tab, tab, tab, tab,
                  const((1, LANES)), const((1, LANES)), const((1, CQ_PAD)), const((1, MLA_KV_RANK)),
                  const((CQ_PAD, N_HEADS_B * LANES)), const((MLA_KV_RANK, 2 * N_HEADS_B * LANES)),
                  const((NRM_GROUPS * LANES, LANES))],
        out_specs=[hm(N_HEADS_A // 2), hmt(2 * N_KV_A), hm(N_KV_A),
                   hm(N_HEADS_B), hmt(N_HEADS_B), hm(N_HEADS_B),
                   hm(N_HEADS_C // 2),
                   pl.BlockSpec((1, 2 * N_HEADS_C, 1, LANES, tm),
                                lambda i: (i // nst, 0, i % nst, 0, 0)),
                   hm(N_HEADS_C),
                   pl.BlockSpec((1, 1, LANES), lambda i: (i, 0, 0))],
        out_shape=[sds(N_HEADS_A // 2), sdt(2 * N_KV_A), sds(N_KV_A),
                   sds(N_HEADS_B), sdt(N_HEADS_B), sds(N_HEADS_B),
                   sds(N_HEADS_C // 2),
                   jax.ShapeDtypeStruct((B, 2 * N_HEADS_C, S // tm, LANES, tm), BF16),
                   sds(N_HEADS_C),
                   jax.ShapeDtypeStruct((T // tm, 1, LANES), F32)],
        compiler_params=pltpu.CompilerParams(dimension_semantics=("arbitrary",),
                                             vmem_limit_bytes=VMEM_LIMIT_BYTES),
        name="in_proj",
    )(x2d, gat, w_in_r, *tabs, qg_a, kg_a, qg_b, kvg_b, wuq_p, wukv_p, _norm_routing())


def _lane_group_max(s):
    m = s[:, :LANES]
    for j in range(1, s.shape[1] // LANES):
        m = jnp.maximum(m, s[:, j * LANES:(j + 1) * LANES])
    return m


def _pv_bounded(streams, n_chunks):
    items = [(r, i) for r in range(n_chunks) for i in range(len(streams))]
    accs = [None] * len(streams)
    in_flight = {}
    for k in range(len(items) + SCORE_LOOKAHEAD):
        if k < len(items):
            r, i = items[k]
            scores, _, row_off = streams[i]
            s, off = scores(r), row_off(r)
            in_flight[k] = s if off is None else s + off
        if k >= SCORE_LOOKAHEAD:
            r, i = items[k - SCORE_LOOKAHEAD]
            d = _dot(jnp.exp2(in_flight.pop(k - SCORE_LOOKAHEAD)).astype(BF16), streams[i][1](r))
            accs[i] = d if accs[i] is None else accs[i] + d
    return accs


def _pv_exact(streams, n_chunks, s_ref):
    maxes = []
    for i, (scores, _, row_off) in enumerate(streams):
        m_run = None
        for r in range(n_chunks):
            s = scores(r)
            s_ref[i * TQ_SUB:(i + 1) * TQ_SUB, r * TK:(r + 1) * TK] = s
            m, off = _lane_group_max(s), row_off(r)
            if off is not None:
                m = m + off
            m_run = m if m_run is None else jnp.maximum(m_run, m)
        maxes.append(jnp.max(m_run, axis=-1, keepdims=True))
    accs = []
    for i, (_, values, row_off) in enumerate(streams):
        acc = None
        for r in range(n_chunks):
            off = row_off(r)
            shift = maxes[i] if off is None else maxes[i] - off
            p = jnp.exp2(s_ref[i * TQ_SUB:(i + 1) * TQ_SUB, r * TK:(r + 1) * TK] - shift)
            d = _dot(p.astype(BF16), values(r))
            acc = d if acc is None else acc + d
        accs.append(acc)
    return accs


def _run_heads(bounded, heads, head_streams, n_chunks, s_ref, acc_ref):
    n_sub = TQ // TQ_SUB

    @pl.when(bounded)
    def _():
        accs = _pv_bounded([s for h in heads for s in head_streams(h)], n_chunks)
        for k, h in enumerate(heads):
            acc_ref[h] = jnp.concatenate(accs[k * n_sub:(k + 1) * n_sub], axis=0)

    @pl.when(jnp.logical_not(bounded))
    def _():
        for h in heads:
            acc_ref[h] = jnp.concatenate(_pv_exact(head_streams(h), n_chunks, s_ref), axis=0)


def _normalised_pair(acc_low, acc_high):
    low = _low_lanes()
    num = jnp.where(low, acc_low, acc_high)
    den = pltpu.roll(jnp.where(low, acc_high, acc_low), HALF, axis=1)
    return num / den


def _attn_ab_kernel(bounded_ref, q_ref, kt_ref, v_ref, o_ref, s_ref, acc_ref, *,
                    n_heads, q_index, kt_index, v_index, out_pairs, n_keys):
    n_chunks = n_keys // TK

    def head_streams(h):
        q = q_ref[0, q_index(h)]
        return [(lambda r, qs=q[r0:r0 + TQ_SUB]:
                 _dot(qs, kt_ref[0, kt_index(h), :, r * TK:(r + 1) * TK]),
                 lambda r: v_ref[0, v_index(h), r * TK:(r + 1) * TK, :],
                 lambda r: None) for r0 in range(0, TQ, TQ_SUB)]

    def head_group(j, carry):
        heads = [HEADS_PER_PIPELINE * j + k for k in range(HEADS_PER_PIPELINE)]
        _run_heads(bounded_ref[pl.program_id(0), j] != 0, heads, head_streams, n_chunks, s_ref,
                   acc_ref)
        return carry

    lax.fori_loop(0, n_heads // HEADS_PER_PIPELINE, head_group, 0)
    o_ref[...] = jnp.concatenate([_normalised_pair(acc_ref[a], acc_ref[b]) for a, b in out_pairs],
                                 axis=-1).astype(BF16)


def _attn_c_kernel(bounded_ref, q_ref, kt_ref, v_ref, lq1_ref, lk1_ref, lq2_ref, lk2_ref,
                   subln_ref, o_ref, s_ref, acc_ref, *, n_keys, lam_init):
    assert TQ == TK
    nc = n_keys // TK
    cd = pl.program_id(1)
    col = lax.broadcasted_iota(jnp.int32, (1, TK), 1).astype(F32)
    row = lax.broadcasted_iota(jnp.int32, (TQ_SUB, 1), 0).astype(F32)
    subs = [(r, r + TQ_SUB) for r in range(0, TQ, TQ_SUB)]
    neg_dist = [-jnp.abs((lax.broadcasted_iota(jnp.int32, (TQ_SUB, TK), 0) + r0
                          - lax.broadcasted_iota(jnp.int32, (TQ_SUB, TK), 1)).astype(F32))
                for r0, _ in subs]

    chunk = [(cd + r) % nc for r in range(nc)]

    def head_bias(hh):
        slope = jnp.float32(LOG2E * 2.0 ** (-8.0 * N_HEADS_C / N_HEADS_C))
        for k in range(N_HEADS_C - 2, -1, -1):
            slope = jnp.where(hh == k, LOG2E * 2.0 ** (-8.0 * (k + 1) / N_HEADS_C), slope)
        key_term, row_off = [None], [[None] * len(subs)]
        for r in range(1, nc):
            side = jnp.where(chunk[r] < cd, slope, -slope)
            key_term.append(col * side)
            base = -slope * (jnp.abs(chunk[r] - cd) * TK).astype(F32)
            row_off.append([base - side * (row + float(r0)) for r0, _ in subs])
        return slope, key_term, row_off

    def head_pair(p, carry):
        bias = [head_bias(2 * p + f) for f in range(2)]

        def streams_of(k):
            f, idx, hh = k // 2, 4 * p + k, 2 * p + k // 2
            slope, key_term, row_off = bias[f]

            def scores(r, si):
                s = _dot(q_ref[0, p][subs[si][0]:subs[si][1]], kt_ref[0, idx, chunk[r]])
                return s + (neg_dist[si] * slope if r == 0 else key_term[r])

            return [(functools.partial(scores, si=si),
                     lambda r: v_ref[0, hh, pl.ds(pl.multiple_of(chunk[r] * TK, TK), TK), :],
                     lambda r, si=si: row_off[r][si]) for si in range(len(subs))]

        _run_heads(bounded_ref[pl.program_id(0), p] != 0, list(range(4)), streams_of, nc, s_ref,
                   acc_ref.at[pl.ds(4 * p, 4)])
        return carry

    lax.fori_loop(0, N_HEADS_C // 2, head_pair, 0)

    lam = (jnp.exp(jnp.sum(lq1_ref[...] * lk1_ref[...], axis=-1, keepdims=True))
           - jnp.exp(jnp.sum(lq2_ref[...] * lk2_ref[...], axis=-1, keepdims=True)) + lam_init)
    outs = []
    for p in range(N_HEADS_C // 2):
        o = (_normalised_pair(acc_ref[4 * p], acc_ref[4 * p + 2])
             - lam * _normalised_pair(acc_ref[4 * p + 1], acc_ref[4 * p + 3]))
        outs.append(_half_rms(o) * subln_ref[...] * (1.0 - lam_init))
    o_ref[...] = jnp.concatenate(outs, axis=-1).astype(BF16)


def _attention(kernel_fn, bounded, q, kt, v, extras, out_width, B, S):
    nq = S // TQ
    hq, hv = q.shape[1], v.shape[1]
    extra_specs = [pl.BlockSpec(e.shape, lambda b, i, f, n=e.ndim: (0,) * n) for e in extras]
    return pl.pallas_call(
        kernel_fn,
        grid_spec=pltpu.PrefetchScalarGridSpec(
            num_scalar_prefetch=1,
            grid=(B, nq),
            in_specs=[pl.BlockSpec((1, hq, TQ, LANES), lambda b, i, f: (b, 0, i, 0)),
                      pl.BlockSpec((1,) + kt.shape[1:],
                                   lambda b, i, f, n=kt.ndim: (b,) + (0,) * (n - 1)),
                      pl.BlockSpec((1, hv, S, LANES), lambda b, i, f: (b, 0, 0, 0))] + extra_specs,
            out_specs=pl.BlockSpec((TQ, out_width), lambda b, i, f: (b * nq + i, 0)),
            scratch_shapes=[pltpu.VMEM((TQ, S), F32),
                            pltpu.VMEM((HEADS_PER_PIPELINE * bounded.shape[1], TQ, LANES), F32)]),
        out_shape=jax.ShapeDtypeStruct((B * S, out_width), BF16),
        compiler_params=pltpu.CompilerParams(dimension_semantics=("arbitrary", "arbitrary"),
                                             vmem_limit_bytes=VMEM_LIMIT_BYTES),
        name=kernel_fn.func.__name__.strip("_"),
    )(bounded, q, kt, v, *extras)


def _out_kernel(x_ref, oa_ref, ob_ref, oc_ref, wo_ref, g_ref, x1_ref, h2_ref):
    na = N_HEADS_A * HEAD_DIM
    nb = na + N_HEADS_B * MLA_V
    y = (x_ref[...] + _dot(oa_ref[...], wo_ref[:na]) + _dot(ob_ref[...], wo_ref[na:nb])
         + _dot(oc_ref[...], wo_ref[nb:]))
    x1_ref[...] = y
    h2_ref[...] = (_rms(y, D_MODEL) * g_ref[...]).astype(BF16)


def _out_proj(x2d, oa, ob, oc, w_out, g_ffn):
    T = x2d.shape[0]
    tm = TM_OUT
    rows = lambda w: pl.BlockSpec((tm, w), lambda i: (i, 0))
    return pl.pallas_call(
        _out_kernel,
        grid=(T // tm,),
        in_specs=[rows(D_MODEL), rows(oa.shape[1]), rows(ob.shape[1]), rows(oc.shape[1]),
                  pl.BlockSpec(w_out.shape, lambda i: (0, 0)),
                  pl.BlockSpec((1, D_MODEL), lambda i: (0, 0))],
        out_specs=[rows(D_MODEL), rows(D_MODEL)],
        out_shape=[jax.ShapeDtypeStruct((T, D_MODEL), F32),
                   jax.ShapeDtypeStruct((T, D_MODEL), BF16)],
        compiler_params=pltpu.CompilerParams(dimension_semantics=("arbitrary",),
                                             vmem_limit_bytes=VMEM_LIMIT_BYTES),
        name="out_proj",
    )(x2d, oa, ob, oc, w_out, g_ffn)


def _ffn_kernel(hp_ref, h_ref, hn_ref, x1_ref, wu_ref, cw_ref, cb_ref, wd_ref, fg_ref,
                o_ref, act_ref, *, tiles_per_seq, final_norm):
    i = pl.program_id(0)
    tm = h_ref.shape[0]
    halo = hp_ref.shape[0]
    at_start = (i % tiles_per_seq) == 0
    at_end = (i % tiles_per_seq) == tiles_per_seq - 1
    hp = jnp.where(at_start, jnp.zeros_like(hp_ref[...]), hp_ref[...])
    hn = jnp.where(at_end, jnp.zeros_like(hn_ref[...]), hn_ref[...])
    hext = jnp.concatenate([hp, h_ref[...], hn], axis=0)

    def branch(k, c0):
        u = _dot(hext, wu_ref[:, k * D_FF + c0:k * D_FF + c0 + FFN_CHUNK])
        cw = cw_ref[k, :, c0:c0 + FFN_CHUNK]
        return (u[halo - 1:halo - 1 + tm] * cw[0:1] + u[halo:halo + tm] * cw[1:2]
                + u[halo + 1:halo + 1 + tm] * cw[2:3] + cb_ref[k, :, c0:c0 + FFN_CHUNK])

    for c0 in range(0, D_FF, FFN_CHUNK):
        g = branch(0, c0)
        val = branch(1, c0)
        act_ref[:, c0:c0 + FFN_CHUNK] = (g / (1.0 + jnp.exp(-g)) * val).astype(BF16)

    y = x1_ref[...] + _dot(act_ref[...], wd_ref[...])
    if final_norm:
        y = _rms(y, D_MODEL) * fg_ref[...]
    o_ref[...] = y


def _ffn(h2, x1, w_up, conv_w, conv_b, w_down, final_g, S, final_norm):
    T = h2.shape[0]
    tm, halo = TM_FFN, BF16_SUBLANES
    per = tm // halo
    last = T // halo - 1
    resident = lambda shape: pl.BlockSpec(shape, lambda i: (0,) * len(shape),
                                          pipeline_mode=pl.Buffered(1))
    return pl.pallas_call(
        functools.partial(_ffn_kernel, tiles_per_seq=S // tm, final_norm=final_norm),
        grid=(T // tm,),
        in_specs=[pl.BlockSpec((halo, D_MODEL), lambda i: (jnp.maximum(i * per - 1, 0), 0)),
                  pl.BlockSpec((tm, D_MODEL), lambda i: (i, 0)),
                  pl.BlockSpec((halo, D_MODEL), lambda i: (jnp.minimum((i + 1) * per, last), 0)),
                  pl.BlockSpec((tm, D_MODEL), lambda i: (i, 0)),
                  resident((D_MODEL, 2 * D_FF)), resident((2, 3, D_FF)), resident((2, 1, D_FF)),
                  resident((D_FF, D_MODEL)), resident((1, D_MODEL))],
        out_specs=pl.BlockSpec((tm, D_MODEL), lambda i: (i, 0)),
        out_shape=jax.ShapeDtypeStruct((T, D_MODEL), F32),
        scratch_shapes=[pltpu.VMEM((tm, D_FF), BF16)],
        compiler_params=pltpu.CompilerParams(dimension_semantics=("arbitrary",),
                                             vmem_limit_bytes=VMEM_LIMIT_BYTES),
        name="conv_mlp",
    )(h2, h2, h2, x1, w_up, conv_w, conv_b, w_down, final_g)


def _pad_groups(w, n_groups, width, offset=0):
    r = w.shape[0]
    w = w.reshape(r, n_groups, width)
    w = jnp.pad(w, ((0, 0), (0, 0), (offset, LANES - width - offset)))
    return w.reshape(r, n_groups * LANES)


def _rope_core(S, dim):
    nf = dim // 4
    t = jnp.arange(S, dtype=jnp.int32)
    rows = (t // GRID_W).astype(F32)
    cols = (t % GRID_W).astype(F32)
    inv = ROPE_BASE ** (-jnp.arange(nf, dtype=F32) / nf)
    ar = rows[:, None] * inv
    ac = cols[:, None] * inv
    cos = jnp.concatenate([jnp.cos(ar), jnp.cos(ar), jnp.cos(ac), jnp.cos(ac)], axis=-1)
    sin = jnp.concatenate([-jnp.sin(ar), jnp.sin(ar), -jnp.sin(ac), jnp.sin(ac)], axis=-1)
    return cos, sin


def _rope_tables(S):
    cos_a, sin_a = _rope_core(S, HEAD_DIM)
    cos_b, sin_b = _rope_core(S, MLA_ROPE)
    pad = LANES - MLA_NOPE - MLA_ROPE
    return (jnp.tile(cos_a, (1, 2)), jnp.tile(sin_a, (1, 2)),
            jnp.concatenate([jnp.ones((S, MLA_NOPE), F32), cos_b, jnp.zeros((S, pad), F32)], -1),
            jnp.concatenate([jnp.zeros((S, MLA_NOPE), F32), sin_b, jnp.zeros((S, pad), F32)], -1))


def _prep_w_in(w):
    a_end = (N_HEADS_A + 2 * N_KV_A) * HEAD_DIM
    b_end = a_end + MLA_Q_RANK + MLA_KV_RANK + MLA_ROPE
    cq = w[:, a_end:a_end + MLA_Q_RANK]
    ckv = w[:, a_end + MLA_Q_RANK:a_end + MLA_Q_RANK + MLA_KV_RANK]
    kr = w[:, b_end - MLA_ROPE:b_end]
    out = jnp.concatenate([
        w[:, :a_end],
        jnp.pad(cq, ((0, 0), (0, CQ_PAD - MLA_Q_RANK))),
        ckv,
        jnp.pad(kr, ((0, 0), (MLA_NOPE, LANES - MLA_NOPE - MLA_ROPE))),
        w[:, b_end:],
    ], axis=-1).astype(BF16)
    assert out.shape[1] == IN_COLS
    return out


def _twice(v):
    return jnp.tile(v, 2).reshape(1, 2 * v.shape[0])


def kernel(x, norm_attn, w_in, q_norm_a, k_norm_a, q_a_norm_b, w_uq_b, kv_a_norm_b, w_ukv_b,
           lambda_q1_c, lambda_k1_c, lambda_q2_c, lambda_k2_c, subln_c, w_out,
           norm_ffn, w_up, conv_w, conv_b, w_down, final_norm):
    B, S, D = x.shape
    depth = w_in.shape[0]
    assert D == D_MODEL and S % max(TM_IN, TM_OUT, TM_FFN, TQ, TK) == 0 and S % GRID_W == 0
    assert TM_IN == TK
    T = B * S

    tabs = _rope_tables(S)
    row = lambda v: v.reshape(1, -1)
    a_pairs = [(j, j + N_HEADS_A // 2) for j in range(N_HEADS_A // 2)]
    a_rows = np.concatenate([np.arange(h * HEAD_DIM, (h + 1) * HEAD_DIM)
                             for pair in a_pairs for h in pair])
    xc = x.reshape(T, D)
    for l in range(depth):
        lam_init = 0.8 - 0.6 * math.exp(-0.3 * l)
        wuq_p = jnp.pad(_pad_groups(w_uq_b[l], N_HEADS_B, MLA_NOPE + MLA_ROPE),
                        ((0, CQ_PAD - MLA_Q_RANK), (0, 0))).astype(BF16)
        wukv = w_ukv_b[l].reshape(MLA_KV_RANK, N_HEADS_B, MLA_NOPE + MLA_V)
        v_cols = [_pad_groups(wukv[:, h, MLA_NOPE:], 1, MLA_V, offset=HALF * (h % 2))
                  for h in range(N_HEADS_B)]
        wukv_p = jnp.concatenate(
            [_pad_groups(wukv[:, :, :MLA_NOPE].reshape(MLA_KV_RANK, -1), N_HEADS_B, MLA_NOPE)]
            + v_cols, axis=-1).astype(BF16)

        qa, kta, va, qb, ktb, vb, qc, ktc, vc, nrm = _in_proj(
            xc, row(norm_attn[l]), _prep_w_in(w_in[l]), tabs,
            _twice(q_norm_a[l]), _twice(k_norm_a[l]),
            jnp.pad(q_a_norm_b[l], (0, CQ_PAD - MLA_Q_RANK)).reshape(1, CQ_PAD),
            row(kv_a_norm_b[l]), wuq_p, wukv_p, B, S)

        nrm = jnp.max(nrm.reshape(B, -1, LANES), axis=1)

        def bounded(q0, nq, q_rep, k0, nk, k_rep):
            q2 = jnp.repeat(nrm[:, q0:q0 + nq], q_rep, axis=1)
            k2 = jnp.repeat(nrm[:, k0:k0 + nk], k_rep, axis=1)
            ok = q2 * k2 * NRM_MARGIN <= SAFE_LOG2_RANGE ** 2
            return jnp.all(ok.reshape(B, -1, HEADS_PER_PIPELINE), axis=-1).astype(jnp.int32)

        qk_a = (HEAD_DIM * jnp.max(jnp.abs(q_norm_a[l])) * jnp.max(jnp.abs(k_norm_a[l]))
                * (HEAD_DIM ** -0.5 * LOG2E))
        bounded_a = jnp.broadcast_to(
            (qk_a * NRM_MARGIN <= SAFE_LOG2_RANGE).astype(jnp.int32),
            (B, N_HEADS_A // HEADS_PER_PIPELINE))

        group = N_HEADS_A // N_KV_A
        oa = _attention(
            functools.partial(_attn_ab_kernel, n_heads=N_HEADS_A, q_index=lambda h: h // 2,
                              kt_index=lambda h: 2 * (h // group) + h % 2,
                              v_index=lambda h: h // group, out_pairs=a_pairs, n_keys=S),
            bounded_a, qa, kta, va, [], N_HEADS_A * HEAD_DIM, B, S)
        same = lambda h: h
        ob = _attention(
            functools.partial(_attn_ab_kernel, n_heads=N_HEADS_B, q_index=same, kt_index=same,
                              v_index=same,
                              out_pairs=[(2 * j, 2 * j + 1) for j in range(N_HEADS_B // 2)],
                              n_keys=S),
            bounded(NRM_QB, N_HEADS_B, 1, NRM_KB, N_HEADS_B, 1),
            qb, ktb, vb, [], N_HEADS_B * MLA_V, B, S)
        oc = _attention(
            functools.partial(_attn_c_kernel, n_keys=S, lam_init=lam_init),
            bounded(NRM_QC, N_HEADS_C, 2, NRM_KC, 2 * N_HEADS_C, 1),
            qc, ktc, vc,
            [row(lambda_q1_c[l]), row(lambda_k1_c[l]), row(lambda_q2_c[l]), row(lambda_k2_c[l]),
             _twice(subln_c[l])],
            N_HEADS_C * DIFF_V, B, S)

        w_out_l = jnp.concatenate([w_out[l][a_rows], w_out[l][N_HEADS_A * HEAD_DIM:]], axis=0)
        x1, h2 = _out_proj(xc, oa, ob, oc, w_out_l.astype(BF16), row(norm_ffn[l]))
        xc = _ffn(h2, x1, w_up[l].astype(BF16),
                  conv_w[l].reshape(3, 2, D_FF).transpose(1, 0, 2),
                  conv_b[l].reshape(2, 1, D_FF), w_down[l].astype(BF16),
                  row(final_norm), S, final_norm=(l == depth - 1))
    return xc.reshape(B, S, D)
```

```python
import functools
import math

import jax
import jax.numpy as jnp
import numpy as np
from jax import lax
from jax.experimental import pallas as pl
from jax.experimental.pallas import tpu as pltpu

D_MODEL = 1024
GRID_W = 64
HEAD_DIM = 64
N_HEADS_A = 8
N_KV_A = 2
N_HEADS_B = 4
MLA_Q_RANK = 192
MLA_KV_RANK = 128
MLA_NOPE = 64
MLA_ROPE = 32
MLA_V = 64
N_HEADS_C = 4
DIFF_QK = 32
DIFF_V = 64
D_FF = 2816
ROPE_BASE = 10000.0
EPS = 1e-6

LANES = 128
HALF = LANES // 2
BF16_SUBLANES = 16
VMEM_LIMIT_BYTES = 56 * 1024 * 1024

LOG2E = math.log2(math.e)
assert HEAD_DIM == MLA_V == DIFF_V == 2 * DIFF_QK == HALF

OFF_QA = 0
OFF_KA = OFF_QA + N_HEADS_A * HEAD_DIM
OFF_VA = OFF_KA + LANES
OFF_CQ = OFF_VA + LANES
CQ_PAD = 2 * LANES
OFF_CKV = OFF_CQ + CQ_PAD
OFF_KR = OFF_CKV + MLA_KV_RANK
OFF_QC = OFF_KR + LANES
OFF_KC = OFF_QC + N_HEADS_C * HALF
OFF_VC = OFF_KC + N_HEADS_C * HALF
IN_COLS = OFF_VC + N_HEADS_C * HALF

NRM_QB = 0
NRM_KB = NRM_QB + N_HEADS_B
NRM_QC = NRM_KB + N_HEADS_B
NRM_KC = NRM_QC + N_HEADS_C
NRM_GROUPS = 2 * N_HEADS_B + N_HEADS_C
NRM_MARGIN = 1.03
SAFE_LOG2_RANGE = 55.0

F32 = jnp.float32
BF16 = jnp.bfloat16

TM_IN = 512
TM_FFN = 512
FFN_CHUNK = 256
TQ = 512
TQ_SUB_AB = 512
TQ_SUB_C = 256
TK = 512
SCORE_LOOKAHEAD = 2
HEADS_PER_PIPELINE = 4


def _dot(a, b):
    return jnp.dot(a, b, preferred_element_type=F32)


def _rms(x, n):
    ms = jnp.sum(x * x, axis=-1, keepdims=True) * (1.0 / n)
    return x * lax.rsqrt(ms + EPS)


def _low_lanes():
    return lax.broadcasted_iota(jnp.int32, (1, LANES), 1) < HALF


def _half_rms(x):
    low = _low_lanes()
    sq = x * x
    s_lo = jnp.sum(jnp.where(low, sq, 0.0), axis=-1, keepdims=True)
    s_hi = jnp.sum(jnp.where(low, 0.0, sq), axis=-1, keepdims=True)
    return x * lax.rsqrt(jnp.where(low, s_lo, s_hi) * (1.0 / HALF) + EPS)


def _in_kernel(x_ref, gat_ref, w_ref, cos_a_ref, sin_a_ref, cos_b_ref, sin_b_ref,
               qg_a_ref, kg_a_ref, qg_b_ref, kvg_b_ref, wuq_ref, wukv_ref, sel_ref,
               qa_ref, kta_ref, va_ref, qb_ref, ktb_ref, vb_ref, qc_ref, ktc_ref, vc_ref, nrm_ref):
    x = x_ref[...]
    tm = x.shape[0]
    h = (_rms(x, D_MODEL) * gat_ref[...]).astype(BF16)

    lane = lax.broadcasted_iota(jnp.int32, (1, LANES), 1)
    low = lane < HALF
    row = lax.broadcasted_iota(jnp.int32, (LANES, 1), 0)

    z = _dot(h, w_ref[...])

    def seg(off, width):
        return z[:, off:off + width]

    def swap(y, nf):
        up = pltpu.roll(y, LANES - nf, axis=1)
        down = pltpu.roll(y, nf, axis=1)
        return jnp.where((lane & nf) == 0, up, down)

    def rope(y, c, s, nf):
        return y * c + swap(y, nf) * s

    def group(z, g):
        return z[:, g * LANES:(g + 1) * LANES]

    squares = []

    def note(y):
        squares.append((y * y).astype(BF16))

    def noted_row_sums(first):
        return _dot(jnp.concatenate(squares[first:], axis=-1),
                    sel_ref[first * LANES:len(squares) * LANES])

    cos_b, sin_b = cos_b_ref[...], sin_b_ref[...]
    nf_b = MLA_ROPE // 4
    cq = (_rms(seg(OFF_CQ, CQ_PAD), MLA_Q_RANK) * qg_b_ref[...]).astype(BF16)
    qb = _dot(cq, wuq_ref[...])
    scale_b = (MLA_NOPE + MLA_ROPE) ** -0.5 * LOG2E
    for hh in range(N_HEADS_B):
        y = rope(group(qb, hh), cos_b, sin_b, nf_b) * scale_b
        qb_ref[0, hh] = y.astype(BF16)
        note(y)
    ckv = (_rms(seg(OFF_CKV, MLA_KV_RANK), MLA_KV_RANK) * kvg_b_ref[...]).astype(BF16)
    kvb = _dot(ckv, wukv_ref[...])
    kr = rope(seg(OFF_KR, LANES), cos_b, sin_b, nf_b)
    for hh in range(N_HEADS_B):
        y = group(kvb, hh) + kr
        ktb_ref[0, hh] = y.T.astype(BF16)
        note(y)
        ones_other_half = jnp.where(low, float(hh % 2), float(1 - hh % 2))
        vb_ref[0, hh] = (group(kvb, N_HEADS_B + hh) + ones_other_half).astype(BF16)
    n2 = noted_row_sums(0)
    n_noted_b = len(squares)

    zq = seg(OFF_QC, N_HEADS_C * HALF)
    zk = seg(OFF_KC, N_HEADS_C * HALF)
    zv = seg(OFF_VC, N_HEADS_C * HALF)
    k_groups = []
    for p in range(N_HEADS_C // 2):
        y = group(zq, p) * (DIFF_QK ** -0.5 * LOG2E)
        qc_ref[0, p] = y.astype(BF16)
        note(y)
        k_groups.append(group(zk, p))
        kt = group(zk, p).T
        for sub in range(4):
            mine = (row >= sub * DIFF_QK) & (row < (sub + 1) * DIFF_QK)
            ktc_ref[0, 4 * p + sub, 0] = jnp.where(mine, kt, 0.0).astype(BF16)
        vz = group(zv, p)
        vc_ref[0, 2 * p] = jnp.where(low, vz, 1.0).astype(BF16)
        vc_ref[0, 2 * p + 1] = jnp.where(low, 1.0, vz).astype(BF16)
    for kz in k_groups:
        note(kz)
    n2 = n2 + noted_row_sums(n_noted_b)
    nrm_ref[0] = jnp.max(n2, axis=0, keepdims=True)

    cos_a, sin_a = cos_a_ref[...], sin_a_ref[...]
    nf_a = HEAD_DIM // 4
    zq = seg(OFF_QA, N_HEADS_A * HEAD_DIM)
    for p in range(N_HEADS_A // 2):
        y = rope(_half_rms(group(zq, p)) * qg_a_ref[...], cos_a, sin_a, nf_a)
        qa_ref[0, p] = (y * (HEAD_DIM ** -0.5 * LOG2E)).astype(BF16)
    y = rope(_half_rms(seg(OFF_KA, LANES)) * kg_a_ref[...], cos_a, sin_a, nf_a)
    kt = y.T.astype(BF16)
    zeros = jnp.zeros((HALF, tm), BF16)
    for g in range(N_KV_A):
        kg = kt[g * HALF:(g + 1) * HALF]
        kta_ref[0, 2 * g] = jnp.concatenate([kg, zeros], axis=0)
        kta_ref[0, 2 * g + 1] = jnp.concatenate([zeros, kg], axis=0)
    zv = seg(OFF_VA, LANES)
    va_ref[0, 0] = jnp.where(low, zv, 1.0).astype(BF16)
    va_ref[0, 1] = jnp.where(low, 1.0, zv).astype(BF16)


def _norm_routing():
    sel = np.zeros((NRM_GROUPS, LANES, LANES), np.float32)
    g = 0
    for slot in ([NRM_QB + h for h in range(N_HEADS_B)] + [NRM_KB + h for h in range(N_HEADS_B)]):
        sel[g, :, slot] = 1.0
        g += 1
    for p in range(N_HEADS_C // 2):
        for f in range(2):
            sel[g, f * HALF:(f + 1) * HALF, NRM_QC + 2 * p + f] = 1.0
        g += 1
    for p in range(N_HEADS_C // 2):
        for sub in range(4):
            sel[g, sub * DIFF_QK:(sub + 1) * DIFF_QK, NRM_KC + 4 * p + sub] = 1.0
        g += 1
    assert g == NRM_GROUPS
    return jnp.asarray(sel.reshape(NRM_GROUPS * LANES, LANES), BF16)


def _in_proj(x2d, gat, w_in_r, tabs, qg_a, kg_a, qg_b, kvg_b, wuq_p, wukv_p, B, S):
    T = B * S
    tm = TM_IN
    nst = S // tm
    const = lambda shape: pl.BlockSpec(shape, lambda i: (0,) * len(shape))
    tab = pl.BlockSpec((tm, LANES), lambda i: (i % nst, 0))
    hm = lambda H: pl.BlockSpec((1, H, tm, LANES), lambda i: (i // nst, 0, i % nst, 0))
    hmt = lambda H: pl.BlockSpec((1, H, LANES, tm), lambda i: (i // nst, 0, 0, i % nst))
    sds = lambda H: jax.ShapeDtypeStruct((B, H, S, LANES), BF16)
    sdt = lambda H: jax.ShapeDtypeStruct((B, H, LANES, S), BF16)
    return pl.pallas_call(
        _in_kernel,
        grid=(T // tm,),
        in_specs=[pl.BlockSpec((tm, D_MODEL), lambda i: (i, 0)),
                  const((1, D_MODEL)), const((D_MODEL, IN_COLS)),
                  tab, tab, tab, tab,
                  const((1, LANES)), const((1, LANES)), const((1, CQ_PAD)), const((1, MLA_KV_RANK)),
                  const((CQ_PAD, N_HEADS_B * LANES)), const((MLA_KV_RANK, 2 * N_HEADS_B * LANES)),
                  const((NRM_GROUPS * LANES, LANES))],
        out_specs=[hm(N_HEADS_A // 2), hmt(2 * N_KV_A), hm(N_KV_A),
                   hm(N_HEADS_B), hmt(N_HEADS_B), hm(N_HEADS_B),
                   hm(N_HEADS_C // 2),
                   pl.BlockSpec((1, 2 * N_HEADS_C, 1, LANES, tm),
                                lambda i: (i // nst, 0, i % nst, 0, 0)),
                   hm(N_HEADS_C),
                   pl.BlockSpec((1, 1, LANES), lambda i: (i, 0, 0))],
        out_shape=[sds(N_HEADS_A // 2), sdt(2 * N_KV_A), sds(N_KV_A),
                   sds(N_HEADS_B), sdt(N_HEADS_B), sds(N_HEADS_B),
                   sds(N_HEADS_C // 2),
                   jax.ShapeDtypeStruct((B, 2 * N_HEADS_C, S // tm, LANES, tm), BF16),
                   sds(N_HEADS_C),
                   jax.ShapeDtypeStruct((T // tm, 1, LANES), F32)],
        compiler_params=pltpu.CompilerParams(dimension_semantics=("arbitrary",),
                                             vmem_limit_bytes=VMEM_LIMIT_BYTES),
        name="in_proj",
    )(x2d, gat, w_in_r, *tabs, qg_a, kg_a, qg_b, kvg_b, wuq_p, wukv_p, _norm_routing())


def _lane_group_max(s):
    m = s[:, :LANES]
    for j in range(1, s.shape[1] // LANES):
        m = jnp.maximum(m, s[:, j * LANES:(j + 1) * LANES])
    return m


def _pv_bounded(streams, n_chunks):
    items = [(r, i) for r in range(n_chunks) for i in range(len(streams))]
    accs = [None] * len(streams)
    in_flight = {}
    for k in range(len(items) + SCORE_LOOKAHEAD):
        if k < len(items):
            r, i = items[k]
            scores, _, row_off = streams[i]
            s, off = scores(r), row_off(r)
            in_flight[k] = s if off is None else s + off
        if k >= SCORE_LOOKAHEAD:
            r, i = items[k - SCORE_LOOKAHEAD]
            d = _dot(jnp.exp2(in_flight.pop(k - SCORE_LOOKAHEAD)).astype(BF16), streams[i][1](r))
            accs[i] = d if accs[i] is None else accs[i] + d
    return accs


def _pv_exact(streams, n_chunks, s_ref):
    rows = s_ref.shape[0] // len(streams)
    maxes = []
    for i, (scores, _, row_off) in enumerate(streams):
        m_run = None
        for r in range(n_chunks):
            s = scores(r)
            s_ref[i * rows:(i + 1) * rows, r * TK:(r + 1) * TK] = s
            m, off = _lane_group_max(s), row_off(r)
            if off is not None:
                m = m + off
            m_run = m if m_run is None else jnp.maximum(m_run, m)
        maxes.append(jnp.max(m_run, axis=-1, keepdims=True))
    accs = []
    for i, (_, values, row_off) in enumerate(streams):
        acc = None
        for r in range(n_chunks):
            off = row_off(r)
            shift = maxes[i] if off is None else maxes[i] - off
            p = jnp.exp2(s_ref[i * rows:(i + 1) * rows, r * TK:(r + 1) * TK] - shift)
            d = _dot(p.astype(BF16), values(r))
            acc = d if acc is None else acc + d
        accs.append(acc)
    return accs


def _run_heads(bounded, heads, head_streams, n_chunks, s_ref, acc_ref):
    @pl.when(bounded)
    def _():
        per_head = [head_streams(h) for h in heads]
        n_sub = len(per_head[0])
        accs = _pv_bounded([s for streams in per_head for s in streams], n_chunks)
        for k, h in enumerate(heads):
            acc_ref[h] = jnp.concatenate(accs[k * n_sub:(k + 1) * n_sub], axis=0)

    @pl.when(jnp.logical_not(bounded))
    def _():
        for h in heads:
            acc_ref[h] = jnp.concatenate(_pv_exact(head_streams(h), n_chunks, s_ref), axis=0)


def _normalised_pair(acc_low, acc_high):
    low = _low_lanes()
    num = jnp.where(low, acc_low, acc_high)
    den = pltpu.roll(jnp.where(low, acc_high, acc_low), HALF, axis=1)
    return num / den


def _attn_ab_kernel(bounded_ref, q_ref, kt_ref, v_ref, o_ref, s_ref, acc_ref, *,
                    n_heads, q_index, kt_index, v_index, out_pairs, n_keys):
    n_chunks = n_keys // TK

    def head_streams(h):
        q = q_ref[0, q_index(h)]
        return [(lambda r, qs=q[r0:r0 + TQ_SUB_AB]:
                 _dot(qs, kt_ref[0, kt_index(h), :, r * TK:(r + 1) * TK]),
                 lambda r: v_ref[0, v_index(h), r * TK:(r + 1) * TK, :],
                 lambda r: None) for r0 in range(0, TQ, TQ_SUB_AB)]

    def head_group(j, carry):
        heads = [HEADS_PER_PIPELINE * j + k for k in range(HEADS_PER_PIPELINE)]
        _run_heads(bounded_ref[pl.program_id(0), j] != 0, heads, head_streams, n_chunks, s_ref,
                   acc_ref)
        return carry

    lax.fori_loop(0, n_heads // HEADS_PER_PIPELINE, head_group, 0)
    o_ref[...] = jnp.concatenate([_normalised_pair(acc_ref[a], acc_ref[b]) for a, b in out_pairs],
                                 axis=-1).astype(BF16)


def _attn_c_kernel(bounded_ref, q_ref, kt_ref, v_ref, lq1_ref, lk1_ref, lq2_ref, lk2_ref,
                   subln_ref, o_ref, s_ref, acc_ref, *, n_keys, lam_init):
    assert TQ == TK
    nc = n_keys // TK
    cd = pl.program_id(1)
    col = lax.broadcasted_iota(jnp.int32, (1, TK), 1).astype(F32)
    row = lax.broadcasted_iota(jnp.int32, (TQ_SUB_C, 1), 0).astype(F32)
    subs = [(r, r + TQ_SUB_C) for r in range(0, TQ, TQ_SUB_C)]
    neg_dist = [-jnp.abs((lax.broadcasted_iota(jnp.int32, (TQ_SUB_C, TK), 0) + r0
                          - lax.broadcasted_iota(jnp.int32, (TQ_SUB_C, TK), 1)).astype(F32))
                for r0, _ in subs]

    chunk = [(cd + r) % nc for r in range(nc)]

    def head_bias(hh):
        slope = jnp.float32(LOG2E * 2.0 ** (-8.0 * N_HEADS_C / N_HEADS_C))
        for k in range(N_HEADS_C - 2, -1, -1):
            slope = jnp.where(hh == k, LOG2E * 2.0 ** (-8.0 * (k + 1) / N_HEADS_C), slope)
        key_term, row_off = [None], [[None] * len(subs)]
        for r in range(1, nc):
            side = jnp.where(chunk[r] < cd, slope, -slope)
            key_term.append(col * side)
            base = -slope * (jnp.abs(chunk[r] - cd) * TK).astype(F32)
            row_off.append([base - side * (row + float(r0)) for r0, _ in subs])
        return slope, key_term, row_off

    def head_pair(p, carry):
        bias = [head_bias(2 * p + f) for f in range(2)]

        def streams_of(k):
            f, idx, hh = k // 2, 4 * p + k, 2 * p + k // 2
            slope, key_term, row_off = bias[f]

            def scores(r, si):
                s = _dot(q_ref[0, p][subs[si][0]:subs[si][1]], kt_ref[0, idx, chunk[r]])
                return s + (neg_dist[si] * slope if r == 0 else key_term[r])

            return [(functools.partial(scores, si=si),
                     lambda r: v_ref[0, hh, pl.ds(pl.multiple_of(chunk[r] * TK, TK), TK), :],
                     lambda r, si=si: row_off[r][si]) for si in range(len(subs))]

        _run_heads(bounded_ref[pl.program_id(0), p] != 0, list(range(4)), streams_of, nc, s_ref,
                   acc_ref.at[pl.ds(4 * p, 4)])
        return carry

    lax.fori_loop(0, N_HEADS_C // 2, head_pair, 0)

    lam = (jnp.exp(jnp.sum(lq1_ref[...] * lk1_ref[...], axis=-1, keepdims=True))
           - jnp.exp(jnp.sum(lq2_ref[...] * lk2_ref[...], axis=-1, keepdims=True)) + lam_init)
    outs = []
    for p in range(N_HEADS_C // 2):
        o = (_normalised_pair(acc_ref[4 * p], acc_ref[4 * p + 2])
             - lam * _normalised_pair(acc_ref[4 * p + 1], acc_ref[4 * p + 3]))
        outs.append(_half_rms(o) * subln_ref[...] * (1.0 - lam_init))
    o_ref[...] = jnp.concatenate(outs, axis=-1).astype(BF16)


def _attention(kernel_fn, bounded, q, kt, v, extras, out_width, B, S):
    nq = S // TQ
    hq, hv = q.shape[1], v.shape[1]
    extra_specs = [pl.BlockSpec(e.shape, lambda b, i, f, n=e.ndim: (0,) * n) for e in extras]
    return pl.pallas_call(
        kernel_fn,
        grid_spec=pltpu.PrefetchScalarGridSpec(
            num_scalar_prefetch=1,
            grid=(B, nq),
            in_specs=[pl.BlockSpec((1, hq, TQ, LANES), lambda b, i, f: (b, 0, i, 0)),
                      pl.BlockSpec((1,) + kt.shape[1:],
                                   lambda b, i, f, n=kt.ndim: (b,) + (0,) * (n - 1)),
                      pl.BlockSpec((1, hv, S, LANES), lambda b, i, f: (b, 0, 0, 0))] + extra_specs,
            out_specs=pl.BlockSpec((TQ, out_width), lambda b, i, f: (b * nq + i, 0)),
            scratch_shapes=[pltpu.VMEM((TQ, S), F32),
                            pltpu.VMEM((HEADS_PER_PIPELINE * bounded.shape[1], TQ, LANES), F32)]),
        out_shape=jax.ShapeDtypeStruct((B * S, out_width), BF16),
        compiler_params=pltpu.CompilerParams(dimension_semantics=("arbitrary", "arbitrary"),
                                             vmem_limit_bytes=VMEM_LIMIT_BYTES),
        name=kernel_fn.func.__name__.strip("_"),
    )(bounded, q, kt, v, *extras)


def _mlp_kernel(xp_ref, x_ref, xn_ref, ap_ref, a_ref, an_ref, bp_ref, b_ref, bn_ref,
                cp_ref, c_ref, cn_ref, wo_ref, g_ref, wu_ref, cw_ref, cb_ref, wd_ref, fg_ref,
                o_ref, act_ref, *, tiles_per_seq, final_norm):
    i = pl.program_id(0)
    tm = x_ref.shape[0]
    halo = xp_ref.shape[0]

    def rows(p, m, n):
        return jnp.concatenate([p[...], m[...], n[...]], axis=0)

    o_ext = jnp.concatenate([rows(ap_ref, a_ref, an_ref), rows(bp_ref, b_ref, bn_ref),
                             rows(cp_ref, c_ref, cn_ref)], axis=1)
    x1 = rows(xp_ref, x_ref, xn_ref) + _dot(o_ext, wo_ref[...])
    h = (_rms(x1, D_MODEL) * g_ref[...]).astype(BF16)
    at_start = (i % tiles_per_seq) == 0
    at_end = (i % tiles_per_seq) == tiles_per_seq - 1
    zeros = jnp.zeros((halo, D_MODEL), BF16)
    hext = jnp.concatenate([jnp.where(at_start, zeros, h[:halo]), h[halo:halo + tm],
                            jnp.where(at_end, zeros, h[halo + tm:])], axis=0)

    def branch(k, c0):
        u = _dot(hext, wu_ref[:, k * D_FF + c0:k * D_FF + c0 + FFN_CHUNK])
        cw = cw_ref[k, :, c0:c0 + FFN_CHUNK]
        return (u[halo - 1:halo - 1 + tm] * cw[0:1] + u[halo:halo + tm] * cw[1:2]
                + u[halo + 1:halo + 1 + tm] * cw[2:3] + cb_ref[k, :, c0:c0 + FFN_CHUNK])

    for c0 in range(0, D_FF, FFN_CHUNK):
        g = branch(0, c0)
        val = branch(1, c0)
        act_ref[:, c0:c0 + FFN_CHUNK] = (g / (1.0 + jnp.exp(-g)) * val).astype(BF16)

    y = x1[halo:halo + tm] + _dot(act_ref[...], wd_ref[...])
    if final_norm:
        y = _rms(y, D_MODEL) * fg_ref[...]
    o_ref[...] = y


def _mlp(x2d, oa, ob, oc, w_out, g_ffn, w_up, conv_w, conv_b, w_down, final_g, S, final_norm):
    T = x2d.shape[0]
    tm, halo = TM_FFN, BF16_SUBLANES
    per = tm // halo
    last = T // halo - 1
    resident = lambda shape: pl.BlockSpec(shape, lambda i: (0,) * len(shape),
                                          pipeline_mode=pl.Buffered(1))

    def with_halos(width):
        return [pl.BlockSpec((halo, width), lambda i: (jnp.maximum(i * per - 1, 0), 0)),
                pl.BlockSpec((tm, width), lambda i: (i, 0)),
                pl.BlockSpec((halo, width), lambda i: (jnp.minimum((i + 1) * per, last), 0))]

    return pl.pallas_call(
        functools.partial(_mlp_kernel, tiles_per_seq=S // tm, final_norm=final_norm),
        grid=(T // tm,),
        in_specs=(with_halos(D_MODEL) + with_halos(oa.shape[1]) + with_halos(ob.shape[1])
                  + with_halos(oc.shape[1])
                  + [resident(w_out.shape), resident((1, D_MODEL)),
                     resident((D_MODEL, 2 * D_FF)), resident((2, 3, D_FF)), resident((2, 1, D_FF)),
                     resident((D_FF, D_MODEL)), resident((1, D_MODEL))]),
        out_specs=pl.BlockSpec((tm, D_MODEL), lambda i: (i, 0)),
        out_shape=jax.ShapeDtypeStruct((T, D_MODEL), F32),
        scratch_shapes=[pltpu.VMEM((tm, D_FF), BF16)],
        compiler_params=pltpu.CompilerParams(dimension_semantics=("arbitrary",),
                                             vmem_limit_bytes=VMEM_LIMIT_BYTES),
        name="out_proj_conv_mlp",
    )(x2d, x2d, x2d, oa, oa, oa, ob, ob, ob, oc, oc, oc, w_out, g_ffn,
      w_up, conv_w, conv_b, w_down, final_g)


def _pad_groups(w, n_groups, width, offset=0):
    r = w.shape[0]
    w = w.reshape(r, n_groups, width)
    w = jnp.pad(w, ((0, 0), (0, 0), (offset, LANES - width - offset)))
    return w.reshape(r, n_groups * LANES)


def _rope_core(S, dim):
    nf = dim // 4
    t = jnp.arange(S, dtype=jnp.int32)
    rows = (t // GRID_W).astype(F32)
    cols = (t % GRID_W).astype(F32)
    inv = ROPE_BASE ** (-jnp.arange(nf, dtype=F32) / nf)
    ar = rows[:, None] * inv
    ac = cols[:, None] * inv
    cos = jnp.concatenate([jnp.cos(ar), jnp.cos(ar), jnp.cos(ac), jnp.cos(ac)], axis=-1)
    sin = jnp.concatenate([-jnp.sin(ar), jnp.sin(ar), -jnp.sin(ac), jnp.sin(ac)], axis=-1)
    return cos, sin


def _rope_tables(S):
    cos_a, sin_a = _rope_core(S, HEAD_DIM)
    cos_b, sin_b = _rope_core(S, MLA_ROPE)
    pad = LANES - MLA_NOPE - MLA_ROPE
    return (jnp.tile(cos_a, (1, 2)), jnp.tile(sin_a, (1, 2)),
            jnp.concatenate([jnp.ones((S, MLA_NOPE), F32), cos_b, jnp.zeros((S, pad), F32)], -1),
            jnp.concatenate([jnp.zeros((S, MLA_NOPE), F32), sin_b, jnp.zeros((S, pad), F32)], -1))


def _prep_w_in(w):
    a_end = (N_HEADS_A + 2 * N_KV_A) * HEAD_DIM
    b_end = a_end + MLA_Q_RANK + MLA_KV_RANK + MLA_ROPE
    cq = w[:, a_end:a_end + MLA_Q_RANK]
    ckv = w[:, a_end + MLA_Q_RANK:a_end + MLA_Q_RANK + MLA_KV_RANK]
    kr = w[:, b_end - MLA_ROPE:b_end]
    out = jnp.concatenate([
        w[:, :a_end],
        jnp.pad(cq, ((0, 0), (0, CQ_PAD - MLA_Q_RANK))),
        ckv,
        jnp.pad(kr, ((0, 0), (MLA_NOPE, LANES - MLA_NOPE - MLA_ROPE))),
        w[:, b_end:],
    ], axis=-1).astype(BF16)
    assert out.shape[1] == IN_COLS
    return out


def _twice(v):
    return jnp.tile(v, 2).reshape(1, 2 * v.shape[0])


def kernel(x, norm_attn, w_in, q_norm_a, k_norm_a, q_a_norm_b, w_uq_b, kv_a_norm_b, w_ukv_b,
           lambda_q1_c, lambda_k1_c, lambda_q2_c, lambda_k2_c, subln_c, w_out,
           norm_ffn, w_up, conv_w, conv_b, w_down, final_norm):
    B, S, D = x.shape
    depth = w_in.shape[0]
    assert D == D_MODEL and S % max(TM_IN, TM_FFN, TQ, TK) == 0 and S % GRID_W == 0
    assert TM_IN == TK
    T = B * S

    tabs = _rope_tables(S)
    row = lambda v: v.reshape(1, -1)
    a_pairs = [(j, j + N_HEADS_A // 2) for j in range(N_HEADS_A // 2)]
    a_rows = np.concatenate([np.arange(h * HEAD_DIM, (h + 1) * HEAD_DIM)
                             for pair in a_pairs for h in pair])
    xc = x.reshape(T, D)
    for l in range(depth):
        lam_init = 0.8 - 0.6 * math.exp(-0.3 * l)
        wuq_p = jnp.pad(_pad_groups(w_uq_b[l], N_HEADS_B, MLA_NOPE + MLA_ROPE),
                        ((0, CQ_PAD - MLA_Q_RANK), (0, 0))).astype(BF16)
        wukv = w_ukv_b[l].reshape(MLA_KV_RANK, N_HEADS_B, MLA_NOPE + MLA_V)
        v_cols = [_pad_groups(wukv[:, h, MLA_NOPE:], 1, MLA_V, offset=HALF * (h % 2))
                  for h in range(N_HEADS_B)]
        wukv_p = jnp.concatenate(
            [_pad_groups(wukv[:, :, :MLA_NOPE].reshape(MLA_KV_RANK, -1), N_HEADS_B, MLA_NOPE)]
            + v_cols, axis=-1).astype(BF16)

        qa, kta, va, qb, ktb, vb, qc, ktc, vc, nrm = _in_proj(
            xc, row(norm_attn[l]), _prep_w_in(w_in[l]), tabs,
            _twice(q_norm_a[l]), _twice(k_norm_a[l]),
            jnp.pad(q_a_norm_b[l], (0, CQ_PAD - MLA_Q_RANK)).reshape(1, CQ_PAD),
            row(kv_a_norm_b[l]), wuq_p, wukv_p, B, S)

        nrm = jnp.max(nrm.reshape(B, -1, LANES), axis=1)

        def bounded(q0, nq, q_rep, k0, nk, k_rep):
            q2 = jnp.repeat(nrm[:, q0:q0 + nq], q_rep, axis=1)
            k2 = jnp.repeat(nrm[:, k0:k0 + nk], k_rep, axis=1)
            ok = q2 * k2 * NRM_MARGIN <= SAFE_LOG2_RANGE ** 2
            return jnp.all(ok.reshape(B, -1, HEADS_PER_PIPELINE), axis=-1).astype(jnp.int32)

        qk_a = (HEAD_DIM * jnp.max(jnp.abs(q_norm_a[l])) * jnp.max(jnp.abs(k_norm_a[l]))
                * (HEAD_DIM ** -0.5 * LOG2E))
        bounded_a = jnp.broadcast_to(
            (qk_a * NRM_MARGIN <= SAFE_LOG2_RANGE).astype(jnp.int32),
            (B, N_HEADS_A // HEADS_PER_PIPELINE))

        group = N_HEADS_A // N_KV_A
        oa = _attention(
            functools.partial(_attn_ab_kernel, n_heads=N_HEADS_A, q_index=lambda h: h // 2,
                              kt_index=lambda h: 2 * (h // group) + h % 2,
                              v_index=lambda h: h // group, out_pairs=a_pairs, n_keys=S),
            bounded_a, qa, kta, va, [], N_HEADS_A * HEAD_DIM, B, S)
        same = lambda h: h
        ob = _attention(
            functools.partial(_attn_ab_kernel, n_heads=N_HEADS_B, q_index=same, kt_index=same,
                              v_index=same,
                              out_pairs=[(2 * j, 2 * j + 1) for j in range(N_HEADS_B // 2)],
                              n_keys=S),
            bounded(NRM_QB, N_HEADS_B, 1, NRM_KB, N_HEADS_B, 1),
            qb, ktb, vb, [], N_HEADS_B * MLA_V, B, S)
        oc = _attention(
            functools.partial(_attn_c_kernel, n_keys=S, lam_init=lam_init),
            bounded(NRM_QC, N_HEADS_C, 2, NRM_KC, 2 * N_HEADS_C, 1),
            qc, ktc, vc,
            [row(lambda_q1_c[l]), row(lambda_k1_c[l]), row(lambda_q2_c[l]), row(lambda_k2_c[l]),
             _twice(subln_c[l])],
            N_HEADS_C * DIFF_V, B, S)

        w_out_l = jnp.concatenate([w_out[l][a_rows], w_out[l][N_HEADS_A * HEAD_DIM:]], axis=0)
        xc = _mlp(xc, oa, ob, oc, w_out_l.astype(BF16), row(norm_ffn[l]), w_up[l].astype(BF16),
                  conv_w[l].reshape(3, 2, D_FF).transpose(1, 0, 2),
                  conv_b[l].reshape(2, 1, D_FF), w_down[l].astype(BF16),
                  row(final_norm), S, final_norm=(l == depth - 1))
    return xc.reshape(B, S, D)
```

```python
import functools
import math

import jax
import jax.numpy as jnp
import numpy as np
from jax import lax
from jax.experimental import pallas as pl
from jax.experimental.pallas import tpu as pltpu

D_MODEL = 1024
GRID_W = 64
HEAD_DIM = 64
N_HEADS_A = 8
N_KV_A = 2
N_HEADS_B = 4
MLA_Q_RANK = 192
MLA_KV_RANK = 128
MLA_NOPE = 64
MLA_ROPE = 32
MLA_V = 64
N_HEADS_C = 4
DIFF_QK = 32
DIFF_V = 64
D_FF = 2816
ROPE_BASE = 10000.0
EPS = 1e-6

LANES = 128
HALF = LANES // 2
BF16_SUBLANES = 16
VMEM_LIMIT_BYTES = 56 * 1024 * 1024

LOG2E = math.log2(math.e)
assert HEAD_DIM == MLA_V == DIFF_V == 2 * DIFF_QK == HALF

OFF_QA = 0
OFF_KA = OFF_QA + N_HEADS_A * HEAD_DIM
OFF_VA = OFF_KA + LANES
OFF_CQ = OFF_VA + LANES
CQ_PAD = 2 * LANES
OFF_CKV = OFF_CQ + CQ_PAD
OFF_KR = OFF_CKV + MLA_KV_RANK
OFF_QC = OFF_KR + LANES
OFF_KC = OFF_QC + N_HEADS_C * HALF
OFF_VC = OFF_KC + N_HEADS_C * HALF
IN_COLS = OFF_VC + N_HEADS_C * HALF

NRM_QB = 0
NRM_KB = NRM_QB + N_HEADS_B
NRM_QC = NRM_KB + N_HEADS_B
NRM_KC = NRM_QC + N_HEADS_C
NRM_GROUPS = 2 * N_HEADS_B + N_HEADS_C
NRM_MARGIN = 1.03
SAFE_LOG2_RANGE = 55.0

F32 = jnp.float32
BF16 = jnp.bfloat16

TM_IN = 512
TM_FFN = 512
FFN_CHUNK = 256
FFN_EARLY_DOWN = 6 * FFN_CHUNK
FFN_DOWN_LAG = 2
TQ = 512
TQ_SUB_AB = 512
TQ_SUB_C = 256
TK = 512
SCORE_LOOKAHEAD = 2
HEADS_PER_PIPELINE = 4


def _dot(a, b):
    return jnp.dot(a, b, preferred_element_type=F32)


def _rms(x, n):
    ms = jnp.sum(x * x, axis=-1, keepdims=True) * (1.0 / n)
    return x * lax.rsqrt(ms + EPS)


def _low_lanes():
    return lax.broadcasted_iota(jnp.int32, (1, LANES), 1) < HALF


def _half_rms(x):
    low = _low_lanes()
    sq = x * x
    s_lo = jnp.sum(jnp.where(low, sq, 0.0), axis=-1, keepdims=True)
    s_hi = jnp.sum(jnp.where(low, 0.0, sq), axis=-1, keepdims=True)
    return x * lax.rsqrt(jnp.where(low, s_lo, s_hi) * (1.0 / HALF) + EPS)


def _in_kernel(x_ref, gat_ref, w_ref, cos_a_ref, sin_a_ref, cos_b_ref, sin_b_ref,
               qg_a_ref, kg_a_ref, qg_b_ref, kvg_b_ref, wuq_ref, wukv_ref, sel_ref,
               qa_ref, kta_ref, va_ref, qb_ref, ktb_ref, vb_ref, qc_ref, ktc_ref, vc_ref, nrm_ref):
    x = x_ref[...]
    tm = x.shape[0]
    h = (_rms(x, D_MODEL) * gat_ref[...]).astype(BF16)

    lane = lax.broadcasted_iota(jnp.int32, (1, LANES), 1)
    low = lane < HALF
    row = lax.broadcasted_iota(jnp.int32, (LANES, 1), 0)

    z = _dot(h, w_ref[...])

    def seg(off, width):
        return z[:, off:off + width]

    def swap(y, nf):
        up = pltpu.roll(y, LANES - nf, axis=1)
        down = pltpu.roll(y, nf, axis=1)
        return jnp.where((lane & nf) == 0, up, down)

    def rope(y, c, s, nf):
        return y * c + swap(y, nf) * s

    def group(z, g):
        return z[:, g * LANES:(g + 1) * LANES]

    squares = []

    def note(y):
        squares.append((y * y).astype(BF16))

    def noted_row_sums(first):
        return _dot(jnp.concatenate(squares[first:], axis=-1),
                    sel_ref[first * LANES:len(squares) * LANES])

    cos_b, sin_b = cos_b_ref[...], sin_b_ref[...]
    nf_b = MLA_ROPE // 4
    cq = (_rms(seg(OFF_CQ, CQ_PAD), MLA_Q_RANK) * qg_b_ref[...]).astype(BF16)
    qb = _dot(cq, wuq_ref[...])
    scale_b = (MLA_NOPE + MLA_ROPE) ** -0.5 * LOG2E
    for hh in range(N_HEADS_B):
        y = rope(group(qb, hh), cos_b, sin_b, nf_b) * scale_b
        qb_ref[0, hh] = y.astype(BF16)
        note(y)
    ckv = (_rms(seg(OFF_CKV, MLA_KV_RANK), MLA_KV_RANK) * kvg_b_ref[...]).astype(BF16)
    kvb = _dot(ckv, wukv_ref[...])
    kr = rope(seg(OFF_KR, LANES), cos_b, sin_b, nf_b)
    for hh in range(N_HEADS_B):
        y = group(kvb, hh) + kr
        ktb_ref[0, hh] = y.T.astype(BF16)
        note(y)
        ones_other_half = jnp.where(low, float(hh % 2), float(1 - hh % 2))
        vb_ref[0, hh] = (group(kvb, N_HEADS_B + hh) + ones_other_half).astype(BF16)
    n2 = noted_row_sums(0)
    n_noted_b = len(squares)

    zq = seg(OFF_QC, N_HEADS_C * HALF)
    zk = seg(OFF_KC, N_HEADS_C * HALF)
    zv = seg(OFF_VC, N_HEADS_C * HALF)
    k_groups = []
    for p in range(N_HEADS_C // 2):
        y = group(zq, p) * (DIFF_QK ** -0.5 * LOG2E)
        qc_ref[0, p] = y.astype(BF16)
        note(y)
        k_groups.append(group(zk, p))
        kt = group(zk, p).T
        for sub in range(4):
            mine = (row >= sub * DIFF_QK) & (row < (sub + 1) * DIFF_QK)
            ktc_ref[0, 4 * p + sub, 0] = jnp.where(mine, kt, 0.0).astype(BF16)
        vz = group(zv, p)
        vc_ref[0, 2 * p] = jnp.where(low, vz, 1.0).astype(BF16)
        vc_ref[0, 2 * p + 1] = jnp.where(low, 1.0, vz).astype(BF16)
    for kz in k_groups:
        note(kz)
    n2 = n2 + noted_row_sums(n_noted_b)
    nrm_ref[0] = jnp.max(n2, axis=0, keepdims=True)

    cos_a, sin_a = cos_a_ref[...], sin_a_ref[...]
    nf_a = HEAD_DIM // 4
    zq = seg(OFF_QA, N_HEADS_A * HEAD_DIM)
    for p in range(N_HEADS_A // 2):
        y = rope(_half_rms(group(zq, p)) * qg_a_ref[...], cos_a, sin_a, nf_a)
        qa_ref[0, p] = (y * (HEAD_DIM ** -0.5 * LOG2E)).astype(BF16)
    y = rope(_half_rms(seg(OFF_KA, LANES)) * kg_a_ref[...], cos_a, sin_a, nf_a)
    kt = y.T.astype(BF16)
    zeros = jnp.zeros((HALF, tm), BF16)
    for g in range(N_KV_A):
        kg = kt[g * HALF:(g + 1) * HALF]
        kta_ref[0, 2 * g] = jnp.concatenate([kg, zeros], axis=0)
        kta_ref[0, 2 * g + 1] = jnp.concatenate([zeros, kg], axis=0)
    zv = seg(OFF_VA, LANES)
    va_ref[0, 0] = jnp.where(low, zv, 1.0).astype(BF16)
    va_ref[0, 1] = jnp.where(low, 1.0, zv).astype(BF16)


def _norm_routing():
    sel = np.zeros((NRM_GROUPS, LANES, LANES), np.float32)
    g = 0
    for slot in ([NRM_QB + h for h in range(N_HEADS_B)] + [NRM_KB + h for h in range(N_HEADS_B)]):
        sel[g, :, slot] = 1.0
        g += 1
    for p in range(N_HEADS_C // 2):
        for f in range(2):
            sel[g, f * HALF:(f + 1) * HALF, NRM_QC + 2 * p + f] = 1.0
        g += 1
    for p in range(N_HEADS_C // 2):
        for sub in range(4):
            sel[g, sub * DIFF_QK:(sub + 1) * DIFF_QK, NRM_KC + 4 * p + sub] = 1.0
        g += 1
    assert g == NRM_GROUPS
    return jnp.asarray(sel.reshape(NRM_GROUPS * LANES, LANES), BF16)


def _in_proj(x2d, gat, w_in_r, tabs, qg_a, kg_a, qg_b, kvg_b, wuq_p, wukv_p, B, S):
    T = B * S
    tm = TM_IN
    nst = S // tm
    const = lambda shape: pl.BlockSpec(shape, lambda i: (0,) * len(shape))
    tab = pl.BlockSpec((tm, LANES), lambda i: (i % nst, 0))
    hm = lambda H: pl.BlockSpec((1, H, tm, LANES), lambda i: (i // nst, 0, i % nst, 0))
    hmt = lambda H: pl.BlockSpec((1, H, LANES, tm), lambda i: (i // nst, 0, 0, i % nst))
    sds = lambda H: jax.ShapeDtypeStruct((B, H, S, LANES), BF16)
    sdt = lambda H: jax.ShapeDtypeStruct((B, H, LANES, S), BF16)
    return pl.pallas_call(
        _in_kernel,
        grid=(T // tm,),
        in_specs=[pl.BlockSpec((tm, D_MODEL), lambda i: (i, 0)),
                  const((1, D_MODEL)), const((D_MODEL, IN_COLS)),
                  tab, tab, tab, tab,
                  const((1, LANES)), const((1, LANES)), const((1, CQ_PAD)), const((1, MLA_KV_RANK)),
                  const((CQ_PAD, N_HEADS_B * LANES)), const((MLA_KV_RANK, 2 * N_HEADS_B * LANES)),
                  const((NRM_GROUPS * LANES, LANES))],
        out_specs=[hm(N_HEADS_A // 2), hmt(2 * N_KV_A), hm(N_KV_A),
                   hm(N_HEADS_B), hmt(N_HEADS_B), hm(N_HEADS_B),
                   hm(N_HEADS_C // 2),
                   pl.BlockSpec((1, 2 * N_HEADS_C, 1, LANES, tm),
                                lambda i: (i // nst, 0, i % nst, 0, 0)),
                   hm(N_HEADS_C),
                   pl.BlockSpec((1, 1, LANES), lambda i: (i, 0, 0))],
        out_shape=[sds(N_HEADS_A // 2), sdt(2 * N_KV_A), sds(N_KV_A),
                   sds(N_HEADS_B), sdt(N_HEADS_B), sds(N_HEADS_B),
                   sds(N_HEADS_C // 2),
                   jax.ShapeDtypeStruct((B, 2 * N_HEADS_C, S // tm, LANES, tm), BF16),
                   sds(N_HEADS_C),
                   jax.ShapeDtypeStruct((T // tm, 1, LANES), F32)],
        compiler_params=pltpu.CompilerParams(dimension_semantics=("arbitrary",),
                                             vmem_limit_bytes=VMEM_LIMIT_BYTES),
        name="in_proj",
    )(x2d, gat, w_in_r, *tabs, qg_a, kg_a, qg_b, kvg_b, wuq_p, wukv_p, _norm_routing())


def _lane_group_max(s):
    m = s[:, :LANES]
    for j in range(1, s.shape[1] // LANES):
        m = jnp.maximum(m, s[:, j * LANES:(j + 1) * LANES])
    return m


def _pv_bounded(streams, n_chunks):
    items = [(r, i) for r in range(n_chunks) for i in range(len(streams))]
    accs = [None] * len(streams)
    in_flight = {}
    for k in range(len(items) + SCORE_LOOKAHEAD):
        if k < len(items):
            r, i = items[k]
            scores, _, row_off = streams[i]
            s, off = scores(r), row_off(r)
            in_flight[k] = s if off is None else s + off
        if k >= SCORE_LOOKAHEAD:
            r, i = items[k - SCORE_LOOKAHEAD]
            d = _dot(jnp.exp2(in_flight.pop(k - SCORE_LOOKAHEAD)).astype(BF16), streams[i][1](r))
            accs[i] = d if accs[i] is None else accs[i] + d
    return accs


def _pv_exact(streams, n_chunks, s_ref):
    rows = s_ref.shape[0] // len(streams)
    maxes = []
    for i, (scores, _, row_off) in enumerate(streams):
        m_run = None
        for r in range(n_chunks):
            s = scores(r)
            s_ref[i * rows:(i + 1) * rows, r * TK:(r + 1) * TK] = s
            m, off = _lane_group_max(s), row_off(r)
            if off is not None:
                m = m + off
            m_run = m if m_run is None else jnp.maximum(m_run, m)
        maxes.append(jnp.max(m_run, axis=-1, keepdims=True))
    accs = []
    for i, (_, values, row_off) in enumerate(streams):
        acc = None
        for r in range(n_chunks):
            off = row_off(r)
            shift = maxes[i] if off is None else maxes[i] - off
            p = jnp.exp2(s_ref[i * rows:(i + 1) * rows, r * TK:(r + 1) * TK] - shift)
            d = _dot(p.astype(BF16), values(r))
            acc = d if acc is None else acc + d
        accs.append(acc)
    return accs


def _run_heads(bounded, heads, head_streams, n_chunks, s_ref, acc_ref):
    @pl.when(bounded)
    def _():
        per_head = [head_streams(h) for h in heads]
        n_sub = len(per_head[0])
        accs = _pv_bounded([s for streams in per_head for s in streams], n_chunks)
        for k, h in enumerate(heads):
            acc_ref[h] = jnp.concatenate(accs[k * n_sub:(k + 1) * n_sub], axis=0)

    @pl.when(jnp.logical_not(bounded))
    def _():
        for h in heads:
            acc_ref[h] = jnp.concatenate(_pv_exact(head_streams(h), n_chunks, s_ref), axis=0)


def _normalised_pair(acc_low, acc_high):
    low = _low_lanes()
    num = jnp.where(low, acc_low, acc_high)
    den = pltpu.roll(jnp.where(low, acc_high, acc_low), HALF, axis=1)
    return num / den


def _attn_ab_kernel(bounded_ref, q_ref, kt_ref, v_ref, o_ref, s_ref, acc_ref, *,
                    n_heads, q_index, kt_index, v_index, out_pairs, n_keys):
    n_chunks = n_keys // TK

    def head_streams(h):
        q = q_ref[0, q_index(h)]
        return [(lambda r, qs=q[r0:r0 + TQ_SUB_AB]:
                 _dot(qs, kt_ref[0, kt_index(h), :, r * TK:(r + 1) * TK]),
                 lambda r: v_ref[0, v_index(h), r * TK:(r + 1) * TK, :],
                 lambda r: None) for r0 in range(0, TQ, TQ_SUB_AB)]

    def head_group(j, carry):
        heads = [HEADS_PER_PIPELINE * j + k for k in range(HEADS_PER_PIPELINE)]
        _run_heads(bounded_ref[pl.program_id(0), j] != 0, heads, head_streams, n_chunks, s_ref,
                   acc_ref)
        return carry

    lax.fori_loop(0, n_heads // HEADS_PER_PIPELINE, head_group, 0)
    o_ref[...] = jnp.concatenate([_normalised_pair(acc_ref[a], acc_ref[b]) for a, b in out_pairs],
                                 axis=-1).astype(BF16)


def _attn_c_kernel(bounded_ref, q_ref, kt_ref, v_ref, lq1_ref, lk1_ref, lq2_ref, lk2_ref,
                   subln_ref, o_ref, s_ref, acc_ref, *, n_keys, lam_init):
    assert TQ == TK
    nc = n_keys // TK
    cd = pl.program_id(1)
    col = lax.broadcasted_iota(jnp.int32, (1, TK), 1).astype(F32)
    row = lax.broadcasted_iota(jnp.int32, (TQ_SUB_C, 1), 0).astype(F32)
    subs = [(r, r + TQ_SUB_C) for r in range(0, TQ, TQ_SUB_C)]
    neg_dist = [-jnp.abs((lax.broadcasted_iota(jnp.int32, (TQ_SUB_C, TK), 0) + r0
                          - lax.broadcasted_iota(jnp.int32, (TQ_SUB_C, TK), 1)).astype(F32))
                for r0, _ in subs]

    chunk = [(cd + r) % nc for r in range(nc)]

    def head_bias(hh):
        slope = jnp.float32(LOG2E * 2.0 ** (-8.0 * N_HEADS_C / N_HEADS_C))
        for k in range(N_HEADS_C - 2, -1, -1):
            slope = jnp.where(hh == k, LOG2E * 2.0 ** (-8.0 * (k + 1) / N_HEADS_C), slope)
        key_term, row_off = [None], [[None] * len(subs)]
        for r in range(1, nc):
            side = jnp.where(chunk[r] < cd, slope, -slope)
            key_term.append(col * side)
            base = -slope * (jnp.abs(chunk[r] - cd) * TK).astype(F32)
            row_off.append([base - side * (row + float(r0)) for r0, _ in subs])
        return slope, key_term, row_off

    def head_pair(p, carry):
        bias = [head_bias(2 * p + f) for f in range(2)]

        def streams_of(k):
            f, idx, hh = k // 2, 4 * p + k, 2 * p + k // 2
            slope, key_term, row_off = bias[f]

            def scores(r, si):
                s = _dot(q_ref[0, p][subs[si][0]:subs[si][1]], kt_ref[0, idx, chunk[r]])
                return s + (neg_dist[si] * slope if r == 0 else key_term[r])

            return [(functools.partial(scores, si=si),
                     lambda r: v_ref[0, hh, pl.ds(pl.multiple_of(chunk[r] * TK, TK), TK), :],
                     lambda r, si=si: row_off[r][si]) for si in range(len(subs))]

        _run_heads(bounded_ref[pl.program_id(0), p] != 0, list(range(4)), streams_of, nc, s_ref,
                   acc_ref.at[pl.ds(4 * p, 4)])
        return carry

    lax.fori_loop(0, N_HEADS_C // 2, head_pair, 0)

    lam = (jnp.exp(jnp.sum(lq1_ref[...] * lk1_ref[...], axis=-1, keepdims=True))
           - jnp.exp(jnp.sum(lq2_ref[...] * lk2_ref[...], axis=-1, keepdims=True)) + lam_init)
    outs = []
    for p in range(N_HEADS_C // 2):
        o = (_normalised_pair(acc_ref[4 * p], acc_ref[4 * p + 2])
             - lam * _normalised_pair(acc_ref[4 * p + 1], acc_ref[4 * p + 3]))
        outs.append(_half_rms(o) * subln_ref[...] * (1.0 - lam_init))
    o_ref[...] = jnp.concatenate(outs, axis=-1).astype(BF16)


def _attention(kernel_fn, bounded, q, kt, v, extras, out_width, B, S):
    nq = S // TQ
    hq, hv = q.shape[1], v.shape[1]
    extra_specs = [pl.BlockSpec(e.shape, lambda b, i, f, n=e.ndim: (0,) * n) for e in extras]
    return pl.pallas_call(
        kernel_fn,
        grid_spec=pltpu.PrefetchScalarGridSpec(
            num_scalar_prefetch=1,
            grid=(B, nq),
            in_specs=[pl.BlockSpec((1, hq, TQ, LANES), lambda b, i, f: (b, 0, i, 0)),
                      pl.BlockSpec((1,) + kt.shape[1:],
                                   lambda b, i, f, n=kt.ndim: (b,) + (0,) * (n - 1)),
                      pl.BlockSpec((1, hv, S, LANES), lambda b, i, f: (b, 0, 0, 0))] + extra_specs,
            out_specs=pl.BlockSpec((TQ, out_width), lambda b, i, f: (b * nq + i, 0)),
            scratch_shapes=[pltpu.VMEM((TQ, S), F32),
                            pltpu.VMEM((HEADS_PER_PIPELINE * bounded.shape[1], TQ, LANES), F32)]),
        out_shape=jax.ShapeDtypeStruct((B * S, out_width), BF16),
        compiler_params=pltpu.CompilerParams(dimension_semantics=("arbitrary", "arbitrary"),
                                             vmem_limit_bytes=VMEM_LIMIT_BYTES),
        name=kernel_fn.func.__name__.strip("_"),
    )(bounded, q, kt, v, *extras)


def _mlp_kernel(xp_ref, x_ref, xn_ref, ap_ref, a_ref, an_ref, bp_ref, b_ref, bn_ref,
                cp_ref, c_ref, cn_ref, wo_ref, g_ref, wu_ref, cw_ref, cb_ref, wd_ref, fg_ref,
                o_ref, act_ref, *, tiles_per_seq, final_norm):
    i = pl.program_id(0)
    tm = x_ref.shape[0]
    halo = xp_ref.shape[0]

    def rows(p, m, n):
        return jnp.concatenate([p[...], m[...], n[...]], axis=0)

    o_ext = jnp.concatenate([rows(ap_ref, a_ref, an_ref), rows(bp_ref, b_ref, bn_ref),
                             rows(cp_ref, c_ref, cn_ref)], axis=1)
    x1 = rows(xp_ref, x_ref, xn_ref) + _dot(o_ext, wo_ref[...])
    h = (_rms(x1, D_MODEL) * g_ref[...]).astype(BF16)
    at_start = (i % tiles_per_seq) == 0
    at_end = (i % tiles_per_seq) == tiles_per_seq - 1
    zeros = jnp.zeros((halo, D_MODEL), BF16)
    hext = jnp.concatenate([jnp.where(at_start, zeros, h[:halo]), h[halo:halo + tm],
                            jnp.where(at_end, zeros, h[halo + tm:])], axis=0)

    def branch(k, c0):
        u = _dot(hext, wu_ref[:, k * D_FF + c0:k * D_FF + c0 + FFN_CHUNK])
        cw = cw_ref[k, :, c0:c0 + FFN_CHUNK]
        n = u.shape[0]
        prev = pltpu.roll(u, 1, axis=0)[halo:halo + tm]
        nxt = pltpu.roll(u, n - 1, axis=0)[halo:halo + tm]
        return (prev * cw[0:1] + u[halo:halo + tm] * cw[1:2] + nxt * cw[2:3]
                + cb_ref[k, :, c0:c0 + FFN_CHUNK])

    y = x1[halo:halo + tm]
    for c0 in range(0, D_FF, FFN_CHUNK):
        g = branch(0, c0)
        val = branch(1, c0)
        act_ref[:, c0:c0 + FFN_CHUNK] = (g / (1.0 + jnp.exp(-g)) * val).astype(BF16)
        if c0 + FFN_CHUNK == FFN_EARLY_DOWN + FFN_DOWN_LAG * FFN_CHUNK:
            y = y + _dot(act_ref[:, :FFN_EARLY_DOWN], wd_ref[:FFN_EARLY_DOWN])
    y = y + _dot(act_ref[:, FFN_EARLY_DOWN:], wd_ref[FFN_EARLY_DOWN:])
    if final_norm:
        y = _rms(y, D_MODEL) * fg_ref[...]
    o_ref[...] = y


def _mlp(x2d, oa, ob, oc, w_out, g_ffn, w_up, conv_w, conv_b, w_down, final_g, S, final_norm):
    T = x2d.shape[0]
    tm, halo = TM_FFN, BF16_SUBLANES
    per = tm // halo
    last = T // halo - 1
    resident = lambda shape: pl.BlockSpec(shape, lambda i: (0,) * len(shape),
                                          pipeline_mode=pl.Buffered(1))

    def with_halos(width):
        return [pl.BlockSpec((halo, width), lambda i: (jnp.maximum(i * per - 1, 0), 0)),
                pl.BlockSpec((tm, width), lambda i: (i, 0)),
                pl.BlockSpec((halo, width), lambda i: (jnp.minimum((i + 1) * per, last), 0))]

    return pl.pallas_call(
        functools.partial(_mlp_kernel, tiles_per_seq=S // tm, final_norm=final_norm),
        grid=(T // tm,),
        in_specs=(with_halos(D_MODEL) + with_halos(oa.shape[1]) + with_halos(ob.shape[1])
                  + with_halos(oc.shape[1])
                  + [resident(w_out.shape), resident((1, D_MODEL)),
                     resident((D_MODEL, 2 * D_FF)), resident((2, 3, D_FF)), resident((2, 1, D_FF)),
                     resident((D_FF, D_MODEL)), resident((1, D_MODEL))]),
        out_specs=pl.BlockSpec((tm, D_MODEL), lambda i: (i, 0)),
        out_shape=jax.ShapeDtypeStruct((T, D_MODEL), F32),
        scratch_shapes=[pltpu.VMEM((tm, D_FF), BF16)],
        compiler_params=pltpu.CompilerParams(dimension_semantics=("arbitrary",),
                                             vmem_limit_bytes=VMEM_LIMIT_BYTES),
        name="out_proj_conv_mlp",
    )(x2d, x2d, x2d, oa, oa, oa, ob, ob, ob, oc, oc, oc, w_out, g_ffn,
      w_up, conv_w, conv_b, w_down, final_g)


def _pad_groups(w, n_groups, width, offset=0):
    r = w.shape[0]
    w = w.reshape(r, n_groups, width)
    w = jnp.pad(w, ((0, 0), (0, 0), (offset, LANES - width - offset)))
    return w.reshape(r, n_groups * LANES)


def _rope_core(S, dim):
    nf = dim // 4
    t = jnp.arange(S, dtype=jnp.int32)
    rows = (t // GRID_W).astype(F32)
    cols = (t % GRID_W).astype(F32)
    inv = ROPE_BASE ** (-jnp.arange(nf, dtype=F32) / nf)
    ar = rows[:, None] * inv
    ac = cols[:, None] * inv
    cos = jnp.concatenate([jnp.cos(ar), jnp.cos(ar), jnp.cos(ac), jnp.cos(ac)], axis=-1)
    sin = jnp.concatenate([-jnp.sin(ar), jnp.sin(ar), -jnp.sin(ac), jnp.sin(ac)], axis=-1)
    return cos, sin


def _rope_tables(S):
    cos_a, sin_a = _rope_core(S, HEAD_DIM)
    cos_b, sin_b = _rope_core(S, MLA_ROPE)
    pad = LANES - MLA_NOPE - MLA_ROPE
    return (jnp.tile(cos_a, (1, 2)), jnp.tile(sin_a, (1, 2)),
            jnp.concatenate([jnp.ones((S, MLA_NOPE), F32), cos_b, jnp.zeros((S, pad), F32)], -1),
            jnp.concatenate([jnp.zeros((S, MLA_NOPE), F32), sin_b, jnp.zeros((S, pad), F32)], -1))


def _prep_w_in(w):
    a_end = (N_HEADS_A + 2 * N_KV_A) * HEAD_DIM
    b_end = a_end + MLA_Q_RANK + MLA_KV_RANK + MLA_ROPE
    cq = w[:, a_end:a_end + MLA_Q_RANK]
    ckv = w[:, a_end + MLA_Q_RANK:a_end + MLA_Q_RANK + MLA_KV_RANK]
    kr = w[:, b_end - MLA_ROPE:b_end]
    out = jnp.concatenate([
        w[:, :a_end],
        jnp.pad(cq, ((0, 0), (0, CQ_PAD - MLA_Q_RANK))),
        ckv,
        jnp.pad(kr, ((0, 0), (MLA_NOPE, LANES - MLA_NOPE - MLA_ROPE))),
        w[:, b_end:],
    ], axis=-1).astype(BF16)
    assert out.shape[1] == IN_COLS
    return out


def _twice(v):
    return jnp.tile(v, 2).reshape(1, 2 * v.shape[0])


def kernel(x, norm_attn, w_in, q_norm_a, k_norm_a, q_a_norm_b, w_uq_b, kv_a_norm_b, w_ukv_b,
           lambda_q1_c, lambda_k1_c, lambda_q2_c, lambda_k2_c, subln_c, w_out,
           norm_ffn, w_up, conv_w, conv_b, w_down, final_norm):
    B, S, D = x.shape
    depth = w_in.shape[0]
    assert D == D_MODEL and S % max(TM_IN, TM_FFN, TQ, TK) == 0 and S % GRID_W == 0
    assert TM_IN == TK
    T = B * S

    tabs = _rope_tables(S)
    row = lambda v: v.reshape(1, -1)
    a_pairs = [(j, j + N_HEADS_A // 2) for j in range(N_HEADS_A // 2)]
    a_rows = np.concatenate([np.arange(h * HEAD_DIM, (h + 1) * HEAD_DIM)
                             for pair in a_pairs for h in pair])
    xc = x.reshape(T, D)
    for l in range(depth):
        lam_init = 0.8 - 0.6 * math.exp(-0.3 * l)
        wuq_p = jnp.pad(_pad_groups(w_uq_b[l], N_HEADS_B, MLA_NOPE + MLA_ROPE),
                        ((0, CQ_PAD - MLA_Q_RANK), (0, 0))).astype(BF16)
        wukv = w_ukv_b[l].reshape(MLA_KV_RANK, N_HEADS_B, MLA_NOPE + MLA_V)
        v_cols = [_pad_groups(wukv[:, h, MLA_NOPE:], 1, MLA_V, offset=HALF * (h % 2))
                  for h in range(N_HEADS_B)]
        wukv_p = jnp.concatenate(
            [_pad_groups(wukv[:, :, :MLA_NOPE].reshape(MLA_KV_RANK, -1), N_HEADS_B, MLA_NOPE)]
            + v_cols, axis=-1).astype(BF16)

        qa, kta, va, qb, ktb, vb, qc, ktc, vc, nrm = _in_proj(
            xc, row(norm_attn[l]), _prep_w_in(w_in[l]), tabs,
            _twice(q_norm_a[l]), _twice(k_norm_a[l]),
            jnp.pad(q_a_norm_b[l], (0, CQ_PAD - MLA_Q_RANK)).reshape(1, CQ_PAD),
            row(kv_a_norm_b[l]), wuq_p, wukv_p, B, S)

        nrm = jnp.max(nrm.reshape(B, -1, LANES), axis=1)

        def bounded(q0, nq, q_rep, k0, nk, k_rep):
            q2 = jnp.repeat(nrm[:, q0:q0 + nq], q_rep, axis=1)
            k2 = jnp.repeat(nrm[:, k0:k0 + nk], k_rep, axis=1)
            ok = q2 * k2 * NRM_MARGIN <= SAFE_LOG2_RANGE ** 2
            return jnp.all(ok.reshape(B, -1, HEADS_PER_PIPELINE), axis=-1).astype(jnp.int32)

        qk_a = (HEAD_DIM * jnp.max(jnp.abs(q_norm_a[l])) * jnp.max(jnp.abs(k_norm_a[l]))
                * (HEAD_DIM ** -0.5 * LOG2E))
        bounded_a = jnp.broadcast_to(
            (qk_a * NRM_MARGIN <= SAFE_LOG2_RANGE).astype(jnp.int32),
            (B, N_HEADS_A // HEADS_PER_PIPELINE))

        group = N_HEADS_A // N_KV_A
        oa = _attention(
            functools.partial(_attn_ab_kernel, n_heads=N_HEADS_A, q_index=lambda h: h // 2,
                              kt_index=lambda h: 2 * (h // group) + h % 2,
                              v_index=lambda h: h // group, out_pairs=a_pairs, n_keys=S),
            bounded_a, qa, kta, va, [], N_HEADS_A * HEAD_DIM, B, S)
        same = lambda h: h
        ob = _attention(
            functools.partial(_attn_ab_kernel, n_heads=N_HEADS_B, q_index=same, kt_index=same,
                              v_index=same,
                              out_pairs=[(2 * j, 2 * j + 1) for j in range(N_HEADS_B // 2)],
                              n_keys=S),
            bounded(NRM_QB, N_HEADS_B, 1, NRM_KB, N_HEADS_B, 1),
            qb, ktb, vb, [], N_HEADS_B * MLA_V, B, S)
        oc = _attention(
            functools.partial(_attn_c_kernel, n_keys=S, lam_init=lam_init),
            bounded(NRM_QC, N_HEADS_C, 2, NRM_KC, 2 * N_HEADS_C, 1),
            qc, ktc, vc,
            [row(lambda_q1_c[l]), row(lambda_k1_c[l]), row(lambda_q2_c[l]), row(lambda_k2_c[l]),
             _twice(subln_c[l])],
            N_HEADS_C * DIFF_V, B, S)

        w_out_l = jnp.concatenate([w_out[l][a_rows], w_out[l][N_HEADS_A * HEAD_DIM:]], axis=0)
        xc = _mlp(xc, oa, ob, oc, w_out_l.astype(BF16), row(norm_ffn[l]), w_up[l].astype(BF16),
                  conv_w[l].reshape(3, 2, D_FF).transpose(1, 0, 2),
                  conv_b[l].reshape(2, 1, D_FF), w_down[l].astype(BF16),
                  row(final_norm), S, final_norm=(l == depth - 1))
    return xc.reshape(B, S, D)
```

```python
import functools
import math

import jax
import jax.numpy as jnp
import numpy as np
from jax import lax
from jax.experimental import pallas as pl
from jax.experimental.pallas import tpu as pltpu

D_MODEL = 1024
GRID_W = 64
HEAD_DIM = 64
N_HEADS_A = 8
N_KV_A = 2
N_HEADS_B = 4
MLA_Q_RANK = 192
MLA_KV_RANK = 128
MLA_NOPE = 64
MLA_ROPE = 32
MLA_V = 64
N_HEADS_C = 4
DIFF_QK = 32
DIFF_V = 64
D_FF = 2816
ROPE_BASE = 10000.0
EPS = 1e-6

LANES = 128
HALF = LANES // 2
BF16_SUBLANES = 16
VMEM_LIMIT_BYTES = 56 * 1024 * 1024

LOG2E = math.log2(math.e)
assert HEAD_DIM == MLA_V == DIFF_V == 2 * DIFF_QK == HALF

OFF_QA = 0
OFF_KA = OFF_QA + N_HEADS_A * HEAD_DIM
OFF_VA = OFF_KA + LANES
OFF_CQ = OFF_VA + LANES
CQ_PAD = 2 * LANES
OFF_CKV = OFF_CQ + CQ_PAD
OFF_KR = OFF_CKV + MLA_KV_RANK
OFF_QC = OFF_KR + LANES
OFF_KC = OFF_QC + N_HEADS_C * HALF
OFF_VC = OFF_KC + N_HEADS_C * HALF
IN_COLS = OFF_VC + N_HEADS_C * HALF

NRM_QB = 0
NRM_KB = NRM_QB + N_HEADS_B
NRM_QC = NRM_KB + N_HEADS_B
NRM_KC = NRM_QC + N_HEADS_C
NRM_GROUPS = 2 * N_HEADS_B + N_HEADS_C
NRM_MARGIN = 1.03
SAFE_LOG2_RANGE = 55.0

F32 = jnp.float32
BF16 = jnp.bfloat16

TM_IN = 512
TM_FFN = 512
FFN_CHUNK = 256
FFN_DOWN_GROUP = 6 * FFN_CHUNK
FFN_DOWN_LAG = 2
TQ = 512
TQ_SUB_AB = 512
TQ_SUB_C = 256
TK = 512
SCORE_LOOKAHEAD = 2
HEADS_PER_PIPELINE = 4


def _dot(a, b):
    return jnp.dot(a, b, preferred_element_type=F32)


def _rms(x, n):
    ms = jnp.sum(x * x, axis=-1, keepdims=True) * (1.0 / n)
    return x * lax.rsqrt(ms + EPS)


def _low_lanes():
    return lax.broadcasted_iota(jnp.int32, (1, LANES), 1) < HALF


def _half_rms(x):
    low = _low_lanes()
    sq = x * x
    s_lo = jnp.sum(jnp.where(low, sq, 0.0), axis=-1, keepdims=True)
    s_hi = jnp.sum(jnp.where(low, 0.0, sq), axis=-1, keepdims=True)
    return x * lax.rsqrt(jnp.where(low, s_lo, s_hi) * (1.0 / HALF) + EPS)


def _in_kernel(x_ref, gat_ref, w_ref, cos_a_ref, sin_a_ref, cos_b_ref, sin_b_ref,
               qg_a_ref, kg_a_ref, qg_b_ref, kvg_b_ref, wuq_ref, wukv_ref, sel_ref,
               qa_ref, kta_ref, va_ref, qb_ref, ktb_ref, vb_ref, qc_ref, ktc_ref, vc_ref, nrm_ref):
    x = x_ref[...]
    tm = x.shape[0]
    h = (_rms(x, D_MODEL) * gat_ref[...]).astype(BF16)

    lane = lax.broadcasted_iota(jnp.int32, (1, LANES), 1)
    low = lane < HALF
    row = lax.broadcasted_iota(jnp.int32, (LANES, 1), 0)

    z = _dot(h, w_ref[...])

    def seg(off, width):
        return z[:, off:off + width]

    def swap(y, nf):
        up = pltpu.roll(y, LANES - nf, axis=1)
        down = pltpu.roll(y, nf, axis=1)
        return jnp.where((lane & nf) == 0, up, down)

    def rope(y, c, s, nf):
        return y * c + swap(y, nf) * s

    def group(z, g):
        return z[:, g * LANES:(g + 1) * LANES]

    squares = []

    def note(y):
        squares.append((y * y).astype(BF16))

    def noted_row_sums(first):
        return _dot(jnp.concatenate(squares[first:], axis=-1),
                    sel_ref[first * LANES:len(squares) * LANES])

    cos_b, sin_b = cos_b_ref[...], sin_b_ref[...]
    nf_b = MLA_ROPE // 4
    cq = (_rms(seg(OFF_CQ, CQ_PAD), MLA_Q_RANK) * qg_b_ref[...]).astype(BF16)
    qb = _dot(cq, wuq_ref[...])
    scale_b = (MLA_NOPE + MLA_ROPE) ** -0.5 * LOG2E
    for hh in range(N_HEADS_B):
        y = rope(group(qb, hh), cos_b, sin_b, nf_b) * scale_b
        qb_ref[0, hh] = y.astype(BF16)
        note(y)
    ckv = (_rms(seg(OFF_CKV, MLA_KV_RANK), MLA_KV_RANK) * kvg_b_ref[...]).astype(BF16)
    kvb = _dot(ckv, wukv_ref[...])
    kr = rope(seg(OFF_KR, LANES), cos_b, sin_b, nf_b)
    for hh in range(N_HEADS_B):
        y = group(kvb, hh) + kr
        ktb_ref[0, hh] = y.T.astype(BF16)
        note(y)
        ones_other_half = jnp.where(low, float(hh % 2), float(1 - hh % 2))
        vb_ref[0, hh] = (group(kvb, N_HEADS_B + hh) + ones_other_half).astype(BF16)
    n2 = noted_row_sums(0)
    n_noted_b = len(squares)

    zq = seg(OFF_QC, N_HEADS_C * HALF)
    zk = seg(OFF_KC, N_HEADS_C * HALF)
    zv = seg(OFF_VC, N_HEADS_C * HALF)
    k_groups = []
    for p in range(N_HEADS_C // 2):
        y = group(zq, p) * (DIFF_QK ** -0.5 * LOG2E)
        qc_ref[0, p] = y.astype(BF16)
        note(y)
        k_groups.append(group(zk, p))
        kt = group(zk, p).T
        for sub in range(4):
            mine = (row >= sub * DIFF_QK) & (row < (sub + 1) * DIFF_QK)
            ktc_ref[0, 4 * p + sub, 0] = jnp.where(mine, kt, 0.0).astype(BF16)
        vz = group(zv, p)
        vc_ref[0, 2 * p] = jnp.where(low, vz, 1.0).astype(BF16)
        vc_ref[0, 2 * p + 1] = jnp.where(low, 1.0, vz).astype(BF16)
    for kz in k_groups:
        note(kz)
    n2 = n2 + noted_row_sums(n_noted_b)
    nrm_ref[0] = jnp.max(n2, axis=0, keepdims=True)

    cos_a, sin_a = cos_a_ref[...], sin_a_ref[...]
    nf_a = HEAD_DIM // 4
    zq = seg(OFF_QA, N_HEADS_A * HEAD_DIM)
    for p in range(N_HEADS_A // 2):
        y = rope(_half_rms(group(zq, p)) * qg_a_ref[...], cos_a, sin_a, nf_a)
        qa_ref[0, p] = (y * (HEAD_DIM ** -0.5 * LOG2E)).astype(BF16)
    y = rope(_half_rms(seg(OFF_KA, LANES)) * kg_a_ref[...], cos_a, sin_a, nf_a)
    kt = y.T.astype(BF16)
    zeros = jnp.zeros((HALF, tm), BF16)
    for g in range(N_KV_A):
        kg = kt[g * HALF:(g + 1) * HALF]
        kta_ref[0, 2 * g] = jnp.concatenate([kg, zeros], axis=0)
        kta_ref[0, 2 * g + 1] = jnp.concatenate([zeros, kg], axis=0)
    zv = seg(OFF_VA, LANES)
    va_ref[0, 0] = jnp.where(low, zv, 1.0).astype(BF16)
    va_ref[0, 1] = jnp.where(low, 1.0, zv).astype(BF16)


def _norm_routing():
    sel = np.zeros((NRM_GROUPS, LANES, LANES), np.float32)
    g = 0
    for slot in ([NRM_QB + h for h in range(N_HEADS_B)] + [NRM_KB + h for h in range(N_HEADS_B)]):
        sel[g, :, slot] = 1.0
        g += 1
    for p in range(N_HEADS_C // 2):
        for f in range(2):
            sel[g, f * HALF:(f + 1) * HALF, NRM_QC + 2 * p + f] = 1.0
        g += 1
    for p in range(N_HEADS_C // 2):
        for sub in range(4):
            sel[g, sub * DIFF_QK:(sub + 1) * DIFF_QK, NRM_KC + 4 * p + sub] = 1.0
        g += 1
    assert g == NRM_GROUPS
    return jnp.asarray(sel.reshape(NRM_GROUPS * LANES, LANES), BF16)


def _in_proj(x2d, gat, w_in_r, tabs, qg_a, kg_a, qg_b, kvg_b, wuq_p, wukv_p, B, S):
    T = B * S
    tm = TM_IN
    nst = S // tm
    const = lambda shape: pl.BlockSpec(shape, lambda i: (0,) * len(shape))
    tab = pl.BlockSpec((tm, LANES), lambda i: (i % nst, 0))
    hm = lambda H: pl.BlockSpec((1, H, tm, LANES), lambda i: (i // nst, 0, i % nst, 0))
    hmt = lambda H: pl.BlockSpec((1, H, LANES, tm), lambda i: (i // nst, 0, 0, i % nst))
    sds = lambda H: jax.ShapeDtypeStruct((B, H, S, LANES), BF16)
    sdt = lambda H: jax.ShapeDtypeStruct((B, H, LANES, S), BF16)
    return pl.pallas_call(
        _in_kernel,
        grid=(T // tm,),
        in_specs=[pl.BlockSpec((tm, D_MODEL), lambda i: (i, 0)),
                  const((1, D_MODEL)), const((D_MODEL, IN_COLS)),
                  tab, tab, tab, tab,
                  const((1, LANES)), const((1, LANES)), const((1, CQ_PAD)), const((1, MLA_KV_RANK)),
                  const((CQ_PAD, N_HEADS_B * LANES)), const((MLA_KV_RANK, 2 * N_HEADS_B * LANES)),
                  const((NRM_GROUPS * LANES, LANES))],
        out_specs=[hm(N_HEADS_A // 2), hmt(2 * N_KV_A), hm(N_KV_A),
                   hm(N_HEADS_B), hmt(N_HEADS_B), hm(N_HEADS_B),
                   hm(N_HEADS_C // 2),
                   pl.BlockSpec((1, 2 * N_HEADS_C, 1, LANES, tm),
                                lambda i: (i // nst, 0, i % nst, 0, 0)),
                   hm(N_HEADS_C),
                   pl.BlockSpec((1, 1, LANES), lambda i: (i, 0, 0))],
        out_shape=[sds(N_HEADS_A // 2), sdt(2 * N_KV_A), sds(N_KV_A),
                   sds(N_HEADS_B), sdt(N_HEADS_B), sds(N_HEADS_B),
                   sds(N_HEADS_C // 2),
                   jax.ShapeDtypeStruct((B, 2 * N_HEADS_C, S // tm, LANES, tm), BF16),
                   sds(N_HEADS_C),
                   jax.ShapeDtypeStruct((T // tm, 1, LANES), F32)],
        compiler_params=pltpu.CompilerParams(dimension_semantics=("arbitrary",),
                                             vmem_limit_bytes=VMEM_LIMIT_BYTES),
        name="in_proj",
    )(x2d, gat, w_in_r, *tabs, qg_a, kg_a, qg_b, kvg_b, wuq_p, wukv_p, _norm_routing())


def _lane_group_max(s):
    m = s[:, :LANES]
    for j in range(1, s.shape[1] // LANES):
        m = jnp.maximum(m, s[:, j * LANES:(j + 1) * LANES])
    return m


def _pv_bounded(streams, n_chunks):
    items = [(r, i) for r in range(n_chunks) for i in range(len(streams))]
    accs = [None] * len(streams)
    in_flight = {}
    for k in range(len(items) + SCORE_LOOKAHEAD):
        if k < len(items):
            r, i = items[k]
            scores, _, row_off = streams[i]
            s, off = scores(r), row_off(r)
            in_flight[k] = s if off is None else s + off
        if k >= SCORE_LOOKAHEAD:
            r, i = items[k - SCORE_LOOKAHEAD]
            d = _dot(jnp.exp2(in_flight.pop(k - SCORE_LOOKAHEAD)).astype(BF16), streams[i][1](r))
            accs[i] = d if accs[i] is None else accs[i] + d
    return accs


def _pv_exact(streams, n_chunks, s_ref):
    rows = s_ref.shape[0] // len(streams)
    maxes = []
    for i, (scores, _, row_off) in enumerate(streams):
        m_run = None
        for r in range(n_chunks):
            s = scores(r)
            s_ref[i * rows:(i + 1) * rows, r * TK:(r + 1) * TK] = s
            m, off = _lane_group_max(s), row_off(r)
            if off is not None:
                m = m + off
            m_run = m if m_run is None else jnp.maximum(m_run, m)
        maxes.append(jnp.max(m_run, axis=-1, keepdims=True))
    accs = []
    for i, (_, values, row_off) in enumerate(streams):
        acc = None
        for r in range(n_chunks):
            off = row_off(r)
            shift = maxes[i] if off is None else maxes[i] - off
            p = jnp.exp2(s_ref[i * rows:(i + 1) * rows, r * TK:(r + 1) * TK] - shift)
            d = _dot(p.astype(BF16), values(r))
            acc = d if acc is None else acc + d
        accs.append(acc)
    return accs


def _run_heads(bounded, heads, head_streams, n_chunks, s_ref, acc_ref):
    @pl.when(bounded)
    def _():
        per_head = [head_streams(h) for h in heads]
        n_sub = len(per_head[0])
        accs = _pv_bounded([s for streams in per_head for s in streams], n_chunks)
        for k, h in enumerate(heads):
            acc_ref[h] = jnp.concatenate(accs[k * n_sub:(k + 1) * n_sub], axis=0)

    @pl.when(jnp.logical_not(bounded))
    def _():
        for h in heads:
            acc_ref[h] = jnp.concatenate(_pv_exact(head_streams(h), n_chunks, s_ref), axis=0)


def _normalised_pair(acc_low, acc_high):
    low = _low_lanes()
    num = jnp.where(low, acc_low, acc_high)
    den = pltpu.roll(jnp.where(low, acc_high, acc_low), HALF, axis=1)
    return num / den


def _attn_ab_kernel(bounded_ref, q_ref, kt_ref, v_ref, o_ref, s_ref, acc_ref, *,
                    n_heads, q_index, kt_index, v_index, out_pairs, n_keys):
    n_chunks = n_keys // TK

    def head_streams(h):
        q = q_ref[0, q_index(h)]
        return [(lambda r, qs=q[r0:r0 + TQ_SUB_AB]:
                 _dot(qs, kt_ref[0, kt_index(h), :, r * TK:(r + 1) * TK]),
                 lambda r: v_ref[0, v_index(h), r * TK:(r + 1) * TK, :],
                 lambda r: None) for r0 in range(0, TQ, TQ_SUB_AB)]

    def head_group(j, carry):
        heads = [HEADS_PER_PIPELINE * j + k for k in range(HEADS_PER_PIPELINE)]
        _run_heads(bounded_ref[pl.program_id(0), j] != 0, heads, head_streams, n_chunks, s_ref,
                   acc_ref)
        return carry

    lax.fori_loop(0, n_heads // HEADS_PER_PIPELINE, head_group, 0)
    o_ref[...] = jnp.concatenate([_normalised_pair(acc_ref[a], acc_ref[b]) for a, b in out_pairs],
                                 axis=-1).astype(BF16)


def _attn_c_kernel(bounded_ref, q_ref, kt_ref, v_ref, lq1_ref, lk1_ref, lq2_ref, lk2_ref,
                   subln_ref, o_ref, s_ref, acc_ref, *, n_keys, lam_init):
    assert TQ == TK
    nc = n_keys // TK
    cd = pl.program_id(1)
    col = lax.broadcasted_iota(jnp.int32, (1, TK), 1).astype(F32)
    row = lax.broadcasted_iota(jnp.int32, (TQ_SUB_C, 1), 0).astype(F32)
    subs = [(r, r + TQ_SUB_C) for r in range(0, TQ, TQ_SUB_C)]
    neg_dist = [-jnp.abs((lax.broadcasted_iota(jnp.int32, (TQ_SUB_C, TK), 0) + r0
                          - lax.broadcasted_iota(jnp.int32, (TQ_SUB_C, TK), 1)).astype(F32))
                for r0, _ in subs]

    chunk = [(cd + r) % nc for r in range(nc)]
    key_unit, row_unit = [None], [[None] * len(subs)]
    for r in range(1, nc):
        sign = jnp.where(chunk[r] < cd, 1.0, -1.0)
        key_unit.append(col * sign)
        dist = (jnp.abs(chunk[r] - cd) * TK).astype(F32)
        row_unit.append([-dist - sign * (row + float(r0)) for r0, _ in subs])

    def head_bias(hh):
        slope = jnp.float32(LOG2E * 2.0 ** (-8.0 * N_HEADS_C / N_HEADS_C))
        for k in range(N_HEADS_C - 2, -1, -1):
            slope = jnp.where(hh == k, LOG2E * 2.0 ** (-8.0 * (k + 1) / N_HEADS_C), slope)
        key_term = [None] + [key_unit[r] * slope for r in range(1, nc)]
        row_off = [row_unit[0]] + [[u * slope for u in row_unit[r]] for r in range(1, nc)]
        return slope, key_term, row_off

    def head_pair(p, carry):
        bias = [head_bias(2 * p + f) for f in range(2)]

        def streams_of(k):
            f, idx, hh = k // 2, 4 * p + k, 2 * p + k // 2
            slope, key_term, row_off = bias[f]

            def scores(r, si):
                s = _dot(q_ref[0, p][subs[si][0]:subs[si][1]], kt_ref[0, idx, chunk[r]])
                return s + (neg_dist[si] * slope if r == 0 else key_term[r])

            return [(functools.partial(scores, si=si),
                     lambda r: v_ref[0, hh, pl.ds(pl.multiple_of(chunk[r] * TK, TK), TK), :],
                     lambda r, si=si: row_off[r][si]) for si in range(len(subs))]

        _run_heads(bounded_ref[pl.program_id(0), p] != 0, list(range(4)), streams_of, nc, s_ref,
                   acc_ref.at[pl.ds(4 * p, 4)])
        return carry

    lax.fori_loop(0, N_HEADS_C // 2, head_pair, 0)

    lam = (jnp.exp(jnp.sum(lq1_ref[...] * lk1_ref[...], axis=-1, keepdims=True))
           - jnp.exp(jnp.sum(lq2_ref[...] * lk2_ref[...], axis=-1, keepdims=True)) + lam_init)
    outs = []
    for p in range(N_HEADS_C // 2):
        o = (_normalised_pair(acc_ref[4 * p], acc_ref[4 * p + 2])
             - lam * _normalised_pair(acc_ref[4 * p + 1], acc_ref[4 * p + 3]))
        outs.append(_half_rms(o) * subln_ref[...] * (1.0 - lam_init))
    o_ref[...] = jnp.concatenate(outs, axis=-1).astype(BF16)


def _attention(kernel_fn, bounded, q, kt, v, extras, out_width, B, S):
    nq = S // TQ
    hq, hv = q.shape[1], v.shape[1]
    extra_specs = [pl.BlockSpec(e.shape, lambda b, i, f, n=e.ndim: (0,) * n) for e in extras]
    return pl.pallas_call(
        kernel_fn,
        grid_spec=pltpu.PrefetchScalarGridSpec(
            num_scalar_prefetch=1,
            grid=(B, nq),
            in_specs=[pl.BlockSpec((1, hq, TQ, LANES), lambda b, i, f: (b, 0, i, 0)),
                      pl.BlockSpec((1,) + kt.shape[1:],
                                   lambda b, i, f, n=kt.ndim: (b,) + (0,) * (n - 1)),
                      pl.BlockSpec((1, hv, S, LANES), lambda b, i, f: (b, 0, 0, 0))] + extra_specs,
            out_specs=pl.BlockSpec((TQ, out_width), lambda b, i, f: (b * nq + i, 0)),
            scratch_shapes=[pltpu.VMEM((TQ, S), F32),
                            pltpu.VMEM((HEADS_PER_PIPELINE * bounded.shape[1], TQ, LANES), F32)]),
        out_shape=jax.ShapeDtypeStruct((B * S, out_width), BF16),
        compiler_params=pltpu.CompilerParams(dimension_semantics=("arbitrary", "arbitrary"),
                                             vmem_limit_bytes=VMEM_LIMIT_BYTES),
        name=kernel_fn.func.__name__.strip("_"),
    )(bounded, q, kt, v, *extras)


def _mlp_kernel(xp_ref, x_ref, xn_ref, ap_ref, a_ref, an_ref, bp_ref, b_ref, bn_ref,
                cp_ref, c_ref, cn_ref, wo_ref, g_ref, wu_ref, cw_ref, cb_ref, wd_ref, fg_ref,
                o_ref, act_ref, *, tiles_per_seq, final_norm):
    i = pl.program_id(0)
    tm = x_ref.shape[0]
    halo = xp_ref.shape[0]

    def rows(p, m, n):
        return jnp.concatenate([p[...], m[...], n[...]], axis=0)

    o_ext = jnp.concatenate([rows(ap_ref, a_ref, an_ref), rows(bp_ref, b_ref, bn_ref),
                             rows(cp_ref, c_ref, cn_ref)], axis=1)
    x1 = rows(xp_ref, x_ref, xn_ref) + _dot(o_ext, wo_ref[...])
    h = (_rms(x1, D_MODEL) * g_ref[...]).astype(BF16)
    at_start = (i % tiles_per_seq) == 0
    at_end = (i % tiles_per_seq) == tiles_per_seq - 1
    zeros = jnp.zeros((halo, D_MODEL), BF16)
    hext = jnp.concatenate([jnp.where(at_start, zeros, h[:halo]), h[halo:halo + tm],
                            jnp.where(at_end, zeros, h[halo + tm:])], axis=0)

    def branch(k, c0):
        u = _dot(hext, wu_ref[:, k * D_FF + c0:k * D_FF + c0 + FFN_CHUNK])
        cw = cw_ref[k, :, c0:c0 + FFN_CHUNK]
        n = u.shape[0]
        prev = pltpu.roll(u, 1, axis=0)[halo:halo + tm]
        nxt = pltpu.roll(u, n - 1, axis=0)[halo:halo + tm]
        return (prev * cw[0:1] + u[halo:halo + tm] * cw[1:2] + nxt * cw[2:3]
                + cb_ref[k, :, c0:c0 + FFN_CHUNK])

    y = x1[halo:halo + tm]
    done = 0
    for c0 in range(0, D_FF, FFN_CHUNK):
        g = branch(0, c0)
        val = branch(1, c0)
        act_ref[:, c0:c0 + FFN_CHUNK] = (g / (1.0 + jnp.exp(-g)) * val).astype(BF16)
        ready = c0 + FFN_CHUNK - FFN_DOWN_LAG * FFN_CHUNK
        if ready - done >= FFN_DOWN_GROUP:
            y = y + _dot(act_ref[:, done:done + FFN_DOWN_GROUP], wd_ref[done:done + FFN_DOWN_GROUP])
            done += FFN_DOWN_GROUP
    y = y + _dot(act_ref[:, done:], wd_ref[done:])
    if final_norm:
        y = _rms(y, D_MODEL) * fg_ref[...]
    o_ref[...] = y


def _mlp(x2d, oa, ob, oc, w_out, g_ffn, w_up, conv_w, conv_b, w_down, final_g, S, final_norm):
    T = x2d.shape[0]
    tm, halo = TM_FFN, BF16_SUBLANES
    per = tm // halo
    last = T // halo - 1
    resident = lambda shape: pl.BlockSpec(shape, lambda i: (0,) * len(shape),
                                          pipeline_mode=pl.Buffered(1))

    def with_halos(width):
        return [pl.BlockSpec((halo, width), lambda i: (jnp.maximum(i * per - 1, 0), 0)),
                pl.BlockSpec((tm, width), lambda i: (i, 0)),
                pl.BlockSpec((halo, width), lambda i: (jnp.minimum((i + 1) * per, last), 0))]

    return pl.pallas_call(
        functools.partial(_mlp_kernel, tiles_per_seq=S // tm, final_norm=final_norm),
        grid=(T // tm,),
        in_specs=(with_halos(D_MODEL) + with_halos(oa.shape[1]) + with_halos(ob.shape[1])
                  + with_halos(oc.shape[1])
                  + [resident(w_out.shape), resident((1, D_MODEL)),
                     resident((D_MODEL, 2 * D_FF)), resident((2, 3, D_FF)), resident((2, 1, D_FF)),
                     resident((D_FF, D_MODEL)), resident((1, D_MODEL))]),
        out_specs=pl.BlockSpec((tm, D_MODEL), lambda i: (i, 0)),
        out_shape=jax.ShapeDtypeStruct((T, D_MODEL), F32),
        scratch_shapes=[pltpu.VMEM((tm, D_FF), BF16)],
        compiler_params=pltpu.CompilerParams(dimension_semantics=("arbitrary",),
                                             vmem_limit_bytes=VMEM_LIMIT_BYTES),
        name="out_proj_conv_mlp",
    )(x2d, x2d, x2d, oa, oa, oa, ob, ob, ob, oc, oc, oc, w_out, g_ffn,
      w_up, conv_w, conv_b, w_down, final_g)


def _pad_groups(w, n_groups, width, offset=0):
    r = w.shape[0]
    w = w.reshape(r, n_groups, width)
    w = jnp.pad(w, ((0, 0), (0, 0), (offset, LANES - width - offset)))
    return w.reshape(r, n_groups * LANES)


def _rope_core(S, dim):
    nf = dim // 4
    t = jnp.arange(S, dtype=jnp.int32)
    rows = (t // GRID_W).astype(F32)
    cols = (t % GRID_W).astype(F32)
    inv = ROPE_BASE ** (-jnp.arange(nf, dtype=F32) / nf)
    ar = rows[:, None] * inv
    ac = cols[:, None] * inv
    cos = jnp.concatenate([jnp.cos(ar), jnp.cos(ar), jnp.cos(ac), jnp.cos(ac)], axis=-1)
    sin = jnp.concatenate([-jnp.sin(ar), jnp.sin(ar), -jnp.sin(ac), jnp.sin(ac)], axis=-1)
    return cos, sin


def _rope_tables(S):
    cos_a, sin_a = _rope_core(S, HEAD_DIM)
    cos_b, sin_b = _rope_core(S, MLA_ROPE)
    pad = LANES - MLA_NOPE - MLA_ROPE
    return (jnp.tile(cos_a, (1, 2)), jnp.tile(sin_a, (1, 2)),
            jnp.concatenate([jnp.ones((S, MLA_NOPE), F32), cos_b, jnp.zeros((S, pad), F32)], -1),
            jnp.concatenate([jnp.zeros((S, MLA_NOPE), F32), sin_b, jnp.zeros((S, pad), F32)], -1))


def _prep_w_in(w):
    a_end = (N_HEADS_A + 2 * N_KV_A) * HEAD_DIM
    b_end = a_end + MLA_Q_RANK + MLA_KV_RANK + MLA_ROPE
    cq = w[:, a_end:a_end + MLA_Q_RANK]
    ckv = w[:, a_end + MLA_Q_RANK:a_end + MLA_Q_RANK + MLA_KV_RANK]
    kr = w[:, b_end - MLA_ROPE:b_end]
    out = jnp.concatenate([
        w[:, :a_end],
        jnp.pad(cq, ((0, 0), (0, CQ_PAD - MLA_Q_RANK))),
        ckv,
        jnp.pad(kr, ((0, 0), (MLA_NOPE, LANES - MLA_NOPE - MLA_ROPE))),
        w[:, b_end:],
    ], axis=-1).astype(BF16)
    assert out.shape[1] == IN_COLS
    return out


def _twice(v):
    return jnp.tile(v, 2).reshape(1, 2 * v.shape[0])


def kernel(x, norm_attn, w_in, q_norm_a, k_norm_a, q_a_norm_b, w_uq_b, kv_a_norm_b, w_ukv_b,
           lambda_q1_c, lambda_k1_c, lambda_q2_c, lambda_k2_c, subln_c, w_out,
           norm_ffn, w_up, conv_w, conv_b, w_down, final_norm):
    B, S, D = x.shape
    depth = w_in.shape[0]
    assert D == D_MODEL and S % max(TM_IN, TM_FFN, TQ, TK) == 0 and S % GRID_W == 0
    assert TM_IN == TK
    T = B * S

    tabs = _rope_tables(S)
    row = lambda v: v.reshape(1, -1)
    a_pairs = [(j, j + N_HEADS_A // 2) for j in range(N_HEADS_A // 2)]
    a_rows = np.concatenate([np.arange(h * HEAD_DIM, (h + 1) * HEAD_DIM)
                             for pair in a_pairs for h in pair])
    xc = x.reshape(T, D)
    for l in range(depth):
        lam_init = 0.8 - 0.6 * math.exp(-0.3 * l)
        wuq_p = jnp.pad(_pad_groups(w_uq_b[l], N_HEADS_B, MLA_NOPE + MLA_ROPE),
                        ((0, CQ_PAD - MLA_Q_RANK), (0, 0))).astype(BF16)
        wukv = w_ukv_b[l].reshape(MLA_KV_RANK, N_HEADS_B, MLA_NOPE + MLA_V)
        v_cols = [_pad_groups(wukv[:, h, MLA_NOPE:], 1, MLA_V, offset=HALF * (h % 2))
                  for h in range(N_HEADS_B)]
        wukv_p = jnp.concatenate(
            [_pad_groups(wukv[:, :, :MLA_NOPE].reshape(MLA_KV_RANK, -1), N_HEADS_B, MLA_NOPE)]
            + v_cols, axis=-1).astype(BF16)

        qa, kta, va, qb, ktb, vb, qc, ktc, vc, nrm = _in_proj(
            xc, row(norm_attn[l]), _prep_w_in(w_in[l]), tabs,
            _twice(q_norm_a[l]), _twice(k_norm_a[l]),
            jnp.pad(q_a_norm_b[l], (0, CQ_PAD - MLA_Q_RANK)).reshape(1, CQ_PAD),
            row(kv_a_norm_b[l]), wuq_p, wukv_p, B, S)

        nrm = jnp.max(nrm.reshape(B, -1, LANES), axis=1)

        def bounded(q0, nq, q_rep, k0, nk, k_rep):
            q2 = jnp.repeat(nrm[:, q0:q0 + nq], q_rep, axis=1)
            k2 = jnp.repeat(nrm[:, k0:k0 + nk], k_rep, axis=1)
            ok = q2 * k2 * NRM_MARGIN <= SAFE_LOG2_RANGE ** 2
            return jnp.all(ok.reshape(B, -1, HEADS_PER_PIPELINE), axis=-1).astype(jnp.int32)

        qk_a = (HEAD_DIM * jnp.max(jnp.abs(q_norm_a[l])) * jnp.max(jnp.abs(k_norm_a[l]))
                * (HEAD_DIM ** -0.5 * LOG2E))
        bounded_a = jnp.broadcast_to(
            (qk_a * NRM_MARGIN <= SAFE_LOG2_RANGE).astype(jnp.int32),
            (B, N_HEADS_A // HEADS_PER_PIPELINE))

        group = N_HEADS_A // N_KV_A
        oa = _attention(
            functools.partial(_attn_ab_kernel, n_heads=N_HEADS_A, q_index=lambda h: h // 2,
                              kt_index=lambda h: 2 * (h // group) + h % 2,
                              v_index=lambda h: h // group, out_pairs=a_pairs, n_keys=S),
            bounded_a, qa, kta, va, [], N_HEADS_A * HEAD_DIM, B, S)
        same = lambda h: h
        ob = _attention(
            functools.partial(_attn_ab_kernel, n_heads=N_HEADS_B, q_index=same, kt_index=same,
                              v_index=same,
                              out_pairs=[(2 * j, 2 * j + 1) for j in range(N_HEADS_B // 2)],
                              n_keys=S),
            bounded(NRM_QB, N_HEADS_B, 1, NRM_KB, N_HEADS_B, 1),
            qb, ktb, vb, [], N_HEADS_B * MLA_V, B, S)
        oc = _attention(
            functools.partial(_attn_c_kernel, n_keys=S, lam_init=lam_init),
            bounded(NRM_QC, N_HEADS_C, 2, NRM_KC, 2 * N_HEADS_C, 1),
            qc, ktc, vc,
            [row(lambda_q1_c[l]), row(lambda_k1_c[l]), row(lambda_q2_c[l]), row(lambda_k2_c[l]),
             _twice(subln_c[l])],
            N_HEADS_C * DIFF_V, B, S)

        w_out_l = jnp.concatenate([w_out[l][a_rows], w_out[l][N_HEADS_A * HEAD_DIM:]], axis=0)
        xc = _mlp(xc, oa, ob, oc, w_out_l.astype(BF16), row(norm_ffn[l]), w_up[l].astype(BF16),
                  conv_w[l].reshape(3, 2, D_FF).transpose(1, 0, 2),
                  conv_b[l].reshape(2, 1, D_FF), w_down[l].astype(BF16),
                  row(final_norm), S, final_norm=(l == depth - 1))
    return xc.reshape(B, S, D)
```

```python
import functools
import math

import jax
import jax.numpy as jnp
import numpy as np
from jax import lax
from jax.experimental import pallas as pl
from jax.experimental.pallas import tpu as pltpu

D_MODEL = 1024
GRID_W = 64
HEAD_DIM = 64
N_HEADS_A = 8
N_KV_A = 2
N_HEADS_B = 4
MLA_Q_RANK = 192
MLA_KV_RANK = 128
MLA_NOPE = 64
MLA_ROPE = 32
MLA_V = 64
N_HEADS_C = 4
DIFF_QK = 32
DIFF_V = 64
D_FF = 2816
ROPE_BASE = 10000.0
EPS = 1e-6

LANES = 128
HALF = LANES // 2
BF16_SUBLANES = 16
VMEM_LIMIT_BYTES = 56 * 1024 * 1024

LOG2E = math.log2(math.e)
assert HEAD_DIM == MLA_V == DIFF_V == 2 * DIFF_QK == HALF

OFF_QA = 0
OFF_KA = OFF_QA + N_HEADS_A * HEAD_DIM
OFF_VA = OFF_KA + LANES
OFF_CQ = OFF_VA + LANES
CQ_PAD = 2 * LANES
OFF_CKV = OFF_CQ + CQ_PAD
OFF_KR = OFF_CKV + MLA_KV_RANK
OFF_QC = OFF_KR + LANES
OFF_KC = OFF_QC + N_HEADS_C * HALF
OFF_VC = OFF_KC + N_HEADS_C * HALF
IN_COLS = OFF_VC + N_HEADS_C * HALF

NRM_QB = 0
NRM_KB = NRM_QB + N_HEADS_B
NRM_QC = NRM_KB + N_HEADS_B
NRM_KC = NRM_QC + N_HEADS_C
NRM_GROUPS = 2 * N_HEADS_B + N_HEADS_C
NRM_MARGIN = 1.03
SAFE_LOG2_RANGE = 55.0

F32 = jnp.float32
BF16 = jnp.bfloat16

TM_IN = 512
TM_FFN = 1024
FFN_CHUNK = 256
FFN_DOWN_GROUP = 6 * FFN_CHUNK
FFN_DOWN_LAG = 2
TQ = 512
TQ_SUB_AB = 512
TQ_SUB_C = 256
TK = 512
SCORE_LOOKAHEAD = 2
HEADS_PER_PIPELINE = 4


def _dot(a, b):
    return jnp.dot(a, b, preferred_element_type=F32)


def _rms(x, n):
    ms = jnp.sum(x * x, axis=-1, keepdims=True) * (1.0 / n)
    return x * lax.rsqrt(ms + EPS)


def _low_lanes():
    return lax.broadcasted_iota(jnp.int32, (1, LANES), 1) < HALF


def _half_rms(x):
    low = _low_lanes()
    sq = x * x
    s_lo = jnp.sum(jnp.where(low, sq, 0.0), axis=-1, keepdims=True)
    s_hi = jnp.sum(jnp.where(low, 0.0, sq), axis=-1, keepdims=True)
    return x * lax.rsqrt(jnp.where(low, s_lo, s_hi) * (1.0 / HALF) + EPS)


def _in_kernel(x_ref, gat_ref, w_ref, cos_a_ref, sin_a_ref, cos_b_ref, sin_b_ref,
               qg_a_ref, kg_a_ref, qg_b_ref, kvg_b_ref, wuq_ref, wukv_ref, sel_ref,
               qa_ref, kta_ref, va_ref, qb_ref, ktb_ref, vb_ref, qc_ref, ktc_ref, vc_ref, nrm_ref):
    x = x_ref[...]
    tm = x.shape[0]
    h = (_rms(x, D_MODEL) * gat_ref[...]).astype(BF16)

    lane = lax.broadcasted_iota(jnp.int32, (1, LANES), 1)
    low = lane < HALF
    row = lax.broadcasted_iota(jnp.int32, (LANES, 1), 0)

    z = _dot(h, w_ref[...])

    def seg(off, width):
        return z[:, off:off + width]

    def swap(y, nf):
        up = pltpu.roll(y, LANES - nf, axis=1)
        down = pltpu.roll(y, nf, axis=1)
        return jnp.where((lane & nf) == 0, up, down)

    def rope(y, c, s, nf):
        return y * c + swap(y, nf) * s

    def group(z, g):
        return z[:, g * LANES:(g + 1) * LANES]

    squares = []

    def note(y):
        squares.append((y * y).astype(BF16))

    def noted_row_sums(first):
        return _dot(jnp.concatenate(squares[first:], axis=-1),
                    sel_ref[first * LANES:len(squares) * LANES])

    cos_b, sin_b = cos_b_ref[...], sin_b_ref[...]
    nf_b = MLA_ROPE // 4
    cq = (_rms(seg(OFF_CQ, CQ_PAD), MLA_Q_RANK) * qg_b_ref[...]).astype(BF16)
    qb = _dot(cq, wuq_ref[...])
    scale_b = (MLA_NOPE + MLA_ROPE) ** -0.5 * LOG2E
    for hh in range(N_HEADS_B):
        y = rope(group(qb, hh), cos_b, sin_b, nf_b) * scale_b
        qb_ref[0, hh] = y.astype(BF16)
        note(y)
    ckv = (_rms(seg(OFF_CKV, MLA_KV_RANK), MLA_KV_RANK) * kvg_b_ref[...]).astype(BF16)
    kvb = _dot(ckv, wukv_ref[...])
    kr = rope(seg(OFF_KR, LANES), cos_b, sin_b, nf_b)
    for hh in range(N_HEADS_B):
        y = group(kvb, hh) + kr
        ktb_ref[0, hh] = y.T.astype(BF16)
        note(y)
        ones_other_half = jnp.where(low, float(hh % 2), float(1 - hh % 2))
        vb_ref[0, hh] = (group(kvb, N_HEADS_B + hh) + ones_other_half).astype(BF16)
    n2 = noted_row_sums(0)
    n_noted_b = len(squares)

    zq = seg(OFF_QC, N_HEADS_C * HALF)
    zk = seg(OFF_KC, N_HEADS_C * HALF)
    zv = seg(OFF_VC, N_HEADS_C * HALF)
    k_groups = []
    for p in range(N_HEADS_C // 2):
        y = group(zq, p) * (DIFF_QK ** -0.5 * LOG2E)
        qc_ref[0, p] = y.astype(BF16)
        note(y)
        k_groups.append(group(zk, p))
        kt = group(zk, p).T
        for sub in range(4):
            mine = (row >= sub * DIFF_QK) & (row < (sub + 1) * DIFF_QK)
            ktc_ref[0, 4 * p + sub, 0] = jnp.where(mine, kt, 0.0).astype(BF16)
        vz = group(zv, p)
        vc_ref[0, 2 * p] = jnp.where(low, vz, 1.0).astype(BF16)
        vc_ref[0, 2 * p + 1] = jnp.where(low, 1.0, vz).astype(BF16)
    for kz in k_groups:
        note(kz)
    n2 = n2 + noted_row_sums(n_noted_b)
    nrm_ref[0] = jnp.max(n2, axis=0, keepdims=True)

    cos_a, sin_a = cos_a_ref[...], sin_a_ref[...]
    nf_a = HEAD_DIM // 4
    zq = seg(OFF_QA, N_HEADS_A * HEAD_DIM)
    for p in range(N_HEADS_A // 2):
        y = rope(_half_rms(group(zq, p)) * qg_a_ref[...], cos_a, sin_a, nf_a)
        qa_ref[0, p] = (y * (HEAD_DIM ** -0.5 * LOG2E)).astype(BF16)
    y = rope(_half_rms(seg(OFF_KA, LANES)) * kg_a_ref[...], cos_a, sin_a, nf_a)
    kt = y.T.astype(BF16)
    zeros = jnp.zeros((HALF, tm), BF16)
    for g in range(N_KV_A):
        kg = kt[g * HALF:(g + 1) * HALF]
        kta_ref[0, 2 * g] = jnp.concatenate([kg, zeros], axis=0)
        kta_ref[0, 2 * g + 1] = jnp.concatenate([zeros, kg], axis=0)
    zv = seg(OFF_VA, LANES)
    va_ref[0, 0] = jnp.where(low, zv, 1.0).astype(BF16)
    va_ref[0, 1] = jnp.where(low, 1.0, zv).astype(BF16)


def _norm_routing():
    sel = np.zeros((NRM_GROUPS, LANES, LANES), np.float32)
    g = 0
    for slot in ([NRM_QB + h for h in range(N_HEADS_B)] + [NRM_KB + h for h in range(N_HEADS_B)]):
        sel[g, :, slot] = 1.0
        g += 1
    for p in range(N_HEADS_C // 2):
        for f in range(2):
            sel[g, f * HALF:(f + 1) * HALF, NRM_QC + 2 * p + f] = 1.0
        g += 1
    for p in range(N_HEADS_C // 2):
        for sub in range(4):
            sel[g, sub * DIFF_QK:(sub + 1) * DIFF_QK, NRM_KC + 4 * p + sub] = 1.0
        g += 1
    assert g == NRM_GROUPS
    return jnp.asarray(sel.reshape(NRM_GROUPS * LANES, LANES), BF16)


def _in_proj(x2d, gat, w_in_r, tabs, qg_a, kg_a, qg_b, kvg_b, wuq_p, wukv_p, B, S):
    T = B * S
    tm = TM_IN
    nst = S // tm
    const = lambda shape: pl.BlockSpec(shape, lambda i: (0,) * len(shape))
    tab = pl.BlockSpec((tm, LANES), lambda i: (i % nst, 0))
    hm = lambda H: pl.BlockSpec((1, H, tm, LANES), lambda i: (i // nst, 0, i % nst, 0))
    hmt = lambda H: pl.BlockSpec((1, H, LANES, tm), lambda i: (i // nst, 0, 0, i % nst))
    sds = lambda H: jax.ShapeDtypeStruct((B, H, S, LANES), BF16)
    sdt = lambda H: jax.ShapeDtypeStruct((B, H, LANES, S), BF16)
    return pl.pallas_call(
        _in_kernel,
        grid=(T // tm,),
        in_specs=[pl.BlockSpec((tm, D_MODEL), lambda i: (i, 0)),
                  const((1, D_MODEL)), const((D_MODEL, IN_COLS)),
                  tab, tab, tab, tab,
                  const((1, LANES)), const((1, LANES)), const((1, CQ_PAD)), const((1, MLA_KV_RANK)),
                  const((CQ_PAD, N_HEADS_B * LANES)), const((MLA_KV_RANK, 2 * N_HEADS_B * LANES)),
                  const((NRM_GROUPS * LANES, LANES))],
        out_specs=[hm(N_HEADS_A // 2), hmt(2 * N_KV_A), hm(N_KV_A),
                   hm(N_HEADS_B), hmt(N_HEADS_B), hm(N_HEADS_B),
                   hm(N_HEADS_C // 2),
                   pl.BlockSpec((1, 2 * N_HEADS_C, 1, LANES, tm),
                                lambda i: (i // nst, 0, i % nst, 0, 0)),
                   hm(N_HEADS_C),
                   pl.BlockSpec((1, 1, LANES), lambda i: (i, 0, 0))],
        out_shape=[sds(N_HEADS_A // 2), sdt(2 * N_KV_A), sds(N_KV_A),
                   sds(N_HEADS_B), sdt(N_HEADS_B), sds(N_HEADS_B),
                   sds(N_HEADS_C // 2),
                   jax.ShapeDtypeStruct((B, 2 * N_HEADS_C, S // tm, LANES, tm), BF16),
                   sds(N_HEADS_C),
                   jax.ShapeDtypeStruct((T // tm, 1, LANES), F32)],
        compiler_params=pltpu.CompilerParams(dimension_semantics=("arbitrary",),
                                             vmem_limit_bytes=VMEM_LIMIT_BYTES),
        name="in_proj",
    )(x2d, gat, w_in_r, *tabs, qg_a, kg_a, qg_b, kvg_b, wuq_p, wukv_p, _norm_routing())


def _lane_group_max(s):
    m = s[:, :LANES]
    for j in range(1, s.shape[1] // LANES):
        m = jnp.maximum(m, s[:, j * LANES:(j + 1) * LANES])
    return m


def _pv_bounded(streams, n_chunks):
    items = [(r, i) for r in range(n_chunks) for i in range(len(streams))]
    accs = [None] * len(streams)
    in_flight = {}
    for k in range(len(items) + SCORE_LOOKAHEAD):
        if k < len(items):
            r, i = items[k]
            scores, _, row_off = streams[i]
            s, off = scores(r), row_off(r)
            in_flight[k] = s if off is None else s + off
        if k >= SCORE_LOOKAHEAD:
            r, i = items[k - SCORE_LOOKAHEAD]
            d = _dot(jnp.exp2(in_flight.pop(k - SCORE_LOOKAHEAD)).astype(BF16), streams[i][1](r))
            accs[i] = d if accs[i] is None else accs[i] + d
    return accs


def _pv_exact(streams, n_chunks, s_ref):
    rows = s_ref.shape[0] // len(streams)
    maxes = []
    for i, (scores, _, row_off) in enumerate(streams):
        m_run = None
        for r in range(n_chunks):
            s = scores(r)
            s_ref[i * rows:(i + 1) * rows, r * TK:(r + 1) * TK] = s
            m, off = _lane_group_max(s), row_off(r)
            if off is not None:
                m = m + off
            m_run = m if m_run is None else jnp.maximum(m_run, m)
        maxes.append(jnp.max(m_run, axis=-1, keepdims=True))
    accs = []
    for i, (_, values, row_off) in enumerate(streams):
        acc = None
        for r in range(n_chunks):
            off = row_off(r)
            shift = maxes[i] if off is None else maxes[i] - off
            p = jnp.exp2(s_ref[i * rows:(i + 1) * rows, r * TK:(r + 1) * TK] - shift)
            d = _dot(p.astype(BF16), values(r))
            acc = d if acc is None else acc + d
        accs.append(acc)
    return accs


def _run_heads(bounded, heads, head_streams, n_chunks, s_ref, acc_ref):
    @pl.when(bounded)
    def _():
        per_head = [head_streams(h) for h in heads]
        n_sub = len(per_head[0])
        accs = _pv_bounded([s for streams in per_head for s in streams], n_chunks)
        for k, h in enumerate(heads):
            acc_ref[h] = jnp.concatenate(accs[k * n_sub:(k + 1) * n_sub], axis=0)

    @pl.when(jnp.logical_not(bounded))
    def _():
        for h in heads:
            acc_ref[h] = jnp.concatenate(_pv_exact(head_streams(h), n_chunks, s_ref), axis=0)


def _normalised_pair(acc_low, acc_high):
    low = _low_lanes()
    num = jnp.where(low, acc_low, acc_high)
    den = pltpu.roll(jnp.where(low, acc_high, acc_low), HALF, axis=1)
    return num / den


def _attn_ab_kernel(bounded_ref, q_ref, kt_ref, v_ref, o_ref, s_ref, acc_ref, *,
                    n_heads, q_index, kt_index, v_index, out_pairs, n_keys):
    n_chunks = n_keys // TK

    def head_streams(h):
        q = q_ref[0, q_index(h)]
        return [(lambda r, qs=q[r0:r0 + TQ_SUB_AB]:
                 _dot(qs, kt_ref[0, kt_index(h), :, r * TK:(r + 1) * TK]),
                 lambda r: v_ref[0, v_index(h), r * TK:(r + 1) * TK, :],
                 lambda r: None) for r0 in range(0, TQ, TQ_SUB_AB)]

    def head_group(j, carry):
        heads = [HEADS_PER_PIPELINE * j + k for k in range(HEADS_PER_PIPELINE)]
        _run_heads(bounded_ref[pl.program_id(0), j] != 0, heads, head_streams, n_chunks, s_ref,
                   acc_ref)
        return carry

    lax.fori_loop(0, n_heads // HEADS_PER_PIPELINE, head_group, 0)
    o_ref[...] = jnp.concatenate([_normalised_pair(acc_ref[a], acc_ref[b]) for a, b in out_pairs],
                                 axis=-1).astype(BF16)


def _attn_c_kernel(bounded_ref, q_ref, kt_ref, v_ref, lq1_ref, lk1_ref, lq2_ref, lk2_ref,
                   subln_ref, o_ref, s_ref, acc_ref, *, n_keys, lam_init):
    assert TQ == TK
    nc = n_keys // TK
    cd = pl.program_id(1)
    col = lax.broadcasted_iota(jnp.int32, (1, TK), 1).astype(F32)
    row = lax.broadcasted_iota(jnp.int32, (TQ_SUB_C, 1), 0).astype(F32)
    subs = [(r, r + TQ_SUB_C) for r in range(0, TQ, TQ_SUB_C)]
    neg_dist = [-jnp.abs((lax.broadcasted_iota(jnp.int32, (TQ_SUB_C, TK), 0) + r0
                          - lax.broadcasted_iota(jnp.int32, (TQ_SUB_C, TK), 1)).astype(F32))
                for r0, _ in subs]

    chunk = [(cd + r) % nc for r in range(nc)]

    def head_bias(hh):
        slope = jnp.float32(LOG2E * 2.0 ** (-8.0 * N_HEADS_C / N_HEADS_C))
        for k in range(N_HEADS_C - 2, -1, -1):
            slope = jnp.where(hh == k, LOG2E * 2.0 ** (-8.0 * (k + 1) / N_HEADS_C), slope)
        key_term, row_off = [None], [[None] * len(subs)]
        for r in range(1, nc):
            side = jnp.where(chunk[r] < cd, slope, -slope)
            key_term.append(col * side)
            base = -slope * (jnp.abs(chunk[r] - cd) * TK).astype(F32)
            row_off.append([base - side * (row + float(r0)) for r0, _ in subs])
        return slope, key_term, row_off

    def head_pair(p, carry):
        bias = [head_bias(2 * p + f) for f in range(2)]

        def streams_of(k):
            f, idx, hh = k // 2, 4 * p + k, 2 * p + k // 2
            slope, key_term, row_off = bias[f]

            def scores(r, si):
                s = _dot(q_ref[0, p][subs[si][0]:subs[si][1]], kt_ref[0, idx, chunk[r]])
                return s + (neg_dist[si] * slope if r == 0 else key_term[r])

            return [(functools.partial(scores, si=si),
                     lambda r: v_ref[0, hh, pl.ds(pl.multiple_of(chunk[r] * TK, TK), TK), :],
                     lambda r, si=si: row_off[r][si]) for si in range(len(subs))]

        _run_heads(bounded_ref[pl.program_id(0), p] != 0, list(range(4)), streams_of, nc, s_ref,
                   acc_ref.at[pl.ds(4 * p, 4)])
        return carry

    lax.fori_loop(0, N_HEADS_C // 2, head_pair, 0)

    lam = (jnp.exp(jnp.sum(lq1_ref[...] * lk1_ref[...], axis=-1, keepdims=True))
           - jnp.exp(jnp.sum(lq2_ref[...] * lk2_ref[...], axis=-1, keepdims=True)) + lam_init)
    outs = []
    for p in range(N_HEADS_C // 2):
        o = (_normalised_pair(acc_ref[4 * p], acc_ref[4 * p + 2])
             - lam * _normalised_pair(acc_ref[4 * p + 1], acc_ref[4 * p + 3]))
        outs.append(_half_rms(o) * subln_ref[...] * (1.0 - lam_init))
    o_ref[...] = jnp.concatenate(outs, axis=-1).astype(BF16)


def _attention(kernel_fn, bounded, q, kt, v, extras, out_width, B, S):
    nq = S // TQ
    hq, hv = q.shape[1], v.shape[1]
    extra_specs = [pl.BlockSpec(e.shape, lambda b, i, f, n=e.ndim: (0,) * n) for e in extras]
    return pl.pallas_call(
        kernel_fn,
        grid_spec=pltpu.PrefetchScalarGridSpec(
            num_scalar_prefetch=1,
            grid=(B, nq),
            in_specs=[pl.BlockSpec((1, hq, TQ, LANES), lambda b, i, f: (b, 0, i, 0)),
                      pl.BlockSpec((1,) + kt.shape[1:],
                                   lambda b, i, f, n=kt.ndim: (b,) + (0,) * (n - 1)),
                      pl.BlockSpec((1, hv, S, LANES), lambda b, i, f: (b, 0, 0, 0))] + extra_specs,
            out_specs=pl.BlockSpec((TQ, out_width), lambda b, i, f: (b * nq + i, 0)),
            scratch_shapes=[pltpu.VMEM((TQ, S), F32),
                            pltpu.VMEM((HEADS_PER_PIPELINE * bounded.shape[1], TQ, LANES), F32)]),
        out_shape=jax.ShapeDtypeStruct((B * S, out_width), BF16),
        compiler_params=pltpu.CompilerParams(dimension_semantics=("arbitrary", "arbitrary"),
                                             vmem_limit_bytes=VMEM_LIMIT_BYTES),
        name=kernel_fn.func.__name__.strip("_"),
    )(bounded, q, kt, v, *extras)


def _mlp_kernel(xp_ref, x_ref, xn_ref, ap_ref, a_ref, an_ref, bp_ref, b_ref, bn_ref,
                cp_ref, c_ref, cn_ref, wo_ref, g_ref, wu_ref, cw_ref, cb_ref, wd_ref, fg_ref,
                o_ref, act_ref, *, tiles_per_seq, final_norm):
    i = pl.program_id(0)
    tm = x_ref.shape[0]
    halo = xp_ref.shape[0]

    def rows(p, m, n):
        return jnp.concatenate([p[...], m[...], n[...]], axis=0)

    o_ext = jnp.concatenate([rows(ap_ref, a_ref, an_ref), rows(bp_ref, b_ref, bn_ref),
                             rows(cp_ref, c_ref, cn_ref)], axis=1)
    x1 = rows(xp_ref, x_ref, xn_ref) + _dot(o_ext, wo_ref[...])
    h = (_rms(x1, D_MODEL) * g_ref[...]).astype(BF16)
    at_start = (i % tiles_per_seq) == 0
    at_end = (i % tiles_per_seq) == tiles_per_seq - 1
    zeros = jnp.zeros((halo, D_MODEL), BF16)
    hext = jnp.concatenate([jnp.where(at_start, zeros, h[:halo]), h[halo:halo + tm],
                            jnp.where(at_end, zeros, h[halo + tm:])], axis=0)

    def branch(k, c0):
        u = _dot(hext, wu_ref[:, k * D_FF + c0:k * D_FF + c0 + FFN_CHUNK])
        cw = cw_ref[k, :, c0:c0 + FFN_CHUNK]
        n = u.shape[0]
        prev = pltpu.roll(u, 1, axis=0)[halo:halo + tm]
        nxt = pltpu.roll(u, n - 1, axis=0)[halo:halo + tm]
        return (prev * cw[0:1] + u[halo:halo + tm] * cw[1:2] + nxt * cw[2:3]
                + cb_ref[k, :, c0:c0 + FFN_CHUNK])

    y = x1[halo:halo + tm]
    done = 0
    for c0 in range(0, D_FF, FFN_CHUNK):
        g = branch(0, c0)
        val = branch(1, c0)
        act_ref[:, c0:c0 + FFN_CHUNK] = (g / (1.0 + jnp.exp(-g)) * val).astype(BF16)
        ready = c0 + FFN_CHUNK - FFN_DOWN_LAG * FFN_CHUNK
        if ready - done >= FFN_DOWN_GROUP:
            y = y + _dot(act_ref[:, done:done + FFN_DOWN_GROUP], wd_ref[done:done + FFN_DOWN_GROUP])
            done += FFN_DOWN_GROUP
    y = y + _dot(act_ref[:, done:], wd_ref[done:])
    if final_norm:
        y = _rms(y, D_MODEL) * fg_ref[...]
    o_ref[...] = y


def _mlp(x2d, oa, ob, oc, w_out, g_ffn, w_up, conv_w, conv_b, w_down, final_g, S, final_norm):
    T = x2d.shape[0]
    tm, halo = TM_FFN, BF16_SUBLANES
    per = tm // halo
    last = T // halo - 1
    resident = lambda shape: pl.BlockSpec(shape, lambda i: (0,) * len(shape),
                                          pipeline_mode=pl.Buffered(1))

    def with_halos(width):
        return [pl.BlockSpec((halo, width), lambda i: (jnp.maximum(i * per - 1, 0), 0)),
                pl.BlockSpec((tm, width), lambda i: (i, 0)),
                pl.BlockSpec((halo, width), lambda i: (jnp.minimum((i + 1) * per, last), 0))]

    return pl.pallas_call(
        functools.partial(_mlp_kernel, tiles_per_seq=S // tm, final_norm=final_norm),
        grid=(T // tm,),
        in_specs=(with_halos(D_MODEL) + with_halos(oa.shape[1]) + with_halos(ob.shape[1])
                  + with_halos(oc.shape[1])
                  + [resident(w_out.shape), resident((1, D_MODEL)),
                     resident((D_MODEL, 2 * D_FF)), resident((2, 3, D_FF)), resident((2, 1, D_FF)),
                     resident((D_FF, D_MODEL)), resident((1, D_MODEL))]),
        out_specs=pl.BlockSpec((tm, D_MODEL), lambda i: (i, 0)),
        out_shape=jax.ShapeDtypeStruct((T, D_MODEL), F32),
        scratch_shapes=[pltpu.VMEM((tm, D_FF), BF16)],
        compiler_params=pltpu.CompilerParams(dimension_semantics=("arbitrary",),
                                             vmem_limit_bytes=VMEM_LIMIT_BYTES),
        name="out_proj_conv_mlp",
    )(x2d, x2d, x2d, oa, oa, oa, ob, ob, ob, oc, oc, oc, w_out, g_ffn,
      w_up, conv_w, conv_b, w_down, final_g)


def _pad_groups(w, n_groups, width, offset=0):
    r = w.shape[0]
    w = w.reshape(r, n_groups, width)
    w = jnp.pad(w, ((0, 0), (0, 0), (offset, LANES - width - offset)))
    return w.reshape(r, n_groups * LANES)


def _rope_core(S, dim):
    nf = dim // 4
    t = jnp.arange(S, dtype=jnp.int32)
    rows = (t // GRID_W).astype(F32)
    cols = (t % GRID_W).astype(F32)
    inv = ROPE_BASE ** (-jnp.arange(nf, dtype=F32) / nf)
    ar = rows[:, None] * inv
    ac = cols[:, None] * inv
    cos = jnp.concatenate([jnp.cos(ar), jnp.cos(ar), jnp.cos(ac), jnp.cos(ac)], axis=-1)
    sin = jnp.concatenate([-jnp.sin(ar), jnp.sin(ar), -jnp.sin(ac), jnp.sin(ac)], axis=-1)
    return cos, sin


def _rope_tables(S):
    cos_a, sin_a = _rope_core(S, HEAD_DIM)
    cos_b, sin_b = _rope_core(S, MLA_ROPE)
    pad = LANES - MLA_NOPE - MLA_ROPE
    return (jnp.tile(cos_a, (1, 2)), jnp.tile(sin_a, (1, 2)),
            jnp.concatenate([jnp.ones((S, MLA_NOPE), F32), cos_b, jnp.zeros((S, pad), F32)], -1),
            jnp.concatenate([jnp.zeros((S, MLA_NOPE), F32), sin_b, jnp.zeros((S, pad), F32)], -1))


def _prep_w_in(w):
    a_end = (N_HEADS_A + 2 * N_KV_A) * HEAD_DIM
    b_end = a_end + MLA_Q_RANK + MLA_KV_RANK + MLA_ROPE
    cq = w[:, a_end:a_end + MLA_Q_RANK]
    ckv = w[:, a_end + MLA_Q_RANK:a_end + MLA_Q_RANK + MLA_KV_RANK]
    kr = w[:, b_end - MLA_ROPE:b_end]
    out = jnp.concatenate([
        w[:, :a_end],
        jnp.pad(cq, ((0, 0), (0, CQ_PAD - MLA_Q_RANK))),
        ckv,
        jnp.pad(kr, ((0, 0), (MLA_NOPE, LANES - MLA_NOPE - MLA_ROPE))),
        w[:, b_end:],
    ], axis=-1).astype(BF16)
    assert out.shape[1] == IN_COLS
    return out


def _twice(v):
    return jnp.tile(v, 2).reshape(1, 2 * v.shape[0])


def kernel(x, norm_attn, w_in, q_norm_a, k_norm_a, q_a_norm_b, w_uq_b, kv_a_norm_b, w_ukv_b,
           lambda_q1_c, lambda_k1_c, lambda_q2_c, lambda_k2_c, subln_c, w_out,
           norm_ffn, w_up, conv_w, conv_b, w_down, final_norm):
    B, S, D = x.shape
    depth = w_in.shape[0]
    assert D == D_MODEL and S % max(TM_IN, TM_FFN, TQ, TK) == 0 and S % GRID_W == 0
    assert TM_IN == TK
    T = B * S

    tabs = _rope_tables(S)
    row = lambda v: v.reshape(1, -1)
    a_pairs = [(j, j + N_HEADS_A // 2) for j in range(N_HEADS_A // 2)]
    a_rows = np.concatenate([np.arange(h * HEAD_DIM, (h + 1) * HEAD_DIM)
                             for pair in a_pairs for h in pair])
    xc = x.reshape(T, D)
    for l in range(depth):
        lam_init = 0.8 - 0.6 * math.exp(-0.3 * l)
        wuq_p = jnp.pad(_pad_groups(w_uq_b[l], N_HEADS_B, MLA_NOPE + MLA_ROPE),
                        ((0, CQ_PAD - MLA_Q_RANK), (0, 0))).astype(BF16)
        wukv = w_ukv_b[l].reshape(MLA_KV_RANK, N_HEADS_B, MLA_NOPE + MLA_V)
        v_cols = [_pad_groups(wukv[:, h, MLA_NOPE:], 1, MLA_V, offset=HALF * (h % 2))
                  for h in range(N_HEADS_B)]
        wukv_p = jnp.concatenate(
            [_pad_groups(wukv[:, :, :MLA_NOPE].reshape(MLA_KV_RANK, -1), N_HEADS_B, MLA_NOPE)]
            + v_cols, axis=-1).astype(BF16)

        qa, kta, va, qb, ktb, vb, qc, ktc, vc, nrm = _in_proj(
            xc, row(norm_attn[l]), _prep_w_in(w_in[l]), tabs,
            _twice(q_norm_a[l]), _twice(k_norm_a[l]),
            jnp.pad(q_a_norm_b[l], (0, CQ_PAD - MLA_Q_RANK)).reshape(1, CQ_PAD),
            row(kv_a_norm_b[l]), wuq_p, wukv_p, B, S)

        nrm = jnp.max(nrm.reshape(B, -1, LANES), axis=1)

        def bounded(q0, nq, q_rep, k0, nk, k_rep):
            q2 = jnp.repeat(nrm[:, q0:q0 + nq], q_rep, axis=1)
            k2 = jnp.repeat(nrm[:, k0:k0 + nk], k_rep, axis=1)
            ok = q2 * k2 * NRM_MARGIN <= SAFE_LOG2_RANGE ** 2
            return jnp.all(ok.reshape(B, -1, HEADS_PER_PIPELINE), axis=-1).astype(jnp.int32)

        qk_a = (HEAD_DIM * jnp.max(jnp.abs(q_norm_a[l])) * jnp.max(jnp.abs(k_norm_a[l]))
                * (HEAD_DIM ** -0.5 * LOG2E))
        bounded_a = jnp.broadcast_to(
            (qk_a * NRM_MARGIN <= SAFE_LOG2_RANGE).astype(jnp.int32),
            (B, N_HEADS_A // HEADS_PER_PIPELINE))

        group = N_HEADS_A // N_KV_A
        oa = _attention(
            functools.partial(_attn_ab_kernel, n_heads=N_HEADS_A, q_index=lambda h: h // 2,
                              kt_index=lambda h: 2 * (h // group) + h % 2,
                              v_index=lambda h: h // group, out_pairs=a_pairs, n_keys=S),
            bounded_a, qa, kta, va, [], N_HEADS_A * HEAD_DIM, B, S)
        same = lambda h: h
        ob = _attention(
            functools.partial(_attn_ab_kernel, n_heads=N_HEADS_B, q_index=same, kt_index=same,
                              v_index=same,
                              out_pairs=[(2 * j, 2 * j + 1) for j in range(N_HEADS_B // 2)],
                              n_keys=S),
            bounded(NRM_QB, N_HEADS_B, 1, NRM_KB, N_HEADS_B, 1),
            qb, ktb, vb, [], N_HEADS_B * MLA_V, B, S)
        oc = _attention(
            functools.partial(_attn_c_kernel, n_keys=S, lam_init=lam_init),
            bounded(NRM_QC, N_HEADS_C, 2, NRM_KC, 2 * N_HEADS_C, 1),
            qc, ktc, vc,
            [row(lambda_q1_c[l]), row(lambda_k1_c[l]), row(lambda_q2_c[l]), row(lambda_k2_c[l]),
             _twice(subln_c[l])],
            N_HEADS_C * DIFF_V, B, S)

        w_out_l = jnp.concatenate([w_out[l][a_rows], w_out[l][N_HEADS_A * HEAD_DIM:]], axis=0)
        xc = _mlp(xc, oa, ob, oc, w_out_l.astype(BF16), row(norm_ffn[l]), w_up[l].astype(BF16),
                  conv_w[l].reshape(3, 2, D_FF).transpose(1, 0, 2),
                  conv_b[l].reshape(2, 1, D_FF), w_down[l].astype(BF16),
                  row(final_norm), S, final_norm=(l == depth - 1))
    return xc.reshape(B, S, D)
```

```python
import functools
import math

import jax
import jax.numpy as jnp
import numpy as np
from jax import lax
from jax.experimental import pallas as pl
from jax.experimental.pallas import tpu as pltpu

D_MODEL = 1024
GRID_W = 64
HEAD_DIM = 64
N_HEADS_A = 8
N_KV_A = 2
N_HEADS_B = 4
MLA_Q_RANK = 192
MLA_KV_RANK = 128
MLA_NOPE = 64
MLA_ROPE = 32
MLA_V = 64
N_HEADS_C = 4
DIFF_QK = 32
DIFF_V = 64
D_FF = 2816
ROPE_BASE = 10000.0
EPS = 1e-6

LANES = 128
HALF = LANES // 2
BF16_SUBLANES = 16
VMEM_LIMIT_BYTES = 56 * 1024 * 1024

LOG2E = math.log2(math.e)
assert HEAD_DIM == MLA_V == DIFF_V == 2 * DIFF_QK == HALF

OFF_QA = 0
OFF_KA = OFF_QA + N_HEADS_A * HEAD_DIM
OFF_VA = OFF_KA + LANES
OFF_CQ = OFF_VA + LANES
CQ_PAD = 2 * LANES
OFF_CKV = OFF_CQ + CQ_PAD
OFF_KR = OFF_CKV + MLA_KV_RANK
OFF_QC = OFF_KR + LANES
OFF_KC = OFF_QC + N_HEADS_C * HALF
OFF_VC = OFF_KC + N_HEADS_C * HALF
IN_COLS = OFF_VC + N_HEADS_C * HALF

NRM_QB = 0
NRM_KB = NRM_QB + N_HEADS_B
NRM_QC = NRM_KB + N_HEADS_B
NRM_KC = NRM_QC + N_HEADS_C
NRM_GROUPS = 2 * N_HEADS_B + N_HEADS_C
NRM_MARGIN = 1.03
SAFE_LOG2_RANGE = 55.0

F32 = jnp.float32
BF16 = jnp.bfloat16

TM_IN = 512
TM_FFN = 512
FFN_CHUNK = 256
FFN_DOWN_GROUP = 6 * FFN_CHUNK
FFN_DOWN_LAG = 2
TQ = 512
TQ_SUB_AB = 512
TQ_SUB_C = 256
TK = 512
SCORE_LOOKAHEAD = 2
HEADS_PER_PIPELINE = 4
HEADS_PER_PIPELINE_A = 8
SUB_HEADS_PER_PIPELINE_C = 8


def _dot(a, b):
    return jnp.dot(a, b, preferred_element_type=F32)


def _rms(x, n):
    ms = jnp.sum(x * x, axis=-1, keepdims=True) * (1.0 / n)
    return x * lax.rsqrt(ms + EPS)


def _low_lanes():
    return lax.broadcasted_iota(jnp.int32, (1, LANES), 1) < HALF


def _half_rms(x):
    low = _low_lanes()
    sq = x * x
    s_lo = jnp.sum(jnp.where(low, sq, 0.0), axis=-1, keepdims=True)
    s_hi = jnp.sum(jnp.where(low, 0.0, sq), axis=-1, keepdims=True)
    return x * lax.rsqrt(jnp.where(low, s_lo, s_hi) * (1.0 / HALF) + EPS)


def _in_kernel(x_ref, gat_ref, w_ref, cos_a_ref, sin_a_ref, cos_b_ref, sin_b_ref,
               qg_a_ref, kg_a_ref, qg_b_ref, kvg_b_ref, wuq_ref, wukv_ref, sel_ref,
               qa_ref, kta_ref, va_ref, qb_ref, ktb_ref, vb_ref, qc_ref, ktc_ref, vc_ref, nrm_ref):
    x = x_ref[...]
    tm = x.shape[0]
    h = (_rms(x, D_MODEL) * gat_ref[...]).astype(BF16)

    lane = lax.broadcasted_iota(jnp.int32, (1, LANES), 1)
    low = lane < HALF
    row = lax.broadcasted_iota(jnp.int32, (LANES, 1), 0)

    z = _dot(h, w_ref[...])

    def seg(off, width):
        return z[:, off:off + width]

    def swap(y, nf):
        up = pltpu.roll(y, LANES - nf, axis=1)
        down = pltpu.roll(y, nf, axis=1)
        return jnp.where((lane & nf) == 0, up, down)

    def rope(y, c, s, nf):
        return y * c + swap(y, nf) * s

    def group(z, g):
        return z[:, g * LANES:(g + 1) * LANES]

    squares = []

    def note(y):
        squares.append((y * y).astype(BF16))

    def noted_row_sums(first):
        return _dot(jnp.concatenate(squares[first:], axis=-1),
                    sel_ref[first * LANES:len(squares) * LANES])

    cos_b, sin_b = cos_b_ref[...], sin_b_ref[...]
    nf_b = MLA_ROPE // 4
    cq = (_rms(seg(OFF_CQ, CQ_PAD), MLA_Q_RANK) * qg_b_ref[...]).astype(BF16)
    qb = _dot(cq, wuq_ref[...])
    scale_b = (MLA_NOPE + MLA_ROPE) ** -0.5 * LOG2E
    for hh in range(N_HEADS_B):
        y = rope(group(qb, hh), cos_b, sin_b, nf_b) * scale_b
        qb_ref[0, hh] = y.astype(BF16)
        note(y)
    ckv = (_rms(seg(OFF_CKV, MLA_KV_RANK), MLA_KV_RANK) * kvg_b_ref[...]).astype(BF16)
    kvb = _dot(ckv, wukv_ref[...])
    kr = rope(seg(OFF_KR, LANES), cos_b, sin_b, nf_b)
    for hh in range(N_HEADS_B):
        y = group(kvb, hh) + kr
        ktb_ref[0, hh] = y.T.astype(BF16)
        note(y)
        ones_other_half = jnp.where(low, float(hh % 2), float(1 - hh % 2))
        vb_ref[0, hh] = (group(kvb, N_HEADS_B + hh) + ones_other_half).astype(BF16)
    n2 = noted_row_sums(0)
    n_noted_b = len(squares)

    zq = seg(OFF_QC, N_HEADS_C * HALF)
    zk = seg(OFF_KC, N_HEADS_C * HALF)
    zv = seg(OFF_VC, N_HEADS_C * HALF)
    k_groups = []
    for p in range(N_HEADS_C // 2):
        y = group(zq, p) * (DIFF_QK ** -0.5 * LOG2E)
        qc_ref[0, p] = y.astype(BF16)
        note(y)
        k_groups.append(group(zk, p))
        kt = group(zk, p).T
        for sub in range(4):
            mine = (row >= sub * DIFF_QK) & (row < (sub + 1) * DIFF_QK)
            ktc_ref[0, 4 * p + sub, 0] = jnp.where(mine, kt, 0.0).astype(BF16)
        vz = group(zv, p)
        vc_ref[0, 2 * p] = jnp.where(low, vz, 1.0).astype(BF16)
        vc_ref[0, 2 * p + 1] = jnp.where(low, 1.0, vz).astype(BF16)
    for kz in k_groups:
        note(kz)
    n2 = n2 + noted_row_sums(n_noted_b)
    nrm_ref[0] = jnp.max(n2, axis=0, keepdims=True)

    cos_a, sin_a = cos_a_ref[...], sin_a_ref[...]
    nf_a = HEAD_DIM // 4
    zq = seg(OFF_QA, N_HEADS_A * HEAD_DIM)
    for p in range(N_HEADS_A // 2):
        y = rope(_half_rms(group(zq, p)) * qg_a_ref[...], cos_a, sin_a, nf_a)
        qa_ref[0, p] = (y * (HEAD_DIM ** -0.5 * LOG2E)).astype(BF16)
    y = rope(_half_rms(seg(OFF_KA, LANES)) * kg_a_ref[...], cos_a, sin_a, nf_a)
    kt = y.T.astype(BF16)
    zeros = jnp.zeros((HALF, tm), BF16)
    for g in range(N_KV_A):
        kg = kt[g * HALF:(g + 1) * HALF]
        kta_ref[0, 2 * g] = jnp.concatenate([kg, zeros], axis=0)
        kta_ref[0, 2 * g + 1] = jnp.concatenate([zeros, kg], axis=0)
    zv = seg(OFF_VA, LANES)
    va_ref[0, 0] = jnp.where(low, zv, 1.0).astype(BF16)
    va_ref[0, 1] = jnp.where(low, 1.0, zv).astype(BF16)


def _norm_routing():
    sel = np.zeros((NRM_GROUPS, LANES, LANES), np.float32)
    g = 0
    for slot in ([NRM_QB + h for h in range(N_HEADS_B)] + [NRM_KB + h for h in range(N_HEADS_B)]):
        sel[g, :, slot] = 1.0
        g += 1
    for p in range(N_HEADS_C // 2):
        for f in range(2):
            sel[g, f * HALF:(f + 1) * HALF, NRM_QC + 2 * p + f] = 1.0
        g += 1
    for p in range(N_HEADS_C // 2):
        for sub in range(4):
            sel[g, sub * DIFF_QK:(sub + 1) * DIFF_QK, NRM_KC + 4 * p + sub] = 1.0
        g += 1
    assert g == NRM_GROUPS
    return jnp.asarray(sel.reshape(NRM_GROUPS * LANES, LANES), BF16)


def _in_proj(x2d, gat, w_in_r, tabs, qg_a, kg_a, qg_b, kvg_b, wuq_p, wukv_p, B, S):
    T = B * S
    tm = TM_IN
    nst = S // tm
    const = lambda shape: pl.BlockSpec(shape, lambda i: (0,) * len(shape))
    tab = pl.BlockSpec((tm, LANES), lambda i: (i % nst, 0))
    hm = lambda H: pl.BlockSpec((1, H, tm, LANES), lambda i: (i // nst, 0, i % nst, 0))
    hmt = lambda H: pl.BlockSpec((1, H, LANES, tm), lambda i: (i // nst, 0, 0, i % nst))
    sds = lambda H: jax.ShapeDtypeStruct((B, H, S, LANES), BF16)
    sdt = lambda H: jax.ShapeDtypeStruct((B, H, LANES, S), BF16)
    return pl.pallas_call(
        _in_kernel,
        grid=(T // tm,),
        in_specs=[pl.BlockSpec((tm, D_MODEL), lambda i: (i, 0)),
                  const((1, D_MODEL)), const((D_MODEL, IN_COLS)),
                  tab, tab, tab, tab,
                  const((1, LANES)), const((1, LANES)), const((1, CQ_PAD)), const((1, MLA_KV_RANK)),
                  const((CQ_PAD, N_HEADS_B * LANES)), const((MLA_KV_RANK, 2 * N_HEADS_B * LANES)),
                  const((NRM_GROUPS * LANES, LANES))],
        out_specs=[hm(N_HEADS_A // 2), hmt(2 * N_KV_A), hm(N_KV_A),
                   hm(N_HEADS_B), hmt(N_HEADS_B), hm(N_HEADS_B),
                   hm(N_HEADS_C // 2),
                   pl.BlockSpec((1, 2 * N_HEADS_C, 1, LANES, tm),
                                lambda i: (i // nst, 0, i % nst, 0, 0)),
                   hm(N_HEADS_C),
                   pl.BlockSpec((1, 1, LANES), lambda i: (i, 0, 0))],
        out_shape=[sds(N_HEADS_A // 2), sdt(2 * N_KV_A), sds(N_KV_A),
                   sds(N_HEADS_B), sdt(N_HEADS_B), sds(N_HEADS_B),
                   sds(N_HEADS_C // 2),
                   jax.ShapeDtypeStruct((B, 2 * N_HEADS_C, S // tm, LANES, tm), BF16),
                   sds(N_HEADS_C),
                   jax.ShapeDtypeStruct((T // tm, 1, LANES), F32)],
        compiler_params=pltpu.CompilerParams(dimension_semantics=("arbitrary",),
                                             vmem_limit_bytes=VMEM_LIMIT_BYTES),
        name="in_proj",
    )(x2d, gat, w_in_r, *tabs, qg_a, kg_a, qg_b, kvg_b, wuq_p, wukv_p, _norm_routing())


def _lane_group_max(s):
    m = s[:, :LANES]
    for j in range(1, s.shape[1] // LANES):
        m = jnp.maximum(m, s[:, j * LANES:(j + 1) * LANES])
    return m


def _pv_bounded(streams, n_chunks):
    items = [(r, i) for r in range(n_chunks) for i in range(len(streams))]
    accs = [None] * len(streams)
    in_flight = {}
    for k in range(len(items) + SCORE_LOOKAHEAD):
        if k < len(items):
            r, i = items[k]
            scores, _, row_off = streams[i]
            s, off = scores(r), row_off(r)
            in_flight[k] = s if off is None else s + off
        if k >= SCORE_LOOKAHEAD:
            r, i = items[k - SCORE_LOOKAHEAD]
            d = _dot(jnp.exp2(in_flight.pop(k - SCORE_LOOKAHEAD)).astype(BF16), streams[i][1](r))
            accs[i] = d if accs[i] is None else accs[i] + d
    return accs


def _pv_exact(streams, n_chunks, s_ref):
    rows = s_ref.shape[0] // len(streams)
    maxes = []
    for i, (scores, _, row_off) in enumerate(streams):
        m_run = None
        for r in range(n_chunks):
            s = scores(r)
            s_ref[i * rows:(i + 1) * rows, r * TK:(r + 1) * TK] = s
            m, off = _lane_group_max(s), row_off(r)
            if off is not None:
                m = m + off
            m_run = m if m_run is None else jnp.maximum(m_run, m)
        maxes.append(jnp.max(m_run, axis=-1, keepdims=True))
    accs = []
    for i, (_, values, row_off) in enumerate(streams):
        acc = None
        for r in range(n_chunks):
            off = row_off(r)
            shift = maxes[i] if off is None else maxes[i] - off
            p = jnp.exp2(s_ref[i * rows:(i + 1) * rows, r * TK:(r + 1) * TK] - shift)
            d = _dot(p.astype(BF16), values(r))
            acc = d if acc is None else acc + d
        accs.append(acc)
    return accs


def _run_heads(bounded, heads, head_streams, n_chunks, s_ref, acc_ref):
    @pl.when(bounded)
    def _():
        per_head = [head_streams(h) for h in heads]
        n_sub = len(per_head[0])
        accs = _pv_bounded([s for streams in per_head for s in streams], n_chunks)
        for k, h in enumerate(heads):
            acc_ref[h] = jnp.concatenate(accs[k * n_sub:(k + 1) * n_sub], axis=0)

    @pl.when(jnp.logical_not(bounded))
    def _():
        for h in heads:
            acc_ref[h] = jnp.concatenate(_pv_exact(head_streams(h), n_chunks, s_ref), axis=0)


def _normalised_pair(acc_low, acc_high):
    low = _low_lanes()
    num = jnp.where(low, acc_low, acc_high)
    den = pltpu.roll(jnp.where(low, acc_high, acc_low), HALF, axis=1)
    return num / den


def _attn_ab_kernel(bounded_ref, q_ref, kt_ref, v_ref, o_ref, s_ref, acc_ref, *,
                    n_heads, per_pipeline, q_index, kt_index, v_index, out_pairs, n_keys):
    n_chunks = n_keys // TK

    def head_streams(h):
        q = q_ref[0, q_index(h)]
        return [(lambda r, qs=q[r0:r0 + TQ_SUB_AB]:
                 _dot(qs, kt_ref[0, kt_index(h), :, r * TK:(r + 1) * TK]),
                 lambda r: v_ref[0, v_index(h), r * TK:(r + 1) * TK, :],
                 lambda r: None) for r0 in range(0, TQ, TQ_SUB_AB)]

    def head_group(j, carry):
        heads = [per_pipeline * j + k for k in range(per_pipeline)]
        _run_heads(bounded_ref[pl.program_id(0), j] != 0, heads, head_streams, n_chunks, s_ref,
                   acc_ref)
        return carry

    if n_heads == per_pipeline:
        head_group(0, None)
    else:
        lax.fori_loop(0, n_heads // per_pipeline, head_group, 0)
    o_ref[...] = jnp.concatenate([_normalised_pair(acc_ref[a], acc_ref[b]) for a, b in out_pairs],
                                 axis=-1).astype(BF16)


def _attn_c_kernel(bounded_ref, q_ref, kt_ref, v_ref, lq1_ref, lk1_ref, lq2_ref, lk2_ref,
                   subln_ref, o_ref, s_ref, acc_ref, *, n_keys, lam_init):
    assert TQ == TK
    nc = n_keys // TK
    cd = pl.program_id(1)
    col = lax.broadcasted_iota(jnp.int32, (1, TK), 1).astype(F32)
    row = lax.broadcasted_iota(jnp.int32, (TQ_SUB_C, 1), 0).astype(F32)
    subs = [(r, r + TQ_SUB_C) for r in range(0, TQ, TQ_SUB_C)]
    neg_dist = [-jnp.abs((lax.broadcasted_iota(jnp.int32, (TQ_SUB_C, TK), 0) + r0
                          - lax.broadcasted_iota(jnp.int32, (TQ_SUB_C, TK), 1)).astype(F32))
                for r0, _ in subs]

    chunk = [(cd + r) % nc for r in range(nc)]

    def head_bias(hh):
        slope = LOG2E * 2.0 ** (-8.0 * (hh + 1) / N_HEADS_C)
        key_term, row_off = [None], [[None] * len(subs)]
        for r in range(1, nc):
            side = jnp.where(chunk[r] < cd, slope, -slope)
            key_term.append(col * side)
            base = -slope * (jnp.abs(chunk[r] - cd) * TK).astype(F32)
            row_off.append([base - side * (row + float(r0)) for r0, _ in subs])
        return slope, key_term, row_off

    assert SUB_HEADS_PER_PIPELINE_C == 2 * N_HEADS_C
    bias = [head_bias(hh) for hh in range(N_HEADS_C)]

    def streams_of(idx):
        hh = idx // 2
        slope, key_term, row_off = bias[hh]

        def scores(r, si):
            s = _dot(q_ref[0, hh // 2][subs[si][0]:subs[si][1]], kt_ref[0, idx, chunk[r]])
            return s + (neg_dist[si] * slope if r == 0 else key_term[r])

        return [(functools.partial(scores, si=si),
                 lambda r: v_ref[0, hh, pl.ds(pl.multiple_of(chunk[r] * TK, TK), TK), :],
                 lambda r, si=si: row_off[r][si]) for si in range(len(subs))]

    _run_heads(bounded_ref[pl.program_id(0), 0] != 0, list(range(2 * N_HEADS_C)), streams_of, nc,
               s_ref, acc_ref)

    lam = (jnp.exp(jnp.sum(lq1_ref[...] * lk1_ref[...], axis=-1, keepdims=True))
           - jnp.exp(jnp.sum(lq2_ref[...] * lk2_ref[...], axis=-1, keepdims=True)) + lam_init)
    outs = []
    for p in range(N_HEADS_C // 2):
        o = (_normalised_pair(acc_ref[4 * p], acc_ref[4 * p + 2])
             - lam * _normalised_pair(acc_ref[4 * p + 1], acc_ref[4 * p + 3]))
        outs.append(_half_rms(o) * subln_ref[...] * (1.0 - lam_init))
    o_ref[...] = jnp.concatenate(outs, axis=-1).astype(BF16)


def _attention(kernel_fn, bounded, n_acc, q, kt, v, extras, out_width, B, S):
    nq = S // TQ
    hq, hv = q.shape[1], v.shape[1]
    extra_specs = [pl.BlockSpec(e.shape, lambda b, i, f, n=e.ndim: (0,) * n) for e in extras]
    return pl.pallas_call(
        kernel_fn,
        grid_spec=pltpu.PrefetchScalarGridSpec(
            num_scalar_prefetch=1,
            grid=(B, nq),
            in_specs=[pl.BlockSpec((1, hq, TQ, LANES), lambda b, i, f: (b, 0, i, 0)),
                      pl.BlockSpec((1,) + kt.shape[1:],
                                   lambda b, i, f, n=kt.ndim: (b,) + (0,) * (n - 1)),
                      pl.BlockSpec((1, hv, S, LANES), lambda b, i, f: (b, 0, 0, 0))] + extra_specs,
            out_specs=pl.BlockSpec((TQ, out_width), lambda b, i, f: (b * nq + i, 0)),
            scratch_shapes=[pltpu.VMEM((TQ, S), F32),
                            pltpu.VMEM((n_acc, TQ, LANES), F32)]),
        out_shape=jax.ShapeDtypeStruct((B * S, out_width), BF16),
        compiler_params=pltpu.CompilerParams(dimension_semantics=("arbitrary", "arbitrary"),
                                             vmem_limit_bytes=VMEM_LIMIT_BYTES),
        name=kernel_fn.func.__name__.strip("_"),
    )(bounded, q, kt, v, *extras)


def _mlp_kernel(xp_ref, x_ref, xn_ref, ap_ref, a_ref, an_ref, bp_ref, b_ref, bn_ref,
                cp_ref, c_ref, cn_ref, wo_ref, g_ref, wu_ref, cw_ref, cb_ref, wd_ref, fg_ref,
                o_ref, act_ref, *, tiles_per_seq, final_norm):
    i = pl.program_id(0)
    tm = x_ref.shape[0]
    halo = xp_ref.shape[0]

    def rows(p, m, n):
        return jnp.concatenate([p[...], m[...], n[...]], axis=0)

    o_ext = jnp.concatenate([rows(ap_ref, a_ref, an_ref), rows(bp_ref, b_ref, bn_ref),
                             rows(cp_ref, c_ref, cn_ref)], axis=1)
    x1 = rows(xp_ref, x_ref, xn_ref) + _dot(o_ext, wo_ref[...])
    h = (_rms(x1, D_MODEL) * g_ref[...]).astype(BF16)
    at_start = (i % tiles_per_seq) == 0
    at_end = (i % tiles_per_seq) == tiles_per_seq - 1
    zeros = jnp.zeros((halo, D_MODEL), BF16)
    hext = jnp.concatenate([jnp.where(at_start, zeros, h[:halo]), h[halo:halo + tm],
                            jnp.where(at_end, zeros, h[halo + tm:])], axis=0)

    def branch(k, c0):
        u = _dot(hext, wu_ref[:, k * D_FF + c0:k * D_FF + c0 + FFN_CHUNK])
        cw = cw_ref[k, :, c0:c0 + FFN_CHUNK]
        n = u.shape[0]
        prev = pltpu.roll(u, 1, axis=0)[halo:halo + tm]
        nxt = pltpu.roll(u, n - 1, axis=0)[halo:halo + tm]
        return (prev * cw[0:1] + u[halo:halo + tm] * cw[1:2] + nxt * cw[2:3]
                + cb_ref[k, :, c0:c0 + FFN_CHUNK])

    y = x1[halo:halo + tm]
    done = 0
    for c0 in range(0, D_FF, FFN_CHUNK):
        g = branch(0, c0)
        val = branch(1, c0)
        act_ref[:, c0:c0 + FFN_CHUNK] = (g / (1.0 + jnp.exp(-g)) * val).astype(BF16)
        ready = c0 + FFN_CHUNK - FFN_DOWN_LAG * FFN_CHUNK
        if ready - done >= FFN_DOWN_GROUP:
            y = y + _dot(act_ref[:, done:done + FFN_DOWN_GROUP], wd_ref[done:done + FFN_DOWN_GROUP])
            done += FFN_DOWN_GROUP
    y = y + _dot(act_ref[:, done:], wd_ref[done:])
    if final_norm:
        y = _rms(y, D_MODEL) * fg_ref[...]
    o_ref[...] = y


def _mlp(x2d, oa, ob, oc, w_out, g_ffn, w_up, conv_w, conv_b, w_down, final_g, S, final_norm):
    T = x2d.shape[0]
    tm, halo = TM_FFN, BF16_SUBLANES
    per = tm // halo
    last = T // halo - 1
    resident = lambda shape: pl.BlockSpec(shape, lambda i: (0,) * len(shape),
                                          pipeline_mode=pl.Buffered(1))

    def with_halos(width):
        return [pl.BlockSpec((halo, width), lambda i: (jnp.maximum(i * per - 1, 0), 0)),
                pl.BlockSpec((tm, width), lambda i: (i, 0)),
                pl.BlockSpec((halo, width), lambda i: (jnp.minimum((i + 1) * per, last), 0))]

    return pl.pallas_call(
        functools.partial(_mlp_kernel, tiles_per_seq=S // tm, final_norm=final_norm),
        grid=(T // tm,),
        in_specs=(with_halos(D_MODEL) + with_halos(oa.shape[1]) + with_halos(ob.shape[1])
                  + with_halos(oc.shape[1])
                  + [resident(w_out.shape), resident((1, D_MODEL)),
                     resident((D_MODEL, 2 * D_FF)), resident((2, 3, D_FF)), resident((2, 1, D_FF)),
                     resident((D_FF, D_MODEL)), resident((1, D_MODEL))]),
        out_specs=pl.BlockSpec((tm, D_MODEL), lambda i: (i, 0)),
        out_shape=jax.ShapeDtypeStruct((T, D_MODEL), F32),
        scratch_shapes=[pltpu.VMEM((tm, D_FF), BF16)],
        compiler_params=pltpu.CompilerParams(dimension_semantics=("arbitrary",),
                                             vmem_limit_bytes=VMEM_LIMIT_BYTES),
        name="out_proj_conv_mlp",
    )(x2d, x2d, x2d, oa, oa, oa, ob, ob, ob, oc, oc, oc, w_out, g_ffn,
      w_up, conv_w, conv_b, w_down, final_g)


def _pad_groups(w, n_groups, width, offset=0):
    r = w.shape[0]
    w = w.reshape(r, n_groups, width)
    w = jnp.pad(w, ((0, 0), (0, 0), (offset, LANES - width - offset)))
    return w.reshape(r, n_groups * LANES)


def _rope_core(S, dim):
    nf = dim // 4
    t = jnp.arange(S, dtype=jnp.int32)
    rows = (t // GRID_W).astype(F32)
    cols = (t % GRID_W).astype(F32)
    inv = ROPE_BASE ** (-jnp.arange(nf, dtype=F32) / nf)
    ar = rows[:, None] * inv
    ac = cols[:, None] * inv
    cos = jnp.concatenate([jnp.cos(ar), jnp.cos(ar), jnp.cos(ac), jnp.cos(ac)], axis=-1)
    sin = jnp.concatenate([-jnp.sin(ar), jnp.sin(ar), -jnp.sin(ac), jnp.sin(ac)], axis=-1)
    return cos, sin


def _rope_tables(S):
    cos_a, sin_a = _rope_core(S, HEAD_DIM)
    cos_b, sin_b = _rope_core(S, MLA_ROPE)
    pad = LANES - MLA_NOPE - MLA_ROPE
    return (jnp.tile(cos_a, (1, 2)), jnp.tile(sin_a, (1, 2)),
            jnp.concatenate([jnp.ones((S, MLA_NOPE), F32), cos_b, jnp.zeros((S, pad), F32)], -1),
            jnp.concatenate([jnp.zeros((S, MLA_NOPE), F32), sin_b, jnp.zeros((S, pad), F32)], -1))


def _prep_w_in(w):
    a_end = (N_HEADS_A + 2 * N_KV_A) * HEAD_DIM
    b_end = a_end + MLA_Q_RANK + MLA_KV_RANK + MLA_ROPE
    cq = w[:, a_end:a_end + MLA_Q_RANK]
    ckv = w[:, a_end + MLA_Q_RANK:a_end + MLA_Q_RANK + MLA_KV_RANK]
    kr = w[:, b_end - MLA_ROPE:b_end]
    out = jnp.concatenate([
        w[:, :a_end],
        jnp.pad(cq, ((0, 0), (0, CQ_PAD - MLA_Q_RANK))),
        ckv,
        jnp.pad(kr, ((0, 0), (MLA_NOPE, LANES - MLA_NOPE - MLA_ROPE))),
        w[:, b_end:],
    ], axis=-1).astype(BF16)
    assert out.shape[1] == IN_COLS
    return out


def _twice(v):
    return jnp.tile(v, 2).reshape(1, 2 * v.shape[0])


def kernel(x, norm_attn, w_in, q_norm_a, k_norm_a, q_a_norm_b, w_uq_b, kv_a_norm_b, w_ukv_b,
           lambda_q1_c, lambda_k1_c, lambda_q2_c, lambda_k2_c, subln_c, w_out,
           norm_ffn, w_up, conv_w, conv_b, w_down, final_norm):
    B, S, D = x.shape
    depth = w_in.shape[0]
    assert D == D_MODEL and S % max(TM_IN, TM_FFN, TQ, TK) == 0 and S % GRID_W == 0
    assert TM_IN == TK
    T = B * S

    tabs = _rope_tables(S)
    row = lambda v: v.reshape(1, -1)
    a_pairs = [(j, j + N_HEADS_A // 2) for j in range(N_HEADS_A // 2)]
    a_rows = np.concatenate([np.arange(h * HEAD_DIM, (h + 1) * HEAD_DIM)
                             for pair in a_pairs for h in pair])
    xc = x.reshape(T, D)
    for l in range(depth):
        lam_init = 0.8 - 0.6 * math.exp(-0.3 * l)
        wuq_p = jnp.pad(_pad_groups(w_uq_b[l], N_HEADS_B, MLA_NOPE + MLA_ROPE),
                        ((0, CQ_PAD - MLA_Q_RANK), (0, 0))).astype(BF16)
        wukv = w_ukv_b[l].reshape(MLA_KV_RANK, N_HEADS_B, MLA_NOPE + MLA_V)
        v_cols = [_pad_groups(wukv[:, h, MLA_NOPE:], 1, MLA_V, offset=HALF * (h % 2))
                  for h in range(N_HEADS_B)]
        wukv_p = jnp.concatenate(
            [_pad_groups(wukv[:, :, :MLA_NOPE].reshape(MLA_KV_RANK, -1), N_HEADS_B, MLA_NOPE)]
            + v_cols, axis=-1).astype(BF16)

        qa, kta, va, qb, ktb, vb, qc, ktc, vc, nrm = _in_proj(
            xc, row(norm_attn[l]), _prep_w_in(w_in[l]), tabs,
            _twice(q_norm_a[l]), _twice(k_norm_a[l]),
            jnp.pad(q_a_norm_b[l], (0, CQ_PAD - MLA_Q_RANK)).reshape(1, CQ_PAD),
            row(kv_a_norm_b[l]), wuq_p, wukv_p, B, S)

        nrm = jnp.max(nrm.reshape(B, -1, LANES), axis=1)

        def bounded(q0, nq, q_rep, k0, nk, k_rep, per_pipeline):
            q2 = jnp.repeat(nrm[:, q0:q0 + nq], q_rep, axis=1)
            k2 = jnp.repeat(nrm[:, k0:k0 + nk], k_rep, axis=1)
            ok = q2 * k2 * NRM_MARGIN <= SAFE_LOG2_RANGE ** 2
            return jnp.all(ok.reshape(B, -1, per_pipeline), axis=-1).astype(jnp.int32)

        qk_a = (HEAD_DIM * jnp.max(jnp.abs(q_norm_a[l])) * jnp.max(jnp.abs(k_norm_a[l]))
                * (HEAD_DIM ** -0.5 * LOG2E))
        bounded_a = jnp.broadcast_to(
            (qk_a * NRM_MARGIN <= SAFE_LOG2_RANGE).astype(jnp.int32),
            (B, N_HEADS_A // HEADS_PER_PIPELINE_A))

        group = N_HEADS_A // N_KV_A
        oa = _attention(
            functools.partial(_attn_ab_kernel, n_heads=N_HEADS_A,
                              per_pipeline=HEADS_PER_PIPELINE_A, q_index=lambda h: h // 2,
                              kt_index=lambda h: 2 * (h // group) + h % 2,
                              v_index=lambda h: h // group, out_pairs=a_pairs, n_keys=S),
            bounded_a, N_HEADS_A, qa, kta, va, [], N_HEADS_A * HEAD_DIM, B, S)
        same = lambda h: h
        ob = _attention(
            functools.partial(_attn_ab_kernel, n_heads=N_HEADS_B,
                              per_pipeline=HEADS_PER_PIPELINE, q_index=same, kt_index=same,
                              v_index=same,
                              out_pairs=[(2 * j, 2 * j + 1) for j in range(N_HEADS_B // 2)],
                              n_keys=S),
            bounded(NRM_QB, N_HEADS_B, 1, NRM_KB, N_HEADS_B, 1, HEADS_PER_PIPELINE), N_HEADS_B,
            qb, ktb, vb, [], N_HEADS_B * MLA_V, B, S)
        oc = _attention(
            functools.partial(_attn_c_kernel, n_keys=S, lam_init=lam_init),
            bounded(NRM_QC, N_HEADS_C, 2, NRM_KC, 2 * N_HEADS_C, 1, SUB_HEADS_PER_PIPELINE_C),
            2 * N_HEADS_C, qc, ktc, vc,
            [row(lambda_q1_c[l]), row(lambda_k1_c[l]), row(lambda_q2_c[l]), row(lambda_k2_c[l]),
             _twice(subln_c[l])],
            N_HEADS_C * DIFF_V, B, S)

        w_out_l = jnp.concatenate([w_out[l][a_rows], w_out[l][N_HEADS_A * HEAD_DIM:]], axis=0)
        xc = _mlp(xc, oa, ob, oc, w_out_l.astype(BF16), row(norm_ffn[l]), w_up[l].astype(BF16),
                  conv_w[l].reshape(3, 2, D_FF).transpose(1, 0, 2),
                  conv_b[l].reshape(2, 1, D_FF), w_down[l].astype(BF16),
                  row(final_norm), S, final_norm=(l == depth - 1))
    return xc.reshape(B, S, D)
```

```python
import functools
import math

import jax
import jax.numpy as jnp
import numpy as np
from jax import lax
from jax.experimental import pallas as pl
from jax.experimental.pallas import tpu as pltpu

D_MODEL = 1024
GRID_W = 64
HEAD_DIM = 64
N_HEADS_A = 8
N_KV_A = 2
N_HEADS_B = 4
MLA_Q_RANK = 192
MLA_KV_RANK = 128
MLA_NOPE = 64
MLA_ROPE = 32
MLA_V = 64
N_HEADS_C = 4
DIFF_QK = 32
DIFF_V = 64
D_FF = 2816
ROPE_BASE = 10000.0
EPS = 1e-6

LANES = 128
HALF = LANES // 2
BF16_SUBLANES = 16
VMEM_LIMIT_BYTES = 56 * 1024 * 1024

LOG2E = math.log2(math.e)
assert HEAD_DIM == MLA_V == DIFF_V == 2 * DIFF_QK == HALF

OFF_QA = 0
OFF_KA = OFF_QA + N_HEADS_A * HEAD_DIM
OFF_VA = OFF_KA + LANES
OFF_CQ = OFF_VA + LANES
CQ_PAD = 2 * LANES
OFF_CKV = OFF_CQ + CQ_PAD
OFF_KR = OFF_CKV + MLA_KV_RANK
OFF_QC = OFF_KR + LANES
OFF_KC = OFF_QC + N_HEADS_C * HALF
OFF_VC = OFF_KC + N_HEADS_C * HALF
IN_COLS = OFF_VC + N_HEADS_C * HALF

NRM_QB = 0
NRM_KB = NRM_QB + N_HEADS_B
NRM_QC = NRM_KB + N_HEADS_B
NRM_KC = NRM_QC + N_HEADS_C
NRM_GROUPS = 2 * N_HEADS_B + N_HEADS_C
NRM_MARGIN = 1.03
SAFE_LOG2_RANGE = 55.0

F32 = jnp.float32
BF16 = jnp.bfloat16

TM_IN = 512
TM_FFN = 512
FFN_CHUNK = 256
FFN_DOWN_GROUP = 6 * FFN_CHUNK
FFN_DOWN_LAG = 2
TQ = 512
TQ_SUB_AB = 512
TQ_SUB_C = 256
TK = 512
SCORE_LOOKAHEAD = 2
HEADS_PER_PIPELINE = 4


def _dot(a, b):
    return jnp.dot(a, b, preferred_element_type=F32)


def _rms(x, n):
    ms = jnp.sum(x * x, axis=-1, keepdims=True) * (1.0 / n)
    return x * lax.rsqrt(ms + EPS)


def _low_lanes():
    return lax.broadcasted_iota(jnp.int32, (1, LANES), 1) < HALF


def _half_rms(x):
    low = _low_lanes()
    sq = x * x
    s_lo = jnp.sum(jnp.where(low, sq, 0.0), axis=-1, keepdims=True)
    s_hi = jnp.sum(jnp.where(low, 0.0, sq), axis=-1, keepdims=True)
    return x * lax.rsqrt(jnp.where(low, s_lo, s_hi) * (1.0 / HALF) + EPS)


def _in_kernel(x_ref, gat_ref, w_ref, cos_a_ref, sin_a_ref, cos_b_ref, sin_b_ref,
               qg_a_ref, kg_a_ref, qg_b_ref, kvg_b_ref, wuq_ref, wukv_ref, sel_ref,
               qa_ref, kta_ref, va_ref, qb_ref, ktb_ref, vb_ref, qc_ref, ktc_ref, vc_ref, nrm_ref):
    x = x_ref[...]
    tm = x.shape[0]
    h = (_rms(x, D_MODEL) * gat_ref[...]).astype(BF16)

    lane = lax.broadcasted_iota(jnp.int32, (1, LANES), 1)
    low = lane < HALF
    row = lax.broadcasted_iota(jnp.int32, (LANES, 1), 0)

    z = _dot(h, w_ref[...])

    def seg(off, width):
        return z[:, off:off + width]

    def swap(y, nf):
        up = pltpu.roll(y, LANES - nf, axis=1)
        down = pltpu.roll(y, nf, axis=1)
        return jnp.where((lane & nf) == 0, up, down)

    def rope(y, c, s, nf):
        return y * c + swap(y, nf) * s

    def group(z, g):
        return z[:, g * LANES:(g + 1) * LANES]

    squares = []

    def note(y):
        squares.append((y * y).astype(BF16))

    def noted_row_sums(first):
        return _dot(jnp.concatenate(squares[first:], axis=-1),
                    sel_ref[first * LANES:len(squares) * LANES])

    cos_b, sin_b = cos_b_ref[...], sin_b_ref[...]
    nf_b = MLA_ROPE // 4
    cq = (_rms(seg(OFF_CQ, CQ_PAD), MLA_Q_RANK) * qg_b_ref[...]).astype(BF16)
    qb = _dot(cq, wuq_ref[...])
    scale_b = (MLA_NOPE + MLA_ROPE) ** -0.5 * LOG2E
    for hh in range(N_HEADS_B):
        y = rope(group(qb, hh), cos_b, sin_b, nf_b) * scale_b
        qb_ref[0, hh] = y.astype(BF16)
        note(y)
    ckv = (_rms(seg(OFF_CKV, MLA_KV_RANK), MLA_KV_RANK) * kvg_b_ref[...]).astype(BF16)
    kvb = _dot(ckv, wukv_ref[...])
    kr = rope(seg(OFF_KR, LANES), cos_b, sin_b, nf_b)
    for hh in range(N_HEADS_B):
        y = group(kvb, hh) + kr
        ktb_ref[0, hh] = y.T.astype(BF16)
        note(y)
        ones_other_half = jnp.where(low, float(hh % 2), float(1 - hh % 2))
        vb_ref[0, hh] = (group(kvb, N_HEADS_B + hh) + ones_other_half).astype(BF16)
    n2 = noted_row_sums(0)
    n_noted_b = len(squares)

    zq = seg(OFF_QC, N_HEADS_C * HALF)
    zk = seg(OFF_KC, N_HEADS_C * HALF)
    zv = seg(OFF_VC, N_HEADS_C * HALF)
    k_groups = []
    for p in range(N_HEADS_C // 2):
        y = group(zq, p) * (DIFF_QK ** -0.5 * LOG2E)
        qc_ref[0, p] = y.astype(BF16)
        note(y)
        k_groups.append(group(zk, p))
        kt = group(zk, p).T
        for sub in range(4):
            mine = (row >= sub * DIFF_QK) & (row < (sub + 1) * DIFF_QK)
            ktc_ref[0, 4 * p + sub, 0] = jnp.where(mine, kt, 0.0).astype(BF16)
        vz = group(zv, p)
        vc_ref[0, 2 * p] = jnp.where(low, vz, 1.0).astype(BF16)
        vc_ref[0, 2 * p + 1] = jnp.where(low, 1.0, vz).astype(BF16)
    for kz in k_groups:
        note(kz)
    n2 = n2 + noted_row_sums(n_noted_b)
    nrm_ref[0] = jnp.max(n2, axis=0, keepdims=True)

    cos_a, sin_a = cos_a_ref[...], sin_a_ref[...]
    nf_a = HEAD_DIM // 4
    zq = seg(OFF_QA, N_HEADS_A * HEAD_DIM)
    for p in range(N_HEADS_A // 2):
        y = rope(_half_rms(group(zq, p)) * qg_a_ref[...], cos_a, sin_a, nf_a)
        qa_ref[0, p] = (y * (HEAD_DIM ** -0.5 * LOG2E)).astype(BF16)
    y = rope(_half_rms(seg(OFF_KA, LANES)) * kg_a_ref[...], cos_a, sin_a, nf_a)
    kt = y.T.astype(BF16)
    zeros = jnp.zeros((HALF, tm), BF16)
    for g in range(N_KV_A):
        kg = kt[g * HALF:(g + 1) * HALF]
        kta_ref[0, 2 * g] = jnp.concatenate([kg, zeros], axis=0)
        kta_ref[0, 2 * g + 1] = jnp.concatenate([zeros, kg], axis=0)
    zv = seg(OFF_VA, LANES)
    va_ref[0, 0] = jnp.where(low, zv, 1.0).astype(BF16)
    va_ref[0, 1] = jnp.where(low, 1.0, zv).astype(BF16)


def _norm_routing():
    sel = np.zeros((NRM_GROUPS, LANES, LANES), np.float32)
    g = 0
    for slot in ([NRM_QB + h for h in range(N_HEADS_B)] + [NRM_KB + h for h in range(N_HEADS_B)]):
        sel[g, :, slot] = 1.0
        g += 1
    for p in range(N_HEADS_C // 2):
        for f in range(2):
            sel[g, f * HALF:(f + 1) * HALF, NRM_QC + 2 * p + f] = 1.0
        g += 1
    for p in range(N_HEADS_C // 2):
        for sub in range(4):
            sel[g, sub * DIFF_QK:(sub + 1) * DIFF_QK, NRM_KC + 4 * p + sub] = 1.0
        g += 1
    assert g == NRM_GROUPS
    return jnp.asarray(sel.reshape(NRM_GROUPS * LANES, LANES), BF16)


def _in_proj(x2d, gat, w_in_r, tabs, qg_a, kg_a, qg_b, kvg_b, wuq_p, wukv_p, B, S):
    T = B * S
    tm = TM_IN
    nst = S // tm
    const = lambda shape: pl.BlockSpec(shape, lambda i: (0,) * len(shape))
    tab = pl.BlockSpec((tm, LANES), lambda i: (i % nst, 0))
    hm = lambda H: pl.BlockSpec((1, H, tm, LANES), lambda i: (i // nst, 0, i % nst, 0))
    hmt = lambda H: pl.BlockSpec((1, H, LANES, tm), lambda i: (i // nst, 0, 0, i % nst))
    sds = lambda H: jax.ShapeDtypeStruct((B, H, S, LANES), BF16)
    sdt = lambda H: jax.ShapeDtypeStruct((B, H, LANES, S), BF16)
    return pl.pallas_call(
        _in_kernel,
        grid=(T // tm,),
        in_specs=[pl.BlockSpec((tm, D_MODEL), lambda i: (i, 0)),
                  const((1, D_MODEL)), const((D_MODEL, IN_COLS)),
                  tab, tab, tab, tab,
                  const((1, LANES)), const((1, LANES)), const((1, CQ_PAD)), const((1, MLA_KV_RANK)),
                  const((CQ_PAD, N_HEADS_B * LANES)), const((MLA_KV_RANK, 2 * N_HEADS_B * LANES)),
                  const((NRM_GROUPS * LANES, LANES))],
        out_specs=[hm(N_HEADS_A // 2), hmt(2 * N_KV_A), hm(N_KV_A),
                   hm(N_HEADS_B), hmt(N_HEADS_B), hm(N_HEADS_B),
                   hm(N_HEADS_C // 2),
                   pl.BlockSpec((1, 2 * N_HEADS_C, 1, LANES, tm),
                                lambda i: (i // nst, 0, i % nst, 0, 0)),
                   hm(N_HEADS_C),
                   pl.BlockSpec((1, 1, LANES), lambda i: (i, 0, 0))],
        out_shape=[sds(N_HEADS_A // 2), sdt(2 * N_KV_A), sds(N_KV_A),
                   sds(N_HEADS_B), sdt(N_HEADS_B), sds(N_HEADS_B),
                   sds(N_HEADS_C // 2),
                   jax.ShapeDtypeStruct((B, 2 * N_HEADS_C, S // tm, LANES, tm), BF16),
                   sds(N_HEADS_C),
                   jax.ShapeDtypeStruct((T // tm, 1, LANES), F32)],
        compiler_params=pltpu.CompilerParams(dimension_semantics=("arbitrary",),
                                             vmem_limit_bytes=VMEM_LIMIT_BYTES),
        name="in_proj",
    )(x2d, gat, w_in_r, *tabs, qg_a, kg_a, qg_b, kvg_b, wuq_p, wukv_p, _norm_routing())


def _lane_group_max(s):
    m = s[:, :LANES]
    for j in range(1, s.shape[1] // LANES):
        m = jnp.maximum(m, s[:, j * LANES:(j + 1) * LANES])
    return m


def _pv_bounded(streams, n_chunks):
    items = [(r, i) for r in range(n_chunks) for i in range(len(streams))]
    accs = [None] * len(streams)
    in_flight = {}
    for k in range(len(items) + SCORE_LOOKAHEAD):
        if k < len(items):
            r, i = items[k]
            scores, _, row_off = streams[i]
            s, off = scores(r), row_off(r)
            in_flight[k] = s if off is None else s + off
        if k >= SCORE_LOOKAHEAD:
            r, i = items[k - SCORE_LOOKAHEAD]
            d = _dot(jnp.exp2(in_flight.pop(k - SCORE_LOOKAHEAD)).astype(BF16), streams[i][1](r))
            accs[i] = d if accs[i] is None else accs[i] + d
    return accs


def _pv_exact(streams, n_chunks, s_ref):
    rows = s_ref.shape[0] // len(streams)
    maxes = []
    for i, (scores, _, row_off) in enumerate(streams):
        m_run = None
        for r in range(n_chunks):
            s = scores(r)
            s_ref[i * rows:(i + 1) * rows, r * TK:(r + 1) * TK] = s
            m, off = _lane_group_max(s), row_off(r)
            if off is not None:
                m = m + off
            m_run = m if m_run is None else jnp.maximum(m_run, m)
        maxes.append(jnp.max(m_run, axis=-1, keepdims=True))
    accs = []
    for i, (_, values, row_off) in enumerate(streams):
        acc = None
        for r in range(n_chunks):
            off = row_off(r)
            shift = maxes[i] if off is None else maxes[i] - off
            p = jnp.exp2(s_ref[i * rows:(i + 1) * rows, r * TK:(r + 1) * TK] - shift)
            d = _dot(p.astype(BF16), values(r))
            acc = d if acc is None else acc + d
        accs.append(acc)
    return accs


def _run_heads(bounded, heads, head_streams, n_chunks, s_ref, acc_ref, roll_exact=False):
    @pl.when(bounded)
    def _():
        per_head = [head_streams(h) for h in heads]
        n_sub = len(per_head[0])
        accs = _pv_bounded([s for streams in per_head for s in streams], n_chunks)
        for k, h in enumerate(heads):
            acc_ref[h] = jnp.concatenate(accs[k * n_sub:(k + 1) * n_sub], axis=0)

    @pl.when(jnp.logical_not(bounded))
    def _():
        def one_head(h):
            acc_ref[h] = jnp.concatenate(_pv_exact(head_streams(h), n_chunks, s_ref), axis=0)

        if roll_exact:
            lax.fori_loop(0, len(heads), lambda k, c: (one_head(heads[0] + k), c)[1], 0)
        else:
            for h in heads:
                one_head(h)


def _normalised_pair(acc_low, acc_high):
    low = _low_lanes()
    num = jnp.where(low, acc_low, acc_high)
    den = pltpu.roll(jnp.where(low, acc_high, acc_low), HALF, axis=1)
    return num / den


def _attn_ab_kernel(bounded_ref, q_ref, kt_ref, v_ref, o_ref, s_ref, acc_ref, *,
                    n_heads, per_pipeline, q_index, kt_index, v_index, out_pairs, n_keys):
    n_chunks = n_keys // TK

    def head_streams(h):
        q = q_ref[0, q_index(h)]
        return [(lambda r, qs=q[r0:r0 + TQ_SUB_AB]:
                 _dot(qs, kt_ref[0, kt_index(h), :, r * TK:(r + 1) * TK]),
                 lambda r: v_ref[0, v_index(h), r * TK:(r + 1) * TK, :],
                 lambda r: None) for r0 in range(0, TQ, TQ_SUB_AB)]

    def head_group(j, carry):
        heads = [per_pipeline * j + k for k in range(per_pipeline)]
        _run_heads(bounded_ref[pl.program_id(0), j] != 0, heads, head_streams, n_chunks, s_ref,
                   acc_ref, roll_exact=True)
        return carry

    if n_heads == per_pipeline:
        head_group(0, None)
    else:
        lax.fori_loop(0, n_heads // per_pipeline, head_group, 0)
    o_ref[...] = jnp.concatenate([_normalised_pair(acc_ref[a], acc_ref[b]) for a, b in out_pairs],
                                 axis=-1).astype(BF16)


def _attn_c_kernel(bounded_ref, q_ref, kt_ref, v_ref, lq1_ref, lk1_ref, lq2_ref, lk2_ref,
                   subln_ref, o_ref, s_ref, acc_ref, *, n_keys, lam_init):
    assert TQ == TK
    nc = n_keys // TK
    cd = pl.program_id(1)
    col = lax.broadcasted_iota(jnp.int32, (1, TK), 1).astype(F32)
    row = lax.broadcasted_iota(jnp.int32, (TQ_SUB_C, 1), 0).astype(F32)
    subs = [(r, r + TQ_SUB_C) for r in range(0, TQ, TQ_SUB_C)]
    neg_dist = [-jnp.abs((lax.broadcasted_iota(jnp.int32, (TQ_SUB_C, TK), 0) + r0
                          - lax.broadcasted_iota(jnp.int32, (TQ_SUB_C, TK), 1)).astype(F32))
                for r0, _ in subs]

    chunk = [(cd + r) % nc for r in range(nc)]

    def head_bias(hh):
        slope = jnp.float32(LOG2E * 2.0 ** (-8.0 * N_HEADS_C / N_HEADS_C))
        for k in range(N_HEADS_C - 2, -1, -1):
            slope = jnp.where(hh == k, LOG2E * 2.0 ** (-8.0 * (k + 1) / N_HEADS_C), slope)
        key_term, row_off = [None], [[None] * len(subs)]
        for r in range(1, nc):
            side = jnp.where(chunk[r] < cd, slope, -slope)
            key_term.append(col * side)
            base = -slope * (jnp.abs(chunk[r] - cd) * TK).astype(F32)
            row_off.append([base - side * (row + float(r0)) for r0, _ in subs])
        return slope, key_term, row_off

    def head_pair(p, carry):
        bias = [head_bias(2 * p + f) for f in range(2)]

        def streams_of(k):
            f, idx, hh = k // 2, 4 * p + k, 2 * p + k // 2
            slope, key_term, row_off = bias[f]

            def scores(r, si):
                s = _dot(q_ref[0, p][subs[si][0]:subs[si][1]], kt_ref[0, idx, chunk[r]])
                return s + (neg_dist[si] * slope if r == 0 else key_term[r])

            return [(functools.partial(scores, si=si),
                     lambda r: v_ref[0, hh, pl.ds(pl.multiple_of(chunk[r] * TK, TK), TK), :],
                     lambda r, si=si: row_off[r][si]) for si in range(len(subs))]

        _run_heads(bounded_ref[pl.program_id(0), p] != 0, list(range(4)), streams_of, nc, s_ref,
                   acc_ref.at[pl.ds(4 * p, 4)])
        return carry

    assert HEADS_PER_PIPELINE == 4
    lax.fori_loop(0, N_HEADS_C // 2, head_pair, 0)

    lam = (jnp.exp(jnp.sum(lq1_ref[...] * lk1_ref[...], axis=-1, keepdims=True))
           - jnp.exp(jnp.sum(lq2_ref[...] * lk2_ref[...], axis=-1, keepdims=True)) + lam_init)
    outs = []
    for p in range(N_HEADS_C // 2):
        o = (_normalised_pair(acc_ref[4 * p], acc_ref[4 * p + 2])
             - lam * _normalised_pair(acc_ref[4 * p + 1], acc_ref[4 * p + 3]))
        outs.append(_half_rms(o) * subln_ref[...] * (1.0 - lam_init))
    o_ref[...] = jnp.concatenate(outs, axis=-1).astype(BF16)


def _attention(kernel_fn, bounded, n_acc, q, kt, v, extras, out_width, B, S):
    nq = S // TQ
    hq, hv = q.shape[1], v.shape[1]
    extra_specs = [pl.BlockSpec(e.shape, lambda b, i, f, n=e.ndim: (0,) * n) for e in extras]
    return pl.pallas_call(
        kernel_fn,
        grid_spec=pltpu.PrefetchScalarGridSpec(
            num_scalar_prefetch=1,
            grid=(B, nq),
            in_specs=[pl.BlockSpec((1, hq, TQ, LANES), lambda b, i, f: (b, 0, i, 0)),
                      pl.BlockSpec((1,) + kt.shape[1:],
                                   lambda b, i, f, n=kt.ndim: (b,) + (0,) * (n - 1)),
                      pl.BlockSpec((1, hv, S, LANES), lambda b, i, f: (b, 0, 0, 0))] + extra_specs,
            out_specs=pl.BlockSpec((TQ, out_width), lambda b, i, f: (b * nq + i, 0)),
            scratch_shapes=[pltpu.VMEM((TQ, S), F32),
                            pltpu.VMEM((n_acc, TQ, LANES), F32)]),
        out_shape=jax.ShapeDtypeStruct((B * S, out_width), BF16),
        compiler_params=pltpu.CompilerParams(dimension_semantics=("arbitrary", "arbitrary"),
                                             vmem_limit_bytes=VMEM_LIMIT_BYTES),
        name=kernel_fn.func.__name__.strip("_"),
    )(bounded, q, kt, v, *extras)


def _mlp_kernel(xp_ref, x_ref, xn_ref, ap_ref, a_ref, an_ref, bp_ref, b_ref, bn_ref,
                cp_ref, c_ref, cn_ref, wo_ref, g_ref, wu_ref, cw_ref, cb_ref, wd_ref, fg_ref,
                o_ref, act_ref, *, tiles_per_seq, final_norm):
    i = pl.program_id(0)
    tm = x_ref.shape[0]
    halo = xp_ref.shape[0]

    def rows(p, m, n):
        return jnp.concatenate([p[...], m[...], n[...]], axis=0)

    o_ext = jnp.concatenate([rows(ap_ref, a_ref, an_ref), rows(bp_ref, b_ref, bn_ref),
                             rows(cp_ref, c_ref, cn_ref)], axis=1)
    x1 = rows(xp_ref, x_ref, xn_ref) + _dot(o_ext, wo_ref[...])
    h = (_rms(x1, D_MODEL) * g_ref[...]).astype(BF16)
    at_start = (i % tiles_per_seq) == 0
    at_end = (i % tiles_per_seq) == tiles_per_seq - 1
    zeros = jnp.zeros((halo, D_MODEL), BF16)
    hext = jnp.concatenate([jnp.where(at_start, zeros, h[:halo]), h[halo:halo + tm],
                            jnp.where(at_end, zeros, h[halo + tm:])], axis=0)

    def branch(k, c0):
        u = _dot(hext, wu_ref[:, k * D_FF + c0:k * D_FF + c0 + FFN_CHUNK])
        cw = cw_ref[k, :, c0:c0 + FFN_CHUNK]
        n = u.shape[0]
        prev = pltpu.roll(u, 1, axis=0)[halo:halo + tm]
        nxt = pltpu.roll(u, n - 1, axis=0)[halo:halo + tm]
        return (prev * cw[0:1] + u[halo:halo + tm] * cw[1:2] + nxt * cw[2:3]
                + cb_ref[k, :, c0:c0 + FFN_CHUNK])

    y = x1[halo:halo + tm]
    done = 0
    for c0 in range(0, D_FF, FFN_CHUNK):
        g = branch(0, c0)
        val = branch(1, c0)
        act_ref[:, c0:c0 + FFN_CHUNK] = (g / (1.0 + jnp.exp(-g)) * val).astype(BF16)
        ready = c0 + FFN_CHUNK - FFN_DOWN_LAG * FFN_CHUNK
        if ready - done >= FFN_DOWN_GROUP:
            y = y + _dot(act_ref[:, done:done + FFN_DOWN_GROUP], wd_ref[done:done + FFN_DOWN_GROUP])
            done += FFN_DOWN_GROUP
    y = y + _dot(act_ref[:, done:], wd_ref[done:])
    if final_norm:
        y = _rms(y, D_MODEL) * fg_ref[...]
    o_ref[...] = y


def _mlp(x2d, oa, ob, oc, w_out, g_ffn, w_up, conv_w, conv_b, w_down, final_g, S, final_norm):
    T = x2d.shape[0]
    tm, halo = TM_FFN, BF16_SUBLANES
    per = tm // halo
    last = T // halo - 1
    resident = lambda shape: pl.BlockSpec(shape, lambda i: (0,) * len(shape),
                                          pipeline_mode=pl.Buffered(1))

    def with_halos(width):
        return [pl.BlockSpec((halo, width), lambda i: (jnp.maximum(i * per - 1, 0), 0)),
                pl.BlockSpec((tm, width), lambda i: (i, 0)),
                pl.BlockSpec((halo, width), lambda i: (jnp.minimum((i + 1) * per, last), 0))]

    return pl.pallas_call(
        functools.partial(_mlp_kernel, tiles_per_seq=S // tm, final_norm=final_norm),
        grid=(T // tm,),
        in_specs=(with_halos(D_MODEL) + with_halos(oa.shape[1]) + with_halos(ob.shape[1])
                  + with_halos(oc.shape[1])
                  + [resident(w_out.shape), resident((1, D_MODEL)),
                     resident((D_MODEL, 2 * D_FF)), resident((2, 3, D_FF)), resident((2, 1, D_FF)),
                     resident((D_FF, D_MODEL)), resident((1, D_MODEL))]),
        out_specs=pl.BlockSpec((tm, D_MODEL), lambda i: (i, 0)),
        out_shape=jax.ShapeDtypeStruct((T, D_MODEL), F32),
        scratch_shapes=[pltpu.VMEM((tm, D_FF), BF16)],
        compiler_params=pltpu.CompilerParams(dimension_semantics=("arbitrary",),
                                             vmem_limit_bytes=VMEM_LIMIT_BYTES),
        name="out_proj_conv_mlp",
    )(x2d, x2d, x2d, oa, oa, oa, ob, ob, ob, oc, oc, oc, w_out, g_ffn,
      w_up, conv_w, conv_b, w_down, final_g)


def _pad_groups(w, n_groups, width, offset=0):
    r = w.shape[0]
    w = w.reshape(r, n_groups, width)
    w = jnp.pad(w, ((0, 0), (0, 0), (offset, LANES - width - offset)))
    return w.reshape(r, n_groups * LANES)


def _rope_core(S, dim):
    nf = dim // 4
    t = jnp.arange(S, dtype=jnp.int32)
    rows = (t // GRID_W).astype(F32)
    cols = (t % GRID_W).astype(F32)
    inv = ROPE_BASE ** (-jnp.arange(nf, dtype=F32) / nf)
    ar = rows[:, None] * inv
    ac = cols[:, None] * inv
    cos = jnp.concatenate([jnp.cos(ar), jnp.cos(ar), jnp.cos(ac), jnp.cos(ac)], axis=-1)
    sin = jnp.concatenate([-jnp.sin(ar), jnp.sin(ar), -jnp.sin(ac), jnp.sin(ac)], axis=-1)
    return cos, sin


def _rope_tables(S):
    cos_a, sin_a = _rope_core(S, HEAD_DIM)
    cos_b, sin_b = _rope_core(S, MLA_ROPE)
    pad = LANES - MLA_NOPE - MLA_ROPE
    return (jnp.tile(cos_a, (1, 2)), jnp.tile(sin_a, (1, 2)),
            jnp.concatenate([jnp.ones((S, MLA_NOPE), F32), cos_b, jnp.zeros((S, pad), F32)], -1),
            jnp.concatenate([jnp.zeros((S, MLA_NOPE), F32), sin_b, jnp.zeros((S, pad), F32)], -1))


def _prep_w_in(w):
    a_end = (N_HEADS_A + 2 * N_KV_A) * HEAD_DIM
    b_end = a_end + MLA_Q_RANK + MLA_KV_RANK + MLA_ROPE
    cq = w[:, a_end:a_end + MLA_Q_RANK]
    ckv = w[:, a_end + MLA_Q_RANK:a_end + MLA_Q_RANK + MLA_KV_RANK]
    kr = w[:, b_end - MLA_ROPE:b_end]
    out = jnp.concatenate([
        w[:, :a_end],
        jnp.pad(cq, ((0, 0), (0, CQ_PAD - MLA_Q_RANK))),
        ckv,
        jnp.pad(kr, ((0, 0), (MLA_NOPE, LANES - MLA_NOPE - MLA_ROPE))),
        w[:, b_end:],
    ], axis=-1).astype(BF16)
    assert out.shape[1] == IN_COLS
    return out


def _twice(v):
    return jnp.tile(v, 2).reshape(1, 2 * v.shape[0])


def kernel(x, norm_attn, w_in, q_norm_a, k_norm_a, q_a_norm_b, w_uq_b, kv_a_norm_b, w_ukv_b,
           lambda_q1_c, lambda_k1_c, lambda_q2_c, lambda_k2_c, subln_c, w_out,
           norm_ffn, w_up, conv_w, conv_b, w_down, final_norm):
    B, S, D = x.shape
    depth = w_in.shape[0]
    assert D == D_MODEL and S % max(TM_IN, TM_FFN, TQ, TK) == 0 and S % GRID_W == 0
    assert TM_IN == TK
    T = B * S

    tabs = _rope_tables(S)
    row = lambda v: v.reshape(1, -1)
    a_pairs = [(j, j + N_HEADS_A // 2) for j in range(N_HEADS_A // 2)]
    a_rows = np.concatenate([np.arange(h * HEAD_DIM, (h + 1) * HEAD_DIM)
                             for pair in a_pairs for h in pair])
    xc = x.reshape(T, D)
    for l in range(depth):
        lam_init = 0.8 - 0.6 * math.exp(-0.3 * l)
        wuq_p = jnp.pad(_pad_groups(w_uq_b[l], N_HEADS_B, MLA_NOPE + MLA_ROPE),
                        ((0, CQ_PAD - MLA_Q_RANK), (0, 0))).astype(BF16)
        wukv = w_ukv_b[l].reshape(MLA_KV_RANK, N_HEADS_B, MLA_NOPE + MLA_V)
        v_cols = [_pad_groups(wukv[:, h, MLA_NOPE:], 1, MLA_V, offset=HALF * (h % 2))
                  for h in range(N_HEADS_B)]
        wukv_p = jnp.concatenate(
            [_pad_groups(wukv[:, :, :MLA_NOPE].reshape(MLA_KV_RANK, -1), N_HEADS_B, MLA_NOPE)]
            + v_cols, axis=-1).astype(BF16)

        qa, kta, va, qb, ktb, vb, qc, ktc, vc, nrm = _in_proj(
            xc, row(norm_attn[l]), _prep_w_in(w_in[l]), tabs,
            _twice(q_norm_a[l]), _twice(k_norm_a[l]),
            jnp.pad(q_a_norm_b[l], (0, CQ_PAD - MLA_Q_RANK)).reshape(1, CQ_PAD),
            row(kv_a_norm_b[l]), wuq_p, wukv_p, B, S)

        nrm = jnp.max(nrm.reshape(B, -1, LANES), axis=1)

        def bounded(q0, nq, q_rep, k0, nk, k_rep, per_pipeline):
            q2 = jnp.repeat(nrm[:, q0:q0 + nq], q_rep, axis=1)
            k2 = jnp.repeat(nrm[:, k0:k0 + nk], k_rep, axis=1)
            ok = q2 * k2 * NRM_MARGIN <= SAFE_LOG2_RANGE ** 2
            return jnp.all(ok.reshape(B, -1, per_pipeline), axis=-1).astype(jnp.int32)

        qk_a = (HEAD_DIM * jnp.max(jnp.abs(q_norm_a[l])) * jnp.max(jnp.abs(k_norm_a[l]))
                * (HEAD_DIM ** -0.5 * LOG2E))
        bounded_a = jnp.broadcast_to(
            (qk_a * NRM_MARGIN <= SAFE_LOG2_RANGE).astype(jnp.int32),
            (B, N_HEADS_A // HEADS_PER_PIPELINE))

        group = N_HEADS_A // N_KV_A
        oa = _attention(
            functools.partial(_attn_ab_kernel, n_heads=N_HEADS_A,
                              per_pipeline=HEADS_PER_PIPELINE, q_index=lambda h: h // 2,
                              kt_index=lambda h: 2 * (h // group) + h % 2,
                              v_index=lambda h: h // group, out_pairs=a_pairs, n_keys=S),
            bounded_a, N_HEADS_A, qa, kta, va, [], N_HEADS_A * HEAD_DIM, B, S)
        same = lambda h: h
        ob = _attention(
            functools.partial(_attn_ab_kernel, n_heads=N_HEADS_B,
                              per_pipeline=HEADS_PER_PIPELINE, q_index=same, kt_index=same,
                              v_index=same,
                              out_pairs=[(2 * j, 2 * j + 1) for j in range(N_HEADS_B // 2)],
                              n_keys=S),
            bounded(NRM_QB, N_HEADS_B, 1, NRM_KB, N_HEADS_B, 1, HEADS_PER_PIPELINE), N_HEADS_B,
            qb, ktb, vb, [], N_HEADS_B * MLA_V, B, S)
        oc = _attention(
            functools.partial(_attn_c_kernel, n_keys=S, lam_init=lam_init),
            bounded(NRM_QC, N_HEADS_C, 2, NRM_KC, 2 * N_HEADS_C, 1, HEADS_PER_PIPELINE),
            2 * N_HEADS_C, qc, ktc, vc,
            [row(lambda_q1_c[l]), row(lambda_k1_c[l]), row(lambda_q2_c[l]), row(lambda_k2_c[l]),
             _twice(subln_c[l])],
            N_HEADS_C * DIFF_V, B, S)

        w_out_l = jnp.concatenate([w_out[l][a_rows], w_out[l][N_HEADS_A * HEAD_DIM:]], axis=0)
        xc = _mlp(xc, oa, ob, oc, w_out_l.astype(BF16), row(norm_ffn[l]), w_up[l].astype(BF16),
                  conv_w[l].reshape(3, 2, D_FF).transpose(1, 0, 2),
                  conv_b[l].reshape(2, 1, D_FF), w_down[l].astype(BF16),
                  row(final_norm), S, final_norm=(l == depth - 1))
    return xc.reshape(B, S, D)
```

```python
import functools
import math

import jax
import jax.numpy as jnp
import numpy as np
from jax import lax
from jax.experimental import pallas as pl
from jax.experimental.pallas import tpu as pltpu

D_MODEL = 1024
GRID_W = 64
HEAD_DIM = 64
N_HEADS_A = 8
N_KV_A = 2
N_HEADS_B = 4
MLA_Q_RANK = 192
MLA_KV_RANK = 128
MLA_NOPE = 64
MLA_ROPE = 32
MLA_V = 64
N_HEADS_C = 4
DIFF_QK = 32
DIFF_V = 64
D_FF = 2816
ROPE_BASE = 10000.0
EPS = 1e-6

LANES = 128
HALF = LANES // 2
BF16_SUBLANES = 16
VMEM_LIMIT_BYTES = 56 * 1024 * 1024

LOG2E = math.log2(math.e)
assert HEAD_DIM == MLA_V == DIFF_V == 2 * DIFF_QK == HALF

OFF_QA = 0
OFF_KA = OFF_QA + N_HEADS_A * HEAD_DIM
OFF_VA = OFF_KA + LANES
OFF_CQ = OFF_VA + LANES
CQ_PAD = 2 * LANES
OFF_CKV = OFF_CQ + CQ_PAD
OFF_KR = OFF_CKV + MLA_KV_RANK
OFF_QC = OFF_KR + LANES
OFF_KC = OFF_QC + N_HEADS_C * HALF
OFF_VC = OFF_KC + N_HEADS_C * HALF
IN_COLS = OFF_VC + N_HEADS_C * HALF

NRM_QB = 0
NRM_KB = NRM_QB + N_HEADS_B
NRM_QC = NRM_KB + N_HEADS_B
NRM_KC = NRM_QC + N_HEADS_C
NRM_GROUPS = 2 * N_HEADS_B + N_HEADS_C
NRM_MARGIN = 1.03
SAFE_LOG2_RANGE = 55.0

F32 = jnp.float32
BF16 = jnp.bfloat16

TM_IN = 512
TM_FFN = 512
FFN_CHUNK = 256
FFN_DOWN_GROUP = 6 * FFN_CHUNK
FFN_DOWN_LAG = 2
TQ = 512
TQ_SUB_AB = 512
TQ_SUB_C = 256
TK = 512
SCORE_LOOKAHEAD = 2
HEADS_PER_PIPELINE = 4


def _dot(a, b):
    return jnp.dot(a, b, preferred_element_type=F32)


def _rms(x, n):
    ms = jnp.sum(x * x, axis=-1, keepdims=True) * (1.0 / n)
    return x * lax.rsqrt(ms + EPS)


def _low_lanes():
    return lax.broadcasted_iota(jnp.int32, (1, LANES), 1) < HALF


def _half_rms(x):
    low = _low_lanes()
    sq = x * x
    s_lo = jnp.sum(jnp.where(low, sq, 0.0), axis=-1, keepdims=True)
    s_hi = jnp.sum(jnp.where(low, 0.0, sq), axis=-1, keepdims=True)
    return x * lax.rsqrt(jnp.where(low, s_lo, s_hi) * (1.0 / HALF) + EPS)


def _in_kernel(x_ref, gat_ref, w_ref, cos_a_ref, sin_a_ref, cos_b_ref, sin_b_ref,
               qg_a_ref, kg_a_ref, qg_b_ref, kvg_b_ref, wuq_ref, wukv_ref, sel_ref,
               qa_ref, kta_ref, va_ref, qb_ref, ktb_ref, vb_ref, qc_ref, ktc_ref, vc_ref, nrm_ref):
    x = x_ref[...]
    tm = x.shape[0]
    h = (_rms(x, D_MODEL) * gat_ref[...]).astype(BF16)

    lane = lax.broadcasted_iota(jnp.int32, (1, LANES), 1)
    low = lane < HALF
    row = lax.broadcasted_iota(jnp.int32, (LANES, 1), 0)

    z = _dot(h, w_ref[...])

    def seg(off, width):
        return z[:, off:off + width]

    def swap(y, nf):
        up = pltpu.roll(y, LANES - nf, axis=1)
        down = pltpu.roll(y, nf, axis=1)
        return jnp.where((lane & nf) == 0, up, down)

    def rope(y, c, s, nf):
        return y * c + swap(y, nf) * s

    def group(z, g):
        return z[:, g * LANES:(g + 1) * LANES]

    squares = []

    def note(y):
        squares.append((y * y).astype(BF16))

    def noted_row_sums(first):
        return _dot(jnp.concatenate(squares[first:], axis=-1),
                    sel_ref[first * LANES:len(squares) * LANES])

    cos_b, sin_b = cos_b_ref[...], sin_b_ref[...]
    nf_b = MLA_ROPE // 4
    cq = (_rms(seg(OFF_CQ, CQ_PAD), MLA_Q_RANK) * qg_b_ref[...]).astype(BF16)
    qb = _dot(cq, wuq_ref[...])
    scale_b = (MLA_NOPE + MLA_ROPE) ** -0.5 * LOG2E
    for hh in range(N_HEADS_B):
        y = rope(group(qb, hh), cos_b, sin_b, nf_b) * scale_b
        qb_ref[0, hh] = y.astype(BF16)
        note(y)
    ckv = (_rms(seg(OFF_CKV, MLA_KV_RANK), MLA_KV_RANK) * kvg_b_ref[...]).astype(BF16)
    kvb = _dot(ckv, wukv_ref[...])
    kr = rope(seg(OFF_KR, LANES), cos_b, sin_b, nf_b)
    for hh in range(N_HEADS_B):
        y = group(kvb, hh) + kr
        ktb_ref[0, hh] = y.T.astype(BF16)
        note(y)
        ones_other_half = jnp.where(low, float(hh % 2), float(1 - hh % 2))
        vb_ref[0, hh] = (group(kvb, N_HEADS_B + hh) + ones_other_half).astype(BF16)
    n2 = noted_row_sums(0)
    n_noted_b = len(squares)

    zq = seg(OFF_QC, N_HEADS_C * HALF)
    zk = seg(OFF_KC, N_HEADS_C * HALF)
    zv = seg(OFF_VC, N_HEADS_C * HALF)
    k_groups = []
    for p in range(N_HEADS_C // 2):
        y = group(zq, p) * (DIFF_QK ** -0.5 * LOG2E)
        qc_ref[0, p] = y.astype(BF16)
        note(y)
        k_groups.append(group(zk, p))
        kt = group(zk, p).T
        for sub in range(4):
            mine = (row >= sub * DIFF_QK) & (row < (sub + 1) * DIFF_QK)
            ktc_ref[0, 4 * p + sub, 0] = jnp.where(mine, kt, 0.0).astype(BF16)
        vz = group(zv, p)
        vc_ref[0, 2 * p] = jnp.where(low, vz, 1.0).astype(BF16)
        vc_ref[0, 2 * p + 1] = jnp.where(low, 1.0, vz).astype(BF16)
    for kz in k_groups:
        note(kz)
    n2 = n2 + noted_row_sums(n_noted_b)
    nrm_ref[0] = jnp.max(n2, axis=0, keepdims=True)

    cos_a, sin_a = cos_a_ref[...], sin_a_ref[...]
    nf_a = HEAD_DIM // 4
    zq = seg(OFF_QA, N_HEADS_A * HEAD_DIM)
    for p in range(N_HEADS_A // 2):
        y = rope(_half_rms(group(zq, p)) * qg_a_ref[...], cos_a, sin_a, nf_a)
        qa_ref[0, p] = (y * (HEAD_DIM ** -0.5 * LOG2E)).astype(BF16)
    y = rope(_half_rms(seg(OFF_KA, LANES)) * kg_a_ref[...], cos_a, sin_a, nf_a)
    kt = y.T.astype(BF16)
    zeros = jnp.zeros((HALF, tm), BF16)
    for g in range(N_KV_A):
        kg = kt[g * HALF:(g + 1) * HALF]
        kta_ref[0, 2 * g] = jnp.concatenate([kg, zeros], axis=0)
        kta_ref[0, 2 * g + 1] = jnp.concatenate([zeros, kg], axis=0)
    zv = seg(OFF_VA, LANES)
    va_ref[0, 0] = jnp.where(low, zv, 1.0).astype(BF16)
    va_ref[0, 1] = jnp.where(low, 1.0, zv).astype(BF16)


def _norm_routing():
    sel = np.zeros((NRM_GROUPS, LANES, LANES), np.float32)
    g = 0
    for slot in ([NRM_QB + h for h in range(N_HEADS_B)] + [NRM_KB + h for h in range(N_HEADS_B)]):
        sel[g, :, slot] = 1.0
        g += 1
    for p in range(N_HEADS_C // 2):
        for f in range(2):
            sel[g, f * HALF:(f + 1) * HALF, NRM_QC + 2 * p + f] = 1.0
        g += 1
    for p in range(N_HEADS_C // 2):
        for sub in range(4):
            sel[g, sub * DIFF_QK:(sub + 1) * DIFF_QK, NRM_KC + 4 * p + sub] = 1.0
        g += 1
    assert g == NRM_GROUPS
    return jnp.asarray(sel.reshape(NRM_GROUPS * LANES, LANES), BF16)


def _in_proj(x2d, gat, w_in_r, tabs, qg_a, kg_a, qg_b, kvg_b, wuq_p, wukv_p, B, S):
    T = B * S
    tm = TM_IN
    nst = S // tm
    const = lambda shape: pl.BlockSpec(shape, lambda i: (0,) * len(shape))
    tab = pl.BlockSpec((tm, LANES), lambda i: (i % nst, 0))
    hm = lambda H: pl.BlockSpec((1, H, tm, LANES), lambda i: (i // nst, 0, i % nst, 0))
    hmt = lambda H: pl.BlockSpec((1, H, LANES, tm), lambda i: (i // nst, 0, 0, i % nst))
    sds = lambda H: jax.ShapeDtypeStruct((B, H, S, LANES), BF16)
    sdt = lambda H: jax.ShapeDtypeStruct((B, H, LANES, S), BF16)
    return pl.pallas_call(
        _in_kernel,
        grid=(T // tm,),
        in_specs=[pl.BlockSpec((tm, D_MODEL), lambda i: (i, 0)),
                  const((1, D_MODEL)), const((D_MODEL, IN_COLS)),
                  tab, tab, tab, tab,
                  const((1, LANES)), const((1, LANES)), const((1, CQ_PAD)), const((1, MLA_KV_RANK)),
                  const((CQ_PAD, N_HEADS_B * LANES)), const((MLA_KV_RANK, 2 * N_HEADS_B * LANES)),
                  const((NRM_GROUPS * LANES, LANES))],
        out_specs=[hm(N_HEADS_A // 2), hmt(2 * N_KV_A), hm(N_KV_A),
                   hm(N_HEADS_B), hmt(N_HEADS_B), hm(N_HEADS_B),
                   hm(N_HEADS_C // 2),
                   pl.BlockSpec((1, 2 * N_HEADS_C, 1, LANES, tm),
                                lambda i: (i // nst, 0, i % nst, 0, 0)),
                   hm(N_HEADS_C),
                   pl.BlockSpec((1, 1, LANES), lambda i: (i, 0, 0))],
        out_shape=[sds(N_HEADS_A // 2), sdt(2 * N_KV_A), sds(N_KV_A),
                   sds(N_HEADS_B), sdt(N_HEADS_B), sds(N_HEADS_B),
                   sds(N_HEADS_C // 2),
                   jax.ShapeDtypeStruct((B, 2 * N_HEADS_C, S // tm, LANES, tm), BF16),
                   sds(N_HEADS_C),
                   jax.ShapeDtypeStruct((T // tm, 1, LANES), F32)],
        compiler_params=pltpu.CompilerParams(dimension_semantics=("arbitrary",),
                                             vmem_limit_bytes=VMEM_LIMIT_BYTES),
        name="in_proj",
    )(x2d, gat, w_in_r, *tabs, qg_a, kg_a, qg_b, kvg_b, wuq_p, wukv_p, _norm_routing())


def _lane_group_max(s):
    m = s[:, :LANES]
    for j in range(1, s.shape[1] // LANES):
        m = jnp.maximum(m, s[:, j * LANES:(j + 1) * LANES])
    return m


def _pv_bounded(streams, n_chunks):
    items = [(r, i) for r in range(n_chunks) for i in range(len(streams))]
    accs = [None] * len(streams)
    in_flight = {}
    for k in range(len(items) + SCORE_LOOKAHEAD):
        if k < len(items):
            r, i = items[k]
            scores, _, row_off = streams[i]
            s, off = scores(r), row_off(r)
            in_flight[k] = s if off is None else s + off
        if k >= SCORE_LOOKAHEAD:
            r, i = items[k - SCORE_LOOKAHEAD]
            d = _dot(jnp.exp2(in_flight.pop(k - SCORE_LOOKAHEAD)).astype(BF16), streams[i][1](r))
            accs[i] = d if accs[i] is None else accs[i] + d
    return accs


def _pv_exact(streams, n_chunks, s_ref):
    rows = s_ref.shape[0] // len(streams)
    maxes = []
    for i, (scores, _, row_off) in enumerate(streams):
        m_run = None
        for r in range(n_chunks):
            s = scores(r)
            s_ref[i * rows:(i + 1) * rows, r * TK:(r + 1) * TK] = s
            m, off = _lane_group_max(s), row_off(r)
            if off is not None:
                m = m + off
            m_run = m if m_run is None else jnp.maximum(m_run, m)
        maxes.append(jnp.max(m_run, axis=-1, keepdims=True))
    accs = []
    for i, (_, values, row_off) in enumerate(streams):
        acc = None
        for r in range(n_chunks):
            off = row_off(r)
            shift = maxes[i] if off is None else maxes[i] - off
            p = jnp.exp2(s_ref[i * rows:(i + 1) * rows, r * TK:(r + 1) * TK] - shift)
            d = _dot(p.astype(BF16), values(r))
            acc = d if acc is None else acc + d
        accs.append(acc)
    return accs


def _run_heads(bounded, heads, head_streams, n_chunks, s_ref, acc_ref, roll_exact=False):
    @pl.when(bounded)
    def _():
        per_head = [head_streams(h) for h in heads]
        n_sub = len(per_head[0])
        accs = _pv_bounded([s for streams in per_head for s in streams], n_chunks)
        for k, h in enumerate(heads):
            acc_ref[h] = jnp.concatenate(accs[k * n_sub:(k + 1) * n_sub], axis=0)

    @pl.when(jnp.logical_not(bounded))
    def _():
        def one_head(h):
            acc_ref[h] = jnp.concatenate(_pv_exact(head_streams(h), n_chunks, s_ref), axis=0)

        if roll_exact:
            lax.fori_loop(0, len(heads), lambda k, c: (one_head(heads[0] + k), c)[1], 0)
        else:
            for h in heads:
                one_head(h)


def _normalised_pair(acc_low, acc_high):
    low = _low_lanes()
    num = jnp.where(low, acc_low, acc_high)
    den = pltpu.roll(jnp.where(low, acc_high, acc_low), HALF, axis=1)
    return num / den


def _attn_ab_kernel(bounded_ref, q_ref, kt_ref, v_ref, o_ref, s_ref, acc_ref, *,
                    n_heads, per_pipeline, q_index, kt_index, v_index, out_pairs, n_keys):
    n_chunks = n_keys // TK

    def head_streams(h):
        q = q_ref[0, q_index(h)]
        return [(lambda r, qs=q[r0:r0 + TQ_SUB_AB]:
                 _dot(qs, kt_ref[0, kt_index(h), :, r * TK:(r + 1) * TK]),
                 lambda r: v_ref[0, v_index(h), r * TK:(r + 1) * TK, :],
                 lambda r: None) for r0 in range(0, TQ, TQ_SUB_AB)]

    def head_group(j, carry):
        heads = [per_pipeline * j + k for k in range(per_pipeline)]
        _run_heads(bounded_ref[pl.program_id(0), j] != 0, heads, head_streams, n_chunks, s_ref,
                   acc_ref, roll_exact=True)
        return carry

    if n_heads == per_pipeline:
        head_group(0, None)
    else:
        lax.fori_loop(0, n_heads // per_pipeline, head_group, 0)
    o_ref[...] = jnp.concatenate([_normalised_pair(acc_ref[a], acc_ref[b]) for a, b in out_pairs],
                                 axis=-1).astype(BF16)


def _attn_c_kernel(bounded_ref, q_ref, kt_ref, v_ref, lq1_ref, lk1_ref, lq2_ref, lk2_ref,
                   subln_ref, o_ref, s_ref, acc_ref, *, n_keys, lam_init):
    assert TQ == TK
    nc = n_keys // TK
    cd = pl.program_id(1)
    col = lax.broadcasted_iota(jnp.int32, (1, TK), 1).astype(F32)
    row = lax.broadcasted_iota(jnp.int32, (TQ_SUB_C, 1), 0).astype(F32)
    subs = [(r, r + TQ_SUB_C) for r in range(0, TQ, TQ_SUB_C)]
    neg_dist = [-jnp.abs((lax.broadcasted_iota(jnp.int32, (TQ_SUB_C, TK), 0) + r0
                          - lax.broadcasted_iota(jnp.int32, (TQ_SUB_C, TK), 1)).astype(F32))
                for r0, _ in subs]

    chunk = [(cd + r) % nc for r in range(nc)]

    def head_bias(hh):
        slope = jnp.float32(LOG2E * 2.0 ** (-8.0 * N_HEADS_C / N_HEADS_C))
        for k in range(N_HEADS_C - 2, -1, -1):
            slope = jnp.where(hh == k, LOG2E * 2.0 ** (-8.0 * (k + 1) / N_HEADS_C), slope)
        key_term, row_off = [None], [[None] * len(subs)]
        for r in range(1, nc):
            side = jnp.where(chunk[r] < cd, slope, -slope)
            key_term.append(col * side)
            base = -slope * (jnp.abs(chunk[r] - cd) * TK).astype(F32)
            row_off.append([base - side * (row + float(r0)) for r0, _ in subs])
        return slope, key_term, row_off

    def head_pair(p, carry):
        bias = {}

        def streams_of(k):
            idx, hh = 4 * p + k, 2 * p + k // 2
            if not isinstance(k, int):
                slope, key_term, row_off = head_bias(hh)
            else:
                if k // 2 not in bias:
                    bias[k // 2] = head_bias(hh)
                slope, key_term, row_off = bias[k // 2]

            def scores(r, si):
                s = _dot(q_ref[0, p][subs[si][0]:subs[si][1]], kt_ref[0, idx, chunk[r]])
                return s + (neg_dist[si] * slope if r == 0 else key_term[r])

            return [(functools.partial(scores, si=si),
                     lambda r: v_ref[0, hh, pl.ds(pl.multiple_of(chunk[r] * TK, TK), TK), :],
                     lambda r, si=si: row_off[r][si]) for si in range(len(subs))]

        _run_heads(bounded_ref[pl.program_id(0), p] != 0, list(range(4)), streams_of, nc, s_ref,
                   acc_ref.at[pl.ds(4 * p, 4)], roll_exact=True)
        return carry

    assert HEADS_PER_PIPELINE == 4
    lax.fori_loop(0, N_HEADS_C // 2, head_pair, 0)

    lam = (jnp.exp(jnp.sum(lq1_ref[...] * lk1_ref[...], axis=-1, keepdims=True))
           - jnp.exp(jnp.sum(lq2_ref[...] * lk2_ref[...], axis=-1, keepdims=True)) + lam_init)
    outs = []
    for p in range(N_HEADS_C // 2):
        o = (_normalised_pair(acc_ref[4 * p], acc_ref[4 * p + 2])
             - lam * _normalised_pair(acc_ref[4 * p + 1], acc_ref[4 * p + 3]))
        outs.append(_half_rms(o) * subln_ref[...] * (1.0 - lam_init))
    o_ref[...] = jnp.concatenate(outs, axis=-1).astype(BF16)


def _attention(kernel_fn, bounded, n_acc, q, kt, v, extras, out_width, B, S):
    nq = S // TQ
    hq, hv = q.shape[1], v.shape[1]
    extra_specs = [pl.BlockSpec(e.shape, lambda b, i, f, n=e.ndim: (0,) * n) for e in extras]
    return pl.pallas_call(
        kernel_fn,
        grid_spec=pltpu.PrefetchScalarGridSpec(
            num_scalar_prefetch=1,
            grid=(B, nq),
            in_specs=[pl.BlockSpec((1, hq, TQ, LANES), lambda b, i, f: (b, 0, i, 0)),
                      pl.BlockSpec((1,) + kt.shape[1:],
                                   lambda b, i, f, n=kt.ndim: (b,) + (0,) * (n - 1)),
                      pl.BlockSpec((1, hv, S, LANES), lambda b, i, f: (b, 0, 0, 0))] + extra_specs,
            out_specs=pl.BlockSpec((TQ, out_width), lambda b, i, f: (b * nq + i, 0)),
            scratch_shapes=[pltpu.VMEM((TQ, S), F32),
                            pltpu.VMEM((n_acc, TQ, LANES), F32)]),
        out_shape=jax.ShapeDtypeStruct((B * S, out_width), BF16),
        compiler_params=pltpu.CompilerParams(dimension_semantics=("arbitrary", "arbitrary"),
                                             vmem_limit_bytes=VMEM_LIMIT_BYTES),
        name=kernel_fn.func.__name__.strip("_"),
    )(bounded, q, kt, v, *extras)


def _mlp_kernel(xp_ref, x_ref, xn_ref, ap_ref, a_ref, an_ref, bp_ref, b_ref, bn_ref,
                cp_ref, c_ref, cn_ref, wo_ref, g_ref, wu_ref, cw_ref, cb_ref, wd_ref, fg_ref,
                o_ref, act_ref, *, tiles_per_seq, final_norm):
    i = pl.program_id(0)
    tm = x_ref.shape[0]
    halo = xp_ref.shape[0]

    def rows(p, m, n):
        return jnp.concatenate([p[...], m[...], n[...]], axis=0)

    o_ext = jnp.concatenate([rows(ap_ref, a_ref, an_ref), rows(bp_ref, b_ref, bn_ref),
                             rows(cp_ref, c_ref, cn_ref)], axis=1)
    x1 = rows(xp_ref, x_ref, xn_ref) + _dot(o_ext, wo_ref[...])
    h = (_rms(x1, D_MODEL) * g_ref[...]).astype(BF16)
    at_start = (i % tiles_per_seq) == 0
    at_end = (i % tiles_per_seq) == tiles_per_seq - 1
    zeros = jnp.zeros((halo, D_MODEL), BF16)
    hext = jnp.concatenate([jnp.where(at_start, zeros, h[:halo]), h[halo:halo + tm],
                            jnp.where(at_end, zeros, h[halo + tm:])], axis=0)

    def branch(k, c0):
        u = _dot(hext, wu_ref[:, k * D_FF + c0:k * D_FF + c0 + FFN_CHUNK])
        cw = cw_ref[k, :, c0:c0 + FFN_CHUNK]
        n = u.shape[0]
        prev = pltpu.roll(u, 1, axis=0)[halo:halo + tm]
        nxt = pltpu.roll(u, n - 1, axis=0)[halo:halo + tm]
        return (prev * cw[0:1] + u[halo:halo + tm] * cw[1:2] + nxt * cw[2:3]
                + cb_ref[k, :, c0:c0 + FFN_CHUNK])

    y = x1[halo:halo + tm]
    done = 0
    for c0 in range(0, D_FF, FFN_CHUNK):
        g = branch(0, c0)
        val = branch(1, c0)
        act_ref[:, c0:c0 + FFN_CHUNK] = (g / (1.0 + jnp.exp(-g)) * val).astype(BF16)
        ready = c0 + FFN_CHUNK - FFN_DOWN_LAG * FFN_CHUNK
        if ready - done >= FFN_DOWN_GROUP:
            y = y + _dot(act_ref[:, done:done + FFN_DOWN_GROUP], wd_ref[done:done + FFN_DOWN_GROUP])
            done += FFN_DOWN_GROUP
    y = y + _dot(act_ref[:, done:], wd_ref[done:])
    if final_norm:
        y = _rms(y, D_MODEL) * fg_ref[...]
    o_ref[...] = y


def _mlp(x2d, oa, ob, oc, w_out, g_ffn, w_up, conv_w, conv_b, w_down, final_g, S, final_norm):
    T = x2d.shape[0]
    tm, halo = TM_FFN, BF16_SUBLANES
    per = tm // halo
    last = T // halo - 1
    resident = lambda shape: pl.BlockSpec(shape, lambda i: (0,) * len(shape),
                                          pipeline_mode=pl.Buffered(1))

    def with_halos(width):
        return [pl.BlockSpec((halo, width), lambda i: (jnp.maximum(i * per - 1, 0), 0)),
                pl.BlockSpec((tm, width), lambda i: (i, 0)),
                pl.BlockSpec((halo, width), lambda i: (jnp.minimum((i + 1) * per, last), 0))]

    return pl.pallas_call(
        functools.partial(_mlp_kernel, tiles_per_seq=S // tm, final_norm=final_norm),
        grid=(T // tm,),
        in_specs=(with_halos(D_MODEL) + with_halos(oa.shape[1]) + with_halos(ob.shape[1])
                  + with_halos(oc.shape[1])
                  + [resident(w_out.shape), resident((1, D_MODEL)),
                     resident((D_MODEL, 2 * D_FF)), resident((2, 3, D_FF)), resident((2, 1, D_FF)),
                     resident((D_FF, D_MODEL)), resident((1, D_MODEL))]),
        out_specs=pl.BlockSpec((tm, D_MODEL), lambda i: (i, 0)),
        out_shape=jax.ShapeDtypeStruct((T, D_MODEL), F32),
        scratch_shapes=[pltpu.VMEM((tm, D_FF), BF16)],
        compiler_params=pltpu.CompilerParams(dimension_semantics=("arbitrary",),
                                             vmem_limit_bytes=VMEM_LIMIT_BYTES),
        name="out_proj_conv_mlp",
    )(x2d, x2d, x2d, oa, oa, oa, ob, ob, ob, oc, oc, oc, w_out, g_ffn,
      w_up, conv_w, conv_b, w_down, final_g)


def _pad_groups(w, n_groups, width, offset=0):
    r = w.shape[0]
    w = w.reshape(r, n_groups, width)
    w = jnp.pad(w, ((0, 0), (0, 0), (offset, LANES - width - offset)))
    return w.reshape(r, n_groups * LANES)


def _rope_core(S, dim):
    nf = dim // 4
    t = jnp.arange(S, dtype=jnp.int32)
    rows = (t // GRID_W).astype(F32)
    cols = (t % GRID_W).astype(F32)
    inv = ROPE_BASE ** (-jnp.arange(nf, dtype=F32) / nf)
    ar = rows[:, None] * inv
    ac = cols[:, None] * inv
    cos = jnp.concatenate([jnp.cos(ar), jnp.cos(ar), jnp.cos(ac), jnp.cos(ac)], axis=-1)
    sin = jnp.concatenate([-jnp.sin(ar), jnp.sin(ar), -jnp.sin(ac), jnp.sin(ac)], axis=-1)
    return cos, sin


def _rope_tables(S):
    cos_a, sin_a = _rope_core(S, HEAD_DIM)
    cos_b, sin_b = _rope_core(S, MLA_ROPE)
    pad = LANES - MLA_NOPE - MLA_ROPE
    return (jnp.tile(cos_a, (1, 2)), jnp.tile(sin_a, (1, 2)),
            jnp.concatenate([jnp.ones((S, MLA_NOPE), F32), cos_b, jnp.zeros((S, pad), F32)], -1),
            jnp.concatenate([jnp.zeros((S, MLA_NOPE), F32), sin_b, jnp.zeros((S, pad), F32)], -1))


def _prep_w_in(w):
    a_end = (N_HEADS_A + 2 * N_KV_A) * HEAD_DIM
    b_end = a_end + MLA_Q_RANK + MLA_KV_RANK + MLA_ROPE
    cq = w[:, a_end:a_end + MLA_Q_RANK]
    ckv = w[:, a_end + MLA_Q_RANK:a_end + MLA_Q_RANK + MLA_KV_RANK]
    kr = w[:, b_end - MLA_ROPE:b_end]
    out = jnp.concatenate([
        w[:, :a_end],
        jnp.pad(cq, ((0, 0), (0, CQ_PAD - MLA_Q_RANK))),
        ckv,
        jnp.pad(kr, ((0, 0), (MLA_NOPE, LANES - MLA_NOPE - MLA_ROPE))),
        w[:, b_end:],
    ], axis=-1).astype(BF16)
    assert out.shape[1] == IN_COLS
    return out


def _twice(v):
    return jnp.tile(v, 2).reshape(1, 2 * v.shape[0])


def kernel(x, norm_attn, w_in, q_norm_a, k_norm_a, q_a_norm_b, w_uq_b, kv_a_norm_b, w_ukv_b,
           lambda_q1_c, lambda_k1_c, lambda_q2_c, lambda_k2_c, subln_c, w_out,
           norm_ffn, w_up, conv_w, conv_b, w_down, final_norm):
    B, S, D = x.shape
    depth = w_in.shape[0]
    assert D == D_MODEL and S % max(TM_IN, TM_FFN, TQ, TK) == 0 and S % GRID_W == 0
    assert TM_IN == TK
    T = B * S

    tabs = _rope_tables(S)
    row = lambda v: v.reshape(1, -1)
    a_pairs = [(j, j + N_HEADS_A // 2) for j in range(N_HEADS_A // 2)]
    a_rows = np.concatenate([np.arange(h * HEAD_DIM, (h + 1) * HEAD_DIM)
                             for pair in a_pairs for h in pair])
    xc = x.reshape(T, D)
    for l in range(depth):
        lam_init = 0.8 - 0.6 * math.exp(-0.3 * l)
        wuq_p = jnp.pad(_pad_groups(w_uq_b[l], N_HEADS_B, MLA_NOPE + MLA_ROPE),
                        ((0, CQ_PAD - MLA_Q_RANK), (0, 0))).astype(BF16)
        wukv = w_ukv_b[l].reshape(MLA_KV_RANK, N_HEADS_B, MLA_NOPE + MLA_V)
        v_cols = [_pad_groups(wukv[:, h, MLA_NOPE:], 1, MLA_V, offset=HALF * (h % 2))
                  for h in range(N_HEADS_B)]
        wukv_p = jnp.concatenate(
            [_pad_groups(wukv[:, :, :MLA_NOPE].reshape(MLA_KV_RANK, -1), N_HEADS_B, MLA_NOPE)]
            + v_cols, axis=-1).astype(BF16)

        qa, kta, va, qb, ktb, vb, qc, ktc, vc, nrm = _in_proj(
            xc, row(norm_attn[l]), _prep_w_in(w_in[l]), tabs,
            _twice(q_norm_a[l]), _twice(k_norm_a[l]),
            jnp.pad(q_a_norm_b[l], (0, CQ_PAD - MLA_Q_RANK)).reshape(1, CQ_PAD),
            row(kv_a_norm_b[l]), wuq_p, wukv_p, B, S)

        nrm = jnp.max(nrm.reshape(B, -1, LANES), axis=1)

        def bounded(q0, nq, q_rep, k0, nk, k_rep, per_pipeline):
            q2 = jnp.repeat(nrm[:, q0:q0 + nq], q_rep, axis=1)
            k2 = jnp.repeat(nrm[:, k0:k0 + nk], k_rep, axis=1)
            ok = q2 * k2 * NRM_MARGIN <= SAFE_LOG2_RANGE ** 2
            return jnp.all(ok.reshape(B, -1, per_pipeline), axis=-1).astype(jnp.int32)

        qk_a = (HEAD_DIM * jnp.max(jnp.abs(q_norm_a[l])) * jnp.max(jnp.abs(k_norm_a[l]))
                * (HEAD_DIM ** -0.5 * LOG2E))
        bounded_a = jnp.broadcast_to(
            (qk_a * NRM_MARGIN <= SAFE_LOG2_RANGE).astype(jnp.int32),
            (B, N_HEADS_A // HEADS_PER_PIPELINE))

        group = N_HEADS_A // N_KV_A
        oa = _attention(
            functools.partial(_attn_ab_kernel, n_heads=N_HEADS_A,
                              per_pipeline=HEADS_PER_PIPELINE, q_index=lambda h: h // 2,
                              kt_index=lambda h: 2 * (h // group) + h % 2,
                              v_index=lambda h: h // group, out_pairs=a_pairs, n_keys=S),
            bounded_a, N_HEADS_A, qa, kta, va, [], N_HEADS_A * HEAD_DIM, B, S)
        same = lambda h: h
        ob = _attention(
            functools.partial(_attn_ab_kernel, n_heads=N_HEADS_B,
                              per_pipeline=HEADS_PER_PIPELINE, q_index=same, kt_index=same,
                              v_index=same,
                              out_pairs=[(2 * j, 2 * j + 1) for j in range(N_HEADS_B // 2)],
                              n_keys=S),
            bounded(NRM_QB, N_HEADS_B, 1, NRM_KB, N_HEADS_B, 1, HEADS_PER_PIPELINE), N_HEADS_B,
            qb, ktb, vb, [], N_HEADS_B * MLA_V, B, S)
        oc = _attention(
            functools.partial(_attn_c_kernel, n_keys=S, lam_init=lam_init),
            bounded(NRM_QC, N_HEADS_C, 2, NRM_KC, 2 * N_HEADS_C, 1, HEADS_PER_PIPELINE),
            2 * N_HEADS_C, qc, ktc, vc,
            [row(lambda_q1_c[l]), row(lambda_k1_c[l]), row(lambda_q2_c[l]), row(lambda_k2_c[l]),
             _twice(subln_c[l])],
            N_HEADS_C * DIFF_V, B, S)

        w_out_l = jnp.concatenate([w_out[l][a_rows], w_out[l][N_HEADS_A * HEAD_DIM:]], axis=0)
        xc = _mlp(xc, oa, ob, oc, w_out_l.astype(BF16), row(norm_ffn[l]), w_up[l].astype(BF16),
                  conv_w[l].reshape(3, 2, D_FF).transpose(1, 0, 2),
                  conv_b[l].reshape(2, 1, D_FF), w_down[l].astype(BF16),
                  row(final_norm), S, final_norm=(l == depth - 1))
    return xc.reshape(B, S, D)
```

```python
import functools
import math

import jax
import jax.numpy as jnp
import numpy as np
from jax import lax
from jax.experimental import pallas as pl
from jax.experimental.pallas import tpu as pltpu

D_MODEL = 1024
GRID_W = 64
HEAD_DIM = 64
N_HEADS_A = 8
N_KV_A = 2
N_HEADS_B = 4
MLA_Q_RANK = 192
MLA_KV_RANK = 128
MLA_NOPE = 64
MLA_ROPE = 32
MLA_V = 64
N_HEADS_C = 4
DIFF_QK = 32
DIFF_V = 64
D_FF = 2816
ROPE_BASE = 10000.0
EPS = 1e-6

LANES = 128
HALF = LANES // 2
BF16_SUBLANES = 16
VMEM_LIMIT_BYTES = 56 * 1024 * 1024

LOG2E = math.log2(math.e)
assert HEAD_DIM == MLA_V == DIFF_V == 2 * DIFF_QK == HALF

OFF_QA = 0
OFF_KA = OFF_QA + N_HEADS_A * HEAD_DIM
OFF_VA = OFF_KA + LANES
OFF_CQ = OFF_VA + LANES
CQ_PAD = 2 * LANES
OFF_CKV = OFF_CQ + CQ_PAD
OFF_KR = OFF_CKV + MLA_KV_RANK
OFF_QC = OFF_KR + LANES
OFF_KC = OFF_QC + N_HEADS_C * HALF
OFF_VC = OFF_KC + N_HEADS_C * HALF
IN_COLS = OFF_VC + N_HEADS_C * HALF

NRM_QB = 0
NRM_KB = NRM_QB + N_HEADS_B
NRM_QC = NRM_KB + N_HEADS_B
NRM_KC = NRM_QC + N_HEADS_C
NRM_GROUPS = 2 * N_HEADS_B + N_HEADS_C
NRM_MARGIN = 1.03
SAFE_LOG2_RANGE = 55.0

F32 = jnp.float32
BF16 = jnp.bfloat16

TM_IN = 512
TM_FFN = 512
FFN_CHUNK = 256
FFN_DOWN_GROUP = 6 * FFN_CHUNK
FFN_DOWN_LAG = 2
TQ = 512
TQ_SUB_AB = 512
TQ_SUB_C = 256
TK = 512
SCORE_LOOKAHEAD = 2
HEADS_PER_PIPELINE = 4
HEADS_PER_PIPELINE_AB = 2


def _dot(a, b):
    return jnp.dot(a, b, preferred_element_type=F32)


def _rms(x, n):
    ms = jnp.sum(x * x, axis=-1, keepdims=True) * (1.0 / n)
    return x * lax.rsqrt(ms + EPS)


def _low_lanes():
    return lax.broadcasted_iota(jnp.int32, (1, LANES), 1) < HALF


def _half_rms(x):
    low = _low_lanes()
    sq = x * x
    s_lo = jnp.sum(jnp.where(low, sq, 0.0), axis=-1, keepdims=True)
    s_hi = jnp.sum(jnp.where(low, 0.0, sq), axis=-1, keepdims=True)
    return x * lax.rsqrt(jnp.where(low, s_lo, s_hi) * (1.0 / HALF) + EPS)


def _in_kernel(x_ref, gat_ref, w_ref, cos_a_ref, sin_a_ref, cos_b_ref, sin_b_ref,
               qg_a_ref, kg_a_ref, qg_b_ref, kvg_b_ref, wuq_ref, wukv_ref, sel_ref,
               qa_ref, kta_ref, va_ref, qb_ref, ktb_ref, vb_ref, qc_ref, ktc_ref, vc_ref, nrm_ref):
    x = x_ref[...]
    tm = x.shape[0]
    h = (_rms(x, D_MODEL) * gat_ref[...]).astype(BF16)

    lane = lax.broadcasted_iota(jnp.int32, (1, LANES), 1)
    low = lane < HALF
    row = lax.broadcasted_iota(jnp.int32, (LANES, 1), 0)

    z = _dot(h, w_ref[...])

    def seg(off, width):
        return z[:, off:off + width]

    def swap(y, nf):
        up = pltpu.roll(y, LANES - nf, axis=1)
        down = pltpu.roll(y, nf, axis=1)
        return jnp.where((lane & nf) == 0, up, down)

    def rope(y, c, s, nf):
        return y * c + swap(y, nf) * s

    def group(z, g):
        return z[:, g * LANES:(g + 1) * LANES]

    squares = []

    def note(y):
        squares.append((y * y).astype(BF16))

    def noted_row_sums(first):
        return _dot(jnp.concatenate(squares[first:], axis=-1),
                    sel_ref[first * LANES:len(squares) * LANES])

    cos_b, sin_b = cos_b_ref[...], sin_b_ref[...]
    nf_b = MLA_ROPE // 4
    cq = (_rms(seg(OFF_CQ, CQ_PAD), MLA_Q_RANK) * qg_b_ref[...]).astype(BF16)
    qb = _dot(cq, wuq_ref[...])
    scale_b = (MLA_NOPE + MLA_ROPE) ** -0.5 * LOG2E
    for hh in range(N_HEADS_B):
        y = rope(group(qb, hh), cos_b, sin_b, nf_b) * scale_b
        qb_ref[0, hh] = y.astype(BF16)
        note(y)
    ckv = (_rms(seg(OFF_CKV, MLA_KV_RANK), MLA_KV_RANK) * kvg_b_ref[...]).astype(BF16)
    kvb = _dot(ckv, wukv_ref[...])
    kr = rope(seg(OFF_KR, LANES), cos_b, sin_b, nf_b)
    for hh in range(N_HEADS_B):
        y = group(kvb, hh) + kr
        ktb_ref[0, hh] = y.T.astype(BF16)
        note(y)
        ones_other_half = jnp.where(low, float(hh % 2), float(1 - hh % 2))
        vb_ref[0, hh] = (group(kvb, N_HEADS_B + hh) + ones_other_half).astype(BF16)
    n2 = noted_row_sums(0)
    n_noted_b = len(squares)

    zq = seg(OFF_QC, N_HEADS_C * HALF)
    zk = seg(OFF_KC, N_HEADS_C * HALF)
    zv = seg(OFF_VC, N_HEADS_C * HALF)
    k_groups = []
    for p in range(N_HEADS_C // 2):
        y = group(zq, p) * (DIFF_QK ** -0.5 * LOG2E)
        qc_ref[0, p] = y.astype(BF16)
        note(y)
        k_groups.append(group(zk, p))
        kt = group(zk, p).T
        for sub in range(4):
            mine = (row >= sub * DIFF_QK) & (row < (sub + 1) * DIFF_QK)
            ktc_ref[0, 4 * p + sub, 0] = jnp.where(mine, kt, 0.0).astype(BF16)
        vz = group(zv, p)
        vc_ref[0, 2 * p] = jnp.where(low, vz, 1.0).astype(BF16)
        vc_ref[0, 2 * p + 1] = jnp.where(low, 1.0, vz).astype(BF16)
    for kz in k_groups:
        note(kz)
    n2 = n2 + noted_row_sums(n_noted_b)
    nrm_ref[0] = jnp.max(n2, axis=0, keepdims=True)

    cos_a, sin_a = cos_a_ref[...], sin_a_ref[...]
    nf_a = HEAD_DIM // 4
    zq = seg(OFF_QA, N_HEADS_A * HEAD_DIM)
    for p in range(N_HEADS_A // 2):
        y = rope(_half_rms(group(zq, p)) * qg_a_ref[...], cos_a, sin_a, nf_a)
        qa_ref[0, p] = (y * (HEAD_DIM ** -0.5 * LOG2E)).astype(BF16)
    y = rope(_half_rms(seg(OFF_KA, LANES)) * kg_a_ref[...], cos_a, sin_a, nf_a)
    kt = y.T.astype(BF16)
    zeros = jnp.zeros((HALF, tm), BF16)
    for g in range(N_KV_A):
        kg = kt[g * HALF:(g + 1) * HALF]
        kta_ref[0, 2 * g] = jnp.concatenate([kg, zeros], axis=0)
        kta_ref[0, 2 * g + 1] = jnp.concatenate([zeros, kg], axis=0)
    zv = seg(OFF_VA, LANES)
    va_ref[0, 0] = jnp.where(low, zv, 1.0).astype(BF16)
    va_ref[0, 1] = jnp.where(low, 1.0, zv).astype(BF16)


def _norm_routing():
    sel = np.zeros((NRM_GROUPS, LANES, LANES), np.float32)
    g = 0
    for slot in ([NRM_QB + h for h in range(N_HEADS_B)] + [NRM_KB + h for h in range(N_HEADS_B)]):
        sel[g, :, slot] = 1.0
        g += 1
    for p in range(N_HEADS_C // 2):
        for f in range(2):
            sel[g, f * HALF:(f + 1) * HALF, NRM_QC + 2 * p + f] = 1.0
        g += 1
    for p in range(N_HEADS_C // 2):
        for sub in range(4):
            sel[g, sub * DIFF_QK:(sub + 1) * DIFF_QK, NRM_KC + 4 * p + sub] = 1.0
        g += 1
    assert g == NRM_GROUPS
    return jnp.asarray(sel.reshape(NRM_GROUPS * LANES, LANES), BF16)


def _in_proj(x2d, gat, w_in_r, tabs, qg_a, kg_a, qg_b, kvg_b, wuq_p, wukv_p, B, S):
    T = B * S
    tm = TM_IN
    nst = S // tm
    const = lambda shape: pl.BlockSpec(shape, lambda i: (0,) * len(shape))
    tab = pl.BlockSpec((tm, LANES), lambda i: (i % nst, 0))
    hm = lambda H: pl.BlockSpec((1, H, tm, LANES), lambda i: (i // nst, 0, i % nst, 0))
    hmt = lambda H: pl.BlockSpec((1, H, LANES, tm), lambda i: (i // nst, 0, 0, i % nst))
    sds = lambda H: jax.ShapeDtypeStruct((B, H, S, LANES), BF16)
    sdt = lambda H: jax.ShapeDtypeStruct((B, H, LANES, S), BF16)
    return pl.pallas_call(
        _in_kernel,
        grid=(T // tm,),
        in_specs=[pl.BlockSpec((tm, D_MODEL), lambda i: (i, 0)),
                  const((1, D_MODEL)), const((D_MODEL, IN_COLS)),
                  tab, tab, tab, tab,
                  const((1, LANES)), const((1, LANES)), const((1, CQ_PAD)), const((1, MLA_KV_RANK)),
                  const((CQ_PAD, N_HEADS_B * LANES)), const((MLA_KV_RANK, 2 * N_HEADS_B * LANES)),
                  const((NRM_GROUPS * LANES, LANES))],
        out_specs=[hm(N_HEADS_A // 2), hmt(2 * N_KV_A), hm(N_KV_A),
                   hm(N_HEADS_B), hmt(N_HEADS_B), hm(N_HEADS_B),
                   hm(N_HEADS_C // 2),
                   pl.BlockSpec((1, 2 * N_HEADS_C, 1, LANES, tm),
                                lambda i: (i // nst, 0, i % nst, 0, 0)),
                   hm(N_HEADS_C),
                   pl.BlockSpec((1, 1, LANES), lambda i: (i, 0, 0))],
        out_shape=[sds(N_HEADS_A // 2), sdt(2 * N_KV_A), sds(N_KV_A),
                   sds(N_HEADS_B), sdt(N_HEADS_B), sds(N_HEADS_B),
                   sds(N_HEADS_C // 2),
                   jax.ShapeDtypeStruct((B, 2 * N_HEADS_C, S // tm, LANES, tm), BF16),
                   sds(N_HEADS_C),
                   jax.ShapeDtypeStruct((T // tm, 1, LANES), F32)],
        compiler_params=pltpu.CompilerParams(dimension_semantics=("arbitrary",),
                                             vmem_limit_bytes=VMEM_LIMIT_BYTES),
        name="in_proj",
    )(x2d, gat, w_in_r, *tabs, qg_a, kg_a, qg_b, kvg_b, wuq_p, wukv_p, _norm_routing())


def _lane_group_max(s):
    m = s[:, :LANES]
    for j in range(1, s.shape[1] // LANES):
        m = jnp.maximum(m, s[:, j * LANES:(j + 1) * LANES])
    return m


def _pv_bounded(streams, n_chunks):
    items = [(r, i) for r in range(n_chunks) for i in range(len(streams))]
    accs = [None] * len(streams)
    in_flight = {}
    for k in range(len(items) + SCORE_LOOKAHEAD):
        if k < len(items):
            r, i = items[k]
            scores, _, row_off = streams[i]
            s, off = scores(r), row_off(r)
            in_flight[k] = s if off is None else s + off
        if k >= SCORE_LOOKAHEAD:
            r, i = items[k - SCORE_LOOKAHEAD]
            d = _dot(jnp.exp2(in_flight.pop(k - SCORE_LOOKAHEAD)).astype(BF16), streams[i][1](r))
            accs[i] = d if accs[i] is None else accs[i] + d
    return accs


def _pv_exact(streams, n_chunks, s_ref):
    rows = s_ref.shape[0] // len(streams)
    maxes = []
    for i, (scores, _, row_off) in enumerate(streams):
        m_run = None
        for r in range(n_chunks):
            s = scores(r)
            s_ref[i * rows:(i + 1) * rows, r * TK:(r + 1) * TK] = s
            m, off = _lane_group_max(s), row_off(r)
            if off is not None:
                m = m + off
            m_run = m if m_run is None else jnp.maximum(m_run, m)
        maxes.append(jnp.max(m_run, axis=-1, keepdims=True))
    accs = []
    for i, (_, values, row_off) in enumerate(streams):
        acc = None
        for r in range(n_chunks):
            off = row_off(r)
            shift = maxes[i] if off is None else maxes[i] - off
            p = jnp.exp2(s_ref[i * rows:(i + 1) * rows, r * TK:(r + 1) * TK] - shift)
            d = _dot(p.astype(BF16), values(r))
            acc = d if acc is None else acc + d
        accs.append(acc)
    return accs


def _run_heads(bounded, heads, head_streams, n_chunks, s_ref, acc_ref, roll_exact=False):
    @pl.when(bounded)
    def _():
        per_head = [head_streams(h) for h in heads]
        n_sub = len(per_head[0])
        accs = _pv_bounded([s for streams in per_head for s in streams], n_chunks)
        for k, h in enumerate(heads):
            acc_ref[h] = jnp.concatenate(accs[k * n_sub:(k + 1) * n_sub], axis=0)

    @pl.when(jnp.logical_not(bounded))
    def _():
        def one_head(h):
            acc_ref[h] = jnp.concatenate(_pv_exact(head_streams(h), n_chunks, s_ref), axis=0)

        if roll_exact:
            lax.fori_loop(0, len(heads), lambda k, c: (one_head(heads[0] + k), c)[1], 0)
        else:
            for h in heads:
                one_head(h)


def _normalised_pair(acc_low, acc_high):
    low = _low_lanes()
    num = jnp.where(low, acc_low, acc_high)
    den = pltpu.roll(jnp.where(low, acc_high, acc_low), HALF, axis=1)
    return num / den


def _attn_ab_kernel(bounded_ref, q_ref, kt_ref, v_ref, o_ref, s_ref, acc_ref, *,
                    n_heads, per_pipeline, q_index, kt_index, v_index, out_pairs, n_keys):
    n_chunks = n_keys // TK

    def head_streams(h):
        q = q_ref[0, q_index(h)]
        return [(lambda r, qs=q[r0:r0 + TQ_SUB_AB]:
                 _dot(qs, kt_ref[0, kt_index(h), :, r * TK:(r + 1) * TK]),
                 lambda r: v_ref[0, v_index(h), r * TK:(r + 1) * TK, :],
                 lambda r: None) for r0 in range(0, TQ, TQ_SUB_AB)]

    def head_group(j, carry):
        heads = [per_pipeline * j + k for k in range(per_pipeline)]
        _run_heads(bounded_ref[pl.program_id(0), j] != 0, heads, head_streams, n_chunks, s_ref,
                   acc_ref, roll_exact=True)
        return carry

    if n_heads == per_pipeline:
        head_group(0, None)
    else:
        lax.fori_loop(0, n_heads // per_pipeline, head_group, 0)
    o_ref[...] = jnp.concatenate([_normalised_pair(acc_ref[a], acc_ref[b]) for a, b in out_pairs],
                                 axis=-1).astype(BF16)


def _attn_c_kernel(bounded_ref, q_ref, kt_ref, v_ref, lq1_ref, lk1_ref, lq2_ref, lk2_ref,
                   subln_ref, o_ref, s_ref, acc_ref, *, n_keys, lam_init):
    assert TQ == TK
    nc = n_keys // TK
    cd = pl.program_id(1)
    col = lax.broadcasted_iota(jnp.int32, (1, TK), 1).astype(F32)
    row = lax.broadcasted_iota(jnp.int32, (TQ_SUB_C, 1), 0).astype(F32)
    subs = [(r, r + TQ_SUB_C) for r in range(0, TQ, TQ_SUB_C)]
    neg_dist = [-jnp.abs((lax.broadcasted_iota(jnp.int32, (TQ_SUB_C, TK), 0) + r0
                          - lax.broadcasted_iota(jnp.int32, (TQ_SUB_C, TK), 1)).astype(F32))
                for r0, _ in subs]

    chunk = [(cd + r) % nc for r in range(nc)]

    def head_bias(hh):
        slope = jnp.float32(LOG2E * 2.0 ** (-8.0 * N_HEADS_C / N_HEADS_C))
        for k in range(N_HEADS_C - 2, -1, -1):
            slope = jnp.where(hh == k, LOG2E * 2.0 ** (-8.0 * (k + 1) / N_HEADS_C), slope)
        key_term, row_off = [None], [[None] * len(subs)]
        for r in range(1, nc):
            side = jnp.where(chunk[r] < cd, slope, -slope)
            key_term.append(col * side)
            base = -slope * (jnp.abs(chunk[r] - cd) * TK).astype(F32)
            row_off.append([base - side * (row + float(r0)) for r0, _ in subs])
        return slope, key_term, row_off

    def head_pair(p, carry):
        bias = {}

        def streams_of(k):
            idx, hh = 4 * p + k, 2 * p + k // 2
            if not isinstance(k, int):
                slope, key_term, row_off = head_bias(hh)
            else:
                if k // 2 not in bias:
                    bias[k // 2] = head_bias(hh)
                slope, key_term, row_off = bias[k // 2]

            def scores(r, si):
                s = _dot(q_ref[0, p][subs[si][0]:subs[si][1]], kt_ref[0, idx, chunk[r]])
                return s + (neg_dist[si] * slope if r == 0 else key_term[r])

            return [(functools.partial(scores, si=si),
                     lambda r: v_ref[0, hh, pl.ds(pl.multiple_of(chunk[r] * TK, TK), TK), :],
                     lambda r, si=si: row_off[r][si]) for si in range(len(subs))]

        _run_heads(bounded_ref[pl.program_id(0), p] != 0, list(range(4)), streams_of, nc, s_ref,
                   acc_ref.at[pl.ds(4 * p, 4)], roll_exact=True)
        return carry

    assert HEADS_PER_PIPELINE == 4
    lax.fori_loop(0, N_HEADS_C // 2, head_pair, 0)

    lam = (jnp.exp(jnp.sum(lq1_ref[...] * lk1_ref[...], axis=-1, keepdims=True))
           - jnp.exp(jnp.sum(lq2_ref[...] * lk2_ref[...], axis=-1, keepdims=True)) + lam_init)
    outs = []
    for p in range(N_HEADS_C // 2):
        o = (_normalised_pair(acc_ref[4 * p], acc_ref[4 * p + 2])
             - lam * _normalised_pair(acc_ref[4 * p + 1], acc_ref[4 * p + 3]))
        outs.append(_half_rms(o) * subln_ref[...] * (1.0 - lam_init))
    o_ref[...] = jnp.concatenate(outs, axis=-1).astype(BF16)


def _attention(kernel_fn, bounded, n_acc, q, kt, v, extras, out_width, B, S):
    nq = S // TQ
    hq, hv = q.shape[1], v.shape[1]
    extra_specs = [pl.BlockSpec(e.shape, lambda b, i, f, n=e.ndim: (0,) * n) for e in extras]
    return pl.pallas_call(
        kernel_fn,
        grid_spec=pltpu.PrefetchScalarGridSpec(
            num_scalar_prefetch=1,
            grid=(B, nq),
            in_specs=[pl.BlockSpec((1, hq, TQ, LANES), lambda b, i, f: (b, 0, i, 0)),
                      pl.BlockSpec((1,) + kt.shape[1:],
                                   lambda b, i, f, n=kt.ndim: (b,) + (0,) * (n - 1)),
                      pl.BlockSpec((1, hv, S, LANES), lambda b, i, f: (b, 0, 0, 0))] + extra_specs,
            out_specs=pl.BlockSpec((TQ, out_width), lambda b, i, f: (b * nq + i, 0)),
            scratch_shapes=[pltpu.VMEM((TQ, S), F32),
                            pltpu.VMEM((n_acc, TQ, LANES), F32)]),
        out_shape=jax.ShapeDtypeStruct((B * S, out_width), BF16),
        compiler_params=pltpu.CompilerParams(dimension_semantics=("arbitrary", "arbitrary"),
                                             vmem_limit_bytes=VMEM_LIMIT_BYTES),
        name=kernel_fn.func.__name__.strip("_"),
    )(bounded, q, kt, v, *extras)


def _mlp_kernel(xp_ref, x_ref, xn_ref, ap_ref, a_ref, an_ref, bp_ref, b_ref, bn_ref,
                cp_ref, c_ref, cn_ref, wo_ref, g_ref, wu_ref, cw_ref, cb_ref, wd_ref, fg_ref,
                o_ref, act_ref, *, tiles_per_seq, final_norm):
    i = pl.program_id(0)
    tm = x_ref.shape[0]
    halo = xp_ref.shape[0]

    def rows(p, m, n):
        return jnp.concatenate([p[...], m[...], n[...]], axis=0)

    o_ext = jnp.concatenate([rows(ap_ref, a_ref, an_ref), rows(bp_ref, b_ref, bn_ref),
                             rows(cp_ref, c_ref, cn_ref)], axis=1)
    x1 = rows(xp_ref, x_ref, xn_ref) + _dot(o_ext, wo_ref[...])
    h = (_rms(x1, D_MODEL) * g_ref[...]).astype(BF16)
    at_start = (i % tiles_per_seq) == 0
    at_end = (i % tiles_per_seq) == tiles_per_seq - 1
    zeros = jnp.zeros((halo, D_MODEL), BF16)
    hext = jnp.concatenate([jnp.where(at_start, zeros, h[:halo]), h[halo:halo + tm],
                            jnp.where(at_end, zeros, h[halo + tm:])], axis=0)

    def branch(k, c0):
        u = _dot(hext, wu_ref[:, k * D_FF + c0:k * D_FF + c0 + FFN_CHUNK])
        cw = cw_ref[k, :, c0:c0 + FFN_CHUNK]
        n = u.shape[0]
        prev = pltpu.roll(u, 1, axis=0)[halo:halo + tm]
        nxt = pltpu.roll(u, n - 1, axis=0)[halo:halo + tm]
        return (prev * cw[0:1] + u[halo:halo + tm] * cw[1:2] + nxt * cw[2:3]
                + cb_ref[k, :, c0:c0 + FFN_CHUNK])

    y = x1[halo:halo + tm]
    done = 0
    for c0 in range(0, D_FF, FFN_CHUNK):
        g = branch(0, c0)
        val = branch(1, c0)
        act_ref[:, c0:c0 + FFN_CHUNK] = (g / (1.0 + jnp.exp(-g)) * val).astype(BF16)
        ready = c0 + FFN_CHUNK - FFN_DOWN_LAG * FFN_CHUNK
        if ready - done >= FFN_DOWN_GROUP:
            y = y + _dot(act_ref[:, done:done + FFN_DOWN_GROUP], wd_ref[done:done + FFN_DOWN_GROUP])
            done += FFN_DOWN_GROUP
    y = y + _dot(act_ref[:, done:], wd_ref[done:])
    if final_norm:
        y = _rms(y, D_MODEL) * fg_ref[...]
    o_ref[...] = y


def _mlp(x2d, oa, ob, oc, w_out, g_ffn, w_up, conv_w, conv_b, w_down, final_g, S, final_norm):
    T = x2d.shape[0]
    tm, halo = TM_FFN, BF16_SUBLANES
    per = tm // halo
    last = T // halo - 1
    resident = lambda shape: pl.BlockSpec(shape, lambda i: (0,) * len(shape),
                                          pipeline_mode=pl.Buffered(1))

    def with_halos(width):
        return [pl.BlockSpec((halo, width), lambda i: (jnp.maximum(i * per - 1, 0), 0)),
                pl.BlockSpec((tm, width), lambda i: (i, 0)),
                pl.BlockSpec((halo, width), lambda i: (jnp.minimum((i + 1) * per, last), 0))]

    return pl.pallas_call(
        functools.partial(_mlp_kernel, tiles_per_seq=S // tm, final_norm=final_norm),
        grid=(T // tm,),
        in_specs=(with_halos(D_MODEL) + with_halos(oa.shape[1]) + with_halos(ob.shape[1])
                  + with_halos(oc.shape[1])
                  + [resident(w_out.shape), resident((1, D_MODEL)),
                     resident((D_MODEL, 2 * D_FF)), resident((2, 3, D_FF)), resident((2, 1, D_FF)),
                     resident((D_FF, D_MODEL)), resident((1, D_MODEL))]),
        out_specs=pl.BlockSpec((tm, D_MODEL), lambda i: (i, 0)),
        out_shape=jax.ShapeDtypeStruct((T, D_MODEL), F32),
        scratch_shapes=[pltpu.VMEM((tm, D_FF), BF16)],
        compiler_params=pltpu.CompilerParams(dimension_semantics=("arbitrary",),
                                             vmem_limit_bytes=VMEM_LIMIT_BYTES),
        name="out_proj_conv_mlp",
    )(x2d, x2d, x2d, oa, oa, oa, ob, ob, ob, oc, oc, oc, w_out, g_ffn,
      w_up, conv_w, conv_b, w_down, final_g)


def _pad_groups(w, n_groups, width, offset=0):
    r = w.shape[0]
    w = w.reshape(r, n_groups, width)
    w = jnp.pad(w, ((0, 0), (0, 0), (offset, LANES - width - offset)))
    return w.reshape(r, n_groups * LANES)


def _rope_core(S, dim):
    nf = dim // 4
    t = jnp.arange(S, dtype=jnp.int32)
    rows = (t // GRID_W).astype(F32)
    cols = (t % GRID_W).astype(F32)
    inv = ROPE_BASE ** (-jnp.arange(nf, dtype=F32) / nf)
    ar = rows[:, None] * inv
    ac = cols[:, None] * inv
    cos = jnp.concatenate([jnp.cos(ar), jnp.cos(ar), jnp.cos(ac), jnp.cos(ac)], axis=-1)
    sin = jnp.concatenate([-jnp.sin(ar), jnp.sin(ar), -jnp.sin(ac), jnp.sin(ac)], axis=-1)
    return cos, sin


def _rope_tables(S):
    cos_a, sin_a = _rope_core(S, HEAD_DIM)
    cos_b, sin_b = _rope_core(S, MLA_ROPE)
    pad = LANES - MLA_NOPE - MLA_ROPE
    return (jnp.tile(cos_a, (1, 2)), jnp.tile(sin_a, (1, 2)),
            jnp.concatenate([jnp.ones((S, MLA_NOPE), F32), cos_b, jnp.zeros((S, pad), F32)], -1),
            jnp.concatenate([jnp.zeros((S, MLA_NOPE), F32), sin_b, jnp.zeros((S, pad), F32)], -1))


def _prep_w_in(w):
    a_end = (N_HEADS_A + 2 * N_KV_A) * HEAD_DIM
    b_end = a_end + MLA_Q_RANK + MLA_KV_RANK + MLA_ROPE
    cq = w[:, a_end:a_end + MLA_Q_RANK]
    ckv = w[:, a_end + MLA_Q_RANK:a_end + MLA_Q_RANK + MLA_KV_RANK]
    kr = w[:, b_end - MLA_ROPE:b_end]
    out = jnp.concatenate([
        w[:, :a_end],
        jnp.pad(cq, ((0, 0), (0, CQ_PAD - MLA_Q_RANK))),
        ckv,
        jnp.pad(kr, ((0, 0), (MLA_NOPE, LANES - MLA_NOPE - MLA_ROPE))),
        w[:, b_end:],
    ], axis=-1).astype(BF16)
    assert out.shape[1] == IN_COLS
    return out


def _twice(v):
    return jnp.tile(v, 2).reshape(1, 2 * v.shape[0])


def kernel(x, norm_attn, w_in, q_norm_a, k_norm_a, q_a_norm_b, w_uq_b, kv_a_norm_b, w_ukv_b,
           lambda_q1_c, lambda_k1_c, lambda_q2_c, lambda_k2_c, subln_c, w_out,
           norm_ffn, w_up, conv_w, conv_b, w_down, final_norm):
    B, S, D = x.shape
    depth = w_in.shape[0]
    assert D == D_MODEL and S % max(TM_IN, TM_FFN, TQ, TK) == 0 and S % GRID_W == 0
    assert TM_IN == TK
    T = B * S

    tabs = _rope_tables(S)
    row = lambda v: v.reshape(1, -1)
    a_pairs = [(j, j + N_HEADS_A // 2) for j in range(N_HEADS_A // 2)]
    a_rows = np.concatenate([np.arange(h * HEAD_DIM, (h + 1) * HEAD_DIM)
                             for pair in a_pairs for h in pair])
    xc = x.reshape(T, D)
    for l in range(depth):
        lam_init = 0.8 - 0.6 * math.exp(-0.3 * l)
        wuq_p = jnp.pad(_pad_groups(w_uq_b[l], N_HEADS_B, MLA_NOPE + MLA_ROPE),
                        ((0, CQ_PAD - MLA_Q_RANK), (0, 0))).astype(BF16)
        wukv = w_ukv_b[l].reshape(MLA_KV_RANK, N_HEADS_B, MLA_NOPE + MLA_V)
        v_cols = [_pad_groups(wukv[:, h, MLA_NOPE:], 1, MLA_V, offset=HALF * (h % 2))
                  for h in range(N_HEADS_B)]
        wukv_p = jnp.concatenate(
            [_pad_groups(wukv[:, :, :MLA_NOPE].reshape(MLA_KV_RANK, -1), N_HEADS_B, MLA_NOPE)]
            + v_cols, axis=-1).astype(BF16)

        qa, kta, va, qb, ktb, vb, qc, ktc, vc, nrm = _in_proj(
            xc, row(norm_attn[l]), _prep_w_in(w_in[l]), tabs,
            _twice(q_norm_a[l]), _twice(k_norm_a[l]),
            jnp.pad(q_a_norm_b[l], (0, CQ_PAD - MLA_Q_RANK)).reshape(1, CQ_PAD),
            row(kv_a_norm_b[l]), wuq_p, wukv_p, B, S)

        nrm = jnp.max(nrm.reshape(B, -1, LANES), axis=1)

        def bounded(q0, nq, q_rep, k0, nk, k_rep, per_pipeline):
            q2 = jnp.repeat(nrm[:, q0:q0 + nq], q_rep, axis=1)
            k2 = jnp.repeat(nrm[:, k0:k0 + nk], k_rep, axis=1)
            ok = q2 * k2 * NRM_MARGIN <= SAFE_LOG2_RANGE ** 2
            return jnp.all(ok.reshape(B, -1, per_pipeline), axis=-1).astype(jnp.int32)

        qk_a = (HEAD_DIM * jnp.max(jnp.abs(q_norm_a[l])) * jnp.max(jnp.abs(k_norm_a[l]))
                * (HEAD_DIM ** -0.5 * LOG2E))
        bounded_a = jnp.broadcast_to(
            (qk_a * NRM_MARGIN <= SAFE_LOG2_RANGE).astype(jnp.int32),
            (B, N_HEADS_A // HEADS_PER_PIPELINE_AB))

        group = N_HEADS_A // N_KV_A
        oa = _attention(
            functools.partial(_attn_ab_kernel, n_heads=N_HEADS_A,
                              per_pipeline=HEADS_PER_PIPELINE_AB, q_index=lambda h: h // 2,
                              kt_index=lambda h: 2 * (h // group) + h % 2,
                              v_index=lambda h: h // group, out_pairs=a_pairs, n_keys=S),
            bounded_a, N_HEADS_A, qa, kta, va, [], N_HEADS_A * HEAD_DIM, B, S)
        same = lambda h: h
        ob = _attention(
            functools.partial(_attn_ab_kernel, n_heads=N_HEADS_B,
                              per_pipeline=HEADS_PER_PIPELINE_AB, q_index=same, kt_index=same,
                              v_index=same,
                              out_pairs=[(2 * j, 2 * j + 1) for j in range(N_HEADS_B // 2)],
                              n_keys=S),
            bounded(NRM_QB, N_HEADS_B, 1, NRM_KB, N_HEADS_B, 1, HEADS_PER_PIPELINE_AB), N_HEADS_B,
            qb, ktb, vb, [], N_HEADS_B * MLA_V, B, S)
        oc = _attention(
            functools.partial(_attn_c_kernel, n_keys=S, lam_init=lam_init),
            bounded(NRM_QC, N_HEADS_C, 2, NRM_KC, 2 * N_HEADS_C, 1, HEADS_PER_PIPELINE),
            2 * N_HEADS_C, qc, ktc, vc,
            [row(lambda_q1_c[l]), row(lambda_k1_c[l]), row(lambda_q2_c[l]), row(lambda_k2_c[l]),
             _twice(subln_c[l])],
            N_HEADS_C * DIFF_V, B, S)

        w_out_l = jnp.concatenate([w_out[l][a_rows], w_out[l][N_HEADS_A * HEAD_DIM:]], axis=0)
        xc = _mlp(xc, oa, ob, oc, w_out_l.astype(BF16), row(norm_ffn[l]), w_up[l].astype(BF16),
                  conv_w[l].reshape(3, 2, D_FF).transpose(1, 0, 2),
                  conv_b[l].reshape(2, 1, D_FF), w_down[l].astype(BF16),
                  row(final_norm), S, final_norm=(l == depth - 1))
    return xc.reshape(B, S, D)
```

```python
import functools
import math

import jax
import jax.numpy as jnp
import numpy as np
from jax import lax
from jax.experimental import pallas as pl
from jax.experimental.pallas import tpu as pltpu

D_MODEL = 1024
GRID_W = 64
HEAD_DIM = 64
N_HEADS_A = 8
N_KV_A = 2
N_HEADS_B = 4
MLA_Q_RANK = 192
MLA_KV_RANK = 128
MLA_NOPE = 64
MLA_ROPE = 32
MLA_V = 64
N_HEADS_C = 4
DIFF_QK = 32
DIFF_V = 64
D_FF = 2816
ROPE_BASE = 10000.0
EPS = 1e-6

LANES = 128
HALF = LANES // 2
BF16_SUBLANES = 16
VMEM_LIMIT_BYTES = 56 * 1024 * 1024

LOG2E = math.log2(math.e)
assert HEAD_DIM == MLA_V == DIFF_V == 2 * DIFF_QK == HALF

OFF_QA = 0
OFF_KA = OFF_QA + N_HEADS_A * HEAD_DIM
OFF_VA = OFF_KA + LANES
OFF_CQ = OFF_VA + LANES
CQ_PAD = 2 * LANES
OFF_CKV = OFF_CQ + CQ_PAD
OFF_KR = OFF_CKV + MLA_KV_RANK
OFF_QC = OFF_KR + LANES
OFF_KC = OFF_QC + N_HEADS_C * HALF
OFF_VC = OFF_KC + N_HEADS_C * HALF
IN_COLS = OFF_VC + N_HEADS_C * HALF

NRM_QB = 0
NRM_KB = NRM_QB + N_HEADS_B
NRM_QC = NRM_KB + N_HEADS_B
NRM_KC = NRM_QC + N_HEADS_C
NRM_GROUPS = 2 * N_HEADS_B + N_HEADS_C
NRM_MARGIN = 1.03
SAFE_LOG2_RANGE = 55.0

F32 = jnp.float32
BF16 = jnp.bfloat16

TM_IN = 512
TM_FFN = 512
FFN_CHUNK = 256
FFN_DOWN_GROUP = 6 * FFN_CHUNK
FFN_DOWN_LAG = 2
TQ = 512
TQ_SUB_AB = 512
TQ_SUB_C = 256
TK = 512
SCORE_LOOKAHEAD = 3
HEADS_PER_PIPELINE = 4
HEADS_PER_PIPELINE_AB = 4


def _dot(a, b):
    return jnp.dot(a, b, preferred_element_type=F32)


def _rms(x, n):
    ms = jnp.sum(x * x, axis=-1, keepdims=True) * (1.0 / n)
    return x * lax.rsqrt(ms + EPS)


def _low_lanes():
    return lax.broadcasted_iota(jnp.int32, (1, LANES), 1) < HALF


def _half_rms(x):
    low = _low_lanes()
    sq = x * x
    s_lo = jnp.sum(jnp.where(low, sq, 0.0), axis=-1, keepdims=True)
    s_hi = jnp.sum(jnp.where(low, 0.0, sq), axis=-1, keepdims=True)
    return x * lax.rsqrt(jnp.where(low, s_lo, s_hi) * (1.0 / HALF) + EPS)


def _in_kernel(x_ref, gat_ref, w_ref, cos_a_ref, sin_a_ref, cos_b_ref, sin_b_ref,
               qg_a_ref, kg_a_ref, qg_b_ref, kvg_b_ref, wuq_ref, wukv_ref, sel_ref,
               qa_ref, kta_ref, va_ref, qb_ref, ktb_ref, vb_ref, qc_ref, ktc_ref, vc_ref, nrm_ref):
    x = x_ref[...]
    tm = x.shape[0]
    h = (_rms(x, D_MODEL) * gat_ref[...]).astype(BF16)

    lane = lax.broadcasted_iota(jnp.int32, (1, LANES), 1)
    low = lane < HALF
    row = lax.broadcasted_iota(jnp.int32, (LANES, 1), 0)

    z = _dot(h, w_ref[...])

    def seg(off, width):
        return z[:, off:off + width]

    def swap(y, nf):
        up = pltpu.roll(y, LANES - nf, axis=1)
        down = pltpu.roll(y, nf, axis=1)
        return jnp.where((lane & nf) == 0, up, down)

    def rope(y, c, s, nf):
        return y * c + swap(y, nf) * s

    def group(z, g):
        return z[:, g * LANES:(g + 1) * LANES]

    squares = []

    def note(y):
        squares.append((y * y).astype(BF16))

    def noted_row_sums(first):
        return _dot(jnp.concatenate(squares[first:], axis=-1),
                    sel_ref[first * LANES:len(squares) * LANES])

    cos_b, sin_b = cos_b_ref[...], sin_b_ref[...]
    nf_b = MLA_ROPE // 4
    cq = (_rms(seg(OFF_CQ, CQ_PAD), MLA_Q_RANK) * qg_b_ref[...]).astype(BF16)
    qb = _dot(cq, wuq_ref[...])
    scale_b = (MLA_NOPE + MLA_ROPE) ** -0.5 * LOG2E
    for hh in range(N_HEADS_B):
        y = rope(group(qb, hh), cos_b, sin_b, nf_b) * scale_b
        qb_ref[0, hh] = y.astype(BF16)
        note(y)
    ckv = (_rms(seg(OFF_CKV, MLA_KV_RANK), MLA_KV_RANK) * kvg_b_ref[...]).astype(BF16)
    kvb = _dot(ckv, wukv_ref[...])
    kr = rope(seg(OFF_KR, LANES), cos_b, sin_b, nf_b)
    for hh in range(N_HEADS_B):
        y = group(kvb, hh) + kr
        ktb_ref[0, hh] = y.T.astype(BF16)
        note(y)
        ones_other_half = jnp.where(low, float(hh % 2), float(1 - hh % 2))
        vb_ref[0, hh] = (group(kvb, N_HEADS_B + hh) + ones_other_half).astype(BF16)
    n2 = noted_row_sums(0)
    n_noted_b = len(squares)

    zq = seg(OFF_QC, N_HEADS_C * HALF)
    zk = seg(OFF_KC, N_HEADS_C * HALF)
    zv = seg(OFF_VC, N_HEADS_C * HALF)
    k_groups = []
    for p in range(N_HEADS_C // 2):
        y = group(zq, p) * (DIFF_QK ** -0.5 * LOG2E)
        qc_ref[0, p] = y.astype(BF16)
        note(y)
        k_groups.append(group(zk, p))
        kt = group(zk, p).T
        for sub in range(4):
            mine = (row >= sub * DIFF_QK) & (row < (sub + 1) * DIFF_QK)
            ktc_ref[0, 4 * p + sub, 0] = jnp.where(mine, kt, 0.0).astype(BF16)
        vz = group(zv, p)
        vc_ref[0, 2 * p] = jnp.where(low, vz, 1.0).astype(BF16)
        vc_ref[0, 2 * p + 1] = jnp.where(low, 1.0, vz).astype(BF16)
    for kz in k_groups:
        note(kz)
    n2 = n2 + noted_row_sums(n_noted_b)
    nrm_ref[0] = jnp.max(n2, axis=0, keepdims=True)

    cos_a, sin_a = cos_a_ref[...], sin_a_ref[...]
    nf_a = HEAD_DIM // 4
    zq = seg(OFF_QA, N_HEADS_A * HEAD_DIM)
    for p in range(N_HEADS_A // 2):
        y = rope(_half_rms(group(zq, p)) * qg_a_ref[...], cos_a, sin_a, nf_a)
        qa_ref[0, p] = (y * (HEAD_DIM ** -0.5 * LOG2E)).astype(BF16)
    y = rope(_half_rms(seg(OFF_KA, LANES)) * kg_a_ref[...], cos_a, sin_a, nf_a)
    kt = y.T.astype(BF16)
    zeros = jnp.zeros((HALF, tm), BF16)
    for g in range(N_KV_A):
        kg = kt[g * HALF:(g + 1) * HALF]
        kta_ref[0, 2 * g] = jnp.concatenate([kg, zeros], axis=0)
        kta_ref[0, 2 * g + 1] = jnp.concatenate([zeros, kg], axis=0)
    zv = seg(OFF_VA, LANES)
    va_ref[0, 0] = jnp.where(low, zv, 1.0).astype(BF16)
    va_ref[0, 1] = jnp.where(low, 1.0, zv).astype(BF16)


def _norm_routing():
    sel = np.zeros((NRM_GROUPS, LANES, LANES), np.float32)
    g = 0
    for slot in ([NRM_QB + h for h in range(N_HEADS_B)] + [NRM_KB + h for h in range(N_HEADS_B)]):
        sel[g, :, slot] = 1.0
        g += 1
    for p in range(N_HEADS_C // 2):
        for f in range(2):
            sel[g, f * HALF:(f + 1) * HALF, NRM_QC + 2 * p + f] = 1.0
        g += 1
    for p in range(N_HEADS_C // 2):
        for sub in range(4):
            sel[g, sub * DIFF_QK:(sub + 1) * DIFF_QK, NRM_KC + 4 * p + sub] = 1.0
        g += 1
    assert g == NRM_GROUPS
    return jnp.asarray(sel.reshape(NRM_GROUPS * LANES, LANES), BF16)


def _in_proj(x2d, gat, w_in_r, tabs, qg_a, kg_a, qg_b, kvg_b, wuq_p, wukv_p, B, S):
    T = B * S
    tm = TM_IN
    nst = S // tm
    const = lambda shape: pl.BlockSpec(shape, lambda i: (0,) * len(shape))
    tab = pl.BlockSpec((tm, LANES), lambda i: (i % nst, 0))
    hm = lambda H: pl.BlockSpec((1, H, tm, LANES), lambda i: (i // nst, 0, i % nst, 0))
    hmt = lambda H: pl.BlockSpec((1, H, LANES, tm), lambda i: (i // nst, 0, 0, i % nst))
    sds = lambda H: jax.ShapeDtypeStruct((B, H, S, LANES), BF16)
    sdt = lambda H: jax.ShapeDtypeStruct((B, H, LANES, S), BF16)
    return pl.pallas_call(
        _in_kernel,
        grid=(T // tm,),
        in_specs=[pl.BlockSpec((tm, D_MODEL), lambda i: (i, 0)),
                  const((1, D_MODEL)), const((D_MODEL, IN_COLS)),
                  tab, tab, tab, tab,
                  const((1, LANES)), const((1, LANES)), const((1, CQ_PAD)), const((1, MLA_KV_RANK)),
                  const((CQ_PAD, N_HEADS_B * LANES)), const((MLA_KV_RANK, 2 * N_HEADS_B * LANES)),
                  const((NRM_GROUPS * LANES, LANES))],
        out_specs=[hm(N_HEADS_A // 2), hmt(2 * N_KV_A), hm(N_KV_A),
                   hm(N_HEADS_B), hmt(N_HEADS_B), hm(N_HEADS_B),
                   hm(N_HEADS_C // 2),
                   pl.BlockSpec((1, 2 * N_HEADS_C, 1, LANES, tm),
                                lambda i: (i // nst, 0, i % nst, 0, 0)),
                   hm(N_HEADS_C),
                   pl.BlockSpec((1, 1, LANES), lambda i: (i, 0, 0))],
        out_shape=[sds(N_HEADS_A // 2), sdt(2 * N_KV_A), sds(N_KV_A),
                   sds(N_HEADS_B), sdt(N_HEADS_B), sds(N_HEADS_B),
                   sds(N_HEADS_C // 2),
                   jax.ShapeDtypeStruct((B, 2 * N_HEADS_C, S // tm, LANES, tm), BF16),
                   sds(N_HEADS_C),
                   jax.ShapeDtypeStruct((T // tm, 1, LANES), F32)],
        compiler_params=pltpu.CompilerParams(dimension_semantics=("arbitrary",),
                                             vmem_limit_bytes=VMEM_LIMIT_BYTES),
        name="in_proj",
    )(x2d, gat, w_in_r, *tabs, qg_a, kg_a, qg_b, kvg_b, wuq_p, wukv_p, _norm_routing())


def _lane_group_max(s):
    m = s[:, :LANES]
    for j in range(1, s.shape[1] // LANES):
        m = jnp.maximum(m, s[:, j * LANES:(j + 1) * LANES])
    return m


def _pv_bounded(streams, n_chunks):
    items = [(r, i) for r in range(n_chunks) for i in range(len(streams))]
    accs = [None] * len(streams)
    in_flight = {}
    for k in range(len(items) + SCORE_LOOKAHEAD):
        if k < len(items):
            r, i = items[k]
            scores, _, row_off = streams[i]
            s, off = scores(r), row_off(r)
            in_flight[k] = s if off is None else s + off
        if k >= SCORE_LOOKAHEAD:
            r, i = items[k - SCORE_LOOKAHEAD]
            d = _dot(jnp.exp2(in_flight.pop(k - SCORE_LOOKAHEAD)).astype(BF16), streams[i][1](r))
            accs[i] = d if accs[i] is None else accs[i] + d
    return accs


def _pv_exact(streams, n_chunks, s_ref):
    rows = s_ref.shape[0] // len(streams)
    maxes = []
    for i, (scores, _, row_off) in enumerate(streams):
        m_run = None
        for r in range(n_chunks):
            s = scores(r)
            s_ref[i * rows:(i + 1) * rows, r * TK:(r + 1) * TK] = s
            m, off = _lane_group_max(s), row_off(r)
            if off is not None:
                m = m + off
            m_run = m if m_run is None else jnp.maximum(m_run, m)
        maxes.append(jnp.max(m_run, axis=-1, keepdims=True))
    accs = []
    for i, (_, values, row_off) in enumerate(streams):
        acc = None
        for r in range(n_chunks):
            off = row_off(r)
            shift = maxes[i] if off is None else maxes[i] - off
            p = jnp.exp2(s_ref[i * rows:(i + 1) * rows, r * TK:(r + 1) * TK] - shift)
            d = _dot(p.astype(BF16), values(r))
            acc = d if acc is None else acc + d
        accs.append(acc)
    return accs


def _run_heads(bounded, heads, head_streams, n_chunks, s_ref, acc_ref, roll_exact=False):
    @pl.when(bounded)
    def _():
        per_head = [head_streams(h) for h in heads]
        n_sub = len(per_head[0])
        accs = _pv_bounded([s for streams in per_head for s in streams], n_chunks)
        for k, h in enumerate(heads):
            acc_ref[h] = jnp.concatenate(accs[k * n_sub:(k + 1) * n_sub], axis=0)

    @pl.when(jnp.logical_not(bounded))
    def _():
        def one_head(h):
            acc_ref[h] = jnp.concatenate(_pv_exact(head_streams(h), n_chunks, s_ref), axis=0)

        if roll_exact:
            lax.fori_loop(0, len(heads), lambda k, c: (one_head(heads[0] + k), c)[1], 0)
        else:
            for h in heads:
                one_head(h)


def _normalised_pair(acc_low, acc_high):
    low = _low_lanes()
    num = jnp.where(low, acc_low, acc_high)
    den = pltpu.roll(jnp.where(low, acc_high, acc_low), HALF, axis=1)
    return num / den


def _attn_ab_kernel(bounded_ref, q_ref, kt_ref, v_ref, o_ref, s_ref, acc_ref, *,
                    n_heads, per_pipeline, q_index, kt_index, v_index, out_pairs, n_keys):
    n_chunks = n_keys // TK

    def head_streams(h):
        q = q_ref[0, q_index(h)]
        return [(lambda r, qs=q[r0:r0 + TQ_SUB_AB]:
                 _dot(qs, kt_ref[0, kt_index(h), :, r * TK:(r + 1) * TK]),
                 lambda r: v_ref[0, v_index(h), r * TK:(r + 1) * TK, :],
                 lambda r: None) for r0 in range(0, TQ, TQ_SUB_AB)]

    def head_group(j, carry):
        heads = [per_pipeline * j + k for k in range(per_pipeline)]
        _run_heads(bounded_ref[pl.program_id(0), j] != 0, heads, head_streams, n_chunks, s_ref,
                   acc_ref, roll_exact=True)
        return carry

    if n_heads == per_pipeline:
        head_group(0, None)
    else:
        lax.fori_loop(0, n_heads // per_pipeline, head_group, 0)
    o_ref[...] = jnp.concatenate([_normalised_pair(acc_ref[a], acc_ref[b]) for a, b in out_pairs],
                                 axis=-1).astype(BF16)


def _attn_c_kernel(bounded_ref, q_ref, kt_ref, v_ref, lq1_ref, lk1_ref, lq2_ref, lk2_ref,
                   subln_ref, o_ref, s_ref, acc_ref, *, n_keys, lam_init):
    assert TQ == TK
    nc = n_keys // TK
    cd = pl.program_id(1)
    col = lax.broadcasted_iota(jnp.int32, (1, TK), 1).astype(F32)
    row = lax.broadcasted_iota(jnp.int32, (TQ_SUB_C, 1), 0).astype(F32)
    subs = [(r, r + TQ_SUB_C) for r in range(0, TQ, TQ_SUB_C)]
    neg_dist = [-jnp.abs((lax.broadcasted_iota(jnp.int32, (TQ_SUB_C, TK), 0) + r0
                          - lax.broadcasted_iota(jnp.int32, (TQ_SUB_C, TK), 1)).astype(F32))
                for r0, _ in subs]

    chunk = [(cd + r) % nc for r in range(nc)]

    def head_bias(hh):
        slope = jnp.float32(LOG2E * 2.0 ** (-8.0 * N_HEADS_C / N_HEADS_C))
        for k in range(N_HEADS_C - 2, -1, -1):
            slope = jnp.where(hh == k, LOG2E * 2.0 ** (-8.0 * (k + 1) / N_HEADS_C), slope)
        key_term, row_off = [None], [[None] * len(subs)]
        for r in range(1, nc):
            side = jnp.where(chunk[r] < cd, slope, -slope)
            key_term.append(col * side)
            base = -slope * (jnp.abs(chunk[r] - cd) * TK).astype(F32)
            row_off.append([base - side * (row + float(r0)) for r0, _ in subs])
        return slope, key_term, row_off

    def head_pair(p, carry):
        bias = {}

        def streams_of(k):
            idx, hh = 4 * p + k, 2 * p + k // 2
            if not isinstance(k, int):
                slope, key_term, row_off = head_bias(hh)
            else:
                if k // 2 not in bias:
                    bias[k // 2] = head_bias(hh)
                slope, key_term, row_off = bias[k // 2]

            def scores(r, si):
                s = _dot(q_ref[0, p][subs[si][0]:subs[si][1]], kt_ref[0, idx, chunk[r]])
                return s + (neg_dist[si] * slope if r == 0 else key_term[r])

            return [(functools.partial(scores, si=si),
                     lambda r: v_ref[0, hh, pl.ds(pl.multiple_of(chunk[r] * TK, TK), TK), :],
                     lambda r, si=si: row_off[r][si]) for si in range(len(subs))]

        _run_heads(bounded_ref[pl.program_id(0), p] != 0, list(range(4)), streams_of, nc, s_ref,
                   acc_ref.at[pl.ds(4 * p, 4)], roll_exact=True)
        return carry

    assert HEADS_PER_PIPELINE == 4
    lax.fori_loop(0, N_HEADS_C // 2, head_pair, 0)

    lam = (jnp.exp(jnp.sum(lq1_ref[...] * lk1_ref[...], axis=-1, keepdims=True))
           - jnp.exp(jnp.sum(lq2_ref[...] * lk2_ref[...], axis=-1, keepdims=True)) + lam_init)
    outs = []
    for p in range(N_HEADS_C // 2):
        o = (_normalised_pair(acc_ref[4 * p], acc_ref[4 * p + 2])
             - lam * _normalised_pair(acc_ref[4 * p + 1], acc_ref[4 * p + 3]))
        outs.append(_half_rms(o) * subln_ref[...] * (1.0 - lam_init))
    o_ref[...] = jnp.concatenate(outs, axis=-1).astype(BF16)


def _attention(kernel_fn, bounded, n_acc, q, kt, v, extras, out_width, B, S):
    nq = S // TQ
    hq, hv = q.shape[1], v.shape[1]
    extra_specs = [pl.BlockSpec(e.shape, lambda b, i, f, n=e.ndim: (0,) * n) for e in extras]
    return pl.pallas_call(
        kernel_fn,
        grid_spec=pltpu.PrefetchScalarGridSpec(
            num_scalar_prefetch=1,
            grid=(B, nq),
            in_specs=[pl.BlockSpec((1, hq, TQ, LANES), lambda b, i, f: (b, 0, i, 0)),
                      pl.BlockSpec((1,) + kt.shape[1:],
                                   lambda b, i, f, n=kt.ndim: (b,) + (0,) * (n - 1)),
                      pl.BlockSpec((1, hv, S, LANES), lambda b, i, f: (b, 0, 0, 0))] + extra_specs,
            out_specs=pl.BlockSpec((TQ, out_width), lambda b, i, f: (b * nq + i, 0)),
            scratch_shapes=[pltpu.VMEM((TQ, S), F32),
                            pltpu.VMEM((n_acc, TQ, LANES), F32)]),
        out_shape=jax.ShapeDtypeStruct((B * S, out_width), BF16),
        compiler_params=pltpu.CompilerParams(dimension_semantics=("arbitrary", "arbitrary"),
                                             vmem_limit_bytes=VMEM_LIMIT_BYTES),
        name=kernel_fn.func.__name__.strip("_"),
    )(bounded, q, kt, v, *extras)


def _mlp_kernel(xp_ref, x_ref, xn_ref, ap_ref, a_ref, an_ref, bp_ref, b_ref, bn_ref,
                cp_ref, c_ref, cn_ref, wo_ref, g_ref, wu_ref, cw_ref, cb_ref, wd_ref, fg_ref,
                o_ref, act_ref, *, tiles_per_seq, final_norm):
    i = pl.program_id(0)
    tm = x_ref.shape[0]
    halo = xp_ref.shape[0]

    def rows(p, m, n):
        return jnp.concatenate([p[...], m[...], n[...]], axis=0)

    o_ext = jnp.concatenate([rows(ap_ref, a_ref, an_ref), rows(bp_ref, b_ref, bn_ref),
                             rows(cp_ref, c_ref, cn_ref)], axis=1)
    x1 = rows(xp_ref, x_ref, xn_ref) + _dot(o_ext, wo_ref[...])
    h = (_rms(x1, D_MODEL) * g_ref[...]).astype(BF16)
    at_start = (i % tiles_per_seq) == 0
    at_end = (i % tiles_per_seq) == tiles_per_seq - 1
    zeros = jnp.zeros((halo, D_MODEL), BF16)
    hext = jnp.concatenate([jnp.where(at_start, zeros, h[:halo]), h[halo:halo + tm],
                            jnp.where(at_end, zeros, h[halo + tm:])], axis=0)

    def branch(k, c0):
        u = _dot(hext, wu_ref[:, k * D_FF + c0:k * D_FF + c0 + FFN_CHUNK])
        cw = cw_ref[k, :, c0:c0 + FFN_CHUNK]
        n = u.shape[0]
        prev = pltpu.roll(u, 1, axis=0)[halo:halo + tm]
        nxt = pltpu.roll(u, n - 1, axis=0)[halo:halo + tm]
        return (prev * cw[0:1] + u[halo:halo + tm] * cw[1:2] + nxt * cw[2:3]
                + cb_ref[k, :, c0:c0 + FFN_CHUNK])

    y = x1[halo:halo + tm]
    done = 0
    for c0 in range(0, D_FF, FFN_CHUNK):
        g = branch(0, c0)
        val = branch(1, c0)
        act_ref[:, c0:c0 + FFN_CHUNK] = (g / (1.0 + jnp.exp(-g)) * val).astype(BF16)
        ready = c0 + FFN_CHUNK - FFN_DOWN_LAG * FFN_CHUNK
        if ready - done >= FFN_DOWN_GROUP:
            y = y + _dot(act_ref[:, done:done + FFN_DOWN_GROUP], wd_ref[done:done + FFN_DOWN_GROUP])
            done += FFN_DOWN_GROUP
    y = y + _dot(act_ref[:, done:], wd_ref[done:])
    if final_norm:
        y = _rms(y, D_MODEL) * fg_ref[...]
    o_ref[...] = y


def _mlp(x2d, oa, ob, oc, w_out, g_ffn, w_up, conv_w, conv_b, w_down, final_g, S, final_norm):
    T = x2d.shape[0]
    tm, halo = TM_FFN, BF16_SUBLANES
    per = tm // halo
    last = T // halo - 1
    resident = lambda shape: pl.BlockSpec(shape, lambda i: (0,) * len(shape),
                                          pipeline_mode=pl.Buffered(1))

    def with_halos(width):
        return [pl.BlockSpec((halo, width), lambda i: (jnp.maximum(i * per - 1, 0), 0)),
                pl.BlockSpec((tm, width), lambda i: (i, 0)),
                pl.BlockSpec((halo, width), lambda i: (jnp.minimum((i + 1) * per, last), 0))]

    return pl.pallas_call(
        functools.partial(_mlp_kernel, tiles_per_seq=S // tm, final_norm=final_norm),
        grid=(T // tm,),
        in_specs=(with_halos(D_MODEL) + with_halos(oa.shape[1]) + with_halos(ob.shape[1])
                  + with_halos(oc.shape[1])
                  + [resident(w_out.shape), resident((1, D_MODEL)),
                     resident((D_MODEL, 2 * D_FF)), resident((2, 3, D_FF)), resident((2, 1, D_FF)),
                     resident((D_FF, D_MODEL)), resident((1, D_MODEL))]),
        out_specs=pl.BlockSpec((tm, D_MODEL), lambda i: (i, 0)),
        out_shape=jax.ShapeDtypeStruct((T, D_MODEL), F32),
        scratch_shapes=[pltpu.VMEM((tm, D_FF), BF16)],
        compiler_params=pltpu.CompilerParams(dimension_semantics=("arbitrary",),
                                             vmem_limit_bytes=VMEM_LIMIT_BYTES),
        name="out_proj_conv_mlp",
    )(x2d, x2d, x2d, oa, oa, oa, ob, ob, ob, oc, oc, oc, w_out, g_ffn,
      w_up, conv_w, conv_b, w_down, final_g)


def _pad_groups(w, n_groups, width, offset=0):
    r = w.shape[0]
    w = w.reshape(r, n_groups, width)
    w = jnp.pad(w, ((0, 0), (0, 0), (offset, LANES - width - offset)))
    return w.reshape(r, n_groups * LANES)


def _rope_core(S, dim):
    nf = dim // 4
    t = jnp.arange(S, dtype=jnp.int32)
    rows = (t // GRID_W).astype(F32)
    cols = (t % GRID_W).astype(F32)
    inv = ROPE_BASE ** (-jnp.arange(nf, dtype=F32) / nf)
    ar = rows[:, None] * inv
    ac = cols[:, None] * inv
    cos = jnp.concatenate([jnp.cos(ar), jnp.cos(ar), jnp.cos(ac), jnp.cos(ac)], axis=-1)
    sin = jnp.concatenate([-jnp.sin(ar), jnp.sin(ar), -jnp.sin(ac), jnp.sin(ac)], axis=-1)
    return cos, sin


def _rope_tables(S):
    cos_a, sin_a = _rope_core(S, HEAD_DIM)
    cos_b, sin_b = _rope_core(S, MLA_ROPE)
    pad = LANES - MLA_NOPE - MLA_ROPE
    return (jnp.tile(cos_a, (1, 2)), jnp.tile(sin_a, (1, 2)),
            jnp.concatenate([jnp.ones((S, MLA_NOPE), F32), cos_b, jnp.zeros((S, pad), F32)], -1),
            jnp.concatenate([jnp.zeros((S, MLA_NOPE), F32), sin_b, jnp.zeros((S, pad), F32)], -1))


def _prep_w_in(w):
    a_end = (N_HEADS_A + 2 * N_KV_A) * HEAD_DIM
    b_end = a_end + MLA_Q_RANK + MLA_KV_RANK + MLA_ROPE
    cq = w[:, a_end:a_end + MLA_Q_RANK]
    ckv = w[:, a_end + MLA_Q_RANK:a_end + MLA_Q_RANK + MLA_KV_RANK]
    kr = w[:, b_end - MLA_ROPE:b_end]
    out = jnp.concatenate([
        w[:, :a_end],
        jnp.pad(cq, ((0, 0), (0, CQ_PAD - MLA_Q_RANK))),
        ckv,
        jnp.pad(kr, ((0, 0), (MLA_NOPE, LANES - MLA_NOPE - MLA_ROPE))),
        w[:, b_end:],
    ], axis=-1).astype(BF16)
    assert out.shape[1] == IN_COLS
    return out


def _twice(v):
    return jnp.tile(v, 2).reshape(1, 2 * v.shape[0])


def kernel(x, norm_attn, w_in, q_norm_a, k_norm_a, q_a_norm_b, w_uq_b, kv_a_norm_b, w_ukv_b,
           lambda_q1_c, lambda_k1_c, lambda_q2_c, lambda_k2_c, subln_c, w_out,
           norm_ffn, w_up, conv_w, conv_b, w_down, final_norm):
    B, S, D = x.shape
    depth = w_in.shape[0]
    assert D == D_MODEL and S % max(TM_IN, TM_FFN, TQ, TK) == 0 and S % GRID_W == 0
    assert TM_IN == TK
    T = B * S

    tabs = _rope_tables(S)
    row = lambda v: v.reshape(1, -1)
    a_pairs = [(j, j + N_HEADS_A // 2) for j in range(N_HEADS_A // 2)]
    a_rows = np.concatenate([np.arange(h * HEAD_DIM, (h + 1) * HEAD_DIM)
                             for pair in a_pairs for h in pair])
    xc = x.reshape(T, D)
    for l in range(depth):
        lam_init = 0.8 - 0.6 * math.exp(-0.3 * l)
        wuq_p = jnp.pad(_pad_groups(w_uq_b[l], N_HEADS_B, MLA_NOPE + MLA_ROPE),
                        ((0, CQ_PAD - MLA_Q_RANK), (0, 0))).astype(BF16)
        wukv = w_ukv_b[l].reshape(MLA_KV_RANK, N_HEADS_B, MLA_NOPE + MLA_V)
        v_cols = [_pad_groups(wukv[:, h, MLA_NOPE:], 1, MLA_V, offset=HALF * (h % 2))
                  for h in range(N_HEADS_B)]
        wukv_p = jnp.concatenate(
            [_pad_groups(wukv[:, :, :MLA_NOPE].reshape(MLA_KV_RANK, -1), N_HEADS_B, MLA_NOPE)]
            + v_cols, axis=-1).astype(BF16)

        qa, kta, va, qb, ktb, vb, qc, ktc, vc, nrm = _in_proj(
            xc, row(norm_attn[l]), _prep_w_in(w_in[l]), tabs,
            _twice(q_norm_a[l]), _twice(k_norm_a[l]),
            jnp.pad(q_a_norm_b[l], (0, CQ_PAD - MLA_Q_RANK)).reshape(1, CQ_PAD),
            row(kv_a_norm_b[l]), wuq_p, wukv_p, B, S)

        nrm = jnp.max(nrm.reshape(B, -1, LANES), axis=1)

        def bounded(q0, nq, q_rep, k0, nk, k_rep, per_pipeline):
            q2 = jnp.repeat(nrm[:, q0:q0 + nq], q_rep, axis=1)
            k2 = jnp.repeat(nrm[:, k0:k0 + nk], k_rep, axis=1)
            ok = q2 * k2 * NRM_MARGIN <= SAFE_LOG2_RANGE ** 2
            return jnp.all(ok.reshape(B, -1, per_pipeline), axis=-1).astype(jnp.int32)

        qk_a = (HEAD_DIM * jnp.max(jnp.abs(q_norm_a[l])) * jnp.max(jnp.abs(k_norm_a[l]))
                * (HEAD_DIM ** -0.5 * LOG2E))
        bounded_a = jnp.broadcast_to(
            (qk_a * NRM_MARGIN <= SAFE_LOG2_RANGE).astype(jnp.int32),
            (B, N_HEADS_A // HEADS_PER_PIPELINE_AB))

        group = N_HEADS_A // N_KV_A
        oa = _attention(
            functools.partial(_attn_ab_kernel, n_heads=N_HEADS_A,
                              per_pipeline=HEADS_PER_PIPELINE_AB, q_index=lambda h: h // 2,
                              kt_index=lambda h: 2 * (h // group) + h % 2,
                              v_index=lambda h: h // group, out_pairs=a_pairs, n_keys=S),
            bounded_a, N_HEADS_A, qa, kta, va, [], N_HEADS_A * HEAD_DIM, B, S)
        same = lambda h: h
        ob = _attention(
            functools.partial(_attn_ab_kernel, n_heads=N_HEADS_B,
                              per_pipeline=HEADS_PER_PIPELINE_AB, q_index=same, kt_index=same,
                              v_index=same,
                              out_pairs=[(2 * j, 2 * j + 1) for j in range(N_HEADS_B // 2)],
                              n_keys=S),
            bounded(NRM_QB, N_HEADS_B, 1, NRM_KB, N_HEADS_B, 1, HEADS_PER_PIPELINE_AB), N_HEADS_B,
            qb, ktb, vb, [], N_HEADS_B * MLA_V, B, S)
        oc = _attention(
            functools.partial(_attn_c_kernel, n_keys=S, lam_init=lam_init),
            bounded(NRM_QC, N_HEADS_C, 2, NRM_KC, 2 * N_HEADS_C, 1, HEADS_PER_PIPELINE),
            2 * N_HEADS_C, qc, ktc, vc,
            [row(lambda_q1_c[l]), row(lambda_k1_c[l]), row(lambda_q2_c[l]), row(lambda_k2_c[l]),
             _twice(subln_c[l])],
            N_HEADS_C * DIFF_V, B, S)

        w_out_l = jnp.concatenate([w_out[l][a_rows], w_out[l][N_HEADS_A * HEAD_DIM:]], axis=0)
        xc = _mlp(xc, oa, ob, oc, w_out_l.astype(BF16), row(norm_ffn[l]), w_up[l].astype(BF16),
                  conv_w[l].reshape(3, 2, D_FF).transpose(1, 0, 2),
                  conv_b[l].reshape(2, 1, D_FF), w_down[l].astype(BF16),
                  row(final_norm), S, final_norm=(l == depth - 1))
    return xc.reshape(B, S, D)
```

```python
import functools
import math

import jax
import jax.numpy as jnp
import numpy as np
from jax import lax
from jax.experimental import pallas as pl
from jax.experimental.pallas import tpu as pltpu

D_MODEL = 1024
GRID_W = 64
HEAD_DIM = 64
N_HEADS_A = 8
N_KV_A = 2
N_HEADS_B = 4
MLA_Q_RANK = 192
MLA_KV_RANK = 128
MLA_NOPE = 64
MLA_ROPE = 32
MLA_V = 64
N_HEADS_C = 4
DIFF_QK = 32
DIFF_V = 64
D_FF = 2816
ROPE_BASE = 10000.0
EPS = 1e-6

LANES = 128
HALF = LANES // 2
A_QUARTER = LANES // 4
BF16_SUBLANES = 16
VMEM_LIMIT_BYTES = 56 * 1024 * 1024

LOG2E = math.log2(math.e)
assert HEAD_DIM == MLA_V == DIFF_V == 2 * DIFF_QK == HALF

OFF_QA = 0
OFF_KA = OFF_QA + N_HEADS_A * HEAD_DIM
OFF_VA = OFF_KA + LANES
OFF_CQ = OFF_VA + LANES
CQ_PAD = 2 * LANES
OFF_CKV = OFF_CQ + CQ_PAD
OFF_KR = OFF_CKV + MLA_KV_RANK
OFF_QC = OFF_KR + LANES
OFF_KC = OFF_QC + N_HEADS_C * HALF
OFF_VC = OFF_KC + N_HEADS_C * HALF
IN_COLS = OFF_VC + N_HEADS_C * HALF

NRM_QB = 0
NRM_KB = NRM_QB + N_HEADS_B
NRM_QC = NRM_KB + N_HEADS_B
NRM_KC = NRM_QC + N_HEADS_C
NRM_GROUPS = 2 * N_HEADS_B + N_HEADS_C
NRM_MARGIN = 1.03
SAFE_LOG2_RANGE = 55.0

F32 = jnp.float32
BF16 = jnp.bfloat16

TM_IN = 512
TM_FFN = 512
FFN_CHUNK = 256
FFN_DOWN_GROUP = 6 * FFN_CHUNK
FFN_DOWN_LAG = 2
TQ = 512
TQ_SUB_AB = 512
TQ_SUB_C = 256
TK = 512
SCORE_LOOKAHEAD = 2
HEADS_PER_PIPELINE = 4
HEADS_PER_PIPELINE_AB = 4


def _dot(a, b):
    return jnp.dot(a, b, preferred_element_type=F32)


def _rms(x, n):
    ms = jnp.sum(x * x, axis=-1, keepdims=True) * (1.0 / n)
    return x * lax.rsqrt(ms + EPS)


def _low_lanes():
    return lax.broadcasted_iota(jnp.int32, (1, LANES), 1) < HALF


def _half_rms(x, low=None):
    low = _low_lanes() if low is None else low
    sq = x * x
    s_lo = jnp.sum(jnp.where(low, sq, 0.0), axis=-1, keepdims=True)
    s_hi = jnp.sum(jnp.where(low, 0.0, sq), axis=-1, keepdims=True)
    return x * lax.rsqrt(jnp.where(low, s_lo, s_hi) * (1.0 / HALF) + EPS)


def _in_kernel(x_ref, gat_ref, w_ref, cos_a_ref, sin_a_ref, cos_b_ref, sin_b_ref,
               qg_a_ref, kg_a_ref, qg_b_ref, kvg_b_ref, wuq_ref, wukv_ref, sel_ref,
               qa_ref, kta_ref, va_ref, qb_ref, ktb_ref, vb_ref, qc_ref, ktc_ref, vc_ref, nrm_ref):
    x = x_ref[...]
    tm = x.shape[0]
    h = (_rms(x, D_MODEL) * gat_ref[...]).astype(BF16)

    lane = lax.broadcasted_iota(jnp.int32, (1, LANES), 1)
    low = lane < HALF
    row = lax.broadcasted_iota(jnp.int32, (LANES, 1), 0)

    z = _dot(h, w_ref[...])

    def seg(off, width):
        return z[:, off:off + width]

    def rope(y, c, s):
        return y * c + pltpu.roll(y, HALF, axis=1) * s

    def group(z, g):
        return z[:, g * LANES:(g + 1) * LANES]

    squares = []

    def note(y):
        squares.append((y * y).astype(BF16))

    def noted_row_sums(first):
        return _dot(jnp.concatenate(squares[first:], axis=-1),
                    sel_ref[first * LANES:len(squares) * LANES])

    cos_b, sin_b = cos_b_ref[...], sin_b_ref[...]
    cq =(_rms(seg(OFF_CQ, CQ_PAD), MLA_Q_RANK) * qg_b_ref[...]).astype(BF16)
    qb = _dot(cq, wuq_ref[...])
    scale_b = (MLA_NOPE + MLA_ROPE) ** -0.5 * LOG2E
    for hh in range(N_HEADS_B):
        y = rope(group(qb, hh), cos_b, sin_b) * scale_b
        qb_ref[0, hh] = y.astype(BF16)
        note(y)
    ckv = (_rms(seg(OFF_CKV, MLA_KV_RANK), MLA_KV_RANK) * kvg_b_ref[...]).astype(BF16)
    kvb = _dot(ckv, wukv_ref[...])
    kr = rope(seg(OFF_KR, LANES), cos_b, sin_b)
    for hh in range(N_HEADS_B):
        y = group(kvb, hh) + kr
        ktb_ref[0, hh] = y.T.astype(BF16)
        note(y)
        ones_other_half = jnp.where(low, float(hh % 2), float(1 - hh % 2))
        vb_ref[0, hh] = (group(kvb, N_HEADS_B + hh) + ones_other_half).astype(BF16)
    n2 = noted_row_sums(0)
    n_noted_b = len(squares)

    zq = seg(OFF_QC, N_HEADS_C * HALF)
    zk = seg(OFF_KC, N_HEADS_C * HALF)
    zv = seg(OFF_VC, N_HEADS_C * HALF)
    k_groups = []
    for p in range(N_HEADS_C // 2):
        y = group(zq, p) * (DIFF_QK ** -0.5 * LOG2E)
        qc_ref[0, p] = y.astype(BF16)
        note(y)
        k_groups.append(group(zk, p))
        kt = group(zk, p).T
        for sub in range(4):
            mine = (row >= sub * DIFF_QK) & (row < (sub + 1) * DIFF_QK)
            ktc_ref[0, 4 * p + sub, 0] = jnp.where(mine, kt, 0.0).astype(BF16)
        vz = group(zv, p)
        vc_ref[0, 2 * p] = jnp.where(low, vz, 1.0).astype(BF16)
        vc_ref[0, 2 * p + 1] = jnp.where(low, 1.0, vz).astype(BF16)
    for kz in k_groups:
        note(kz)
    n2 = n2 + noted_row_sums(n_noted_b)
    nrm_ref[0] = jnp.max(n2, axis=0, keepdims=True)

    cos_a, sin_a = cos_a_ref[...], sin_a_ref[...]
    first_head = (lane & A_QUARTER) == 0
    zq = seg(OFF_QA, N_HEADS_A * HEAD_DIM)
    for p in range(N_HEADS_A // 2):
        y = rope(_half_rms(group(zq, p), first_head) * qg_a_ref[...], cos_a, sin_a)
        qa_ref[0, p] = (y * (HEAD_DIM ** -0.5 * LOG2E)).astype(BF16)
    y = rope(_half_rms(seg(OFF_KA, LANES), first_head) * kg_a_ref[...], cos_a, sin_a)
    kt = y.T.astype(BF16)
    quarter = [kt[j * A_QUARTER:(j + 1) * A_QUARTER] for j in range(4)]
    zeros = jnp.zeros((A_QUARTER, tm), BF16)
    for g in range(N_KV_A):
        k1, k2 = quarter[g], quarter[2 + g]
        kta_ref[0, 2 * g] = jnp.concatenate([k1, zeros, k2, zeros], axis=0)
        kta_ref[0, 2 * g + 1] = jnp.concatenate([zeros, k1, zeros, k2], axis=0)
    zv = seg(OFF_VA, LANES)
    va_ref[0, 0] = jnp.where(low, zv, 1.0).astype(BF16)
    va_ref[0, 1] = jnp.where(low, 1.0, zv).astype(BF16)


def _norm_routing():
    sel = np.zeros((NRM_GROUPS, LANES, LANES), np.float32)
    g = 0
    for slot in ([NRM_QB + h for h in range(N_HEADS_B)] + [NRM_KB + h for h in range(N_HEADS_B)]):
        sel[g, :, slot] = 1.0
        g += 1
    for p in range(N_HEADS_C // 2):
        for f in range(2):
            sel[g, f * HALF:(f + 1) * HALF, NRM_QC + 2 * p + f] = 1.0
        g += 1
    for p in range(N_HEADS_C // 2):
        for sub in range(4):
            sel[g, sub * DIFF_QK:(sub + 1) * DIFF_QK, NRM_KC + 4 * p + sub] = 1.0
        g += 1
    assert g == NRM_GROUPS
    return jnp.asarray(sel.reshape(NRM_GROUPS * LANES, LANES), BF16)


def _in_proj(x2d, gat, w_in_r, tabs, qg_a, kg_a, qg_b, kvg_b, wuq_p, wukv_p, B, S):
    T = B * S
    tm = TM_IN
    nst = S // tm
    const = lambda shape: pl.BlockSpec(shape, lambda i: (0,) * len(shape))
    tab = pl.BlockSpec((tm, LANES), lambda i: (i % nst, 0))
    hm = lambda H: pl.BlockSpec((1, H, tm, LANES), lambda i: (i // nst, 0, i % nst, 0))
    hmt = lambda H: pl.BlockSpec((1, H, LANES, tm), lambda i: (i // nst, 0, 0, i % nst))
    sds = lambda H: jax.ShapeDtypeStruct((B, H, S, LANES), BF16)
    sdt = lambda H: jax.ShapeDtypeStruct((B, H, LANES, S), BF16)
    return pl.pallas_call(
        _in_kernel,
        grid=(T // tm,),
        in_specs=[pl.BlockSpec((tm, D_MODEL), lambda i: (i, 0)),
                  const((1, D_MODEL)), const((D_MODEL, IN_COLS)),
                  tab, tab, tab, tab,
                  const((1, LANES)), const((1, LANES)), const((1, CQ_PAD)), const((1, MLA_KV_RANK)),
                  const((CQ_PAD, N_HEADS_B * LANES)), const((MLA_KV_RANK, 2 * N_HEADS_B * LANES)),
                  const((NRM_GROUPS * LANES, LANES))],
        out_specs=[hm(N_HEADS_A // 2), hmt(2 * N_KV_A), hm(N_KV_A),
                   hm(N_HEADS_B), hmt(N_HEADS_B), hm(N_HEADS_B),
                   hm(N_HEADS_C // 2),
                   pl.BlockSpec((1, 2 * N_HEADS_C, 1, LANES, tm),
                                lambda i: (i // nst, 0, i % nst, 0, 0)),
                   hm(N_HEADS_C),
                   pl.BlockSpec((1, 1, LANES), lambda i: (i, 0, 0))],
        out_shape=[sds(N_HEADS_A // 2), sdt(2 * N_KV_A), sds(N_KV_A),
                   sds(N_HEADS_B), sdt(N_HEADS_B), sds(N_HEADS_B),
                   sds(N_HEADS_C // 2),
                   jax.ShapeDtypeStruct((B, 2 * N_HEADS_C, S // tm, LANES, tm), BF16),
                   sds(N_HEADS_C),
                   jax.ShapeDtypeStruct((T // tm, 1, LANES), F32)],
        compiler_params=pltpu.CompilerParams(dimension_semantics=("arbitrary",),
                                             vmem_limit_bytes=VMEM_LIMIT_BYTES),
        name="in_proj",
    )(x2d, gat, w_in_r, *tabs, qg_a, kg_a, qg_b, kvg_b, wuq_p, wukv_p, _norm_routing())


def _lane_group_max(s):
    m = s[:, :LANES]
    for j in range(1, s.shape[1] // LANES):
        m = jnp.maximum(m, s[:, j * LANES:(j + 1) * LANES])
    return m


def _pv_bounded(streams, n_chunks):
    items = [(r, i) for r in range(n_chunks) for i in range(len(streams))]
    accs = [None] * len(streams)
    in_flight = {}
    for k in range(len(items) + SCORE_LOOKAHEAD):
        if k < len(items):
            r, i = items[k]
            scores, _, row_off = streams[i]
            s, off = scores(r), row_off(r)
            in_flight[k] = s if off is None else s + off
        if k >= SCORE_LOOKAHEAD:
            r, i = items[k - SCORE_LOOKAHEAD]
            d = _dot(jnp.exp2(in_flight.pop(k - SCORE_LOOKAHEAD)).astype(BF16), streams[i][1](r))
            accs[i] = d if accs[i] is None else accs[i] + d
    return accs


def _pv_exact(streams, n_chunks, s_ref):
    rows = s_ref.shape[0] // len(streams)
    maxes = []
    for i, (scores, _, row_off) in enumerate(streams):
        m_run = None
        for r in range(n_chunks):
            s = scores(r)
            s_ref[i * rows:(i + 1) * rows, r * TK:(r + 1) * TK] = s
            m, off = _lane_group_max(s), row_off(r)
            if off is not None:
                m = m + off
            m_run = m if m_run is None else jnp.maximum(m_run, m)
        maxes.append(jnp.max(m_run, axis=-1, keepdims=True))
    accs = []
    for i, (_, values, row_off) in enumerate(streams):
        acc = None
        for r in range(n_chunks):
            off = row_off(r)
            shift = maxes[i] if off is None else maxes[i] - off
            p = jnp.exp2(s_ref[i * rows:(i + 1) * rows, r * TK:(r + 1) * TK] - shift)
            d = _dot(p.astype(BF16), values(r))
            acc = d if acc is None else acc + d
        accs.append(acc)
    return accs


def _run_heads(bounded, heads, head_streams, n_chunks, s_ref, acc_ref, roll_exact=False):
    @pl.when(bounded)
    def _():
        per_head = [head_streams(h) for h in heads]
        n_sub = len(per_head[0])
        accs = _pv_bounded([s for streams in per_head for s in streams], n_chunks)
        for k, h in enumerate(heads):
            acc_ref[h] = jnp.concatenate(accs[k * n_sub:(k + 1) * n_sub], axis=0)

    @pl.when(jnp.logical_not(bounded))
    def _():
        def one_head(h):
            acc_ref[h] = jnp.concatenate(_pv_exact(head_streams(h), n_chunks, s_ref), axis=0)

        if roll_exact:
            lax.fori_loop(0, len(heads), lambda k, c: (one_head(heads[0] + k), c)[1], 0)
        else:
            for h in heads:
                one_head(h)


def _normalised_pair(acc_low, acc_high):
    low = _low_lanes()
    num = jnp.where(low, acc_low, acc_high)
    den = pltpu.roll(jnp.where(low, acc_high, acc_low), HALF, axis=1)
    return num / den


def _attn_ab_kernel(bounded_ref, q_ref, kt_ref, v_ref, o_ref, s_ref, acc_ref, *,
                    n_heads, per_pipeline, q_index, kt_index, v_index, out_pairs, n_keys):
    n_chunks = n_keys // TK

    def head_streams(h):
        q = q_ref[0, q_index(h)]
        return [(lambda r, qs=q[r0:r0 + TQ_SUB_AB]:
                 _dot(qs, kt_ref[0, kt_index(h), :, r * TK:(r + 1) * TK]),
                 lambda r: v_ref[0, v_index(h), r * TK:(r + 1) * TK, :],
                 lambda r: None) for r0 in range(0, TQ, TQ_SUB_AB)]

    def head_group(j, carry):
        heads = [per_pipeline * j + k for k in range(per_pipeline)]
        _run_heads(bounded_ref[pl.program_id(0), j] != 0, heads, head_streams, n_chunks, s_ref,
                   acc_ref, roll_exact=True)
        return carry

    if n_heads == per_pipeline:
        head_group(0, None)
    else:
        lax.fori_loop(0, n_heads // per_pipeline, head_group, 0)
    o_ref[...] = jnp.concatenate([_normalised_pair(acc_ref[a], acc_ref[b]) for a, b in out_pairs],
                                 axis=-1).astype(BF16)


def _attn_c_kernel(bounded_ref, q_ref, kt_ref, v_ref, lq1_ref, lk1_ref, lq2_ref, lk2_ref,
                   subln_ref, o_ref, s_ref, acc_ref, *, n_keys, lam_init):
    assert TQ == TK
    nc = n_keys // TK
    cd = pl.program_id(1)
    col = lax.broadcasted_iota(jnp.int32, (1, TK), 1).astype(F32)
    row = lax.broadcasted_iota(jnp.int32, (TQ_SUB_C, 1), 0).astype(F32)
    subs = [(r, r + TQ_SUB_C) for r in range(0, TQ, TQ_SUB_C)]
    neg_dist = [-jnp.abs((lax.broadcasted_iota(jnp.int32, (TQ_SUB_C, TK), 0) + r0
                          - lax.broadcasted_iota(jnp.int32, (TQ_SUB_C, TK), 1)).astype(F32))
                for r0, _ in subs]

    chunk = [(cd + r) % nc for r in range(nc)]

    def head_bias(hh):
        slope = jnp.float32(LOG2E * 2.0 ** (-8.0 * N_HEADS_C / N_HEADS_C))
        for k in range(N_HEADS_C - 2, -1, -1):
            slope = jnp.where(hh == k, LOG2E * 2.0 ** (-8.0 * (k + 1) / N_HEADS_C), slope)
        key_term, row_off = [None], [[None] * len(subs)]
        for r in range(1, nc):
            side = jnp.where(chunk[r] < cd, slope, -slope)
            key_term.append(col * side)
            base = -slope * (jnp.abs(chunk[r] - cd) * TK).astype(F32)
            row_off.append([base - side * (row + float(r0)) for r0, _ in subs])
        return slope, key_term, row_off

    def head_pair(p, carry):
        bias = {}

        def streams_of(k):
            idx, hh = 4 * p + k, 2 * p + k // 2
            if not isinstance(k, int):
                slope, key_term, row_off = head_bias(hh)
            else:
                if k // 2 not in bias:
                    bias[k // 2] = head_bias(hh)
                slope, key_term, row_off = bias[k // 2]

            def scores(r, si):
                s = _dot(q_ref[0, p][subs[si][0]:subs[si][1]], kt_ref[0, idx, chunk[r]])
                return s + (neg_dist[si] * slope if r == 0 else key_term[r])

            return [(functools.partial(scores, si=si),
                     lambda r: v_ref[0, hh, pl.ds(pl.multiple_of(chunk[r] * TK, TK), TK), :],
                     lambda r, si=si: row_off[r][si]) for si in range(len(subs))]

        _run_heads(bounded_ref[pl.program_id(0), p] != 0, list(range(4)), streams_of, nc, s_ref,
                   acc_ref.at[pl.ds(4 * p, 4)], roll_exact=True)
        return carry

    assert HEADS_PER_PIPELINE == 4
    lax.fori_loop(0, N_HEADS_C // 2, head_pair, 0)

    lam = (jnp.exp(jnp.sum(lq1_ref[...] * lk1_ref[...], axis=-1, keepdims=True))
           - jnp.exp(jnp.sum(lq2_ref[...] * lk2_ref[...], axis=-1, keepdims=True)) + lam_init)
    outs = []
    for p in range(N_HEADS_C // 2):
        o = (_normalised_pair(acc_ref[4 * p], acc_ref[4 * p + 2])
             - lam * _normalised_pair(acc_ref[4 * p + 1], acc_ref[4 * p + 3]))
        outs.append(_half_rms(o) * subln_ref[...] * (1.0 - lam_init))
    o_ref[...] = jnp.concatenate(outs, axis=-1).astype(BF16)


def _attention(kernel_fn, bounded, n_acc, q, kt, v, extras, out_width, B, S):
    nq = S // TQ
    hq, hv = q.shape[1], v.shape[1]
    extra_specs = [pl.BlockSpec(e.shape, lambda b, i, f, n=e.ndim: (0,) * n) for e in extras]
    return pl.pallas_call(
        kernel_fn,
        grid_spec=pltpu.PrefetchScalarGridSpec(
            num_scalar_prefetch=1,
            grid=(B, nq),
            in_specs=[pl.BlockSpec((1, hq, TQ, LANES), lambda b, i, f: (b, 0, i, 0)),
                      pl.BlockSpec((1,) + kt.shape[1:],
                                   lambda b, i, f, n=kt.ndim: (b,) + (0,) * (n - 1)),
                      pl.BlockSpec((1, hv, S, LANES), lambda b, i, f: (b, 0, 0, 0))] + extra_specs,
            out_specs=pl.BlockSpec((TQ, out_width), lambda b, i, f: (b * nq + i, 0)),
            scratch_shapes=[pltpu.VMEM((TQ, S), F32),
                            pltpu.VMEM((n_acc, TQ, LANES), F32)]),
        out_shape=jax.ShapeDtypeStruct((B * S, out_width), BF16),
        compiler_params=pltpu.CompilerParams(dimension_semantics=("arbitrary", "arbitrary"),
                                             vmem_limit_bytes=VMEM_LIMIT_BYTES),
        name=kernel_fn.func.__name__.strip("_"),
    )(bounded, q, kt, v, *extras)


def _mlp_kernel(xp_ref, x_ref, xn_ref, ap_ref, a_ref, an_ref, bp_ref, b_ref, bn_ref,
                cp_ref, c_ref, cn_ref, wo_ref, g_ref, wu_ref, cw_ref, cb_ref, wd_ref, fg_ref,
                o_ref, act_ref, *, tiles_per_seq, final_norm):
    i = pl.program_id(0)
    tm = x_ref.shape[0]
    halo = xp_ref.shape[0]

    def rows(p, m, n):
        return jnp.concatenate([p[...], m[...], n[...]], axis=0)

    o_ext = jnp.concatenate([rows(ap_ref, a_ref, an_ref), rows(bp_ref, b_ref, bn_ref),
                             rows(cp_ref, c_ref, cn_ref)], axis=1)
    x1 = rows(xp_ref, x_ref, xn_ref) + _dot(o_ext, wo_ref[...])
    h = (_rms(x1, D_MODEL) * g_ref[...]).astype(BF16)
    at_start = (i % tiles_per_seq) == 0
    at_end = (i % tiles_per_seq) == tiles_per_seq - 1
    zeros = jnp.zeros((halo, D_MODEL), BF16)
    hext = jnp.concatenate([jnp.where(at_start, zeros, h[:halo]), h[halo:halo + tm],
                            jnp.where(at_end, zeros, h[halo + tm:])], axis=0)

    def branch(k, c0):
        u = _dot(hext, wu_ref[:, k * D_FF + c0:k * D_FF + c0 + FFN_CHUNK])
        cw = cw_ref[k, :, c0:c0 + FFN_CHUNK]
        n = u.shape[0]
        prev = pltpu.roll(u, 1, axis=0)[halo:halo + tm]
        nxt = pltpu.roll(u, n - 1, axis=0)[halo:halo + tm]
        return (prev * cw[0:1] + u[halo:halo + tm] * cw[1:2] + nxt * cw[2:3]
                + cb_ref[k, :, c0:c0 + FFN_CHUNK])

    y = x1[halo:halo + tm]
    done = 0
    for c0 in range(0, D_FF, FFN_CHUNK):
        g = branch(0, c0)
        val = branch(1, c0)
        act_ref[:, c0:c0 + FFN_CHUNK] = (g / (1.0 + jnp.exp(-g)) * val).astype(BF16)
        ready = c0 + FFN_CHUNK - FFN_DOWN_LAG * FFN_CHUNK
        if ready - done >= FFN_DOWN_GROUP:
            y = y + _dot(act_ref[:, done:done + FFN_DOWN_GROUP], wd_ref[done:done + FFN_DOWN_GROUP])
            done += FFN_DOWN_GROUP
    y = y + _dot(act_ref[:, done:], wd_ref[done:])
    if final_norm:
        y = _rms(y, D_MODEL) * fg_ref[...]
    o_ref[...] = y


def _mlp(x2d, oa, ob, oc, w_out, g_ffn, w_up, conv_w, conv_b, w_down, final_g, S, final_norm):
    T = x2d.shape[0]
    tm, halo = TM_FFN, BF16_SUBLANES
    per = tm // halo
    last = T // halo - 1
    resident = lambda shape: pl.BlockSpec(shape, lambda i: (0,) * len(shape),
                                          pipeline_mode=pl.Buffered(1))

    def with_halos(width):
        return [pl.BlockSpec((halo, width), lambda i: (jnp.maximum(i * per - 1, 0), 0)),
                pl.BlockSpec((tm, width), lambda i: (i, 0)),
                pl.BlockSpec((halo, width), lambda i: (jnp.minimum((i + 1) * per, last), 0))]

    return pl.pallas_call(
        functools.partial(_mlp_kernel, tiles_per_seq=S // tm, final_norm=final_norm),
        grid=(T // tm,),
        in_specs=(with_halos(D_MODEL) + with_halos(oa.shape[1]) + with_halos(ob.shape[1])
                  + with_halos(oc.shape[1])
                  + [resident(w_out.shape), resident((1, D_MODEL)),
                     resident((D_MODEL, 2 * D_FF)), resident((2, 3, D_FF)), resident((2, 1, D_FF)),
                     resident((D_FF, D_MODEL)), resident((1, D_MODEL))]),
        out_specs=pl.BlockSpec((tm, D_MODEL), lambda i: (i, 0)),
        out_shape=jax.ShapeDtypeStruct((T, D_MODEL), F32),
        scratch_shapes=[pltpu.VMEM((tm, D_FF), BF16)],
        compiler_params=pltpu.CompilerParams(dimension_semantics=("arbitrary",),
                                             vmem_limit_bytes=VMEM_LIMIT_BYTES),
        name="out_proj_conv_mlp",
    )(x2d, x2d, x2d, oa, oa, oa, ob, ob, ob, oc, oc, oc, w_out, g_ffn,
      w_up, conv_w, conv_b, w_down, final_g)


def _pad_groups(w, n_groups, width, offset=0):
    r = w.shape[0]
    w = w.reshape(r, n_groups, width)
    w = jnp.pad(w, ((0, 0), (0, 0), (offset, LANES - width - offset)))
    return w.reshape(r, n_groups * LANES)


def _group_a_lane_order():
    nf = HEAD_DIM // 4
    lanes = np.arange(LANES)
    second, r = lanes // HALF, lanes % HALF
    head, rr = r // A_QUARTER, r % A_QUARTER
    block, freq = rr // nf, rr % nf
    dim = block * 2 * nf + second * nf + freq
    return head, dim, block, freq, second


def _group_b_lane_order():
    nf = MLA_ROPE // 4
    src = np.full(LANES, -1)
    block = np.zeros(LANES, np.int64)
    freq = np.zeros(LANES, np.int64)
    second = np.zeros(LANES, np.int64)
    is_rope = np.zeros(LANES, bool)
    for e in range(2):
        for j in range(2 * nf):
            lane = e * HALF + j
            is_rope[lane], block[lane], freq[lane], second[lane] = True, j // nf, j % nf, e
            src[lane] = MLA_NOPE + (j // nf) * 2 * nf + e * nf + j % nf
    n_low = HALF - 2 * nf
    src[2 * nf:HALF] = np.arange(n_low)
    src[HALF + 2 * nf:HALF + 2 * nf + MLA_NOPE - n_low] = np.arange(n_low, MLA_NOPE)
    return src, is_rope, block, freq, second


def _gather_cols(w, src):
    return jnp.where(jnp.asarray(src >= 0), w[:, np.maximum(src, 0)], 0.0)


def _rope_tables(S):
    t = jnp.arange(S, dtype=jnp.int32)
    pos = jnp.stack([(t // GRID_W).astype(F32), (t % GRID_W).astype(F32)], axis=0)

    def table(nf, block, freq, second, is_rope):
        inv = ROPE_BASE ** (-jnp.arange(nf, dtype=F32) / nf)
        ang = pos[block].T * inv[freq]
        sign = jnp.asarray(2.0 * second - 1.0, F32)
        cos = jnp.where(jnp.asarray(is_rope), jnp.cos(ang), 1.0)
        sin = jnp.where(jnp.asarray(is_rope), jnp.sin(ang) * sign, 0.0)
        return cos, sin

    _, _, block, freq, second = _group_a_lane_order()
    cos_a, sin_a = table(HEAD_DIM // 4, block, freq, second, np.ones(LANES, bool))
    _, is_rope, block, freq, second = _group_b_lane_order()
    cos_b, sin_b = table(MLA_ROPE // 4, block, freq, second, is_rope)
    return cos_a, sin_a, cos_b, sin_b


def _prep_w_in(w):
    n_qa, n_ka = N_HEADS_A * HEAD_DIM, N_KV_A * HEAD_DIM
    a_end = n_qa + 2 * n_ka
    b_end = a_end + MLA_Q_RANK + MLA_KV_RANK + MLA_ROPE
    head, dim, _, _, _ = _group_a_lane_order()
    pair_cols = head * HEAD_DIM + dim
    qa_cols = np.concatenate([p * LANES + pair_cols for p in range(N_HEADS_A // 2)])
    cq = w[:, a_end:a_end + MLA_Q_RANK]
    ckv = w[:, a_end + MLA_Q_RANK:a_end + MLA_Q_RANK + MLA_KV_RANK]
    kr = w[:, b_end - MLA_ROPE:b_end]
    src_b, is_rope, _, _, _ = _group_b_lane_order()
    out = jnp.concatenate([
        w[:, qa_cols], w[:, n_qa + pair_cols], w[:, n_qa + n_ka:a_end],
        jnp.pad(cq, ((0, 0), (0, CQ_PAD - MLA_Q_RANK))),
        ckv,
        _gather_cols(kr, np.where(is_rope, src_b - MLA_NOPE, -1)),
        w[:, b_end:],
    ], axis=-1).astype(BF16)
    assert out.shape[1] == IN_COLS
    return out


def _twice(v):
    return jnp.tile(v, 2).reshape(1, 2 * v.shape[0])


def kernel(x, norm_attn, w_in, q_norm_a, k_norm_a, q_a_norm_b, w_uq_b, kv_a_norm_b, w_ukv_b,
           lambda_q1_c, lambda_k1_c, lambda_q2_c, lambda_k2_c, subln_c, w_out,
           norm_ffn, w_up, conv_w, conv_b, w_down, final_norm):
    B, S, D = x.shape
    depth = w_in.shape[0]
    assert D == D_MODEL and S % max(TM_IN, TM_FFN, TQ, TK) == 0 and S % GRID_W == 0
    assert TM_IN == TK
    T = B * S

    tabs = _rope_tables(S)
    row = lambda v: v.reshape(1, -1)
    a_pairs = [(j, j + N_HEADS_A // 2) for j in range(N_HEADS_A // 2)]
    a_rows = np.concatenate([np.arange(h * HEAD_DIM, (h + 1) * HEAD_DIM)
                             for pair in a_pairs for h in pair])
    xc = x.reshape(T, D)
    for l in range(depth):
        lam_init = 0.8 - 0.6 * math.exp(-0.3 * l)
        src_b, is_rope, _, _, _ = _group_b_lane_order()
        d_qk = MLA_NOPE + MLA_ROPE
        wuq_p = jnp.pad(
            jnp.concatenate([_gather_cols(w_uq_b[l][:, h * d_qk:(h + 1) * d_qk], src_b)
                             for h in range(N_HEADS_B)], axis=-1),
            ((0, CQ_PAD - MLA_Q_RANK), (0, 0))).astype(BF16)
        wukv = w_ukv_b[l].reshape(MLA_KV_RANK, N_HEADS_B, MLA_NOPE + MLA_V)
        k_cols = [_gather_cols(wukv[:, h, :MLA_NOPE], np.where(is_rope, -1, src_b))
                  for h in range(N_HEADS_B)]
        v_cols = [_pad_groups(wukv[:, h, MLA_NOPE:], 1, MLA_V, offset=HALF * (h % 2))
                  for h in range(N_HEADS_B)]
        wukv_p = jnp.concatenate(k_cols + v_cols, axis=-1).astype(BF16)

        qa, kta, va, qb, ktb, vb, qc, ktc, vc, nrm = _in_proj(
            xc, row(norm_attn[l]), _prep_w_in(w_in[l]), tabs,
            q_norm_a[l][_group_a_lane_order()[1]].reshape(1, LANES),
            k_norm_a[l][_group_a_lane_order()[1]].reshape(1, LANES),
            jnp.pad(q_a_norm_b[l], (0, CQ_PAD - MLA_Q_RANK)).reshape(1, CQ_PAD),
            row(kv_a_norm_b[l]), wuq_p, wukv_p, B, S)

        nrm = jnp.max(nrm.reshape(B, -1, LANES), axis=1)

        def bounded(q0, nq, q_rep, k0, nk, k_rep, per_pipeline):
            q2 = jnp.repeat(nrm[:, q0:q0 + nq], q_rep, axis=1)
            k2 = jnp.repeat(nrm[:, k0:k0 + nk], k_rep, axis=1)
            ok = q2 * k2 * NRM_MARGIN <= SAFE_LOG2_RANGE ** 2
            return jnp.all(ok.reshape(B, -1, per_pipeline), axis=-1).astype(jnp.int32)

        qk_a = (HEAD_DIM * jnp.max(jnp.abs(q_norm_a[l])) * jnp.max(jnp.abs(k_norm_a[l]))
                * (HEAD_DIM ** -0.5 * LOG2E))
        bounded_a = jnp.broadcast_to(
            (qk_a * NRM_MARGIN <= SAFE_LOG2_RANGE).astype(jnp.int32),
            (B, N_HEADS_A // HEADS_PER_PIPELINE_AB))

        group = N_HEADS_A // N_KV_A
        oa = _attention(
            functools.partial(_attn_ab_kernel, n_heads=N_HEADS_A,
                              per_pipeline=HEADS_PER_PIPELINE_AB, q_index=lambda h: h // 2,
                              kt_index=lambda h: 2 * (h // group) + h % 2,
                              v_index=lambda h: h // group, out_pairs=a_pairs, n_keys=S),
            bounded_a, N_HEADS_A, qa, kta, va, [], N_HEADS_A * HEAD_DIM, B, S)
        same = lambda h: h
        ob = _attention(
            functools.partial(_attn_ab_kernel, n_heads=N_HEADS_B,
                              per_pipeline=HEADS_PER_PIPELINE_AB, q_index=same, kt_index=same,
                              v_index=same,
                              out_pairs=[(2 * j, 2 * j + 1) for j in range(N_HEADS_B // 2)],
                              n_keys=S),
            bounded(NRM_QB, N_HEADS_B, 1, NRM_KB, N_HEADS_B, 1, HEADS_PER_PIPELINE_AB), N_HEADS_B,
            qb, ktb, vb, [], N_HEADS_B * MLA_V, B, S)
        oc = _attention(
            functools.partial(_attn_c_kernel, n_keys=S, lam_init=lam_init),
            bounded(NRM_QC, N_HEADS_C, 2, NRM_KC, 2 * N_HEADS_C, 1, HEADS_PER_PIPELINE),
            2 * N_HEADS_C, qc, ktc, vc,
            [row(lambda_q1_c[l]), row(lambda_k1_c[l]), row(lambda_q2_c[l]), row(lambda_k2_c[l]),
             _twice(subln_c[l])],
            N_HEADS_C * DIFF_V, B, S)

        w_out_l = jnp.concatenate([w_out[l][a_rows], w_out[l][N_HEADS_A * HEAD_DIM:]], axis=0)
        xc = _mlp(xc, oa, ob, oc, w_out_l.astype(BF16), row(norm_ffn[l]), w_up[l].astype(BF16),
                  conv_w[l].reshape(3, 2, D_FF).transpose(1, 0, 2),
                  conv_b[l].reshape(2, 1, D_FF), w_down[l].astype(BF16),
                  row(final_norm), S, final_norm=(l == depth - 1))
    return xc.reshape(B, S, D)
```

```python
import functools
import math

import jax
import jax.numpy as jnp
import numpy as np
from jax import lax
from jax.experimental import pallas as pl
from jax.experimental.pallas import tpu as pltpu

D_MODEL = 1024
GRID_W = 64
HEAD_DIM = 64
N_HEADS_A = 8
N_KV_A = 2
N_HEADS_B = 4
MLA_Q_RANK = 192
MLA_KV_RANK = 128
MLA_NOPE = 64
MLA_ROPE = 32
MLA_V = 64
N_HEADS_C = 4
DIFF_QK = 32
DIFF_V = 64
D_FF = 2816
ROPE_BASE = 10000.0
EPS = 1e-6

LANES = 128
HALF = LANES // 2
A_QUARTER = LANES // 4
BF16_SUBLANES = 16
VMEM_LIMIT_BYTES = 56 * 1024 * 1024

LOG2E = math.log2(math.e)
assert HEAD_DIM == MLA_V == DIFF_V == 2 * DIFF_QK == HALF

OFF_QA = 0
OFF_KA = OFF_QA + N_HEADS_A * HEAD_DIM
OFF_VA = OFF_KA + LANES
OFF_CQ = OFF_VA + LANES
CQ_PAD = 2 * LANES
OFF_CKV = OFF_CQ + CQ_PAD
OFF_KR = OFF_CKV + MLA_KV_RANK
OFF_QC = OFF_KR + LANES
OFF_KC = OFF_QC + N_HEADS_C * HALF
OFF_VC = OFF_KC + N_HEADS_C * HALF
IN_COLS = OFF_VC + N_HEADS_C * HALF

NRM_QB = 0
NRM_KB = NRM_QB + N_HEADS_B
NRM_QC = NRM_KB + N_HEADS_B
NRM_KC = NRM_QC + N_HEADS_C
NRM_GROUPS = 2 * N_HEADS_B + N_HEADS_C
NRM_MARGIN = 1.03
SAFE_LOG2_RANGE = 55.0

F32 = jnp.float32
BF16 = jnp.bfloat16

TM_IN = 512
TM_FFN = 512
FFN_CHUNK = 256
FFN_DOWN_GROUP = 6 * FFN_CHUNK
FFN_DOWN_LAG = 2
TQ = 512
TQ_SUB_AB = 512
TQ_SUB_C = 256
TK = 512
SCORE_LOOKAHEAD = 2
HEADS_PER_PIPELINE = 4
HEADS_PER_PIPELINE_AB = 4


def _dot(a, b):
    return jnp.dot(a, b, preferred_element_type=F32)


def _rms(x, n):
    ms = jnp.sum(x * x, axis=-1, keepdims=True) * (1.0 / n)
    return x * lax.rsqrt(ms + EPS)


def _low_lanes():
    return lax.broadcasted_iota(jnp.int32, (1, LANES), 1) < HALF


def _half_rms(x, low=None):
    low = _low_lanes() if low is None else low
    sq = x * x
    s_lo = jnp.sum(jnp.where(low, sq, 0.0), axis=-1, keepdims=True)
    s_hi = jnp.sum(jnp.where(low, 0.0, sq), axis=-1, keepdims=True)
    return x * lax.rsqrt(jnp.where(low, s_lo, s_hi) * (1.0 / HALF) + EPS)


def _in_kernel(x_ref, gat_ref, w_ref, cos_a_ref, sin_a_ref, cos_b_ref, sin_b_ref,
               qg_a_ref, kg_a_ref, qg_b_ref, kvg_b_ref, wuq_ref, wukv_ref, sel_ref,
               qa_ref, kta_ref, va_ref, qb_ref, ktb_ref, vb_ref, qc_ref, ktc_ref, vc_ref, nrm_ref):
    x = x_ref[...]
    tm = x.shape[0]
    h = (_rms(x, D_MODEL) * gat_ref[...]).astype(BF16)

    lane = lax.broadcasted_iota(jnp.int32, (1, LANES), 1)
    low = lane < HALF
    row = lax.broadcasted_iota(jnp.int32, (LANES, 1), 0)

    z = _dot(h, w_ref[...])

    def seg(off, width):
        return z[:, off:off + width]

    def rope(y, c, s):
        return y * c + pltpu.roll(y, HALF, axis=1) * s

    def group(z, g):
        return z[:, g * LANES:(g + 1) * LANES]

    squares = []

    def note(y):
        squares.append((y * y).astype(BF16))

    def noted_row_sums(first):
        return _dot(jnp.concatenate(squares[first:], axis=-1),
                    sel_ref[first * LANES:len(squares) * LANES])

    cos_b, sin_b = cos_b_ref[...], sin_b_ref[...]
    cq =(_rms(seg(OFF_CQ, CQ_PAD), MLA_Q_RANK) * qg_b_ref[...]).astype(BF16)
    qb = _dot(cq, wuq_ref[...])
    scale_b = (MLA_NOPE + MLA_ROPE) ** -0.5 * LOG2E
    for hh in range(N_HEADS_B):
        y = rope(group(qb, hh), cos_b, sin_b) * scale_b
        qb_ref[0, hh] = y.astype(BF16)
        note(y)
    ckv = (_rms(seg(OFF_CKV, MLA_KV_RANK), MLA_KV_RANK) * kvg_b_ref[...]).astype(BF16)
    kvb = _dot(ckv, wukv_ref[...])
    kr = rope(seg(OFF_KR, LANES), cos_b, sin_b)
    for hh in range(N_HEADS_B):
        y = group(kvb, hh) + kr
        ktb_ref[0, hh] = y.T.astype(BF16)
        note(y)
        ones_other_half = jnp.where(low, float(hh % 2), float(1 - hh % 2))
        vb_ref[0, hh] = (group(kvb, N_HEADS_B + hh) + ones_other_half).astype(BF16)
    n2 = noted_row_sums(0)
    n_noted_b = len(squares)

    zq = seg(OFF_QC, N_HEADS_C * HALF)
    zk = seg(OFF_KC, N_HEADS_C * HALF)
    zv = seg(OFF_VC, N_HEADS_C * HALF)
    k_groups = []
    for p in range(N_HEADS_C // 2):
        y = group(zq, p) * (DIFF_QK ** -0.5 * LOG2E)
        qc_ref[0, p] = y.astype(BF16)
        note(y)
        k_groups.append(group(zk, p))
        kt = group(zk, p).T
        for sub in range(4):
            mine = (row >= sub * DIFF_QK) & (row < (sub + 1) * DIFF_QK)
            ktc_ref[0, 4 * p + sub, 0] = jnp.where(mine, kt, 0.0).astype(BF16)
        vz = group(zv, p)
        vc_ref[0, 2 * p] = jnp.where(low, vz, 1.0).astype(BF16)
        vc_ref[0, 2 * p + 1] = jnp.where(low, 1.0, vz).astype(BF16)
    for kz in k_groups:
        note(kz)
    n2 = n2 + noted_row_sums(n_noted_b)
    nrm_ref[0] = jnp.max(n2, axis=0, keepdims=True)

    cos_a, sin_a = cos_a_ref[...], sin_a_ref[...]
    first_head = (lane & A_QUARTER) == 0
    zq = seg(OFF_QA, N_HEADS_A * HEAD_DIM)
    for p in range(N_HEADS_A // 2):
        y = rope(_half_rms(group(zq, p), first_head) * qg_a_ref[...], cos_a, sin_a)
        qa_ref[0, p] = (y * (HEAD_DIM ** -0.5 * LOG2E)).astype(BF16)
    y = rope(_half_rms(seg(OFF_KA, LANES), first_head) * kg_a_ref[...], cos_a, sin_a)
    kt = y.T.astype(BF16)
    quarter = [kt[j * A_QUARTER:(j + 1) * A_QUARTER] for j in range(4)]
    zeros = jnp.zeros((A_QUARTER, tm), BF16)
    for g in range(N_KV_A):
        k1, k2 = quarter[g], quarter[2 + g]
        kta_ref[0, 2 * g] = jnp.concatenate([k1, zeros, k2, zeros], axis=0)
        kta_ref[0, 2 * g + 1] = jnp.concatenate([zeros, k1, zeros, k2], axis=0)
    zv = seg(OFF_VA, LANES)
    va_ref[0, 0] = jnp.where(low, zv, 1.0).astype(BF16)
    va_ref[0, 1] = jnp.where(low, 1.0, zv).astype(BF16)


def _norm_routing():
    sel = np.zeros((NRM_GROUPS, LANES, LANES), np.float32)
    g = 0
    for slot in ([NRM_QB + h for h in range(N_HEADS_B)] + [NRM_KB + h for h in range(N_HEADS_B)]):
        sel[g, :, slot] = 1.0
        g += 1
    for p in range(N_HEADS_C // 2):
        for f in range(2):
            sel[g, f * HALF:(f + 1) * HALF, NRM_QC + 2 * p + f] = 1.0
        g += 1
    for p in range(N_HEADS_C // 2):
        for sub in range(4):
            sel[g, sub * DIFF_QK:(sub + 1) * DIFF_QK, NRM_KC + 4 * p + sub] = 1.0
        g += 1
    assert g == NRM_GROUPS
    return jnp.asarray(sel.reshape(NRM_GROUPS * LANES, LANES), BF16)


def _in_proj(x2d, gat, w_in_r, tabs, qg_a, kg_a, qg_b, kvg_b, wuq_p, wukv_p, B, S):
    T = B * S
    tm = TM_IN
    nst = S // tm
    const = lambda shape: pl.BlockSpec(shape, lambda i: (0,) * len(shape))
    tab = pl.BlockSpec((tm, LANES), lambda i: (i % nst, 0))
    hm = lambda H: pl.BlockSpec((1, H, tm, LANES), lambda i: (i // nst, 0, i % nst, 0))
    hmt = lambda H: pl.BlockSpec((1, H, LANES, tm), lambda i: (i // nst, 0, 0, i % nst))
    sds = lambda H: jax.ShapeDtypeStruct((B, H, S, LANES), BF16)
    sdt = lambda H: jax.ShapeDtypeStruct((B, H, LANES, S), BF16)
    return pl.pallas_call(
        _in_kernel,
        grid=(T // tm,),
        in_specs=[pl.BlockSpec((tm, D_MODEL), lambda i: (i, 0)),
                  const((1, D_MODEL)), const((D_MODEL, IN_COLS)),
                  tab, tab, tab, tab,
                  const((1, LANES)), const((1, LANES)), const((1, CQ_PAD)), const((1, MLA_KV_RANK)),
                  const((CQ_PAD, N_HEADS_B * LANES)), const((MLA_KV_RANK, 2 * N_HEADS_B * LANES)),
                  const((NRM_GROUPS * LANES, LANES))],
        out_specs=[hm(N_HEADS_A // 2), hmt(2 * N_KV_A), hm(N_KV_A),
                   hm(N_HEADS_B), hmt(N_HEADS_B), hm(N_HEADS_B),
                   hm(N_HEADS_C // 2),
                   pl.BlockSpec((1, 2 * N_HEADS_C, 1, LANES, tm),
                                lambda i: (i // nst, 0, i % nst, 0, 0)),
                   hm(N_HEADS_C),
                   pl.BlockSpec((1, 1, LANES), lambda i: (i, 0, 0))],
        out_shape=[sds(N_HEADS_A // 2), sdt(2 * N_KV_A), sds(N_KV_A),
                   sds(N_HEADS_B), sdt(N_HEADS_B), sds(N_HEADS_B),
                   sds(N_HEADS_C // 2),
                   jax.ShapeDtypeStruct((B, 2 * N_HEADS_C, S // tm, LANES, tm), BF16),
                   sds(N_HEADS_C),
                   jax.ShapeDtypeStruct((T // tm, 1, LANES), F32)],
        compiler_params=pltpu.CompilerParams(dimension_semantics=("arbitrary",),
                                             vmem_limit_bytes=VMEM_LIMIT_BYTES),
        name="in_proj",
    )(x2d, gat, w_in_r, *tabs, qg_a, kg_a, qg_b, kvg_b, wuq_p, wukv_p, _norm_routing())


def _lane_group_max(s):
    m = s[:, :LANES]
    for j in range(1, s.shape[1] // LANES):
        m = jnp.maximum(m, s[:, j * LANES:(j + 1) * LANES])
    return m


def _pv_bounded(streams, n_chunks):
    items = [(r, i) for r in range(n_chunks) for i in range(len(streams))]
    accs = [None] * len(streams)
    in_flight = {}
    for k in range(len(items) + SCORE_LOOKAHEAD):
        if k < len(items):
            r, i = items[k]
            scores, _, row_off = streams[i]
            s, off = scores(r), row_off(r)
            in_flight[k] = s if off is None else s + off
        if k >= SCORE_LOOKAHEAD:
            r, i = items[k - SCORE_LOOKAHEAD]
            d = _dot(jnp.exp2(in_flight.pop(k - SCORE_LOOKAHEAD)).astype(BF16), streams[i][1](r))
            accs[i] = d if accs[i] is None else accs[i] + d
    return accs


def _pv_exact(streams, n_chunks, s_ref):
    rows = s_ref.shape[0] // len(streams)
    maxes = []
    for i, (scores, _, row_off) in enumerate(streams):
        m_run = None
        for r in range(n_chunks):
            s = scores(r)
            s_ref[i * rows:(i + 1) * rows, r * TK:(r + 1) * TK] = s
            m, off = _lane_group_max(s), row_off(r)
            if off is not None:
                m = m + off
            m_run = m if m_run is None else jnp.maximum(m_run, m)
        maxes.append(jnp.max(m_run, axis=-1, keepdims=True))
    accs = []
    for i, (_, values, row_off) in enumerate(streams):
        acc = None
        for r in range(n_chunks):
            off = row_off(r)
            shift = maxes[i] if off is None else maxes[i] - off
            p = jnp.exp2(s_ref[i * rows:(i + 1) * rows, r * TK:(r + 1) * TK] - shift)
            d = _dot(p.astype(BF16), values(r))
            acc = d if acc is None else acc + d
        accs.append(acc)
    return accs


def _run_heads(bounded, heads, head_streams, n_chunks, s_ref, acc_ref, roll_exact=False):
    @pl.when(bounded)
    def _():
        shared = {}
        per_head = [head_streams(h, shared) for h in heads]
        n_sub = len(per_head[0])
        accs = _pv_bounded([s for streams in per_head for s in streams], n_chunks)
        for k, h in enumerate(heads):
            acc_ref[h] = jnp.concatenate(accs[k * n_sub:(k + 1) * n_sub], axis=0)

    @pl.when(jnp.logical_not(bounded))
    def _():
        def one_head(h):
            acc_ref[h] = jnp.concatenate(_pv_exact(head_streams(h, {}), n_chunks, s_ref), axis=0)

        if roll_exact:
            lax.fori_loop(0, len(heads), lambda k, c: (one_head(heads[0] + k), c)[1], 0)
        else:
            for h in heads:
                one_head(h)


def _normalised_pair(acc_low, acc_high):
    low = _low_lanes()
    num = jnp.where(low, acc_low, acc_high)
    den = pltpu.roll(jnp.where(low, acc_high, acc_low), HALF, axis=1)
    return num / den


def _attn_ab_kernel(bounded_ref, q_ref, kt_ref, v_ref, o_ref, s_ref, acc_ref, *,
                    n_heads, per_pipeline, q_index, kt_index, v_index, out_pairs, n_keys):
    n_chunks = n_keys // TK

    def head_streams(h, shared):
        q = q_ref[0, q_index(h)]
        return [(lambda r, qs=q[r0:r0 + TQ_SUB_AB]:
                 _dot(qs, kt_ref[0, kt_index(h), :, r * TK:(r + 1) * TK]),
                 lambda r: v_ref[0, v_index(h), r * TK:(r + 1) * TK, :],
                 lambda r: None) for r0 in range(0, TQ, TQ_SUB_AB)]

    def head_group(j, carry):
        heads = [per_pipeline * j + k for k in range(per_pipeline)]
        _run_heads(bounded_ref[pl.program_id(0), j] != 0, heads, head_streams, n_chunks, s_ref,
                   acc_ref, roll_exact=True)
        return carry

    if n_heads == per_pipeline:
        head_group(0, None)
    else:
        lax.fori_loop(0, n_heads // per_pipeline, head_group, 0)
    o_ref[...] = jnp.concatenate([_normalised_pair(acc_ref[a], acc_ref[b]) for a, b in out_pairs],
                                 axis=-1).astype(BF16)


def _attn_c_kernel(bounded_ref, q_ref, kt_ref, v_ref, lq1_ref, lk1_ref, lq2_ref, lk2_ref,
                   subln_ref, o_ref, s_ref, acc_ref, *, n_keys, lam_init):
    assert TQ == TK
    nc = n_keys // TK
    cd = pl.program_id(1)
    col = lax.broadcasted_iota(jnp.int32, (1, TK), 1).astype(F32)
    row = lax.broadcasted_iota(jnp.int32, (TQ_SUB_C, 1), 0).astype(F32)
    subs = [(r, r + TQ_SUB_C) for r in range(0, TQ, TQ_SUB_C)]
    neg_dist = [-jnp.abs((lax.broadcasted_iota(jnp.int32, (TQ_SUB_C, TK), 0) + r0
                          - lax.broadcasted_iota(jnp.int32, (TQ_SUB_C, TK), 1)).astype(F32))
                for r0, _ in subs]

    chunk = [(cd + r) % nc for r in range(nc)]

    def head_bias(hh):
        slope = jnp.float32(LOG2E * 2.0 ** (-8.0 * N_HEADS_C / N_HEADS_C))
        for k in range(N_HEADS_C - 2, -1, -1):
            slope = jnp.where(hh == k, LOG2E * 2.0 ** (-8.0 * (k + 1) / N_HEADS_C), slope)
        key_term, row_off = [None], [[None] * len(subs)]
        for r in range(1, nc):
            side = jnp.where(chunk[r] < cd, slope, -slope)
            key_term.append(col * side)
            base = -slope * (jnp.abs(chunk[r] - cd) * TK).astype(F32)
            row_off.append([base - side * (row + float(r0)) for r0, _ in subs])
        return slope, key_term, row_off

    def head_pair(p, carry):
        def streams_of(k, shared):
            idx, hh = 4 * p + k, 2 * p + k // 2
            if isinstance(k, int):
                if k // 2 not in shared:
                    shared[k // 2] = head_bias(hh)
                slope, key_term, row_off = shared[k // 2]
            else:
                slope, key_term, row_off = head_bias(hh)

            def scores(r, si):
                s = _dot(q_ref[0, p][subs[si][0]:subs[si][1]], kt_ref[0, idx, chunk[r]])
                return s + (neg_dist[si] * slope if r == 0 else key_term[r])

            return [(functools.partial(scores, si=si),
                     lambda r: v_ref[0, hh, pl.ds(pl.multiple_of(chunk[r] * TK, TK), TK), :],
                     lambda r, si=si: row_off[r][si]) for si in range(len(subs))]

        _run_heads(bounded_ref[pl.program_id(0), p] != 0, list(range(4)), streams_of, nc, s_ref,
                   acc_ref.at[pl.ds(4 * p, 4)], roll_exact=True)
        return carry

    assert HEADS_PER_PIPELINE == 4
    lax.fori_loop(0, N_HEADS_C // 2, head_pair, 0)

    lam = (jnp.exp(jnp.sum(lq1_ref[...] * lk1_ref[...], axis=-1, keepdims=True))
           - jnp.exp(jnp.sum(lq2_ref[...] * lk2_ref[...], axis=-1, keepdims=True)) + lam_init)
    outs = []
    for p in range(N_HEADS_C // 2):
        o = (_normalised_pair(acc_ref[4 * p], acc_ref[4 * p + 2])
             - lam * _normalised_pair(acc_ref[4 * p + 1], acc_ref[4 * p + 3]))
        outs.append(_half_rms(o) * subln_ref[...] * (1.0 - lam_init))
    o_ref[...] = jnp.concatenate(outs, axis=-1).astype(BF16)


def _attention(kernel_fn, bounded, n_acc, q, kt, v, extras, out_width, B, S):
    nq = S // TQ
    hq, hv = q.shape[1], v.shape[1]
    extra_specs = [pl.BlockSpec(e.shape, lambda b, i, f, n=e.ndim: (0,) * n) for e in extras]
    return pl.pallas_call(
        kernel_fn,
        grid_spec=pltpu.PrefetchScalarGridSpec(
            num_scalar_prefetch=1,
            grid=(B, nq),
            in_specs=[pl.BlockSpec((1, hq, TQ, LANES), lambda b, i, f: (b, 0, i, 0)),
                      pl.BlockSpec((1,) + kt.shape[1:],
                                   lambda b, i, f, n=kt.ndim: (b,) + (0,) * (n - 1)),
                      pl.BlockSpec((1, hv, S, LANES), lambda b, i, f: (b, 0, 0, 0))] + extra_specs,
            out_specs=pl.BlockSpec((TQ, out_width), lambda b, i, f: (b * nq + i, 0)),
            scratch_shapes=[pltpu.VMEM((TQ, S), F32),
                            pltpu.VMEM((n_acc, TQ, LANES), F32)]),
        out_shape=jax.ShapeDtypeStruct((B * S, out_width), BF16),
        compiler_params=pltpu.CompilerParams(dimension_semantics=("arbitrary", "arbitrary"),
                                             vmem_limit_bytes=VMEM_LIMIT_BYTES),
        name=kernel_fn.func.__name__.strip("_"),
    )(bounded, q, kt, v, *extras)


def _mlp_kernel(xp_ref, x_ref, xn_ref, ap_ref, a_ref, an_ref, bp_ref, b_ref, bn_ref,
                cp_ref, c_ref, cn_ref, wo_ref, g_ref, wu_ref, cw_ref, cb_ref, wd_ref, fg_ref,
                o_ref, act_ref, *, tiles_per_seq, final_norm):
    i = pl.program_id(0)
    tm = x_ref.shape[0]
    halo = xp_ref.shape[0]

    def rows(p, m, n):
        return jnp.concatenate([p[...], m[...], n[...]], axis=0)

    o_ext = jnp.concatenate([rows(ap_ref, a_ref, an_ref), rows(bp_ref, b_ref, bn_ref),
                             rows(cp_ref, c_ref, cn_ref)], axis=1)
    x1 = rows(xp_ref, x_ref, xn_ref) + _dot(o_ext, wo_ref[...])
    h = (_rms(x1, D_MODEL) * g_ref[...]).astype(BF16)
    at_start = (i % tiles_per_seq) == 0
    at_end = (i % tiles_per_seq) == tiles_per_seq - 1
    zeros = jnp.zeros((halo, D_MODEL), BF16)
    hext = jnp.concatenate([jnp.where(at_start, zeros, h[:halo]), h[halo:halo + tm],
                            jnp.where(at_end, zeros, h[halo + tm:])], axis=0)

    def branch(k, c0):
        u = _dot(hext, wu_ref[:, k * D_FF + c0:k * D_FF + c0 + FFN_CHUNK])
        cw = cw_ref[k, :, c0:c0 + FFN_CHUNK]
        n = u.shape[0]
        prev = pltpu.roll(u, 1, axis=0)[halo:halo + tm]
        nxt = pltpu.roll(u, n - 1, axis=0)[halo:halo + tm]
        return (prev * cw[0:1] + u[halo:halo + tm] * cw[1:2] + nxt * cw[2:3]
                + cb_ref[k, :, c0:c0 + FFN_CHUNK])

    y = x1[halo:halo + tm]
    done = 0
    for c0 in range(0, D_FF, FFN_CHUNK):
        g = branch(0, c0)
        val = branch(1, c0)
        act_ref[:, c0:c0 + FFN_CHUNK] = (g / (1.0 + jnp.exp(-g)) * val).astype(BF16)
        ready = c0 + FFN_CHUNK - FFN_DOWN_LAG * FFN_CHUNK
        if ready - done >= FFN_DOWN_GROUP:
            y = y + _dot(act_ref[:, done:done + FFN_DOWN_GROUP], wd_ref[done:done + FFN_DOWN_GROUP])
            done += FFN_DOWN_GROUP
    y = y + _dot(act_ref[:, done:], wd_ref[done:])
    if final_norm:
        y = _rms(y, D_MODEL) * fg_ref[...]
    o_ref[...] = y


def _mlp(x2d, oa, ob, oc, w_out, g_ffn, w_up, conv_w, conv_b, w_down, final_g, S, final_norm):
    T = x2d.shape[0]
    tm, halo = TM_FFN, BF16_SUBLANES
    per = tm // halo
    last = T // halo - 1
    resident = lambda shape: pl.BlockSpec(shape, lambda i: (0,) * len(shape),
                                          pipeline_mode=pl.Buffered(1))

    def with_halos(width):
        return [pl.BlockSpec((halo, width), lambda i: (jnp.maximum(i * per - 1, 0), 0)),
                pl.BlockSpec((tm, width), lambda i: (i, 0)),
                pl.BlockSpec((halo, width), lambda i: (jnp.minimum((i + 1) * per, last), 0))]

    return pl.pallas_call(
        functools.partial(_mlp_kernel, tiles_per_seq=S // tm, final_norm=final_norm),
        grid=(T // tm,),
        in_specs=(with_halos(D_MODEL) + with_halos(oa.shape[1]) + with_halos(ob.shape[1])
                  + with_halos(oc.shape[1])
                  + [resident(w_out.shape), resident((1, D_MODEL)),
                     resident((D_MODEL, 2 * D_FF)), resident((2, 3, D_FF)), resident((2, 1, D_FF)),
                     resident((D_FF, D_MODEL)), resident((1, D_MODEL))]),
        out_specs=pl.BlockSpec((tm, D_MODEL), lambda i: (i, 0)),
        out_shape=jax.ShapeDtypeStruct((T, D_MODEL), F32),
        scratch_shapes=[pltpu.VMEM((tm, D_FF), BF16)],
        compiler_params=pltpu.CompilerParams(dimension_semantics=("arbitrary",),
                                             vmem_limit_bytes=VMEM_LIMIT_BYTES),
        name="out_proj_conv_mlp",
    )(x2d, x2d, x2d, oa, oa, oa, ob, ob, ob, oc, oc, oc, w_out, g_ffn,
      w_up, conv_w, conv_b, w_down, final_g)


def _pad_groups(w, n_groups, width, offset=0):
    r = w.shape[0]
    w = w.reshape(r, n_groups, width)
    w = jnp.pad(w, ((0, 0), (0, 0), (offset, LANES - width - offset)))
    return w.reshape(r, n_groups * LANES)


def _group_a_lane_order():
    nf = HEAD_DIM // 4
    lanes = np.arange(LANES)
    second, r = lanes // HALF, lanes % HALF
    head, rr = r // A_QUARTER, r % A_QUARTER
    block, freq = rr // nf, rr % nf
    dim = block * 2 * nf + second * nf + freq
    return head, dim, block, freq, second


def _group_b_lane_order():
    nf = MLA_ROPE // 4
    src = np.full(LANES, -1)
    block = np.zeros(LANES, np.int64)
    freq = np.zeros(LANES, np.int64)
    second = np.zeros(LANES, np.int64)
    is_rope = np.zeros(LANES, bool)
    for e in range(2):
        for j in range(2 * nf):
            lane = e * HALF + j
            is_rope[lane], block[lane], freq[lane], second[lane] = True, j // nf, j % nf, e
            src[lane] = MLA_NOPE + (j // nf) * 2 * nf + e * nf + j % nf
    n_low = HALF - 2 * nf
    src[2 * nf:HALF] = np.arange(n_low)
    src[HALF + 2 * nf:HALF + 2 * nf + MLA_NOPE - n_low] = np.arange(n_low, MLA_NOPE)
    return src, is_rope, block, freq, second


def _gather_cols(w, src):
    return jnp.where(jnp.asarray(src >= 0), w[:, np.maximum(src, 0)], 0.0)


def _rope_tables(S):
    t = jnp.arange(S, dtype=jnp.int32)
    pos = jnp.stack([(t // GRID_W).astype(F32), (t % GRID_W).astype(F32)], axis=0)

    def table(nf, block, freq, second, is_rope):
        inv = ROPE_BASE ** (-jnp.arange(nf, dtype=F32) / nf)
        ang = pos[block].T * inv[freq]
        sign = jnp.asarray(2.0 * second - 1.0, F32)
        cos = jnp.where(jnp.asarray(is_rope), jnp.cos(ang), 1.0)
        sin = jnp.where(jnp.asarray(is_rope), jnp.sin(ang) * sign, 0.0)
        return cos, sin

    _, _, block, freq, second = _group_a_lane_order()
    cos_a, sin_a = table(HEAD_DIM // 4, block, freq, second, np.ones(LANES, bool))
    _, is_rope, block, freq, second = _group_b_lane_order()
    cos_b, sin_b = table(MLA_ROPE // 4, block, freq, second, is_rope)
    return cos_a, sin_a, cos_b, sin_b


def _prep_w_in(w):
    n_qa, n_ka = N_HEADS_A * HEAD_DIM, N_KV_A * HEAD_DIM
    a_end = n_qa + 2 * n_ka
    b_end = a_end + MLA_Q_RANK + MLA_KV_RANK + MLA_ROPE
    head, dim, _, _, _ = _group_a_lane_order()
    pair_cols = head * HEAD_DIM + dim
    qa_cols = np.concatenate([p * LANES + pair_cols for p in range(N_HEADS_A // 2)])
    cq = w[:, a_end:a_end + MLA_Q_RANK]
    ckv = w[:, a_end + MLA_Q_RANK:a_end + MLA_Q_RANK + MLA_KV_RANK]
    kr = w[:, b_end - MLA_ROPE:b_end]
    src_b, is_rope, _, _, _ = _group_b_lane_order()
    out = jnp.concatenate([
        w[:, qa_cols], w[:, n_qa + pair_cols], w[:, n_qa + n_ka:a_end],
        jnp.pad(cq, ((0, 0), (0, CQ_PAD - MLA_Q_RANK))),
        ckv,
        _gather_cols(kr, np.where(is_rope, src_b - MLA_NOPE, -1)),
        w[:, b_end:],
    ], axis=-1).astype(BF16)
    assert out.shape[1] == IN_COLS
    return out


def _twice(v):
    return jnp.tile(v, 2).reshape(1, 2 * v.shape[0])


def kernel(x, norm_attn, w_in, q_norm_a, k_norm_a, q_a_norm_b, w_uq_b, kv_a_norm_b, w_ukv_b,
           lambda_q1_c, lambda_k1_c, lambda_q2_c, lambda_k2_c, subln_c, w_out,
           norm_ffn, w_up, conv_w, conv_b, w_down, final_norm):
    B, S, D = x.shape
    depth = w_in.shape[0]
    assert D == D_MODEL and S % max(TM_IN, TM_FFN, TQ, TK) == 0 and S % GRID_W == 0
    assert TM_IN == TK
    T = B * S

    tabs = _rope_tables(S)
    row = lambda v: v.reshape(1, -1)
    a_pairs = [(j, j + N_HEADS_A // 2) for j in range(N_HEADS_A // 2)]
    a_rows = np.concatenate([np.arange(h * HEAD_DIM, (h + 1) * HEAD_DIM)
                             for pair in a_pairs for h in pair])
    xc = x.reshape(T, D)
    for l in range(depth):
        lam_init = 0.8 - 0.6 * math.exp(-0.3 * l)
        src_b, is_rope, _, _, _ = _group_b_lane_order()
        d_qk = MLA_NOPE + MLA_ROPE
        wuq_p = jnp.pad(
            jnp.concatenate([_gather_cols(w_uq_b[l][:, h * d_qk:(h + 1) * d_qk], src_b)
                             for h in range(N_HEADS_B)], axis=-1),
            ((0, CQ_PAD - MLA_Q_RANK), (0, 0))).astype(BF16)
        wukv = w_ukv_b[l].reshape(MLA_KV_RANK, N_HEADS_B, MLA_NOPE + MLA_V)
        k_cols = [_gather_cols(wukv[:, h, :MLA_NOPE], np.where(is_rope, -1, src_b))
                  for h in range(N_HEADS_B)]
        v_cols = [_pad_groups(wukv[:, h, MLA_NOPE:], 1, MLA_V, offset=HALF * (h % 2))
                  for h in range(N_HEADS_B)]
        wukv_p = jnp.concatenate(k_cols + v_cols, axis=-1).astype(BF16)

        qa, kta, va, qb, ktb, vb, qc, ktc, vc, nrm = _in_proj(
            xc, row(norm_attn[l]), _prep_w_in(w_in[l]), tabs,
            q_norm_a[l][_group_a_lane_order()[1]].reshape(1, LANES),
            k_norm_a[l][_group_a_lane_order()[1]].reshape(1, LANES),
            jnp.pad(q_a_norm_b[l], (0, CQ_PAD - MLA_Q_RANK)).reshape(1, CQ_PAD),
            row(kv_a_norm_b[l]), wuq_p, wukv_p, B, S)

        nrm = jnp.max(nrm.reshape(B, -1, LANES), axis=1)

        def bounded(q0, nq, q_rep, k0, nk, k_rep, per_pipeline):
            q2 = jnp.repeat(nrm[:, q0:q0 + nq], q_rep, axis=1)
            k2 = jnp.repeat(nrm[:, k0:k0 + nk], k_rep, axis=1)
            ok = q2 * k2 * NRM_MARGIN <= SAFE_LOG2_RANGE ** 2
            return jnp.all(ok.reshape(B, -1, per_pipeline), axis=-1).astype(jnp.int32)

        qk_a = (HEAD_DIM * jnp.max(jnp.abs(q_norm_a[l])) * jnp.max(jnp.abs(k_norm_a[l]))
                * (HEAD_DIM ** -0.5 * LOG2E))
        bounded_a = jnp.broadcast_to(
            (qk_a * NRM_MARGIN <= SAFE_LOG2_RANGE).astype(jnp.int32),
            (B, N_HEADS_A // HEADS_PER_PIPELINE_AB))

        group = N_HEADS_A // N_KV_A
        oa = _attention(
            functools.partial(_attn_ab_kernel, n_heads=N_HEADS_A,
                              per_pipeline=HEADS_PER_PIPELINE_AB, q_index=lambda h: h // 2,
                              kt_index=lambda h: 2 * (h // group) + h % 2,
                              v_index=lambda h: h // group, out_pairs=a_pairs, n_keys=S),
            bounded_a, N_HEADS_A, qa, kta, va, [], N_HEADS_A * HEAD_DIM, B, S)
        same = lambda h: h
        ob = _attention(
            functools.partial(_attn_ab_kernel, n_heads=N_HEADS_B,
                              per_pipeline=HEADS_PER_PIPELINE_AB, q_index=same, kt_index=same,
                              v_index=same,
                              out_pairs=[(2 * j, 2 * j + 1) for j in range(N_HEADS_B // 2)],
                              n_keys=S),
            bounded(NRM_QB, N_HEADS_B, 1, NRM_KB, N_HEADS_B, 1, HEADS_PER_PIPELINE_AB), N_HEADS_B,
            qb, ktb, vb, [], N_HEADS_B * MLA_V, B, S)
        oc = _attention(
            functools.partial(_attn_c_kernel, n_keys=S, lam_init=lam_init),
            bounded(NRM_QC, N_HEADS_C, 2, NRM_KC, 2 * N_HEADS_C, 1, HEADS_PER_PIPELINE),
            2 * N_HEADS_C, qc, ktc, vc,
            [row(lambda_q1_c[l]), row(lambda_k1_c[l]), row(lambda_q2_c[l]), row(lambda_k2_c[l]),
             _twice(subln_c[l])],
            N_HEADS_C * DIFF_V, B, S)

        w_out_l = jnp.concatenate([w_out[l][a_rows], w_out[l][N_HEADS_A * HEAD_DIM:]], axis=0)
        xc = _mlp(xc, oa, ob, oc, w_out_l.astype(BF16), row(norm_ffn[l]), w_up[l].astype(BF16),
                  conv_w[l].reshape(3, 2, D_FF).transpose(1, 0, 2),
                  conv_b[l].reshape(2, 1, D_FF), w_down[l].astype(BF16),
                  row(final_norm), S, final_norm=(l == depth - 1))
    return xc.reshape(B, S, D)
```

```python
import functools
import math

import jax
import jax.numpy as jnp
import numpy as np
from jax import lax
from jax.experimental import pallas as pl
from jax.experimental.pallas import tpu as pltpu

D_MODEL = 1024
GRID_W = 64
HEAD_DIM = 64
N_HEADS_A = 8
N_KV_A = 2
N_HEADS_B = 4
MLA_Q_RANK = 192
MLA_KV_RANK = 128
MLA_NOPE = 64
MLA_ROPE = 32
MLA_V = 64
N_HEADS_C = 4
DIFF_QK = 32
DIFF_V = 64
D_FF = 2816
ROPE_BASE = 10000.0
EPS = 1e-6

LANES = 128
HALF = LANES // 2
A_QUARTER = LANES // 4
BF16_SUBLANES = 16
VMEM_LIMIT_BYTES = 56 * 1024 * 1024

LOG2E = math.log2(math.e)
assert HEAD_DIM == MLA_V == DIFF_V == 2 * DIFF_QK == HALF

OFF_QA = 0
OFF_KA = OFF_QA + N_HEADS_A * HEAD_DIM
OFF_VA = OFF_KA + LANES
OFF_CQ = OFF_VA + LANES
CQ_PAD = 2 * LANES
OFF_CKV = OFF_CQ + CQ_PAD
OFF_KR = OFF_CKV + MLA_KV_RANK
OFF_QC = OFF_KR + LANES
OFF_KC = OFF_QC + N_HEADS_C * HALF
OFF_VC = OFF_KC + N_HEADS_C * HALF
IN_COLS = OFF_VC + N_HEADS_C * HALF

NRM_QB = 0
NRM_KB = NRM_QB + N_HEADS_B
NRM_QC = NRM_KB + N_HEADS_B
NRM_KC = NRM_QC + N_HEADS_C
NRM_GROUPS = 2 * N_HEADS_B + N_HEADS_C
NRM_MARGIN = 1.03
SAFE_LOG2_RANGE = 55.0

F32 = jnp.float32
BF16 = jnp.bfloat16

TM_IN = 512
TM_FFN = 256
FFN_CHUNK = 256
FFN_DOWN_GROUP = 6 * FFN_CHUNK
FFN_DOWN_LAG = 2
TQ = 512
TQ_SUB_AB = 512
TQ_SUB_C = 256
TK = 512
SCORE_LOOKAHEAD = 2
HEADS_PER_PIPELINE = 4
HEADS_PER_PIPELINE_AB = 4


def _dot(a, b):
    return jnp.dot(a, b, preferred_element_type=F32)


def _rms(x, n):
    ms = jnp.sum(x * x, axis=-1, keepdims=True) * (1.0 / n)
    return x * lax.rsqrt(ms + EPS)


def _low_lanes():
    return lax.broadcasted_iota(jnp.int32, (1, LANES), 1) < HALF


def _half_rms(x, low=None):
    low = _low_lanes() if low is None else low
    sq = x * x
    s_lo = jnp.sum(jnp.where(low, sq, 0.0), axis=-1, keepdims=True)
    s_hi = jnp.sum(jnp.where(low, 0.0, sq), axis=-1, keepdims=True)
    return x * lax.rsqrt(jnp.where(low, s_lo, s_hi) * (1.0 / HALF) + EPS)


def _in_kernel(x_ref, gat_ref, w_ref, cos_a_ref, sin_a_ref, cos_b_ref, sin_b_ref,
               qg_a_ref, kg_a_ref, qg_b_ref, kvg_b_ref, wuq_ref, wukv_ref, sel_ref,
               qa_ref, kta_ref, va_ref, qb_ref, ktb_ref, vb_ref, qc_ref, ktc_ref, vc_ref, nrm_ref):
    x = x_ref[...]
    tm = x.shape[0]
    h = (_rms(x, D_MODEL) * gat_ref[...]).astype(BF16)

    lane = lax.broadcasted_iota(jnp.int32, (1, LANES), 1)
    low = lane < HALF
    row = lax.broadcasted_iota(jnp.int32, (LANES, 1), 0)

    z = _dot(h, w_ref[...])

    def seg(off, width):
        return z[:, off:off + width]

    def rope(y, c, s):
        return y * c + pltpu.roll(y, HALF, axis=1) * s

    def group(z, g):
        return z[:, g * LANES:(g + 1) * LANES]

    squares = []

    def note(y):
        squares.append((y * y).astype(BF16))

    def noted_row_sums(first):
        return _dot(jnp.concatenate(squares[first:], axis=-1),
                    sel_ref[first * LANES:len(squares) * LANES])

    cos_b, sin_b = cos_b_ref[...], sin_b_ref[...]
    cq =(_rms(seg(OFF_CQ, CQ_PAD), MLA_Q_RANK) * qg_b_ref[...]).astype(BF16)
    qb = _dot(cq, wuq_ref[...])
    scale_b = (MLA_NOPE + MLA_ROPE) ** -0.5 * LOG2E
    for hh in range(N_HEADS_B):
        y = rope(group(qb, hh), cos_b, sin_b) * scale_b
        qb_ref[0, hh] = y.astype(BF16)
        note(y)
    ckv = (_rms(seg(OFF_CKV, MLA_KV_RANK), MLA_KV_RANK) * kvg_b_ref[...]).astype(BF16)
    kvb = _dot(ckv, wukv_ref[...])
    kr = rope(seg(OFF_KR, LANES), cos_b, sin_b)
    for hh in range(N_HEADS_B):
        y = group(kvb, hh) + kr
        ktb_ref[0, hh] = y.T.astype(BF16)
        note(y)
        ones_other_half = jnp.where(low, float(hh % 2), float(1 - hh % 2))
        vb_ref[0, hh] = (group(kvb, N_HEADS_B + hh) + ones_other_half).astype(BF16)
    n2 = noted_row_sums(0)
    n_noted_b = len(squares)

    zq = seg(OFF_QC, N_HEADS_C * HALF)
    zk = seg(OFF_KC, N_HEADS_C * HALF)
    zv = seg(OFF_VC, N_HEADS_C * HALF)
    k_groups = []
    for p in range(N_HEADS_C // 2):
        y = group(zq, p) * (DIFF_QK ** -0.5 * LOG2E)
        qc_ref[0, p] = y.astype(BF16)
        note(y)
        k_groups.append(group(zk, p))
        kt = group(zk, p).T
        for sub in range(4):
            mine = (row >= sub * DIFF_QK) & (row < (sub + 1) * DIFF_QK)
            ktc_ref[0, 4 * p + sub, 0] = jnp.where(mine, kt, 0.0).astype(BF16)
        vz = group(zv, p)
        vc_ref[0, 2 * p] = jnp.where(low, vz, 1.0).astype(BF16)
        vc_ref[0, 2 * p + 1] = jnp.where(low, 1.0, vz).astype(BF16)
    for kz in k_groups:
        note(kz)
    n2 = n2 + noted_row_sums(n_noted_b)
    nrm_ref[0] = jnp.max(n2, axis=0, keepdims=True)

    cos_a, sin_a = cos_a_ref[...], sin_a_ref[...]
    first_head = (lane & A_QUARTER) == 0
    zq = seg(OFF_QA, N_HEADS_A * HEAD_DIM)
    for p in range(N_HEADS_A // 2):
        y = rope(_half_rms(group(zq, p), first_head) * qg_a_ref[...], cos_a, sin_a)
        qa_ref[0, p] = (y * (HEAD_DIM ** -0.5 * LOG2E)).astype(BF16)
    y = rope(_half_rms(seg(OFF_KA, LANES), first_head) * kg_a_ref[...], cos_a, sin_a)
    kt = y.T.astype(BF16)
    quarter = [kt[j * A_QUARTER:(j + 1) * A_QUARTER] for j in range(4)]
    zeros = jnp.zeros((A_QUARTER, tm), BF16)
    for g in range(N_KV_A):
        k1, k2 = quarter[g], quarter[2 + g]
        kta_ref[0, 2 * g] = jnp.concatenate([k1, zeros, k2, zeros], axis=0)
        kta_ref[0, 2 * g + 1] = jnp.concatenate([zeros, k1, zeros, k2], axis=0)
    zv = seg(OFF_VA, LANES)
    va_ref[0, 0] = jnp.where(low, zv, 1.0).astype(BF16)
    va_ref[0, 1] = jnp.where(low, 1.0, zv).astype(BF16)


def _norm_routing():
    sel = np.zeros((NRM_GROUPS, LANES, LANES), np.float32)
    g = 0
    for slot in ([NRM_QB + h for h in range(N_HEADS_B)] + [NRM_KB + h for h in range(N_HEADS_B)]):
        sel[g, :, slot] = 1.0
        g += 1
    for p in range(N_HEADS_C // 2):
        for f in range(2):
            sel[g, f * HALF:(f + 1) * HALF, NRM_QC + 2 * p + f] = 1.0
        g += 1
    for p in range(N_HEADS_C // 2):
        for sub in range(4):
            sel[g, sub * DIFF_QK:(sub + 1) * DIFF_QK, NRM_KC + 4 * p + sub] = 1.0
        g += 1
    assert g == NRM_GROUPS
    return jnp.asarray(sel.reshape(NRM_GROUPS * LANES, LANES), BF16)


def _in_proj(x2d, gat, w_in_r, tabs, qg_a, kg_a, qg_b, kvg_b, wuq_p, wukv_p, B, S):
    T = B * S
    tm = TM_IN
    nst = S // tm
    const = lambda shape: pl.BlockSpec(shape, lambda i: (0,) * len(shape))
    tab = pl.BlockSpec((tm, LANES), lambda i: (i % nst, 0))
    hm = lambda H: pl.BlockSpec((1, H, tm, LANES), lambda i: (i // nst, 0, i % nst, 0))
    hmt = lambda H: pl.BlockSpec((1, H, LANES, tm), lambda i: (i // nst, 0, 0, i % nst))
    sds = lambda H: jax.ShapeDtypeStruct((B, H, S, LANES), BF16)
    sdt = lambda H: jax.ShapeDtypeStruct((B, H, LANES, S), BF16)
    return pl.pallas_call(
        _in_kernel,
        grid=(T // tm,),
        in_specs=[pl.BlockSpec((tm, D_MODEL), lambda i: (i, 0)),
                  const((1, D_MODEL)), const((D_MODEL, IN_COLS)),
                  tab, tab, tab, tab,
                  const((1, LANES)), const((1, LANES)), const((1, CQ_PAD)), const((1, MLA_KV_RANK)),
                  const((CQ_PAD, N_HEADS_B * LANES)), const((MLA_KV_RANK, 2 * N_HEADS_B * LANES)),
                  const((NRM_GROUPS * LANES, LANES))],
        out_specs=[hm(N_HEADS_A // 2), hmt(2 * N_KV_A), hm(N_KV_A),
                   hm(N_HEADS_B), hmt(N_HEADS_B), hm(N_HEADS_B),
                   hm(N_HEADS_C // 2),
                   pl.BlockSpec((1, 2 * N_HEADS_C, 1, LANES, tm),
                                lambda i: (i // nst, 0, i % nst, 0, 0)),
                   hm(N_HEADS_C),
                   pl.BlockSpec((1, 1, LANES), lambda i: (i, 0, 0))],
        out_shape=[sds(N_HEADS_A // 2), sdt(2 * N_KV_A), sds(N_KV_A),
                   sds(N_HEADS_B), sdt(N_HEADS_B), sds(N_HEADS_B),
                   sds(N_HEADS_C // 2),
                   jax.ShapeDtypeStruct((B, 2 * N_HEADS_C, S // tm, LANES, tm), BF16),
                   sds(N_HEADS_C),
                   jax.ShapeDtypeStruct((T // tm, 1, LANES), F32)],
        compiler_params=pltpu.CompilerParams(dimension_semantics=("arbitrary",),
                                             vmem_limit_bytes=VMEM_LIMIT_BYTES),
        name="in_proj",
    )(x2d, gat, w_in_r, *tabs, qg_a, kg_a, qg_b, kvg_b, wuq_p, wukv_p, _norm_routing())


def _lane_group_max(s):
    m = s[:, :LANES]
    for j in range(1, s.shape[1] // LANES):
        m = jnp.maximum(m, s[:, j * LANES:(j + 1) * LANES])
    return m


def _pv_bounded(streams, n_chunks):
    items = [(r, i) for r in range(n_chunks) for i in range(len(streams))]
    accs = [None] * len(streams)
    in_flight = {}
    for k in range(len(items) + SCORE_LOOKAHEAD):
        if k < len(items):
            r, i = items[k]
            scores, _, row_off = streams[i]
            s, off = scores(r), row_off(r)
            in_flight[k] = s if off is None else s + off
        if k >= SCORE_LOOKAHEAD:
            r, i = items[k - SCORE_LOOKAHEAD]
            d = _dot(jnp.exp2(in_flight.pop(k - SCORE_LOOKAHEAD)).astype(BF16), streams[i][1](r))
            accs[i] = d if accs[i] is None else accs[i] + d
    return accs


def _pv_exact(streams, n_chunks, s_ref):
    rows = s_ref.shape[0] // len(streams)
    maxes = []
    for i, (scores, _, row_off) in enumerate(streams):
        m_run = None
        for r in range(n_chunks):
            s = scores(r)
            s_ref[i * rows:(i + 1) * rows, r * TK:(r + 1) * TK] = s
            m, off = _lane_group_max(s), row_off(r)
            if off is not None:
                m = m + off
            m_run = m if m_run is None else jnp.maximum(m_run, m)
        maxes.append(jnp.max(m_run, axis=-1, keepdims=True))
    accs = []
    for i, (_, values, row_off) in enumerate(streams):
        acc = None
        for r in range(n_chunks):
            off = row_off(r)
            shift = maxes[i] if off is None else maxes[i] - off
            p = jnp.exp2(s_ref[i * rows:(i + 1) * rows, r * TK:(r + 1) * TK] - shift)
            d = _dot(p.astype(BF16), values(r))
            acc = d if acc is None else acc + d
        accs.append(acc)
    return accs


def _run_heads(bounded, heads, head_streams, n_chunks, s_ref, acc_ref, roll_exact=False):
    @pl.when(bounded)
    def _():
        shared = {}
        per_head = [head_streams(h, shared) for h in heads]
        n_sub = len(per_head[0])
        accs = _pv_bounded([s for streams in per_head for s in streams], n_chunks)
        for k, h in enumerate(heads):
            acc_ref[h] = jnp.concatenate(accs[k * n_sub:(k + 1) * n_sub], axis=0)

    @pl.when(jnp.logical_not(bounded))
    def _():
        def one_head(h):
            acc_ref[h] = jnp.concatenate(_pv_exact(head_streams(h, {}), n_chunks, s_ref), axis=0)

        if roll_exact:
            lax.fori_loop(0, len(heads), lambda k, c: (one_head(heads[0] + k), c)[1], 0)
        else:
            for h in heads:
                one_head(h)


def _normalised_pair(acc_low, acc_high):
    low = _low_lanes()
    num = jnp.where(low, acc_low, acc_high)
    den = pltpu.roll(jnp.where(low, acc_high, acc_low), HALF, axis=1)
    return num / den


def _attn_ab_kernel(bounded_ref, q_ref, kt_ref, v_ref, o_ref, s_ref, acc_ref, *,
                    n_heads, per_pipeline, q_index, kt_index, v_index, out_pairs, n_keys):
    n_chunks = n_keys // TK

    def head_streams(h, shared):
        q = q_ref[0, q_index(h)]
        return [(lambda r, qs=q[r0:r0 + TQ_SUB_AB]:
                 _dot(qs, kt_ref[0, kt_index(h), :, r * TK:(r + 1) * TK]),
                 lambda r: v_ref[0, v_index(h), r * TK:(r + 1) * TK, :],
                 lambda r: None) for r0 in range(0, TQ, TQ_SUB_AB)]

    def head_group(j, carry):
        heads = [per_pipeline * j + k for k in range(per_pipeline)]
        _run_heads(bounded_ref[pl.program_id(0), j] != 0, heads, head_streams, n_chunks, s_ref,
                   acc_ref, roll_exact=True)
        return carry

    if n_heads == per_pipeline:
        head_group(0, None)
    else:
        lax.fori_loop(0, n_heads // per_pipeline, head_group, 0)
    o_ref[...] = jnp.concatenate([_normalised_pair(acc_ref[a], acc_ref[b]) for a, b in out_pairs],
                                 axis=-1).astype(BF16)


def _attn_c_kernel(bounded_ref, q_ref, kt_ref, v_ref, lq1_ref, lk1_ref, lq2_ref, lk2_ref,
                   subln_ref, o_ref, s_ref, acc_ref, *, n_keys, lam_init):
    assert TQ == TK
    nc = n_keys // TK
    cd = pl.program_id(1)
    col = lax.broadcasted_iota(jnp.int32, (1, TK), 1).astype(F32)
    row = lax.broadcasted_iota(jnp.int32, (TQ_SUB_C, 1), 0).astype(F32)
    subs = [(r, r + TQ_SUB_C) for r in range(0, TQ, TQ_SUB_C)]
    neg_dist = [-jnp.abs((lax.broadcasted_iota(jnp.int32, (TQ_SUB_C, TK), 0) + r0
                          - lax.broadcasted_iota(jnp.int32, (TQ_SUB_C, TK), 1)).astype(F32))
                for r0, _ in subs]

    chunk = [(cd + r) % nc for r in range(nc)]

    def head_bias(hh):
        slope = jnp.float32(LOG2E * 2.0 ** (-8.0 * N_HEADS_C / N_HEADS_C))
        for k in range(N_HEADS_C - 2, -1, -1):
            slope = jnp.where(hh == k, LOG2E * 2.0 ** (-8.0 * (k + 1) / N_HEADS_C), slope)
        key_term, row_off = [None], [[None] * len(subs)]
        for r in range(1, nc):
            side = jnp.where(chunk[r] < cd, slope, -slope)
            key_term.append(col * side)
            base = -slope * (jnp.abs(chunk[r] - cd) * TK).astype(F32)
            row_off.append([base - side * (row + float(r0)) for r0, _ in subs])
        return slope, key_term, row_off

    def head_pair(p, carry):
        def streams_of(k, shared):
            idx, hh = 4 * p + k, 2 * p + k // 2
            if isinstance(k, int):
                if k // 2 not in shared:
                    shared[k // 2] = head_bias(hh)
                slope, key_term, row_off = shared[k // 2]
            else:
                slope, key_term, row_off = head_bias(hh)

            def scores(r, si):
                s = _dot(q_ref[0, p][subs[si][0]:subs[si][1]], kt_ref[0, idx, chunk[r]])
                return s + (neg_dist[si] * slope if r == 0 else key_term[r])

            return [(functools.partial(scores, si=si),
                     lambda r: v_ref[0, hh, pl.ds(pl.multiple_of(chunk[r] * TK, TK), TK), :],
                     lambda r, si=si: row_off[r][si]) for si in range(len(subs))]

        _run_heads(bounded_ref[pl.program_id(0), p] != 0, list(range(4)), streams_of, nc, s_ref,
                   acc_ref.at[pl.ds(4 * p, 4)], roll_exact=True)
        return carry

    assert HEADS_PER_PIPELINE == 4
    lax.fori_loop(0, N_HEADS_C // 2, head_pair, 0)

    lam = (jnp.exp(jnp.sum(lq1_ref[...] * lk1_ref[...], axis=-1, keepdims=True))
           - jnp.exp(jnp.sum(lq2_ref[...] * lk2_ref[...], axis=-1, keepdims=True)) + lam_init)
    outs = []
    for p in range(N_HEADS_C // 2):
        o = (_normalised_pair(acc_ref[4 * p], acc_ref[4 * p + 2])
             - lam * _normalised_pair(acc_ref[4 * p + 1], acc_ref[4 * p + 3]))
        outs.append(_half_rms(o) * subln_ref[...] * (1.0 - lam_init))
    o_ref[...] = jnp.concatenate(outs, axis=-1).astype(BF16)


def _attention(kernel_fn, bounded, n_acc, q, kt, v, extras, out_width, B, S):
    nq = S // TQ
    hq, hv = q.shape[1], v.shape[1]
    extra_specs = [pl.BlockSpec(e.shape, lambda b, i, f, n=e.ndim: (0,) * n) for e in extras]
    return pl.pallas_call(
        kernel_fn,
        grid_spec=pltpu.PrefetchScalarGridSpec(
            num_scalar_prefetch=1,
            grid=(B, nq),
            in_specs=[pl.BlockSpec((1, hq, TQ, LANES), lambda b, i, f: (b, 0, i, 0)),
                      pl.BlockSpec((1,) + kt.shape[1:],
                                   lambda b, i, f, n=kt.ndim: (b,) + (0,) * (n - 1)),
                      pl.BlockSpec((1, hv, S, LANES), lambda b, i, f: (b, 0, 0, 0))] + extra_specs,
            out_specs=pl.BlockSpec((TQ, out_width), lambda b, i, f: (b * nq + i, 0)),
            scratch_shapes=[pltpu.VMEM((TQ, S), F32),
                            pltpu.VMEM((n_acc, TQ, LANES), F32)]),
        out_shape=jax.ShapeDtypeStruct((B * S, out_width), BF16),
        compiler_params=pltpu.CompilerParams(dimension_semantics=("arbitrary", "arbitrary"),
                                             vmem_limit_bytes=VMEM_LIMIT_BYTES),
        name=kernel_fn.func.__name__.strip("_"),
    )(bounded, q, kt, v, *extras)


def _mlp_kernel(xp_ref, x_ref, xn_ref, ap_ref, a_ref, an_ref, bp_ref, b_ref, bn_ref,
                cp_ref, c_ref, cn_ref, wo_ref, g_ref, wu_ref, cw_ref, cb_ref, wd_ref, fg_ref,
                o_ref, act_ref, *, tiles_per_seq, final_norm):
    i = pl.program_id(0)
    tm = x_ref.shape[0]
    halo = xp_ref.shape[0]

    def rows(p, m, n):
        return jnp.concatenate([p[...], m[...], n[...]], axis=0)

    o_ext = jnp.concatenate([rows(ap_ref, a_ref, an_ref), rows(bp_ref, b_ref, bn_ref),
                             rows(cp_ref, c_ref, cn_ref)], axis=1)
    x1 = rows(xp_ref, x_ref, xn_ref) + _dot(o_ext, wo_ref[...])
    h = (_rms(x1, D_MODEL) * g_ref[...]).astype(BF16)
    at_start = (i % tiles_per_seq) == 0
    at_end = (i % tiles_per_seq) == tiles_per_seq - 1
    zeros = jnp.zeros((halo, D_MODEL), BF16)
    hext = jnp.concatenate([jnp.where(at_start, zeros, h[:halo]), h[halo:halo + tm],
                            jnp.where(at_end, zeros, h[halo + tm:])], axis=0)

    def branch(k, c0):
        u = _dot(hext, wu_ref[:, k * D_FF + c0:k * D_FF + c0 + FFN_CHUNK])
        cw = cw_ref[k, :, c0:c0 + FFN_CHUNK]
        n = u.shape[0]
        prev = pltpu.roll(u, 1, axis=0)[halo:halo + tm]
        nxt = pltpu.roll(u, n - 1, axis=0)[halo:halo + tm]
        return (prev * cw[0:1] + u[halo:halo + tm] * cw[1:2] + nxt * cw[2:3]
                + cb_ref[k, :, c0:c0 + FFN_CHUNK])

    y = x1[halo:halo + tm]
    done = 0
    for c0 in range(0, D_FF, FFN_CHUNK):
        g = branch(0, c0)
        val = branch(1, c0)
        act_ref[:, c0:c0 + FFN_CHUNK] = (g / (1.0 + jnp.exp(-g)) * val).astype(BF16)
        ready = c0 + FFN_CHUNK - FFN_DOWN_LAG * FFN_CHUNK
        if ready - done >= FFN_DOWN_GROUP:
            y = y + _dot(act_ref[:, done:done + FFN_DOWN_GROUP], wd_ref[done:done + FFN_DOWN_GROUP])
            done += FFN_DOWN_GROUP
    y = y + _dot(act_ref[:, done:], wd_ref[done:])
    if final_norm:
        y = _rms(y, D_MODEL) * fg_ref[...]
    o_ref[...] = y


def _mlp(x2d, oa, ob, oc, w_out, g_ffn, w_up, conv_w, conv_b, w_down, final_g, S, final_norm):
    T = x2d.shape[0]
    tm, halo = TM_FFN, BF16_SUBLANES
    per = tm // halo
    last = T // halo - 1
    resident = lambda shape: pl.BlockSpec(shape, lambda i: (0,) * len(shape),
                                          pipeline_mode=pl.Buffered(1))

    def with_halos(width):
        return [pl.BlockSpec((halo, width), lambda i: (jnp.maximum(i * per - 1, 0), 0)),
                pl.BlockSpec((tm, width), lambda i: (i, 0)),
                pl.BlockSpec((halo, width), lambda i: (jnp.minimum((i + 1) * per, last), 0))]

    return pl.pallas_call(
        functools.partial(_mlp_kernel, tiles_per_seq=S // tm, final_norm=final_norm),
        grid=(T // tm,),
        in_specs=(with_halos(D_MODEL) + with_halos(oa.shape[1]) + with_halos(ob.shape[1])
                  + with_halos(oc.shape[1])
                  + [resident(w_out.shape), resident((1, D_MODEL)),
                     resident((D_MODEL, 2 * D_FF)), resident((2, 3, D_FF)), resident((2, 1, D_FF)),
                     resident((D_FF, D_MODEL)), resident((1, D_MODEL))]),
        out_specs=pl.BlockSpec((tm, D_MODEL), lambda i: (i, 0)),
        out_shape=jax.ShapeDtypeStruct((T, D_MODEL), F32),
        scratch_shapes=[pltpu.VMEM((tm, D_FF), BF16)],
        compiler_params=pltpu.CompilerParams(dimension_semantics=("arbitrary",),
                                             vmem_limit_bytes=VMEM_LIMIT_BYTES),
        name="out_proj_conv_mlp",
    )(x2d, x2d, x2d, oa, oa, oa, ob, ob, ob, oc, oc, oc, w_out, g_ffn,
      w_up, conv_w, conv_b, w_down, final_g)


def _pad_groups(w, n_groups, width, offset=0):
    r = w.shape[0]
    w = w.reshape(r, n_groups, width)
    w = jnp.pad(w, ((0, 0), (0, 0), (offset, LANES - width - offset)))
    return w.reshape(r, n_groups * LANES)


def _group_a_lane_order():
    nf = HEAD_DIM // 4
    lanes = np.arange(LANES)
    second, r = lanes // HALF, lanes % HALF
    head, rr = r // A_QUARTER, r % A_QUARTER
    block, freq = rr // nf, rr % nf
    dim = block * 2 * nf + second * nf + freq
    return head, dim, block, freq, second


def _group_b_lane_order():
    nf = MLA_ROPE // 4
    src = np.full(LANES, -1)
    block = np.zeros(LANES, np.int64)
    freq = np.zeros(LANES, np.int64)
    second = np.zeros(LANES, np.int64)
    is_rope = np.zeros(LANES, bool)
    for e in range(2):
        for j in range(2 * nf):
            lane = e * HALF + j
            is_rope[lane], block[lane], freq[lane], second[lane] = True, j // nf, j % nf, e
            src[lane] = MLA_NOPE + (j // nf) * 2 * nf + e * nf + j % nf
    n_low = HALF - 2 * nf
    src[2 * nf:HALF] = np.arange(n_low)
    src[HALF + 2 * nf:HALF + 2 * nf + MLA_NOPE - n_low] = np.arange(n_low, MLA_NOPE)
    return src, is_rope, block, freq, second


def _gather_cols(w, src):
    return jnp.where(jnp.asarray(src >= 0), w[:, np.maximum(src, 0)], 0.0)


def _rope_tables(S):
    t = jnp.arange(S, dtype=jnp.int32)
    pos = jnp.stack([(t // GRID_W).astype(F32), (t % GRID_W).astype(F32)], axis=0)

    def table(nf, block, freq, second, is_rope):
        inv = ROPE_BASE ** (-jnp.arange(nf, dtype=F32) / nf)
        ang = pos[block].T * inv[freq]
        sign = jnp.asarray(2.0 * second - 1.0, F32)
        cos = jnp.where(jnp.asarray(is_rope), jnp.cos(ang), 1.0)
        sin = jnp.where(jnp.asarray(is_rope), jnp.sin(ang) * sign, 0.0)
        return cos, sin

    _, _, block, freq, second = _group_a_lane_order()
    cos_a, sin_a = table(HEAD_DIM // 4, block, freq, second, np.ones(LANES, bool))
    _, is_rope, block, freq, second = _group_b_lane_order()
    cos_b, sin_b = table(MLA_ROPE // 4, block, freq, second, is_rope)
    return cos_a, sin_a, cos_b, sin_b


def _prep_w_in(w):
    n_qa, n_ka = N_HEADS_A * HEAD_DIM, N_KV_A * HEAD_DIM
    a_end = n_qa + 2 * n_ka
    b_end = a_end + MLA_Q_RANK + MLA_KV_RANK + MLA_ROPE
    head, dim, _, _, _ = _group_a_lane_order()
    pair_cols = head * HEAD_DIM + dim
    qa_cols = np.concatenate([p * LANES + pair_cols for p in range(N_HEADS_A // 2)])
    cq = w[:, a_end:a_end + MLA_Q_RANK]
    ckv = w[:, a_end + MLA_Q_RANK:a_end + MLA_Q_RANK + MLA_KV_RANK]
    kr = w[:, b_end - MLA_ROPE:b_end]
    src_b, is_rope, _, _, _ = _group_b_lane_order()
    out = jnp.concatenate([
        w[:, qa_cols], w[:, n_qa + pair_cols], w[:, n_qa + n_ka:a_end],
        jnp.pad(cq, ((0, 0), (0, CQ_PAD - MLA_Q_RANK))),
        ckv,
        _gather_cols(kr, np.where(is_rope, src_b - MLA_NOPE, -1)),
        w[:, b_end:],
    ], axis=-1).astype(BF16)
    assert out.shape[1] == IN_COLS
    return out


def _twice(v):
    return jnp.tile(v, 2).reshape(1, 2 * v.shape[0])


def kernel(x, norm_attn, w_in, q_norm_a, k_norm_a, q_a_norm_b, w_uq_b, kv_a_norm_b, w_ukv_b,
           lambda_q1_c, lambda_k1_c, lambda_q2_c, lambda_k2_c, subln_c, w_out,
           norm_ffn, w_up, conv_w, conv_b, w_down, final_norm):
    B, S, D = x.shape
    depth = w_in.shape[0]
    assert D == D_MODEL and S % max(TM_IN, TM_FFN, TQ, TK) == 0 and S % GRID_W == 0
    assert TM_IN == TK
    T = B * S

    tabs = _rope_tables(S)
    row = lambda v: v.reshape(1, -1)
    a_pairs = [(j, j + N_HEADS_A // 2) for j in range(N_HEADS_A // 2)]
    a_rows = np.concatenate([np.arange(h * HEAD_DIM, (h + 1) * HEAD_DIM)
                             for pair in a_pairs for h in pair])
    xc = x.reshape(T, D)
    for l in range(depth):
        lam_init = 0.8 - 0.6 * math.exp(-0.3 * l)
        src_b, is_rope, _, _, _ = _group_b_lane_order()
        d_qk = MLA_NOPE + MLA_ROPE
        wuq_p = jnp.pad(
            jnp.concatenate([_gather_cols(w_uq_b[l][:, h * d_qk:(h + 1) * d_qk], src_b)
                             for h in range(N_HEADS_B)], axis=-1),
            ((0, CQ_PAD - MLA_Q_RANK), (0, 0))).astype(BF16)
        wukv = w_ukv_b[l].reshape(MLA_KV_RANK, N_HEADS_B, MLA_NOPE + MLA_V)
        k_cols = [_gather_cols(wukv[:, h, :MLA_NOPE], np.where(is_rope, -1, src_b))
                  for h in range(N_HEADS_B)]
        v_cols = [_pad_groups(wukv[:, h, MLA_NOPE:], 1, MLA_V, offset=HALF * (h % 2))
                  for h in range(N_HEADS_B)]
        wukv_p = jnp.concatenate(k_cols + v_cols, axis=-1).astype(BF16)

        qa, kta, va, qb, ktb, vb, qc, ktc, vc, nrm = _in_proj(
            xc, row(norm_attn[l]), _prep_w_in(w_in[l]), tabs,
            q_norm_a[l][_group_a_lane_order()[1]].reshape(1, LANES),
            k_norm_a[l][_group_a_lane_order()[1]].reshape(1, LANES),
            jnp.pad(q_a_norm_b[l], (0, CQ_PAD - MLA_Q_RANK)).reshape(1, CQ_PAD),
            row(kv_a_norm_b[l]), wuq_p, wukv_p, B, S)

        nrm = jnp.max(nrm.reshape(B, -1, LANES), axis=1)

        def bounded(q0, nq, q_rep, k0, nk, k_rep, per_pipeline):
            q2 = jnp.repeat(nrm[:, q0:q0 + nq], q_rep, axis=1)
            k2 = jnp.repeat(nrm[:, k0:k0 + nk], k_rep, axis=1)
            ok = q2 * k2 * NRM_MARGIN <= SAFE_LOG2_RANGE ** 2
            return jnp.all(ok.reshape(B, -1, per_pipeline), axis=-1).astype(jnp.int32)

        qk_a = (HEAD_DIM * jnp.max(jnp.abs(q_norm_a[l])) * jnp.max(jnp.abs(k_norm_a[l]))
                * (HEAD_DIM ** -0.5 * LOG2E))
        bounded_a = jnp.broadcast_to(
            (qk_a * NRM_MARGIN <= SAFE_LOG2_RANGE).astype(jnp.int32),
            (B, N_HEADS_A // HEADS_PER_PIPELINE_AB))

        group = N_HEADS_A // N_KV_A
        oa = _attention(
            functools.partial(_attn_ab_kernel, n_heads=N_HEADS_A,
                              per_pipeline=HEADS_PER_PIPELINE_AB, q_index=lambda h: h // 2,
                              kt_index=lambda h: 2 * (h // group) + h % 2,
                              v_index=lambda h: h // group, out_pairs=a_pairs, n_keys=S),
            bounded_a, N_HEADS_A, qa, kta, va, [], N_HEADS_A * HEAD_DIM, B, S)
        same = lambda h: h
        ob = _attention(
            functools.partial(_attn_ab_kernel, n_heads=N_HEADS_B,
                              per_pipeline=HEADS_PER_PIPELINE_AB, q_index=same, kt_index=same,
                              v_index=same,
                              out_pairs=[(2 * j, 2 * j + 1) for j in range(N_HEADS_B // 2)],
                              n_keys=S),
            bounded(NRM_QB, N_HEADS_B, 1, NRM_KB, N_HEADS_B, 1, HEADS_PER_PIPELINE_AB), N_HEADS_B,
            qb, ktb, vb, [], N_HEADS_B * MLA_V, B, S)
        oc = _attention(
            functools.partial(_attn_c_kernel, n_keys=S, lam_init=lam_init),
            bounded(NRM_QC, N_HEADS_C, 2, NRM_KC, 2 * N_HEADS_C, 1, HEADS_PER_PIPELINE),
            2 * N_HEADS_C, qc, ktc, vc,
            [row(lambda_q1_c[l]), row(lambda_k1_c[l]), row(lambda_q2_c[l]), row(lambda_k2_c[l]),
             _twice(subln_c[l])],
            N_HEADS_C * DIFF_V, B, S)

        w_out_l = jnp.concatenate([w_out[l][a_rows], w_out[l][N_HEADS_A * HEAD_DIM:]], axis=0)
        xc = _mlp(xc, oa, ob, oc, w_out_l.astype(BF16), row(norm_ffn[l]), w_up[l].astype(BF16),
                  conv_w[l].reshape(3, 2, D_FF).transpose(1, 0, 2),
                  conv_b[l].reshape(2, 1, D_FF), w_down[l].astype(BF16),
                  row(final_norm), S, final_norm=(l == depth - 1))
    return xc.reshape(B, S, D)
```

```python
import functools
import math

import jax
import jax.numpy as jnp
import numpy as np
from jax import lax
from jax.experimental import pallas as pl
from jax.experimental.pallas import tpu as pltpu

D_MODEL = 1024
GRID_W = 64
HEAD_DIM = 64
N_HEADS_A = 8
N_KV_A = 2
N_HEADS_B = 4
MLA_Q_RANK = 192
MLA_KV_RANK = 128
MLA_NOPE = 64
MLA_ROPE = 32
MLA_V = 64
N_HEADS_C = 4
DIFF_QK = 32
DIFF_V = 64
D_FF = 2816
ROPE_BASE = 10000.0
EPS = 1e-6

LANES = 128
HALF = LANES // 2
A_QUARTER = LANES // 4
BF16_SUBLANES = 16
VMEM_LIMIT_BYTES = 56 * 1024 * 1024

LOG2E = math.log2(math.e)
assert HEAD_DIM == MLA_V == DIFF_V == 2 * DIFF_QK == HALF

OFF_QA = 0
OFF_KA = OFF_QA + N_HEADS_A * HEAD_DIM
OFF_VA = OFF_KA + LANES
OFF_CQ = OFF_VA + LANES
CQ_PAD = 2 * LANES
OFF_CKV = OFF_CQ + CQ_PAD
OFF_KR = OFF_CKV + MLA_KV_RANK
OFF_QC = OFF_KR + LANES
OFF_KC = OFF_QC + N_HEADS_C * HALF
OFF_VC = OFF_KC + N_HEADS_C * HALF
IN_COLS = OFF_VC + N_HEADS_C * HALF

NRM_QB = 0
NRM_KB = NRM_QB + N_HEADS_B
NRM_QC = NRM_KB + N_HEADS_B
NRM_KC = NRM_QC + N_HEADS_C
NRM_GROUPS = 2 * N_HEADS_B + N_HEADS_C
NRM_MARGIN = 1.03
SAFE_LOG2_RANGE = 55.0

F32 = jnp.float32
BF16 = jnp.bfloat16

TM_IN = 512
TM_FFN = 512
FFN_CHUNK = 256
FFN_DOWN_GROUP = 6 * FFN_CHUNK
FFN_DOWN_LAG = 2
TQ = 512
TQ_SUB_AB = 512
TQ_SUB_C = 512
TK = 512
SCORE_LOOKAHEAD = 2
HEADS_PER_PIPELINE = 4
HEADS_PER_PIPELINE_AB = 4


def _dot(a, b):
    return jnp.dot(a, b, preferred_element_type=F32)


def _rms(x, n):
    ms = jnp.sum(x * x, axis=-1, keepdims=True) * (1.0 / n)
    return x * lax.rsqrt(ms + EPS)


def _low_lanes():
    return lax.broadcasted_iota(jnp.int32, (1, LANES), 1) < HALF


def _half_rms(x, low=None):
    low = _low_lanes() if low is None else low
    sq = x * x
    s_lo = jnp.sum(jnp.where(low, sq, 0.0), axis=-1, keepdims=True)
    s_hi = jnp.sum(jnp.where(low, 0.0, sq), axis=-1, keepdims=True)
    return x * lax.rsqrt(jnp.where(low, s_lo, s_hi) * (1.0 / HALF) + EPS)


def _in_kernel(x_ref, gat_ref, w_ref, cos_a_ref, sin_a_ref, cos_b_ref, sin_b_ref,
               qg_a_ref, kg_a_ref, qg_b_ref, kvg_b_ref, wuq_ref, wukv_ref, sel_ref,
               qa_ref, kta_ref, va_ref, qb_ref, ktb_ref, vb_ref, qc_ref, ktc_ref, vc_ref, nrm_ref):
    x = x_ref[...]
    tm = x.shape[0]
    h = (_rms(x, D_MODEL) * gat_ref[...]).astype(BF16)

    lane = lax.broadcasted_iota(jnp.int32, (1, LANES), 1)
    low = lane < HALF
    row = lax.broadcasted_iota(jnp.int32, (LANES, 1), 0)

    z = _dot(h, w_ref[...])

    def seg(off, width):
        return z[:, off:off + width]

    def rope(y, c, s):
        return y * c + pltpu.roll(y, HALF, axis=1) * s

    def group(z, g):
        return z[:, g * LANES:(g + 1) * LANES]

    squares = []

    def note(y):
        squares.append((y * y).astype(BF16))

    def noted_row_sums(first):
        return _dot(jnp.concatenate(squares[first:], axis=-1),
                    sel_ref[first * LANES:len(squares) * LANES])

    cos_b, sin_b = cos_b_ref[...], sin_b_ref[...]
    cq =(_rms(seg(OFF_CQ, CQ_PAD), MLA_Q_RANK) * qg_b_ref[...]).astype(BF16)
    qb = _dot(cq, wuq_ref[...])
    scale_b = (MLA_NOPE + MLA_ROPE) ** -0.5 * LOG2E
    for hh in range(N_HEADS_B):
        y = rope(group(qb, hh), cos_b, sin_b) * scale_b
        qb_ref[0, hh] = y.astype(BF16)
        note(y)
    ckv = (_rms(seg(OFF_CKV, MLA_KV_RANK), MLA_KV_RANK) * kvg_b_ref[...]).astype(BF16)
    kvb = _dot(ckv, wukv_ref[...])
    kr = rope(seg(OFF_KR, LANES), cos_b, sin_b)
    for hh in range(N_HEADS_B):
        y = group(kvb, hh) + kr
        ktb_ref[0, hh] = y.T.astype(BF16)
        note(y)
        ones_other_half = jnp.where(low, float(hh % 2), float(1 - hh % 2))
        vb_ref[0, hh] = (group(kvb, N_HEADS_B + hh) + ones_other_half).astype(BF16)
    n2 = noted_row_sums(0)
    n_noted_b = len(squares)

    zq = seg(OFF_QC, N_HEADS_C * HALF)
    zk = seg(OFF_KC, N_HEADS_C * HALF)
    zv = seg(OFF_VC, N_HEADS_C * HALF)
    k_groups = []
    for p in range(N_HEADS_C // 2):
        y = group(zq, p) * (DIFF_QK ** -0.5 * LOG2E)
        qc_ref[0, p] = y.astype(BF16)
        note(y)
        k_groups.append(group(zk, p))
        kt = group(zk, p).T
        for sub in range(4):
            mine = (row >= sub * DIFF_QK) & (row < (sub + 1) * DIFF_QK)
            ktc_ref[0, 4 * p + sub, 0] = jnp.where(mine, kt, 0.0).astype(BF16)
        vz = group(zv, p)
        vc_ref[0, 2 * p] = jnp.where(low, vz, 1.0).astype(BF16)
        vc_ref[0, 2 * p + 1] = jnp.where(low, 1.0, vz).astype(BF16)
    for kz in k_groups:
        note(kz)
    n2 = n2 + noted_row_sums(n_noted_b)
    nrm_ref[0] = jnp.max(n2, axis=0, keepdims=True)

    cos_a, sin_a = cos_a_ref[...], sin_a_ref[...]
    first_head = (lane & A_QUARTER) == 0
    zq = seg(OFF_QA, N_HEADS_A * HEAD_DIM)
    for p in range(N_HEADS_A // 2):
        y = rope(_half_rms(group(zq, p), first_head) * qg_a_ref[...], cos_a, sin_a)
        qa_ref[0, p] = (y * (HEAD_DIM ** -0.5 * LOG2E)).astype(BF16)
    y = rope(_half_rms(seg(OFF_KA, LANES), first_head) * kg_a_ref[...], cos_a, sin_a)
    kt = y.T.astype(BF16)
    quarter = [kt[j * A_QUARTER:(j + 1) * A_QUARTER] for j in range(4)]
    zeros = jnp.zeros((A_QUARTER, tm), BF16)
    for g in range(N_KV_A):
        k1, k2 = quarter[g], quarter[2 + g]
        kta_ref[0, 2 * g] = jnp.concatenate([k1, zeros, k2, zeros], axis=0)
        kta_ref[0, 2 * g + 1] = jnp.concatenate([zeros, k1, zeros, k2], axis=0)
    zv = seg(OFF_VA, LANES)
    va_ref[0, 0] = jnp.where(low, zv, 1.0).astype(BF16)
    va_ref[0, 1] = jnp.where(low, 1.0, zv).astype(BF16)


def _norm_routing():
    sel = np.zeros((NRM_GROUPS, LANES, LANES), np.float32)
    g = 0
    for slot in ([NRM_QB + h for h in range(N_HEADS_B)] + [NRM_KB + h for h in range(N_HEADS_B)]):
        sel[g, :, slot] = 1.0
        g += 1
    for p in range(N_HEADS_C // 2):
        for f in range(2):
            sel[g, f * HALF:(f + 1) * HALF, NRM_QC + 2 * p + f] = 1.0
        g += 1
    for p in range(N_HEADS_C // 2):
        for sub in range(4):
            sel[g, sub * DIFF_QK:(sub + 1) * DIFF_QK, NRM_KC + 4 * p + sub] = 1.0
        g += 1
    assert g == NRM_GROUPS
    return jnp.asarray(sel.reshape(NRM_GROUPS * LANES, LANES), BF16)


def _in_proj(x2d, gat, w_in_r, tabs, qg_a, kg_a, qg_b, kvg_b, wuq_p, wukv_p, B, S):
    T = B * S
    tm = TM_IN
    nst = S // tm
    const = lambda shape: pl.BlockSpec(shape, lambda i: (0,) * len(shape))
    tab = pl.BlockSpec((tm, LANES), lambda i: (i % nst, 0))
    hm = lambda H: pl.BlockSpec((1, H, tm, LANES), lambda i: (i // nst, 0, i % nst, 0))
    hmt = lambda H: pl.BlockSpec((1, H, LANES, tm), lambda i: (i // nst, 0, 0, i % nst))
    sds = lambda H: jax.ShapeDtypeStruct((B, H, S, LANES), BF16)
    sdt = lambda H: jax.ShapeDtypeStruct((B, H, LANES, S), BF16)
    return pl.pallas_call(
        _in_kernel,
        grid=(T // tm,),
        in_specs=[pl.BlockSpec((tm, D_MODEL), lambda i: (i, 0)),
                  const((1, D_MODEL)), const((D_MODEL, IN_COLS)),
                  tab, tab, tab, tab,
                  const((1, LANES)), const((1, LANES)), const((1, CQ_PAD)), const((1, MLA_KV_RANK)),
                  const((CQ_PAD, N_HEADS_B * LANES)), const((MLA_KV_RANK, 2 * N_HEADS_B * LANES)),
                  const((NRM_GROUPS * LANES, LANES))],
        out_specs=[hm(N_HEADS_A // 2), hmt(2 * N_KV_A), hm(N_KV_A),
                   hm(N_HEADS_B), hmt(N_HEADS_B), hm(N_HEADS_B),
                   hm(N_HEADS_C // 2),
                   pl.BlockSpec((1, 2 * N_HEADS_C, 1, LANES, tm),
                                lambda i: (i // nst, 0, i % nst, 0, 0)),
                   hm(N_HEADS_C),
                   pl.BlockSpec((1, 1, LANES), lambda i: (i, 0, 0))],
        out_shape=[sds(N_HEADS_A // 2), sdt(2 * N_KV_A), sds(N_KV_A),
                   sds(N_HEADS_B), sdt(N_HEADS_B), sds(N_HEADS_B),
                   sds(N_HEADS_C // 2),
                   jax.ShapeDtypeStruct((B, 2 * N_HEADS_C, S // tm, LANES, tm), BF16),
                   sds(N_HEADS_C),
                   jax.ShapeDtypeStruct((T // tm, 1, LANES), F32)],
        compiler_params=pltpu.CompilerParams(dimension_semantics=("arbitrary",),
                                             vmem_limit_bytes=VMEM_LIMIT_BYTES),
        name="in_proj",
    )(x2d, gat, w_in_r, *tabs, qg_a, kg_a, qg_b, kvg_b, wuq_p, wukv_p, _norm_routing())


def _lane_group_max(s):
    m = s[:, :LANES]
    for j in range(1, s.shape[1] // LANES):
        m = jnp.maximum(m, s[:, j * LANES:(j + 1) * LANES])
    return m


def _pv_bounded(streams, n_chunks):
    items = [(r, i) for r in range(n_chunks) for i in range(len(streams))]
    accs = [None] * len(streams)
    in_flight = {}
    for k in range(len(items) + SCORE_LOOKAHEAD):
        if k < len(items):
            r, i = items[k]
            scores, _, row_off = streams[i]
            s, off = scores(r), row_off(r)
            in_flight[k] = s if off is None else s + off
        if k >= SCORE_LOOKAHEAD:
            r, i = items[k - SCORE_LOOKAHEAD]
            d = _dot(jnp.exp2(in_flight.pop(k - SCORE_LOOKAHEAD)).astype(BF16), streams[i][1](r))
            accs[i] = d if accs[i] is None else accs[i] + d
    return accs


def _pv_exact(streams, n_chunks, s_ref):
    rows = s_ref.shape[0] // len(streams)
    maxes = []
    for i, (scores, _, row_off) in enumerate(streams):
        m_run = None
        for r in range(n_chunks):
            s = scores(r)
            s_ref[i * rows:(i + 1) * rows, r * TK:(r + 1) * TK] = s
            m, off = _lane_group_max(s), row_off(r)
            if off is not None:
                m = m + off
            m_run = m if m_run is None else jnp.maximum(m_run, m)
        maxes.append(jnp.max(m_run, axis=-1, keepdims=True))
    accs = []
    for i, (_, values, row_off) in enumerate(streams):
        acc = None
        for r in range(n_chunks):
            off = row_off(r)
            shift = maxes[i] if off is None else maxes[i] - off
            p = jnp.exp2(s_ref[i * rows:(i + 1) * rows, r * TK:(r + 1) * TK] - shift)
            d = _dot(p.astype(BF16), values(r))
            acc = d if acc is None else acc + d
        accs.append(acc)
    return accs


def _run_heads(bounded, heads, head_streams, n_chunks, s_ref, acc_ref, roll_exact=False):
    @pl.when(bounded)
    def _():
        shared = {}
        per_head = [head_streams(h, shared) for h in heads]
        n_sub = len(per_head[0])
        accs = _pv_bounded([s for streams in per_head for s in streams], n_chunks)
        for k, h in enumerate(heads):
            acc_ref[h] = jnp.concatenate(accs[k * n_sub:(k + 1) * n_sub], axis=0)

    @pl.when(jnp.logical_not(bounded))
    def _():
        def one_head(h):
            acc_ref[h] = jnp.concatenate(_pv_exact(head_streams(h, {}), n_chunks, s_ref), axis=0)

        if roll_exact:
            lax.fori_loop(0, len(heads), lambda k, c: (one_head(heads[0] + k), c)[1], 0)
        else:
            for h in heads:
                one_head(h)


def _normalised_pair(acc_low, acc_high):
    low = _low_lanes()
    num = jnp.where(low, acc_low, acc_high)
    den = pltpu.roll(jnp.where(low, acc_high, acc_low), HALF, axis=1)
    return num / den


def _attn_ab_kernel(bounded_ref, q_ref, kt_ref, v_ref, o_ref, s_ref, acc_ref, *,
                    n_heads, per_pipeline, q_index, kt_index, v_index, out_pairs, n_keys):
    n_chunks = n_keys // TK

    def head_streams(h, shared):
        q = q_ref[0, q_index(h)]
        return [(lambda r, qs=q[r0:r0 + TQ_SUB_AB]:
                 _dot(qs, kt_ref[0, kt_index(h), :, r * TK:(r + 1) * TK]),
                 lambda r: v_ref[0, v_index(h), r * TK:(r + 1) * TK, :],
                 lambda r: None) for r0 in range(0, TQ, TQ_SUB_AB)]

    def head_group(j, carry):
        heads = [per_pipeline * j + k for k in range(per_pipeline)]
        _run_heads(bounded_ref[pl.program_id(0), j] != 0, heads, head_streams, n_chunks, s_ref,
                   acc_ref, roll_exact=True)
        return carry

    if n_heads == per_pipeline:
        head_group(0, None)
    else:
        lax.fori_loop(0, n_heads // per_pipeline, head_group, 0)
    o_ref[...] = jnp.concatenate([_normalised_pair(acc_ref[a], acc_ref[b]) for a, b in out_pairs],
                                 axis=-1).astype(BF16)


def _attn_c_kernel(bounded_ref, q_ref, kt_ref, v_ref, lq1_ref, lk1_ref, lq2_ref, lk2_ref,
                   subln_ref, o_ref, s_ref, acc_ref, *, n_keys, lam_init):
    assert TQ == TK
    nc = n_keys // TK
    cd = pl.program_id(1)
    col = lax.broadcasted_iota(jnp.int32, (1, TK), 1).astype(F32)
    row = lax.broadcasted_iota(jnp.int32, (TQ_SUB_C, 1), 0).astype(F32)
    subs = [(r, r + TQ_SUB_C) for r in range(0, TQ, TQ_SUB_C)]
    neg_dist = [-jnp.abs((lax.broadcasted_iota(jnp.int32, (TQ_SUB_C, TK), 0) + r0
                          - lax.broadcasted_iota(jnp.int32, (TQ_SUB_C, TK), 1)).astype(F32))
                for r0, _ in subs]

    chunk = [(cd + r) % nc for r in range(nc)]

    def head_bias(hh):
        slope = jnp.float32(LOG2E * 2.0 ** (-8.0 * N_HEADS_C / N_HEADS_C))
        for k in range(N_HEADS_C - 2, -1, -1):
            slope = jnp.where(hh == k, LOG2E * 2.0 ** (-8.0 * (k + 1) / N_HEADS_C), slope)
        key_term, row_off = [None], [[None] * len(subs)]
        for r in range(1, nc):
            side = jnp.where(chunk[r] < cd, slope, -slope)
            key_term.append(col * side)
            base = -slope * (jnp.abs(chunk[r] - cd) * TK).astype(F32)
            row_off.append([base - side * (row + float(r0)) for r0, _ in subs])
        return slope, key_term, row_off

    def head_pair(p, carry):
        def streams_of(k, shared):
            idx, hh = 4 * p + k, 2 * p + k // 2
            if isinstance(k, int):
                if k // 2 not in shared:
                    shared[k // 2] = head_bias(hh)
                slope, key_term, row_off = shared[k // 2]
            else:
                slope, key_term, row_off = head_bias(hh)

            def scores(r, si):
                s = _dot(q_ref[0, p][subs[si][0]:subs[si][1]], kt_ref[0, idx, chunk[r]])
                return s + (neg_dist[si] * slope if r == 0 else key_term[r])

            return [(functools.partial(scores, si=si),
                     lambda r: v_ref[0, hh, pl.ds(pl.multiple_of(chunk[r] * TK, TK), TK), :],
                     lambda r, si=si: row_off[r][si]) for si in range(len(subs))]

        _run_heads(bounded_ref[pl.program_id(0), p] != 0, list(range(4)), streams_of, nc, s_ref,
                   acc_ref.at[pl.ds(4 * p, 4)], roll_exact=True)
        return carry

    assert HEADS_PER_PIPELINE == 4
    lax.fori_loop(0, N_HEADS_C // 2, head_pair, 0)

    lam = (jnp.exp(jnp.sum(lq1_ref[...] * lk1_ref[...], axis=-1, keepdims=True))
           - jnp.exp(jnp.sum(lq2_ref[...] * lk2_ref[...], axis=-1, keepdims=True)) + lam_init)
    outs = []
    for p in range(N_HEADS_C // 2):
        o = (_normalised_pair(acc_ref[4 * p], acc_ref[4 * p + 2])
             - lam * _normalised_pair(acc_ref[4 * p + 1], acc_ref[4 * p + 3]))
        outs.append(_half_rms(o) * subln_ref[...] * (1.0 - lam_init))
    o_ref[...] = jnp.concatenate(outs, axis=-1).astype(BF16)


def _attention(kernel_fn, bounded, n_acc, q, kt, v, extras, out_width, B, S):
    nq = S // TQ
    hq, hv = q.shape[1], v.shape[1]
    extra_specs = [pl.BlockSpec(e.shape, lambda b, i, f, n=e.ndim: (0,) * n) for e in extras]
    return pl.pallas_call(
        kernel_fn,
        grid_spec=pltpu.PrefetchScalarGridSpec(
            num_scalar_prefetch=1,
            grid=(B, nq),
            in_specs=[pl.BlockSpec((1, hq, TQ, LANES), lambda b, i, f: (b, 0, i, 0)),
                      pl.BlockSpec((1,) + kt.shape[1:],
                                   lambda b, i, f, n=kt.ndim: (b,) + (0,) * (n - 1)),
                      pl.BlockSpec((1, hv, S, LANES), lambda b, i, f: (b, 0, 0, 0))] + extra_specs,
            out_specs=pl.BlockSpec((TQ, out_width), lambda b, i, f: (b * nq + i, 0)),
            scratch_shapes=[pltpu.VMEM((TQ, S), F32),
                            pltpu.VMEM((n_acc, TQ, LANES), F32)]),
        out_shape=jax.ShapeDtypeStruct((B * S, out_width), BF16),
        compiler_params=pltpu.CompilerParams(dimension_semantics=("arbitrary", "arbitrary"),
                                             vmem_limit_bytes=VMEM_LIMIT_BYTES),
        name=kernel_fn.func.__name__.strip("_"),
    )(bounded, q, kt, v, *extras)


def _mlp_kernel(xp_ref, x_ref, xn_ref, ap_ref, a_ref, an_ref, bp_ref, b_ref, bn_ref,
                cp_ref, c_ref, cn_ref, wo_ref, g_ref, wu_ref, cw_ref, cb_ref, wd_ref, fg_ref,
                o_ref, act_ref, *, tiles_per_seq, final_norm):
    i = pl.program_id(0)
    tm = x_ref.shape[0]
    halo = xp_ref.shape[0]

    def rows(p, m, n):
        return jnp.concatenate([p[...], m[...], n[...]], axis=0)

    o_ext = jnp.concatenate([rows(ap_ref, a_ref, an_ref), rows(bp_ref, b_ref, bn_ref),
                             rows(cp_ref, c_ref, cn_ref)], axis=1)
    x1 = rows(xp_ref, x_ref, xn_ref) + _dot(o_ext, wo_ref[...])
    h = (_rms(x1, D_MODEL) * g_ref[...]).astype(BF16)
    at_start = (i % tiles_per_seq) == 0
    at_end = (i % tiles_per_seq) == tiles_per_seq - 1
    zeros = jnp.zeros((halo, D_MODEL), BF16)
    hext = jnp.concatenate([jnp.where(at_start, zeros, h[:halo]), h[halo:halo + tm],
                            jnp.where(at_end, zeros, h[halo + tm:])], axis=0)

    def branch(k, c0):
        u = _dot(hext, wu_ref[:, k * D_FF + c0:k * D_FF + c0 + FFN_CHUNK])
        cw = cw_ref[k, :, c0:c0 + FFN_CHUNK]
        n = u.shape[0]
        prev = pltpu.roll(u, 1, axis=0)[halo:halo + tm]
        nxt = pltpu.roll(u, n - 1, axis=0)[halo:halo + tm]
        return (prev * cw[0:1] + u[halo:halo + tm] * cw[1:2] + nxt * cw[2:3]
                + cb_ref[k, :, c0:c0 + FFN_CHUNK])

    y = x1[halo:halo + tm]
    done = 0
    for c0 in range(0, D_FF, FFN_CHUNK):
        g = branch(0, c0)
        val = branch(1, c0)
        act_ref[:, c0:c0 + FFN_CHUNK] = (g / (1.0 + jnp.exp(-g)) * val).astype(BF16)
        ready = c0 + FFN_CHUNK - FFN_DOWN_LAG * FFN_CHUNK
        if ready - done >= FFN_DOWN_GROUP:
            y = y + _dot(act_ref[:, done:done + FFN_DOWN_GROUP], wd_ref[done:done + FFN_DOWN_GROUP])
            done += FFN_DOWN_GROUP
    y = y + _dot(act_ref[:, done:], wd_ref[done:])
    if final_norm:
        y = _rms(y, D_MODEL) * fg_ref[...]
    o_ref[...] = y


def _mlp(x2d, oa, ob, oc, w_out, g_ffn, w_up, conv_w, conv_b, w_down, final_g, S, final_norm):
    T = x2d.shape[0]
    tm, halo = TM_FFN, BF16_SUBLANES
    per = tm // halo
    last = T // halo - 1
    resident = lambda shape: pl.BlockSpec(shape, lambda i: (0,) * len(shape),
                                          pipeline_mode=pl.Buffered(1))

    def with_halos(width):
        return [pl.BlockSpec((halo, width), lambda i: (jnp.maximum(i * per - 1, 0), 0)),
                pl.BlockSpec((tm, width), lambda i: (i, 0)),
                pl.BlockSpec((halo, width), lambda i: (jnp.minimum((i + 1) * per, last), 0))]

    return pl.pallas_call(
        functools.partial(_mlp_kernel, tiles_per_seq=S // tm, final_norm=final_norm),
        grid=(T // tm,),
        in_specs=(with_halos(D_MODEL) + with_halos(oa.shape[1]) + with_halos(ob.shape[1])
                  + with_halos(oc.shape[1])
                  + [resident(w_out.shape), resident((1, D_MODEL)),
                     resident((D_MODEL, 2 * D_FF)), resident((2, 3, D_FF)), resident((2, 1, D_FF)),
                     resident((D_FF, D_MODEL)), resident((1, D_MODEL))]),
        out_specs=pl.BlockSpec((tm, D_MODEL), lambda i: (i, 0)),
        out_shape=jax.ShapeDtypeStruct((T, D_MODEL), F32),
        scratch_shapes=[pltpu.VMEM((tm, D_FF), BF16)],
        compiler_params=pltpu.CompilerParams(dimension_semantics=("arbitrary",),
                                             vmem_limit_bytes=VMEM_LIMIT_BYTES),
        name="out_proj_conv_mlp",
    )(x2d, x2d, x2d, oa, oa, oa, ob, ob, ob, oc, oc, oc, w_out, g_ffn,
      w_up, conv_w, conv_b, w_down, final_g)


def _pad_groups(w, n_groups, width, offset=0):
    r = w.shape[0]
    w = w.reshape(r, n_groups, width)
    w = jnp.pad(w, ((0, 0), (0, 0), (offset, LANES - width - offset)))
    return w.reshape(r, n_groups * LANES)


def _group_a_lane_order():
    nf = HEAD_DIM // 4
    lanes = np.arange(LANES)
    second, r = lanes // HALF, lanes % HALF
    head, rr = r // A_QUARTER, r % A_QUARTER
    block, freq = rr // nf, rr % nf
    dim = block * 2 * nf + second * nf + freq
    return head, dim, block, freq, second


def _group_b_lane_order():
    nf = MLA_ROPE // 4
    src = np.full(LANES, -1)
    block = np.zeros(LANES, np.int64)
    freq = np.zeros(LANES, np.int64)
    second = np.zeros(LANES, np.int64)
    is_rope = np.zeros(LANES, bool)
    for e in range(2):
        for j in range(2 * nf):
            lane = e * HALF + j
            is_rope[lane], block[lane], freq[lane], second[lane] = True, j // nf, j % nf, e
            src[lane] = MLA_NOPE + (j // nf) * 2 * nf + e * nf + j % nf
    n_low = HALF - 2 * nf
    src[2 * nf:HALF] = np.arange(n_low)
    src[HALF + 2 * nf:HALF + 2 * nf + MLA_NOPE - n_low] = np.arange(n_low, MLA_NOPE)
    return src, is_rope, block, freq, second


def _gather_cols(w, src):
    return jnp.where(jnp.asarray(src >= 0), w[:, np.maximum(src, 0)], 0.0)


def _rope_tables(S):
    t = jnp.arange(S, dtype=jnp.int32)
    pos = jnp.stack([(t // GRID_W).astype(F32), (t % GRID_W).astype(F32)], axis=0)

    def table(nf, block, freq, second, is_rope):
        inv = ROPE_BASE ** (-jnp.arange(nf, dtype=F32) / nf)
        ang = pos[block].T * inv[freq]
        sign = jnp.asarray(2.0 * second - 1.0, F32)
        cos = jnp.where(jnp.asarray(is_rope), jnp.cos(ang), 1.0)
        sin = jnp.where(jnp.asarray(is_rope), jnp.sin(ang) * sign, 0.0)
        return cos, sin

    _, _, block, freq, second = _group_a_lane_order()
    cos_a, sin_a = table(HEAD_DIM // 4, block, freq, second, np.ones(LANES, bool))
    _, is_rope, block, freq, second = _group_b_lane_order()
    cos_b, sin_b = table(MLA_ROPE // 4, block, freq, second, is_rope)
    return cos_a, sin_a, cos_b, sin_b


def _prep_w_in(w):
    n_qa, n_ka = N_HEADS_A * HEAD_DIM, N_KV_A * HEAD_DIM
    a_end = n_qa + 2 * n_ka
    b_end = a_end + MLA_Q_RANK + MLA_KV_RANK + MLA_ROPE
    head, dim, _, _, _ = _group_a_lane_order()
    pair_cols = head * HEAD_DIM + dim
    qa_cols = np.concatenate([p * LANES + pair_cols for p in range(N_HEADS_A // 2)])
    cq = w[:, a_end:a_end + MLA_Q_RANK]
    ckv = w[:, a_end + MLA_Q_RANK:a_end + MLA_Q_RANK + MLA_KV_RANK]
    kr = w[:, b_end - MLA_ROPE:b_end]
    src_b, is_rope, _, _, _ = _group_b_lane_order()
    out = jnp.concatenate([
        w[:, qa_cols], w[:, n_qa + pair_cols], w[:, n_qa + n_ka:a_end],
        jnp.pad(cq, ((0, 0), (0, CQ_PAD - MLA_Q_RANK))),
        ckv,
        _gather_cols(kr, np.where(is_rope, src_b - MLA_NOPE, -1)),
        w[:, b_end:],
    ], axis=-1).astype(BF16)
    assert out.shape[1] == IN_COLS
    return out


def _twice(v):
    return jnp.tile(v, 2).reshape(1, 2 * v.shape[0])


def kernel(x, norm_attn, w_in, q_norm_a, k_norm_a, q_a_norm_b, w_uq_b, kv_a_norm_b, w_ukv_b,
           lambda_q1_c, lambda_k1_c, lambda_q2_c, lambda_k2_c, subln_c, w_out,
           norm_ffn, w_up, conv_w, conv_b, w_down, final_norm):
    B, S, D = x.shape
    depth = w_in.shape[0]
    assert D == D_MODEL and S % max(TM_IN, TM_FFN, TQ, TK) == 0 and S % GRID_W == 0
    assert TM_IN == TK
    T = B * S

    tabs = _rope_tables(S)
    row = lambda v: v.reshape(1, -1)
    a_pairs = [(j, j + N_HEADS_A // 2) for j in range(N_HEADS_A // 2)]
    a_rows = np.concatenate([np.arange(h * HEAD_DIM, (h + 1) * HEAD_DIM)
                             for pair in a_pairs for h in pair])
    xc = x.reshape(T, D)
    for l in range(depth):
        lam_init = 0.8 - 0.6 * math.exp(-0.3 * l)
        src_b, is_rope, _, _, _ = _group_b_lane_order()
        d_qk = MLA_NOPE + MLA_ROPE
        wuq_p = jnp.pad(
            jnp.concatenate([_gather_cols(w_uq_b[l][:, h * d_qk:(h + 1) * d_qk], src_b)
                             for h in range(N_HEADS_B)], axis=-1),
            ((0, CQ_PAD - MLA_Q_RANK), (0, 0))).astype(BF16)
        wukv = w_ukv_b[l].reshape(MLA_KV_RANK, N_HEADS_B, MLA_NOPE + MLA_V)
        k_cols = [_gather_cols(wukv[:, h, :MLA_NOPE], np.where(is_rope, -1, src_b))
                  for h in range(N_HEADS_B)]
        v_cols = [_pad_groups(wukv[:, h, MLA_NOPE:], 1, MLA_V, offset=HALF * (h % 2))
                  for h in range(N_HEADS_B)]
        wukv_p = jnp.concatenate(k_cols + v_cols, axis=-1).astype(BF16)

        qa, kta, va, qb, ktb, vb, qc, ktc, vc, nrm = _in_proj(
            xc, row(norm_attn[l]), _prep_w_in(w_in[l]), tabs,
            q_norm_a[l][_group_a_lane_order()[1]].reshape(1, LANES),
            k_norm_a[l][_group_a_lane_order()[1]].reshape(1, LANES),
            jnp.pad(q_a_norm_b[l], (0, CQ_PAD - MLA_Q_RANK)).reshape(1, CQ_PAD),
            row(kv_a_norm_b[l]), wuq_p, wukv_p, B, S)

        nrm = jnp.max(nrm.reshape(B, -1, LANES), axis=1)

        def bounded(q0, nq, q_rep, k0, nk, k_rep, per_pipeline):
            q2 = jnp.repeat(nrm[:, q0:q0 + nq], q_rep, axis=1)
            k2 = jnp.repeat(nrm[:, k0:k0 + nk], k_rep, axis=1)
            ok = q2 * k2 * NRM_MARGIN <= SAFE_LOG2_RANGE ** 2
            return jnp.all(ok.reshape(B, -1, per_pipeline), axis=-1).astype(jnp.int32)

        qk_a = (HEAD_DIM * jnp.max(jnp.abs(q_norm_a[l])) * jnp.max(jnp.abs(k_norm_a[l]))
                * (HEAD_DIM ** -0.5 * LOG2E))
        bounded_a = jnp.broadcast_to(
            (qk_a * NRM_MARGIN <= SAFE_LOG2_RANGE).astype(jnp.int32),
            (B, N_HEADS_A // HEADS_PER_PIPELINE_AB))

        group = N_HEADS_A // N_KV_A
        oa = _attention(
            functools.partial(_attn_ab_kernel, n_heads=N_HEADS_A,
                              per_pipeline=HEADS_PER_PIPELINE_AB, q_index=lambda h: h // 2,
                              kt_index=lambda h: 2 * (h // group) + h % 2,
                              v_index=lambda h: h // group, out_pairs=a_pairs, n_keys=S),
            bounded_a, N_HEADS_A, qa, kta, va, [], N_HEADS_A * HEAD_DIM, B, S)
        same = lambda h: h
        ob = _attention(
            functools.partial(_attn_ab_kernel, n_heads=N_HEADS_B,
                              per_pipeline=HEADS_PER_PIPELINE_AB, q_index=same, kt_index=same,
                              v_index=same,
                              out_pairs=[(2 * j, 2 * j + 1) for j in range(N_HEADS_B // 2)],
                              n_keys=S),
            bounded(NRM_QB, N_HEADS_B, 1, NRM_KB, N_HEADS_B, 1, HEADS_PER_PIPELINE_AB), N_HEADS_B,
            qb, ktb, vb, [], N_HEADS_B * MLA_V, B, S)
        oc = _attention(
            functools.partial(_attn_c_kernel, n_keys=S, lam_init=lam_init),
            bounded(NRM_QC, N_HEADS_C, 2, NRM_KC, 2 * N_HEADS_C, 1, HEADS_PER_PIPELINE),
            2 * N_HEADS_C, qc, ktc, vc,
            [row(lambda_q1_c[l]), row(lambda_k1_c[l]), row(lambda_q2_c[l]), row(lambda_k2_c[l]),
             _twice(subln_c[l])],
            N_HEADS_C * DIFF_V, B, S)

        w_out_l = jnp.concatenate([w_out[l][a_rows], w_out[l][N_HEADS_A * HEAD_DIM:]], axis=0)
        xc = _mlp(xc, oa, ob, oc, w_out_l.astype(BF16), row(norm_ffn[l]), w_up[l].astype(BF16),
                  conv_w[l].reshape(3, 2, D_FF).transpose(1, 0, 2),
                  conv_b[l].reshape(2, 1, D_FF), w_down[l].astype(BF16),
                  row(final_norm), S, final_norm=(l == depth - 1))
    return xc.reshape(B, S, D)
```

```python
import functools
import math

import jax
import jax.numpy as jnp
import numpy as np
from jax import lax
from jax.experimental import pallas as pl
from jax.experimental.pallas import tpu as pltpu

D_MODEL = 1024
GRID_W = 64
HEAD_DIM = 64
N_HEADS_A = 8
N_KV_A = 2
N_HEADS_B = 4
MLA_Q_RANK = 192
MLA_KV_RANK = 128
MLA_NOPE = 64
MLA_ROPE = 32
MLA_V = 64
N_HEADS_C = 4
DIFF_QK = 32
DIFF_V = 64
D_FF = 2816
ROPE_BASE = 10000.0
EPS = 1e-6

LANES = 128
HALF = LANES // 2
A_QUARTER = LANES // 4
BF16_SUBLANES = 16
VMEM_LIMIT_BYTES = 56 * 1024 * 1024

LOG2E = math.log2(math.e)
assert HEAD_DIM == MLA_V == DIFF_V == 2 * DIFF_QK == HALF

OFF_QA = 0
OFF_KA = OFF_QA + N_HEADS_A * HEAD_DIM
OFF_VA = OFF_KA + LANES
OFF_CQ = OFF_VA + LANES
CQ_PAD = 2 * LANES
OFF_CKV = OFF_CQ + CQ_PAD
OFF_KR = OFF_CKV + MLA_KV_RANK
OFF_QC = OFF_KR + LANES
OFF_KC = OFF_QC + N_HEADS_C * HALF
OFF_VC = OFF_KC + N_HEADS_C * HALF
IN_COLS = OFF_VC + N_HEADS_C * HALF

NRM_QB = 0
NRM_KB = NRM_QB + N_HEADS_B
NRM_QC = NRM_KB + N_HEADS_B
NRM_KC = NRM_QC + N_HEADS_C
NRM_GROUPS = 2 * N_HEADS_B + N_HEADS_C
NRM_MARGIN = 1.03
SAFE_LOG2_RANGE = 55.0

F32 = jnp.float32
BF16 = jnp.bfloat16

TM_IN = 512
TM_FFN = 512
FFN_CHUNK = 256
FFN_DOWN_GROUP = 6 * FFN_CHUNK
FFN_DOWN_LAG = 2
TQ = 512
TQ_SUB_AB = 512
TQ_SUB_C = 256
TK = 512
SCORE_LOOKAHEAD = 2
HEADS_PER_PIPELINE = 4
HEADS_PER_PIPELINE_AB = 4


def _dot(a, b):
    return jnp.dot(a, b, preferred_element_type=F32)


def _rms(x, n):
    ms = jnp.sum(x * x, axis=-1, keepdims=True) * (1.0 / n)
    return x * lax.rsqrt(ms + EPS)


def _low_lanes():
    return lax.broadcasted_iota(jnp.int32, (1, LANES), 1) < HALF


def _half_rms(x):
    low = _low_lanes()
    sq = x * x
    s_lo = jnp.sum(jnp.where(low, sq, 0.0), axis=-1, keepdims=True)
    s_hi = jnp.sum(jnp.where(low, 0.0, sq), axis=-1, keepdims=True)
    return x * lax.rsqrt(jnp.where(low, s_lo, s_hi) * (1.0 / HALF) + EPS)


def _in_kernel(x_ref, gat_ref, w_ref, cos_a_ref, sin_a_ref, cos_b_ref, sin_b_ref,
               qg_a_ref, kg_a_ref, qg_b_ref, kvg_b_ref, wuq_ref, wukv_ref, sel_ref, same_head_ref,
               qa_ref, kta_ref, va_ref, qb_ref, ktb_ref, vb_ref, qc_ref, ktc_ref, vc_ref, nrm_ref):
    x = x_ref[...]
    tm = x.shape[0]
    h = (_rms(x, D_MODEL) * gat_ref[...]).astype(BF16)

    lane = lax.broadcasted_iota(jnp.int32, (1, LANES), 1)
    low = lane < HALF
    row = lax.broadcasted_iota(jnp.int32, (LANES, 1), 0)

    z = _dot(h, w_ref[...])

    def seg(off, width):
        return z[:, off:off + width]

    def rope(y, c, s):
        return y * c + pltpu.roll(y, HALF, axis=1) * s

    def group(z, g):
        return z[:, g * LANES:(g + 1) * LANES]

    squares = []

    def note(y):
        squares.append((y * y).astype(BF16))

    def noted_row_sums(first):
        return _dot(jnp.concatenate(squares[first:], axis=-1),
                    sel_ref[first * LANES:len(squares) * LANES])

    cos_b, sin_b = cos_b_ref[...], sin_b_ref[...]
    cq =(_rms(seg(OFF_CQ, CQ_PAD), MLA_Q_RANK) * qg_b_ref[...]).astype(BF16)
    qb = _dot(cq, wuq_ref[...])
    scale_b = (MLA_NOPE + MLA_ROPE) ** -0.5 * LOG2E
    for hh in range(N_HEADS_B):
        y = rope(group(qb, hh), cos_b, sin_b) * scale_b
        qb_ref[0, hh] = y.astype(BF16)
        note(y)
    ckv = (_rms(seg(OFF_CKV, MLA_KV_RANK), MLA_KV_RANK) * kvg_b_ref[...]).astype(BF16)
    kvb = _dot(ckv, wukv_ref[...])
    kr = rope(seg(OFF_KR, LANES), cos_b, sin_b)
    for hh in range(N_HEADS_B):
        y = group(kvb, hh) + kr
        ktb_ref[0, hh] = y.T.astype(BF16)
        note(y)
        ones_other_half = jnp.where(low, float(hh % 2), float(1 - hh % 2))
        vb_ref[0, hh] = (group(kvb, N_HEADS_B + hh) + ones_other_half).astype(BF16)
    n2 = noted_row_sums(0)
    n_noted_b = len(squares)

    zq = seg(OFF_QC, N_HEADS_C * HALF)
    zk = seg(OFF_KC, N_HEADS_C * HALF)
    zv = seg(OFF_VC, N_HEADS_C * HALF)
    k_groups = []
    for p in range(N_HEADS_C // 2):
        y = group(zq, p) * (DIFF_QK ** -0.5 * LOG2E)
        qc_ref[0, p] = y.astype(BF16)
        note(y)
        k_groups.append(group(zk, p))
        kt = group(zk, p).T
        for sub in range(4):
            mine = (row >= sub * DIFF_QK) & (row < (sub + 1) * DIFF_QK)
            ktc_ref[0, 4 * p + sub, 0] = jnp.where(mine, kt, 0.0).astype(BF16)
        vz = group(zv, p)
        vc_ref[0, 2 * p] = jnp.where(low, vz, 1.0).astype(BF16)
        vc_ref[0, 2 * p + 1] = jnp.where(low, 1.0, vz).astype(BF16)
    for kz in k_groups:
        note(kz)
    n2 = n2 + noted_row_sums(n_noted_b)
    nrm_ref[0] = jnp.max(n2, axis=0, keepdims=True)

    cos_a, sin_a = cos_a_ref[...], sin_a_ref[...]

    def head_rms(zg):
        sq = zg * zg
        hi = sq.astype(BF16)
        lo = (sq - hi.astype(F32)).astype(BF16)
        ss = _dot(hi, same_head_ref[...]) + _dot(lo, same_head_ref[...])
        return zg * lax.rsqrt(ss * (1.0 / HEAD_DIM) + EPS)

    zq = seg(OFF_QA, N_HEADS_A * HEAD_DIM)
    for p in range(N_HEADS_A // 2):
        y = rope(head_rms(group(zq, p)) * qg_a_ref[...], cos_a, sin_a)
        qa_ref[0, p] = (y * (HEAD_DIM ** -0.5 * LOG2E)).astype(BF16)
    y = rope(head_rms(seg(OFF_KA, LANES)) * kg_a_ref[...], cos_a, sin_a)
    kt = y.T.astype(BF16)
    quarter = [kt[j * A_QUARTER:(j + 1) * A_QUARTER] for j in range(4)]
    zeros = jnp.zeros((A_QUARTER, tm), BF16)
    for g in range(N_KV_A):
        k1, k2 = quarter[g], quarter[2 + g]
        kta_ref[0, 2 * g] = jnp.concatenate([k1, zeros, k2, zeros], axis=0)
        kta_ref[0, 2 * g + 1] = jnp.concatenate([zeros, k1, zeros, k2], axis=0)
    zv = seg(OFF_VA, LANES)
    va_ref[0, 0] = jnp.where(low, zv, 1.0).astype(BF16)
    va_ref[0, 1] = jnp.where(low, 1.0, zv).astype(BF16)


def _norm_routing():
    sel = np.zeros((NRM_GROUPS, LANES, LANES), np.float32)
    g = 0
    for slot in ([NRM_QB + h for h in range(N_HEADS_B)] + [NRM_KB + h for h in range(N_HEADS_B)]):
        sel[g, :, slot] = 1.0
        g += 1
    for p in range(N_HEADS_C // 2):
        for f in range(2):
            sel[g, f * HALF:(f + 1) * HALF, NRM_QC + 2 * p + f] = 1.0
        g += 1
    for p in range(N_HEADS_C // 2):
        for sub in range(4):
            sel[g, sub * DIFF_QK:(sub + 1) * DIFF_QK, NRM_KC + 4 * p + sub] = 1.0
        g += 1
    assert g == NRM_GROUPS
    return jnp.asarray(sel.reshape(NRM_GROUPS * LANES, LANES), BF16)


def _in_proj(x2d, gat, w_in_r, tabs, qg_a, kg_a, qg_b, kvg_b, wuq_p, wukv_p, B, S):
    T = B * S
    tm = TM_IN
    nst = S // tm
    const = lambda shape: pl.BlockSpec(shape, lambda i: (0,) * len(shape))
    tab = pl.BlockSpec((tm, LANES), lambda i: (i % nst, 0))
    hm = lambda H: pl.BlockSpec((1, H, tm, LANES), lambda i: (i // nst, 0, i % nst, 0))
    hmt = lambda H: pl.BlockSpec((1, H, LANES, tm), lambda i: (i // nst, 0, 0, i % nst))
    sds = lambda H: jax.ShapeDtypeStruct((B, H, S, LANES), BF16)
    sdt = lambda H: jax.ShapeDtypeStruct((B, H, LANES, S), BF16)
    return pl.pallas_call(
        _in_kernel,
        grid=(T // tm,),
        in_specs=[pl.BlockSpec((tm, D_MODEL), lambda i: (i, 0)),
                  const((1, D_MODEL)), const((D_MODEL, IN_COLS)),
                  tab, tab, tab, tab,
                  const((1, LANES)), const((1, LANES)), const((1, CQ_PAD)), const((1, MLA_KV_RANK)),
                  const((CQ_PAD, N_HEADS_B * LANES)), const((MLA_KV_RANK, 2 * N_HEADS_B * LANES)),
                  const((NRM_GROUPS * LANES, LANES)), const((LANES, LANES))],
        out_specs=[hm(N_HEADS_A // 2), hmt(2 * N_KV_A), hm(N_KV_A),
                   hm(N_HEADS_B), hmt(N_HEADS_B), hm(N_HEADS_B),
                   hm(N_HEADS_C // 2),
                   pl.BlockSpec((1, 2 * N_HEADS_C, 1, LANES, tm),
                                lambda i: (i // nst, 0, i % nst, 0, 0)),
                   hm(N_HEADS_C),
                   pl.BlockSpec((1, 1, LANES), lambda i: (i, 0, 0))],
        out_shape=[sds(N_HEADS_A // 2), sdt(2 * N_KV_A), sds(N_KV_A),
                   sds(N_HEADS_B), sdt(N_HEADS_B), sds(N_HEADS_B),
                   sds(N_HEADS_C // 2),
                   jax.ShapeDtypeStruct((B, 2 * N_HEADS_C, S // tm, LANES, tm), BF16),
                   sds(N_HEADS_C),
                   jax.ShapeDtypeStruct((T // tm, 1, LANES), F32)],
        compiler_params=pltpu.CompilerParams(dimension_semantics=("arbitrary",),
                                             vmem_limit_bytes=VMEM_LIMIT_BYTES),
        name="in_proj",
    )(x2d, gat, w_in_r, *tabs, qg_a, kg_a, qg_b, kvg_b, wuq_p, wukv_p, _norm_routing(),
      _same_head_a())


def _lane_group_max(s):
    m = s[:, :LANES]
    for j in range(1, s.shape[1] // LANES):
        m = jnp.maximum(m, s[:, j * LANES:(j + 1) * LANES])
    return m


def _pv_bounded(streams, n_chunks):
    items = [(r, i) for r in range(n_chunks) for i in range(len(streams))]
    accs = [None] * len(streams)
    in_flight = {}
    for k in range(len(items) + SCORE_LOOKAHEAD):
        if k < len(items):
            r, i = items[k]
            scores, _, row_off = streams[i]
            s, off = scores(r), row_off(r)
            in_flight[k] = s if off is None else s + off
        if k >= SCORE_LOOKAHEAD:
            r, i = items[k - SCORE_LOOKAHEAD]
            d = _dot(jnp.exp2(in_flight.pop(k - SCORE_LOOKAHEAD)).astype(BF16), streams[i][1](r))
            accs[i] = d if accs[i] is None else accs[i] + d
    return accs


def _pv_exact(streams, n_chunks, s_ref):
    rows = s_ref.shape[0] // len(streams)
    maxes = []
    for i, (scores, _, row_off) in enumerate(streams):
        m_run = None
        for r in range(n_chunks):
            s = scores(r)
            s_ref[i * rows:(i + 1) * rows, r * TK:(r + 1) * TK] = s
            m, off = _lane_group_max(s), row_off(r)
            if off is not None:
                m = m + off
            m_run = m if m_run is None else jnp.maximum(m_run, m)
        maxes.append(jnp.max(m_run, axis=-1, keepdims=True))
    accs = []
    for i, (_, values, row_off) in enumerate(streams):
        acc = None
        for r in range(n_chunks):
            off = row_off(r)
            shift = maxes[i] if off is None else maxes[i] - off
            p = jnp.exp2(s_ref[i * rows:(i + 1) * rows, r * TK:(r + 1) * TK] - shift)
            d = _dot(p.astype(BF16), values(r))
            acc = d if acc is None else acc + d
        accs.append(acc)
    return accs


def _run_heads(bounded, heads, head_streams, n_chunks, s_ref, acc_ref, roll_exact=False):
    @pl.when(bounded)
    def _():
        shared = {}
        per_head = [head_streams(h, shared) for h in heads]
        n_sub = len(per_head[0])
        accs = _pv_bounded([s for streams in per_head for s in streams], n_chunks)
        for k, h in enumerate(heads):
            acc_ref[h] = jnp.concatenate(accs[k * n_sub:(k + 1) * n_sub], axis=0)

    @pl.when(jnp.logical_not(bounded))
    def _():
        def one_head(h):
            acc_ref[h] = jnp.concatenate(_pv_exact(head_streams(h, {}), n_chunks, s_ref), axis=0)

        if roll_exact:
            lax.fori_loop(0, len(heads), lambda k, c: (one_head(heads[0] + k), c)[1], 0)
        else:
            for h in heads:
                one_head(h)


def _normalised_pair(acc_low, acc_high):
    low = _low_lanes()
    num = jnp.where(low, acc_low, acc_high)
    den = pltpu.roll(jnp.where(low, acc_high, acc_low), HALF, axis=1)
    return num / den


def _attn_ab_kernel(bounded_ref, q_ref, kt_ref, v_ref, o_ref, s_ref, acc_ref, *,
                    n_heads, per_pipeline, q_index, kt_index, v_index, out_pairs, n_keys):
    n_chunks = n_keys // TK

    def head_streams(h, shared):
        q = q_ref[0, q_index(h)]
        return [(lambda r, qs=q[r0:r0 + TQ_SUB_AB]:
                 _dot(qs, kt_ref[0, kt_index(h), :, r * TK:(r + 1) * TK]),
                 lambda r: v_ref[0, v_index(h), r * TK:(r + 1) * TK, :],
                 lambda r: None) for r0 in range(0, TQ, TQ_SUB_AB)]

    def head_group(j, carry):
        heads = [per_pipeline * j + k for k in range(per_pipeline)]
        _run_heads(bounded_ref[pl.program_id(0), j] != 0, heads, head_streams, n_chunks, s_ref,
                   acc_ref, roll_exact=True)
        return carry

    if n_heads == per_pipeline:
        head_group(0, None)
    else:
        lax.fori_loop(0, n_heads // per_pipeline, head_group, 0)
    o_ref[...] = jnp.concatenate([_normalised_pair(acc_ref[a], acc_ref[b]) for a, b in out_pairs],
                                 axis=-1).astype(BF16)


def _attn_c_kernel(bounded_ref, q_ref, kt_ref, v_ref, lq1_ref, lk1_ref, lq2_ref, lk2_ref,
                   subln_ref, o_ref, s_ref, acc_ref, *, n_keys, lam_init):
    assert TQ == TK
    nc = n_keys // TK
    cd = pl.program_id(1)
    col = lax.broadcasted_iota(jnp.int32, (1, TK), 1).astype(F32)
    row = lax.broadcasted_iota(jnp.int32, (TQ_SUB_C, 1), 0).astype(F32)
    subs = [(r, r + TQ_SUB_C) for r in range(0, TQ, TQ_SUB_C)]
    neg_dist = [-jnp.abs((lax.broadcasted_iota(jnp.int32, (TQ_SUB_C, TK), 0) + r0
                          - lax.broadcasted_iota(jnp.int32, (TQ_SUB_C, TK), 1)).astype(F32))
                for r0, _ in subs]

    chunk = [(cd + r) % nc for r in range(nc)]

    def head_bias(hh):
        slope = jnp.float32(LOG2E * 2.0 ** (-8.0 * N_HEADS_C / N_HEADS_C))
        for k in range(N_HEADS_C - 2, -1, -1):
            slope = jnp.where(hh == k, LOG2E * 2.0 ** (-8.0 * (k + 1) / N_HEADS_C), slope)
        key_term, row_off = [None], [[None] * len(subs)]
        for r in range(1, nc):
            side = jnp.where(chunk[r] < cd, slope, -slope)
            key_term.append(col * side)
            base = -slope * (jnp.abs(chunk[r] - cd) * TK).astype(F32)
            row_off.append([base - side * (row + float(r0)) for r0, _ in subs])
        return slope, key_term, row_off

    def head_pair(p, carry):
        def streams_of(k, shared):
            idx, hh = 4 * p + k, 2 * p + k // 2
            if isinstance(k, int):
                if k // 2 not in shared:
                    shared[k // 2] = head_bias(hh)
                slope, key_term, row_off = shared[k // 2]
            else:
                slope, key_term, row_off = head_bias(hh)

            def scores(r, si):
                s = _dot(q_ref[0, p][subs[si][0]:subs[si][1]], kt_ref[0, idx, chunk[r]])
                return s + (neg_dist[si] * slope if r == 0 else key_term[r])

            return [(functools.partial(scores, si=si),
                     lambda r: v_ref[0, hh, pl.ds(pl.multiple_of(chunk[r] * TK, TK), TK), :],
                     lambda r, si=si: row_off[r][si]) for si in range(len(subs))]

        _run_heads(bounded_ref[pl.program_id(0), p] != 0, list(range(4)), streams_of, nc, s_ref,
                   acc_ref.at[pl.ds(4 * p, 4)], roll_exact=True)
        return carry

    assert HEADS_PER_PIPELINE == 4
    lax.fori_loop(0, N_HEADS_C // 2, head_pair, 0)

    lam = (jnp.exp(jnp.sum(lq1_ref[...] * lk1_ref[...], axis=-1, keepdims=True))
           - jnp.exp(jnp.sum(lq2_ref[...] * lk2_ref[...], axis=-1, keepdims=True)) + lam_init)
    outs = []
    for p in range(N_HEADS_C // 2):
        o = (_normalised_pair(acc_ref[4 * p], acc_ref[4 * p + 2])
             - lam * _normalised_pair(acc_ref[4 * p + 1], acc_ref[4 * p + 3]))
        outs.append(_half_rms(o) * subln_ref[...] * (1.0 - lam_init))
    o_ref[...] = jnp.concatenate(outs, axis=-1).astype(BF16)


def _attention(kernel_fn, bounded, n_acc, q, kt, v, extras, out_width, B, S):
    nq = S // TQ
    hq, hv = q.shape[1], v.shape[1]
    extra_specs = [pl.BlockSpec(e.shape, lambda b, i, f, n=e.ndim: (0,) * n) for e in extras]
    return pl.pallas_call(
        kernel_fn,
        grid_spec=pltpu.PrefetchScalarGridSpec(
            num_scalar_prefetch=1,
            grid=(B, nq),
            in_specs=[pl.BlockSpec((1, hq, TQ, LANES), lambda b, i, f: (b, 0, i, 0)),
                      pl.BlockSpec((1,) + kt.shape[1:],
                                   lambda b, i, f, n=kt.ndim: (b,) + (0,) * (n - 1)),
                      pl.BlockSpec((1, hv, S, LANES), lambda b, i, f: (b, 0, 0, 0))] + extra_specs,
            out_specs=pl.BlockSpec((TQ, out_width), lambda b, i, f: (b * nq + i, 0)),
            scratch_shapes=[pltpu.VMEM((TQ, S), F32),
                            pltpu.VMEM((n_acc, TQ, LANES), F32)]),
        out_shape=jax.ShapeDtypeStruct((B * S, out_width), BF16),
        compiler_params=pltpu.CompilerParams(dimension_semantics=("arbitrary", "arbitrary"),
                                             vmem_limit_bytes=VMEM_LIMIT_BYTES),
        name=kernel_fn.func.__name__.strip("_"),
    )(bounded, q, kt, v, *extras)


def _mlp_kernel(xp_ref, x_ref, xn_ref, ap_ref, a_ref, an_ref, bp_ref, b_ref, bn_ref,
                cp_ref, c_ref, cn_ref, wo_ref, g_ref, wu_ref, cw_ref, cb_ref, wd_ref, fg_ref,
                o_ref, act_ref, *, tiles_per_seq, final_norm):
    i = pl.program_id(0)
    tm = x_ref.shape[0]
    halo = xp_ref.shape[0]

    def rows(p, m, n):
        return jnp.concatenate([p[...], m[...], n[...]], axis=0)

    o_ext = jnp.concatenate([rows(ap_ref, a_ref, an_ref), rows(bp_ref, b_ref, bn_ref),
                             rows(cp_ref, c_ref, cn_ref)], axis=1)
    x1 = rows(xp_ref, x_ref, xn_ref) + _dot(o_ext, wo_ref[...])
    h = (_rms(x1, D_MODEL) * g_ref[...]).astype(BF16)
    at_start = (i % tiles_per_seq) == 0
    at_end = (i % tiles_per_seq) == tiles_per_seq - 1
    zeros = jnp.zeros((halo, D_MODEL), BF16)
    hext = jnp.concatenate([jnp.where(at_start, zeros, h[:halo]), h[halo:halo + tm],
                            jnp.where(at_end, zeros, h[halo + tm:])], axis=0)

    def branch(k, c0):
        u = _dot(hext, wu_ref[:, k * D_FF + c0:k * D_FF + c0 + FFN_CHUNK])
        cw = cw_ref[k, :, c0:c0 + FFN_CHUNK]
        n = u.shape[0]
        prev = pltpu.roll(u, 1, axis=0)[halo:halo + tm]
        nxt = pltpu.roll(u, n - 1, axis=0)[halo:halo + tm]
        return (prev * cw[0:1] + u[halo:halo + tm] * cw[1:2] + nxt * cw[2:3]
                + cb_ref[k, :, c0:c0 + FFN_CHUNK])

    y = x1[halo:halo + tm]
    done = 0
    for c0 in range(0, D_FF, FFN_CHUNK):
        g = branch(0, c0)
        val = branch(1, c0)
        act_ref[:, c0:c0 + FFN_CHUNK] = (g / (1.0 + jnp.exp(-g)) * val).astype(BF16)
        ready = c0 + FFN_CHUNK - FFN_DOWN_LAG * FFN_CHUNK
        if ready - done >= FFN_DOWN_GROUP:
            y = y + _dot(act_ref[:, done:done + FFN_DOWN_GROUP], wd_ref[done:done + FFN_DOWN_GROUP])
            done += FFN_DOWN_GROUP
    y = y + _dot(act_ref[:, done:], wd_ref[done:])
    if final_norm:
        y = _rms(y, D_MODEL) * fg_ref[...]
    o_ref[...] = y


def _mlp(x2d, oa, ob, oc, w_out, g_ffn, w_up, conv_w, conv_b, w_down, final_g, S, final_norm):
    T = x2d.shape[0]
    tm, halo = TM_FFN, BF16_SUBLANES
    per = tm // halo
    last = T // halo - 1
    resident = lambda shape: pl.BlockSpec(shape, lambda i: (0,) * len(shape),
                                          pipeline_mode=pl.Buffered(1))

    def with_halos(width):
        return [pl.BlockSpec((halo, width), lambda i: (jnp.maximum(i * per - 1, 0), 0)),
                pl.BlockSpec((tm, width), lambda i: (i, 0)),
                pl.BlockSpec((halo, width), lambda i: (jnp.minimum((i + 1) * per, last), 0))]

    return pl.pallas_call(
        functools.partial(_mlp_kernel, tiles_per_seq=S // tm, final_norm=final_norm),
        grid=(T // tm,),
        in_specs=(with_halos(D_MODEL) + with_halos(oa.shape[1]) + with_halos(ob.shape[1])
                  + with_halos(oc.shape[1])
                  + [resident(w_out.shape), resident((1, D_MODEL)),
                     resident((D_MODEL, 2 * D_FF)), resident((2, 3, D_FF)), resident((2, 1, D_FF)),
                     resident((D_FF, D_MODEL)), resident((1, D_MODEL))]),
        out_specs=pl.BlockSpec((tm, D_MODEL), lambda i: (i, 0)),
        out_shape=jax.ShapeDtypeStruct((T, D_MODEL), F32),
        scratch_shapes=[pltpu.VMEM((tm, D_FF), BF16)],
        compiler_params=pltpu.CompilerParams(dimension_semantics=("arbitrary",),
                                             vmem_limit_bytes=VMEM_LIMIT_BYTES),
        name="out_proj_conv_mlp",
    )(x2d, x2d, x2d, oa, oa, oa, ob, ob, ob, oc, oc, oc, w_out, g_ffn,
      w_up, conv_w, conv_b, w_down, final_g)


def _pad_groups(w, n_groups, width, offset=0):
    r = w.shape[0]
    w = w.reshape(r, n_groups, width)
    w = jnp.pad(w, ((0, 0), (0, 0), (offset, LANES - width - offset)))
    return w.reshape(r, n_groups * LANES)


def _group_a_lane_order():
    nf = HEAD_DIM // 4
    lanes = np.arange(LANES)
    second, r = lanes // HALF, lanes % HALF
    head, rr = r // A_QUARTER, r % A_QUARTER
    block, freq = rr // nf, rr % nf
    dim = block * 2 * nf + second * nf + freq
    return head, dim, block, freq, second


def _same_head_a():
    head = _group_a_lane_order()[0]
    return jnp.asarray(head[:, None] == head[None, :], BF16)


def _group_b_lane_order():
    nf = MLA_ROPE // 4
    src = np.full(LANES, -1)
    block = np.zeros(LANES, np.int64)
    freq = np.zeros(LANES, np.int64)
    second = np.zeros(LANES, np.int64)
    is_rope = np.zeros(LANES, bool)
    for e in range(2):
        for j in range(2 * nf):
            lane = e * HALF + j
            is_rope[lane], block[lane], freq[lane], second[lane] = True, j // nf, j % nf, e
            src[lane] = MLA_NOPE + (j // nf) * 2 * nf + e * nf + j % nf
    n_low = HALF - 2 * nf
    src[2 * nf:HALF] = np.arange(n_low)
    src[HALF + 2 * nf:HALF + 2 * nf + MLA_NOPE - n_low] = np.arange(n_low, MLA_NOPE)
    return src, is_rope, block, freq, second


def _gather_cols(w, src):
    return jnp.where(jnp.asarray(src >= 0), w[:, np.maximum(src, 0)], 0.0)


def _rope_tables(S):
    t = jnp.arange(S, dtype=jnp.int32)
    pos = jnp.stack([(t // GRID_W).astype(F32), (t % GRID_W).astype(F32)], axis=0)

    def table(nf, block, freq, second, is_rope):
        inv = ROPE_BASE ** (-jnp.arange(nf, dtype=F32) / nf)
        ang = pos[block].T * inv[freq]
        sign = jnp.asarray(2.0 * second - 1.0, F32)
        cos = jnp.where(jnp.asarray(is_rope), jnp.cos(ang), 1.0)
        sin = jnp.where(jnp.asarray(is_rope), jnp.sin(ang) * sign, 0.0)
        return cos, sin

    _, _, block, freq, second = _group_a_lane_order()
    cos_a, sin_a = table(HEAD_DIM // 4, block, freq, second, np.ones(LANES, bool))
    _, is_rope, block, freq, second = _group_b_lane_order()
    cos_b, sin_b = table(MLA_ROPE // 4, block, freq, second, is_rope)
    return cos_a, sin_a, cos_b, sin_b


def _prep_w_in(w):
    n_qa, n_ka = N_HEADS_A * HEAD_DIM, N_KV_A * HEAD_DIM
    a_end = n_qa + 2 * n_ka
    b_end = a_end + MLA_Q_RANK + MLA_KV_RANK + MLA_ROPE
    head, dim, _, _, _ = _group_a_lane_order()
    pair_cols = head * HEAD_DIM + dim
    qa_cols = np.concatenate([p * LANES + pair_cols for p in range(N_HEADS_A // 2)])
    cq = w[:, a_end:a_end + MLA_Q_RANK]
    ckv = w[:, a_end + MLA_Q_RANK:a_end + MLA_Q_RANK + MLA_KV_RANK]
    kr = w[:, b_end - MLA_ROPE:b_end]
    src_b, is_rope, _, _, _ = _group_b_lane_order()
    out = jnp.concatenate([
        w[:, qa_cols], w[:, n_qa + pair_cols], w[:, n_qa + n_ka:a_end],
        jnp.pad(cq, ((0, 0), (0, CQ_PAD - MLA_Q_RANK))),
        ckv,
        _gather_cols(kr, np.where(is_rope, src_b - MLA_NOPE, -1)),
        w[:, b_end:],
    ], axis=-1).astype(BF16)
    assert out.shape[1] == IN_COLS
    return out


def _twice(v):
    return jnp.tile(v, 2).reshape(1, 2 * v.shape[0])


def kernel(x, norm_attn, w_in, q_norm_a, k_norm_a, q_a_norm_b, w_uq_b, kv_a_norm_b, w_ukv_b,
           lambda_q1_c, lambda_k1_c, lambda_q2_c, lambda_k2_c, subln_c, w_out,
           norm_ffn, w_up, conv_w, conv_b, w_down, final_norm):
    B, S, D = x.shape
    depth = w_in.shape[0]
    assert D == D_MODEL and S % max(TM_IN, TM_FFN, TQ, TK) == 0 and S % GRID_W == 0
    assert TM_IN == TK
    T = B * S

    tabs = _rope_tables(S)
    row = lambda v: v.reshape(1, -1)
    a_pairs = [(j, j + N_HEADS_A // 2) for j in range(N_HEADS_A // 2)]
    a_rows = np.concatenate([np.arange(h * HEAD_DIM, (h + 1) * HEAD_DIM)
                             for pair in a_pairs for h in pair])
    xc = x.reshape(T, D)
    for l in range(depth):
        lam_init = 0.8 - 0.6 * math.exp(-0.3 * l)
        src_b, is_rope, _, _, _ = _group_b_lane_order()
        d_qk = MLA_NOPE + MLA_ROPE
        wuq_p = jnp.pad(
            jnp.concatenate([_gather_cols(w_uq_b[l][:, h * d_qk:(h + 1) * d_qk], src_b)
                             for h in range(N_HEADS_B)], axis=-1),
            ((0, CQ_PAD - MLA_Q_RANK), (0, 0))).astype(BF16)
        wukv = w_ukv_b[l].reshape(MLA_KV_RANK, N_HEADS_B, MLA_NOPE + MLA_V)
        k_cols = [_gather_cols(wukv[:, h, :MLA_NOPE], np.where(is_rope, -1, src_b))
                  for h in range(N_HEADS_B)]
        v_cols = [_pad_groups(wukv[:, h, MLA_NOPE:], 1, MLA_V, offset=HALF * (h % 2))
                  for h in range(N_HEADS_B)]
        wukv_p = jnp.concatenate(k_cols + v_cols, axis=-1).astype(BF16)

        qa, kta, va, qb, ktb, vb, qc, ktc, vc, nrm = _in_proj(
            xc, row(norm_attn[l]), _prep_w_in(w_in[l]), tabs,
            q_norm_a[l][_group_a_lane_order()[1]].reshape(1, LANES),
            k_norm_a[l][_group_a_lane_order()[1]].reshape(1, LANES),
            jnp.pad(q_a_norm_b[l], (0, CQ_PAD - MLA_Q_RANK)).reshape(1, CQ_PAD),
            row(kv_a_norm_b[l]), wuq_p, wukv_p, B, S)

        nrm = jnp.max(nrm.reshape(B, -1, LANES), axis=1)

        def bounded(q0, nq, q_rep, k0, nk, k_rep, per_pipeline):
            q2 = jnp.repeat(nrm[:, q0:q0 + nq], q_rep, axis=1)
            k2 = jnp.repeat(nrm[:, k0:k0 + nk], k_rep, axis=1)
            ok = q2 * k2 * NRM_MARGIN <= SAFE_LOG2_RANGE ** 2
            return jnp.all(ok.reshape(B, -1, per_pipeline), axis=-1).astype(jnp.int32)

        qk_a = (HEAD_DIM * jnp.max(jnp.abs(q_norm_a[l])) * jnp.max(jnp.abs(k_norm_a[l]))
                * (HEAD_DIM ** -0.5 * LOG2E))
        bounded_a = jnp.broadcast_to(
            (qk_a * NRM_MARGIN <= SAFE_LOG2_RANGE).astype(jnp.int32),
            (B, N_HEADS_A // HEADS_PER_PIPELINE_AB))

        group = N_HEADS_A // N_KV_A
        oa = _attention(
            functools.partial(_attn_ab_kernel, n_heads=N_HEADS_A,
                              per_pipeline=HEADS_PER_PIPELINE_AB, q_index=lambda h: h // 2,
                              kt_index=lambda h: 2 * (h // group) + h % 2,
                              v_index=lambda h: h // group, out_pairs=a_pairs, n_keys=S),
            bounded_a, N_HEADS_A, qa, kta, va, [], N_HEADS_A * HEAD_DIM, B, S)
        same = lambda h: h
        ob = _attention(
            functools.partial(_attn_ab_kernel, n_heads=N_HEADS_B,
                              per_pipeline=HEADS_PER_PIPELINE_AB, q_index=same, kt_index=same,
                              v_index=same,
                              out_pairs=[(2 * j, 2 * j + 1) for j in range(N_HEADS_B // 2)],
                              n_keys=S),
            bounded(NRM_QB, N_HEADS_B, 1, NRM_KB, N_HEADS_B, 1, HEADS_PER_PIPELINE_AB), N_HEADS_B,
            qb, ktb, vb, [], N_HEADS_B * MLA_V, B, S)
        oc = _attention(
            functools.partial(_attn_c_kernel, n_keys=S, lam_init=lam_init),
            bounded(NRM_QC, N_HEADS_C, 2, NRM_KC, 2 * N_HEADS_C, 1, HEADS_PER_PIPELINE),
            2 * N_HEADS_C, qc, ktc, vc,
            [row(lambda_q1_c[l]), row(lambda_k1_c[l]), row(lambda_q2_c[l]), row(lambda_k2_c[l]),
             _twice(subln_c[l])],
            N_HEADS_C * DIFF_V, B, S)

        w_out_l = jnp.concatenate([w_out[l][a_rows], w_out[l][N_HEADS_A * HEAD_DIM:]], axis=0)
        xc = _mlp(xc, oa, ob, oc, w_out_l.astype(BF16), row(norm_ffn[l]), w_up[l].astype(BF16),
                  conv_w[l].reshape(3, 2, D_FF).transpose(1, 0, 2),
                  conv_b[l].reshape(2, 1, D_FF), w_down[l].astype(BF16),
                  row(final_norm), S, final_norm=(l == depth - 1))
    return xc.reshape(B, S, D)
```

```python
import functools
import math

import jax
import jax.numpy as jnp
import numpy as np
from jax import lax
from jax.experimental import pallas as pl
from jax.experimental.pallas import tpu as pltpu

D_MODEL = 1024
GRID_W = 64
HEAD_DIM = 64
N_HEADS_A = 8
N_KV_A = 2
N_HEADS_B = 4
MLA_Q_RANK = 192
MLA_KV_RANK = 128
MLA_NOPE = 64
MLA_ROPE = 32
MLA_V = 64
N_HEADS_C = 4
DIFF_QK = 32
DIFF_V = 64
D_FF = 2816
ROPE_BASE = 10000.0
EPS = 1e-6

LANES = 128
HALF = LANES // 2
A_QUARTER = LANES // 4
BF16_SUBLANES = 16
VMEM_LIMIT_BYTES = 56 * 1024 * 1024

LOG2E = math.log2(math.e)
assert HEAD_DIM == MLA_V == DIFF_V == 2 * DIFF_QK == HALF

OFF_QA = 0
OFF_KA = OFF_QA + N_HEADS_A * HEAD_DIM
OFF_VA = OFF_KA + LANES
OFF_CQ = OFF_VA + LANES
CQ_PAD = 2 * LANES
OFF_CKV = OFF_CQ + CQ_PAD
OFF_KR = OFF_CKV + MLA_KV_RANK
OFF_QC = OFF_KR + LANES
OFF_KC = OFF_QC + N_HEADS_C * HALF
OFF_VC = OFF_KC + N_HEADS_C * HALF
IN_COLS = OFF_VC + N_HEADS_C * HALF

NRM_QB = 0
NRM_KB = NRM_QB + N_HEADS_B
NRM_QC = NRM_KB + N_HEADS_B
NRM_KC = NRM_QC + N_HEADS_C
NRM_GROUPS = 2 * N_HEADS_B + N_HEADS_C
NRM_MARGIN = 1.03
SAFE_LOG2_RANGE = 55.0

F32 = jnp.float32
BF16 = jnp.bfloat16

TM_IN = 512
TM_FFN = 512
FFN_CHUNK = 256
FFN_DOWN_GROUP = 6 * FFN_CHUNK
FFN_DOWN_LAG = 2
TQ = 512
TQ_SUB_AB = 512
TQ_SUB_C = 256
TK = 512
SCORE_LOOKAHEAD = 2
HEADS_PER_PIPELINE = 4
HEADS_PER_PIPELINE_AB = 4


def _dot(a, b):
    return jnp.dot(a, b, preferred_element_type=F32)


def _rms(x, n):
    ms = jnp.sum(x * x, axis=-1, keepdims=True) * (1.0 / n)
    return x * lax.rsqrt(ms + EPS)


def _low_lanes():
    return lax.broadcasted_iota(jnp.int32, (1, LANES), 1) < HALF


def _half_rms(x, low=None):
    low = _low_lanes() if low is None else low
    sq = x * x
    s_lo = jnp.sum(jnp.where(low, sq, 0.0), axis=-1, keepdims=True)
    s_hi = jnp.sum(jnp.where(low, 0.0, sq), axis=-1, keepdims=True)
    return x * lax.rsqrt(jnp.where(low, s_lo, s_hi) * (1.0 / HALF) + EPS)


def _in_kernel(x_ref, gat_ref, w_ref, cos_a_ref, sin_a_ref, cos_b_ref, sin_b_ref,
               qg_a_ref, kg_a_ref, qg_b_ref, kvg_b_ref, wuq_ref, wukv_ref, sel_ref,
               qa_ref, kta_ref, va_ref, qb_ref, ktb_ref, vb_ref, qc_ref, ktc_ref, vc_ref, nrm_ref):
    x = x_ref[...]
    tm = x.shape[0]
    h = (_rms(x, D_MODEL) * gat_ref[...]).astype(BF16)

    lane = lax.broadcasted_iota(jnp.int32, (1, LANES), 1)
    low = lane < HALF
    row = lax.broadcasted_iota(jnp.int32, (LANES, 1), 0)

    z = _dot(h, w_ref[...])

    def seg(off, width):
        return z[:, off:off + width]

    def rope(y, c, s):
        return y * c + pltpu.roll(y, HALF, axis=1) * s

    def group(z, g):
        return z[:, g * LANES:(g + 1) * LANES]

    squares = []

    def note(y):
        squares.append((y * y).astype(BF16))

    def noted_row_sums(first):
        return _dot(jnp.concatenate(squares[first:], axis=-1),
                    sel_ref[first * LANES:len(squares) * LANES])

    cos_b, sin_b = cos_b_ref[...], sin_b_ref[...]
    cq =(_rms(seg(OFF_CQ, CQ_PAD), MLA_Q_RANK) * qg_b_ref[...]).astype(BF16)
    qb = _dot(cq, wuq_ref[...])
    scale_b = (MLA_NOPE + MLA_ROPE) ** -0.5 * LOG2E
    for hh in range(N_HEADS_B):
        y = rope(group(qb, hh), cos_b, sin_b) * scale_b
        qb_ref[0, hh] = y.astype(BF16)
        note(y)
    ckv = (_rms(seg(OFF_CKV, MLA_KV_RANK), MLA_KV_RANK) * kvg_b_ref[...]).astype(BF16)
    kvb = _dot(ckv, wukv_ref[...])
    kr = rope(seg(OFF_KR, LANES), cos_b, sin_b)
    for hh in range(N_HEADS_B):
        y = group(kvb, hh) + kr
        ktb_ref[0, hh] = y.T.astype(BF16)
        note(y)
        ones_other_half = jnp.where(low, float(hh % 2), float(1 - hh % 2))
        vb_ref[0, hh] = (group(kvb, N_HEADS_B + hh) + ones_other_half).astype(BF16)
    n2 = noted_row_sums(0)
    n_noted_b = len(squares)

    zq = seg(OFF_QC, N_HEADS_C * HALF)
    zk = seg(OFF_KC, N_HEADS_C * HALF)
    zv = seg(OFF_VC, N_HEADS_C * HALF)
    k_groups = []
    for p in range(N_HEADS_C // 2):
        y = group(zq, p) * (DIFF_QK ** -0.5 * LOG2E)
        qc_ref[0, p] = y.astype(BF16)
        note(y)
        k_groups.append(group(zk, p))
        kt = group(zk, p).T
        for sub in range(4):
            mine = (row >= sub * DIFF_QK) & (row < (sub + 1) * DIFF_QK)
            ktc_ref[0, 4 * p + sub, 0] = jnp.where(mine, kt, 0.0).astype(BF16)
        vz = group(zv, p)
        vc_ref[0, 2 * p] = jnp.where(low, vz, 1.0).astype(BF16)
        vc_ref[0, 2 * p + 1] = jnp.where(low, 1.0, vz).astype(BF16)
    for kz in k_groups:
        note(kz)
    n2 = n2 + noted_row_sums(n_noted_b)
    nrm_ref[0] = jnp.max(n2, axis=0, keepdims=True)

    cos_a, sin_a = cos_a_ref[...], sin_a_ref[...]
    first_head = (lane & A_QUARTER) == 0
    zq = seg(OFF_QA, N_HEADS_A * HEAD_DIM)
    for p in range(N_HEADS_A // 2):
        y = rope(_half_rms(group(zq, p), first_head) * qg_a_ref[...], cos_a, sin_a)
        qa_ref[0, p] = (y * (HEAD_DIM ** -0.5 * LOG2E)).astype(BF16)
    y = rope(_half_rms(seg(OFF_KA, LANES), first_head) * kg_a_ref[...], cos_a, sin_a)
    kt = y.T.astype(BF16)
    quarter = [kt[j * A_QUARTER:(j + 1) * A_QUARTER] for j in range(4)]
    zeros = jnp.zeros((A_QUARTER, tm), BF16)
    for g in range(N_KV_A):
        k1, k2 = quarter[g], quarter[2 + g]
        kta_ref[0, 2 * g] = jnp.concatenate([k1, zeros, k2, zeros], axis=0)
        kta_ref[0, 2 * g + 1] = jnp.concatenate([zeros, k1, zeros, k2], axis=0)
    zv = seg(OFF_VA, LANES)
    va_ref[0, 0] = jnp.where(low, zv, 1.0).astype(BF16)
    va_ref[0, 1] = jnp.where(low, 1.0, zv).astype(BF16)


def _norm_routing():
    sel = np.zeros((NRM_GROUPS, LANES, LANES), np.float32)
    g = 0
    for slot in ([NRM_QB + h for h in range(N_HEADS_B)] + [NRM_KB + h for h in range(N_HEADS_B)]):
        sel[g, :, slot] = 1.0
        g += 1
    for p in range(N_HEADS_C // 2):
        for f in range(2):
            sel[g, f * HALF:(f + 1) * HALF, NRM_QC + 2 * p + f] = 1.0
        g += 1
    for p in range(N_HEADS_C // 2):
        for sub in range(4):
            sel[g, sub * DIFF_QK:(sub + 1) * DIFF_QK, NRM_KC + 4 * p + sub] = 1.0
        g += 1
    assert g == NRM_GROUPS
    return jnp.asarray(sel.reshape(NRM_GROUPS * LANES, LANES), BF16)


def _in_proj(x2d, gat, w_in_r, tabs, qg_a, kg_a, qg_b, kvg_b, wuq_p, wukv_p, B, S):
    T = B * S
    tm = TM_IN
    nst = S // tm
    const = lambda shape: pl.BlockSpec(shape, lambda i: (0,) * len(shape))
    tab = pl.BlockSpec((tm, LANES), lambda i: (i % nst, 0))
    hm = lambda H: pl.BlockSpec((1, H, tm, LANES), lambda i: (i // nst, 0, i % nst, 0))
    hmt = lambda H: pl.BlockSpec((1, H, LANES, tm), lambda i: (i // nst, 0, 0, i % nst))
    sds = lambda H: jax.ShapeDtypeStruct((B, H, S, LANES), BF16)
    sdt = lambda H: jax.ShapeDtypeStruct((B, H, LANES, S), BF16)
    return pl.pallas_call(
        _in_kernel,
        grid=(T // tm,),
        in_specs=[pl.BlockSpec((tm, D_MODEL), lambda i: (i, 0)),
                  const((1, D_MODEL)), const((D_MODEL, IN_COLS)),
                  tab, tab, tab, tab,
                  const((1, LANES)), const((1, LANES)), const((1, CQ_PAD)), const((1, MLA_KV_RANK)),
                  const((CQ_PAD, N_HEADS_B * LANES)), const((MLA_KV_RANK, 2 * N_HEADS_B * LANES)),
                  const((NRM_GROUPS * LANES, LANES))],
        out_specs=[hm(N_HEADS_A // 2), hmt(2 * N_KV_A), hm(N_KV_A),
                   hm(N_HEADS_B), hmt(N_HEADS_B), hm(N_HEADS_B),
                   hm(N_HEADS_C // 2),
                   pl.BlockSpec((1, 2 * N_HEADS_C, 1, LANES, tm),
                                lambda i: (i // nst, 0, i % nst, 0, 0)),
                   hm(N_HEADS_C),
                   pl.BlockSpec((1, 1, LANES), lambda i: (i, 0, 0))],
        out_shape=[sds(N_HEADS_A // 2), sdt(2 * N_KV_A), sds(N_KV_A),
                   sds(N_HEADS_B), sdt(N_HEADS_B), sds(N_HEADS_B),
                   sds(N_HEADS_C // 2),
                   jax.ShapeDtypeStruct((B, 2 * N_HEADS_C, S // tm, LANES, tm), BF16),
                   sds(N_HEADS_C),
                   jax.ShapeDtypeStruct((T // tm, 1, LANES), F32)],
        compiler_params=pltpu.CompilerParams(dimension_semantics=("arbitrary",),
                                             vmem_limit_bytes=VMEM_LIMIT_BYTES),
        name="in_proj",
    )(x2d, gat, w_in_r, *tabs, qg_a, kg_a, qg_b, kvg_b, wuq_p, wukv_p, _norm_routing())


def _lane_group_max(s):
    m = s[:, :LANES]
    for j in range(1, s.shape[1] // LANES):
        m = jnp.maximum(m, s[:, j * LANES:(j + 1) * LANES])
    return m


def _pv_bounded(streams, n_chunks):
    items = [(r, i) for r in range(n_chunks) for i in range(len(streams))]
    accs = [None] * len(streams)
    in_flight = {}
    for k in range(len(items) + SCORE_LOOKAHEAD):
        if k < len(items):
            r, i = items[k]
            scores, _, row_off = streams[i]
            s, off = scores(r), row_off(r)
            in_flight[k] = s if off is None else s + off
        if k >= SCORE_LOOKAHEAD:
            r, i = items[k - SCORE_LOOKAHEAD]
            d = _dot(jnp.exp2(in_flight.pop(k - SCORE_LOOKAHEAD)).astype(BF16), streams[i][1](r))
            accs[i] = d if accs[i] is None else accs[i] + d
    return accs


def _pv_exact(streams, n_chunks, s_ref):
    rows = s_ref.shape[0] // len(streams)
    maxes = []
    for i, (scores, _, row_off) in enumerate(streams):
        m_run = None
        for r in range(n_chunks):
            s = scores(r)
            s_ref[i * rows:(i + 1) * rows, r * TK:(r + 1) * TK] = s
            m, off = _lane_group_max(s), row_off(r)
            if off is not None:
                m = m + off
            m_run = m if m_run is None else jnp.maximum(m_run, m)
        maxes.append(jnp.max(m_run, axis=-1, keepdims=True))
    accs = []
    for i, (_, values, row_off) in enumerate(streams):
        acc = None
        for r in range(n_chunks):
            off = row_off(r)
            shift = maxes[i] if off is None else maxes[i] - off
            p = jnp.exp2(s_ref[i * rows:(i + 1) * rows, r * TK:(r + 1) * TK] - shift)
            d = _dot(p.astype(BF16), values(r))
            acc = d if acc is None else acc + d
        accs.append(acc)
    return accs


def _run_heads(bounded, heads, head_streams, n_chunks, s_ref, acc_ref, roll_exact=False):
    @pl.when(bounded)
    def _():
        shared = {}
        per_head = [head_streams(h, shared) for h in heads]
        n_sub = len(per_head[0])
        accs = _pv_bounded([s for streams in per_head for s in streams], n_chunks)
        for k, h in enumerate(heads):
            acc_ref[h] = jnp.concatenate(accs[k * n_sub:(k + 1) * n_sub], axis=0)

    @pl.when(jnp.logical_not(bounded))
    def _():
        def one_head(h):
            acc_ref[h] = jnp.concatenate(_pv_exact(head_streams(h, {}), n_chunks, s_ref), axis=0)

        if roll_exact:
            lax.fori_loop(0, len(heads), lambda k, c: (one_head(heads[0] + k), c)[1], 0)
        else:
            for h in heads:
                one_head(h)


def _normalised_pair(acc_low, acc_high):
    low = _low_lanes()
    num = jnp.where(low, acc_low, acc_high)
    den = pltpu.roll(jnp.where(low, acc_high, acc_low), HALF, axis=1)
    return num / den


def _attn_ab_kernel(bounded_ref, q_ref, kt_ref, v_ref, o_ref, s_ref, acc_ref, *,
                    n_heads, per_pipeline, q_index, kt_index, v_index, out_pairs, n_keys):
    n_chunks = n_keys // TK

    def head_streams(h, shared):
        q = q_ref[0, q_index(h)]
        return [(lambda r, qs=q[r0:r0 + TQ_SUB_AB]:
                 _dot(qs, kt_ref[0, kt_index(h), :, r * TK:(r + 1) * TK]),
                 lambda r: v_ref[0, v_index(h), r * TK:(r + 1) * TK, :],
                 lambda r: None) for r0 in range(0, TQ, TQ_SUB_AB)]

    def head_group(j, carry):
        heads = [per_pipeline * j + k for k in range(per_pipeline)]
        _run_heads(bounded_ref[pl.program_id(0), j] != 0, heads, head_streams, n_chunks, s_ref,
                   acc_ref, roll_exact=True)
        return carry

    if n_heads == per_pipeline:
        head_group(0, None)
    else:
        lax.fori_loop(0, n_heads // per_pipeline, head_group, 0)
    o_ref[...] = jnp.concatenate([_normalised_pair(acc_ref[a], acc_ref[b]) for a, b in out_pairs],
                                 axis=-1).astype(BF16)


def _attn_c_kernel(bounded_ref, q_ref, kt_ref, v_ref, lq1_ref, lk1_ref, lq2_ref, lk2_ref,
                   subln_ref, o_ref, s_ref, acc_ref, *, n_keys, lam_init):
    assert TQ == TK
    nc = n_keys // TK
    cd = pl.program_id(1)
    col = lax.broadcasted_iota(jnp.int32, (1, TK), 1).astype(F32)
    row = lax.broadcasted_iota(jnp.int32, (TQ_SUB_C, 1), 0).astype(F32)
    subs = [(r, r + TQ_SUB_C) for r in range(0, TQ, TQ_SUB_C)]
    neg_dist = [-jnp.abs((lax.broadcasted_iota(jnp.int32, (TQ_SUB_C, TK), 0) + r0
                          - lax.broadcasted_iota(jnp.int32, (TQ_SUB_C, TK), 1)).astype(F32))
                for r0, _ in subs]

    chunk = [(cd + r) % nc for r in range(nc)]

    def head_bias(hh):
        slope = jnp.float32(LOG2E * 2.0 ** (-8.0 * N_HEADS_C / N_HEADS_C))
        for k in range(N_HEADS_C - 2, -1, -1):
            slope = jnp.where(hh == k, LOG2E * 2.0 ** (-8.0 * (k + 1) / N_HEADS_C), slope)
        key_term, row_off = [None], [[None] * len(subs)]
        for r in range(1, nc):
            side = jnp.where(chunk[r] < cd, slope, -slope)
            key_term.append(col * side)
            base = -slope * (jnp.abs(chunk[r] - cd) * TK).astype(F32)
            row_off.append([base - side * (row + float(r0)) for r0, _ in subs])
        return slope, key_term, row_off

    def head_pair(p, carry):
        def streams_of(k, shared):
            idx, hh = 4 * p + k, 2 * p + k // 2
            if isinstance(k, int):
                if k // 2 not in shared:
                    shared[k // 2] = head_bias(hh)
                slope, key_term, row_off = shared[k // 2]
            else:
                slope, key_term, row_off = head_bias(hh)

            def scores(r, si):
                s = _dot(q_ref[0, p][subs[si][0]:subs[si][1]], kt_ref[0, idx, chunk[r]])
                return s + (neg_dist[si] * slope if r == 0 else key_term[r])

            return [(functools.partial(scores, si=si),
                     lambda r: v_ref[0, hh, pl.ds(pl.multiple_of(chunk[r] * TK, TK), TK), :],
                     lambda r, si=si: row_off[r][si]) for si in range(len(subs))]

        _run_heads(bounded_ref[pl.program_id(0), p] != 0, list(range(4)), streams_of, nc, s_ref,
                   acc_ref.at[pl.ds(4 * p, 4)], roll_exact=True)
        return carry

    assert HEADS_PER_PIPELINE == 4
    lax.fori_loop(0, N_HEADS_C // 2, head_pair, 0)

    lam = (jnp.exp(jnp.sum(lq1_ref[...] * lk1_ref[...], axis=-1, keepdims=True))
           - jnp.exp(jnp.sum(lq2_ref[...] * lk2_ref[...], axis=-1, keepdims=True)) + lam_init)
    outs = []
    for p in range(N_HEADS_C // 2):
        o = (_normalised_pair(acc_ref[4 * p], acc_ref[4 * p + 2])
             - lam * _normalised_pair(acc_ref[4 * p + 1], acc_ref[4 * p + 3]))
        outs.append(_half_rms(o) * subln_ref[...] * (1.0 - lam_init))
    o_ref[...] = jnp.concatenate(outs, axis=-1).astype(BF16)


def _attention(kernel_fn, bounded, n_acc, q, kt, v, extras, out_width, B, S):
    nq = S // TQ
    hq, hv = q.shape[1], v.shape[1]
    extra_specs = [pl.BlockSpec(e.shape, lambda b, i, f, n=e.ndim: (0,) * n) for e in extras]
    return pl.pallas_call(
        kernel_fn,
        grid_spec=pltpu.PrefetchScalarGridSpec(
            num_scalar_prefetch=1,
            grid=(B, nq),
            in_specs=[pl.BlockSpec((1, hq, TQ, LANES), lambda b, i, f: (b, 0, i, 0)),
                      pl.BlockSpec((1,) + kt.shape[1:],
                                   lambda b, i, f, n=kt.ndim: (b,) + (0,) * (n - 1)),
                      pl.BlockSpec((1, hv, S, LANES), lambda b, i, f: (b, 0, 0, 0))] + extra_specs,
            out_specs=pl.BlockSpec((TQ, out_width), lambda b, i, f: (b * nq + i, 0)),
            scratch_shapes=[pltpu.VMEM((TQ, S), F32),
                            pltpu.VMEM((n_acc, TQ, LANES), F32)]),
        out_shape=jax.ShapeDtypeStruct((B * S, out_width), BF16),
        compiler_params=pltpu.CompilerParams(dimension_semantics=("arbitrary", "arbitrary"),
                                             vmem_limit_bytes=VMEM_LIMIT_BYTES),
        name=kernel_fn.func.__name__.strip("_"),
    )(bounded, q, kt, v, *extras)


def _mlp_kernel(xp_ref, x_ref, xn_ref, ap_ref, a_ref, an_ref, bp_ref, b_ref, bn_ref,
                cp_ref, c_ref, cn_ref, wo_ref, g_ref, wu_ref, cw_ref, cb_ref, wd_ref, fg_ref,
                o_ref, act_ref, *, tiles_per_seq, final_norm):
    i = pl.program_id(0)
    tm = x_ref.shape[0]
    halo = xp_ref.shape[0]

    def rows(p, m, n):
        return jnp.concatenate([p[...], m[...], n[...]], axis=0)

    o_ext = jnp.concatenate([rows(ap_ref, a_ref, an_ref), rows(bp_ref, b_ref, bn_ref),
                             rows(cp_ref, c_ref, cn_ref)], axis=1)
    x1 = rows(xp_ref, x_ref, xn_ref) + _dot(o_ext, wo_ref[...])
    h = (_rms(x1, D_MODEL) * g_ref[...]).astype(BF16)
    at_start = (i % tiles_per_seq) == 0
    at_end = (i % tiles_per_seq) == tiles_per_seq - 1
    zeros = jnp.zeros((halo, D_MODEL), BF16)
    hext = jnp.concatenate([jnp.where(at_start, zeros, h[:halo]), h[halo:halo + tm],
                            jnp.where(at_end, zeros, h[halo + tm:])], axis=0)

    def branch(k, c0):
        u = _dot(hext, wu_ref[:, k * D_FF + c0:k * D_FF + c0 + FFN_CHUNK])
        cw = cw_ref[k, :, c0:c0 + FFN_CHUNK]
        n = u.shape[0]
        prev = pltpu.roll(u, 1, axis=0)[halo:halo + tm]
        nxt = pltpu.roll(u, n - 1, axis=0)[halo:halo + tm]
        return (prev * cw[0:1] + u[halo:halo + tm] * cw[1:2] + nxt * cw[2:3]
                + cb_ref[k, :, c0:c0 + FFN_CHUNK])

    y = x1[halo:halo + tm]
    done = 0
    for c0 in range(0, D_FF, FFN_CHUNK):
        g = branch(0, c0)
        val = branch(1, c0)
        half = 0.5 * g
        act_ref[:, c0:c0 + FFN_CHUNK] = ((half + half * jnp.tanh(half)) * val).astype(BF16)
        ready = c0 + FFN_CHUNK - FFN_DOWN_LAG * FFN_CHUNK
        if ready - done >= FFN_DOWN_GROUP:
            y = y + _dot(act_ref[:, done:done + FFN_DOWN_GROUP], wd_ref[done:done + FFN_DOWN_GROUP])
            done += FFN_DOWN_GROUP
    y = y + _dot(act_ref[:, done:], wd_ref[done:])
    if final_norm:
        y = _rms(y, D_MODEL) * fg_ref[...]
    o_ref[...] = y


def _mlp(x2d, oa, ob, oc, w_out, g_ffn, w_up, conv_w, conv_b, w_down, final_g, S, final_norm):
    T = x2d.shape[0]
    tm, halo = TM_FFN, BF16_SUBLANES
    per = tm // halo
    last = T // halo - 1
    resident = lambda shape: pl.BlockSpec(shape, lambda i: (0,) * len(shape),
                                          pipeline_mode=pl.Buffered(1))

    def with_halos(width):
        return [pl.BlockSpec((halo, width), lambda i: (jnp.maximum(i * per - 1, 0), 0)),
                pl.BlockSpec((tm, width), lambda i: (i, 0)),
                pl.BlockSpec((halo, width), lambda i: (jnp.minimum((i + 1) * per, last), 0))]

    return pl.pallas_call(
        functools.partial(_mlp_kernel, tiles_per_seq=S // tm, final_norm=final_norm),
        grid=(T // tm,),
        in_specs=(with_halos(D_MODEL) + with_halos(oa.shape[1]) + with_halos(ob.shape[1])
                  + with_halos(oc.shape[1])
                  + [resident(w_out.shape), resident((1, D_MODEL)),
                     resident((D_MODEL, 2 * D_FF)), resident((2, 3, D_FF)), resident((2, 1, D_FF)),
                     resident((D_FF, D_MODEL)), resident((1, D_MODEL))]),
        out_specs=pl.BlockSpec((tm, D_MODEL), lambda i: (i, 0)),
        out_shape=jax.ShapeDtypeStruct((T, D_MODEL), F32),
        scratch_shapes=[pltpu.VMEM((tm, D_FF), BF16)],
        compiler_params=pltpu.CompilerParams(dimension_semantics=("arbitrary",),
                                             vmem_limit_bytes=VMEM_LIMIT_BYTES),
        name="out_proj_conv_mlp",
    )(x2d, x2d, x2d, oa, oa, oa, ob, ob, ob, oc, oc, oc, w_out, g_ffn,
      w_up, conv_w, conv_b, w_down, final_g)


def _pad_groups(w, n_groups, width, offset=0):
    r = w.shape[0]
    w = w.reshape(r, n_groups, width)
    w = jnp.pad(w, ((0, 0), (0, 0), (offset, LANES - width - offset)))
    return w.reshape(r, n_groups * LANES)


def _group_a_lane_order():
    nf = HEAD_DIM // 4
    lanes = np.arange(LANES)
    second, r = lanes // HALF, lanes % HALF
    head, rr = r // A_QUARTER, r % A_QUARTER
    block, freq = rr // nf, rr % nf
    dim = block * 2 * nf + second * nf + freq
    return head, dim, block, freq, second


def _group_b_lane_order():
    nf = MLA_ROPE // 4
    src = np.full(LANES, -1)
    block = np.zeros(LANES, np.int64)
    freq = np.zeros(LANES, np.int64)
    second = np.zeros(LANES, np.int64)
    is_rope = np.zeros(LANES, bool)
    for e in range(2):
        for j in range(2 * nf):
            lane = e * HALF + j
            is_rope[lane], block[lane], freq[lane], second[lane] = True, j // nf, j % nf, e
            src[lane] = MLA_NOPE + (j // nf) * 2 * nf + e * nf + j % nf
    n_low = HALF - 2 * nf
    src[2 * nf:HALF] = np.arange(n_low)
    src[HALF + 2 * nf:HALF + 2 * nf + MLA_NOPE - n_low] = np.arange(n_low, MLA_NOPE)
    return src, is_rope, block, freq, second


def _gather_cols(w, src):
    return jnp.where(jnp.asarray(src >= 0), w[:, np.maximum(src, 0)], 0.0)


def _rope_tables(S):
    t = jnp.arange(S, dtype=jnp.int32)
    pos = jnp.stack([(t // GRID_W).astype(F32), (t % GRID_W).astype(F32)], axis=0)

    def table(nf, block, freq, second, is_rope):
        inv = ROPE_BASE ** (-jnp.arange(nf, dtype=F32) / nf)
        ang = pos[block].T * inv[freq]
        sign = jnp.asarray(2.0 * second - 1.0, F32)
        cos = jnp.where(jnp.asarray(is_rope), jnp.cos(ang), 1.0)
        sin = jnp.where(jnp.asarray(is_rope), jnp.sin(ang) * sign, 0.0)
        return cos, sin

    _, _, block, freq, second = _group_a_lane_order()
    cos_a, sin_a = table(HEAD_DIM // 4, block, freq, second, np.ones(LANES, bool))
    _, is_rope, block, freq, second = _group_b_lane_order()
    cos_b, sin_b = table(MLA_ROPE // 4, block, freq, second, is_rope)
    return cos_a, sin_a, cos_b, sin_b


def _prep_w_in(w):
    n_qa, n_ka = N_HEADS_A * HEAD_DIM, N_KV_A * HEAD_DIM
    a_end = n_qa + 2 * n_ka
    b_end = a_end + MLA_Q_RANK + MLA_KV_RANK + MLA_ROPE
    head, dim, _, _, _ = _group_a_lane_order()
    pair_cols = head * HEAD_DIM + dim
    qa_cols = np.concatenate([p * LANES + pair_cols for p in range(N_HEADS_A // 2)])
    cq = w[:, a_end:a_end + MLA_Q_RANK]
    ckv = w[:, a_end + MLA_Q_RANK:a_end + MLA_Q_RANK + MLA_KV_RANK]
    kr = w[:, b_end - MLA_ROPE:b_end]
    src_b, is_rope, _, _, _ = _group_b_lane_order()
    out = jnp.concatenate([
        w[:, qa_cols], w[:, n_qa + pair_cols], w[:, n_qa + n_ka:a_end],
        jnp.pad(cq, ((0, 0), (0, CQ_PAD - MLA_Q_RANK))),
        ckv,
        _gather_cols(kr, np.where(is_rope, src_b - MLA_NOPE, -1)),
        w[:, b_end:],
    ], axis=-1).astype(BF16)
    assert out.shape[1] == IN_COLS
    return out


def _twice(v):
    return jnp.tile(v, 2).reshape(1, 2 * v.shape[0])


def kernel(x, norm_attn, w_in, q_norm_a, k_norm_a, q_a_norm_b, w_uq_b, kv_a_norm_b, w_ukv_b,
           lambda_q1_c, lambda_k1_c, lambda_q2_c, lambda_k2_c, subln_c, w_out,
           norm_ffn, w_up, conv_w, conv_b, w_down, final_norm):
    B, S, D = x.shape
    depth = w_in.shape[0]
    assert D == D_MODEL and S % max(TM_IN, TM_FFN, TQ, TK) == 0 and S % GRID_W == 0
    assert TM_IN == TK
    T = B * S

    tabs = _rope_tables(S)
    row = lambda v: v.reshape(1, -1)
    a_pairs = [(j, j + N_HEADS_A // 2) for j in range(N_HEADS_A // 2)]
    a_rows = np.concatenate([np.arange(h * HEAD_DIM, (h + 1) * HEAD_DIM)
                             for pair in a_pairs for h in pair])
    xc = x.reshape(T, D)
    for l in range(depth):
        lam_init = 0.8 - 0.6 * math.exp(-0.3 * l)
        src_b, is_rope, _, _, _ = _group_b_lane_order()
        d_qk = MLA_NOPE + MLA_ROPE
        wuq_p = jnp.pad(
            jnp.concatenate([_gather_cols(w_uq_b[l][:, h * d_qk:(h + 1) * d_qk], src_b)
                             for h in range(N_HEADS_B)], axis=-1),
            ((0, CQ_PAD - MLA_Q_RANK), (0, 0))).astype(BF16)
        wukv = w_ukv_b[l].reshape(MLA_KV_RANK, N_HEADS_B, MLA_NOPE + MLA_V)
        k_cols = [_gather_cols(wukv[:, h, :MLA_NOPE], np.where(is_rope, -1, src_b))
                  for h in range(N_HEADS_B)]
        v_cols = [_pad_groups(wukv[:, h, MLA_NOPE:], 1, MLA_V, offset=HALF * (h % 2))
                  for h in range(N_HEADS_B)]
        wukv_p = jnp.concatenate(k_cols + v_cols, axis=-1).astype(BF16)

        qa, kta, va, qb, ktb, vb, qc, ktc, vc, nrm = _in_proj(
            xc, row(norm_attn[l]), _prep_w_in(w_in[l]), tabs,
            q_norm_a[l][_group_a_lane_order()[1]].reshape(1, LANES),
            k_norm_a[l][_group_a_lane_order()[1]].reshape(1, LANES),
            jnp.pad(q_a_norm_b[l], (0, CQ_PAD - MLA_Q_RANK)).reshape(1, CQ_PAD),
            row(kv_a_norm_b[l]), wuq_p, wukv_p, B, S)

        nrm = jnp.max(nrm.reshape(B, -1, LANES), axis=1)

        def bounded(q0, nq, q_rep, k0, nk, k_rep, per_pipeline):
            q2 = jnp.repeat(nrm[:, q0:q0 + nq], q_rep, axis=1)
            k2 = jnp.repeat(nrm[:, k0:k0 + nk], k_rep, axis=1)
            ok = q2 * k2 * NRM_MARGIN <= SAFE_LOG2_RANGE ** 2
            return jnp.all(ok.reshape(B, -1, per_pipeline), axis=-1).astype(jnp.int32)

        qk_a = (HEAD_DIM * jnp.max(jnp.abs(q_norm_a[l])) * jnp.max(jnp.abs(k_norm_a[l]))
                * (HEAD_DIM ** -0.5 * LOG2E))
        bounded_a = jnp.broadcast_to(
            (qk_a * NRM_MARGIN <= SAFE_LOG2_RANGE).astype(jnp.int32),
            (B, N_HEADS_A // HEADS_PER_PIPELINE_AB))

        group = N_HEADS_A // N_KV_A
        oa = _attention(
            functools.partial(_attn_ab_kernel, n_heads=N_HEADS_A,
                              per_pipeline=HEADS_PER_PIPELINE_AB, q_index=lambda h: h // 2,
                              kt_index=lambda h: 2 * (h // group) + h % 2,
                              v_index=lambda h: h // group, out_pairs=a_pairs, n_keys=S),
            bounded_a, N_HEADS_A, qa, kta, va, [], N_HEADS_A * HEAD_DIM, B, S)
        same = lambda h: h
        ob = _attention(
            functools.partial(_attn_ab_kernel, n_heads=N_HEADS_B,
                              per_pipeline=HEADS_PER_PIPELINE_AB, q_index=same, kt_index=same,
                              v_index=same,
                              out_pairs=[(2 * j, 2 * j + 1) for j in range(N_HEADS_B // 2)],
                              n_keys=S),
            bounded(NRM_QB, N_HEADS_B, 1, NRM_KB, N_HEADS_B, 1, HEADS_PER_PIPELINE_AB), N_HEADS_B,
            qb, ktb, vb, [], N_HEADS_B * MLA_V, B, S)
        oc = _attention(
            functools.partial(_attn_c_kernel, n_keys=S, lam_init=lam_init),
            bounded(NRM_QC, N_HEADS_C, 2, NRM_KC, 2 * N_HEADS_C, 1, HEADS_PER_PIPELINE),
            2 * N_HEADS_C, qc, ktc, vc,
            [row(lambda_q1_c[l]), row(lambda_k1_c[l]), row(lambda_q2_c[l]), row(lambda_k2_c[l]),
             _twice(subln_c[l])],
            N_HEADS_C * DIFF_V, B, S)

        w_out_l = jnp.concatenate([w_out[l][a_rows], w_out[l][N_HEADS_A * HEAD_DIM:]], axis=0)
        xc = _mlp(xc, oa, ob, oc, w_out_l.astype(BF16), row(norm_ffn[l]), w_up[l].astype(BF16),
                  conv_w[l].reshape(3, 2, D_FF).transpose(1, 0, 2),
                  conv_b[l].reshape(2, 1, D_FF), w_down[l].astype(BF16),
                  row(final_norm), S, final_norm=(l == depth - 1))
    return xc.reshape(B, S, D)
```

```python
import functools
import math

import jax
import jax.numpy as jnp
import numpy as np
from jax import lax
from jax.experimental import pallas as pl
from jax.experimental.pallas import tpu as pltpu

D_MODEL = 1024
GRID_W = 64
HEAD_DIM = 64
N_HEADS_A = 8
N_KV_A = 2
N_HEADS_B = 4
MLA_Q_RANK = 192
MLA_KV_RANK = 128
MLA_NOPE = 64
MLA_ROPE = 32
MLA_V = 64
N_HEADS_C = 4
DIFF_QK = 32
DIFF_V = 64
D_FF = 2816
ROPE_BASE = 10000.0
EPS = 1e-6

LANES = 128
HALF = LANES // 2
A_QUARTER = LANES // 4
BF16_SUBLANES = 16
VMEM_LIMIT_BYTES = 56 * 1024 * 1024

LOG2E = math.log2(math.e)
assert HEAD_DIM == MLA_V == DIFF_V == 2 * DIFF_QK == HALF

OFF_QA = 0
OFF_KA = OFF_QA + N_HEADS_A * HEAD_DIM
OFF_VA = OFF_KA + LANES
OFF_CQ = OFF_VA + LANES
CQ_PAD = 2 * LANES
OFF_CKV = OFF_CQ + CQ_PAD
OFF_KR = OFF_CKV + MLA_KV_RANK
OFF_QC = OFF_KR + LANES
OFF_KC = OFF_QC + N_HEADS_C * HALF
OFF_VC = OFF_KC + N_HEADS_C * HALF
IN_COLS = OFF_VC + N_HEADS_C * HALF

NRM_QB = 0
NRM_KB = NRM_QB + N_HEADS_B
NRM_QC = NRM_KB + N_HEADS_B
NRM_KC = NRM_QC + N_HEADS_C
NRM_GROUPS = 2 * N_HEADS_B + N_HEADS_C
NRM_MARGIN = 1.03
SAFE_LOG2_RANGE = 55.0

F32 = jnp.float32
BF16 = jnp.bfloat16

TM_IN = 512
TM_FFN = 512
FFN_CHUNK = 256
FFN_DOWN_GROUP = 8 * FFN_CHUNK
FFN_DOWN_LAG = 1
TQ = 512
TQ_SUB_AB = 512
TQ_SUB_C = 256
TK = 512
SCORE_LOOKAHEAD = 2
HEADS_PER_PIPELINE = 4
HEADS_PER_PIPELINE_AB = 4


def _dot(a, b):
    return jnp.dot(a, b, preferred_element_type=F32)


def _rms(x, n):
    ms = jnp.sum(x * x, axis=-1, keepdims=True) * (1.0 / n)
    return x * lax.rsqrt(ms + EPS)


def _low_lanes():
    return lax.broadcasted_iota(jnp.int32, (1, LANES), 1) < HALF


def _half_rms(x, low=None):
    low = _low_lanes() if low is None else low
    sq = x * x
    s_lo = jnp.sum(jnp.where(low, sq, 0.0), axis=-1, keepdims=True)
    s_hi = jnp.sum(jnp.where(low, 0.0, sq), axis=-1, keepdims=True)
    return x * lax.rsqrt(jnp.where(low, s_lo, s_hi) * (1.0 / HALF) + EPS)


def _in_kernel(x_ref, gat_ref, w_ref, cos_a_ref, sin_a_ref, cos_b_ref, sin_b_ref,
               qg_a_ref, kg_a_ref, qg_b_ref, kvg_b_ref, wuq_ref, wukv_ref, sel_ref,
               qa_ref, kta_ref, va_ref, qb_ref, ktb_ref, vb_ref, qc_ref, ktc_ref, vc_ref, nrm_ref):
    x = x_ref[...]
    tm = x.shape[0]
    h = (_rms(x, D_MODEL) * gat_ref[...]).astype(BF16)

    lane = lax.broadcasted_iota(jnp.int32, (1, LANES), 1)
    low = lane < HALF
    row = lax.broadcasted_iota(jnp.int32, (LANES, 1), 0)

    z = _dot(h, w_ref[...])

    def seg(off, width):
        return z[:, off:off + width]

    def rope(y, c, s):
        return y * c + pltpu.roll(y, HALF, axis=1) * s

    def group(z, g):
        return z[:, g * LANES:(g + 1) * LANES]

    squares = []

    def note(y):
        squares.append((y * y).astype(BF16))

    def noted_row_sums(first):
        return _dot(jnp.concatenate(squares[first:], axis=-1),
                    sel_ref[first * LANES:len(squares) * LANES])

    cos_b, sin_b = cos_b_ref[...], sin_b_ref[...]
    cq =(_rms(seg(OFF_CQ, CQ_PAD), MLA_Q_RANK) * qg_b_ref[...]).astype(BF16)
    qb = _dot(cq, wuq_ref[...])
    scale_b = (MLA_NOPE + MLA_ROPE) ** -0.5 * LOG2E
    for hh in range(N_HEADS_B):
        y = rope(group(qb, hh), cos_b, sin_b) * scale_b
        qb_ref[0, hh] = y.astype(BF16)
        note(y)
    ckv = (_rms(seg(OFF_CKV, MLA_KV_RANK), MLA_KV_RANK) * kvg_b_ref[...]).astype(BF16)
    kvb = _dot(ckv, wukv_ref[...])
    kr = rope(seg(OFF_KR, LANES), cos_b, sin_b)
    for hh in range(N_HEADS_B):
        y = group(kvb, hh) + kr
        ktb_ref[0, hh] = y.T.astype(BF16)
        note(y)
        ones_other_half = jnp.where(low, float(hh % 2), float(1 - hh % 2))
        vb_ref[0, hh] = (group(kvb, N_HEADS_B + hh) + ones_other_half).astype(BF16)
    n2 = noted_row_sums(0)
    n_noted_b = len(squares)

    zq = seg(OFF_QC, N_HEADS_C * HALF)
    zk = seg(OFF_KC, N_HEADS_C * HALF)
    zv = seg(OFF_VC, N_HEADS_C * HALF)
    k_groups = []
    for p in range(N_HEADS_C // 2):
        y = group(zq, p) * (DIFF_QK ** -0.5 * LOG2E)
        qc_ref[0, p] = y.astype(BF16)
        note(y)
        k_groups.append(group(zk, p))
        kt = group(zk, p).T
        for sub in range(4):
            mine = (row >= sub * DIFF_QK) & (row < (sub + 1) * DIFF_QK)
            ktc_ref[0, 4 * p + sub, 0] = jnp.where(mine, kt, 0.0).astype(BF16)
        vz = group(zv, p)
        vc_ref[0, 2 * p] = jnp.where(low, vz, 1.0).astype(BF16)
        vc_ref[0, 2 * p + 1] = jnp.where(low, 1.0, vz).astype(BF16)
    for kz in k_groups:
        note(kz)
    n2 = n2 + noted_row_sums(n_noted_b)
    nrm_ref[0] = jnp.max(n2, axis=0, keepdims=True)

    cos_a, sin_a = cos_a_ref[...], sin_a_ref[...]
    first_head = (lane & A_QUARTER) == 0
    zq = seg(OFF_QA, N_HEADS_A * HEAD_DIM)
    for p in range(N_HEADS_A // 2):
        y = rope(_half_rms(group(zq, p), first_head) * qg_a_ref[...], cos_a, sin_a)
        qa_ref[0, p] = (y * (HEAD_DIM ** -0.5 * LOG2E)).astype(BF16)
    y = rope(_half_rms(seg(OFF_KA, LANES), first_head) * kg_a_ref[...], cos_a, sin_a)
    kt = y.T.astype(BF16)
    quarter = [kt[j * A_QUARTER:(j + 1) * A_QUARTER] for j in range(4)]
    zeros = jnp.zeros((A_QUARTER, tm), BF16)
    for g in range(N_KV_A):
        k1, k2 = quarter[g], quarter[2 + g]
        kta_ref[0, 2 * g] = jnp.concatenate([k1, zeros, k2, zeros], axis=0)
        kta_ref[0, 2 * g + 1] = jnp.concatenate([zeros, k1, zeros, k2], axis=0)
    zv = seg(OFF_VA, LANES)
    va_ref[0, 0] = jnp.where(low, zv, 1.0).astype(BF16)
    va_ref[0, 1] = jnp.where(low, 1.0, zv).astype(BF16)


def _norm_routing():
    sel = np.zeros((NRM_GROUPS, LANES, LANES), np.float32)
    g = 0
    for slot in ([NRM_QB + h for h in range(N_HEADS_B)] + [NRM_KB + h for h in range(N_HEADS_B)]):
        sel[g, :, slot] = 1.0
        g += 1
    for p in range(N_HEADS_C // 2):
        for f in range(2):
            sel[g, f * HALF:(f + 1) * HALF, NRM_QC + 2 * p + f] = 1.0
        g += 1
    for p in range(N_HEADS_C // 2):
        for sub in range(4):
            sel[g, sub * DIFF_QK:(sub + 1) * DIFF_QK, NRM_KC + 4 * p + sub] = 1.0
        g += 1
    assert g == NRM_GROUPS
    return jnp.asarray(sel.reshape(NRM_GROUPS * LANES, LANES), BF16)


def _in_proj(x2d, gat, w_in_r, tabs, qg_a, kg_a, qg_b, kvg_b, wuq_p, wukv_p, B, S):
    T = B * S
    tm = TM_IN
    nst = S // tm
    const = lambda shape: pl.BlockSpec(shape, lambda i: (0,) * len(shape))
    tab = pl.BlockSpec((tm, LANES), lambda i: (i % nst, 0))
    hm = lambda H: pl.BlockSpec((1, H, tm, LANES), lambda i: (i // nst, 0, i % nst, 0))
    hmt = lambda H: pl.BlockSpec((1, H, LANES, tm), lambda i: (i // nst, 0, 0, i % nst))
    sds = lambda H: jax.ShapeDtypeStruct((B, H, S, LANES), BF16)
    sdt = lambda H: jax.ShapeDtypeStruct((B, H, LANES, S), BF16)
    return pl.pallas_call(
        _in_kernel,
        grid=(T // tm,),
        in_specs=[pl.BlockSpec((tm, D_MODEL), lambda i: (i, 0)),
                  const((1, D_MODEL)), const((D_MODEL, IN_COLS)),
                  tab, tab, tab, tab,
                  const((1, LANES)), const((1, LANES)), const((1, CQ_PAD)), const((1, MLA_KV_RANK)),
                  const((CQ_PAD, N_HEADS_B * LANES)), const((MLA_KV_RANK, 2 * N_HEADS_B * LANES)),
                  const((NRM_GROUPS * LANES, LANES))],
        out_specs=[hm(N_HEADS_A // 2), hmt(2 * N_KV_A), hm(N_KV_A),
                   hm(N_HEADS_B), hmt(N_HEADS_B), hm(N_HEADS_B),
                   hm(N_HEADS_C // 2),
                   pl.BlockSpec((1, 2 * N_HEADS_C, 1, LANES, tm),
                                lambda i: (i // nst, 0, i % nst, 0, 0)),
                   hm(N_HEADS_C),
                   pl.BlockSpec((1, 1, LANES), lambda i: (i, 0, 0))],
        out_shape=[sds(N_HEADS_A // 2), sdt(2 * N_KV_A), sds(N_KV_A),
                   sds(N_HEADS_B), sdt(N_HEADS_B), sds(N_HEADS_B),
                   sds(N_HEADS_C // 2),
                   jax.ShapeDtypeStruct((B, 2 * N_HEADS_C, S // tm, LANES, tm), BF16),
                   sds(N_HEADS_C),
                   jax.ShapeDtypeStruct((T // tm, 1, LANES), F32)],
        compiler_params=pltpu.CompilerParams(dimension_semantics=("arbitrary",),
                                             vmem_limit_bytes=VMEM_LIMIT_BYTES),
        name="in_proj",
    )(x2d, gat, w_in_r, *tabs, qg_a, kg_a, qg_b, kvg_b, wuq_p, wukv_p, _norm_routing())


def _lane_group_max(s):
    m = s[:, :LANES]
    for j in range(1, s.shape[1] // LANES):
        m = jnp.maximum(m, s[:, j * LANES:(j + 1) * LANES])
    return m


def _pv_bounded(streams, n_chunks):
    items = [(r, i) for r in range(n_chunks) for i in range(len(streams))]
    accs = [None] * len(streams)
    in_flight = {}
    for k in range(len(items) + SCORE_LOOKAHEAD):
        if k < len(items):
            r, i = items[k]
            scores, _, row_off = streams[i]
            s, off = scores(r), row_off(r)
            in_flight[k] = s if off is None else s + off
        if k >= SCORE_LOOKAHEAD:
            r, i = items[k - SCORE_LOOKAHEAD]
            d = _dot(jnp.exp2(in_flight.pop(k - SCORE_LOOKAHEAD)).astype(BF16), streams[i][1](r))
            accs[i] = d if accs[i] is None else accs[i] + d
    return accs


def _pv_exact(streams, n_chunks, s_ref):
    rows = s_ref.shape[0] // len(streams)
    maxes = []
    for i, (scores, _, row_off) in enumerate(streams):
        m_run = None
        for r in range(n_chunks):
            s = scores(r)
            s_ref[i * rows:(i + 1) * rows, r * TK:(r + 1) * TK] = s
            m, off = _lane_group_max(s), row_off(r)
            if off is not None:
                m = m + off
            m_run = m if m_run is None else jnp.maximum(m_run, m)
        maxes.append(jnp.max(m_run, axis=-1, keepdims=True))
    accs = []
    for i, (_, values, row_off) in enumerate(streams):
        acc = None
        for r in range(n_chunks):
            off = row_off(r)
            shift = maxes[i] if off is None else maxes[i] - off
            p = jnp.exp2(s_ref[i * rows:(i + 1) * rows, r * TK:(r + 1) * TK] - shift)
            d = _dot(p.astype(BF16), values(r))
            acc = d if acc is None else acc + d
        accs.append(acc)
    return accs


def _run_heads(bounded, heads, head_streams, n_chunks, s_ref, acc_ref, roll_exact=False):
    @pl.when(bounded)
    def _():
        shared = {}
        per_head = [head_streams(h, shared) for h in heads]
        n_sub = len(per_head[0])
        accs = _pv_bounded([s for streams in per_head for s in streams], n_chunks)
        for k, h in enumerate(heads):
            acc_ref[h] = jnp.concatenate(accs[k * n_sub:(k + 1) * n_sub], axis=0)

    @pl.when(jnp.logical_not(bounded))
    def _():
        def one_head(h):
            acc_ref[h] = jnp.concatenate(_pv_exact(head_streams(h, {}), n_chunks, s_ref), axis=0)

        if roll_exact:
            lax.fori_loop(0, len(heads), lambda k, c: (one_head(heads[0] + k), c)[1], 0)
        else:
            for h in heads:
                one_head(h)


def _normalised_pair(acc_low, acc_high):
    low = _low_lanes()
    num = jnp.where(low, acc_low, acc_high)
    den = pltpu.roll(jnp.where(low, acc_high, acc_low), HALF, axis=1)
    return num / den


def _attn_ab_kernel(bounded_ref, q_ref, kt_ref, v_ref, o_ref, s_ref, acc_ref, *,
                    n_heads, per_pipeline, q_index, kt_index, v_index, out_pairs, n_keys):
    n_chunks = n_keys // TK

    def head_streams(h, shared):
        q = q_ref[0, q_index(h)]
        return [(lambda r, qs=q[r0:r0 + TQ_SUB_AB]:
                 _dot(qs, kt_ref[0, kt_index(h), :, r * TK:(r + 1) * TK]),
                 lambda r: v_ref[0, v_index(h), r * TK:(r + 1) * TK, :],
                 lambda r: None) for r0 in range(0, TQ, TQ_SUB_AB)]

    def head_group(j, carry):
        heads = [per_pipeline * j + k for k in range(per_pipeline)]
        _run_heads(bounded_ref[pl.program_id(0), j] != 0, heads, head_streams, n_chunks, s_ref,
                   acc_ref, roll_exact=True)
        return carry

    if n_heads == per_pipeline:
        head_group(0, None)
    else:
        lax.fori_loop(0, n_heads // per_pipeline, head_group, 0)
    o_ref[...] = jnp.concatenate([_normalised_pair(acc_ref[a], acc_ref[b]) for a, b in out_pairs],
                                 axis=-1).astype(BF16)


def _attn_c_kernel(bounded_ref, q_ref, kt_ref, v_ref, lq1_ref, lk1_ref, lq2_ref, lk2_ref,
                   subln_ref, o_ref, s_ref, acc_ref, *, n_keys, lam_init):
    assert TQ == TK
    nc = n_keys // TK
    cd = pl.program_id(1)
    col = lax.broadcasted_iota(jnp.int32, (1, TK), 1).astype(F32)
    row = lax.broadcasted_iota(jnp.int32, (TQ_SUB_C, 1), 0).astype(F32)
    subs = [(r, r + TQ_SUB_C) for r in range(0, TQ, TQ_SUB_C)]
    neg_dist = [-jnp.abs((lax.broadcasted_iota(jnp.int32, (TQ_SUB_C, TK), 0) + r0
                          - lax.broadcasted_iota(jnp.int32, (TQ_SUB_C, TK), 1)).astype(F32))
                for r0, _ in subs]

    chunk = [(cd + r) % nc for r in range(nc)]

    def head_bias(hh):
        slope = jnp.float32(LOG2E * 2.0 ** (-8.0 * N_HEADS_C / N_HEADS_C))
        for k in range(N_HEADS_C - 2, -1, -1):
            slope = jnp.where(hh == k, LOG2E * 2.0 ** (-8.0 * (k + 1) / N_HEADS_C), slope)
        key_term, row_off = [None], [[None] * len(subs)]
        for r in range(1, nc):
            side = jnp.where(chunk[r] < cd, slope, -slope)
            key_term.append(col * side)
            base = -slope * (jnp.abs(chunk[r] - cd) * TK).astype(F32)
            row_off.append([base - side * (row + float(r0)) for r0, _ in subs])
        return slope, key_term, row_off

    def head_pair(p, carry):
        def streams_of(k, shared):
            idx, hh = 4 * p + k, 2 * p + k // 2
            if isinstance(k, int):
                if k // 2 not in shared:
                    shared[k // 2] = head_bias(hh)
                slope, key_term, row_off = shared[k // 2]
            else:
                slope, key_term, row_off = head_bias(hh)

            def scores(r, si):
                s = _dot(q_ref[0, p][subs[si][0]:subs[si][1]], kt_ref[0, idx, chunk[r]])
                return s + (neg_dist[si] * slope if r == 0 else key_term[r])

            return [(functools.partial(scores, si=si),
                     lambda r: v_ref[0, hh, pl.ds(pl.multiple_of(chunk[r] * TK, TK), TK), :],
                     lambda r, si=si: row_off[r][si]) for si in range(len(subs))]

        _run_heads(bounded_ref[pl.program_id(0), p] != 0, list(range(4)), streams_of, nc, s_ref,
                   acc_ref.at[pl.ds(4 * p, 4)], roll_exact=True)
        return carry

    assert HEADS_PER_PIPELINE == 4
    lax.fori_loop(0, N_HEADS_C // 2, head_pair, 0)

    lam = (jnp.exp(jnp.sum(lq1_ref[...] * lk1_ref[...], axis=-1, keepdims=True))
           - jnp.exp(jnp.sum(lq2_ref[...] * lk2_ref[...], axis=-1, keepdims=True)) + lam_init)
    outs = []
    for p in range(N_HEADS_C // 2):
        o = (_normalised_pair(acc_ref[4 * p], acc_ref[4 * p + 2])
             - lam * _normalised_pair(acc_ref[4 * p + 1], acc_ref[4 * p + 3]))
        outs.append(_half_rms(o) * subln_ref[...] * (1.0 - lam_init))
    o_ref[...] = jnp.concatenate(outs, axis=-1).astype(BF16)


def _attention(kernel_fn, bounded, n_acc, q, kt, v, extras, out_width, B, S):
    nq = S // TQ
    hq, hv = q.shape[1], v.shape[1]
    extra_specs = [pl.BlockSpec(e.shape, lambda b, i, f, n=e.ndim: (0,) * n) for e in extras]
    return pl.pallas_call(
        kernel_fn,
        grid_spec=pltpu.PrefetchScalarGridSpec(
            num_scalar_prefetch=1,
            grid=(B, nq),
            in_specs=[pl.BlockSpec((1, hq, TQ, LANES), lambda b, i, f: (b, 0, i, 0)),
                      pl.BlockSpec((1,) + kt.shape[1:],
                                   lambda b, i, f, n=kt.ndim: (b,) + (0,) * (n - 1)),
                      pl.BlockSpec((1, hv, S, LANES), lambda b, i, f: (b, 0, 0, 0))] + extra_specs,
            out_specs=pl.BlockSpec((TQ, out_width), lambda b, i, f: (b * nq + i, 0)),
            scratch_shapes=[pltpu.VMEM((TQ, S), F32),
                            pltpu.VMEM((n_acc, TQ, LANES), F32)]),
        out_shape=jax.ShapeDtypeStruct((B * S, out_width), BF16),
        compiler_params=pltpu.CompilerParams(dimension_semantics=("arbitrary", "arbitrary"),
                                             vmem_limit_bytes=VMEM_LIMIT_BYTES),
        name=kernel_fn.func.__name__.strip("_"),
    )(bounded, q, kt, v, *extras)


def _mlp_kernel(xp_ref, x_ref, xn_ref, ap_ref, a_ref, an_ref, bp_ref, b_ref, bn_ref,
                cp_ref, c_ref, cn_ref, wo_ref, g_ref, wu_ref, cw_ref, cb_ref, wd_ref, fg_ref,
                o_ref, act_ref, *, tiles_per_seq, final_norm):
    i = pl.program_id(0)
    tm = x_ref.shape[0]
    halo = xp_ref.shape[0]

    def rows(p, m, n):
        return jnp.concatenate([p[...], m[...], n[...]], axis=0)

    o_ext = jnp.concatenate([rows(ap_ref, a_ref, an_ref), rows(bp_ref, b_ref, bn_ref),
                             rows(cp_ref, c_ref, cn_ref)], axis=1)
    x1 = rows(xp_ref, x_ref, xn_ref) + _dot(o_ext, wo_ref[...])
    h = (_rms(x1, D_MODEL) * g_ref[...]).astype(BF16)
    at_start = (i % tiles_per_seq) == 0
    at_end = (i % tiles_per_seq) == tiles_per_seq - 1
    zeros = jnp.zeros((halo, D_MODEL), BF16)
    hext = jnp.concatenate([jnp.where(at_start, zeros, h[:halo]), h[halo:halo + tm],
                            jnp.where(at_end, zeros, h[halo + tm:])], axis=0)

    def branch(k, c0):
        u = _dot(hext, wu_ref[:, k * D_FF + c0:k * D_FF + c0 + FFN_CHUNK])
        cw = cw_ref[k, :, c0:c0 + FFN_CHUNK]
        n = u.shape[0]
        prev = pltpu.roll(u, 1, axis=0)[halo:halo + tm]
        nxt = pltpu.roll(u, n - 1, axis=0)[halo:halo + tm]
        return (prev * cw[0:1] + u[halo:halo + tm] * cw[1:2] + nxt * cw[2:3]
                + cb_ref[k, :, c0:c0 + FFN_CHUNK])

    y = x1[halo:halo + tm]
    done = 0
    for c0 in range(0, D_FF, FFN_CHUNK):
        g = branch(0, c0)
        val = branch(1, c0)
        half = 0.5 * g
        act_ref[:, c0:c0 + FFN_CHUNK] = ((half + half * jnp.tanh(half)) * val).astype(BF16)
        ready = c0 + FFN_CHUNK - FFN_DOWN_LAG * FFN_CHUNK
        if ready - done >= FFN_DOWN_GROUP:
            y = y + _dot(act_ref[:, done:done + FFN_DOWN_GROUP], wd_ref[done:done + FFN_DOWN_GROUP])
            done += FFN_DOWN_GROUP
    y = y + _dot(act_ref[:, done:], wd_ref[done:])
    if final_norm:
        y = _rms(y, D_MODEL) * fg_ref[...]
    o_ref[...] = y


def _mlp(x2d, oa, ob, oc, w_out, g_ffn, w_up, conv_w, conv_b, w_down, final_g, S, final_norm):
    T = x2d.shape[0]
    tm, halo = TM_FFN, BF16_SUBLANES
    per = tm // halo
    last = T // halo - 1
    resident = lambda shape: pl.BlockSpec(shape, lambda i: (0,) * len(shape),
                                          pipeline_mode=pl.Buffered(1))

    def with_halos(width):
        return [pl.BlockSpec((halo, width), lambda i: (jnp.maximum(i * per - 1, 0), 0)),
                pl.BlockSpec((tm, width), lambda i: (i, 0)),
                pl.BlockSpec((halo, width), lambda i: (jnp.minimum((i + 1) * per, last), 0))]

    return pl.pallas_call(
        functools.partial(_mlp_kernel, tiles_per_seq=S // tm, final_norm=final_norm),
        grid=(T // tm,),
        in_specs=(with_halos(D_MODEL) + with_halos(oa.shape[1]) + with_halos(ob.shape[1])
                  + with_halos(oc.shape[1])
                  + [resident(w_out.shape), resident((1, D_MODEL)),
                     resident((D_MODEL, 2 * D_FF)), resident((2, 3, D_FF)), resident((2, 1, D_FF)),
                     resident((D_FF, D_MODEL)), resident((1, D_MODEL))]),
        out_specs=pl.BlockSpec((tm, D_MODEL), lambda i: (i, 0)),
        out_shape=jax.ShapeDtypeStruct((T, D_MODEL), F32),
        scratch_shapes=[pltpu.VMEM((tm, D_FF), BF16)],
        compiler_params=pltpu.CompilerParams(dimension_semantics=("arbitrary",),
                                             vmem_limit_bytes=VMEM_LIMIT_BYTES),
        name="out_proj_conv_mlp",
    )(x2d, x2d, x2d, oa, oa, oa, ob, ob, ob, oc, oc, oc, w_out, g_ffn,
      w_up, conv_w, conv_b, w_down, final_g)


def _pad_groups(w, n_groups, width, offset=0):
    r = w.shape[0]
    w = w.reshape(r, n_groups, width)
    w = jnp.pad(w, ((0, 0), (0, 0), (offset, LANES - width - offset)))
    return w.reshape(r, n_groups * LANES)


def _group_a_lane_order():
    nf = HEAD_DIM // 4
    lanes = np.arange(LANES)
    second, r = lanes // HALF, lanes % HALF
    head, rr = r // A_QUARTER, r % A_QUARTER
    block, freq = rr // nf, rr % nf
    dim = block * 2 * nf + second * nf + freq
    return head, dim, block, freq, second


def _group_b_lane_order():
    nf = MLA_ROPE // 4
    src = np.full(LANES, -1)
    block = np.zeros(LANES, np.int64)
    freq = np.zeros(LANES, np.int64)
    second = np.zeros(LANES, np.int64)
    is_rope = np.zeros(LANES, bool)
    for e in range(2):
        for j in range(2 * nf):
            lane = e * HALF + j
            is_rope[lane], block[lane], freq[lane], second[lane] = True, j // nf, j % nf, e
            src[lane] = MLA_NOPE + (j // nf) * 2 * nf + e * nf + j % nf
    n_low = HALF - 2 * nf
    src[2 * nf:HALF] = np.arange(n_low)
    src[HALF + 2 * nf:HALF + 2 * nf + MLA_NOPE - n_low] = np.arange(n_low, MLA_NOPE)
    return src, is_rope, block, freq, second


def _gather_cols(w, src):
    return jnp.where(jnp.asarray(src >= 0), w[:, np.maximum(src, 0)], 0.0)


def _rope_tables(S):
    t = jnp.arange(S, dtype=jnp.int32)
    pos = jnp.stack([(t // GRID_W).astype(F32), (t % GRID_W).astype(F32)], axis=0)

    def table(nf, block, freq, second, is_rope):
        inv = ROPE_BASE ** (-jnp.arange(nf, dtype=F32) / nf)
        ang = pos[block].T * inv[freq]
        sign = jnp.asarray(2.0 * second - 1.0, F32)
        cos = jnp.where(jnp.asarray(is_rope), jnp.cos(ang), 1.0)
        sin = jnp.where(jnp.asarray(is_rope), jnp.sin(ang) * sign, 0.0)
        return cos, sin

    _, _, block, freq, second = _group_a_lane_order()
    cos_a, sin_a = table(HEAD_DIM // 4, block, freq, second, np.ones(LANES, bool))
    _, is_rope, block, freq, second = _group_b_lane_order()
    cos_b, sin_b = table(MLA_ROPE // 4, block, freq, second, is_rope)
    return cos_a, sin_a, cos_b, sin_b


def _prep_w_in(w):
    n_qa, n_ka = N_HEADS_A * HEAD_DIM, N_KV_A * HEAD_DIM
    a_end = n_qa + 2 * n_ka
    b_end = a_end + MLA_Q_RANK + MLA_KV_RANK + MLA_ROPE
    head, dim, _, _, _ = _group_a_lane_order()
    pair_cols = head * HEAD_DIM + dim
    qa_cols = np.concatenate([p * LANES + pair_cols for p in range(N_HEADS_A // 2)])
    cq = w[:, a_end:a_end + MLA_Q_RANK]
    ckv = w[:, a_end + MLA_Q_RANK:a_end + MLA_Q_RANK + MLA_KV_RANK]
    kr = w[:, b_end - MLA_ROPE:b_end]
    src_b, is_rope, _, _, _ = _group_b_lane_order()
    out = jnp.concatenate([
        w[:, qa_cols], w[:, n_qa + pair_cols], w[:, n_qa + n_ka:a_end],
        jnp.pad(cq, ((0, 0), (0, CQ_PAD - MLA_Q_RANK))),
        ckv,
        _gather_cols(kr, np.where(is_rope, src_b - MLA_NOPE, -1)),
        w[:, b_end:],
    ], axis=-1).astype(BF16)
    assert out.shape[1] == IN_COLS
    return out


def _twice(v):
    return jnp.tile(v, 2).reshape(1, 2 * v.shape[0])


def kernel(x, norm_attn, w_in, q_norm_a, k_norm_a, q_a_norm_b, w_uq_b, kv_a_norm_b, w_ukv_b,
           lambda_q1_c, lambda_k1_c, lambda_q2_c, lambda_k2_c, subln_c, w_out,
           norm_ffn, w_up, conv_w, conv_b, w_down, final_norm):
    B, S, D = x.shape
    depth = w_in.shape[0]
    assert D == D_MODEL and S % max(TM_IN, TM_FFN, TQ, TK) == 0 and S % GRID_W == 0
    assert TM_IN == TK
    T = B * S

    tabs = _rope_tables(S)
    row = lambda v: v.reshape(1, -1)
    a_pairs = [(j, j + N_HEADS_A // 2) for j in range(N_HEADS_A // 2)]
    a_rows = np.concatenate([np.arange(h * HEAD_DIM, (h + 1) * HEAD_DIM)
                             for pair in a_pairs for h in pair])
    xc = x.reshape(T, D)
    for l in range(depth):
        lam_init = 0.8 - 0.6 * math.exp(-0.3 * l)
        src_b, is_rope, _, _, _ = _group_b_lane_order()
        d_qk = MLA_NOPE + MLA_ROPE
        wuq_p = jnp.pad(
            jnp.concatenate([_gather_cols(w_uq_b[l][:, h * d_qk:(h + 1) * d_qk], src_b)
                             for h in range(N_HEADS_B)], axis=-1),
            ((0, CQ_PAD - MLA_Q_RANK), (0, 0))).astype(BF16)
        wukv = w_ukv_b[l].reshape(MLA_KV_RANK, N_HEADS_B, MLA_NOPE + MLA_V)
        k_cols = [_gather_cols(wukv[:, h, :MLA_NOPE], np.where(is_rope, -1, src_b))
                  for h in range(N_HEADS_B)]
        v_cols = [_pad_groups(wukv[:, h, MLA_NOPE:], 1, MLA_V, offset=HALF * (h % 2))
                  for h in range(N_HEADS_B)]
        wukv_p = jnp.concatenate(k_cols + v_cols, axis=-1).astype(BF16)

        qa, kta, va, qb, ktb, vb, qc, ktc, vc, nrm = _in_proj(
            xc, row(norm_attn[l]), _prep_w_in(w_in[l]), tabs,
            q_norm_a[l][_group_a_lane_order()[1]].reshape(1, LANES),
            k_norm_a[l][_group_a_lane_order()[1]].reshape(1, LANES),
            jnp.pad(q_a_norm_b[l], (0, CQ_PAD - MLA_Q_RANK)).reshape(1, CQ_PAD),
            row(kv_a_norm_b[l]), wuq_p, wukv_p, B, S)

        nrm = jnp.max(nrm.reshape(B, -1, LANES), axis=1)

        def bounded(q0, nq, q_rep, k0, nk, k_rep, per_pipeline):
            q2 = jnp.repeat(nrm[:, q0:q0 + nq], q_rep, axis=1)
            k2 = jnp.repeat(nrm[:, k0:k0 + nk], k_rep, axis=1)
            ok = q2 * k2 * NRM_MARGIN <= SAFE_LOG2_RANGE ** 2
            return jnp.all(ok.reshape(B, -1, per_pipeline), axis=-1).astype(jnp.int32)

        qk_a = (HEAD_DIM * jnp.max(jnp.abs(q_norm_a[l])) * jnp.max(jnp.abs(k_norm_a[l]))
                * (HEAD_DIM ** -0.5 * LOG2E))
        bounded_a = jnp.broadcast_to(
            (qk_a * NRM_MARGIN <= SAFE_LOG2_RANGE).astype(jnp.int32),
            (B, N_HEADS_A // HEADS_PER_PIPELINE_AB))

        group = N_HEADS_A // N_KV_A
        oa = _attention(
            functools.partial(_attn_ab_kernel, n_heads=N_HEADS_A,
                              per_pipeline=HEADS_PER_PIPELINE_AB, q_index=lambda h: h // 2,
                              kt_index=lambda h: 2 * (h // group) + h % 2,
                              v_index=lambda h: h // group, out_pairs=a_pairs, n_keys=S),
            bounded_a, N_HEADS_A, qa, kta, va, [], N_HEADS_A * HEAD_DIM, B, S)
        same = lambda h: h
        ob = _attention(
            functools.partial(_attn_ab_kernel, n_heads=N_HEADS_B,
                              per_pipeline=HEADS_PER_PIPELINE_AB, q_index=same, kt_index=same,
                              v_index=same,
                              out_pairs=[(2 * j, 2 * j + 1) for j in range(N_HEADS_B // 2)],
                              n_keys=S),
            bounded(NRM_QB, N_HEADS_B, 1, NRM_KB, N_HEADS_B, 1, HEADS_PER_PIPELINE_AB), N_HEADS_B,
            qb, ktb, vb, [], N_HEADS_B * MLA_V, B, S)
        oc = _attention(
            functools.partial(_attn_c_kernel, n_keys=S, lam_init=lam_init),
            bounded(NRM_QC, N_HEADS_C, 2, NRM_KC, 2 * N_HEADS_C, 1, HEADS_PER_PIPELINE),
            2 * N_HEADS_C, qc, ktc, vc,
            [row(lambda_q1_c[l]), row(lambda_k1_c[l]), row(lambda_q2_c[l]), row(lambda_k2_c[l]),
             _twice(subln_c[l])],
            N_HEADS_C * DIFF_V, B, S)

        w_out_l = jnp.concatenate([w_out[l][a_rows], w_out[l][N_HEADS_A * HEAD_DIM:]], axis=0)
        xc = _mlp(xc, oa, ob, oc, w_out_l.astype(BF16), row(norm_ffn[l]), w_up[l].astype(BF16),
                  conv_w[l].reshape(3, 2, D_FF).transpose(1, 0, 2),
                  conv_b[l].reshape(2, 1, D_FF), w_down[l].astype(BF16),
                  row(final_norm), S, final_norm=(l == depth - 1))
    return xc.reshape(B, S, D)
```

```python
import functools
import math

import jax
import jax.numpy as jnp
import numpy as np
from jax import lax
from jax.experimental import pallas as pl
from jax.experimental.pallas import tpu as pltpu

D_MODEL = 1024
GRID_W = 64
HEAD_DIM = 64
N_HEADS_A = 8
N_KV_A = 2
N_HEADS_B = 4
MLA_Q_RANK = 192
MLA_KV_RANK = 128
MLA_NOPE = 64
MLA_ROPE = 32
MLA_V = 64
N_HEADS_C = 4
DIFF_QK = 32
DIFF_V = 64
D_FF = 2816
ROPE_BASE = 10000.0
EPS = 1e-6

LANES = 128
HALF = LANES // 2
A_QUARTER = LANES // 4
BF16_SUBLANES = 16
VMEM_LIMIT_BYTES = 56 * 1024 * 1024

LOG2E = math.log2(math.e)
assert HEAD_DIM == MLA_V == DIFF_V == 2 * DIFF_QK == HALF

OFF_QA = 0
OFF_KA = OFF_QA + N_HEADS_A * HEAD_DIM
OFF_VA = OFF_KA + LANES
OFF_CQ = OFF_VA + LANES
CQ_PAD = 2 * LANES
OFF_CKV = OFF_CQ + CQ_PAD
OFF_KR = OFF_CKV + MLA_KV_RANK
OFF_QC = OFF_KR + LANES
OFF_KC = OFF_QC + N_HEADS_C * HALF
OFF_VC = OFF_KC + N_HEADS_C * HALF
IN_COLS = OFF_VC + N_HEADS_C * HALF

NRM_QB = 0
NRM_KB = NRM_QB + N_HEADS_B
NRM_QC = NRM_KB + N_HEADS_B
NRM_KC = NRM_QC + N_HEADS_C
NRM_GROUPS = 2 * N_HEADS_B + N_HEADS_C
NRM_MARGIN = 1.03
SAFE_LOG2_RANGE = 55.0

F32 = jnp.float32
BF16 = jnp.bfloat16

TM_IN = 512
TM_FFN = 512
FFN_CHUNK = 256
FFN_DOWN_GROUP = 4 * FFN_CHUNK
FFN_DOWN_LAG = 2
TQ = 512
TQ_SUB_AB = 512
TQ_SUB_C = 256
TK = 512
SCORE_LOOKAHEAD = 2
HEADS_PER_PIPELINE = 4
HEADS_PER_PIPELINE_AB = 4


def _dot(a, b):
    return jnp.dot(a, b, preferred_element_type=F32)


def _rms(x, n):
    ms = jnp.sum(x * x, axis=-1, keepdims=True) * (1.0 / n)
    return x * lax.rsqrt(ms + EPS)


def _low_lanes():
    return lax.broadcasted_iota(jnp.int32, (1, LANES), 1) < HALF


def _half_rms(x, low=None):
    low = _low_lanes() if low is None else low
    sq = x * x
    s_lo = jnp.sum(jnp.where(low, sq, 0.0), axis=-1, keepdims=True)
    s_hi = jnp.sum(jnp.where(low, 0.0, sq), axis=-1, keepdims=True)
    return x * lax.rsqrt(jnp.where(low, s_lo, s_hi) * (1.0 / HALF) + EPS)


def _in_kernel(x_ref, gat_ref, w_ref, cos_a_ref, sin_a_ref, cos_b_ref, sin_b_ref,
               qg_a_ref, kg_a_ref, qg_b_ref, kvg_b_ref, wuq_ref, wukv_ref, sel_ref,
               qa_ref, kta_ref, va_ref, qb_ref, ktb_ref, vb_ref, qc_ref, ktc_ref, vc_ref, nrm_ref):
    x = x_ref[...]
    tm = x.shape[0]
    h = (_rms(x, D_MODEL) * gat_ref[...]).astype(BF16)

    lane = lax.broadcasted_iota(jnp.int32, (1, LANES), 1)
    low = lane < HALF
    row = lax.broadcasted_iota(jnp.int32, (LANES, 1), 0)

    z = _dot(h, w_ref[...])

    def seg(off, width):
        return z[:, off:off + width]

    def rope(y, c, s):
        return y * c + pltpu.roll(y, HALF, axis=1) * s

    def group(z, g):
        return z[:, g * LANES:(g + 1) * LANES]

    squares = []

    def note(y):
        squares.append((y * y).astype(BF16))

    def noted_row_sums(first):
        return _dot(jnp.concatenate(squares[first:], axis=-1),
                    sel_ref[first * LANES:len(squares) * LANES])

    cos_b, sin_b = cos_b_ref[...], sin_b_ref[...]
    cq =(_rms(seg(OFF_CQ, CQ_PAD), MLA_Q_RANK) * qg_b_ref[...]).astype(BF16)
    qb = _dot(cq, wuq_ref[...])
    scale_b = (MLA_NOPE + MLA_ROPE) ** -0.5 * LOG2E
    for hh in range(N_HEADS_B):
        y = rope(group(qb, hh), cos_b, sin_b) * scale_b
        qb_ref[0, hh] = y.astype(BF16)
        note(y)
    ckv = (_rms(seg(OFF_CKV, MLA_KV_RANK), MLA_KV_RANK) * kvg_b_ref[...]).astype(BF16)
    kvb = _dot(ckv, wukv_ref[...])
    kr = rope(seg(OFF_KR, LANES), cos_b, sin_b)
    for hh in range(N_HEADS_B):
        y = group(kvb, hh) + kr
        ktb_ref[0, hh] = y.T.astype(BF16)
        note(y)
        ones_other_half = jnp.where(low, float(hh % 2), float(1 - hh % 2))
        vb_ref[0, hh] = (group(kvb, N_HEADS_B + hh) + ones_other_half).astype(BF16)
    n2 = noted_row_sums(0)
    n_noted_b = len(squares)

    zq = seg(OFF_QC, N_HEADS_C * HALF)
    zk = seg(OFF_KC, N_HEADS_C * HALF)
    zv = seg(OFF_VC, N_HEADS_C * HALF)
    k_groups = []
    for p in range(N_HEADS_C // 2):
        y = group(zq, p) * (DIFF_QK ** -0.5 * LOG2E)
        qc_ref[0, p] = y.astype(BF16)
        note(y)
        k_groups.append(group(zk, p))
        kt = group(zk, p).T
        for sub in range(4):
            mine = (row >= sub * DIFF_QK) & (row < (sub + 1) * DIFF_QK)
            ktc_ref[0, 4 * p + sub, 0] = jnp.where(mine, kt, 0.0).astype(BF16)
        vz = group(zv, p)
        vc_ref[0, 2 * p] = jnp.where(low, vz, 1.0).astype(BF16)
        vc_ref[0, 2 * p + 1] = jnp.where(low, 1.0, vz).astype(BF16)
    for kz in k_groups:
        note(kz)
    n2 = n2 + noted_row_sums(n_noted_b)
    nrm_ref[0] = jnp.max(n2, axis=0, keepdims=True)

    cos_a, sin_a = cos_a_ref[...], sin_a_ref[...]
    first_head = (lane & A_QUARTER) == 0
    zq = seg(OFF_QA, N_HEADS_A * HEAD_DIM)
    for p in range(N_HEADS_A // 2):
        y = rope(_half_rms(group(zq, p), first_head) * qg_a_ref[...], cos_a, sin_a)
        qa_ref[0, p] = (y * (HEAD_DIM ** -0.5 * LOG2E)).astype(BF16)
    y = rope(_half_rms(seg(OFF_KA, LANES), first_head) * kg_a_ref[...], cos_a, sin_a)
    kt = y.T.astype(BF16)
    quarter = [kt[j * A_QUARTER:(j + 1) * A_QUARTER] for j in range(4)]
    zeros = jnp.zeros((A_QUARTER, tm), BF16)
    for g in range(N_KV_A):
        k1, k2 = quarter[g], quarter[2 + g]
        kta_ref[0, 2 * g] = jnp.concatenate([k1, zeros, k2, zeros], axis=0)
        kta_ref[0, 2 * g + 1] = jnp.concatenate([zeros, k1, zeros, k2], axis=0)
    zv = seg(OFF_VA, LANES)
    va_ref[0, 0] = jnp.where(low, zv, 1.0).astype(BF16)
    va_ref[0, 1] = jnp.where(low, 1.0, zv).astype(BF16)


def _norm_routing():
    sel = np.zeros((NRM_GROUPS, LANES, LANES), np.float32)
    g = 0
    for slot in ([NRM_QB + h for h in range(N_HEADS_B)] + [NRM_KB + h for h in range(N_HEADS_B)]):
        sel[g, :, slot] = 1.0
        g += 1
    for p in range(N_HEADS_C // 2):
        for f in range(2):
            sel[g, f * HALF:(f + 1) * HALF, NRM_QC + 2 * p + f] = 1.0
        g += 1
    for p in range(N_HEADS_C // 2):
        for sub in range(4):
            sel[g, sub * DIFF_QK:(sub + 1) * DIFF_QK, NRM_KC + 4 * p + sub] = 1.0
        g += 1
    assert g == NRM_GROUPS
    return jnp.asarray(sel.reshape(NRM_GROUPS * LANES, LANES), BF16)


def _in_proj(x2d, gat, w_in_r, tabs, qg_a, kg_a, qg_b, kvg_b, wuq_p, wukv_p, B, S):
    T = B * S
    tm = TM_IN
    nst = S // tm
    const = lambda shape: pl.BlockSpec(shape, lambda i: (0,) * len(shape))
    tab = pl.BlockSpec((tm, LANES), lambda i: (i % nst, 0))
    hm = lambda H: pl.BlockSpec((1, H, tm, LANES), lambda i: (i // nst, 0, i % nst, 0))
    hmt = lambda H: pl.BlockSpec((1, H, LANES, tm), lambda i: (i // nst, 0, 0, i % nst))
    sds = lambda H: jax.ShapeDtypeStruct((B, H, S, LANES), BF16)
    sdt = lambda H: jax.ShapeDtypeStruct((B, H, LANES, S), BF16)
    return pl.pallas_call(
        _in_kernel,
        grid=(T // tm,),
        in_specs=[pl.BlockSpec((tm, D_MODEL), lambda i: (i, 0)),
                  const((1, D_MODEL)), const((D_MODEL, IN_COLS)),
                  tab, tab, tab, tab,
                  const((1, LANES)), const((1, LANES)), const((1, CQ_PAD)), const((1, MLA_KV_RANK)),
                  const((CQ_PAD, N_HEADS_B * LANES)), const((MLA_KV_RANK, 2 * N_HEADS_B * LANES)),
                  const((NRM_GROUPS * LANES, LANES))],
        out_specs=[hm(N_HEADS_A // 2), hmt(2 * N_KV_A), hm(N_KV_A),
                   hm(N_HEADS_B), hmt(N_HEADS_B), hm(N_HEADS_B),
                   hm(N_HEADS_C // 2),
                   pl.BlockSpec((1, 2 * N_HEADS_C, 1, LANES, tm),
                                lambda i: (i // nst, 0, i % nst, 0, 0)),
                   hm(N_HEADS_C),
                   pl.BlockSpec((1, 1, LANES), lambda i: (i, 0, 0))],
        out_shape=[sds(N_HEADS_A // 2), sdt(2 * N_KV_A), sds(N_KV_A),
                   sds(N_HEADS_B), sdt(N_HEADS_B), sds(N_HEADS_B),
                   sds(N_HEADS_C // 2),
                   jax.ShapeDtypeStruct((B, 2 * N_HEADS_C, S // tm, LANES, tm), BF16),
                   sds(N_HEADS_C),
                   jax.ShapeDtypeStruct((T // tm, 1, LANES), F32)],
        compiler_params=pltpu.CompilerParams(dimension_semantics=("arbitrary",),
                                             vmem_limit_bytes=VMEM_LIMIT_BYTES),
        name="in_proj",
    )(x2d, gat, w_in_r, *tabs, qg_a, kg_a, qg_b, kvg_b, wuq_p, wukv_p, _norm_routing())


def _lane_group_max(s):
    m = s[:, :LANES]
    for j in range(1, s.shape[1] // LANES):
        m = jnp.maximum(m, s[:, j * LANES:(j + 1) * LANES])
    return m


def _pv_bounded(streams, n_chunks):
    items = [(r, i) for r in range(n_chunks) for i in range(len(streams))]
    accs = [None] * len(streams)
    in_flight = {}
    for k in range(len(items) + SCORE_LOOKAHEAD):
        if k < len(items):
            r, i = items[k]
            scores, _, row_off = streams[i]
            s, off = scores(r), row_off(r)
            in_flight[k] = s if off is None else s + off
        if k >= SCORE_LOOKAHEAD:
            r, i = items[k - SCORE_LOOKAHEAD]
            d = _dot(jnp.exp2(in_flight.pop(k - SCORE_LOOKAHEAD)).astype(BF16), streams[i][1](r))
            accs[i] = d if accs[i] is None else accs[i] + d
    return accs


def _pv_exact(streams, n_chunks, s_ref):
    rows = s_ref.shape[0] // len(streams)
    maxes = []
    for i, (scores, _, row_off) in enumerate(streams):
        m_run = None
        for r in range(n_chunks):
            s = scores(r)
            s_ref[i * rows:(i + 1) * rows, r * TK:(r + 1) * TK] = s
            m, off = _lane_group_max(s), row_off(r)
            if off is not None:
                m = m + off
            m_run = m if m_run is None else jnp.maximum(m_run, m)
        maxes.append(jnp.max(m_run, axis=-1, keepdims=True))
    accs = []
    for i, (_, values, row_off) in enumerate(streams):
        acc = None
        for r in range(n_chunks):
            off = row_off(r)
            shift = maxes[i] if off is None else maxes[i] - off
            p = jnp.exp2(s_ref[i * rows:(i + 1) * rows, r * TK:(r + 1) * TK] - shift)
            d = _dot(p.astype(BF16), values(r))
            acc = d if acc is None else acc + d
        accs.append(acc)
    return accs


def _run_heads(bounded, heads, head_streams, n_chunks, s_ref, acc_ref, roll_exact=False):
    @pl.when(bounded)
    def _():
        shared = {}
        per_head = [head_streams(h, shared) for h in heads]
        n_sub = len(per_head[0])
        accs = _pv_bounded([s for streams in per_head for s in streams], n_chunks)
        for k, h in enumerate(heads):
            acc_ref[h] = jnp.concatenate(accs[k * n_sub:(k + 1) * n_sub], axis=0)

    @pl.when(jnp.logical_not(bounded))
    def _():
        def one_head(h):
            acc_ref[h] = jnp.concatenate(_pv_exact(head_streams(h, {}), n_chunks, s_ref), axis=0)

        if roll_exact:
            lax.fori_loop(0, len(heads), lambda k, c: (one_head(heads[0] + k), c)[1], 0)
        else:
            for h in heads:
                one_head(h)


def _normalised_pair(acc_low, acc_high):
    low = _low_lanes()
    num = jnp.where(low, acc_low, acc_high)
    den = pltpu.roll(jnp.where(low, acc_high, acc_low), HALF, axis=1)
    return num / den


def _attn_ab_kernel(bounded_ref, q_ref, kt_ref, v_ref, o_ref, s_ref, acc_ref, *,
                    n_heads, per_pipeline, q_index, kt_index, v_index, out_pairs, n_keys):
    n_chunks = n_keys // TK

    def head_streams(h, shared):
        q = q_ref[0, q_index(h)]
        return [(lambda r, qs=q[r0:r0 + TQ_SUB_AB]:
                 _dot(qs, kt_ref[0, kt_index(h), :, r * TK:(r + 1) * TK]),
                 lambda r: v_ref[0, v_index(h), r * TK:(r + 1) * TK, :],
                 lambda r: None) for r0 in range(0, TQ, TQ_SUB_AB)]

    def head_group(j, carry):
        heads = [per_pipeline * j + k for k in range(per_pipeline)]
        _run_heads(bounded_ref[pl.program_id(0), j] != 0, heads, head_streams, n_chunks, s_ref,
                   acc_ref, roll_exact=True)
        return carry

    if n_heads == per_pipeline:
        head_group(0, None)
    else:
        lax.fori_loop(0, n_heads // per_pipeline, head_group, 0)
    o_ref[...] = jnp.concatenate([_normalised_pair(acc_ref[a], acc_ref[b]) for a, b in out_pairs],
                                 axis=-1).astype(BF16)


def _attn_c_kernel(bounded_ref, q_ref, kt_ref, v_ref, lq1_ref, lk1_ref, lq2_ref, lk2_ref,
                   subln_ref, o_ref, s_ref, acc_ref, *, n_keys, lam_init):
    assert TQ == TK
    nc = n_keys // TK
    cd = pl.program_id(1)
    col = lax.broadcasted_iota(jnp.int32, (1, TK), 1).astype(F32)
    row = lax.broadcasted_iota(jnp.int32, (TQ_SUB_C, 1), 0).astype(F32)
    subs = [(r, r + TQ_SUB_C) for r in range(0, TQ, TQ_SUB_C)]
    neg_dist = [-jnp.abs((lax.broadcasted_iota(jnp.int32, (TQ_SUB_C, TK), 0) + r0
                          - lax.broadcasted_iota(jnp.int32, (TQ_SUB_C, TK), 1)).astype(F32))
                for r0, _ in subs]

    chunk = [(cd + r) % nc for r in range(nc)]

    def head_bias(hh):
        slope = jnp.float32(LOG2E * 2.0 ** (-8.0 * N_HEADS_C / N_HEADS_C))
        for k in range(N_HEADS_C - 2, -1, -1):
            slope = jnp.where(hh == k, LOG2E * 2.0 ** (-8.0 * (k + 1) / N_HEADS_C), slope)
        key_term, row_off = [None], [[None] * len(subs)]
        for r in range(1, nc):
            side = jnp.where(chunk[r] < cd, slope, -slope)
            key_term.append(col * side)
            base = -slope * (jnp.abs(chunk[r] - cd) * TK).astype(F32)
            row_off.append([base - side * (row + float(r0)) for r0, _ in subs])
        return slope, key_term, row_off

    def head_pair(p, carry):
        def streams_of(k, shared):
            idx, hh = 4 * p + k, 2 * p + k // 2
            if isinstance(k, int):
                if k // 2 not in shared:
                    shared[k // 2] = head_bias(hh)
                slope, key_term, row_off = shared[k // 2]
            else:
                slope, key_term, row_off = head_bias(hh)

            def scores(r, si):
                s = _dot(q_ref[0, p][subs[si][0]:subs[si][1]], kt_ref[0, idx, chunk[r]])
                return s + (neg_dist[si] * slope if r == 0 else key_term[r])

            return [(functools.partial(scores, si=si),
                     lambda r: v_ref[0, hh, pl.ds(pl.multiple_of(chunk[r] * TK, TK), TK), :],
                     lambda r, si=si: row_off[r][si]) for si in range(len(subs))]

        _run_heads(bounded_ref[pl.program_id(0), p] != 0, list(range(4)), streams_of, nc, s_ref,
                   acc_ref.at[pl.ds(4 * p, 4)], roll_exact=True)
        return carry

    assert HEADS_PER_PIPELINE == 4
    lax.fori_loop(0, N_HEADS_C // 2, head_pair, 0)

    lam = (jnp.exp(jnp.sum(lq1_ref[...] * lk1_ref[...], axis=-1, keepdims=True))
           - jnp.exp(jnp.sum(lq2_ref[...] * lk2_ref[...], axis=-1, keepdims=True)) + lam_init)
    outs = []
    for p in range(N_HEADS_C // 2):
        o = (_normalised_pair(acc_ref[4 * p], acc_ref[4 * p + 2])
             - lam * _normalised_pair(acc_ref[4 * p + 1], acc_ref[4 * p + 3]))
        outs.append(_half_rms(o) * subln_ref[...] * (1.0 - lam_init))
    o_ref[...] = jnp.concatenate(outs, axis=-1).astype(BF16)


def _attention(kernel_fn, bounded, n_acc, q, kt, v, extras, out_width, B, S):
    nq = S // TQ
    hq, hv = q.shape[1], v.shape[1]
    extra_specs = [pl.BlockSpec(e.shape, lambda b, i, f, n=e.ndim: (0,) * n) for e in extras]
    return pl.pallas_call(
        kernel_fn,
        grid_spec=pltpu.PrefetchScalarGridSpec(
            num_scalar_prefetch=1,
            grid=(B, nq),
            in_specs=[pl.BlockSpec((1, hq, TQ, LANES), lambda b, i, f: (b, 0, i, 0)),
                      pl.BlockSpec((1,) + kt.shape[1:],
                                   lambda b, i, f, n=kt.ndim: (b,) + (0,) * (n - 1)),
                      pl.BlockSpec((1, hv, S, LANES), lambda b, i, f: (b, 0, 0, 0))] + extra_specs,
            out_specs=pl.BlockSpec((TQ, out_width), lambda b, i, f: (b * nq + i, 0)),
            scratch_shapes=[pltpu.VMEM((TQ, S), F32),
                            pltpu.VMEM((n_acc, TQ, LANES), F32)]),
        out_shape=jax.ShapeDtypeStruct((B * S, out_width), BF16),
        compiler_params=pltpu.CompilerParams(dimension_semantics=("arbitrary", "arbitrary"),
                                             vmem_limit_bytes=VMEM_LIMIT_BYTES),
        name=kernel_fn.func.__name__.strip("_"),
    )(bounded, q, kt, v, *extras)


def _mlp_kernel(xp_ref, x_ref, xn_ref, ap_ref, a_ref, an_ref, bp_ref, b_ref, bn_ref,
                cp_ref, c_ref, cn_ref, wo_ref, g_ref, wu_ref, cw_ref, cb_ref, wd_ref, fg_ref,
                o_ref, act_ref, *, tiles_per_seq, final_norm):
    i = pl.program_id(0)
    tm = x_ref.shape[0]
    halo = xp_ref.shape[0]

    def rows(p, m, n):
        return jnp.concatenate([p[...], m[...], n[...]], axis=0)

    o_ext = jnp.concatenate([rows(ap_ref, a_ref, an_ref), rows(bp_ref, b_ref, bn_ref),
                             rows(cp_ref, c_ref, cn_ref)], axis=1)
    x1 = rows(xp_ref, x_ref, xn_ref) + _dot(o_ext, wo_ref[...])
    h = (_rms(x1, D_MODEL) * g_ref[...]).astype(BF16)
    at_start = (i % tiles_per_seq) == 0
    at_end = (i % tiles_per_seq) == tiles_per_seq - 1
    zeros = jnp.zeros((halo, D_MODEL), BF16)
    hext = jnp.concatenate([jnp.where(at_start, zeros, h[:halo]), h[halo:halo + tm],
                            jnp.where(at_end, zeros, h[halo + tm:])], axis=0)

    def branch(k, c0):
        u = _dot(hext, wu_ref[:, k * D_FF + c0:k * D_FF + c0 + FFN_CHUNK])
        cw = cw_ref[k, :, c0:c0 + FFN_CHUNK]
        n = u.shape[0]
        prev = pltpu.roll(u, 1, axis=0)[halo:halo + tm]
        nxt = pltpu.roll(u, n - 1, axis=0)[halo:halo + tm]
        return (prev * cw[0:1] + u[halo:halo + tm] * cw[1:2] + nxt * cw[2:3]
                + cb_ref[k, :, c0:c0 + FFN_CHUNK])

    y = x1[halo:halo + tm]
    done = 0
    for c0 in range(0, D_FF, FFN_CHUNK):
        g = branch(0, c0)
        val = branch(1, c0)
        half = 0.5 * g
        act_ref[:, c0:c0 + FFN_CHUNK] = ((half + half * jnp.tanh(half)) * val).astype(BF16)
        ready = c0 + FFN_CHUNK - FFN_DOWN_LAG * FFN_CHUNK
        if ready - done >= FFN_DOWN_GROUP:
            y = y + _dot(act_ref[:, done:done + FFN_DOWN_GROUP], wd_ref[done:done + FFN_DOWN_GROUP])
            done += FFN_DOWN_GROUP
    y = y + _dot(act_ref[:, done:], wd_ref[done:])
    if final_norm:
        y = _rms(y, D_MODEL) * fg_ref[...]
    o_ref[...] = y


def _mlp(x2d, oa, ob, oc, w_out, g_ffn, w_up, conv_w, conv_b, w_down, final_g, S, final_norm):
    T = x2d.shape[0]
    tm, halo = TM_FFN, BF16_SUBLANES
    per = tm // halo
    last = T // halo - 1
    resident = lambda shape: pl.BlockSpec(shape, lambda i: (0,) * len(shape),
                                          pipeline_mode=pl.Buffered(1))

    def with_halos(width):
        return [pl.BlockSpec((halo, width), lambda i: (jnp.maximum(i * per - 1, 0), 0)),
                pl.BlockSpec((tm, width), lambda i: (i, 0)),
                pl.BlockSpec((halo, width), lambda i: (jnp.minimum((i + 1) * per, last), 0))]

    return pl.pallas_call(
        functools.partial(_mlp_kernel, tiles_per_seq=S // tm, final_norm=final_norm),
        grid=(T // tm,),
        in_specs=(with_halos(D_MODEL) + with_halos(oa.shape[1]) + with_halos(ob.shape[1])
                  + with_halos(oc.shape[1])
                  + [resident(w_out.shape), resident((1, D_MODEL)),
                     resident((D_MODEL, 2 * D_FF)), resident((2, 3, D_FF)), resident((2, 1, D_FF)),
                     resident((D_FF, D_MODEL)), resident((1, D_MODEL))]),
        out_specs=pl.BlockSpec((tm, D_MODEL), lambda i: (i, 0)),
        out_shape=jax.ShapeDtypeStruct((T, D_MODEL), F32),
        scratch_shapes=[pltpu.VMEM((tm, D_FF), BF16)],
        compiler_params=pltpu.CompilerParams(dimension_semantics=("arbitrary",),
                                             vmem_limit_bytes=VMEM_LIMIT_BYTES),
        name="out_proj_conv_mlp",
    )(x2d, x2d, x2d, oa, oa, oa, ob, ob, ob, oc, oc, oc, w_out, g_ffn,
      w_up, conv_w, conv_b, w_down, final_g)


def _pad_groups(w, n_groups, width, offset=0):
    r = w.shape[0]
    w = w.reshape(r, n_groups, width)
    w = jnp.pad(w, ((0, 0), (0, 0), (offset, LANES - width - offset)))
    return w.reshape(r, n_groups * LANES)


def _group_a_lane_order():
    nf = HEAD_DIM // 4
    lanes = np.arange(LANES)
    second, r = lanes // HALF, lanes % HALF
    head, rr = r // A_QUARTER, r % A_QUARTER
    block, freq = rr // nf, rr % nf
    dim = block * 2 * nf + second * nf + freq
    return head, dim, block, freq, second


def _group_b_lane_order():
    nf = MLA_ROPE // 4
    src = np.full(LANES, -1)
    block = np.zeros(LANES, np.int64)
    freq = np.zeros(LANES, np.int64)
    second = np.zeros(LANES, np.int64)
    is_rope = np.zeros(LANES, bool)
    for e in range(2):
        for j in range(2 * nf):
            lane = e * HALF + j
            is_rope[lane], block[lane], freq[lane], second[lane] = True, j // nf, j % nf, e
            src[lane] = MLA_NOPE + (j // nf) * 2 * nf + e * nf + j % nf
    n_low = HALF - 2 * nf
    src[2 * nf:HALF] = np.arange(n_low)
    src[HALF + 2 * nf:HALF + 2 * nf + MLA_NOPE - n_low] = np.arange(n_low, MLA_NOPE)
    return src, is_rope, block, freq, second


def _gather_cols(w, src):
    return jnp.where(jnp.asarray(src >= 0), w[:, np.maximum(src, 0)], 0.0)


def _rope_tables(S):
    t = jnp.arange(S, dtype=jnp.int32)
    pos = jnp.stack([(t // GRID_W).astype(F32), (t % GRID_W).astype(F32)], axis=0)

    def table(nf, block, freq, second, is_rope):
        inv = ROPE_BASE ** (-jnp.arange(nf, dtype=F32) / nf)
        ang = pos[block].T * inv[freq]
        sign = jnp.asarray(2.0 * second - 1.0, F32)
        cos = jnp.where(jnp.asarray(is_rope), jnp.cos(ang), 1.0)
        sin = jnp.where(jnp.asarray(is_rope), jnp.sin(ang) * sign, 0.0)
        return cos, sin

    _, _, block, freq, second = _group_a_lane_order()
    cos_a, sin_a = table(HEAD_DIM // 4, block, freq, second, np.ones(LANES, bool))
    _, is_rope, block, freq, second = _group_b_lane_order()
    cos_b, sin_b = table(MLA_ROPE // 4, block, freq, second, is_rope)
    return cos_a, sin_a, cos_b, sin_b


def _prep_w_in(w):
    n_qa, n_ka = N_HEADS_A * HEAD_DIM, N_KV_A * HEAD_DIM
    a_end = n_qa + 2 * n_ka
    b_end = a_end + MLA_Q_RANK + MLA_KV_RANK + MLA_ROPE
    head, dim, _, _, _ = _group_a_lane_order()
    pair_cols = head * HEAD_DIM + dim
    qa_cols = np.concatenate([p * LANES + pair_cols for p in range(N_HEADS_A // 2)])
    cq = w[:, a_end:a_end + MLA_Q_RANK]
    ckv = w[:, a_end + MLA_Q_RANK:a_end + MLA_Q_RANK + MLA_KV_RANK]
    kr = w[:, b_end - MLA_ROPE:b_end]
    src_b, is_rope, _, _, _ = _group_b_lane_order()
    out = jnp.concatenate([
        w[:, qa_cols], w[:, n_qa + pair_cols], w[:, n_qa + n_ka:a_end],
        jnp.pad(cq, ((0, 0), (0, CQ_PAD - MLA_Q_RANK))),
        ckv,
        _gather_cols(kr, np.where(is_rope, src_b - MLA_NOPE, -1)),
        w[:, b_end:],
    ], axis=-1).astype(BF16)
    assert out.shape[1] == IN_COLS
    return out


def _twice(v):
    return jnp.tile(v, 2).reshape(1, 2 * v.shape[0])


def kernel(x, norm_attn, w_in, q_norm_a, k_norm_a, q_a_norm_b, w_uq_b, kv_a_norm_b, w_ukv_b,
           lambda_q1_c, lambda_k1_c, lambda_q2_c, lambda_k2_c, subln_c, w_out,
           norm_ffn, w_up, conv_w, conv_b, w_down, final_norm):
    B, S, D = x.shape
    depth = w_in.shape[0]
    assert D == D_MODEL and S % max(TM_IN, TM_FFN, TQ, TK) == 0 and S % GRID_W == 0
    assert TM_IN == TK
    T = B * S

    tabs = _rope_tables(S)
    row = lambda v: v.reshape(1, -1)
    a_pairs = [(j, j + N_HEADS_A // 2) for j in range(N_HEADS_A // 2)]
    a_rows = np.concatenate([np.arange(h * HEAD_DIM, (h + 1) * HEAD_DIM)
                             for pair in a_pairs for h in pair])
    xc = x.reshape(T, D)
    for l in range(depth):
        lam_init = 0.8 - 0.6 * math.exp(-0.3 * l)
        src_b, is_rope, _, _, _ = _group_b_lane_order()
        d_qk = MLA_NOPE + MLA_ROPE
        wuq_p = jnp.pad(
            jnp.concatenate([_gather_cols(w_uq_b[l][:, h * d_qk:(h + 1) * d_qk], src_b)
                             for h in range(N_HEADS_B)], axis=-1),
            ((0, CQ_PAD - MLA_Q_RANK), (0, 0))).astype(BF16)
        wukv = w_ukv_b[l].reshape(MLA_KV_RANK, N_HEADS_B, MLA_NOPE + MLA_V)
        k_cols = [_gather_cols(wukv[:, h, :MLA_NOPE], np.where(is_rope, -1, src_b))
                  for h in range(N_HEADS_B)]
        v_cols = [_pad_groups(wukv[:, h, MLA_NOPE:], 1, MLA_V, offset=HALF * (h % 2))
                  for h in range(N_HEADS_B)]
        wukv_p = jnp.concatenate(k_cols + v_cols, axis=-1).astype(BF16)

        qa, kta, va, qb, ktb, vb, qc, ktc, vc, nrm = _in_proj(
            xc, row(norm_attn[l]), _prep_w_in(w_in[l]), tabs,
            q_norm_a[l][_group_a_lane_order()[1]].reshape(1, LANES),
            k_norm_a[l][_group_a_lane_order()[1]].reshape(1, LANES),
            jnp.pad(q_a_norm_b[l], (0, CQ_PAD - MLA_Q_RANK)).reshape(1, CQ_PAD),
            row(kv_a_norm_b[l]), wuq_p, wukv_p, B, S)

        nrm = jnp.max(nrm.reshape(B, -1, LANES), axis=1)

        def bounded(q0, nq, q_rep, k0, nk, k_rep, per_pipeline):
            q2 = jnp.repeat(nrm[:, q0:q0 + nq], q_rep, axis=1)
            k2 = jnp.repeat(nrm[:, k0:k0 + nk], k_rep, axis=1)
            ok = q2 * k2 * NRM_MARGIN <= SAFE_LOG2_RANGE ** 2
            return jnp.all(ok.reshape(B, -1, per_pipeline), axis=-1).astype(jnp.int32)

        qk_a = (HEAD_DIM * jnp.max(jnp.abs(q_norm_a[l])) * jnp.max(jnp.abs(k_norm_a[l]))
                * (HEAD_DIM ** -0.5 * LOG2E))
        bounded_a = jnp.broadcast_to(
            (qk_a * NRM_MARGIN <= SAFE_LOG2_RANGE).astype(jnp.int32),
            (B, N_HEADS_A // HEADS_PER_PIPELINE_AB))

        group = N_HEADS_A // N_KV_A
        oa = _attention(
            functools.partial(_attn_ab_kernel, n_heads=N_HEADS_A,
                              per_pipeline=HEADS_PER_PIPELINE_AB, q_index=lambda h: h // 2,
                              kt_index=lambda h: 2 * (h // group) + h % 2,
                              v_index=lambda h: h // group, out_pairs=a_pairs, n_keys=S),
            bounded_a, N_HEADS_A, qa, kta, va, [], N_HEADS_A * HEAD_DIM, B, S)
        same = lambda h: h
        ob = _attention(
            functools.partial(_attn_ab_kernel, n_heads=N_HEADS_B,
                              per_pipeline=HEADS_PER_PIPELINE_AB, q_index=same, kt_index=same,
                              v_index=same,
                              out_pairs=[(2 * j, 2 * j + 1) for j in range(N_HEADS_B // 2)],
                              n_keys=S),
            bounded(NRM_QB, N_HEADS_B, 1, NRM_KB, N_HEADS_B, 1, HEADS_PER_PIPELINE_AB), N_HEADS_B,
            qb, ktb, vb, [], N_HEADS_B * MLA_V, B, S)
        oc = _attention(
            functools.partial(_attn_c_kernel, n_keys=S, lam_init=lam_init),
            bounded(NRM_QC, N_HEADS_C, 2, NRM_KC, 2 * N_HEADS_C, 1, HEADS_PER_PIPELINE),
            2 * N_HEADS_C, qc, ktc, vc,
            [row(lambda_q1_c[l]), row(lambda_k1_c[l]), row(lambda_q2_c[l]), row(lambda_k2_c[l]),
             _twice(subln_c[l])],
            N_HEADS_C * DIFF_V, B, S)

        w_out_l = jnp.concatenate([w_out[l][a_rows], w_out[l][N_HEADS_A * HEAD_DIM:]], axis=0)
        xc = _mlp(xc, oa, ob, oc, w_out_l.astype(BF16), row(norm_ffn[l]), w_up[l].astype(BF16),
                  conv_w[l].reshape(3, 2, D_FF).transpose(1, 0, 2),
                  conv_b[l].reshape(2, 1, D_FF), w_down[l].astype(BF16),
                  row(final_norm), S, final_norm=(l == depth - 1))
    return xc.reshape(B, S, D)
```

```python
import functools
import math

import jax
import jax.numpy as jnp
import numpy as np
from jax import lax
from jax.experimental import pallas as pl
from jax.experimental.pallas import tpu as pltpu

D_MODEL = 1024
GRID_W = 64
HEAD_DIM = 64
N_HEADS_A = 8
N_KV_A = 2
N_HEADS_B = 4
MLA_Q_RANK = 192
MLA_KV_RANK = 128
MLA_NOPE = 64
MLA_ROPE = 32
MLA_V = 64
N_HEADS_C = 4
DIFF_QK = 32
DIFF_V = 64
D_FF = 2816
ROPE_BASE = 10000.0
EPS = 1e-6

LANES = 128
HALF = LANES // 2
A_QUARTER = LANES // 4
BF16_SUBLANES = 16
VMEM_LIMIT_BYTES = 56 * 1024 * 1024

LOG2E = math.log2(math.e)
assert HEAD_DIM == MLA_V == DIFF_V == 2 * DIFF_QK == HALF

OFF_QA = 0
OFF_KA = OFF_QA + N_HEADS_A * HEAD_DIM
OFF_VA = OFF_KA + LANES
OFF_CQ = OFF_VA + LANES
CQ_PAD = 2 * LANES
OFF_CKV = OFF_CQ + CQ_PAD
OFF_KR = OFF_CKV + MLA_KV_RANK
OFF_QC = OFF_KR + LANES
OFF_KC = OFF_QC + N_HEADS_C * HALF
OFF_VC = OFF_KC + N_HEADS_C * HALF
IN_COLS = OFF_VC + N_HEADS_C * HALF

NRM_QB = 0
NRM_KB = NRM_QB + N_HEADS_B
NRM_QC = NRM_KB + N_HEADS_B
NRM_KC = NRM_QC + N_HEADS_C
NRM_GROUPS = 2 * N_HEADS_B + N_HEADS_C
NRM_MARGIN = 1.03
SAFE_LOG2_RANGE = 55.0

F32 = jnp.float32
BF16 = jnp.bfloat16

TM_IN = 512
TM_FFN = 512
FFN_CHUNK = 256
FFN_DOWN_GROUP = 6 * FFN_CHUNK
FFN_DOWN_LAG = 2
TQ = 512
TQ_SUB_AB = 512
TQ_SUB_C = 256
TK = 512
SCORE_LOOKAHEAD = 1
HEADS_PER_PIPELINE = 4
HEADS_PER_PIPELINE_AB = 4


def _dot(a, b):
    return jnp.dot(a, b, preferred_element_type=F32)


def _rms(x, n):
    ms = jnp.sum(x * x, axis=-1, keepdims=True) * (1.0 / n)
    return x * lax.rsqrt(ms + EPS)


def _low_lanes():
    return lax.broadcasted_iota(jnp.int32, (1, LANES), 1) < HALF


def _half_rms(x, low=None):
    low = _low_lanes() if low is None else low
    sq = x * x
    s_lo = jnp.sum(jnp.where(low, sq, 0.0), axis=-1, keepdims=True)
    s_hi = jnp.sum(jnp.where(low, 0.0, sq), axis=-1, keepdims=True)
    return x * lax.rsqrt(jnp.where(low, s_lo, s_hi) * (1.0 / HALF) + EPS)


def _in_kernel(x_ref, gat_ref, w_ref, cos_a_ref, sin_a_ref, cos_b_ref, sin_b_ref,
               qg_a_ref, kg_a_ref, qg_b_ref, kvg_b_ref, wuq_ref, wukv_ref, sel_ref,
               qa_ref, kta_ref, va_ref, qb_ref, ktb_ref, vb_ref, qc_ref, ktc_ref, vc_ref, nrm_ref):
    x = x_ref[...]
    tm = x.shape[0]
    h = (_rms(x, D_MODEL) * gat_ref[...]).astype(BF16)

    lane = lax.broadcasted_iota(jnp.int32, (1, LANES), 1)
    low = lane < HALF
    row = lax.broadcasted_iota(jnp.int32, (LANES, 1), 0)

    z = _dot(h, w_ref[...])

    def seg(off, width):
        return z[:, off:off + width]

    def rope(y, c, s):
        return y * c + pltpu.roll(y, HALF, axis=1) * s

    def group(z, g):
        return z[:, g * LANES:(g + 1) * LANES]

    squares = []

    def note(y):
        squares.append((y * y).astype(BF16))

    def noted_row_sums(first):
        return _dot(jnp.concatenate(squares[first:], axis=-1),
                    sel_ref[first * LANES:len(squares) * LANES])

    cos_b, sin_b = cos_b_ref[...], sin_b_ref[...]
    cq =(_rms(seg(OFF_CQ, CQ_PAD), MLA_Q_RANK) * qg_b_ref[...]).astype(BF16)
    qb = _dot(cq, wuq_ref[...])
    scale_b = (MLA_NOPE + MLA_ROPE) ** -0.5 * LOG2E
    for hh in range(N_HEADS_B):
        y = rope(group(qb, hh), cos_b, sin_b) * scale_b
        qb_ref[0, hh] = y.astype(BF16)
        note(y)
    ckv = (_rms(seg(OFF_CKV, MLA_KV_RANK), MLA_KV_RANK) * kvg_b_ref[...]).astype(BF16)
    kvb = _dot(ckv, wukv_ref[...])
    kr = rope(seg(OFF_KR, LANES), cos_b, sin_b)
    for hh in range(N_HEADS_B):
        y = group(kvb, hh) + kr
        ktb_ref[0, hh] = y.T.astype(BF16)
        note(y)
        ones_other_half = jnp.where(low, float(hh % 2), float(1 - hh % 2))
        vb_ref[0, hh] = (group(kvb, N_HEADS_B + hh) + ones_other_half).astype(BF16)
    n2 = noted_row_sums(0)
    n_noted_b = len(squares)

    zq = seg(OFF_QC, N_HEADS_C * HALF)
    zk = seg(OFF_KC, N_HEADS_C * HALF)
    zv = seg(OFF_VC, N_HEADS_C * HALF)
    k_groups = []
    for p in range(N_HEADS_C // 2):
        y = group(zq, p) * (DIFF_QK ** -0.5 * LOG2E)
        qc_ref[0, p] = y.astype(BF16)
        note(y)
        k_groups.append(group(zk, p))
        kt = group(zk, p).T
        for sub in range(4):
            mine = (row >= sub * DIFF_QK) & (row < (sub + 1) * DIFF_QK)
            ktc_ref[0, 4 * p + sub, 0] = jnp.where(mine, kt, 0.0).astype(BF16)
        vz = group(zv, p)
        vc_ref[0, 2 * p] = jnp.where(low, vz, 1.0).astype(BF16)
        vc_ref[0, 2 * p + 1] = jnp.where(low, 1.0, vz).astype(BF16)
    for kz in k_groups:
        note(kz)
    n2 = n2 + noted_row_sums(n_noted_b)
    nrm_ref[0] = jnp.max(n2, axis=0, keepdims=True)

    cos_a, sin_a = cos_a_ref[...], sin_a_ref[...]
    first_head = (lane & A_QUARTER) == 0
    zq = seg(OFF_QA, N_HEADS_A * HEAD_DIM)
    for p in range(N_HEADS_A // 2):
        y = rope(_half_rms(group(zq, p), first_head) * qg_a_ref[...], cos_a, sin_a)
        qa_ref[0, p] = (y * (HEAD_DIM ** -0.5 * LOG2E)).astype(BF16)
    y = rope(_half_rms(seg(OFF_KA, LANES), first_head) * kg_a_ref[...], cos_a, sin_a)
    kt = y.T.astype(BF16)
    quarter = [kt[j * A_QUARTER:(j + 1) * A_QUARTER] for j in range(4)]
    zeros = jnp.zeros((A_QUARTER, tm), BF16)
    for g in range(N_KV_A):
        k1, k2 = quarter[g], quarter[2 + g]
        kta_ref[0, 2 * g] = jnp.concatenate([k1, zeros, k2, zeros], axis=0)
        kta_ref[0, 2 * g + 1] = jnp.concatenate([zeros, k1, zeros, k2], axis=0)
    zv = seg(OFF_VA, LANES)
    va_ref[0, 0] = jnp.where(low, zv, 1.0).astype(BF16)
    va_ref[0, 1] = jnp.where(low, 1.0, zv).astype(BF16)


def _norm_routing():
    sel = np.zeros((NRM_GROUPS, LANES, LANES), np.float32)
    g = 0
    for slot in ([NRM_QB + h for h in range(N_HEADS_B)] + [NRM_KB + h for h in range(N_HEADS_B)]):
        sel[g, :, slot] = 1.0
        g += 1
    for p in range(N_HEADS_C // 2):
        for f in range(2):
            sel[g, f * HALF:(f + 1) * HALF, NRM_QC + 2 * p + f] = 1.0
        g += 1
    for p in range(N_HEADS_C // 2):
        for sub in range(4):
            sel[g, sub * DIFF_QK:(sub + 1) * DIFF_QK, NRM_KC + 4 * p + sub] = 1.0
        g += 1
    assert g == NRM_GROUPS
    return jnp.asarray(sel.reshape(NRM_GROUPS * LANES, LANES), BF16)


def _in_proj(x2d, gat, w_in_r, tabs, qg_a, kg_a, qg_b, kvg_b, wuq_p, wukv_p, B, S):
    T = B * S
    tm = TM_IN
    nst = S // tm
    const = lambda shape: pl.BlockSpec(shape, lambda i: (0,) * len(shape))
    tab = pl.BlockSpec((tm, LANES), lambda i: (i % nst, 0))
    hm = lambda H: pl.BlockSpec((1, H, tm, LANES), lambda i: (i // nst, 0, i % nst, 0))
    hmt = lambda H: pl.BlockSpec((1, H, LANES, tm), lambda i: (i // nst, 0, 0, i % nst))
    sds = lambda H: jax.ShapeDtypeStruct((B, H, S, LANES), BF16)
    sdt = lambda H: jax.ShapeDtypeStruct((B, H, LANES, S), BF16)
    return pl.pallas_call(
        _in_kernel,
        grid=(T // tm,),
        in_specs=[pl.BlockSpec((tm, D_MODEL), lambda i: (i, 0)),
                  const((1, D_MODEL)), const((D_MODEL, IN_COLS)),
                  tab, tab, tab, tab,
                  const((1, LANES)), const((1, LANES)), const((1, CQ_PAD)), const((1, MLA_KV_RANK)),
                  const((CQ_PAD, N_HEADS_B * LANES)), const((MLA_KV_RANK, 2 * N_HEADS_B * LANES)),
                  const((NRM_GROUPS * LANES, LANES))],
        out_specs=[hm(N_HEADS_A // 2), hmt(2 * N_KV_A), hm(N_KV_A),
                   hm(N_HEADS_B), hmt(N_HEADS_B), hm(N_HEADS_B),
                   hm(N_HEADS_C // 2),
                   pl.BlockSpec((1, 2 * N_HEADS_C, 1, LANES, tm),
                                lambda i: (i // nst, 0, i % nst, 0, 0)),
                   hm(N_HEADS_C),
                   pl.BlockSpec((1, 1, LANES), lambda i: (i, 0, 0))],
        out_shape=[sds(N_HEADS_A // 2), sdt(2 * N_KV_A), sds(N_KV_A),
                   sds(N_HEADS_B), sdt(N_HEADS_B), sds(N_HEADS_B),
                   sds(N_HEADS_C // 2),
                   jax.ShapeDtypeStruct((B, 2 * N_HEADS_C, S // tm, LANES, tm), BF16),
                   sds(N_HEADS_C),
                   jax.ShapeDtypeStruct((T // tm, 1, LANES), F32)],
        compiler_params=pltpu.CompilerParams(dimension_semantics=("arbitrary",),
                                             vmem_limit_bytes=VMEM_LIMIT_BYTES),
        name="in_proj",
    )(x2d, gat, w_in_r, *tabs, qg_a, kg_a, qg_b, kvg_b, wuq_p, wukv_p, _norm_routing())


def _lane_group_max(s):
    m = s[:, :LANES]
    for j in range(1, s.shape[1] // LANES):
        m = jnp.maximum(m, s[:, j * LANES:(j + 1) * LANES])
    return m


def _pv_bounded(streams, n_chunks):
    items = [(r, i) for r in range(n_chunks) for i in range(len(streams))]
    accs = [None] * len(streams)
    in_flight = {}
    for k in range(len(items) + SCORE_LOOKAHEAD):
        if k < len(items):
            r, i = items[k]
            scores, _, row_off = streams[i]
            s, off = scores(r), row_off(r)
            in_flight[k] = s if off is None else s + off
        if k >= SCORE_LOOKAHEAD:
            r, i = items[k - SCORE_LOOKAHEAD]
            d = _dot(jnp.exp2(in_flight.pop(k - SCORE_LOOKAHEAD)).astype(BF16), streams[i][1](r))
            accs[i] = d if accs[i] is None else accs[i] + d
    return accs


def _pv_exact(streams, n_chunks, s_ref):
    rows = s_ref.shape[0] // len(streams)
    maxes = []
    for i, (scores, _, row_off) in enumerate(streams):
        m_run = None
        for r in range(n_chunks):
            s = scores(r)
            s_ref[i * rows:(i + 1) * rows, r * TK:(r + 1) * TK] = s
            m, off = _lane_group_max(s), row_off(r)
            if off is not None:
                m = m + off
            m_run = m if m_run is None else jnp.maximum(m_run, m)
        maxes.append(jnp.max(m_run, axis=-1, keepdims=True))
    accs = []
    for i, (_, values, row_off) in enumerate(streams):
        acc = None
        for r in range(n_chunks):
            off = row_off(r)
            shift = maxes[i] if off is None else maxes[i] - off
            p = jnp.exp2(s_ref[i * rows:(i + 1) * rows, r * TK:(r + 1) * TK] - shift)
            d = _dot(p.astype(BF16), values(r))
            acc = d if acc is None else acc + d
        accs.append(acc)
    return accs


def _run_heads(bounded, heads, head_streams, n_chunks, s_ref, acc_ref, roll_exact=False):
    @pl.when(bounded)
    def _():
        shared = {}
        per_head = [head_streams(h, shared) for h in heads]
        n_sub = len(per_head[0])
        accs = _pv_bounded([s for streams in per_head for s in streams], n_chunks)
        for k, h in enumerate(heads):
            acc_ref[h] = jnp.concatenate(accs[k * n_sub:(k + 1) * n_sub], axis=0)

    @pl.when(jnp.logical_not(bounded))
    def _():
        def one_head(h):
            acc_ref[h] = jnp.concatenate(_pv_exact(head_streams(h, {}), n_chunks, s_ref), axis=0)

        if roll_exact:
            lax.fori_loop(0, len(heads), lambda k, c: (one_head(heads[0] + k), c)[1], 0)
        else:
            for h in heads:
                one_head(h)


def _normalised_pair(acc_low, acc_high):
    low = _low_lanes()
    num = jnp.where(low, acc_low, acc_high)
    den = pltpu.roll(jnp.where(low, acc_high, acc_low), HALF, axis=1)
    return num / den


def _attn_ab_kernel(bounded_ref, q_ref, kt_ref, v_ref, o_ref, s_ref, acc_ref, *,
                    n_heads, per_pipeline, q_index, kt_index, v_index, out_pairs, n_keys):
    n_chunks = n_keys // TK

    def head_streams(h, shared):
        q = q_ref[0, q_index(h)]
        return [(lambda r, qs=q[r0:r0 + TQ_SUB_AB]:
                 _dot(qs, kt_ref[0, kt_index(h), :, r * TK:(r + 1) * TK]),
                 lambda r: v_ref[0, v_index(h), r * TK:(r + 1) * TK, :],
                 lambda r: None) for r0 in range(0, TQ, TQ_SUB_AB)]

    def head_group(j, carry):
        heads = [per_pipeline * j + k for k in range(per_pipeline)]
        _run_heads(bounded_ref[pl.program_id(0), j] != 0, heads, head_streams, n_chunks, s_ref,
                   acc_ref, roll_exact=True)
        return carry

    if n_heads == per_pipeline:
        head_group(0, None)
    else:
        lax.fori_loop(0, n_heads // per_pipeline, head_group, 0)
    o_ref[...] = jnp.concatenate([_normalised_pair(acc_ref[a], acc_ref[b]) for a, b in out_pairs],
                                 axis=-1).astype(BF16)


def _attn_c_kernel(bounded_ref, q_ref, kt_ref, v_ref, lq1_ref, lk1_ref, lq2_ref, lk2_ref,
                   subln_ref, o_ref, s_ref, acc_ref, *, n_keys, lam_init):
    assert TQ == TK
    nc = n_keys // TK
    cd = pl.program_id(1)
    col = lax.broadcasted_iota(jnp.int32, (1, TK), 1).astype(F32)
    row = lax.broadcasted_iota(jnp.int32, (TQ_SUB_C, 1), 0).astype(F32)
    subs = [(r, r + TQ_SUB_C) for r in range(0, TQ, TQ_SUB_C)]
    neg_dist = [-jnp.abs((lax.broadcasted_iota(jnp.int32, (TQ_SUB_C, TK), 0) + r0
                          - lax.broadcasted_iota(jnp.int32, (TQ_SUB_C, TK), 1)).astype(F32))
                for r0, _ in subs]

    chunk = [(cd + r) % nc for r in range(nc)]

    def head_bias(hh):
        slope = jnp.float32(LOG2E * 2.0 ** (-8.0 * N_HEADS_C / N_HEADS_C))
        for k in range(N_HEADS_C - 2, -1, -1):
            slope = jnp.where(hh == k, LOG2E * 2.0 ** (-8.0 * (k + 1) / N_HEADS_C), slope)
        key_term, row_off = [None], [[None] * len(subs)]
        for r in range(1, nc):
            side = jnp.where(chunk[r] < cd, slope, -slope)
            key_term.append(col * side)
            base = -slope * (jnp.abs(chunk[r] - cd) * TK).astype(F32)
            row_off.append([base - side * (row + float(r0)) for r0, _ in subs])
        return slope, key_term, row_off

    def head_pair(p, carry):
        def streams_of(k, shared):
            idx, hh = 4 * p + k, 2 * p + k // 2
            if isinstance(k, int):
                if k // 2 not in shared:
                    shared[k // 2] = head_bias(hh)
                slope, key_term, row_off = shared[k // 2]
            else:
                slope, key_term, row_off = head_bias(hh)

            def scores(r, si):
                s = _dot(q_ref[0, p][subs[si][0]:subs[si][1]], kt_ref[0, idx, chunk[r]])
                return s + (neg_dist[si] * slope if r == 0 else key_term[r])

            return [(functools.partial(scores, si=si),
                     lambda r: v_ref[0, hh, pl.ds(pl.multiple_of(chunk[r] * TK, TK), TK), :],
                     lambda r, si=si: row_off[r][si]) for si in range(len(subs))]

        _run_heads(bounded_ref[pl.program_id(0), p] != 0, list(range(4)), streams_of, nc, s_ref,
                   acc_ref.at[pl.ds(4 * p, 4)], roll_exact=True)
        return carry

    assert HEADS_PER_PIPELINE == 4
    lax.fori_loop(0, N_HEADS_C // 2, head_pair, 0)

    lam = (jnp.exp(jnp.sum(lq1_ref[...] * lk1_ref[...], axis=-1, keepdims=True))
           - jnp.exp(jnp.sum(lq2_ref[...] * lk2_ref[...], axis=-1, keepdims=True)) + lam_init)
    outs = []
    for p in range(N_HEADS_C // 2):
        o = (_normalised_pair(acc_ref[4 * p], acc_ref[4 * p + 2])
             - lam * _normalised_pair(acc_ref[4 * p + 1], acc_ref[4 * p + 3]))
        outs.append(_half_rms(o) * subln_ref[...] * (1.0 - lam_init))
    o_ref[...] = jnp.concatenate(outs, axis=-1).astype(BF16)


def _attention(kernel_fn, bounded, n_acc, q, kt, v, extras, out_width, B, S):
    nq = S // TQ
    hq, hv = q.shape[1], v.shape[1]
    extra_specs = [pl.BlockSpec(e.shape, lambda b, i, f, n=e.ndim: (0,) * n) for e in extras]
    return pl.pallas_call(
        kernel_fn,
        grid_spec=pltpu.PrefetchScalarGridSpec(
            num_scalar_prefetch=1,
            grid=(B, nq),
            in_specs=[pl.BlockSpec((1, hq, TQ, LANES), lambda b, i, f: (b, 0, i, 0)),
                      pl.BlockSpec((1,) + kt.shape[1:],
                                   lambda b, i, f, n=kt.ndim: (b,) + (0,) * (n - 1)),
                      pl.BlockSpec((1, hv, S, LANES), lambda b, i, f: (b, 0, 0, 0))] + extra_specs,
            out_specs=pl.BlockSpec((TQ, out_width), lambda b, i, f: (b * nq + i, 0)),
            scratch_shapes=[pltpu.VMEM((TQ, S), F32),
                            pltpu.VMEM((n_acc, TQ, LANES), F32)]),
        out_shape=jax.ShapeDtypeStruct((B * S, out_width), BF16),
        compiler_params=pltpu.CompilerParams(dimension_semantics=("arbitrary", "arbitrary"),
                                             vmem_limit_bytes=VMEM_LIMIT_BYTES),
        name=kernel_fn.func.__name__.strip("_"),
    )(bounded, q, kt, v, *extras)


def _mlp_kernel(xp_ref, x_ref, xn_ref, ap_ref, a_ref, an_ref, bp_ref, b_ref, bn_ref,
                cp_ref, c_ref, cn_ref, wo_ref, g_ref, wu_ref, cw_ref, cb_ref, wd_ref, fg_ref,
                o_ref, act_ref, *, tiles_per_seq, final_norm):
    i = pl.program_id(0)
    tm = x_ref.shape[0]
    halo = xp_ref.shape[0]

    def rows(p, m, n):
        return jnp.concatenate([p[...], m[...], n[...]], axis=0)

    o_ext = jnp.concatenate([rows(ap_ref, a_ref, an_ref), rows(bp_ref, b_ref, bn_ref),
                             rows(cp_ref, c_ref, cn_ref)], axis=1)
    x1 = rows(xp_ref, x_ref, xn_ref) + _dot(o_ext, wo_ref[...])
    h = (_rms(x1, D_MODEL) * g_ref[...]).astype(BF16)
    at_start = (i % tiles_per_seq) == 0
    at_end = (i % tiles_per_seq) == tiles_per_seq - 1
    zeros = jnp.zeros((halo, D_MODEL), BF16)
    hext = jnp.concatenate([jnp.where(at_start, zeros, h[:halo]), h[halo:halo + tm],
                            jnp.where(at_end, zeros, h[halo + tm:])], axis=0)

    def branch(k, c0):
        u = _dot(hext, wu_ref[:, k * D_FF + c0:k * D_FF + c0 + FFN_CHUNK])
        cw = cw_ref[k, :, c0:c0 + FFN_CHUNK]
        n = u.shape[0]
        prev = pltpu.roll(u, 1, axis=0)[halo:halo + tm]
        nxt = pltpu.roll(u, n - 1, axis=0)[halo:halo + tm]
        return (prev * cw[0:1] + u[halo:halo + tm] * cw[1:2] + nxt * cw[2:3]
                + cb_ref[k, :, c0:c0 + FFN_CHUNK])

    y = x1[halo:halo + tm]
    done = 0
    for c0 in range(0, D_FF, FFN_CHUNK):
        g = branch(0, c0)
        val = branch(1, c0)
        half = 0.5 * g
        act_ref[:, c0:c0 + FFN_CHUNK] = ((half + half * jnp.tanh(half)) * val).astype(BF16)
        ready = c0 + FFN_CHUNK - FFN_DOWN_LAG * FFN_CHUNK
        if ready - done >= FFN_DOWN_GROUP:
            y = y + _dot(act_ref[:, done:done + FFN_DOWN_GROUP], wd_ref[done:done + FFN_DOWN_GROUP])
            done += FFN_DOWN_GROUP
    y = y + _dot(act_ref[:, done:], wd_ref[done:])
    if final_norm:
        y = _rms(y, D_MODEL) * fg_ref[...]
    o_ref[...] = y


def _mlp(x2d, oa, ob, oc, w_out, g_ffn, w_up, conv_w, conv_b, w_down, final_g, S, final_norm):
    T = x2d.shape[0]
    tm, halo = TM_FFN, BF16_SUBLANES
    per = tm // halo
    last = T // halo - 1
    resident = lambda shape: pl.BlockSpec(shape, lambda i: (0,) * len(shape),
                                          pipeline_mode=pl.Buffered(1))

    def with_halos(width):
        return [pl.BlockSpec((halo, width), lambda i: (jnp.maximum(i * per - 1, 0), 0)),
                pl.BlockSpec((tm, width), lambda i: (i, 0)),
                pl.BlockSpec((halo, width), lambda i: (jnp.minimum((i + 1) * per, last), 0))]

    return pl.pallas_call(
        functools.partial(_mlp_kernel, tiles_per_seq=S // tm, final_norm=final_norm),
        grid=(T // tm,),
        in_specs=(with_halos(D_MODEL) + with_halos(oa.shape[1]) + with_halos(ob.shape[1])
                  + with_halos(oc.shape[1])
                  + [resident(w_out.shape), resident((1, D_MODEL)),
                     resident((D_MODEL, 2 * D_FF)), resident((2, 3, D_FF)), resident((2, 1, D_FF)),
                     resident((D_FF, D_MODEL)), resident((1, D_MODEL))]),
        out_specs=pl.BlockSpec((tm, D_MODEL), lambda i: (i, 0)),
        out_shape=jax.ShapeDtypeStruct((T, D_MODEL), F32),
        scratch_shapes=[pltpu.VMEM((tm, D_FF), BF16)],
        compiler_params=pltpu.CompilerParams(dimension_semantics=("arbitrary",),
                                             vmem_limit_bytes=VMEM_LIMIT_BYTES),
        name="out_proj_conv_mlp",
    )(x2d, x2d, x2d, oa, oa, oa, ob, ob, ob, oc, oc, oc, w_out, g_ffn,
      w_up, conv_w, conv_b, w_down, final_g)


def _pad_groups(w, n_groups, width, offset=0):
    r = w.shape[0]
    w = w.reshape(r, n_groups, width)
    w = jnp.pad(w, ((0, 0), (0, 0), (offset, LANES - width - offset)))
    return w.reshape(r, n_groups * LANES)


def _group_a_lane_order():
    nf = HEAD_DIM // 4
    lanes = np.arange(LANES)
    second, r = lanes // HALF, lanes % HALF
    head, rr = r // A_QUARTER, r % A_QUARTER
    block, freq = rr // nf, rr % nf
    dim = block * 2 * nf + second * nf + freq
    return head, dim, block, freq, second


def _group_b_lane_order():
    nf = MLA_ROPE // 4
    src = np.full(LANES, -1)
    block = np.zeros(LANES, np.int64)
    freq = np.zeros(LANES, np.int64)
    second = np.zeros(LANES, np.int64)
    is_rope = np.zeros(LANES, bool)
    for e in range(2):
        for j in range(2 * nf):
            lane = e * HALF + j
            is_rope[lane], block[lane], freq[lane], second[lane] = True, j // nf, j % nf, e
            src[lane] = MLA_NOPE + (j // nf) * 2 * nf + e * nf + j % nf
    n_low = HALF - 2 * nf
    src[2 * nf:HALF] = np.arange(n_low)
    src[HALF + 2 * nf:HALF + 2 * nf + MLA_NOPE - n_low] = np.arange(n_low, MLA_NOPE)
    return src, is_rope, block, freq, second


def _gather_cols(w, src):
    return jnp.where(jnp.asarray(src >= 0), w[:, np.maximum(src, 0)], 0.0)


def _rope_tables(S):
    t = jnp.arange(S, dtype=jnp.int32)
    pos = jnp.stack([(t // GRID_W).astype(F32), (t % GRID_W).astype(F32)], axis=0)

    def table(nf, block, freq, second, is_rope):
        inv = ROPE_BASE ** (-jnp.arange(nf, dtype=F32) / nf)
        ang = pos[block].T * inv[freq]
        sign = jnp.asarray(2.0 * second - 1.0, F32)
        cos = jnp.where(jnp.asarray(is_rope), jnp.cos(ang), 1.0)
        sin = jnp.where(jnp.asarray(is_rope), jnp.sin(ang) * sign, 0.0)
        return cos, sin

    _, _, block, freq, second = _group_a_lane_order()
    cos_a, sin_a = table(HEAD_DIM // 4, block, freq, second, np.ones(LANES, bool))
    _, is_rope, block, freq, second = _group_b_lane_order()
    cos_b, sin_b = table(MLA_ROPE // 4, block, freq, second, is_rope)
    return cos_a, sin_a, cos_b, sin_b


def _prep_w_in(w):
    n_qa, n_ka = N_HEADS_A * HEAD_DIM, N_KV_A * HEAD_DIM
    a_end = n_qa + 2 * n_ka
    b_end = a_end + MLA_Q_RANK + MLA_KV_RANK + MLA_ROPE
    head, dim, _, _, _ = _group_a_lane_order()
    pair_cols = head * HEAD_DIM + dim
    qa_cols = np.concatenate([p * LANES + pair_cols for p in range(N_HEADS_A // 2)])
    cq = w[:, a_end:a_end + MLA_Q_RANK]
    ckv = w[:, a_end + MLA_Q_RANK:a_end + MLA_Q_RANK + MLA_KV_RANK]
    kr = w[:, b_end - MLA_ROPE:b_end]
    src_b, is_rope, _, _, _ = _group_b_lane_order()
    out = jnp.concatenate([
        w[:, qa_cols], w[:, n_qa + pair_cols], w[:, n_qa + n_ka:a_end],
        jnp.pad(cq, ((0, 0), (0, CQ_PAD - MLA_Q_RANK))),
        ckv,
        _gather_cols(kr, np.where(is_rope, src_b - MLA_NOPE, -1)),
        w[:, b_end:],
    ], axis=-1).astype(BF16)
    assert out.shape[1] == IN_COLS
    return out


def _twice(v):
    return jnp.tile(v, 2).reshape(1, 2 * v.shape[0])


def kernel(x, norm_attn, w_in, q_norm_a, k_norm_a, q_a_norm_b, w_uq_b, kv_a_norm_b, w_ukv_b,
           lambda_q1_c, lambda_k1_c, lambda_q2_c, lambda_k2_c, subln_c, w_out,
           norm_ffn, w_up, conv_w, conv_b, w_down, final_norm):
    B, S, D = x.shape
    depth = w_in.shape[0]
    assert D == D_MODEL and S % max(TM_IN, TM_FFN, TQ, TK) == 0 and S % GRID_W == 0
    assert TM_IN == TK
    T = B * S

    tabs = _rope_tables(S)
    row = lambda v: v.reshape(1, -1)
    a_pairs = [(j, j + N_HEADS_A // 2) for j in range(N_HEADS_A // 2)]
    a_rows = np.concatenate([np.arange(h * HEAD_DIM, (h + 1) * HEAD_DIM)
                             for pair in a_pairs for h in pair])
    xc = x.reshape(T, D)
    for l in range(depth):
        lam_init = 0.8 - 0.6 * math.exp(-0.3 * l)
        src_b, is_rope, _, _, _ = _group_b_lane_order()
        d_qk = MLA_NOPE + MLA_ROPE
        wuq_p = jnp.pad(
            jnp.concatenate([_gather_cols(w_uq_b[l][:, h * d_qk:(h + 1) * d_qk], src_b)
                             for h in range(N_HEADS_B)], axis=-1),
            ((0, CQ_PAD - MLA_Q_RANK), (0, 0))).astype(BF16)
        wukv = w_ukv_b[l].reshape(MLA_KV_RANK, N_HEADS_B, MLA_NOPE + MLA_V)
        k_cols = [_gather_cols(wukv[:, h, :MLA_NOPE], np.where(is_rope, -1, src_b))
                  for h in range(N_HEADS_B)]
        v_cols = [_pad_groups(wukv[:, h, MLA_NOPE:], 1, MLA_V, offset=HALF * (h % 2))
                  for h in range(N_HEADS_B)]
        wukv_p = jnp.concatenate(k_cols + v_cols, axis=-1).astype(BF16)

        qa, kta, va, qb, ktb, vb, qc, ktc, vc, nrm = _in_proj(
            xc, row(norm_attn[l]), _prep_w_in(w_in[l]), tabs,
            q_norm_a[l][_group_a_lane_order()[1]].reshape(1, LANES),
            k_norm_a[l][_group_a_lane_order()[1]].reshape(1, LANES),
            jnp.pad(q_a_norm_b[l], (0, CQ_PAD - MLA_Q_RANK)).reshape(1, CQ_PAD),
            row(kv_a_norm_b[l]), wuq_p, wukv_p, B, S)

        nrm = jnp.max(nrm.reshape(B, -1, LANES), axis=1)

        def bounded(q0, nq, q_rep, k0, nk, k_rep, per_pipeline):
            q2 = jnp.repeat(nrm[:, q0:q0 + nq], q_rep, axis=1)
            k2 = jnp.repeat(nrm[:, k0:k0 + nk], k_rep, axis=1)
            ok = q2 * k2 * NRM_MARGIN <= SAFE_LOG2_RANGE ** 2
            return jnp.all(ok.reshape(B, -1, per_pipeline), axis=-1).astype(jnp.int32)

        qk_a = (HEAD_DIM * jnp.max(jnp.abs(q_norm_a[l])) * jnp.max(jnp.abs(k_norm_a[l]))
                * (HEAD_DIM ** -0.5 * LOG2E))
        bounded_a = jnp.broadcast_to(
            (qk_a * NRM_MARGIN <= SAFE_LOG2_RANGE).astype(jnp.int32),
            (B, N_HEADS_A // HEADS_PER_PIPELINE_AB))

        group = N_HEADS_A // N_KV_A
        oa = _attention(
            functools.partial(_attn_ab_kernel, n_heads=N_HEADS_A,
                              per_pipeline=HEADS_PER_PIPELINE_AB, q_index=lambda h: h // 2,
                              kt_index=lambda h: 2 * (h // group) + h % 2,
                              v_index=lambda h: h // group, out_pairs=a_pairs, n_keys=S),
            bounded_a, N_HEADS_A, qa, kta, va, [], N_HEADS_A * HEAD_DIM, B, S)
        same = lambda h: h
        ob = _attention(
            functools.partial(_attn_ab_kernel, n_heads=N_HEADS_B,
                              per_pipeline=HEADS_PER_PIPELINE_AB, q_index=same, kt_index=same,
                              v_index=same,
                              out_pairs=[(2 * j, 2 * j + 1) for j in range(N_HEADS_B // 2)],
                              n_keys=S),
            bounded(NRM_QB, N_HEADS_B, 1, NRM_KB, N_HEADS_B, 1, HEADS_PER_PIPELINE_AB), N_HEADS_B,
            qb, ktb, vb, [], N_HEADS_B * MLA_V, B, S)
        oc = _attention(
            functools.partial(_attn_c_kernel, n_keys=S, lam_init=lam_init),
            bounded(NRM_QC, N_HEADS_C, 2, NRM_KC, 2 * N_HEADS_C, 1, HEADS_PER_PIPELINE),
            2 * N_HEADS_C, qc, ktc, vc,
            [row(lambda_q1_c[l]), row(lambda_k1_c[l]), row(lambda_q2_c[l]), row(lambda_k2_c[l]),
             _twice(subln_c[l])],
            N_HEADS_C * DIFF_V, B, S)

        w_out_l = jnp.concatenate([w_out[l][a_rows], w_out[l][N_HEADS_A * HEAD_DIM:]], axis=0)
        xc = _mlp(xc, oa, ob, oc, w_out_l.astype(BF16), row(norm_ffn[l]), w_up[l].astype(BF16),
                  conv_w[l].reshape(3, 2, D_FF).transpose(1, 0, 2),
                  conv_b[l].reshape(2, 1, D_FF), w_down[l].astype(BF16),
                  row(final_norm), S, final_norm=(l == depth - 1))
    return xc.reshape(B, S, D)
```

```python
import functools
import math

import jax
import jax.numpy as jnp
import numpy as np
from jax import lax
from jax.experimental import pallas as pl
from jax.experimental.pallas import tpu as pltpu

D_MODEL = 1024
GRID_W = 64
HEAD_DIM = 64
N_HEADS_A = 8
N_KV_A = 2
N_HEADS_B = 4
MLA_Q_RANK = 192
MLA_KV_RANK = 128
MLA_NOPE = 64
MLA_ROPE = 32
MLA_V = 64
N_HEADS_C = 4
DIFF_QK = 32
DIFF_V = 64
D_FF = 2816
ROPE_BASE = 10000.0
EPS = 1e-6

LANES = 128
HALF = LANES // 2
A_QUARTER = LANES // 4
BF16_SUBLANES = 16
VMEM_LIMIT_BYTES = 56 * 1024 * 1024

LOG2E = math.log2(math.e)
assert HEAD_DIM == MLA_V == DIFF_V == 2 * DIFF_QK == HALF

OFF_QA = 0
OFF_KA = OFF_QA + N_HEADS_A * HEAD_DIM
OFF_VA = OFF_KA + LANES
OFF_CQ = OFF_VA + LANES
CQ_PAD = 2 * LANES
OFF_CKV = OFF_CQ + CQ_PAD
OFF_KR = OFF_CKV + MLA_KV_RANK
OFF_QC = OFF_KR + LANES
OFF_KC = OFF_QC + N_HEADS_C * HALF
OFF_VC = OFF_KC + N_HEADS_C * HALF
IN_COLS = OFF_VC + N_HEADS_C * HALF

NRM_QB = 0
NRM_KB = NRM_QB + N_HEADS_B
NRM_QC = NRM_KB + N_HEADS_B
NRM_KC = NRM_QC + N_HEADS_C
NRM_GROUPS = 2 * N_HEADS_B + N_HEADS_C
NRM_MARGIN = 1.03
SAFE_LOG2_RANGE = 55.0

F32 = jnp.float32
BF16 = jnp.bfloat16

TM_IN = 512
TM_FFN = 512
FFN_CHUNK = 256
FFN_DOWN_GROUP = 6 * FFN_CHUNK
FFN_DOWN_LAG = 2
TQ = 512
TQ_SUB_AB = 512
TQ_SUB_C = 256
TK = 512
SCORE_LOOKAHEAD = 2
HEADS_PER_PIPELINE = 4
HEADS_PER_PIPELINE_AB = 4


def _dot(a, b):
    return jnp.dot(a, b, preferred_element_type=F32)


def _rms(x, n):
    ms = jnp.sum(x * x, axis=-1, keepdims=True) * (1.0 / n)
    return x * lax.rsqrt(ms + EPS)


def _low_lanes():
    return lax.broadcasted_iota(jnp.int32, (1, LANES), 1) < HALF


def _half_rms(x, low=None):
    low = _low_lanes() if low is None else low
    sq = x * x
    s_lo = jnp.sum(jnp.where(low, sq, 0.0), axis=-1, keepdims=True)
    s_hi = jnp.sum(jnp.where(low, 0.0, sq), axis=-1, keepdims=True)
    return x * lax.rsqrt(jnp.where(low, s_lo, s_hi) * (1.0 / HALF) + EPS)


def _in_kernel(x_ref, gat_ref, w_ref, cos_a_ref, sin_a_ref, cos_b_ref, sin_b_ref,
               qg_a_ref, kg_a_ref, qg_b_ref, kvg_b_ref, wuq_ref, wukv_ref, sel_ref,
               qa_ref, kta_ref, va_ref, qb_ref, ktb_ref, vb_ref, qc_ref, ktc_ref, vc_ref, nrm_ref):
    x = x_ref[...]
    tm = x.shape[0]
    lane = lax.broadcasted_iota(jnp.int32, (1, LANES), 1)
    low = lane < HALF
    row = lax.broadcasted_iota(jnp.int32, (LANES, 1), 0)

    inv_rms = lax.rsqrt(jnp.mean(x * x, axis=-1, keepdims=True) + EPS)
    z = _dot((x * gat_ref[...]).astype(BF16), w_ref[...]) * inv_rms

    def seg(off, width):
        return z[:, off:off + width]

    def rope(y, c, s):
        return y * c + pltpu.roll(y, HALF, axis=1) * s

    def group(z, g):
        return z[:, g * LANES:(g + 1) * LANES]

    squares = []

    def note(y):
        squares.append((y * y).astype(BF16))

    def noted_row_sums(first):
        return _dot(jnp.concatenate(squares[first:], axis=-1),
                    sel_ref[first * LANES:len(squares) * LANES])

    cos_b, sin_b = cos_b_ref[...], sin_b_ref[...]
    cq =(_rms(seg(OFF_CQ, CQ_PAD), MLA_Q_RANK) * qg_b_ref[...]).astype(BF16)
    qb = _dot(cq, wuq_ref[...])
    scale_b = (MLA_NOPE + MLA_ROPE) ** -0.5 * LOG2E
    for hh in range(N_HEADS_B):
        y = rope(group(qb, hh), cos_b, sin_b) * scale_b
        qb_ref[0, hh] = y.astype(BF16)
        note(y)
    ckv = (_rms(seg(OFF_CKV, MLA_KV_RANK), MLA_KV_RANK) * kvg_b_ref[...]).astype(BF16)
    kvb = _dot(ckv, wukv_ref[...])
    kr = rope(seg(OFF_KR, LANES), cos_b, sin_b)
    for hh in range(N_HEADS_B):
        y = group(kvb, hh) + kr
        ktb_ref[0, hh] = y.T.astype(BF16)
        note(y)
        ones_other_half = jnp.where(low, float(hh % 2), float(1 - hh % 2))
        vb_ref[0, hh] = (group(kvb, N_HEADS_B + hh) + ones_other_half).astype(BF16)
    n2 = noted_row_sums(0)
    n_noted_b = len(squares)

    zq = seg(OFF_QC, N_HEADS_C * HALF)
    zk = seg(OFF_KC, N_HEADS_C * HALF)
    zv = seg(OFF_VC, N_HEADS_C * HALF)
    k_groups = []
    for p in range(N_HEADS_C // 2):
        y = group(zq, p) * (DIFF_QK ** -0.5 * LOG2E)
        qc_ref[0, p] = y.astype(BF16)
        note(y)
        k_groups.append(group(zk, p))
        kt = group(zk, p).T
        for sub in range(4):
            mine = (row >= sub * DIFF_QK) & (row < (sub + 1) * DIFF_QK)
            ktc_ref[0, 4 * p + sub, 0] = jnp.where(mine, kt, 0.0).astype(BF16)
        vz = group(zv, p)
        vc_ref[0, 2 * p] = jnp.where(low, vz, 1.0).astype(BF16)
        vc_ref[0, 2 * p + 1] = jnp.where(low, 1.0, vz).astype(BF16)
    for kz in k_groups:
        note(kz)
    n2 = n2 + noted_row_sums(n_noted_b)
    nrm_ref[0] = jnp.max(n2, axis=0, keepdims=True)

    cos_a, sin_a = cos_a_ref[...], sin_a_ref[...]
    first_head = (lane & A_QUARTER) == 0
    zq = seg(OFF_QA, N_HEADS_A * HEAD_DIM)
    for p in range(N_HEADS_A // 2):
        y = rope(_half_rms(group(zq, p), first_head) * qg_a_ref[...], cos_a, sin_a)
        qa_ref[0, p] = (y * (HEAD_DIM ** -0.5 * LOG2E)).astype(BF16)
    y = rope(_half_rms(seg(OFF_KA, LANES), first_head) * kg_a_ref[...], cos_a, sin_a)
    kt = y.T.astype(BF16)
    quarter = [kt[j * A_QUARTER:(j + 1) * A_QUARTER] for j in range(4)]
    zeros = jnp.zeros((A_QUARTER, tm), BF16)
    for g in range(N_KV_A):
        k1, k2 = quarter[g], quarter[2 + g]
        kta_ref[0, 2 * g] = jnp.concatenate([k1, zeros, k2, zeros], axis=0)
        kta_ref[0, 2 * g + 1] = jnp.concatenate([zeros, k1, zeros, k2], axis=0)
    zv = seg(OFF_VA, LANES)
    va_ref[0, 0] = jnp.where(low, zv, 1.0).astype(BF16)
    va_ref[0, 1] = jnp.where(low, 1.0, zv).astype(BF16)


def _norm_routing():
    sel = np.zeros((NRM_GROUPS, LANES, LANES), np.float32)
    g = 0
    for slot in ([NRM_QB + h for h in range(N_HEADS_B)] + [NRM_KB + h for h in range(N_HEADS_B)]):
        sel[g, :, slot] = 1.0
        g += 1
    for p in range(N_HEADS_C // 2):
        for f in range(2):
            sel[g, f * HALF:(f + 1) * HALF, NRM_QC + 2 * p + f] = 1.0
        g += 1
    for p in range(N_HEADS_C // 2):
        for sub in range(4):
            sel[g, sub * DIFF_QK:(sub + 1) * DIFF_QK, NRM_KC + 4 * p + sub] = 1.0
        g += 1
    assert g == NRM_GROUPS
    return jnp.asarray(sel.reshape(NRM_GROUPS * LANES, LANES), BF16)


def _in_proj(x2d, gat, w_in_r, tabs, qg_a, kg_a, qg_b, kvg_b, wuq_p, wukv_p, B, S):
    T = B * S
    tm = TM_IN
    nst = S // tm
    const = lambda shape: pl.BlockSpec(shape, lambda i: (0,) * len(shape))
    tab = pl.BlockSpec((tm, LANES), lambda i: (i % nst, 0))
    hm = lambda H: pl.BlockSpec((1, H, tm, LANES), lambda i: (i // nst, 0, i % nst, 0))
    hmt = lambda H: pl.BlockSpec((1, H, LANES, tm), lambda i: (i // nst, 0, 0, i % nst))
    sds = lambda H: jax.ShapeDtypeStruct((B, H, S, LANES), BF16)
    sdt = lambda H: jax.ShapeDtypeStruct((B, H, LANES, S), BF16)
    return pl.pallas_call(
        _in_kernel,
        grid=(T // tm,),
        in_specs=[pl.BlockSpec((tm, D_MODEL), lambda i: (i, 0)),
                  const((1, D_MODEL)), const((D_MODEL, IN_COLS)),
                  tab, tab, tab, tab,
                  const((1, LANES)), const((1, LANES)), const((1, CQ_PAD)), const((1, MLA_KV_RANK)),
                  const((CQ_PAD, N_HEADS_B * LANES)), const((MLA_KV_RANK, 2 * N_HEADS_B * LANES)),
                  const((NRM_GROUPS * LANES, LANES))],
        out_specs=[hm(N_HEADS_A // 2), hmt(2 * N_KV_A), hm(N_KV_A),
                   hm(N_HEADS_B), hmt(N_HEADS_B), hm(N_HEADS_B),
                   hm(N_HEADS_C // 2),
                   pl.BlockSpec((1, 2 * N_HEADS_C, 1, LANES, tm),
                                lambda i: (i // nst, 0, i % nst, 0, 0)),
                   hm(N_HEADS_C),
                   pl.BlockSpec((1, 1, LANES), lambda i: (i, 0, 0))],
        out_shape=[sds(N_HEADS_A // 2), sdt(2 * N_KV_A), sds(N_KV_A),
                   sds(N_HEADS_B), sdt(N_HEADS_B), sds(N_HEADS_B),
                   sds(N_HEADS_C // 2),
                   jax.ShapeDtypeStruct((B, 2 * N_HEADS_C, S // tm, LANES, tm), BF16),
                   sds(N_HEADS_C),
                   jax.ShapeDtypeStruct((T // tm, 1, LANES), F32)],
        compiler_params=pltpu.CompilerParams(dimension_semantics=("arbitrary",),
                                             vmem_limit_bytes=VMEM_LIMIT_BYTES),
        name="in_proj",
    )(x2d, gat, w_in_r, *tabs, qg_a, kg_a, qg_b, kvg_b, wuq_p, wukv_p, _norm_routing())


def _lane_group_max(s):
    m = s[:, :LANES]
    for j in range(1, s.shape[1] // LANES):
        m = jnp.maximum(m, s[:, j * LANES:(j + 1) * LANES])
    return m


def _pv_bounded(streams, n_chunks):
    items = [(r, i) for r in range(n_chunks) for i in range(len(streams))]
    accs = [None] * len(streams)
    in_flight = {}
    for k in range(len(items) + SCORE_LOOKAHEAD):
        if k < len(items):
            r, i = items[k]
            scores, _, row_off = streams[i]
            s, off = scores(r), row_off(r)
            in_flight[k] = s if off is None else s + off
        if k >= SCORE_LOOKAHEAD:
            r, i = items[k - SCORE_LOOKAHEAD]
            d = _dot(jnp.exp2(in_flight.pop(k - SCORE_LOOKAHEAD)).astype(BF16), streams[i][1](r))
            accs[i] = d if accs[i] is None else accs[i] + d
    return accs


def _pv_exact(streams, n_chunks, s_ref):
    rows = s_ref.shape[0] // len(streams)
    maxes = []
    for i, (scores, _, row_off) in enumerate(streams):
        m_run = None
        for r in range(n_chunks):
            s = scores(r)
            s_ref[i * rows:(i + 1) * rows, r * TK:(r + 1) * TK] = s
            m, off = _lane_group_max(s), row_off(r)
            if off is not None:
                m = m + off
            m_run = m if m_run is None else jnp.maximum(m_run, m)
        maxes.append(jnp.max(m_run, axis=-1, keepdims=True))
    accs = []
    for i, (_, values, row_off) in enumerate(streams):
        acc = None
        for r in range(n_chunks):
            off = row_off(r)
            shift = maxes[i] if off is None else maxes[i] - off
            p = jnp.exp2(s_ref[i * rows:(i + 1) * rows, r * TK:(r + 1) * TK] - shift)
            d = _dot(p.astype(BF16), values(r))
            acc = d if acc is None else acc + d
        accs.append(acc)
    return accs


def _run_heads(bounded, heads, head_streams, n_chunks, s_ref, acc_ref, roll_exact=False):
    @pl.when(bounded)
    def _():
        shared = {}
        per_head = [head_streams(h, shared) for h in heads]
        n_sub = len(per_head[0])
        accs = _pv_bounded([s for streams in per_head for s in streams], n_chunks)
        for k, h in enumerate(heads):
            acc_ref[h] = jnp.concatenate(accs[k * n_sub:(k + 1) * n_sub], axis=0)

    @pl.when(jnp.logical_not(bounded))
    def _():
        def one_head(h):
            acc_ref[h] = jnp.concatenate(_pv_exact(head_streams(h, {}), n_chunks, s_ref), axis=0)

        if roll_exact:
            lax.fori_loop(0, len(heads), lambda k, c: (one_head(heads[0] + k), c)[1], 0)
        else:
            for h in heads:
                one_head(h)


def _normalised_pair(acc_low, acc_high):
    low = _low_lanes()
    num = jnp.where(low, acc_low, acc_high)
    den = pltpu.roll(jnp.where(low, acc_high, acc_low), HALF, axis=1)
    return num / den


def _attn_ab_kernel(bounded_ref, q_ref, kt_ref, v_ref, o_ref, s_ref, acc_ref, *,
                    n_heads, per_pipeline, q_index, kt_index, v_index, out_pairs, n_keys):
    n_chunks = n_keys // TK

    def head_streams(h, shared):
        q = q_ref[0, q_index(h)]
        return [(lambda r, qs=q[r0:r0 + TQ_SUB_AB]:
                 _dot(qs, kt_ref[0, kt_index(h), :, r * TK:(r + 1) * TK]),
                 lambda r: v_ref[0, v_index(h), r * TK:(r + 1) * TK, :],
                 lambda r: None) for r0 in range(0, TQ, TQ_SUB_AB)]

    def head_group(j, carry):
        heads = [per_pipeline * j + k for k in range(per_pipeline)]
        _run_heads(bounded_ref[pl.program_id(0), j] != 0, heads, head_streams, n_chunks, s_ref,
                   acc_ref, roll_exact=True)
        return carry

    if n_heads == per_pipeline:
        head_group(0, None)
    else:
        lax.fori_loop(0, n_heads // per_pipeline, head_group, 0)
    o_ref[...] = jnp.concatenate([_normalised_pair(acc_ref[a], acc_ref[b]) for a, b in out_pairs],
                                 axis=-1).astype(BF16)


def _attn_c_kernel(bounded_ref, q_ref, kt_ref, v_ref, lq1_ref, lk1_ref, lq2_ref, lk2_ref,
                   subln_ref, o_ref, s_ref, acc_ref, *, n_keys, lam_init):
    assert TQ == TK
    nc = n_keys // TK
    cd = pl.program_id(1)
    col = lax.broadcasted_iota(jnp.int32, (1, TK), 1).astype(F32)
    row = lax.broadcasted_iota(jnp.int32, (TQ_SUB_C, 1), 0).astype(F32)
    subs = [(r, r + TQ_SUB_C) for r in range(0, TQ, TQ_SUB_C)]
    neg_dist = [-jnp.abs((lax.broadcasted_iota(jnp.int32, (TQ_SUB_C, TK), 0) + r0
                          - lax.broadcasted_iota(jnp.int32, (TQ_SUB_C, TK), 1)).astype(F32))
                for r0, _ in subs]

    chunk = [(cd + r) % nc for r in range(nc)]

    def head_bias(hh):
        slope = jnp.float32(LOG2E * 2.0 ** (-8.0 * N_HEADS_C / N_HEADS_C))
        for k in range(N_HEADS_C - 2, -1, -1):
            slope = jnp.where(hh == k, LOG2E * 2.0 ** (-8.0 * (k + 1) / N_HEADS_C), slope)
        key_term, row_off = [None], [[None] * len(subs)]
        for r in range(1, nc):
            side = jnp.where(chunk[r] < cd, slope, -slope)
            key_term.append(col * side)
            base = -slope * (jnp.abs(chunk[r] - cd) * TK).astype(F32)
            row_off.append([base - side * (row + float(r0)) for r0, _ in subs])
        return slope, key_term, row_off

    def head_pair(p, carry):
        def streams_of(k, shared):
            idx, hh = 4 * p + k, 2 * p + k // 2
            if isinstance(k, int):
                if k // 2 not in shared:
                    shared[k // 2] = head_bias(hh)
                slope, key_term, row_off = shared[k // 2]
            else:
                slope, key_term, row_off = head_bias(hh)

            def scores(r, si):
                s = _dot(q_ref[0, p][subs[si][0]:subs[si][1]], kt_ref[0, idx, chunk[r]])
                return s + (neg_dist[si] * slope if r == 0 else key_term[r])

            return [(functools.partial(scores, si=si),
                     lambda r: v_ref[0, hh, pl.ds(pl.multiple_of(chunk[r] * TK, TK), TK), :],
                     lambda r, si=si: row_off[r][si]) for si in range(len(subs))]

        _run_heads(bounded_ref[pl.program_id(0), p] != 0, list(range(4)), streams_of, nc, s_ref,
                   acc_ref.at[pl.ds(4 * p, 4)], roll_exact=True)
        return carry

    assert HEADS_PER_PIPELINE == 4
    lax.fori_loop(0, N_HEADS_C // 2, head_pair, 0)

    lam = (jnp.exp(jnp.sum(lq1_ref[...] * lk1_ref[...], axis=-1, keepdims=True))
           - jnp.exp(jnp.sum(lq2_ref[...] * lk2_ref[...], axis=-1, keepdims=True)) + lam_init)
    outs = []
    for p in range(N_HEADS_C // 2):
        o = (_normalised_pair(acc_ref[4 * p], acc_ref[4 * p + 2])
             - lam * _normalised_pair(acc_ref[4 * p + 1], acc_ref[4 * p + 3]))
        outs.append(_half_rms(o) * subln_ref[...] * (1.0 - lam_init))
    o_ref[...] = jnp.concatenate(outs, axis=-1).astype(BF16)


def _attention(kernel_fn, bounded, n_acc, q, kt, v, extras, out_width, B, S):
    nq = S // TQ
    hq, hv = q.shape[1], v.shape[1]
    extra_specs = [pl.BlockSpec(e.shape, lambda b, i, f, n=e.ndim: (0,) * n) for e in extras]
    return pl.pallas_call(
        kernel_fn,
        grid_spec=pltpu.PrefetchScalarGridSpec(
            num_scalar_prefetch=1,
            grid=(B, nq),
            in_specs=[pl.BlockSpec((1, hq, TQ, LANES), lambda b, i, f: (b, 0, i, 0)),
                      pl.BlockSpec((1,) + kt.shape[1:],
                                   lambda b, i, f, n=kt.ndim: (b,) + (0,) * (n - 1)),
                      pl.BlockSpec((1, hv, S, LANES), lambda b, i, f: (b, 0, 0, 0))] + extra_specs,
            out_specs=pl.BlockSpec((TQ, out_width), lambda b, i, f: (b * nq + i, 0)),
            scratch_shapes=[pltpu.VMEM((TQ, S), F32),
                            pltpu.VMEM((n_acc, TQ, LANES), F32)]),
        out_shape=jax.ShapeDtypeStruct((B * S, out_width), BF16),
        compiler_params=pltpu.CompilerParams(dimension_semantics=("arbitrary", "arbitrary"),
                                             vmem_limit_bytes=VMEM_LIMIT_BYTES),
        name=kernel_fn.func.__name__.strip("_"),
    )(bounded, q, kt, v, *extras)


def _mlp_kernel(xp_ref, x_ref, xn_ref, ap_ref, a_ref, an_ref, bp_ref, b_ref, bn_ref,
                cp_ref, c_ref, cn_ref, wo_ref, g_ref, wu_ref, cw_ref, cb_ref, wd_ref, fg_ref,
                o_ref, act_ref, *, tiles_per_seq, final_norm):
    i = pl.program_id(0)
    tm = x_ref.shape[0]
    halo = xp_ref.shape[0]

    def rows(p, m, n):
        return jnp.concatenate([p[...], m[...], n[...]], axis=0)

    o_ext = jnp.concatenate([rows(ap_ref, a_ref, an_ref), rows(bp_ref, b_ref, bn_ref),
                             rows(cp_ref, c_ref, cn_ref)], axis=1)
    x1 = rows(xp_ref, x_ref, xn_ref) + _dot(o_ext, wo_ref[...])
    h = (_rms(x1, D_MODEL) * g_ref[...]).astype(BF16)
    at_start = (i % tiles_per_seq) == 0
    at_end = (i % tiles_per_seq) == tiles_per_seq - 1
    zeros = jnp.zeros((halo, D_MODEL), BF16)
    hext = jnp.concatenate([jnp.where(at_start, zeros, h[:halo]), h[halo:halo + tm],
                            jnp.where(at_end, zeros, h[halo + tm:])], axis=0)

    def branch(k, c0):
        u = _dot(hext, wu_ref[:, k * D_FF + c0:k * D_FF + c0 + FFN_CHUNK])
        cw = cw_ref[k, :, c0:c0 + FFN_CHUNK]
        n = u.shape[0]
        prev = pltpu.roll(u, 1, axis=0)[halo:halo + tm]
        nxt = pltpu.roll(u, n - 1, axis=0)[halo:halo + tm]
        return (prev * cw[0:1] + u[halo:halo + tm] * cw[1:2] + nxt * cw[2:3]
                + cb_ref[k, :, c0:c0 + FFN_CHUNK])

    y = x1[halo:halo + tm]
    done = 0
    for c0 in range(0, D_FF, FFN_CHUNK):
        g = branch(0, c0)
        val = branch(1, c0)
        half = 0.5 * g
        act_ref[:, c0:c0 + FFN_CHUNK] = ((half + half * jnp.tanh(half)) * val).astype(BF16)
        ready = c0 + FFN_CHUNK - FFN_DOWN_LAG * FFN_CHUNK
        if ready - done >= FFN_DOWN_GROUP:
            y = y + _dot(act_ref[:, done:done + FFN_DOWN_GROUP], wd_ref[done:done + FFN_DOWN_GROUP])
            done += FFN_DOWN_GROUP
    y = y + _dot(act_ref[:, done:], wd_ref[done:])
    if final_norm:
        y = _rms(y, D_MODEL) * fg_ref[...]
    o_ref[...] = y


def _mlp(x2d, oa, ob, oc, w_out, g_ffn, w_up, conv_w, conv_b, w_down, final_g, S, final_norm):
    T = x2d.shape[0]
    tm, halo = TM_FFN, BF16_SUBLANES
    per = tm // halo
    last = T // halo - 1
    resident = lambda shape: pl.BlockSpec(shape, lambda i: (0,) * len(shape),
                                          pipeline_mode=pl.Buffered(1))

    def with_halos(width):
        return [pl.BlockSpec((halo, width), lambda i: (jnp.maximum(i * per - 1, 0), 0)),
                pl.BlockSpec((tm, width), lambda i: (i, 0)),
                pl.BlockSpec((halo, width), lambda i: (jnp.minimum((i + 1) * per, last), 0))]

    return pl.pallas_call(
        functools.partial(_mlp_kernel, tiles_per_seq=S // tm, final_norm=final_norm),
        grid=(T // tm,),
        in_specs=(with_halos(D_MODEL) + with_halos(oa.shape[1]) + with_halos(ob.shape[1])
                  + with_halos(oc.shape[1])
                  + [resident(w_out.shape), resident((1, D_MODEL)),
                     resident((D_MODEL, 2 * D_FF)), resident((2, 3, D_FF)), resident((2, 1, D_FF)),
                     resident((D_FF, D_MODEL)), resident((1, D_MODEL))]),
        out_specs=pl.BlockSpec((tm, D_MODEL), lambda i: (i, 0)),
        out_shape=jax.ShapeDtypeStruct((T, D_MODEL), F32),
        scratch_shapes=[pltpu.VMEM((tm, D_FF), BF16)],
        compiler_params=pltpu.CompilerParams(dimension_semantics=("arbitrary",),
                                             vmem_limit_bytes=VMEM_LIMIT_BYTES),
        name="out_proj_conv_mlp",
    )(x2d, x2d, x2d, oa, oa, oa, ob, ob, ob, oc, oc, oc, w_out, g_ffn,
      w_up, conv_w, conv_b, w_down, final_g)


def _pad_groups(w, n_groups, width, offset=0):
    r = w.shape[0]
    w = w.reshape(r, n_groups, width)
    w = jnp.pad(w, ((0, 0), (0, 0), (offset, LANES - width - offset)))
    return w.reshape(r, n_groups * LANES)


def _group_a_lane_order():
    nf = HEAD_DIM // 4
    lanes = np.arange(LANES)
    second, r = lanes // HALF, lanes % HALF
    head, rr = r // A_QUARTER, r % A_QUARTER
    block, freq = rr // nf, rr % nf
    dim = block * 2 * nf + second * nf + freq
    return head, dim, block, freq, second


def _group_b_lane_order():
    nf = MLA_ROPE // 4
    src = np.full(LANES, -1)
    block = np.zeros(LANES, np.int64)
    freq = np.zeros(LANES, np.int64)
    second = np.zeros(LANES, np.int64)
    is_rope = np.zeros(LANES, bool)
    for e in range(2):
        for j in range(2 * nf):
            lane = e * HALF + j
            is_rope[lane], block[lane], freq[lane], second[lane] = True, j // nf, j % nf, e
            src[lane] = MLA_NOPE + (j // nf) * 2 * nf + e * nf + j % nf
    n_low = HALF - 2 * nf
    src[2 * nf:HALF] = np.arange(n_low)
    src[HALF + 2 * nf:HALF + 2 * nf + MLA_NOPE - n_low] = np.arange(n_low, MLA_NOPE)
    return src, is_rope, block, freq, second


def _gather_cols(w, src):
    return jnp.where(jnp.asarray(src >= 0), w[:, np.maximum(src, 0)], 0.0)


def _rope_tables(S):
    t = jnp.arange(S, dtype=jnp.int32)
    pos = jnp.stack([(t // GRID_W).astype(F32), (t % GRID_W).astype(F32)], axis=0)

    def table(nf, block, freq, second, is_rope):
        inv = ROPE_BASE ** (-jnp.arange(nf, dtype=F32) / nf)
        ang = pos[block].T * inv[freq]
        sign = jnp.asarray(2.0 * second - 1.0, F32)
        cos = jnp.where(jnp.asarray(is_rope), jnp.cos(ang), 1.0)
        sin = jnp.where(jnp.asarray(is_rope), jnp.sin(ang) * sign, 0.0)
        return cos, sin

    _, _, block, freq, second = _group_a_lane_order()
    cos_a, sin_a = table(HEAD_DIM // 4, block, freq, second, np.ones(LANES, bool))
    _, is_rope, block, freq, second = _group_b_lane_order()
    cos_b, sin_b = table(MLA_ROPE // 4, block, freq, second, is_rope)
    return cos_a, sin_a, cos_b, sin_b


def _prep_w_in(w):
    n_qa, n_ka = N_HEADS_A * HEAD_DIM, N_KV_A * HEAD_DIM
    a_end = n_qa + 2 * n_ka
    b_end = a_end + MLA_Q_RANK + MLA_KV_RANK + MLA_ROPE
    head, dim, _, _, _ = _group_a_lane_order()
    pair_cols = head * HEAD_DIM + dim
    qa_cols = np.concatenate([p * LANES + pair_cols for p in range(N_HEADS_A // 2)])
    cq = w[:, a_end:a_end + MLA_Q_RANK]
    ckv = w[:, a_end + MLA_Q_RANK:a_end + MLA_Q_RANK + MLA_KV_RANK]
    kr = w[:, b_end - MLA_ROPE:b_end]
    src_b, is_rope, _, _, _ = _group_b_lane_order()
    out = jnp.concatenate([
        w[:, qa_cols], w[:, n_qa + pair_cols], w[:, n_qa + n_ka:a_end],
        jnp.pad(cq, ((0, 0), (0, CQ_PAD - MLA_Q_RANK))),
        ckv,
        _gather_cols(kr, np.where(is_rope, src_b - MLA_NOPE, -1)),
        w[:, b_end:],
    ], axis=-1).astype(BF16)
    assert out.shape[1] == IN_COLS
    return out


def _twice(v):
    return jnp.tile(v, 2).reshape(1, 2 * v.shape[0])


def kernel(x, norm_attn, w_in, q_norm_a, k_norm_a, q_a_norm_b, w_uq_b, kv_a_norm_b, w_ukv_b,
           lambda_q1_c, lambda_k1_c, lambda_q2_c, lambda_k2_c, subln_c, w_out,
           norm_ffn, w_up, conv_w, conv_b, w_down, final_norm):
    B, S, D = x.shape
    depth = w_in.shape[0]
    assert D == D_MODEL and S % max(TM_IN, TM_FFN, TQ, TK) == 0 and S % GRID_W == 0
    assert TM_IN == TK
    T = B * S

    tabs = _rope_tables(S)
    row = lambda v: v.reshape(1, -1)
    a_pairs = [(j, j + N_HEADS_A // 2) for j in range(N_HEADS_A // 2)]
    a_rows = np.concatenate([np.arange(h * HEAD_DIM, (h + 1) * HEAD_DIM)
                             for pair in a_pairs for h in pair])
    xc = x.reshape(T, D)
    for l in range(depth):
        lam_init = 0.8 - 0.6 * math.exp(-0.3 * l)
        src_b, is_rope, _, _, _ = _group_b_lane_order()
        d_qk = MLA_NOPE + MLA_ROPE
        wuq_p = jnp.pad(
            jnp.concatenate([_gather_cols(w_uq_b[l][:, h * d_qk:(h + 1) * d_qk], src_b)
                             for h in range(N_HEADS_B)], axis=-1),
            ((0, CQ_PAD - MLA_Q_RANK), (0, 0))).astype(BF16)
        wukv = w_ukv_b[l].reshape(MLA_KV_RANK, N_HEADS_B, MLA_NOPE + MLA_V)
        k_cols = [_gather_cols(wukv[:, h, :MLA_NOPE], np.where(is_rope, -1, src_b))
                  for h in range(N_HEADS_B)]
        v_cols = [_pad_groups(wukv[:, h, MLA_NOPE:], 1, MLA_V, offset=HALF * (h % 2))
                  for h in range(N_HEADS_B)]
        wukv_p = jnp.concatenate(k_cols + v_cols, axis=-1).astype(BF16)

        qa, kta, va, qb, ktb, vb, qc, ktc, vc, nrm = _in_proj(
            xc, row(norm_attn[l]), _prep_w_in(w_in[l]), tabs,
            q_norm_a[l][_group_a_lane_order()[1]].reshape(1, LANES),
            k_norm_a[l][_group_a_lane_order()[1]].reshape(1, LANES),
            jnp.pad(q_a_norm_b[l], (0, CQ_PAD - MLA_Q_RANK)).reshape(1, CQ_PAD),
            row(kv_a_norm_b[l]), wuq_p, wukv_p, B, S)

        nrm = jnp.max(nrm.reshape(B, -1, LANES), axis=1)

        def bounded(q0, nq, q_rep, k0, nk, k_rep, per_pipeline):
            q2 = jnp.repeat(nrm[:, q0:q0 + nq], q_rep, axis=1)
            k2 = jnp.repeat(nrm[:, k0:k0 + nk], k_rep, axis=1)
            ok = q2 * k2 * NRM_MARGIN <= SAFE_LOG2_RANGE ** 2
            return jnp.all(ok.reshape(B, -1, per_pipeline), axis=-1).astype(jnp.int32)

        qk_a = (HEAD_DIM * jnp.max(jnp.abs(q_norm_a[l])) * jnp.max(jnp.abs(k_norm_a[l]))
                * (HEAD_DIM ** -0.5 * LOG2E))
        bounded_a = jnp.broadcast_to(
            (qk_a * NRM_MARGIN <= SAFE_LOG2_RANGE).astype(jnp.int32),
            (B, N_HEADS_A // HEADS_PER_PIPELINE_AB))

        group = N_HEADS_A // N_KV_A
        oa = _attention(
            functools.partial(_attn_ab_kernel, n_heads=N_HEADS_A,
                              per_pipeline=HEADS_PER_PIPELINE_AB, q_index=lambda h: h // 2,
                              kt_index=lambda h: 2 * (h // group) + h % 2,
                              v_index=lambda h: h // group, out_pairs=a_pairs, n_keys=S),
            bounded_a, N_HEADS_A, qa, kta, va, [], N_HEADS_A * HEAD_DIM, B, S)
        same = lambda h: h
        ob = _attention(
            functools.partial(_attn_ab_kernel, n_heads=N_HEADS_B,
                              per_pipeline=HEADS_PER_PIPELINE_AB, q_index=same, kt_index=same,
                              v_index=same,
                              out_pairs=[(2 * j, 2 * j + 1) for j in range(N_HEADS_B // 2)],
                              n_keys=S),
            bounded(NRM_QB, N_HEADS_B, 1, NRM_KB, N_HEADS_B, 1, HEADS_PER_PIPELINE_AB), N_HEADS_B,
            qb, ktb, vb, [], N_HEADS_B * MLA_V, B, S)
        oc = _attention(
            functools.partial(_attn_c_kernel, n_keys=S, lam_init=lam_init),
            bounded(NRM_QC, N_HEADS_C, 2, NRM_KC, 2 * N_HEADS_C, 1, HEADS_PER_PIPELINE),
            2 * N_HEADS_C, qc, ktc, vc,
            [row(lambda_q1_c[l]), row(lambda_k1_c[l]), row(lambda_q2_c[l]), row(lambda_k2_c[l]),
             _twice(subln_c[l])],
            N_HEADS_C * DIFF_V, B, S)

        w_out_l = jnp.concatenate([w_out[l][a_rows], w_out[l][N_HEADS_A * HEAD_DIM:]], axis=0)
        xc = _mlp(xc, oa, ob, oc, w_out_l.astype(BF16), row(norm_ffn[l]), w_up[l].astype(BF16),
                  conv_w[l].reshape(3, 2, D_FF).transpose(1, 0, 2),
                  conv_b[l].reshape(2, 1, D_FF), w_down[l].astype(BF16),
                  row(final_norm), S, final_norm=(l == depth - 1))
    return xc.reshape(B, S, D)
```

```python
import functools
import math

import jax
import jax.numpy as jnp
import numpy as np
from jax import lax
from jax.experimental import pallas as pl
from jax.experimental.pallas import tpu as pltpu

D_MODEL = 1024
GRID_W = 64
HEAD_DIM = 64
N_HEADS_A = 8
N_KV_A = 2
N_HEADS_B = 4
MLA_Q_RANK = 192
MLA_KV_RANK = 128
MLA_NOPE = 64
MLA_ROPE = 32
MLA_V = 64
N_HEADS_C = 4
DIFF_QK = 32
DIFF_V = 64
D_FF = 2816
ROPE_BASE = 10000.0
EPS = 1e-6

LANES = 128
HALF = LANES // 2
A_QUARTER = LANES // 4
BF16_SUBLANES = 16
VMEM_LIMIT_BYTES = 56 * 1024 * 1024

LOG2E = math.log2(math.e)
assert HEAD_DIM == MLA_V == DIFF_V == 2 * DIFF_QK == HALF

OFF_QA = 0
OFF_KA = OFF_QA + N_HEADS_A * HEAD_DIM
OFF_VA = OFF_KA + LANES
OFF_CQ = OFF_VA + LANES
CQ_PAD = 2 * LANES
OFF_CKV = OFF_CQ + CQ_PAD
OFF_KR = OFF_CKV + MLA_KV_RANK
OFF_QC = OFF_KR + LANES
OFF_KC = OFF_QC + N_HEADS_C * HALF
OFF_VC = OFF_KC + N_HEADS_C * HALF
IN_COLS = OFF_VC + N_HEADS_C * HALF

NRM_QB = 0
NRM_KB = NRM_QB + N_HEADS_B
NRM_QC = NRM_KB + N_HEADS_B
NRM_KC = NRM_QC + N_HEADS_C
NRM_GROUPS = 2 * N_HEADS_B + N_HEADS_C
NRM_MARGIN = 1.03
SAFE_LOG2_RANGE = 55.0

F32 = jnp.float32
BF16 = jnp.bfloat16

TM_IN = 512
TM_FFN = 512
FFN_CHUNK = 256
FFN_DOWN_GROUP = 6 * FFN_CHUNK
FFN_DOWN_LAG = 2
TQ = 512
TQ_SUB_AB = 512
TQ_SUB_C = 256
TK = 512
SCORE_LOOKAHEAD = 2
HEADS_PER_PIPELINE = 4
HEADS_PER_PIPELINE_AB = 4


def _dot(a, b):
    return jnp.dot(a, b, preferred_element_type=F32)


def _rms(x, n):
    ms = jnp.sum(x * x, axis=-1, keepdims=True) * (1.0 / n)
    return x * lax.rsqrt(ms + EPS)


def _low_lanes():
    return lax.broadcasted_iota(jnp.int32, (1, LANES), 1) < HALF


def _half_rms(x, low=None):
    low = _low_lanes() if low is None else low
    sq = x * x
    s_lo = jnp.sum(jnp.where(low, sq, 0.0), axis=-1, keepdims=True)
    s_hi = jnp.sum(jnp.where(low, 0.0, sq), axis=-1, keepdims=True)
    return x * lax.rsqrt(jnp.where(low, s_lo, s_hi) * (1.0 / HALF) + EPS)


def _in_kernel(x_ref, gat_ref, w_ref, cos_a_ref, sin_a_ref, cos_b_ref, sin_b_ref,
               qg_a_ref, kg_a_ref, qg_b_ref, kvg_b_ref, wuq_ref, wukv_ref, sel_ref,
               qa_ref, kta_ref, va_ref, qb_ref, ktb_ref, vb_ref, qc_ref, ktc_ref, vc_ref, nrm_ref):
    x = x_ref[...]
    tm = x.shape[0]
    h = (_rms(x, D_MODEL) * gat_ref[...]).astype(BF16)

    lane = lax.broadcasted_iota(jnp.int32, (1, LANES), 1)
    low = lane < HALF
    row = lax.broadcasted_iota(jnp.int32, (LANES, 1), 0)

    z = _dot(h, w_ref[...])

    def seg(off, width):
        return z[:, off:off + width]

    def rope(y, c, s):
        return y * c + pltpu.roll(y, HALF, axis=1) * s

    def group(z, g):
        return z[:, g * LANES:(g + 1) * LANES]

    squares = []

    def note(y):
        squares.append((y * y).astype(BF16))

    def noted_row_sums(first):
        return _dot(jnp.concatenate(squares[first:], axis=-1),
                    sel_ref[first * LANES:len(squares) * LANES])

    cos_b, sin_b = cos_b_ref[...], sin_b_ref[...]
    cq =(_rms(seg(OFF_CQ, CQ_PAD), MLA_Q_RANK) * qg_b_ref[...]).astype(BF16)
    qb = _dot(cq, wuq_ref[...])
    scale_b = (MLA_NOPE + MLA_ROPE) ** -0.5 * LOG2E
    for hh in range(N_HEADS_B):
        y = rope(group(qb, hh), cos_b, sin_b) * scale_b
        qb_ref[0, hh] = y.astype(BF16)
        note(y)
    ckv = (_rms(seg(OFF_CKV, MLA_KV_RANK), MLA_KV_RANK) * kvg_b_ref[...]).astype(BF16)
    kvb = _dot(ckv, wukv_ref[...])
    kr = rope(seg(OFF_KR, LANES), cos_b, sin_b)
    for hh in range(N_HEADS_B):
        y = group(kvb, hh) + kr
        ktb_ref[0, hh] = y.T.astype(BF16)
        note(y)
        ones_other_half = jnp.where(low, float(hh % 2), float(1 - hh % 2))
        vb_ref[0, hh] = (group(kvb, N_HEADS_B + hh) + ones_other_half).astype(BF16)
    n2 = noted_row_sums(0)
    n_noted_b = len(squares)

    zq = seg(OFF_QC, N_HEADS_C * HALF)
    zk = seg(OFF_KC, N_HEADS_C * HALF)
    zv = seg(OFF_VC, N_HEADS_C * HALF)
    k_groups = []
    for p in range(N_HEADS_C // 2):
        y = group(zq, p) * (DIFF_QK ** -0.5 * LOG2E)
        qc_ref[0, p] = y.astype(BF16)
        note(y)
        k_groups.append(group(zk, p))
        kt = group(zk, p).T
        for sub in range(4):
            mine = (row >= sub * DIFF_QK) & (row < (sub + 1) * DIFF_QK)
            ktc_ref[0, 4 * p + sub, 0] = jnp.where(mine, kt, 0.0).astype(BF16)
        vz = group(zv, p)
        vc_ref[0, 2 * p] = jnp.where(low, vz, 1.0).astype(BF16)
        vc_ref[0, 2 * p + 1] = jnp.where(low, 1.0, vz).astype(BF16)
    for kz in k_groups:
        note(kz)
    n2 = n2 + noted_row_sums(n_noted_b)
    nrm_ref[0] = jnp.max(n2, axis=0, keepdims=True)

    cos_a, sin_a = cos_a_ref[...], sin_a_ref[...]
    first_head = (lane & A_QUARTER) == 0
    zq = seg(OFF_QA, N_HEADS_A * HEAD_DIM)
    for p in range(N_HEADS_A // 2):
        y = rope(_half_rms(group(zq, p), first_head) * qg_a_ref[...], cos_a, sin_a)
        qa_ref[0, p] = (y * (HEAD_DIM ** -0.5 * LOG2E)).astype(BF16)
    y = rope(_half_rms(seg(OFF_KA, LANES), first_head) * kg_a_ref[...], cos_a, sin_a)
    kt = y.T.astype(BF16)
    quarter = [kt[j * A_QUARTER:(j + 1) * A_QUARTER] for j in range(4)]
    zeros = jnp.zeros((A_QUARTER, tm), BF16)
    for g in range(N_KV_A):
        k1, k2 = quarter[g], quarter[2 + g]
        kta_ref[0, 2 * g] = jnp.concatenate([k1, zeros, k2, zeros], axis=0)
        kta_ref[0, 2 * g + 1] = jnp.concatenate([zeros, k1, zeros, k2], axis=0)
    zv = seg(OFF_VA, LANES)
    va_ref[0, 0] = jnp.where(low, zv, 1.0).astype(BF16)
    va_ref[0, 1] = jnp.where(low, 1.0, zv).astype(BF16)


def _norm_routing():
    sel = np.zeros((NRM_GROUPS, LANES, LANES), np.float32)
    g = 0
    for slot in ([NRM_QB + h for h in range(N_HEADS_B)] + [NRM_KB + h for h in range(N_HEADS_B)]):
        sel[g, :, slot] = 1.0
        g += 1
    for p in range(N_HEADS_C // 2):
        for f in range(2):
            sel[g, f * HALF:(f + 1) * HALF, NRM_QC + 2 * p + f] = 1.0
        g += 1
    for p in range(N_HEADS_C // 2):
        for sub in range(4):
            sel[g, sub * DIFF_QK:(sub + 1) * DIFF_QK, NRM_KC + 4 * p + sub] = 1.0
        g += 1
    assert g == NRM_GROUPS
    return jnp.asarray(sel.reshape(NRM_GROUPS * LANES, LANES), BF16)


def _in_proj(x2d, gat, w_in_r, tabs, qg_a, kg_a, qg_b, kvg_b, wuq_p, wukv_p, B, S):
    T = B * S
    tm = TM_IN
    nst = S // tm
    const = lambda shape: pl.BlockSpec(shape, lambda i: (0,) * len(shape))
    tab = pl.BlockSpec((tm, LANES), lambda i: (i % nst, 0))
    hm = lambda H: pl.BlockSpec((1, H, tm, LANES), lambda i: (i // nst, 0, i % nst, 0))
    hmt = lambda H: pl.BlockSpec((1, H, LANES, tm), lambda i: (i // nst, 0, 0, i % nst))
    sds = lambda H: jax.ShapeDtypeStruct((B, H, S, LANES), BF16)
    sdt = lambda H: jax.ShapeDtypeStruct((B, H, LANES, S), BF16)
    return pl.pallas_call(
        _in_kernel,
        grid=(T // tm,),
        in_specs=[pl.BlockSpec((tm, D_MODEL), lambda i: (i, 0)),
                  const((1, D_MODEL)), const((D_MODEL, IN_COLS)),
                  tab, tab, tab, tab,
                  const((1, LANES)), const((1, LANES)), const((1, CQ_PAD)), const((1, MLA_KV_RANK)),
                  const((CQ_PAD, N_HEADS_B * LANES)), const((MLA_KV_RANK, 2 * N_HEADS_B * LANES)),
                  const((NRM_GROUPS * LANES, LANES))],
        out_specs=[hm(N_HEADS_A // 2), hmt(2 * N_KV_A), hm(N_KV_A),
                   hm(N_HEADS_B), hmt(N_HEADS_B), hm(N_HEADS_B),
                   hm(N_HEADS_C // 2),
                   pl.BlockSpec((1, 2 * N_HEADS_C, 1, LANES, tm),
                                lambda i: (i // nst, 0, i % nst, 0, 0)),
                   hm(N_HEADS_C),
                   pl.BlockSpec((1, 1, LANES), lambda i: (i, 0, 0))],
        out_shape=[sds(N_HEADS_A // 2), sdt(2 * N_KV_A), sds(N_KV_A),
                   sds(N_HEADS_B), sdt(N_HEADS_B), sds(N_HEADS_B),
                   sds(N_HEADS_C // 2),
                   jax.ShapeDtypeStruct((B, 2 * N_HEADS_C, S // tm, LANES, tm), BF16),
                   sds(N_HEADS_C),
                   jax.ShapeDtypeStruct((T // tm, 1, LANES), F32)],
        compiler_params=pltpu.CompilerParams(dimension_semantics=("arbitrary",),
                                             vmem_limit_bytes=VMEM_LIMIT_BYTES),
        name="in_proj",
    )(x2d, gat, w_in_r, *tabs, qg_a, kg_a, qg_b, kvg_b, wuq_p, wukv_p, _norm_routing())


def _lane_group_max(s):
    m = s[:, :LANES]
    for j in range(1, s.shape[1] // LANES):
        m = jnp.maximum(m, s[:, j * LANES:(j + 1) * LANES])
    return m


def _pv_bounded(streams, n_chunks):
    items = [(r, i) for r in range(n_chunks) for i in range(len(streams))]
    accs = [None] * len(streams)
    in_flight = {}
    for k in range(len(items) + SCORE_LOOKAHEAD):
        if k < len(items):
            r, i = items[k]
            scores, _, row_off = streams[i]
            s, off = scores(r), row_off(r)
            in_flight[k] = s if off is None else s + off
        if k >= SCORE_LOOKAHEAD:
            r, i = items[k - SCORE_LOOKAHEAD]
            d = _dot(jnp.exp2(in_flight.pop(k - SCORE_LOOKAHEAD)).astype(BF16), streams[i][1](r))
            accs[i] = d if accs[i] is None else accs[i] + d
    return accs


def _pv_exact(streams, n_chunks, s_ref):
    rows = s_ref.shape[0] // len(streams)
    maxes = []
    for i, (scores, _, row_off) in enumerate(streams):
        m_run = None
        for r in range(n_chunks):
            s = scores(r)
            s_ref[i * rows:(i + 1) * rows, r * TK:(r + 1) * TK] = s
            m, off = _lane_group_max(s), row_off(r)
            if off is not None:
                m = m + off
            m_run = m if m_run is None else jnp.maximum(m_run, m)
        maxes.append(jnp.max(m_run, axis=-1, keepdims=True))
    accs = []
    for i, (_, values, row_off) in enumerate(streams):
        acc = None
        for r in range(n_chunks):
            off = row_off(r)
            shift = maxes[i] if off is None else maxes[i] - off
            p = jnp.exp2(s_ref[i * rows:(i + 1) * rows, r * TK:(r + 1) * TK] - shift)
            d = _dot(p.astype(BF16), values(r))
            acc = d if acc is None else acc + d
        accs.append(acc)
    return accs


def _run_heads(bounded, heads, head_streams, n_chunks, s_ref, acc_ref, roll_exact=False):
    @pl.when(bounded)
    def _():
        shared = {}
        per_head = [head_streams(h, shared) for h in heads]
        n_sub = len(per_head[0])
        accs = _pv_bounded([s for streams in per_head for s in streams], n_chunks)
        for k, h in enumerate(heads):
            acc_ref[h] = jnp.concatenate(accs[k * n_sub:(k + 1) * n_sub], axis=0)

    @pl.when(jnp.logical_not(bounded))
    def _():
        def one_head(h):
            acc_ref[h] = jnp.concatenate(_pv_exact(head_streams(h, {}), n_chunks, s_ref), axis=0)

        if roll_exact:
            lax.fori_loop(0, len(heads), lambda k, c: (one_head(heads[0] + k), c)[1], 0)
        else:
            for h in heads:
                one_head(h)


def _normalised_pair(acc_low, acc_high):
    low = _low_lanes()
    num = jnp.where(low, acc_low, acc_high)
    den = pltpu.roll(jnp.where(low, acc_high, acc_low), HALF, axis=1)
    return num / den


def _attn_ab_kernel(bounded_ref, q_ref, kt_ref, v_ref, o_ref, s_ref, acc_ref, *,
                    n_heads, per_pipeline, q_index, kt_index, v_index, out_pairs, n_keys):
    n_chunks = n_keys // TK

    def head_streams(h, shared):
        q = q_ref[0, q_index(h)]
        return [(lambda r, qs=q[r0:r0 + TQ_SUB_AB]:
                 _dot(qs, kt_ref[0, kt_index(h), :, r * TK:(r + 1) * TK]),
                 lambda r: v_ref[0, v_index(h), r * TK:(r + 1) * TK, :],
                 lambda r: None) for r0 in range(0, TQ, TQ_SUB_AB)]

    def head_group(j, carry):
        heads = [per_pipeline * j + k for k in range(per_pipeline)]
        _run_heads(bounded_ref[pl.program_id(0), j] != 0, heads, head_streams, n_chunks, s_ref,
                   acc_ref, roll_exact=True)
        return carry

    if n_heads == per_pipeline:
        head_group(0, None)
    else:
        lax.fori_loop(0, n_heads // per_pipeline, head_group, 0)
    o_ref[...] = jnp.concatenate([_normalised_pair(acc_ref[a], acc_ref[b]) for a, b in out_pairs],
                                 axis=-1).astype(BF16)


def _attn_c_kernel(bounded_ref, q_ref, kt_ref, v_ref, lq1_ref, lk1_ref, lq2_ref, lk2_ref,
                   subln_ref, o_ref, s_ref, acc_ref, *, n_keys, lam_init):
    assert TQ == TK
    nc = n_keys // TK
    cd = pl.program_id(1)
    col = lax.broadcasted_iota(jnp.int32, (1, TK), 1).astype(F32)
    row = lax.broadcasted_iota(jnp.int32, (TQ_SUB_C, 1), 0).astype(F32)
    subs = [(r, r + TQ_SUB_C) for r in range(0, TQ, TQ_SUB_C)]
    neg_dist = [-jnp.abs((lax.broadcasted_iota(jnp.int32, (TQ_SUB_C, TK), 0) + r0
                          - lax.broadcasted_iota(jnp.int32, (TQ_SUB_C, TK), 1)).astype(F32))
                for r0, _ in subs]

    chunk = [(cd + r) % nc for r in range(nc)]

    def head_bias(hh):
        slope = jnp.float32(LOG2E * 2.0 ** (-8.0 * N_HEADS_C / N_HEADS_C))
        for k in range(N_HEADS_C - 2, -1, -1):
            slope = jnp.where(hh == k, LOG2E * 2.0 ** (-8.0 * (k + 1) / N_HEADS_C), slope)
        key_term, row_off = [None], [[None] * len(subs)]
        for r in range(1, nc):
            side = jnp.where(chunk[r] < cd, slope, -slope)
            key_term.append(col * side)
            base = -slope * (jnp.abs(chunk[r] - cd) * TK).astype(F32)
            row_off.append([base - side * (row + float(r0)) for r0, _ in subs])
        return slope, key_term, row_off

    def head_pair(p, carry):
        def streams_of(k, shared):
            idx, hh = 4 * p + k, 2 * p + k // 2
            if isinstance(k, int):
                if k // 2 not in shared:
                    shared[k // 2] = head_bias(hh)
                slope, key_term, row_off = shared[k // 2]
            else:
                slope, key_term, row_off = head_bias(hh)

            def scores(r, si):
                s = _dot(q_ref[0, p][subs[si][0]:subs[si][1]], kt_ref[0, idx, chunk[r]])
                return s + (neg_dist[si] * slope if r == 0 else key_term[r])

            return [(functools.partial(scores, si=si),
                     lambda r: v_ref[0, hh, pl.ds(pl.multiple_of(chunk[r] * TK, TK), TK), :],
                     lambda r, si=si: row_off[r][si]) for si in range(len(subs))]

        _run_heads(bounded_ref[pl.program_id(0), p] != 0, list(range(4)), streams_of, nc, s_ref,
                   acc_ref.at[pl.ds(4 * p, 4)], roll_exact=True)
        return carry

    assert HEADS_PER_PIPELINE == 4
    lax.fori_loop(0, N_HEADS_C // 2, head_pair, 0)

    lam = (jnp.exp(jnp.sum(lq1_ref[...] * lk1_ref[...], axis=-1, keepdims=True))
           - jnp.exp(jnp.sum(lq2_ref[...] * lk2_ref[...], axis=-1, keepdims=True)) + lam_init)
    outs = []
    for p in range(N_HEADS_C // 2):
        o = (_normalised_pair(acc_ref[4 * p], acc_ref[4 * p + 2])
             - lam * _normalised_pair(acc_ref[4 * p + 1], acc_ref[4 * p + 3]))
        outs.append(_half_rms(o) * subln_ref[...] * (1.0 - lam_init))
    o_ref[...] = jnp.concatenate(outs, axis=-1).astype(BF16)


def _attention(kernel_fn, bounded, n_acc, q, kt, v, extras, out_width, B, S):
    nq = S // TQ
    hq, hv = q.shape[1], v.shape[1]
    extra_specs = [pl.BlockSpec(e.shape, lambda b, i, f, n=e.ndim: (0,) * n) for e in extras]
    return pl.pallas_call(
        kernel_fn,
        grid_spec=pltpu.PrefetchScalarGridSpec(
            num_scalar_prefetch=1,
            grid=(B, nq),
            in_specs=[pl.BlockSpec((1, hq, TQ, LANES), lambda b, i, f: (b, 0, i, 0)),
                      pl.BlockSpec((1,) + kt.shape[1:],
                                   lambda b, i, f, n=kt.ndim: (b,) + (0,) * (n - 1)),
                      pl.BlockSpec((1, hv, S, LANES), lambda b, i, f: (b, 0, 0, 0))] + extra_specs,
            out_specs=pl.BlockSpec((TQ, out_width), lambda b, i, f: (b * nq + i, 0)),
            scratch_shapes=[pltpu.VMEM((TQ, S), F32),
                            pltpu.VMEM((n_acc, TQ, LANES), F32)]),
        out_shape=jax.ShapeDtypeStruct((B * S, out_width), BF16),
        compiler_params=pltpu.CompilerParams(dimension_semantics=("arbitrary", "arbitrary"),
                                             vmem_limit_bytes=VMEM_LIMIT_BYTES),
        name=kernel_fn.func.__name__.strip("_"),
    )(bounded, q, kt, v, *extras)


def _mlp_kernel(xp_ref, x_ref, xn_ref, ap_ref, a_ref, an_ref, bp_ref, b_ref, bn_ref,
                cp_ref, c_ref, cn_ref, wo_ref, g_ref, wu_ref, cw_ref, cb_ref, wd_ref, fg_ref,
                o_ref, act_ref, *, tiles_per_seq, final_norm):
    i = pl.program_id(0)
    tm = x_ref.shape[0]
    halo = xp_ref.shape[0]

    def rows(p, m, n):
        return jnp.concatenate([p[...], m[...], n[...]], axis=0)

    o_ext = jnp.concatenate([rows(ap_ref, a_ref, an_ref), rows(bp_ref, b_ref, bn_ref),
                             rows(cp_ref, c_ref, cn_ref)], axis=1)
    x1 = rows(xp_ref, x_ref, xn_ref) + _dot(o_ext, wo_ref[...])
    h = (_rms(x1, D_MODEL) * g_ref[...]).astype(BF16)
    at_start = (i % tiles_per_seq) == 0
    at_end = (i % tiles_per_seq) == tiles_per_seq - 1
    zeros = jnp.zeros((halo, D_MODEL), BF16)
    hext = jnp.concatenate([jnp.where(at_start, zeros, h[:halo]), h[halo:halo + tm],
                            jnp.where(at_end, zeros, h[halo + tm:])], axis=0)

    def branch(k, c0):
        u = _dot(hext, wu_ref[:, k * D_FF + c0:k * D_FF + c0 + FFN_CHUNK])
        cw = cw_ref[k, :, c0:c0 + FFN_CHUNK]
        n = u.shape[0]
        prev = pltpu.roll(u, 1, axis=0)[halo:halo + tm]
        nxt = pltpu.roll(u, n - 1, axis=0)[halo:halo + tm]
        return (prev * cw[0:1] + u[halo:halo + tm] * cw[1:2] + nxt * cw[2:3]
                + cb_ref[k, :, c0:c0 + FFN_CHUNK])

    y = x1[halo:halo + tm]
    done = 0
    for c0 in range(0, D_FF, FFN_CHUNK):
        g = branch(0, c0)
        val = branch(1, c0)
        half = 0.5 * g
        act_ref[:, c0:c0 + FFN_CHUNK] = ((half + half * jnp.tanh(half)) * val).astype(BF16)
        ready = c0 + FFN_CHUNK - FFN_DOWN_LAG * FFN_CHUNK
        if ready - done >= FFN_DOWN_GROUP:
            y = y + _dot(act_ref[:, done:done + FFN_DOWN_GROUP], wd_ref[done:done + FFN_DOWN_GROUP])
            done += FFN_DOWN_GROUP
    y = y + _dot(act_ref[:, done:], wd_ref[done:])
    if final_norm:
        y = _rms(y, D_MODEL) * fg_ref[...]
    o_ref[...] = y


def _mlp(x2d, oa, ob, oc, w_out, g_ffn, w_up, conv_w, conv_b, w_down, final_g, S, final_norm):
    T = x2d.shape[0]
    tm, halo = TM_FFN, BF16_SUBLANES
    per = tm // halo
    last = T // halo - 1
    resident = lambda shape: pl.BlockSpec(shape, lambda i: (0,) * len(shape),
                                          pipeline_mode=pl.Buffered(1))

    def with_halos(width):
        return [pl.BlockSpec((halo, width), lambda i: (jnp.maximum(i * per - 1, 0), 0)),
                pl.BlockSpec((tm, width), lambda i: (i, 0)),
                pl.BlockSpec((halo, width), lambda i: (jnp.minimum((i + 1) * per, last), 0))]

    return pl.pallas_call(
        functools.partial(_mlp_kernel, tiles_per_seq=S // tm, final_norm=final_norm),
        grid=(T // tm,),
        in_specs=(with_halos(D_MODEL) + with_halos(oa.shape[1]) + with_halos(ob.shape[1])
                  + with_halos(oc.shape[1])
                  + [resident(w_out.shape), resident((1, D_MODEL)),
                     resident((D_MODEL, 2 * D_FF)), resident((2, 3, D_FF)), resident((2, 1, D_FF)),
                     resident((D_FF, D_MODEL)), resident((1, D_MODEL))]),
        out_specs=pl.BlockSpec((tm, D_MODEL), lambda i: (i, 0)),
        out_shape=jax.ShapeDtypeStruct((T, D_MODEL), F32),
        scratch_shapes=[pltpu.VMEM((tm, D_FF), BF16)],
        compiler_params=pltpu.CompilerParams(dimension_semantics=("arbitrary",),
                                             vmem_limit_bytes=VMEM_LIMIT_BYTES),
        name="out_proj_conv_mlp",
    )(x2d, x2d, x2d, oa, oa, oa, ob, ob, ob, oc, oc, oc, w_out, g_ffn,
      w_up, conv_w, conv_b, w_down, final_g)


def _group_a_lane_order():
    nf = HEAD_DIM // 4
    lanes = np.arange(LANES)
    second, r = lanes // HALF, lanes % HALF
    head, rr = r // A_QUARTER, r % A_QUARTER
    block, freq = rr // nf, rr % nf
    dim = block * 2 * nf + second * nf + freq
    return head, dim, block, freq, second


def _group_b_lane_order():
    nf = MLA_ROPE // 4
    src = np.full(LANES, -1)
    block = np.zeros(LANES, np.int64)
    freq = np.zeros(LANES, np.int64)
    second = np.zeros(LANES, np.int64)
    is_rope = np.zeros(LANES, bool)
    for e in range(2):
        for j in range(2 * nf):
            lane = e * HALF + j
            is_rope[lane], block[lane], freq[lane], second[lane] = True, j // nf, j % nf, e
            src[lane] = MLA_NOPE + (j // nf) * 2 * nf + e * nf + j % nf
    n_low = HALF - 2 * nf
    src[2 * nf:HALF] = np.arange(n_low)
    src[HALF + 2 * nf:HALF + 2 * nf + MLA_NOPE - n_low] = np.arange(n_low, MLA_NOPE)
    return src, is_rope, block, freq, second


def _gather_cols(w, src):
    return jnp.where(jnp.asarray(src >= 0), w[:, np.maximum(src, 0)], 0.0)


def _rope_tables(S):
    t = jnp.arange(S, dtype=jnp.int32)
    pos = jnp.stack([(t // GRID_W).astype(F32), (t % GRID_W).astype(F32)], axis=0)

    def table(nf, block, freq, second, is_rope):
        inv = ROPE_BASE ** (-jnp.arange(nf, dtype=F32) / nf)
        ang = pos[block].T * inv[freq]
        sign = jnp.asarray(2.0 * second - 1.0, F32)
        cos = jnp.where(jnp.asarray(is_rope), jnp.cos(ang), 1.0)
        sin = jnp.where(jnp.asarray(is_rope), jnp.sin(ang) * sign, 0.0)
        return cos, sin

    _, _, block, freq, second = _group_a_lane_order()
    cos_a, sin_a = table(HEAD_DIM // 4, block, freq, second, np.ones(LANES, bool))
    _, is_rope, block, freq, second = _group_b_lane_order()
    cos_b, sin_b = table(MLA_ROPE // 4, block, freq, second, is_rope)
    return cos_a, sin_a, cos_b, sin_b


def _w_in_columns():
    n_qa, n_ka = N_HEADS_A * HEAD_DIM, N_KV_A * HEAD_DIM
    a_end = n_qa + 2 * n_ka
    kv_at = a_end + MLA_Q_RANK
    kr_at = kv_at + MLA_KV_RANK
    c_at = kr_at + MLA_ROPE
    head, dim, _, _, _ = _group_a_lane_order()
    pair_cols = head * HEAD_DIM + dim
    src_b, is_rope, _, _, _ = _group_b_lane_order()
    cols = np.concatenate(
        [p * LANES + pair_cols for p in range(N_HEADS_A // 2)]
        + [n_qa + pair_cols, np.arange(n_qa + n_ka, a_end),
           np.arange(a_end, kv_at), np.full(CQ_PAD - MLA_Q_RANK, -1),
           np.arange(kv_at, kr_at),
           np.where(is_rope, kr_at + src_b - MLA_NOPE, -1),
           np.arange(c_at, c_at + 3 * N_HEADS_C * HALF)])
    assert cols.shape == (IN_COLS,)
    return cols


def _w_uq_columns():
    src_b = _group_b_lane_order()[0]
    d_qk = MLA_NOPE + MLA_ROPE
    return np.concatenate([np.where(src_b >= 0, h * d_qk + src_b, -1) for h in range(N_HEADS_B)])


def _w_ukv_columns():
    src_b, is_rope, _, _, _ = _group_b_lane_order()
    d_kv = MLA_NOPE + MLA_V
    lanes = np.arange(LANES)
    keys = [np.where(~is_rope & (src_b >= 0), h * d_kv + src_b, -1) for h in range(N_HEADS_B)]
    vals = [np.where(lanes // HALF == h % 2, h * d_kv + MLA_NOPE + lanes % HALF, -1)
            for h in range(N_HEADS_B)]
    return np.concatenate(keys + vals)


def _twice(v):
    return jnp.tile(v, 2).reshape(1, 2 * v.shape[0])


def kernel(x, norm_attn, w_in, q_norm_a, k_norm_a, q_a_norm_b, w_uq_b, kv_a_norm_b, w_ukv_b,
           lambda_q1_c, lambda_k1_c, lambda_q2_c, lambda_k2_c, subln_c, w_out,
           norm_ffn, w_up, conv_w, conv_b, w_down, final_norm):
    B, S, D = x.shape
    depth = w_in.shape[0]
    assert D == D_MODEL and S % max(TM_IN, TM_FFN, TQ, TK) == 0 and S % GRID_W == 0
    assert TM_IN == TK
    T = B * S

    tabs = _rope_tables(S)
    row = lambda v: v.reshape(1, -1)
    a_pairs = [(j, j + N_HEADS_A // 2) for j in range(N_HEADS_A // 2)]
    a_rows = np.concatenate([np.arange(h * HEAD_DIM, (h + 1) * HEAD_DIM)
                             for pair in a_pairs for h in pair])
    xc = x.reshape(T, D)
    for l in range(depth):
        lam_init = 0.8 - 0.6 * math.exp(-0.3 * l)
        wuq_p = jnp.pad(_gather_cols(w_uq_b[l], _w_uq_columns()),
                        ((0, CQ_PAD - MLA_Q_RANK), (0, 0))).astype(BF16)
        wukv_p = _gather_cols(w_ukv_b[l], _w_ukv_columns()).astype(BF16)

        qa, kta, va, qb, ktb, vb, qc, ktc, vc, nrm = _in_proj(
            xc, row(norm_attn[l]), _gather_cols(w_in[l], _w_in_columns()).astype(BF16), tabs,
            q_norm_a[l][_group_a_lane_order()[1]].reshape(1, LANES),
            k_norm_a[l][_group_a_lane_order()[1]].reshape(1, LANES),
            jnp.pad(q_a_norm_b[l], (0, CQ_PAD - MLA_Q_RANK)).reshape(1, CQ_PAD),
            row(kv_a_norm_b[l]), wuq_p, wukv_p, B, S)

        nrm = jnp.max(nrm.reshape(B, -1, LANES), axis=1)

        def bounded(q0, nq, q_rep, k0, nk, k_rep, per_pipeline):
            q2 = jnp.repeat(nrm[:, q0:q0 + nq], q_rep, axis=1)
            k2 = jnp.repeat(nrm[:, k0:k0 + nk], k_rep, axis=1)
            ok = q2 * k2 * NRM_MARGIN <= SAFE_LOG2_RANGE ** 2
            return jnp.all(ok.reshape(B, -1, per_pipeline), axis=-1).astype(jnp.int32)

        qk_a = (HEAD_DIM * jnp.max(jnp.abs(q_norm_a[l])) * jnp.max(jnp.abs(k_norm_a[l]))
                * (HEAD_DIM ** -0.5 * LOG2E))
        bounded_a = jnp.broadcast_to(
            (qk_a * NRM_MARGIN <= SAFE_LOG2_RANGE).astype(jnp.int32),
            (B, N_HEADS_A // HEADS_PER_PIPELINE_AB))

        group = N_HEADS_A // N_KV_A
        oa = _attention(
            functools.partial(_attn_ab_kernel, n_heads=N_HEADS_A,
                              per_pipeline=HEADS_PER_PIPELINE_AB, q_index=lambda h: h // 2,
                              kt_index=lambda h: 2 * (h // group) + h % 2,
                              v_index=lambda h: h // group, out_pairs=a_pairs, n_keys=S),
            bounded_a, N_HEADS_A, qa, kta, va, [], N_HEADS_A * HEAD_DIM, B, S)
        same = lambda h: h
        ob = _attention(
            functools.partial(_attn_ab_kernel, n_heads=N_HEADS_B,
                              per_pipeline=HEADS_PER_PIPELINE_AB, q_index=same, kt_index=same,
                              v_index=same,
                              out_pairs=[(2 * j, 2 * j + 1) for j in range(N_HEADS_B // 2)],
                              n_keys=S),
            bounded(NRM_QB, N_HEADS_B, 1, NRM_KB, N_HEADS_B, 1, HEADS_PER_PIPELINE_AB), N_HEADS_B,
            qb, ktb, vb, [], N_HEADS_B * MLA_V, B, S)
        oc = _attention(
            functools.partial(_attn_c_kernel, n_keys=S, lam_init=lam_init),
            bounded(NRM_QC, N_HEADS_C, 2, NRM_KC, 2 * N_HEADS_C, 1, HEADS_PER_PIPELINE),
            2 * N_HEADS_C, qc, ktc, vc,
            [row(lambda_q1_c[l]), row(lambda_k1_c[l]), row(lambda_q2_c[l]), row(lambda_k2_c[l]),
             _twice(subln_c[l])],
            N_HEADS_C * DIFF_V, B, S)

        w_out_l = w_out[l][np.concatenate([a_rows, np.arange(N_HEADS_A * HEAD_DIM, D_MODEL)])]
        xc = _mlp(xc, oa, ob, oc, w_out_l.astype(BF16), row(norm_ffn[l]), w_up[l].astype(BF16),
                  conv_w[l].reshape(3, 2, D_FF).transpose(1, 0, 2),
                  conv_b[l].reshape(2, 1, D_FF), w_down[l].astype(BF16),
                  row(final_norm), S, final_norm=(l == depth - 1))
    return xc.reshape(B, S, D)
```

```python
import functools
import math

import jax
import jax.numpy as jnp
import numpy as np
from jax import lax
from jax.experimental import pallas as pl
from jax.experimental.pallas import tpu as pltpu

D_MODEL = 1024
GRID_W = 64
HEAD_DIM = 64
N_HEADS_A = 8
N_KV_A = 2
N_HEADS_B = 4
MLA_Q_RANK = 192
MLA_KV_RANK = 128
MLA_NOPE = 64
MLA_ROPE = 32
MLA_V = 64
N_HEADS_C = 4
DIFF_QK = 32
DIFF_V = 64
D_FF = 2816
ROPE_BASE = 10000.0
EPS = 1e-6

LANES = 128
HALF = LANES // 2
A_QUARTER = LANES // 4
BF16_SUBLANES = 16
VMEM_LIMIT_BYTES = 56 * 1024 * 1024

LOG2E = math.log2(math.e)
assert HEAD_DIM == MLA_V == DIFF_V == 2 * DIFF_QK == HALF

OFF_QA = 0
OFF_KA = OFF_QA + N_HEADS_A * HEAD_DIM
OFF_VA = OFF_KA + LANES
OFF_CQ = OFF_VA + LANES
CQ_PAD = 2 * LANES
OFF_CKV = OFF_CQ + CQ_PAD
OFF_KR = OFF_CKV + MLA_KV_RANK
OFF_QC = OFF_KR + LANES
OFF_KC = OFF_QC + N_HEADS_C * HALF
OFF_VC = OFF_KC + N_HEADS_C * HALF
IN_COLS = OFF_VC + N_HEADS_C * HALF

NRM_QB = 0
NRM_KB = NRM_QB + N_HEADS_B
NRM_QC = NRM_KB + N_HEADS_B
NRM_KC = NRM_QC + N_HEADS_C
NRM_GROUPS = 2 * N_HEADS_B + N_HEADS_C
NRM_MARGIN = 1.03
SAFE_LOG2_RANGE = 55.0

F32 = jnp.float32
BF16 = jnp.bfloat16

TM_IN = 512
TM_FFN = 512
FFN_CHUNK = 256
FFN_DOWN_GROUP = 6 * FFN_CHUNK
FFN_DOWN_LAG = 2
TQ = 512
TQ_SUB_AB = 512
TQ_SUB_C = 256
TK = 512
SCORE_LOOKAHEAD = 2
HEADS_PER_PIPELINE = 4
HEADS_PER_PIPELINE_AB = 4


def _dot(a, b):
    return jnp.dot(a, b, preferred_element_type=F32)


def _rms(x, n):
    ms = jnp.sum(x * x, axis=-1, keepdims=True) * (1.0 / n)
    return x * lax.rsqrt(ms + EPS)


def _low_lanes():
    return lax.broadcasted_iota(jnp.int32, (1, LANES), 1) < HALF


def _half_rms(x, low=None):
    low = _low_lanes() if low is None else low
    sq = x * x
    s_lo = jnp.sum(jnp.where(low, sq, 0.0), axis=-1, keepdims=True)
    s_hi = jnp.sum(jnp.where(low, 0.0, sq), axis=-1, keepdims=True)
    return x * lax.rsqrt(jnp.where(low, s_lo, s_hi) * (1.0 / HALF) + EPS)


def _in_kernel(x_ref, gat_ref, w_ref, cos_a_ref, sin_a_ref, cos_b_ref, sin_b_ref,
               qg_a_ref, kg_a_ref, qg_b_ref, kvg_b_ref, wuq_ref, wukv_ref, sel_ref,
               qa_ref, kta_ref, va_ref, qb_ref, ktb_ref, vb_ref, qc_ref, ktc_ref, vc_ref, nrm_ref):
    x = x_ref[...]
    tm = x.shape[0]
    h = (_rms(x, D_MODEL) * gat_ref[...]).astype(BF16)

    lane = lax.broadcasted_iota(jnp.int32, (1, LANES), 1)
    low = lane < HALF
    row = lax.broadcasted_iota(jnp.int32, (LANES, 1), 0)

    z = _dot(h, w_ref[...])

    def seg(off, width):
        return z[:, off:off + width]

    def rope(y, c, s):
        return y * c + pltpu.roll(y, HALF, axis=1) * s

    def group(z, g):
        return z[:, g * LANES:(g + 1) * LANES]

    squares = []

    def note(y):
        squares.append((y * y).astype(BF16))

    def noted_row_sums(first):
        return _dot(jnp.concatenate(squares[first:], axis=-1),
                    sel_ref[first * LANES:len(squares) * LANES])

    cos_b, sin_b = cos_b_ref[...], sin_b_ref[...]
    cq =(_rms(seg(OFF_CQ, CQ_PAD), MLA_Q_RANK) * qg_b_ref[...]).astype(BF16)
    qb = _dot(cq, wuq_ref[...])
    scale_b = (MLA_NOPE + MLA_ROPE) ** -0.5 * LOG2E
    for hh in range(N_HEADS_B):
        y = rope(group(qb, hh), cos_b, sin_b) * scale_b
        qb_ref[0, hh] = y.astype(BF16)
        note(y)
    ckv = (_rms(seg(OFF_CKV, MLA_KV_RANK), MLA_KV_RANK) * kvg_b_ref[...]).astype(BF16)
    kvb = _dot(ckv, wukv_ref[...])
    kr = rope(seg(OFF_KR, LANES), cos_b, sin_b)
    for hh in range(N_HEADS_B):
        y = group(kvb, hh) + kr
        ktb_ref[0, hh] = y.T.astype(BF16)
        note(y)
        ones_other_half = jnp.where(low, float(hh % 2), float(1 - hh % 2))
        vb_ref[0, hh] = (group(kvb, N_HEADS_B + hh) + ones_other_half).astype(BF16)
    n2 = noted_row_sums(0)
    n_noted_b = len(squares)

    zq = seg(OFF_QC, N_HEADS_C * HALF)
    zk = seg(OFF_KC, N_HEADS_C * HALF)
    zv = seg(OFF_VC, N_HEADS_C * HALF)
    k_groups = []
    for p in range(N_HEADS_C // 2):
        y = group(zq, p) * (DIFF_QK ** -0.5 * LOG2E)
        qc_ref[0, p] = y.astype(BF16)
        note(y)
        k_groups.append(group(zk, p))
        kt = group(zk, p).T
        for sub in range(4):
            mine = (row >= sub * DIFF_QK) & (row < (sub + 1) * DIFF_QK)
            ktc_ref[0, 4 * p + sub, 0] = jnp.where(mine, kt, 0.0).astype(BF16)
        vz = group(zv, p)
        vc_ref[0, 2 * p] = jnp.where(low, vz, 1.0).astype(BF16)
        vc_ref[0, 2 * p + 1] = jnp.where(low, 1.0, vz).astype(BF16)
    for kz in k_groups:
        note(kz)
    n2 = n2 + noted_row_sums(n_noted_b)
    nrm_ref[0] = jnp.max(n2, axis=0, keepdims=True)

    cos_a, sin_a = cos_a_ref[...], sin_a_ref[...]
    first_head = (lane & A_QUARTER) == 0
    zq = seg(OFF_QA, N_HEADS_A * HEAD_DIM)
    for p in range(N_HEADS_A // 2):
        y = rope(_half_rms(group(zq, p), first_head) * qg_a_ref[...], cos_a, sin_a)
        qa_ref[0, p] = (y * (HEAD_DIM ** -0.5 * LOG2E)).astype(BF16)
    y = rope(_half_rms(seg(OFF_KA, LANES), first_head) * kg_a_ref[...], cos_a, sin_a)
    kt = y.T.astype(BF16)
    quarter = [kt[j * A_QUARTER:(j + 1) * A_QUARTER] for j in range(4)]
    zeros = jnp.zeros((A_QUARTER, tm), BF16)
    for g in range(N_KV_A):
        k1, k2 = quarter[g], quarter[2 + g]
        kta_ref[0, 2 * g] = jnp.concatenate([k1, zeros, k2, zeros], axis=0)
        kta_ref[0, 2 * g + 1] = jnp.concatenate([zeros, k1, zeros, k2], axis=0)
    zv = seg(OFF_VA, LANES)
    va_ref[0, 0] = jnp.where(low, zv, 1.0).astype(BF16)
    va_ref[0, 1] = jnp.where(low, 1.0, zv).astype(BF16)


def _norm_routing():
    sel = np.zeros((NRM_GROUPS, LANES, LANES), np.float32)
    g = 0
    for slot in ([NRM_QB + h for h in range(N_HEADS_B)] + [NRM_KB + h for h in range(N_HEADS_B)]):
        sel[g, :, slot] = 1.0
        g += 1
    for p in range(N_HEADS_C // 2):
        for f in range(2):
            sel[g, f * HALF:(f + 1) * HALF, NRM_QC + 2 * p + f] = 1.0
        g += 1
    for p in range(N_HEADS_C // 2):
        for sub in range(4):
            sel[g, sub * DIFF_QK:(sub + 1) * DIFF_QK, NRM_KC + 4 * p + sub] = 1.0
        g += 1
    assert g == NRM_GROUPS
    return jnp.asarray(sel.reshape(NRM_GROUPS * LANES, LANES), BF16)


def _in_proj(x2d, gat, w_in_r, tabs, qg_a, kg_a, qg_b, kvg_b, wuq_p, wukv_p, B, S):
    T = B * S
    tm = TM_IN
    nst = S // tm
    const = lambda shape: pl.BlockSpec(shape, lambda i: (0,) * len(shape))
    tab = pl.BlockSpec((tm, LANES), lambda i: (i % nst, 0))
    hm = lambda H: pl.BlockSpec((1, H, tm, LANES), lambda i: (i // nst, 0, i % nst, 0))
    hmt = lambda H: pl.BlockSpec((1, H, LANES, tm), lambda i: (i // nst, 0, 0, i % nst))
    sds = lambda H: jax.ShapeDtypeStruct((B, H, S, LANES), BF16)
    sdt = lambda H: jax.ShapeDtypeStruct((B, H, LANES, S), BF16)
    return pl.pallas_call(
        _in_kernel,
        grid=(T // tm,),
        in_specs=[pl.BlockSpec((tm, D_MODEL), lambda i: (i, 0)),
                  const((1, D_MODEL)), const((D_MODEL, IN_COLS)),
                  tab, tab, tab, tab,
                  const((1, LANES)), const((1, LANES)), const((1, CQ_PAD)), const((1, MLA_KV_RANK)),
                  const((CQ_PAD, N_HEADS_B * LANES)), const((MLA_KV_RANK, 2 * N_HEADS_B * LANES)),
                  const((NRM_GROUPS * LANES, LANES))],
        out_specs=[hm(N_HEADS_A // 2), hmt(2 * N_KV_A), hm(N_KV_A),
                   hm(N_HEADS_B), hmt(N_HEADS_B), hm(N_HEADS_B),
                   hm(N_HEADS_C // 2),
                   pl.BlockSpec((1, 2 * N_HEADS_C, 1, LANES, tm),
                                lambda i: (i // nst, 0, i % nst, 0, 0)),
                   hm(N_HEADS_C),
                   pl.BlockSpec((1, 1, LANES), lambda i: (i, 0, 0))],
        out_shape=[sds(N_HEADS_A // 2), sdt(2 * N_KV_A), sds(N_KV_A),
                   sds(N_HEADS_B), sdt(N_HEADS_B), sds(N_HEADS_B),
                   sds(N_HEADS_C // 2),
                   jax.ShapeDtypeStruct((B, 2 * N_HEADS_C, S // tm, LANES, tm), BF16),
                   sds(N_HEADS_C),
                   jax.ShapeDtypeStruct((T // tm, 1, LANES), F32)],
        compiler_params=pltpu.CompilerParams(dimension_semantics=("arbitrary",),
                                             vmem_limit_bytes=VMEM_LIMIT_BYTES),
        name="in_proj",
    )(x2d, gat, w_in_r, *tabs, qg_a, kg_a, qg_b, kvg_b, wuq_p, wukv_p, _norm_routing())


def _lane_group_max(s):
    m = s[:, :LANES]
    for j in range(1, s.shape[1] // LANES):
        m = jnp.maximum(m, s[:, j * LANES:(j + 1) * LANES])
    return m


def _pv_bounded(streams, n_chunks):
    items = [(r, i) for r in range(n_chunks) for i in range(len(streams))]
    accs = [None] * len(streams)
    in_flight = {}
    for k in range(len(items) + SCORE_LOOKAHEAD):
        if k < len(items):
            r, i = items[k]
            scores, _, row_off = streams[i]
            s, off = scores(r), row_off(r)
            in_flight[k] = s if off is None else s + off
        if k >= SCORE_LOOKAHEAD:
            r, i = items[k - SCORE_LOOKAHEAD]
            d = _dot(jnp.exp2(in_flight.pop(k - SCORE_LOOKAHEAD)).astype(BF16), streams[i][1](r))
            accs[i] = d if accs[i] is None else accs[i] + d
    return accs


def _pv_exact(streams, n_chunks, s_ref):
    rows = s_ref.shape[0] // len(streams)
    maxes = []
    for i, (scores, _, row_off) in enumerate(streams):
        m_run = None
        for r in range(n_chunks):
            s = scores(r)
            s_ref[i * rows:(i + 1) * rows, r * TK:(r + 1) * TK] = s
            m, off = _lane_group_max(s), row_off(r)
            if off is not None:
                m = m + off
            m_run = m if m_run is None else jnp.maximum(m_run, m)
        maxes.append(jnp.max(m_run, axis=-1, keepdims=True))
    accs = []
    for i, (_, values, row_off) in enumerate(streams):
        acc = None
        for r in range(n_chunks):
            off = row_off(r)
            shift = maxes[i] if off is None else maxes[i] - off
            p = jnp.exp2(s_ref[i * rows:(i + 1) * rows, r * TK:(r + 1) * TK] - shift)
            d = _dot(p.astype(BF16), values(r))
            acc = d if acc is None else acc + d
        accs.append(acc)
    return accs


def _run_heads(bounded, heads, head_streams, n_chunks, s_ref, acc_ref, roll_exact=False):
    @pl.when(bounded)
    def _():
        shared = {}
        per_head = [head_streams(h, shared) for h in heads]
        n_sub = len(per_head[0])
        accs = _pv_bounded([s for streams in per_head for s in streams], n_chunks)
        for k, h in enumerate(heads):
            acc_ref[h] = jnp.concatenate(accs[k * n_sub:(k + 1) * n_sub], axis=0)

    @pl.when(jnp.logical_not(bounded))
    def _():
        def one_head(h):
            acc_ref[h] = jnp.concatenate(_pv_exact(head_streams(h, {}), n_chunks, s_ref), axis=0)

        if roll_exact:
            lax.fori_loop(0, len(heads), lambda k, c: (one_head(heads[0] + k), c)[1], 0)
        else:
            for h in heads:
                one_head(h)


def _normalised_pair(acc_low, acc_high):
    low = _low_lanes()
    num = jnp.where(low, acc_low, acc_high)
    den = pltpu.roll(jnp.where(low, acc_high, acc_low), HALF, axis=1)
    return num / den


def _attn_ab_kernel(bounded_ref, q_ref, kt_ref, v_ref, o_ref, s_ref, acc_ref, *,
                    n_heads, per_pipeline, q_index, kt_index, v_index, out_pairs, n_keys):
    n_chunks = n_keys // TK

    def head_streams(h, shared):
        q = q_ref[0, q_index(h)]
        return [(lambda r, qs=q[r0:r0 + TQ_SUB_AB]:
                 _dot(qs, kt_ref[0, kt_index(h), :, r * TK:(r + 1) * TK]),
                 lambda r: v_ref[0, v_index(h), r * TK:(r + 1) * TK, :],
                 lambda r: None) for r0 in range(0, TQ, TQ_SUB_AB)]

    def head_group(j, carry):
        heads = [per_pipeline * j + k for k in range(per_pipeline)]
        _run_heads(bounded_ref[pl.program_id(0), j] != 0, heads, head_streams, n_chunks, s_ref,
                   acc_ref, roll_exact=True)
        return carry

    if n_heads == per_pipeline:
        head_group(0, None)
    else:
        lax.fori_loop(0, n_heads // per_pipeline, head_group, 0)
    o_ref[...] = jnp.concatenate([_normalised_pair(acc_ref[a], acc_ref[b]) for a, b in out_pairs],
                                 axis=-1).astype(BF16)


def _attn_c_kernel(bounded_ref, q_ref, kt_ref, v_ref, lq1_ref, lk1_ref, lq2_ref, lk2_ref,
                   subln_ref, o_ref, s_ref, acc_ref, *, n_keys, lam_init):
    assert TQ == TK
    nc = n_keys // TK
    cd = pl.program_id(1)
    col = lax.broadcasted_iota(jnp.int32, (1, TK), 1).astype(F32)
    row = lax.broadcasted_iota(jnp.int32, (TQ_SUB_C, 1), 0).astype(F32)
    subs = [(r, r + TQ_SUB_C) for r in range(0, TQ, TQ_SUB_C)]
    neg_dist = [-jnp.abs((lax.broadcasted_iota(jnp.int32, (TQ_SUB_C, TK), 0) + r0
                          - lax.broadcasted_iota(jnp.int32, (TQ_SUB_C, TK), 1)).astype(F32))
                for r0, _ in subs]

    chunk = [(cd + r) % nc for r in range(nc)]

    def head_bias(hh):
        slope = jnp.float32(LOG2E * 2.0 ** (-8.0 * N_HEADS_C / N_HEADS_C))
        for k in range(N_HEADS_C - 2, -1, -1):
            slope = jnp.where(hh == k, LOG2E * 2.0 ** (-8.0 * (k + 1) / N_HEADS_C), slope)
        key_term, row_off = [None], [[None] * len(subs)]
        for r in range(1, nc):
            side = jnp.where(chunk[r] < cd, slope, -slope)
            key_term.append(col * side)
            base = -slope * (jnp.abs(chunk[r] - cd) * TK).astype(F32)
            row_off.append([base - side * (row + float(r0)) for r0, _ in subs])
        return slope, key_term, row_off

    def head_pair(p, carry):
        def streams_of(k, shared):
            idx, hh = 4 * p + k, 2 * p + k // 2
            if isinstance(k, int):
                if k // 2 not in shared:
                    shared[k // 2] = head_bias(hh)
                slope, key_term, row_off = shared[k // 2]
            else:
                slope, key_term, row_off = head_bias(hh)

            def scores(r, si):
                s = _dot(q_ref[0, p][subs[si][0]:subs[si][1]], kt_ref[0, idx, chunk[r]])
                return s + (neg_dist[si] * slope if r == 0 else key_term[r])

            return [(functools.partial(scores, si=si),
                     lambda r: v_ref[0, hh, pl.ds(pl.multiple_of(chunk[r] * TK, TK), TK), :],
                     lambda r, si=si: row_off[r][si]) for si in range(len(subs))]

        _run_heads(bounded_ref[pl.program_id(0), p] != 0, list(range(4)), streams_of, nc, s_ref,
                   acc_ref.at[pl.ds(4 * p, 4)], roll_exact=True)
        return carry

    assert HEADS_PER_PIPELINE == 4
    lax.fori_loop(0, N_HEADS_C // 2, head_pair, 0)

    lam = (jnp.exp(jnp.sum(lq1_ref[...] * lk1_ref[...], axis=-1, keepdims=True))
           - jnp.exp(jnp.sum(lq2_ref[...] * lk2_ref[...], axis=-1, keepdims=True)) + lam_init)
    outs = []
    for p in range(N_HEADS_C // 2):
        o = (_normalised_pair(acc_ref[4 * p], acc_ref[4 * p + 2])
             - lam * _normalised_pair(acc_ref[4 * p + 1], acc_ref[4 * p + 3]))
        outs.append(_half_rms(o) * subln_ref[...] * (1.0 - lam_init))
    o_ref[...] = jnp.concatenate(outs, axis=-1).astype(BF16)


def _attention(kernel_fn, bounded, n_acc, q, kt, v, extras, out_width, B, S):
    nq = S // TQ
    hq, hv = q.shape[1], v.shape[1]
    extra_specs = [pl.BlockSpec(e.shape, lambda b, i, f, n=e.ndim: (0,) * n) for e in extras]
    return pl.pallas_call(
        kernel_fn,
        grid_spec=pltpu.PrefetchScalarGridSpec(
            num_scalar_prefetch=1,
            grid=(B, nq),
            in_specs=[pl.BlockSpec((1, hq, TQ, LANES), lambda b, i, f: (b, 0, i, 0)),
                      pl.BlockSpec((1,) + kt.shape[1:],
                                   lambda b, i, f, n=kt.ndim: (b,) + (0,) * (n - 1)),
                      pl.BlockSpec((1, hv, S, LANES), lambda b, i, f: (b, 0, 0, 0))] + extra_specs,
            out_specs=pl.BlockSpec((TQ, out_width), lambda b, i, f: (b * nq + i, 0)),
            scratch_shapes=[pltpu.VMEM((TQ, S), F32),
                            pltpu.VMEM((n_acc, TQ, LANES), F32)]),
        out_shape=jax.ShapeDtypeStruct((B * S, out_width), BF16),
        compiler_params=pltpu.CompilerParams(dimension_semantics=("arbitrary", "arbitrary"),
                                             vmem_limit_bytes=VMEM_LIMIT_BYTES),
        name=kernel_fn.func.__name__.strip("_"),
    )(bounded, q, kt, v, *extras)


def _mlp_kernel(xp_ref, x_ref, xn_ref, ap_ref, a_ref, an_ref, bp_ref, b_ref, bn_ref,
                cp_ref, c_ref, cn_ref, wo_ref, g_ref, wu_ref, cw_ref, cb_ref, wd_ref, fg_ref,
                o_ref, act_ref, *, tiles_per_seq, final_norm):
    i = pl.program_id(0)
    tm = x_ref.shape[0]
    halo = xp_ref.shape[0]

    def rows(p, m, n):
        return jnp.concatenate([p[...], m[...], n[...]], axis=0)

    o_ext = jnp.concatenate([rows(ap_ref, a_ref, an_ref), rows(bp_ref, b_ref, bn_ref),
                             rows(cp_ref, c_ref, cn_ref)], axis=1)
    x1 = rows(xp_ref, x_ref, xn_ref) + _dot(o_ext, wo_ref[...])
    h = (_rms(x1, D_MODEL) * g_ref[...]).astype(BF16)
    at_start = (i % tiles_per_seq) == 0
    at_end = (i % tiles_per_seq) == tiles_per_seq - 1
    zeros = jnp.zeros((halo, D_MODEL), BF16)
    hext = jnp.concatenate([jnp.where(at_start, zeros, h[:halo]), h[halo:halo + tm],
                            jnp.where(at_end, zeros, h[halo + tm:])], axis=0)

    def branch(k, c0):
        u = _dot(hext, wu_ref[:, k * D_FF + c0:k * D_FF + c0 + FFN_CHUNK])
        cw = cw_ref[k, :, c0:c0 + FFN_CHUNK]
        n = u.shape[0]
        prev = pltpu.roll(u, 1, axis=0)[halo:halo + tm]
        nxt = pltpu.roll(u, n - 1, axis=0)[halo:halo + tm]
        return (prev * cw[0:1] + u[halo:halo + tm] * cw[1:2] + nxt * cw[2:3]
                + cb_ref[k, :, c0:c0 + FFN_CHUNK])

    y = x1[halo:halo + tm]
    done = 0
    for c0 in range(0, D_FF, FFN_CHUNK):
        g = branch(0, c0)
        val = branch(1, c0)
        half = 0.5 * g
        act_ref[:, c0:c0 + FFN_CHUNK] = ((half + half * jnp.tanh(half)) * val).astype(BF16)
        ready = c0 + FFN_CHUNK - FFN_DOWN_LAG * FFN_CHUNK
        if ready - done >= FFN_DOWN_GROUP:
            y = y + _dot(act_ref[:, done:done + FFN_DOWN_GROUP], wd_ref[done:done + FFN_DOWN_GROUP])
            done += FFN_DOWN_GROUP
    y = y + _dot(act_ref[:, done:], wd_ref[done:])
    if final_norm:
        y = _rms(y, D_MODEL) * fg_ref[...]
    o_ref[...] = y


def _mlp(x2d, oa, ob, oc, w_out, g_ffn, w_up, conv_w, conv_b, w_down, final_g, S, final_norm):
    T = x2d.shape[0]
    tm, halo = TM_FFN, BF16_SUBLANES
    per = tm // halo
    last = T // halo - 1
    resident = lambda shape: pl.BlockSpec(shape, lambda i: (0,) * len(shape),
                                          pipeline_mode=pl.Buffered(1))

    def with_halos(width):
        return [pl.BlockSpec((halo, width), lambda i: (jnp.maximum(i * per - 1, 0), 0)),
                pl.BlockSpec((tm, width), lambda i: (i, 0)),
                pl.BlockSpec((halo, width), lambda i: (jnp.minimum((i + 1) * per, last), 0))]

    return pl.pallas_call(
        functools.partial(_mlp_kernel, tiles_per_seq=S // tm, final_norm=final_norm),
        grid=(T // tm,),
        in_specs=(with_halos(D_MODEL) + with_halos(oa.shape[1]) + with_halos(ob.shape[1])
                  + with_halos(oc.shape[1])
                  + [resident(w_out.shape), resident((1, D_MODEL)),
                     resident((D_MODEL, 2 * D_FF)), resident((2, 3, D_FF)), resident((2, 1, D_FF)),
                     resident((D_FF, D_MODEL)), resident((1, D_MODEL))]),
        out_specs=pl.BlockSpec((tm, D_MODEL), lambda i: (i, 0)),
        out_shape=jax.ShapeDtypeStruct((T, D_MODEL), F32),
        scratch_shapes=[pltpu.VMEM((tm, D_FF), BF16)],
        compiler_params=pltpu.CompilerParams(dimension_semantics=("arbitrary",),
                                             vmem_limit_bytes=VMEM_LIMIT_BYTES),
        name="out_proj_conv_mlp",
    )(x2d, x2d, x2d, oa, oa, oa, ob, ob, ob, oc, oc, oc, w_out, g_ffn,
      w_up, conv_w, conv_b, w_down, final_g)


def _group_a_lane_order():
    nf = HEAD_DIM // 4
    lanes = np.arange(LANES)
    second, r = lanes // HALF, lanes % HALF
    head, rr = r // A_QUARTER, r % A_QUARTER
    block, freq = rr // nf, rr % nf
    dim = block * 2 * nf + second * nf + freq
    return head, dim, block, freq, second


def _group_b_lane_order():
    nf = MLA_ROPE // 4
    src = np.full(LANES, -1)
    block = np.zeros(LANES, np.int64)
    freq = np.zeros(LANES, np.int64)
    second = np.zeros(LANES, np.int64)
    is_rope = np.zeros(LANES, bool)
    for e in range(2):
        for j in range(2 * nf):
            lane = e * HALF + j
            is_rope[lane], block[lane], freq[lane], second[lane] = True, j // nf, j % nf, e
            src[lane] = MLA_NOPE + (j // nf) * 2 * nf + e * nf + j % nf
    n_low = HALF - 2 * nf
    src[2 * nf:HALF] = np.arange(n_low)
    src[HALF + 2 * nf:HALF + 2 * nf + MLA_NOPE - n_low] = np.arange(n_low, MLA_NOPE)
    return src, is_rope, block, freq, second


def _gather_cols(w, src):
    return jnp.where(jnp.asarray(src >= 0), w[:, np.maximum(src, 0)], 0.0)


def _rope_tables(S):
    t = jnp.arange(S, dtype=jnp.int32)
    pos = jnp.stack([(t // GRID_W).astype(F32), (t % GRID_W).astype(F32)], axis=0)

    def table(nf, block, freq, second, is_rope):
        inv = ROPE_BASE ** (-jnp.arange(nf, dtype=F32) / nf)
        ang = (pos[:, :, None] * inv).transpose(1, 0, 2).reshape(S, 2 * nf)
        lane_src = block * nf + freq
        sign = jnp.asarray(2.0 * second - 1.0, F32)
        cos = jnp.where(jnp.asarray(is_rope), jnp.cos(ang)[:, lane_src], 1.0)
        sin = jnp.where(jnp.asarray(is_rope), jnp.sin(ang)[:, lane_src] * sign, 0.0)
        return cos, sin

    _, _, block, freq, second = _group_a_lane_order()
    cos_a, sin_a = table(HEAD_DIM // 4, block, freq, second, np.ones(LANES, bool))
    _, is_rope, block, freq, second = _group_b_lane_order()
    cos_b, sin_b = table(MLA_ROPE // 4, block, freq, second, is_rope)
    return cos_a, sin_a, cos_b, sin_b


def _w_in_columns():
    n_qa, n_ka = N_HEADS_A * HEAD_DIM, N_KV_A * HEAD_DIM
    a_end = n_qa + 2 * n_ka
    kv_at = a_end + MLA_Q_RANK
    kr_at = kv_at + MLA_KV_RANK
    c_at = kr_at + MLA_ROPE
    head, dim, _, _, _ = _group_a_lane_order()
    pair_cols = head * HEAD_DIM + dim
    src_b, is_rope, _, _, _ = _group_b_lane_order()
    cols = np.concatenate(
        [p * LANES + pair_cols for p in range(N_HEADS_A // 2)]
        + [n_qa + pair_cols, np.arange(n_qa + n_ka, a_end),
           np.arange(a_end, kv_at), np.full(CQ_PAD - MLA_Q_RANK, -1),
           np.arange(kv_at, kr_at),
           np.where(is_rope, kr_at + src_b - MLA_NOPE, -1),
           np.arange(c_at, c_at + 3 * N_HEADS_C * HALF)])
    assert cols.shape == (IN_COLS,)
    return cols


def _w_uq_columns():
    src_b = _group_b_lane_order()[0]
    d_qk = MLA_NOPE + MLA_ROPE
    return np.concatenate([np.where(src_b >= 0, h * d_qk + src_b, -1) for h in range(N_HEADS_B)])


def _w_ukv_columns():
    src_b, is_rope, _, _, _ = _group_b_lane_order()
    d_kv = MLA_NOPE + MLA_V
    lanes = np.arange(LANES)
    keys = [np.where(~is_rope & (src_b >= 0), h * d_kv + src_b, -1) for h in range(N_HEADS_B)]
    vals = [np.where(lanes // HALF == h % 2, h * d_kv + MLA_NOPE + lanes % HALF, -1)
            for h in range(N_HEADS_B)]
    return np.concatenate(keys + vals)


def _twice(v):
    return jnp.tile(v, 2).reshape(1, 2 * v.shape[0])


def kernel(x, norm_attn, w_in, q_norm_a, k_norm_a, q_a_norm_b, w_uq_b, kv_a_norm_b, w_ukv_b,
           lambda_q1_c, lambda_k1_c, lambda_q2_c, lambda_k2_c, subln_c, w_out,
           norm_ffn, w_up, conv_w, conv_b, w_down, final_norm):
    B, S, D = x.shape
    depth = w_in.shape[0]
    assert D == D_MODEL and S % max(TM_IN, TM_FFN, TQ, TK) == 0 and S % GRID_W == 0
    assert TM_IN == TK
    T = B * S

    tabs = _rope_tables(S)
    row = lambda v: v.reshape(1, -1)
    a_pairs = [(j, j + N_HEADS_A // 2) for j in range(N_HEADS_A // 2)]
    a_rows = np.concatenate([np.arange(h * HEAD_DIM, (h + 1) * HEAD_DIM)
                             for pair in a_pairs for h in pair])
    xc = x.reshape(T, D)
    for l in range(depth):
        lam_init = 0.8 - 0.6 * math.exp(-0.3 * l)
        wuq_p = jnp.pad(_gather_cols(w_uq_b[l], _w_uq_columns()),
                        ((0, CQ_PAD - MLA_Q_RANK), (0, 0))).astype(BF16)
        wukv_p = _gather_cols(w_ukv_b[l], _w_ukv_columns()).astype(BF16)

        qa, kta, va, qb, ktb, vb, qc, ktc, vc, nrm = _in_proj(
            xc, row(norm_attn[l]), _gather_cols(w_in[l], _w_in_columns()).astype(BF16), tabs,
            q_norm_a[l][_group_a_lane_order()[1]].reshape(1, LANES),
            k_norm_a[l][_group_a_lane_order()[1]].reshape(1, LANES),
            jnp.pad(q_a_norm_b[l], (0, CQ_PAD - MLA_Q_RANK)).reshape(1, CQ_PAD),
            row(kv_a_norm_b[l]), wuq_p, wukv_p, B, S)

        nrm = jnp.max(nrm.reshape(B, -1, LANES), axis=1)

        def bounded(q0, nq, q_rep, k0, nk, k_rep, per_pipeline):
            q2 = jnp.repeat(nrm[:, q0:q0 + nq], q_rep, axis=1)
            k2 = jnp.repeat(nrm[:, k0:k0 + nk], k_rep, axis=1)
            ok = q2 * k2 * NRM_MARGIN <= SAFE_LOG2_RANGE ** 2
            return jnp.all(ok.reshape(B, -1, per_pipeline), axis=-1).astype(jnp.int32)

        qk_a = (HEAD_DIM * jnp.max(jnp.abs(q_norm_a[l])) * jnp.max(jnp.abs(k_norm_a[l]))
                * (HEAD_DIM ** -0.5 * LOG2E))
        bounded_a = jnp.broadcast_to(
            (qk_a * NRM_MARGIN <= SAFE_LOG2_RANGE).astype(jnp.int32),
            (B, N_HEADS_A // HEADS_PER_PIPELINE_AB))

        group = N_HEADS_A // N_KV_A
        oa = _attention(
            functools.partial(_attn_ab_kernel, n_heads=N_HEADS_A,
                              per_pipeline=HEADS_PER_PIPELINE_AB, q_index=lambda h: h // 2,
                              kt_index=lambda h: 2 * (h // group) + h % 2,
                              v_index=lambda h: h // group, out_pairs=a_pairs, n_keys=S),
            bounded_a, N_HEADS_A, qa, kta, va, [], N_HEADS_A * HEAD_DIM, B, S)
        same = lambda h: h
        ob = _attention(
            functools.partial(_attn_ab_kernel, n_heads=N_HEADS_B,
                              per_pipeline=HEADS_PER_PIPELINE_AB, q_index=same, kt_index=same,
                              v_index=same,
                              out_pairs=[(2 * j, 2 * j + 1) for j in range(N_HEADS_B // 2)],
                              n_keys=S),
            bounded(NRM_QB, N_HEADS_B, 1, NRM_KB, N_HEADS_B, 1, HEADS_PER_PIPELINE_AB), N_HEADS_B,
            qb, ktb, vb, [], N_HEADS_B * MLA_V, B, S)
        oc = _attention(
            functools.partial(_attn_c_kernel, n_keys=S, lam_init=lam_init),
            bounded(NRM_QC, N_HEADS_C, 2, NRM_KC, 2 * N_HEADS_C, 1, HEADS_PER_PIPELINE),
            2 * N_HEADS_C, qc, ktc, vc,
            [row(lambda_q1_c[l]), row(lambda_k1_c[l]), row(lambda_q2_c[l]), row(lambda_k2_c[l]),
             _twice(subln_c[l])],
            N_HEADS_C * DIFF_V, B, S)

        w_out_l = w_out[l][np.concatenate([a_rows, np.arange(N_HEADS_A * HEAD_DIM, D_MODEL)])]
        xc = _mlp(xc, oa, ob, oc, w_out_l.astype(BF16), row(norm_ffn[l]), w_up[l].astype(BF16),
                  conv_w[l].reshape(3, 2, D_FF).transpose(1, 0, 2),
                  conv_b[l].reshape(2, 1, D_FF), w_down[l].astype(BF16),
                  row(final_norm), S, final_norm=(l == depth - 1))
    return xc.reshape(B, S, D)
```

```python
import functools
import math

import jax
import jax.numpy as jnp
import numpy as np
from jax import lax
from jax.experimental import pallas as pl
from jax.experimental.pallas import tpu as pltpu

D_MODEL = 1024
GRID_W = 64
HEAD_DIM = 64
N_HEADS_A = 8
N_KV_A = 2
N_HEADS_B = 4
MLA_Q_RANK = 192
MLA_KV_RANK = 128
MLA_NOPE = 64
MLA_ROPE = 32
MLA_V = 64
N_HEADS_C = 4
DIFF_QK = 32
DIFF_V = 64
D_FF = 2816
ROPE_BASE = 10000.0
EPS = 1e-6

LANES = 128
HALF = LANES // 2
A_QUARTER = LANES // 4
BF16_SUBLANES = 16
VMEM_LIMIT_BYTES = 56 * 1024 * 1024

LOG2E = math.log2(math.e)
assert HEAD_DIM == MLA_V == DIFF_V == 2 * DIFF_QK == HALF

OFF_QA = 0
OFF_KA = OFF_QA + N_HEADS_A * HEAD_DIM
OFF_VA = OFF_KA + LANES
OFF_CQ = OFF_VA + LANES
CQ_PAD = 2 * LANES
OFF_CKV = OFF_CQ + CQ_PAD
OFF_KR = OFF_CKV + MLA_KV_RANK
OFF_QC = OFF_KR + LANES
OFF_KC = OFF_QC + N_HEADS_C * HALF
OFF_VC = OFF_KC + N_HEADS_C * HALF
IN_COLS = OFF_VC + N_HEADS_C * HALF

NRM_QB = 0
NRM_KB = NRM_QB + N_HEADS_B
NRM_QC = NRM_KB + N_HEADS_B
NRM_KC = NRM_QC + N_HEADS_C
NRM_GROUPS = 2 * N_HEADS_B + N_HEADS_C
NRM_MARGIN = 1.03
SAFE_LOG2_RANGE = 55.0

F32 = jnp.float32
BF16 = jnp.bfloat16

TM_IN = 512
TM_FFN = 512
FFN_CHUNK = 256
FFN_DOWN_GROUP = 6 * FFN_CHUNK
FFN_DOWN_LAG = 2
TQ = 512
TQ_SUB_AB = 512
TQ_SUB_C = 256
TK = 512
SCORE_LOOKAHEAD = 2
HEADS_PER_PIPELINE = 4
HEADS_PER_PIPELINE_AB = 4


def _dot(a, b):
    return jnp.dot(a, b, preferred_element_type=F32)


def _rms(x, n):
    ms = jnp.sum(x * x, axis=-1, keepdims=True) * (1.0 / n)
    return x * lax.rsqrt(ms + EPS)


def _low_lanes():
    return lax.broadcasted_iota(jnp.int32, (1, LANES), 1) < HALF


def _half_rms(x, low=None):
    low = _low_lanes() if low is None else low
    sq = x * x
    s_lo = jnp.sum(jnp.where(low, sq, 0.0), axis=-1, keepdims=True)
    s_hi = jnp.sum(jnp.where(low, 0.0, sq), axis=-1, keepdims=True)
    return x * lax.rsqrt(jnp.where(low, s_lo, s_hi) * (1.0 / HALF) + EPS)


def _in_kernel(x_ref, gat_ref, w_ref, cos_a_ref, sin_a_ref, cos_b_ref, sin_b_ref,
               qg_a_ref, kg_a_ref, qg_b_ref, kvg_b_ref, wuq_ref, wukv_ref, sel_ref,
               qa_ref, kta_ref, va_ref, qb_ref, ktb_ref, vb_ref, qc_ref, ktc_ref, vc_ref, nrm_ref):
    x = x_ref[...]
    tm = x.shape[0]
    h = (_rms(x, D_MODEL) * gat_ref[...]).astype(BF16)

    lane = lax.broadcasted_iota(jnp.int32, (1, LANES), 1)
    low = lane < HALF
    row = lax.broadcasted_iota(jnp.int32, (LANES, 1), 0)

    z = _dot(h, w_ref[...])

    def seg(off, width):
        return z[:, off:off + width]

    def rope(y, c, s):
        return y * c + pltpu.roll(y, HALF, axis=1) * s

    def group(z, g):
        return z[:, g * LANES:(g + 1) * LANES]

    squares = []

    def note(y):
        squares.append((y * y).astype(BF16))

    def noted_row_sums(first):
        return _dot(jnp.concatenate(squares[first:], axis=-1),
                    sel_ref[first * LANES:len(squares) * LANES])

    cos_b, sin_b = cos_b_ref[...], sin_b_ref[...]
    cq =(_rms(seg(OFF_CQ, CQ_PAD), MLA_Q_RANK) * qg_b_ref[...]).astype(BF16)
    qb = _dot(cq, wuq_ref[...])
    scale_b = (MLA_NOPE + MLA_ROPE) ** -0.5 * LOG2E
    for hh in range(N_HEADS_B):
        y = rope(group(qb, hh), cos_b, sin_b) * scale_b
        qb_ref[0, hh] = y.astype(BF16)
        note(y)
    ckv = (_rms(seg(OFF_CKV, MLA_KV_RANK), MLA_KV_RANK) * kvg_b_ref[...]).astype(BF16)
    kvb = _dot(ckv, wukv_ref[...])
    kr = rope(seg(OFF_KR, LANES), cos_b, sin_b)
    for hh in range(N_HEADS_B):
        y = group(kvb, hh) + kr
        ktb_ref[0, hh] = y.T.astype(BF16)
        note(y)
        ones_other_half = jnp.where(low, float(hh % 2), float(1 - hh % 2))
        vb_ref[0, hh] = (group(kvb, N_HEADS_B + hh) + ones_other_half).astype(BF16)
    n2 = noted_row_sums(0)
    n_noted_b = len(squares)

    zq = seg(OFF_QC, N_HEADS_C * HALF)
    zk = seg(OFF_KC, N_HEADS_C * HALF)
    zv = seg(OFF_VC, N_HEADS_C * HALF)
    k_groups = []
    for p in range(N_HEADS_C // 2):
        y = group(zq, p) * (DIFF_QK ** -0.5 * LOG2E)
        qc_ref[0, p] = y.astype(BF16)
        note(y)
        k_groups.append(group(zk, p))
        kt = group(zk, p).T
        for sub in range(4):
            mine = (row >= sub * DIFF_QK) & (row < (sub + 1) * DIFF_QK)
            ktc_ref[0, 4 * p + sub, 0] = jnp.where(mine, kt, 0.0).astype(BF16)
        vz = group(zv, p)
        vc_ref[0, 2 * p] = jnp.where(low, vz, 1.0).astype(BF16)
        vc_ref[0, 2 * p + 1] = jnp.where(low, 1.0, vz).astype(BF16)
    for kz in k_groups:
        note(kz)
    n2 = n2 + noted_row_sums(n_noted_b)
    nrm_ref[0] = jnp.max(n2, axis=0, keepdims=True)

    cos_a, sin_a = cos_a_ref[...], sin_a_ref[...]
    first_head = (lane & A_QUARTER) == 0
    zq = seg(OFF_QA, N_HEADS_A * HEAD_DIM)
    for p in range(N_HEADS_A // 2):
        y = rope(_half_rms(group(zq, p), first_head) * qg_a_ref[...], cos_a, sin_a)
        qa_ref[0, p] = (y * (HEAD_DIM ** -0.5 * LOG2E)).astype(BF16)
    y = rope(_half_rms(seg(OFF_KA, LANES), first_head) * kg_a_ref[...], cos_a, sin_a)
    kt = y.T.astype(BF16)
    quarter = [kt[j * A_QUARTER:(j + 1) * A_QUARTER] for j in range(4)]
    zeros = jnp.zeros((A_QUARTER, tm), BF16)
    for g in range(N_KV_A):
        k1, k2 = quarter[g], quarter[2 + g]
        kta_ref[0, 2 * g] = jnp.concatenate([k1, zeros, k2, zeros], axis=0)
        kta_ref[0, 2 * g + 1] = jnp.concatenate([zeros, k1, zeros, k2], axis=0)
    zv = seg(OFF_VA, LANES)
    va_ref[0, 0] = jnp.where(low, zv, 1.0).astype(BF16)
    va_ref[0, 1] = jnp.where(low, 1.0, zv).astype(BF16)


def _norm_routing():
    sel = np.zeros((NRM_GROUPS, LANES, LANES), np.float32)
    g = 0
    for slot in ([NRM_QB + h for h in range(N_HEADS_B)] + [NRM_KB + h for h in range(N_HEADS_B)]):
        sel[g, :, slot] = 1.0
        g += 1
    for p in range(N_HEADS_C // 2):
        for f in range(2):
            sel[g, f * HALF:(f + 1) * HALF, NRM_QC + 2 * p + f] = 1.0
        g += 1
    for p in range(N_HEADS_C // 2):
        for sub in range(4):
            sel[g, sub * DIFF_QK:(sub + 1) * DIFF_QK, NRM_KC + 4 * p + sub] = 1.0
        g += 1
    assert g == NRM_GROUPS
    return jnp.asarray(sel.reshape(NRM_GROUPS * LANES, LANES), BF16)


def _in_proj(x2d, gat, w_in_r, tabs, qg_a, kg_a, qg_b, kvg_b, wuq_p, wukv_p, B, S):
    T = B * S
    tm = TM_IN
    nst = S // tm
    const = lambda shape: pl.BlockSpec(shape, lambda i: (0,) * len(shape))
    tab = pl.BlockSpec((tm, LANES), lambda i: (i % nst, 0))
    hm = lambda H: pl.BlockSpec((1, H, tm, LANES), lambda i: (i // nst, 0, i % nst, 0))
    hmt = lambda H: pl.BlockSpec((1, H, LANES, tm), lambda i: (i // nst, 0, 0, i % nst))
    sds = lambda H: jax.ShapeDtypeStruct((B, H, S, LANES), BF16)
    sdt = lambda H: jax.ShapeDtypeStruct((B, H, LANES, S), BF16)
    return pl.pallas_call(
        _in_kernel,
        grid=(T // tm,),
        in_specs=[pl.BlockSpec((tm, D_MODEL), lambda i: (i, 0)),
                  const((1, D_MODEL)), const((D_MODEL, IN_COLS)),
                  tab, tab, tab, tab,
                  const((1, LANES)), const((1, LANES)), const((1, CQ_PAD)), const((1, MLA_KV_RANK)),
                  const((CQ_PAD, N_HEADS_B * LANES)), const((MLA_KV_RANK, 2 * N_HEADS_B * LANES)),
                  const((NRM_GROUPS * LANES, LANES))],
        out_specs=[hm(N_HEADS_A // 2), hmt(2 * N_KV_A), hm(N_KV_A),
                   hm(N_HEADS_B), hmt(N_HEADS_B), hm(N_HEADS_B),
                   hm(N_HEADS_C // 2),
                   pl.BlockSpec((1, 2 * N_HEADS_C, 1, LANES, tm),
                                lambda i: (i // nst, 0, i % nst, 0, 0)),
                   hm(N_HEADS_C),
                   pl.BlockSpec((1, 1, LANES), lambda i: (i, 0, 0))],
        out_shape=[sds(N_HEADS_A // 2), sdt(2 * N_KV_A), sds(N_KV_A),
                   sds(N_HEADS_B), sdt(N_HEADS_B), sds(N_HEADS_B),
                   sds(N_HEADS_C // 2),
                   jax.ShapeDtypeStruct((B, 2 * N_HEADS_C, S // tm, LANES, tm), BF16),
                   sds(N_HEADS_C),
                   jax.ShapeDtypeStruct((T // tm, 1, LANES), F32)],
        compiler_params=pltpu.CompilerParams(dimension_semantics=("arbitrary",),
                                             vmem_limit_bytes=VMEM_LIMIT_BYTES),
        name="in_proj",
    )(x2d, gat, w_in_r, *tabs, qg_a, kg_a, qg_b, kvg_b, wuq_p, wukv_p, _norm_routing())


def _lane_group_max(s):
    m = s[:, :LANES]
    for j in range(1, s.shape[1] // LANES):
        m = jnp.maximum(m, s[:, j * LANES:(j + 1) * LANES])
    return m


def _pv_bounded(streams, n_chunks):
    items = [(r, i) for r in range(n_chunks) for i in range(len(streams))]
    accs = [None] * len(streams)
    in_flight = {}
    for k in range(len(items) + SCORE_LOOKAHEAD):
        if k < len(items):
            r, i = items[k]
            scores, _, row_off = streams[i]
            s, off = scores(r), row_off(r)
            in_flight[k] = s if off is None else s + off
        if k >= SCORE_LOOKAHEAD:
            r, i = items[k - SCORE_LOOKAHEAD]
            d = _dot(jnp.exp2(in_flight.pop(k - SCORE_LOOKAHEAD)).astype(BF16), streams[i][1](r))
            accs[i] = d if accs[i] is None else accs[i] + d
    return accs


def _pv_exact(streams, n_chunks, s_ref):
    rows = s_ref.shape[0] // len(streams)
    maxes = []
    for i, (scores, _, row_off) in enumerate(streams):
        m_run = None
        for r in range(n_chunks):
            s = scores(r)
            s_ref[i * rows:(i + 1) * rows, r * TK:(r + 1) * TK] = s
            m, off = _lane_group_max(s), row_off(r)
            if off is not None:
                m = m + off
            m_run = m if m_run is None else jnp.maximum(m_run, m)
        maxes.append(jnp.max(m_run, axis=-1, keepdims=True))
    accs = []
    for i, (_, values, row_off) in enumerate(streams):
        acc = None
        for r in range(n_chunks):
            off = row_off(r)
            shift = maxes[i] if off is None else maxes[i] - off
            p = jnp.exp2(s_ref[i * rows:(i + 1) * rows, r * TK:(r + 1) * TK] - shift)
            d = _dot(p.astype(BF16), values(r))
            acc = d if acc is None else acc + d
        accs.append(acc)
    return accs


def _run_heads(bounded, heads, head_streams, n_chunks, s_ref, acc_ref, roll_exact=False):
    @pl.when(bounded)
    def _():
        shared = {}
        per_head = [head_streams(h, shared) for h in heads]
        n_sub = len(per_head[0])
        accs = _pv_bounded([s for streams in per_head for s in streams], n_chunks)
        for k, h in enumerate(heads):
            acc_ref[h] = jnp.concatenate(accs[k * n_sub:(k + 1) * n_sub], axis=0)

    @pl.when(jnp.logical_not(bounded))
    def _():
        def one_head(h):
            acc_ref[h] = jnp.concatenate(_pv_exact(head_streams(h, {}), n_chunks, s_ref), axis=0)

        if roll_exact:
            lax.fori_loop(0, len(heads), lambda k, c: (one_head(heads[0] + k), c)[1], 0)
        else:
            for h in heads:
                one_head(h)


def _normalised_pair(acc_low, acc_high):
    low = _low_lanes()
    num = jnp.where(low, acc_low, acc_high)
    den = pltpu.roll(jnp.where(low, acc_high, acc_low), HALF, axis=1)
    return num / den


def _attn_ab_kernel(bounded_ref, q_ref, kt_ref, v_ref, o_ref, s_ref, acc_ref, *,
                    n_heads, per_pipeline, q_index, kt_index, v_index, out_pairs, n_keys):
    n_chunks = n_keys // TK

    def head_streams(h, shared):
        q = q_ref[0, q_index(h)]
        return [(lambda r, qs=q[r0:r0 + TQ_SUB_AB]:
                 _dot(qs, kt_ref[0, kt_index(h), :, r * TK:(r + 1) * TK]),
                 lambda r: v_ref[0, v_index(h), r * TK:(r + 1) * TK, :],
                 lambda r: None) for r0 in range(0, TQ, TQ_SUB_AB)]

    def head_group(j, carry):
        heads = [per_pipeline * j + k for k in range(per_pipeline)]
        _run_heads(bounded_ref[pl.program_id(0), j] != 0, heads, head_streams, n_chunks, s_ref,
                   acc_ref, roll_exact=True)
        return carry

    if n_heads == per_pipeline:
        head_group(0, None)
    else:
        lax.fori_loop(0, n_heads // per_pipeline, head_group, 0)
    o_ref[...] = jnp.concatenate([_normalised_pair(acc_ref[a], acc_ref[b]) for a, b in out_pairs],
                                 axis=-1).astype(BF16)


def _attn_c_kernel(bounded_ref, q_ref, kt_ref, v_ref, lq1_ref, lk1_ref, lq2_ref, lk2_ref,
                   subln_ref, o_ref, s_ref, acc_ref, *, n_keys, lam_init):
    assert TQ == TK
    nc = n_keys // TK
    cd = pl.program_id(1)
    col = lax.broadcasted_iota(jnp.int32, (1, TK), 1).astype(F32)
    row = lax.broadcasted_iota(jnp.int32, (TQ_SUB_C, 1), 0).astype(F32)
    subs = [(r, r + TQ_SUB_C) for r in range(0, TQ, TQ_SUB_C)]
    neg_dist = [-jnp.abs((lax.broadcasted_iota(jnp.int32, (TQ_SUB_C, TK), 0) + r0
                          - lax.broadcasted_iota(jnp.int32, (TQ_SUB_C, TK), 1)).astype(F32))
                for r0, _ in subs]

    chunk = [(cd + r) % nc for r in range(nc)]

    def head_bias(hh):
        slope = jnp.float32(LOG2E * 2.0 ** (-8.0 * N_HEADS_C / N_HEADS_C))
        for k in range(N_HEADS_C - 2, -1, -1):
            slope = jnp.where(hh == k, LOG2E * 2.0 ** (-8.0 * (k + 1) / N_HEADS_C), slope)
        key_term, row_off = [None], [[None] * len(subs)]
        for r in range(1, nc):
            side = jnp.where(chunk[r] < cd, slope, -slope)
            key_term.append(col * side)
            base = -slope * (jnp.abs(chunk[r] - cd) * TK).astype(F32)
            row_off.append([base - side * (row + float(r0)) for r0, _ in subs])
        return slope, key_term, row_off

    def head_pair(p, carry):
        def streams_of(k, shared):
            idx, hh = 4 * p + k, 2 * p + k // 2
            if isinstance(k, int):
                if k // 2 not in shared:
                    shared[k // 2] = head_bias(hh)
                slope, key_term, row_off = shared[k // 2]
            else:
                slope, key_term, row_off = head_bias(hh)

            def scores(r, si):
                s = _dot(q_ref[0, p][subs[si][0]:subs[si][1]], kt_ref[0, idx, chunk[r]])
                return s + (neg_dist[si] * slope if r == 0 else key_term[r])

            return [(functools.partial(scores, si=si),
                     lambda r: v_ref[0, hh, pl.ds(pl.multiple_of(chunk[r] * TK, TK), TK), :],
                     lambda r, si=si: row_off[r][si]) for si in range(len(subs))]

        _run_heads(bounded_ref[pl.program_id(0), p] != 0, list(range(4)), streams_of, nc, s_ref,
                   acc_ref.at[pl.ds(4 * p, 4)], roll_exact=True)
        return carry

    assert HEADS_PER_PIPELINE == 4
    lax.fori_loop(0, N_HEADS_C // 2, head_pair, 0)

    lam = (jnp.exp(jnp.sum(lq1_ref[...] * lk1_ref[...], axis=-1, keepdims=True))
           - jnp.exp(jnp.sum(lq2_ref[...] * lk2_ref[...], axis=-1, keepdims=True)) + lam_init)
    outs = []
    for p in range(N_HEADS_C // 2):
        o = (_normalised_pair(acc_ref[4 * p], acc_ref[4 * p + 2])
             - lam * _normalised_pair(acc_ref[4 * p + 1], acc_ref[4 * p + 3]))
        outs.append(_half_rms(o) * subln_ref[...] * (1.0 - lam_init))
    o_ref[...] = jnp.concatenate(outs, axis=-1).astype(BF16)


def _attention(kernel_fn, bounded, n_acc, q, kt, v, extras, out_width, B, S):
    nq = S // TQ
    hq, hv = q.shape[1], v.shape[1]
    extra_specs = [pl.BlockSpec(e.shape, lambda b, i, f, n=e.ndim: (0,) * n) for e in extras]
    return pl.pallas_call(
        kernel_fn,
        grid_spec=pltpu.PrefetchScalarGridSpec(
            num_scalar_prefetch=1,
            grid=(B, nq),
            in_specs=[pl.BlockSpec((1, hq, TQ, LANES), lambda b, i, f: (b, 0, i, 0)),
                      pl.BlockSpec((1,) + kt.shape[1:],
                                   lambda b, i, f, n=kt.ndim: (b,) + (0,) * (n - 1)),
                      pl.BlockSpec((1, hv, S, LANES), lambda b, i, f: (b, 0, 0, 0))] + extra_specs,
            out_specs=pl.BlockSpec((TQ, out_width), lambda b, i, f: (b * nq + i, 0)),
            scratch_shapes=[pltpu.VMEM((TQ, S), F32),
                            pltpu.VMEM((n_acc, TQ, LANES), F32)]),
        out_shape=jax.ShapeDtypeStruct((B * S, out_width), BF16),
        compiler_params=pltpu.CompilerParams(dimension_semantics=("arbitrary", "arbitrary"),
                                             vmem_limit_bytes=VMEM_LIMIT_BYTES),
        name=kernel_fn.func.__name__.strip("_"),
    )(bounded, q, kt, v, *extras)


def _mlp_kernel(xp_ref, x_ref, xn_ref, ap_ref, a_ref, an_ref, bp_ref, b_ref, bn_ref,
                cp_ref, c_ref, cn_ref, wo_ref, g_ref, wu_ref, cw_ref, cb_ref, wd_ref, fg_ref,
                o_ref, act_ref, *, tiles_per_seq, final_norm):
    i = pl.program_id(0)
    tm = x_ref.shape[0]
    halo = xp_ref.shape[0]

    def rows(p, m, n):
        return jnp.concatenate([p[...], m[...], n[...]], axis=0)

    o_ext = jnp.concatenate([rows(ap_ref, a_ref, an_ref), rows(bp_ref, b_ref, bn_ref),
                             rows(cp_ref, c_ref, cn_ref)], axis=1)
    x1 = rows(xp_ref, x_ref, xn_ref) + _dot(o_ext, wo_ref[...])
    h = (_rms(x1, D_MODEL) * g_ref[...]).astype(BF16)
    at_start = (i % tiles_per_seq) == 0
    at_end = (i % tiles_per_seq) == tiles_per_seq - 1
    zeros = jnp.zeros((halo, D_MODEL), BF16)
    hext = jnp.concatenate([jnp.where(at_start, zeros, h[:halo]), h[halo:halo + tm],
                            jnp.where(at_end, zeros, h[halo + tm:])], axis=0)

    def branch(k, c0):
        u = _dot(hext, wu_ref[:, k * D_FF + c0:k * D_FF + c0 + FFN_CHUNK])
        cw = cw_ref[k, :, c0:c0 + FFN_CHUNK]
        n = u.shape[0]
        prev = pltpu.roll(u, 1, axis=0)[halo:halo + tm]
        nxt = pltpu.roll(u, n - 1, axis=0)[halo:halo + tm]
        return (prev * cw[0:1] + u[halo:halo + tm] * cw[1:2] + nxt * cw[2:3]
                + cb_ref[k, :, c0:c0 + FFN_CHUNK])

    y = x1[halo:halo + tm]
    done = 0
    for c0 in range(0, D_FF, FFN_CHUNK):
        g = branch(0, c0)
        val = branch(1, c0)
        half = 0.5 * g
        act_ref[:, c0:c0 + FFN_CHUNK] = ((half + half * jnp.tanh(half)) * val).astype(BF16)
        ready = c0 + FFN_CHUNK - FFN_DOWN_LAG * FFN_CHUNK
        if ready - done >= FFN_DOWN_GROUP:
            y = y + _dot(act_ref[:, done:done + FFN_DOWN_GROUP], wd_ref[done:done + FFN_DOWN_GROUP])
            done += FFN_DOWN_GROUP
    y = y + _dot(act_ref[:, done:], wd_ref[done:])
    if final_norm:
        y = _rms(y, D_MODEL) * fg_ref[...]
    o_ref[...] = y


def _mlp(x2d, oa, ob, oc, w_out, g_ffn, w_up, conv_w, conv_b, w_down, final_g, S, final_norm):
    T = x2d.shape[0]
    tm, halo = TM_FFN, BF16_SUBLANES
    per = tm // halo
    last = T // halo - 1
    resident = lambda shape: pl.BlockSpec(shape, lambda i: (0,) * len(shape),
                                          pipeline_mode=pl.Buffered(1))

    def with_halos(width):
        return [pl.BlockSpec((halo, width), lambda i: (jnp.maximum(i * per - 1, 0), 0)),
                pl.BlockSpec((tm, width), lambda i: (i, 0)),
                pl.BlockSpec((halo, width), lambda i: (jnp.minimum((i + 1) * per, last), 0))]

    return pl.pallas_call(
        functools.partial(_mlp_kernel, tiles_per_seq=S // tm, final_norm=final_norm),
        grid=(T // tm,),
        in_specs=(with_halos(D_MODEL) + with_halos(oa.shape[1]) + with_halos(ob.shape[1])
                  + with_halos(oc.shape[1])
                  + [resident(w_out.shape), resident((1, D_MODEL)),
                     resident((D_MODEL, 2 * D_FF)), resident((2, 3, D_FF)), resident((2, 1, D_FF)),
                     resident((D_FF, D_MODEL)), resident((1, D_MODEL))]),
        out_specs=pl.BlockSpec((tm, D_MODEL), lambda i: (i, 0)),
        out_shape=jax.ShapeDtypeStruct((T, D_MODEL), F32),
        scratch_shapes=[pltpu.VMEM((tm, D_FF), BF16)],
        compiler_params=pltpu.CompilerParams(dimension_semantics=("arbitrary",),
                                             vmem_limit_bytes=VMEM_LIMIT_BYTES),
        name="out_proj_conv_mlp",
    )(x2d, x2d, x2d, oa, oa, oa, ob, ob, ob, oc, oc, oc, w_out, g_ffn,
      w_up, conv_w, conv_b, w_down, final_g)


def _group_a_lane_order():
    nf = HEAD_DIM // 4
    lanes = np.arange(LANES)
    second, r = lanes // HALF, lanes % HALF
    head, rr = r // A_QUARTER, r % A_QUARTER
    block, freq = rr // nf, rr % nf
    dim = block * 2 * nf + second * nf + freq
    return head, dim, block, freq, second


def _group_b_lane_order():
    nf = MLA_ROPE // 4
    src = np.full(LANES, -1)
    block = np.zeros(LANES, np.int64)
    freq = np.zeros(LANES, np.int64)
    second = np.zeros(LANES, np.int64)
    is_rope = np.zeros(LANES, bool)
    for e in range(2):
        for j in range(2 * nf):
            lane = e * HALF + j
            is_rope[lane], block[lane], freq[lane], second[lane] = True, j // nf, j % nf, e
            src[lane] = MLA_NOPE + (j // nf) * 2 * nf + e * nf + j % nf
    n_low = HALF - 2 * nf
    src[2 * nf:HALF] = np.arange(n_low)
    src[HALF + 2 * nf:HALF + 2 * nf + MLA_NOPE - n_low] = np.arange(n_low, MLA_NOPE)
    return src, is_rope, block, freq, second


def _gather_cols(w, src):
    return jnp.where(jnp.asarray(src >= 0), w[:, np.maximum(src, 0)], 0.0)


def _rope_tables(S):
    t = jnp.arange(S, dtype=jnp.int32)
    pos = jnp.stack([(t // GRID_W).astype(F32), (t % GRID_W).astype(F32)], axis=0)

    def table(nf, block, freq, second, is_rope):
        inv = ROPE_BASE ** (-jnp.arange(nf, dtype=F32) / nf)
        ang = (pos[:, :, None] * inv).transpose(1, 0, 2).reshape(S, 2 * nf)
        lane_src = block * nf + freq
        sign = jnp.asarray(2.0 * second - 1.0, F32)
        cos = jnp.where(jnp.asarray(is_rope), jnp.cos(ang)[:, lane_src], 1.0)
        sin = jnp.where(jnp.asarray(is_rope), jnp.sin(ang)[:, lane_src] * sign, 0.0)
        return cos, sin

    _, _, block, freq, second = _group_a_lane_order()
    cos_a, sin_a = table(HEAD_DIM // 4, block, freq, second, np.ones(LANES, bool))
    _, is_rope, block, freq, second = _group_b_lane_order()
    cos_b, sin_b = table(MLA_ROPE // 4, block, freq, second, is_rope)
    return cos_a, sin_a, cos_b, sin_b


def _w_in_columns():
    n_qa, n_ka = N_HEADS_A * HEAD_DIM, N_KV_A * HEAD_DIM
    a_end = n_qa + 2 * n_ka
    kv_at = a_end + MLA_Q_RANK
    kr_at = kv_at + MLA_KV_RANK
    c_at = kr_at + MLA_ROPE
    head, dim, _, _, _ = _group_a_lane_order()
    pair_cols = head * HEAD_DIM + dim
    src_b, is_rope, _, _, _ = _group_b_lane_order()
    cols = np.concatenate(
        [p * LANES + pair_cols for p in range(N_HEADS_A // 2)]
        + [n_qa + pair_cols, np.arange(n_qa + n_ka, a_end),
           np.arange(a_end, kv_at), np.full(CQ_PAD - MLA_Q_RANK, -1),
           np.arange(kv_at, kr_at),
           np.where(is_rope, kr_at + src_b - MLA_NOPE, -1),
           np.arange(c_at, c_at + 3 * N_HEADS_C * HALF)])
    assert cols.shape == (IN_COLS,)
    return cols


def _w_uq_columns():
    src_b = _group_b_lane_order()[0]
    d_qk = MLA_NOPE + MLA_ROPE
    return np.concatenate([np.where(src_b >= 0, h * d_qk + src_b, -1) for h in range(N_HEADS_B)])


def _w_ukv_columns():
    src_b, is_rope, _, _, _ = _group_b_lane_order()
    d_kv = MLA_NOPE + MLA_V
    lanes = np.arange(LANES)
    keys = [np.where(~is_rope & (src_b >= 0), h * d_kv + src_b, -1) for h in range(N_HEADS_B)]
    vals = [np.where(lanes // HALF == h % 2, h * d_kv + MLA_NOPE + lanes % HALF, -1)
            for h in range(N_HEADS_B)]
    return np.concatenate(keys + vals)


def _twice(v):
    return jnp.tile(v, 2).reshape(1, 2 * v.shape[0])


def kernel(x, norm_attn, w_in, q_norm_a, k_norm_a, q_a_norm_b, w_uq_b, kv_a_norm_b, w_ukv_b,
           lambda_q1_c, lambda_k1_c, lambda_q2_c, lambda_k2_c, subln_c, w_out,
           norm_ffn, w_up, conv_w, conv_b, w_down, final_norm):
    B, S, D = x.shape
    depth = w_in.shape[0]
    assert D == D_MODEL and S % max(TM_IN, TM_FFN, TQ, TK) == 0 and S % GRID_W == 0
    assert TM_IN == TK
    T = B * S

    tabs = _rope_tables(S)
    row = lambda v: v.reshape(1, -1)
    a_pairs = [(j, j + N_HEADS_A // 2) for j in range(N_HEADS_A // 2)]
    a_rows = np.concatenate([np.arange(h * HEAD_DIM, (h + 1) * HEAD_DIM)
                             for pair in a_pairs for h in pair])
    xc = x.reshape(T, D)
    w_up_bf, w_down_bf = w_up.astype(BF16), w_down.astype(BF16)
    for l in range(depth):
        lam_init = 0.8 - 0.6 * math.exp(-0.3 * l)
        wuq_p = jnp.pad(_gather_cols(w_uq_b[l], _w_uq_columns()),
                        ((0, CQ_PAD - MLA_Q_RANK), (0, 0))).astype(BF16)
        wukv_p = _gather_cols(w_ukv_b[l], _w_ukv_columns()).astype(BF16)

        qa, kta, va, qb, ktb, vb, qc, ktc, vc, nrm = _in_proj(
            xc, row(norm_attn[l]), _gather_cols(w_in[l], _w_in_columns()).astype(BF16), tabs,
            q_norm_a[l][_group_a_lane_order()[1]].reshape(1, LANES),
            k_norm_a[l][_group_a_lane_order()[1]].reshape(1, LANES),
            jnp.pad(q_a_norm_b[l], (0, CQ_PAD - MLA_Q_RANK)).reshape(1, CQ_PAD),
            row(kv_a_norm_b[l]), wuq_p, wukv_p, B, S)

        nrm = jnp.max(nrm.reshape(B, -1, LANES), axis=1)

        def bounded(q0, nq, q_rep, k0, nk, k_rep, per_pipeline):
            q2 = jnp.repeat(nrm[:, q0:q0 + nq], q_rep, axis=1)
            k2 = jnp.repeat(nrm[:, k0:k0 + nk], k_rep, axis=1)
            ok = q2 * k2 * NRM_MARGIN <= SAFE_LOG2_RANGE ** 2
            return jnp.all(ok.reshape(B, -1, per_pipeline), axis=-1).astype(jnp.int32)

        qk_a = (HEAD_DIM * jnp.max(jnp.abs(q_norm_a[l])) * jnp.max(jnp.abs(k_norm_a[l]))
                * (HEAD_DIM ** -0.5 * LOG2E))
        bounded_a = jnp.broadcast_to(
            (qk_a * NRM_MARGIN <= SAFE_LOG2_RANGE).astype(jnp.int32),
            (B, N_HEADS_A // HEADS_PER_PIPELINE_AB))

        group = N_HEADS_A // N_KV_A
        oa = _attention(
            functools.partial(_attn_ab_kernel, n_heads=N_HEADS_A,
                              per_pipeline=HEADS_PER_PIPELINE_AB, q_index=lambda h: h // 2,
                              kt_index=lambda h: 2 * (h // group) + h % 2,
                              v_index=lambda h: h // group, out_pairs=a_pairs, n_keys=S),
            bounded_a, N_HEADS_A, qa, kta, va, [], N_HEADS_A * HEAD_DIM, B, S)
        same = lambda h: h
        ob = _attention(
            functools.partial(_attn_ab_kernel, n_heads=N_HEADS_B,
                              per_pipeline=HEADS_PER_PIPELINE_AB, q_index=same, kt_index=same,
                              v_index=same,
                              out_pairs=[(2 * j, 2 * j + 1) for j in range(N_HEADS_B // 2)],
                              n_keys=S),
            bounded(NRM_QB, N_HEADS_B, 1, NRM_KB, N_HEADS_B, 1, HEADS_PER_PIPELINE_AB), N_HEADS_B,
            qb, ktb, vb, [], N_HEADS_B * MLA_V, B, S)
        oc = _attention(
            functools.partial(_attn_c_kernel, n_keys=S, lam_init=lam_init),
            bounded(NRM_QC, N_HEADS_C, 2, NRM_KC, 2 * N_HEADS_C, 1, HEADS_PER_PIPELINE),
            2 * N_HEADS_C, qc, ktc, vc,
            [row(lambda_q1_c[l]), row(lambda_k1_c[l]), row(lambda_q2_c[l]), row(lambda_k2_c[l]),
             _twice(subln_c[l])],
            N_HEADS_C * DIFF_V, B, S)

        w_out_l = w_out[l][np.concatenate([a_rows, np.arange(N_HEADS_A * HEAD_DIM, D_MODEL)])]
        xc = _mlp(xc, oa, ob, oc, w_out_l.astype(BF16), row(norm_ffn[l]), w_up_bf[l],
                  conv_w[l].reshape(3, 2, D_FF).transpose(1, 0, 2),
                  conv_b[l].reshape(2, 1, D_FF), w_down_bf[l],
                  row(final_norm), S, final_norm=(l == depth - 1))
    return xc.reshape(B, S, D)
```

```python
import functools
import math

import jax
import jax.numpy as jnp
import numpy as np
from jax import lax
from jax.experimental import pallas as pl
from jax.experimental.pallas import tpu as pltpu

D_MODEL = 1024
GRID_W = 64
HEAD_DIM = 64
N_HEADS_A = 8
N_KV_A = 2
N_HEADS_B = 4
MLA_Q_RANK = 192
MLA_KV_RANK = 128
MLA_NOPE = 64
MLA_ROPE = 32
MLA_V = 64
N_HEADS_C = 4
DIFF_QK = 32
DIFF_V = 64
D_FF = 2816
ROPE_BASE = 10000.0
EPS = 1e-6

LANES = 128
HALF = LANES // 2
A_QUARTER = LANES // 4
BF16_SUBLANES = 16
VMEM_LIMIT_BYTES = 56 * 1024 * 1024

LOG2E = math.log2(math.e)
assert HEAD_DIM == MLA_V == DIFF_V == 2 * DIFF_QK == HALF

OFF_QA = 0
OFF_KA = OFF_QA + N_HEADS_A * HEAD_DIM
OFF_VA = OFF_KA + LANES
OFF_CQ = OFF_VA + LANES
CQ_PAD = 2 * LANES
OFF_CKV = OFF_CQ + CQ_PAD
OFF_KR = OFF_CKV + MLA_KV_RANK
OFF_QC = OFF_KR + LANES
OFF_KC = OFF_QC + N_HEADS_C * HALF
OFF_VC = OFF_KC + N_HEADS_C * HALF
IN_COLS = OFF_VC + N_HEADS_C * HALF

NRM_QB = 0
NRM_KB = NRM_QB + N_HEADS_B
NRM_QC = NRM_KB + N_HEADS_B
NRM_KC = NRM_QC + N_HEADS_C
NRM_GROUPS = 2 * N_HEADS_B + N_HEADS_C
NRM_MARGIN = 1.03
SAFE_LOG2_RANGE = 55.0

F32 = jnp.float32
BF16 = jnp.bfloat16

TM_IN = 512
TM_FFN = 512
FFN_CHUNK = 256
FFN_DOWN_GROUP = 6 * FFN_CHUNK
FFN_DOWN_LAG = 2
TQ = 512
TQ_SUB_AB = 512
TQ_SUB_C = 256
TK = 512
SCORE_LOOKAHEAD = 2
HEADS_PER_PIPELINE = 4
HEADS_PER_PIPELINE_AB = 4


def _dot(a, b):
    return jnp.dot(a, b, preferred_element_type=F32)


def _rms(x, n):
    ms = jnp.sum(x * x, axis=-1, keepdims=True) * (1.0 / n)
    return x * lax.rsqrt(ms + EPS)


def _low_lanes():
    return lax.broadcasted_iota(jnp.int32, (1, LANES), 1) < HALF


def _half_rms(x, low=None):
    low = _low_lanes() if low is None else low
    sq = x * x
    s_lo = jnp.sum(jnp.where(low, sq, 0.0), axis=-1, keepdims=True)
    s_hi = jnp.sum(jnp.where(low, 0.0, sq), axis=-1, keepdims=True)
    return x * lax.rsqrt(jnp.where(low, s_lo, s_hi) * (1.0 / HALF) + EPS)


def _in_kernel(x_ref, gat_ref, w_ref, cos_a_ref, sin_a_ref, cos_b_ref, sin_b_ref,
               qg_a_ref, kg_a_ref, qg_b_ref, kvg_b_ref, wuq_ref, wukv_ref, sel_ref,
               qa_ref, kta_ref, va_ref, qb_ref, ktb_ref, vb_ref, qc_ref, ktc_ref, vc_ref, nrm_ref):
    x = x_ref[...]
    tm = x.shape[0]
    h = (_rms(x, D_MODEL) * gat_ref[...]).astype(BF16)

    lane = lax.broadcasted_iota(jnp.int32, (1, LANES), 1)
    low = lane < HALF
    row = lax.broadcasted_iota(jnp.int32, (LANES, 1), 0)

    z = _dot(h, w_ref[...])

    def seg(off, width):
        return z[:, off:off + width]

    def rope(y, c, s):
        return y * c + pltpu.roll(y, HALF, axis=1) * s

    def group(z, g):
        return z[:, g * LANES:(g + 1) * LANES]

    squares = []

    def note(y):
        squares.append((y * y).astype(BF16))

    def noted_row_sums(first):
        return _dot(jnp.concatenate(squares[first:], axis=-1),
                    sel_ref[first * LANES:len(squares) * LANES])

    cos_b, sin_b = cos_b_ref[...], sin_b_ref[...]
    cq =(_rms(seg(OFF_CQ, CQ_PAD), MLA_Q_RANK) * qg_b_ref[...]).astype(BF16)
    qb = _dot(cq, wuq_ref[...])
    scale_b = (MLA_NOPE + MLA_ROPE) ** -0.5 * LOG2E
    for hh in range(N_HEADS_B):
        y = rope(group(qb, hh), cos_b, sin_b) * scale_b
        qb_ref[0, hh] = y.astype(BF16)
        note(y)
    ckv = (_rms(seg(OFF_CKV, MLA_KV_RANK), MLA_KV_RANK) * kvg_b_ref[...]).astype(BF16)
    kvb = _dot(ckv, wukv_ref[...])
    kr = rope(seg(OFF_KR, LANES), cos_b, sin_b)
    for hh in range(N_HEADS_B):
        y = group(kvb, hh) + kr
        ktb_ref[0, hh] = y.T.astype(BF16)
        note(y)
        ones_other_half = jnp.where(low, float(hh % 2), float(1 - hh % 2))
        vb_ref[0, hh] = (group(kvb, N_HEADS_B + hh) + ones_other_half).astype(BF16)
    n2 = noted_row_sums(0)
    n_noted_b = len(squares)

    zq = seg(OFF_QC, N_HEADS_C * HALF)
    zk = seg(OFF_KC, N_HEADS_C * HALF)
    zv = seg(OFF_VC, N_HEADS_C * HALF)
    k_groups = []
    for p in range(N_HEADS_C // 2):
        y = group(zq, p) * (DIFF_QK ** -0.5 * LOG2E)
        qc_ref[0, p] = y.astype(BF16)
        note(y)
        k_groups.append(group(zk, p))
        kt = group(zk, p).T
        for sub in range(4):
            mine = (row >= sub * DIFF_QK) & (row < (sub + 1) * DIFF_QK)
            ktc_ref[0, 4 * p + sub, 0] = jnp.where(mine, kt, 0.0).astype(BF16)
        vz = group(zv, p)
        vc_ref[0, 2 * p] = jnp.where(low, vz, 1.0).astype(BF16)
        vc_ref[0, 2 * p + 1] = jnp.where(low, 1.0, vz).astype(BF16)
    for kz in k_groups:
        note(kz)
    n2 = n2 + noted_row_sums(n_noted_b)
    nrm_ref[0] = jnp.max(n2, axis=0, keepdims=True)

    cos_a, sin_a = cos_a_ref[...], sin_a_ref[...]
    first_head = (lane & A_QUARTER) == 0
    zq = seg(OFF_QA, N_HEADS_A * HEAD_DIM)
    for p in range(N_HEADS_A // 2):
        y = rope(_half_rms(group(zq, p), first_head) * qg_a_ref[...], cos_a, sin_a)
        qa_ref[0, p] = (y * (HEAD_DIM ** -0.5 * LOG2E)).astype(BF16)
    y = rope(_half_rms(seg(OFF_KA, LANES), first_head) * kg_a_ref[...], cos_a, sin_a)
    kt = y.T.astype(BF16)
    quarter = [kt[j * A_QUARTER:(j + 1) * A_QUARTER] for j in range(4)]
    zeros = jnp.zeros((A_QUARTER, tm), BF16)
    for g in range(N_KV_A):
        k1, k2 = quarter[g], quarter[2 + g]
        kta_ref[0, 2 * g] = jnp.concatenate([k1, zeros, k2, zeros], axis=0)
        kta_ref[0, 2 * g + 1] = jnp.concatenate([zeros, k1, zeros, k2], axis=0)
    zv = seg(OFF_VA, LANES)
    va_ref[0, 0] = jnp.where(low, zv, 1.0).astype(BF16)
    va_ref[0, 1] = jnp.where(low, 1.0, zv).astype(BF16)


def _norm_routing():
    sel = np.zeros((NRM_GROUPS, LANES, LANES), np.float32)
    g = 0
    for slot in ([NRM_QB + h for h in range(N_HEADS_B)] + [NRM_KB + h for h in range(N_HEADS_B)]):
        sel[g, :, slot] = 1.0
        g += 1
    for p in range(N_HEADS_C // 2):
        for f in range(2):
            sel[g, f * HALF:(f + 1) * HALF, NRM_QC + 2 * p + f] = 1.0
        g += 1
    for p in range(N_HEADS_C // 2):
        for sub in range(4):
            sel[g, sub * DIFF_QK:(sub + 1) * DIFF_QK, NRM_KC + 4 * p + sub] = 1.0
        g += 1
    assert g == NRM_GROUPS
    return jnp.asarray(sel.reshape(NRM_GROUPS * LANES, LANES), BF16)


def _in_proj(x2d, gat, w_in_r, tabs, qg_a, kg_a, qg_b, kvg_b, wuq_p, wukv_p, B, S):
    T = B * S
    tm = TM_IN
    nst = S // tm
    const = lambda shape: pl.BlockSpec(shape, lambda i: (0,) * len(shape))
    tab = pl.BlockSpec((tm, LANES), lambda i: (i % nst, 0))
    hm = lambda H: pl.BlockSpec((1, H, tm, LANES), lambda i: (i // nst, 0, i % nst, 0))
    hmt = lambda H: pl.BlockSpec((1, H, LANES, tm), lambda i: (i // nst, 0, 0, i % nst))
    sds = lambda H: jax.ShapeDtypeStruct((B, H, S, LANES), BF16)
    sdt = lambda H: jax.ShapeDtypeStruct((B, H, LANES, S), BF16)
    return pl.pallas_call(
        _in_kernel,
        grid=(T // tm,),
        in_specs=[pl.BlockSpec((tm, D_MODEL), lambda i: (i, 0)),
                  const((1, D_MODEL)), const((D_MODEL, IN_COLS)),
                  tab, tab, tab, tab,
                  const((1, LANES)), const((1, LANES)), const((1, CQ_PAD)), const((1, MLA_KV_RANK)),
                  const((CQ_PAD, N_HEADS_B * LANES)), const((MLA_KV_RANK, 2 * N_HEADS_B * LANES)),
                  const((NRM_GROUPS * LANES, LANES))],
        out_specs=[hm(N_HEADS_A // 2), hmt(2 * N_KV_A), hm(N_KV_A),
                   hm(N_HEADS_B), hmt(N_HEADS_B), hm(N_HEADS_B),
                   hm(N_HEADS_C // 2),
                   pl.BlockSpec((1, 2 * N_HEADS_C, 1, LANES, tm),
                                lambda i: (i // nst, 0, i % nst, 0, 0)),
                   hm(N_HEADS_C),
                   pl.BlockSpec((1, 1, LANES), lambda i: (i, 0, 0))],
        out_shape=[sds(N_HEADS_A // 2), sdt(2 * N_KV_A), sds(N_KV_A),
                   sds(N_HEADS_B), sdt(N_HEADS_B), sds(N_HEADS_B),
                   sds(N_HEADS_C // 2),
                   jax.ShapeDtypeStruct((B, 2 * N_HEADS_C, S // tm, LANES, tm), BF16),
                   sds(N_HEADS_C),
                   jax.ShapeDtypeStruct((T // tm, 1, LANES), F32)],
        compiler_params=pltpu.CompilerParams(dimension_semantics=("arbitrary",),
                                             vmem_limit_bytes=VMEM_LIMIT_BYTES),
        name="in_proj",
    )(x2d, gat, w_in_r, *tabs, qg_a, kg_a, qg_b, kvg_b, wuq_p, wukv_p, _norm_routing())


def _lane_group_max(s):
    m = s[:, :LANES]
    for j in range(1, s.shape[1] // LANES):
        m = jnp.maximum(m, s[:, j * LANES:(j + 1) * LANES])
    return m


def _pv_bounded(streams, n_chunks):
    items = [(r, i) for r in range(n_chunks) for i in range(len(streams))]
    accs = [None] * len(streams)
    in_flight = {}
    for k in range(len(items) + SCORE_LOOKAHEAD):
        if k < len(items):
            r, i = items[k]
            scores, _, row_off = streams[i]
            s, off = scores(r), row_off(r)
            in_flight[k] = s if off is None else s + off
        if k >= SCORE_LOOKAHEAD:
            r, i = items[k - SCORE_LOOKAHEAD]
            d = _dot(jnp.exp2(in_flight.pop(k - SCORE_LOOKAHEAD)).astype(BF16), streams[i][1](r))
            accs[i] = d if accs[i] is None else accs[i] + d
    return accs


def _pv_exact(streams, n_chunks, s_ref):
    rows = s_ref.shape[0] // len(streams)
    maxes = []
    for i, (scores, _, row_off) in enumerate(streams):
        m_run = None
        for r in range(n_chunks):
            s = scores(r)
            s_ref[i * rows:(i + 1) * rows, r * TK:(r + 1) * TK] = s
            m, off = _lane_group_max(s), row_off(r)
            if off is not None:
                m = m + off
            m_run = m if m_run is None else jnp.maximum(m_run, m)
        maxes.append(jnp.max(m_run, axis=-1, keepdims=True))
    accs = []
    for i, (_, values, row_off) in enumerate(streams):
        acc = None
        for r in range(n_chunks):
            off = row_off(r)
            shift = maxes[i] if off is None else maxes[i] - off
            p = jnp.exp2(s_ref[i * rows:(i + 1) * rows, r * TK:(r + 1) * TK] - shift)
            d = _dot(p.astype(BF16), values(r))
            acc = d if acc is None else acc + d
        accs.append(acc)
    return accs


def _run_heads(bounded, heads, head_streams, n_chunks, s_ref, acc_ref, roll_exact=False):
    @pl.when(bounded)
    def _():
        shared = {}
        per_head = [head_streams(h, shared) for h in heads]
        n_sub = len(per_head[0])
        accs = _pv_bounded([s for streams in per_head for s in streams], n_chunks)
        for k, h in enumerate(heads):
            acc_ref[h] = jnp.concatenate(accs[k * n_sub:(k + 1) * n_sub], axis=0)

    @pl.when(jnp.logical_not(bounded))
    def _():
        def one_head(h):
            acc_ref[h] = jnp.concatenate(_pv_exact(head_streams(h, {}), n_chunks, s_ref), axis=0)

        if roll_exact:
            lax.fori_loop(0, len(heads), lambda k, c: (one_head(heads[0] + k), c)[1], 0)
        else:
            for h in heads:
                one_head(h)


def _normalised_pair(acc_low, acc_high):
    low = _low_lanes()
    num = jnp.where(low, acc_low, acc_high)
    den = pltpu.roll(jnp.where(low, acc_high, acc_low), HALF, axis=1)
    return num / den


def _attn_ab_kernel(bounded_ref, q_ref, kt_ref, v_ref, o_ref, s_ref, acc_ref, *,
                    n_heads, per_pipeline, q_index, kt_index, v_index, out_pairs, n_keys):
    n_chunks = n_keys // TK

    def head_streams(h, shared):
        q = q_ref[0, q_index(h)]
        return [(lambda r, qs=q[r0:r0 + TQ_SUB_AB]:
                 _dot(qs, kt_ref[0, kt_index(h), :, r * TK:(r + 1) * TK]),
                 lambda r: v_ref[0, v_index(h), r * TK:(r + 1) * TK, :],
                 lambda r: None) for r0 in range(0, TQ, TQ_SUB_AB)]

    def head_group(j, carry):
        heads = [per_pipeline * j + k for k in range(per_pipeline)]
        _run_heads(bounded_ref[pl.program_id(0), j] != 0, heads, head_streams, n_chunks, s_ref,
                   acc_ref, roll_exact=True)
        return carry

    if n_heads == per_pipeline:
        head_group(0, None)
    else:
        lax.fori_loop(0, n_heads // per_pipeline, head_group, 0)
    o_ref[...] = jnp.concatenate([_normalised_pair(acc_ref[a], acc_ref[b]) for a, b in out_pairs],
                                 axis=-1).astype(BF16)


def _attn_c_kernel(bounded_ref, q_ref, kt_ref, v_ref, lq1_ref, lk1_ref, lq2_ref, lk2_ref,
                   subln_ref, o_ref, s_ref, acc_ref, *, n_keys, lam_init):
    assert TQ == TK
    nc = n_keys // TK
    cd = pl.program_id(1)
    col = lax.broadcasted_iota(jnp.int32, (1, TK), 1).astype(F32)
    row = lax.broadcasted_iota(jnp.int32, (TQ_SUB_C, 1), 0).astype(F32)
    subs = [(r, r + TQ_SUB_C) for r in range(0, TQ, TQ_SUB_C)]
    neg_dist = [-jnp.abs((lax.broadcasted_iota(jnp.int32, (TQ_SUB_C, TK), 0) + r0
                          - lax.broadcasted_iota(jnp.int32, (TQ_SUB_C, TK), 1)).astype(F32))
                for r0, _ in subs]

    chunk = [(cd + r) % nc for r in range(nc)]

    def head_bias(hh):
        slope = jnp.float32(LOG2E * 2.0 ** (-8.0 * N_HEADS_C / N_HEADS_C))
        for k in range(N_HEADS_C - 2, -1, -1):
            slope = jnp.where(hh == k, LOG2E * 2.0 ** (-8.0 * (k + 1) / N_HEADS_C), slope)
        key_term, row_off = [None], [[None] * len(subs)]
        for r in range(1, nc):
            side = jnp.where(chunk[r] < cd, slope, -slope)
            key_term.append(col * side)
            base = -slope * (jnp.abs(chunk[r] - cd) * TK).astype(F32)
            row_off.append([base - side * (row + float(r0)) for r0, _ in subs])
        return slope, key_term, row_off

    def head_pair(p, carry):
        def streams_of(k, shared):
            idx, hh = 4 * p + k, 2 * p + k // 2
            if isinstance(k, int):
                if k // 2 not in shared:
                    shared[k // 2] = head_bias(hh)
                slope, key_term, row_off = shared[k // 2]
            else:
                slope, key_term, row_off = head_bias(hh)

            def scores(r, si):
                s = _dot(q_ref[0, p][subs[si][0]:subs[si][1]], kt_ref[0, idx, chunk[r]])
                return s + (neg_dist[si] * slope if r == 0 else key_term[r])

            return [(functools.partial(scores, si=si),
                     lambda r: v_ref[0, hh, pl.ds(pl.multiple_of(chunk[r] * TK, TK), TK), :],
                     lambda r, si=si: row_off[r][si]) for si in range(len(subs))]

        _run_heads(bounded_ref[pl.program_id(0), p] != 0, list(range(4)), streams_of, nc, s_ref,
                   acc_ref.at[pl.ds(4 * p, 4)], roll_exact=True)
        return carry

    assert HEADS_PER_PIPELINE == 4
    lax.fori_loop(0, N_HEADS_C // 2, head_pair, 0)

    lam = (jnp.exp(jnp.sum(lq1_ref[...] * lk1_ref[...], axis=-1, keepdims=True))
           - jnp.exp(jnp.sum(lq2_ref[...] * lk2_ref[...], axis=-1, keepdims=True)) + lam_init)
    outs = []
    for p in range(N_HEADS_C // 2):
        o = (_normalised_pair(acc_ref[4 * p], acc_ref[4 * p + 2])
             - lam * _normalised_pair(acc_ref[4 * p + 1], acc_ref[4 * p + 3]))
        outs.append(_half_rms(o) * subln_ref[...] * (1.0 - lam_init))
    o_ref[...] = jnp.concatenate(outs, axis=-1).astype(BF16)


def _attention(kernel_fn, bounded, n_acc, q, kt, v, extras, out_width, B, S):
    nq = S // TQ
    hq, hv = q.shape[1], v.shape[1]
    extra_specs = [pl.BlockSpec(e.shape, lambda b, i, f, n=e.ndim: (0,) * n) for e in extras]
    return pl.pallas_call(
        kernel_fn,
        grid_spec=pltpu.PrefetchScalarGridSpec(
            num_scalar_prefetch=1,
            grid=(B, nq),
            in_specs=[pl.BlockSpec((1, hq, TQ, LANES), lambda b, i, f: (b, 0, i, 0)),
                      pl.BlockSpec((1,) + kt.shape[1:],
                                   lambda b, i, f, n=kt.ndim: (b,) + (0,) * (n - 1)),
                      pl.BlockSpec((1, hv, S, LANES), lambda b, i, f: (b, 0, 0, 0))] + extra_specs,
            out_specs=pl.BlockSpec((TQ, out_width), lambda b, i, f: (b * nq + i, 0)),
            scratch_shapes=[pltpu.VMEM((TQ, S), F32),
                            pltpu.VMEM((n_acc, TQ, LANES), F32)]),
        out_shape=jax.ShapeDtypeStruct((B * S, out_width), BF16),
        compiler_params=pltpu.CompilerParams(dimension_semantics=("arbitrary", "arbitrary"),
                                             vmem_limit_bytes=VMEM_LIMIT_BYTES),
        name=kernel_fn.func.__name__.strip("_"),
    )(bounded, q, kt, v, *extras)


def _mlp_kernel(xp_ref, x_ref, xn_ref, ap_ref, a_ref, an_ref, bp_ref, b_ref, bn_ref,
                cp_ref, c_ref, cn_ref, wo_ref, g_ref, wu_ref, cw_ref, cb_ref, wd_ref, fg_ref,
                o_ref, act_ref, *, tiles_per_seq, final_norm):
    i = pl.program_id(0)
    tm = x_ref.shape[0]
    halo = xp_ref.shape[0]

    def rows(p, m, n):
        return jnp.concatenate([p[...], m[...], n[...]], axis=0)

    o_ext = jnp.concatenate([rows(ap_ref, a_ref, an_ref), rows(bp_ref, b_ref, bn_ref),
                             rows(cp_ref, c_ref, cn_ref)], axis=1)
    x1 = rows(xp_ref, x_ref, xn_ref) + _dot(o_ext, wo_ref[...])
    h = (_rms(x1, D_MODEL) * g_ref[...]).astype(BF16)
    at_start = (i % tiles_per_seq) == 0
    at_end = (i % tiles_per_seq) == tiles_per_seq - 1
    zeros = jnp.zeros((halo, D_MODEL), BF16)
    hext = jnp.concatenate([jnp.where(at_start, zeros, h[:halo]), h[halo:halo + tm],
                            jnp.where(at_end, zeros, h[halo + tm:])], axis=0)

    def branch(k, c0):
        u = _dot(hext, wu_ref[:, k * D_FF + c0:k * D_FF + c0 + FFN_CHUNK])
        cw = cw_ref[k, :, c0:c0 + FFN_CHUNK]
        n = u.shape[0]
        prev = pltpu.roll(u, 1, axis=0)[halo:halo + tm]
        nxt = pltpu.roll(u, n - 1, axis=0)[halo:halo + tm]
        return (prev * cw[0:1] + u[halo:halo + tm] * cw[1:2] + nxt * cw[2:3]
                + cb_ref[k, :, c0:c0 + FFN_CHUNK])

    y = x1[halo:halo + tm]
    done = 0
    for c0 in range(0, D_FF, FFN_CHUNK):
        g = branch(0, c0)
        val = branch(1, c0)
        half = 0.5 * g
        act_ref[:, c0:c0 + FFN_CHUNK] = ((half + half * jnp.tanh(half)) * val).astype(BF16)
        ready = c0 + FFN_CHUNK - FFN_DOWN_LAG * FFN_CHUNK
        if ready - done >= FFN_DOWN_GROUP:
            y = y + _dot(act_ref[:, done:done + FFN_DOWN_GROUP], wd_ref[done:done + FFN_DOWN_GROUP])
            done += FFN_DOWN_GROUP
    y = y + _dot(act_ref[:, done:], wd_ref[done:])
    if final_norm:
        y = _rms(y, D_MODEL) * fg_ref[...]
    o_ref[...] = y


def _mlp(x2d, oa, ob, oc, w_out, g_ffn, w_up, conv_w, conv_b, w_down, final_g, S, layer,
         final_norm):
    T = x2d.shape[0]
    tm, halo = TM_FFN, BF16_SUBLANES
    per = tm // halo
    last = T // halo - 1
    resident = lambda shape: pl.BlockSpec(shape, lambda i: (0,) * len(shape),
                                          pipeline_mode=pl.Buffered(1))
    of_layer = lambda shape: pl.BlockSpec((None,) + shape, lambda i: (layer,) + (0,) * len(shape),
                                          pipeline_mode=pl.Buffered(1))

    def with_halos(width):
        return [pl.BlockSpec((halo, width), lambda i: (jnp.maximum(i * per - 1, 0), 0)),
                pl.BlockSpec((tm, width), lambda i: (i, 0)),
                pl.BlockSpec((halo, width), lambda i: (jnp.minimum((i + 1) * per, last), 0))]

    return pl.pallas_call(
        functools.partial(_mlp_kernel, tiles_per_seq=S // tm, final_norm=final_norm),
        grid=(T // tm,),
        in_specs=(with_halos(D_MODEL) + with_halos(oa.shape[1]) + with_halos(ob.shape[1])
                  + with_halos(oc.shape[1])
                  + [resident(w_out.shape), resident((1, D_MODEL)),
                     of_layer((D_MODEL, 2 * D_FF)), resident((2, 3, D_FF)), resident((2, 1, D_FF)),
                     of_layer((D_FF, D_MODEL)), resident((1, D_MODEL))]),
        out_specs=pl.BlockSpec((tm, D_MODEL), lambda i: (i, 0)),
        out_shape=jax.ShapeDtypeStruct((T, D_MODEL), F32),
        scratch_shapes=[pltpu.VMEM((tm, D_FF), BF16)],
        compiler_params=pltpu.CompilerParams(dimension_semantics=("arbitrary",),
                                             vmem_limit_bytes=VMEM_LIMIT_BYTES),
        name="out_proj_conv_mlp",
    )(x2d, x2d, x2d, oa, oa, oa, ob, ob, ob, oc, oc, oc, w_out, g_ffn,
      w_up, conv_w, conv_b, w_down, final_g)


def _group_a_lane_order():
    nf = HEAD_DIM // 4
    lanes = np.arange(LANES)
    second, r = lanes // HALF, lanes % HALF
    head, rr = r // A_QUARTER, r % A_QUARTER
    block, freq = rr // nf, rr % nf
    dim = block * 2 * nf + second * nf + freq
    return head, dim, block, freq, second


def _group_b_lane_order():
    nf = MLA_ROPE // 4
    src = np.full(LANES, -1)
    block = np.zeros(LANES, np.int64)
    freq = np.zeros(LANES, np.int64)
    second = np.zeros(LANES, np.int64)
    is_rope = np.zeros(LANES, bool)
    for e in range(2):
        for j in range(2 * nf):
            lane = e * HALF + j
            is_rope[lane], block[lane], freq[lane], second[lane] = True, j // nf, j % nf, e
            src[lane] = MLA_NOPE + (j // nf) * 2 * nf + e * nf + j % nf
    n_low = HALF - 2 * nf
    src[2 * nf:HALF] = np.arange(n_low)
    src[HALF + 2 * nf:HALF + 2 * nf + MLA_NOPE - n_low] = np.arange(n_low, MLA_NOPE)
    return src, is_rope, block, freq, second


def _gather_cols(w, src):
    return jnp.where(jnp.asarray(src >= 0), w[:, np.maximum(src, 0)], 0.0)


def _rope_tables(S):
    t = jnp.arange(S, dtype=jnp.int32)
    pos = jnp.stack([(t // GRID_W).astype(F32), (t % GRID_W).astype(F32)], axis=0)

    def table(nf, block, freq, second, is_rope):
        inv = ROPE_BASE ** (-jnp.arange(nf, dtype=F32) / nf)
        ang = (pos[:, :, None] * inv).transpose(1, 0, 2).reshape(S, 2 * nf)
        lane_src = block * nf + freq
        sign = jnp.asarray(2.0 * second - 1.0, F32)
        cos = jnp.where(jnp.asarray(is_rope), jnp.cos(ang)[:, lane_src], 1.0)
        sin = jnp.where(jnp.asarray(is_rope), jnp.sin(ang)[:, lane_src] * sign, 0.0)
        return cos, sin

    _, _, block, freq, second = _group_a_lane_order()
    cos_a, sin_a = table(HEAD_DIM // 4, block, freq, second, np.ones(LANES, bool))
    _, is_rope, block, freq, second = _group_b_lane_order()
    cos_b, sin_b = table(MLA_ROPE // 4, block, freq, second, is_rope)
    return cos_a, sin_a, cos_b, sin_b


def _w_in_columns():
    n_qa, n_ka = N_HEADS_A * HEAD_DIM, N_KV_A * HEAD_DIM
    a_end = n_qa + 2 * n_ka
    kv_at = a_end + MLA_Q_RANK
    kr_at = kv_at + MLA_KV_RANK
    c_at = kr_at + MLA_ROPE
    head, dim, _, _, _ = _group_a_lane_order()
    pair_cols = head * HEAD_DIM + dim
    src_b, is_rope, _, _, _ = _group_b_lane_order()
    cols = np.concatenate(
        [p * LANES + pair_cols for p in range(N_HEADS_A // 2)]
        + [n_qa + pair_cols, np.arange(n_qa + n_ka, a_end),
           np.arange(a_end, kv_at), np.full(CQ_PAD - MLA_Q_RANK, -1),
           np.arange(kv_at, kr_at),
           np.where(is_rope, kr_at + src_b - MLA_NOPE, -1),
           np.arange(c_at, c_at + 3 * N_HEADS_C * HALF)])
    assert cols.shape == (IN_COLS,)
    return cols


def _w_uq_columns():
    src_b = _group_b_lane_order()[0]
    d_qk = MLA_NOPE + MLA_ROPE
    return np.concatenate([np.where(src_b >= 0, h * d_qk + src_b, -1) for h in range(N_HEADS_B)])


def _w_ukv_columns():
    src_b, is_rope, _, _, _ = _group_b_lane_order()
    d_kv = MLA_NOPE + MLA_V
    lanes = np.arange(LANES)
    keys = [np.where(~is_rope & (src_b >= 0), h * d_kv + src_b, -1) for h in range(N_HEADS_B)]
    vals = [np.where(lanes // HALF == h % 2, h * d_kv + MLA_NOPE + lanes % HALF, -1)
            for h in range(N_HEADS_B)]
    return np.concatenate(keys + vals)


def _twice(v):
    return jnp.tile(v, 2).reshape(1, 2 * v.shape[0])


def kernel(x, norm_attn, w_in, q_norm_a, k_norm_a, q_a_norm_b, w_uq_b, kv_a_norm_b, w_ukv_b,
           lambda_q1_c, lambda_k1_c, lambda_q2_c, lambda_k2_c, subln_c, w_out,
           norm_ffn, w_up, conv_w, conv_b, w_down, final_norm):
    B, S, D = x.shape
    depth = w_in.shape[0]
    assert D == D_MODEL and S % max(TM_IN, TM_FFN, TQ, TK) == 0 and S % GRID_W == 0
    assert TM_IN == TK
    T = B * S

    tabs = _rope_tables(S)
    row = lambda v: v.reshape(1, -1)
    a_pairs = [(j, j + N_HEADS_A // 2) for j in range(N_HEADS_A // 2)]
    a_rows = np.concatenate([np.arange(h * HEAD_DIM, (h + 1) * HEAD_DIM)
                             for pair in a_pairs for h in pair])
    xc = x.reshape(T, D)
    w_up_bf, w_down_bf = w_up.astype(BF16), w_down.astype(BF16)
    for l in range(depth):
        lam_init = 0.8 - 0.6 * math.exp(-0.3 * l)
        wuq_p = jnp.pad(_gather_cols(w_uq_b[l], _w_uq_columns()),
                        ((0, CQ_PAD - MLA_Q_RANK), (0, 0))).astype(BF16)
        wukv_p = _gather_cols(w_ukv_b[l], _w_ukv_columns()).astype(BF16)

        qa, kta, va, qb, ktb, vb, qc, ktc, vc, nrm = _in_proj(
            xc, row(norm_attn[l]), _gather_cols(w_in[l], _w_in_columns()).astype(BF16), tabs,
            q_norm_a[l][_group_a_lane_order()[1]].reshape(1, LANES),
            k_norm_a[l][_group_a_lane_order()[1]].reshape(1, LANES),
            jnp.pad(q_a_norm_b[l], (0, CQ_PAD - MLA_Q_RANK)).reshape(1, CQ_PAD),
            row(kv_a_norm_b[l]), wuq_p, wukv_p, B, S)

        nrm = jnp.max(nrm.reshape(B, -1, LANES), axis=1)

        def bounded(q0, nq, q_rep, k0, nk, k_rep, per_pipeline):
            q2 = jnp.repeat(nrm[:, q0:q0 + nq], q_rep, axis=1)
            k2 = jnp.repeat(nrm[:, k0:k0 + nk], k_rep, axis=1)
            ok = q2 * k2 * NRM_MARGIN <= SAFE_LOG2_RANGE ** 2
            return jnp.all(ok.reshape(B, -1, per_pipeline), axis=-1).astype(jnp.int32)

        qk_a = (HEAD_DIM * jnp.max(jnp.abs(q_norm_a[l])) * jnp.max(jnp.abs(k_norm_a[l]))
                * (HEAD_DIM ** -0.5 * LOG2E))
        bounded_a = jnp.broadcast_to(
            (qk_a * NRM_MARGIN <= SAFE_LOG2_RANGE).astype(jnp.int32),
            (B, N_HEADS_A // HEADS_PER_PIPELINE_AB))

        group = N_HEADS_A // N_KV_A
        oa = _attention(
            functools.partial(_attn_ab_kernel, n_heads=N_HEADS_A,
                              per_pipeline=HEADS_PER_PIPELINE_AB, q_index=lambda h: h // 2,
                              kt_index=lambda h: 2 * (h // group) + h % 2,
                              v_index=lambda h: h // group, out_pairs=a_pairs, n_keys=S),
            bounded_a, N_HEADS_A, qa, kta, va, [], N_HEADS_A * HEAD_DIM, B, S)
        same = lambda h: h
        ob = _attention(
            functools.partial(_attn_ab_kernel, n_heads=N_HEADS_B,
                              per_pipeline=HEADS_PER_PIPELINE_AB, q_index=same, kt_index=same,
                              v_index=same,
                              out_pairs=[(2 * j, 2 * j + 1) for j in range(N_HEADS_B // 2)],
                              n_keys=S),
            bounded(NRM_QB, N_HEADS_B, 1, NRM_KB, N_HEADS_B, 1, HEADS_PER_PIPELINE_AB), N_HEADS_B,
            qb, ktb, vb, [], N_HEADS_B * MLA_V, B, S)
        oc = _attention(
            functools.partial(_attn_c_kernel, n_keys=S, lam_init=lam_init),
            bounded(NRM_QC, N_HEADS_C, 2, NRM_KC, 2 * N_HEADS_C, 1, HEADS_PER_PIPELINE),
            2 * N_HEADS_C, qc, ktc, vc,
            [row(lambda_q1_c[l]), row(lambda_k1_c[l]), row(lambda_q2_c[l]), row(lambda_k2_c[l]),
             _twice(subln_c[l])],
            N_HEADS_C * DIFF_V, B, S)

        w_out_l = w_out[l][np.concatenate([a_rows, np.arange(N_HEADS_A * HEAD_DIM, D_MODEL)])]
        xc = _mlp(xc, oa, ob, oc, w_out_l.astype(BF16), row(norm_ffn[l]), w_up_bf,
                  conv_w[l].reshape(3, 2, D_FF).transpose(1, 0, 2),
                  conv_b[l].reshape(2, 1, D_FF), w_down_bf,
                  row(final_norm), S, l, final_norm=(l == depth - 1))
    return xc.reshape(B, S, D)
```
